```python
import jax, jax.numpy as jnp
from jax import lax
import numpy as np

D_MODEL = 1024
BATCH = 8
SEQ = 4096
DEPTH = 2

N_MIXERS = 2
DN_HEADS = 8
DN_HEAD_DIM = 128
DN_INNER = DN_HEADS * DN_HEAD_DIM
DN_CONV = 4
DN_CHUNK = 64
DN_IN_COLS = 4 * DN_INNER + 2 * DN_HEADS
CV_WIDTH = 31
MEM_LEN = 256
XA_HEADS = 4
XA_HEAD_DIM = D_MODEL // XA_HEADS
D_FF = 4 * D_MODEL
N_DN_LAYERS = (DEPTH + 1) // 2
N_CV_LAYERS = DEPTH // 2
RMS_EPS = 1e-6
LN_EPS = 1e-5

kernel_name = "hybrid_deltanet_conformer_xattn_trunk"


def rms_norm(x, g):
    xf = x.astype(jnp.float32)
    y = xf * lax.rsqrt(jnp.mean(xf * xf, axis=-1, keepdims=True) + RMS_EPS)
    return (y * g.astype(jnp.float32)).astype(x.dtype)


def layer_norm(x, g, b):
    xf = x.astype(jnp.float32)
    mu = jnp.mean(xf, axis=-1, keepdims=True)
    xc = xf - mu
    y = xc * lax.rsqrt(jnp.mean(xc * xc, axis=-1, keepdims=True) + LN_EPS)
    return (y * g.astype(jnp.float32) + b.astype(jnp.float32)).astype(x.dtype)


def l2_normalize(x):
    return x * lax.rsqrt(jnp.sum(x * x, axis=-1, keepdims=True) + 1e-6)


def causal_depthwise_conv(x, w):
    width = w.shape[0]
    return lax.conv_general_dilated(
        x, w[:, None, :].astype(x.dtype), window_strides=(1,), padding=[(width - 1, 0)],
        dimension_numbers=("NWC", "WIO", "NWC"), feature_group_count=x.shape[-1])


def chunk_gated_delta_rule(q, k, v, g, beta):
    B, S, H, dk = q.shape
    dv = v.shape[-1]
    n = S // DN_CHUNK

    def blocks(t):
        return t.reshape(B, n, DN_CHUNK, H, t.shape[-1]).transpose(0, 3, 1, 2, 4)

    q, k, v = blocks(q), blocks(k), blocks(v)
    g = g.reshape(B, n, DN_CHUNK, H).transpose(0, 3, 1, 2)
    beta = beta.reshape(B, n, DN_CHUNK, H).transpose(0, 3, 1, 2)
    g_cum = jnp.cumsum(g, axis=-1)

    idx = jnp.arange(DN_CHUNK)
    causal = idx[:, None] >= idx[None, :]
    strict = idx[:, None] > idx[None, :]
    diff = g_cum[..., :, None] - g_cum[..., None, :]
    decay = jnp.exp(jnp.where(causal, diff, -jnp.inf))

    k_beta = k * beta[..., None]
    lower = jnp.where(strict, jnp.einsum("bhnik,bhnjk->bhnij", k_beta, k) * decay, 0.0)
    a_mat = jnp.eye(DN_CHUNK, dtype=q.dtype) + lower
    rhs = jnp.concatenate([v * beta[..., None], k_beta * jnp.exp(g_cum)[..., None]], axis=-1)
    sol = lax.linalg.triangular_solve(a_mat, rhs, left_side=True, lower=True, unit_diagonal=True)
    u, w = sol[..., :dv], sol[..., dv:]

    attn_intra = jnp.einsum("bhnik,bhnjk->bhnij", q, k) * decay
    q_dec = q * jnp.exp(g_cum)[..., None]
    g_last = g_cum[..., -1]
    k_dec = k * jnp.exp(g_last[..., None] - g_cum)[..., None]

    xs = tuple(jnp.moveaxis(t, 2, 0) for t in (q_dec, k_dec, u, w, attn_intra, g_last))

    def step(state, inp):
        qd, kd, uc, wc, ai, gl = inp
        v_new = uc - jnp.einsum("bhck,bhkv->bhcv", wc, state)
        o = jnp.einsum("bhck,bhkv->bhcv", qd, state) + jnp.einsum("bhcj,bhjv->bhcv", ai, v_new)
        state = state * jnp.exp(gl)[..., None, None] + jnp.einsum("bhck,bhcv->bhkv", kd, v_new)
        return state, o

    s0 = jnp.zeros((B, H, dk, dv), jnp.float32)
    _, o = lax.scan(step, s0, xs)
    return o.transpose(1, 0, 3, 2, 4).reshape(B, S, H, dv)


def gated_deltanet(h, w_in, w_conv, a_log, dt_bias, out_norm, w_out):
    B, S, _ = h.shape
    proj = h @ w_in
    qkv = proj[..., :3 * DN_INNER]
    z = proj[..., 3 * DN_INNER:4 * DN_INNER]
    b_raw = proj[..., 4 * DN_INNER:4 * DN_INNER + DN_HEADS].astype(jnp.float32)
    a_raw = proj[..., 4 * DN_INNER + DN_HEADS:].astype(jnp.float32)
    qkv = jax.nn.silu(causal_depthwise_conv(qkv, w_conv)).astype(jnp.float32)
    q = qkv[..., :DN_INNER].reshape(B, S, DN_HEADS, DN_HEAD_DIM)
    k = qkv[..., DN_INNER:2 * DN_INNER].reshape(B, S, DN_HEADS, DN_HEAD_DIM)
    v = qkv[..., 2 * DN_INNER:].reshape(B, S, DN_HEADS, DN_HEAD_DIM)
    q = l2_normalize(q) * (DN_HEAD_DIM ** -0.5)
    k = l2_normalize(k)
    beta = jax.nn.sigmoid(b_raw)
    g = -jnp.exp(a_log.astype(jnp.float32)) * jax.nn.softplus(a_raw + dt_bias.astype(jnp.float32))
    o = chunk_gated_delta_rule(q, k, v, g, beta)
    o = o * lax.rsqrt(jnp.mean(o * o, axis=-1, keepdims=True) + RMS_EPS) * out_norm.astype(jnp.float32)
    o = o * jax.nn.silu(z.astype(jnp.float32).reshape(B, S, DN_HEADS, DN_HEAD_DIM))
    return o.astype(h.dtype).reshape(B, S, DN_INNER) @ w_out


def conformer_conv(h, w_pw1, b_pw1, w_dw, b_dw, ln_g, ln_b, w_pw2, b_pw2):
    u = h @ w_pw1 + b_pw1
    u = u[..., :D_MODEL] * jax.nn.sigmoid(u[..., D_MODEL:])
    c = causal_depthwise_conv(u, w_dw) + b_dw
    c = jax.nn.silu(layer_norm(c, ln_g, ln_b))
    return c @ w_pw2 + b_pw2


def memory_cross_attention(h, mem_h, w_q, w_kv, w_o):
    B, S, _ = h.shape
    M = mem_h.shape[1]
    q = (h @ w_q).reshape(B, S, XA_HEADS, XA_HEAD_DIM)
    kv = (mem_h @ w_kv).reshape(B, M, 2, XA_HEADS, XA_HEAD_DIM)
    k, v = kv[:, :, 0], kv[:, :, 1]
    s = jnp.einsum("bshd,bmhd->bhsm", q, k).astype(jnp.float32) * (XA_HEAD_DIM ** -0.5)
    p = jax.nn.softmax(s, axis=-1).astype(v.dtype)
    o = jnp.einsum("bhsm,bmhd->bshd", p, v).reshape(B, S, D_MODEL)
    return o @ w_o


def sq_relu_mlp(h, w_up, w_down):
    return jnp.square(jax.nn.relu(h @ w_up)) @ w_down


def _fwd_setup_inputs(seed: int = 0) -> dict:
    key = jax.random.key(seed)
    ks = jax.random.split(key, 32)
    f32 = jnp.float32

    def nrm(k, shape, scale):
        return jax.random.normal(k, shape, f32) * scale

    def gain(k, shape):
        return 1.0 + 0.02 * jax.random.normal(k, shape, f32)

    dt = jax.random.uniform(ks[6], (N_DN_LAYERS, DN_HEADS), f32, 1e-3, 1e-1)
    return {
        "x": jax.random.normal(ks[0], (BATCH, SEQ, D_MODEL), f32),
        "mem": jax.random.normal(ks[1], (BATCH, MEM_LEN, D_MODEL), f32),
        "dn_norm": gain(ks[2], (N_DN_LAYERS, D_MODEL)),
        "dn_w_in": nrm(ks[3], (N_DN_LAYERS, D_MODEL, DN_IN_COLS), D_MODEL ** -0.5),
        "dn_w_conv": nrm(ks[4], (N_DN_LAYERS, DN_CONV, 3 * DN_INNER), DN_CONV ** -0.5),
        "dn_a_log": jnp.log(jax.random.uniform(ks[5], (N_DN_LAYERS, DN_HEADS), f32, 1.0, 16.0)),
        "dn_dt_bias": dt + jnp.log(-jnp.expm1(-dt)),
        "dn_out_norm": gain(ks[7], (N_DN_LAYERS, DN_HEAD_DIM)),
        "dn_w_out": nrm(ks[8], (N_DN_LAYERS, DN_INNER, D_MODEL), DN_INNER ** -0.5),
        "cv_norm": gain(ks[9], (N_CV_LAYERS, D_MODEL)),
        "cv_w_pw1": nrm(ks[10], (N_CV_LAYERS, D_MODEL, 2 * D_MODEL), D_MODEL ** -0.5),
        "cv_b_pw1": nrm(ks[11], (N_CV_LAYERS, 2 * D_MODEL), 0.02),
        "cv_w_dw": nrm(ks[12], (N_CV_LAYERS, CV_WIDTH, D_MODEL), CV_WIDTH ** -0.5),
        "cv_b_dw": nrm(ks[13], (N_CV_LAYERS, D_MODEL), 0.02),
        "cv_ln_g": gain(ks[14], (N_CV_LAYERS, D_MODEL)),
        "cv_ln_b": nrm(ks[15], (N_CV_LAYERS, D_MODEL), 0.02),
        "cv_w_pw2": nrm(ks[16], (N_CV_LAYERS, D_MODEL, D_MODEL), D_MODEL ** -0.5),
        "cv_b_pw2": nrm(ks[17], (N_CV_LAYERS, D_MODEL), 0.02),
        "xa_norm": gain(ks[18], (DEPTH, D_MODEL)),
        "xa_mem_norm": gain(ks[19], (DEPTH, D_MODEL)),
        "xa_w_q": nrm(ks[20], (DEPTH, D_MODEL, D_MODEL), D_MODEL ** -0.5),
        "xa_w_kv": nrm(ks[21], (DEPTH, D_MODEL, 2 * D_MODEL), D_MODEL ** -0.5),
        "xa_w_o": nrm(ks[22], (DEPTH, D_MODEL, D_MODEL), D_MODEL ** -0.5),
        "mlp_norm": gain(ks[23], (DEPTH, D_MODEL)),
        "mlp_w_up": nrm(ks[24], (DEPTH, D_MODEL, D_FF), D_MODEL ** -0.5),
        "mlp_w_down": nrm(ks[25], (DEPTH, D_FF, D_MODEL), D_FF ** -0.5),
        "final_norm": gain(ks[26], (D_MODEL,)),
    }


def _fwd_reference(x, mem, dn_norm, dn_w_in, dn_w_conv, dn_a_log, dn_dt_bias, dn_out_norm, dn_w_out,
              cv_norm, cv_w_pw1, cv_b_pw1, cv_w_dw, cv_b_dw, cv_ln_g, cv_ln_b, cv_w_pw2, cv_b_pw2,
              xa_norm, xa_mem_norm, xa_w_q, xa_w_kv, xa_w_o, mlp_norm, mlp_w_up, mlp_w_down,
              final_norm):
    h = x
    for layer in range(DEPTH):
        j = layer // N_MIXERS
        if layer % N_MIXERS == 0:
            h = h + gated_deltanet(rms_norm(h, dn_norm[j]), dn_w_in[j], dn_w_conv[j], dn_a_log[j],
                                   dn_dt_bias[j], dn_out_norm[j], dn_w_out[j])
        else:
            h = h + conformer_conv(rms_norm(h, cv_norm[j]), cv_w_pw1[j], cv_b_pw1[j], cv_w_dw[j],
                                   cv_b_dw[j], cv_ln_g[j], cv_ln_b[j], cv_w_pw2[j], cv_b_pw2[j])
        h = h + memory_cross_attention(rms_norm(h, xa_norm[layer]), rms_norm(mem, xa_mem_norm[layer]),
                                       xa_w_q[layer], xa_w_kv[layer], xa_w_o[layer])
        h = h + sq_relu_mlp(rms_norm(h, mlp_norm[layer]), mlp_w_up[layer], mlp_w_down[layer])
    return rms_norm(h, final_norm)


import jax as _jax
import jax.numpy as _jnp

TWIN_FORMAT = 'train_step'
FWD_PARAMS = ['x', 'mem', 'dn_norm', 'dn_w_in', 'dn_w_conv', 'dn_a_log', 'dn_dt_bias', 'dn_out_norm', 'dn_w_out', 'cv_norm', 'cv_w_pw1', 'cv_b_pw1', 'cv_w_dw', 'cv_b_dw', 'cv_ln_g', 'cv_ln_b', 'cv_w_pw2', 'cv_b_pw2', 'xa_norm', 'xa_mem_norm', 'xa_w_q', 'xa_w_kv', 'xa_w_o', 'mlp_norm', 'mlp_w_up', 'mlp_w_down', 'final_norm']
TWIN_WEIGHTS = ['dn_norm', 'dn_w_in', 'dn_w_conv', 'dn_a_log', 'dn_dt_bias', 'dn_out_norm', 'dn_w_out', 'cv_norm', 'cv_w_pw1', 'cv_b_pw1', 'cv_w_dw', 'cv_b_dw', 'cv_ln_g', 'cv_ln_b', 'cv_w_pw2', 'cv_b_pw2', 'xa_norm', 'xa_mem_norm', 'xa_w_q', 'xa_w_kv', 'xa_w_o', 'mlp_norm', 'mlp_w_up', 'mlp_w_down', 'final_norm']
TWIN_DIFF_INPUT = 'x'
TWIN_INPUTS = ['x', 'mem', 'dn_norm', 'dn_w_in', 'dn_w_conv', 'dn_a_log', 'dn_dt_bias', 'dn_out_norm', 'dn_w_out', 'cv_norm', 'cv_w_pw1', 'cv_b_pw1', 'cv_w_dw', 'cv_b_dw', 'cv_ln_g', 'cv_ln_b', 'cv_w_pw2', 'cv_b_pw2', 'xa_norm', 'xa_mem_norm', 'xa_w_q', 'xa_w_kv', 'xa_w_o', 'mlp_norm', 'mlp_w_up', 'mlp_w_down', 'final_norm', 'loss_target', 'm_dn_norm', 'm_dn_w_in', 'm_dn_w_conv', 'm_dn_a_log', 'm_dn_dt_bias', 'm_dn_out_norm', 'm_dn_w_out', 'm_cv_norm', 'm_cv_w_pw1', 'm_cv_b_pw1', 'm_cv_w_dw', 'm_cv_b_dw', 'm_cv_ln_g', 'm_cv_ln_b', 'm_cv_w_pw2', 'm_cv_b_pw2', 'm_xa_norm', 'm_xa_mem_norm', 'm_xa_w_q', 'm_xa_w_kv', 'm_xa_w_o', 'm_mlp_norm', 'm_mlp_w_up', 'm_mlp_w_down', 'm_final_norm', 'v_dn_norm', 'v_dn_w_in', 'v_dn_w_conv', 'v_dn_a_log', 'v_dn_dt_bias', 'v_dn_out_norm', 'v_dn_w_out', 'v_cv_norm', 'v_cv_w_pw1', 'v_cv_b_pw1', 'v_cv_w_dw', 'v_cv_b_dw', 'v_cv_ln_g', 'v_cv_ln_b', 'v_cv_w_pw2', 'v_cv_b_pw2', 'v_xa_norm', 'v_xa_mem_norm', 'v_xa_w_q', 'v_xa_w_kv', 'v_xa_w_o', 'v_mlp_norm', 'v_mlp_w_up', 'v_mlp_w_down', 'v_final_norm']
TWIN_OUTPUTS = ['loss', 'grad_x', 'grad_dn_norm', 'grad_dn_w_in', 'grad_dn_w_conv', 'grad_dn_a_log', 'grad_dn_dt_bias', 'grad_dn_out_norm', 'grad_dn_w_out', 'grad_cv_norm', 'grad_cv_w_pw1', 'grad_cv_b_pw1', 'grad_cv_w_dw', 'grad_cv_b_dw', 'grad_cv_ln_g', 'grad_cv_ln_b', 'grad_cv_w_pw2', 'grad_cv_b_pw2', 'grad_xa_norm', 'grad_xa_mem_norm', 'grad_xa_w_q', 'grad_xa_w_kv', 'grad_xa_w_o', 'grad_mlp_norm', 'grad_mlp_w_up', 'grad_mlp_w_down', 'grad_final_norm', 'delta_dn_norm', 'delta_dn_w_in', 'delta_dn_w_conv', 'delta_dn_a_log', 'delta_dn_dt_bias', 'delta_dn_out_norm', 'delta_dn_w_out', 'delta_cv_norm', 'delta_cv_w_pw1', 'delta_cv_b_pw1', 'delta_cv_w_dw', 'delta_cv_b_dw', 'delta_cv_ln_g', 'delta_cv_ln_b', 'delta_cv_w_pw2', 'delta_cv_b_pw2', 'delta_xa_norm', 'delta_xa_mem_norm', 'delta_xa_w_q', 'delta_xa_w_kv', 'delta_xa_w_o', 'delta_mlp_norm', 'delta_mlp_w_up', 'delta_mlp_w_down', 'delta_final_norm', 'new_m_dn_norm', 'new_m_dn_w_in', 'new_m_dn_w_conv', 'new_m_dn_a_log', 'new_m_dn_dt_bias', 'new_m_dn_out_norm', 'new_m_dn_w_out', 'new_m_cv_norm', 'new_m_cv_w_pw1', 'new_m_cv_b_pw1', 'new_m_cv_w_dw', 'new_m_cv_b_dw', 'new_m_cv_ln_g', 'new_m_cv_ln_b', 'new_m_cv_w_pw2', 'new_m_cv_b_pw2', 'new_m_xa_norm', 'new_m_xa_mem_norm', 'new_m_xa_w_q', 'new_m_xa_w_kv', 'new_m_xa_w_o', 'new_m_mlp_norm', 'new_m_mlp_w_up', 'new_m_mlp_w_down', 'new_m_final_norm', 'new_v_dn_norm', 'new_v_dn_w_in', 'new_v_dn_w_conv', 'new_v_dn_a_log', 'new_v_dn_dt_bias', 'new_v_dn_out_norm', 'new_v_dn_w_out', 'new_v_cv_norm', 'new_v_cv_w_pw1', 'new_v_cv_b_pw1', 'new_v_cv_w_dw', 'new_v_cv_b_dw', 'new_v_cv_ln_g', 'new_v_cv_ln_b', 'new_v_cv_w_pw2', 'new_v_cv_b_pw2', 'new_v_xa_norm', 'new_v_xa_mem_norm', 'new_v_xa_w_q', 'new_v_xa_w_kv', 'new_v_xa_w_o', 'new_v_mlp_norm', 'new_v_mlp_w_up', 'new_v_mlp_w_down', 'new_v_final_norm']
TWIN_LEAF_KINDS = {'loss': 'loss', 'grad_x': 'grad_x', 'grad_dn_norm': 'grad_w', 'grad_dn_w_in': 'grad_w', 'grad_dn_w_conv': 'grad_w', 'grad_dn_a_log': 'grad_w', 'grad_dn_dt_bias': 'grad_w', 'grad_dn_out_norm': 'grad_w', 'grad_dn_w_out': 'grad_w', 'grad_cv_norm': 'grad_w', 'grad_cv_w_pw1': 'grad_w', 'grad_cv_b_pw1': 'grad_w', 'grad_cv_w_dw': 'grad_w', 'grad_cv_b_dw': 'grad_w', 'grad_cv_ln_g': 'grad_w', 'grad_cv_ln_b': 'grad_w', 'grad_cv_w_pw2': 'grad_w', 'grad_cv_b_pw2': 'grad_w', 'grad_xa_norm': 'grad_w', 'grad_xa_mem_norm': 'grad_w', 'grad_xa_w_q': 'grad_w', 'grad_xa_w_kv': 'grad_w', 'grad_xa_w_o': 'grad_w', 'grad_mlp_norm': 'grad_w', 'grad_mlp_w_up': 'grad_w', 'grad_mlp_w_down': 'grad_w', 'grad_final_norm': 'grad_w', 'delta_dn_norm': 'delta_w', 'delta_dn_w_in': 'delta_w', 'delta_dn_w_conv': 'delta_w', 'delta_dn_a_log': 'delta_w', 'delta_dn_dt_bias': 'delta_w', 'delta_dn_out_norm': 'delta_w', 'delta_dn_w_out': 'delta_w', 'delta_cv_norm': 'delta_w', 'delta_cv_w_pw1': 'delta_w', 'delta_cv_b_pw1': 'delta_w', 'delta_cv_w_dw': 'delta_w', 'delta_cv_b_dw': 'delta_w', 'delta_cv_ln_g': 'delta_w', 'delta_cv_ln_b': 'delta_w', 'delta_cv_w_pw2': 'delta_w', 'delta_cv_b_pw2': 'delta_w', 'delta_xa_norm': 'delta_w', 'delta_xa_mem_norm': 'delta_w', 'delta_xa_w_q': 'delta_w', 'delta_xa_w_kv': 'delta_w', 'delta_xa_w_o': 'delta_w', 'delta_mlp_norm': 'delta_w', 'delta_mlp_w_up': 'delta_w', 'delta_mlp_w_down': 'delta_w', 'delta_final_norm': 'delta_w', 'new_m_dn_norm': 'new_m', 'new_m_dn_w_in': 'new_m', 'new_m_dn_w_conv': 'new_m', 'new_m_dn_a_log': 'new_m', 'new_m_dn_dt_bias': 'new_m', 'new_m_dn_out_norm': 'new_m', 'new_m_dn_w_out': 'new_m', 'new_m_cv_norm': 'new_m', 'new_m_cv_w_pw1': 'new_m', 'new_m_cv_b_pw1': 'new_m', 'new_m_cv_w_dw': 'new_m', 'new_m_cv_b_dw': 'new_m', 'new_m_cv_ln_g': 'new_m', 'new_m_cv_ln_b': 'new_m', 'new_m_cv_w_pw2': 'new_m', 'new_m_cv_b_pw2': 'new_m', 'new_m_xa_norm': 'new_m', 'new_m_xa_mem_norm': 'new_m', 'new_m_xa_w_q': 'new_m', 'new_m_xa_w_kv': 'new_m', 'new_m_xa_w_o': 'new_m', 'new_m_mlp_norm': 'new_m', 'new_m_mlp_w_up': 'new_m', 'new_m_mlp_w_down': 'new_m', 'new_m_final_norm': 'new_m', 'new_v_dn_norm': 'new_v', 'new_v_dn_w_in': 'new_v', 'new_v_dn_w_conv': 'new_v', 'new_v_dn_a_log': 'new_v', 'new_v_dn_dt_bias': 'new_v', 'new_v_dn_out_norm': 'new_v', 'new_v_dn_w_out': 'new_v', 'new_v_cv_norm': 'new_v', 'new_v_cv_w_pw1': 'new_v', 'new_v_cv_b_pw1': 'new_v', 'new_v_cv_w_dw': 'new_v', 'new_v_cv_b_dw': 'new_v', 'new_v_cv_ln_g': 'new_v', 'new_v_cv_ln_b': 'new_v', 'new_v_cv_w_pw2': 'new_v', 'new_v_cv_b_pw2': 'new_v', 'new_v_xa_norm': 'new_v', 'new_v_xa_mem_norm': 'new_v', 'new_v_xa_w_q': 'new_v', 'new_v_xa_w_kv': 'new_v', 'new_v_xa_w_o': 'new_v', 'new_v_mlp_norm': 'new_v', 'new_v_mlp_w_up': 'new_v', 'new_v_mlp_w_down': 'new_v', 'new_v_final_norm': 'new_v'}


def _forward(args):
    return _fwd_reference(*[args[k] for k in FWD_PARAMS])


def _output_shape():
    out = _jax.eval_shape(lambda: _forward(_fwd_setup_inputs(0)))
    return out.shape, out.dtype

N_MICROBATCH = 1
ADAM_LR = 0.001
ADAM_B1 = 0.9
ADAM_B2 = 0.999
ADAM_EPS = 1e-08
ADAM_WD = 0.01
ADAM_STEP = 10
PER_EXAMPLE_BATCH_AXIS = {'x': 0, 'mem': 0, 'loss_target': 0}
SHARED_INPUTS = []
_WEIGHT_DTYPES = {'dn_norm': _jnp.float32, 'dn_w_in': _jnp.float32, 'dn_w_conv': _jnp.float32, 'dn_a_log': _jnp.float32, 'dn_dt_bias': _jnp.float32, 'dn_out_norm': _jnp.float32, 'dn_w_out': _jnp.float32, 'cv_norm': _jnp.float32, 'cv_w_pw1': _jnp.float32, 'cv_b_pw1': _jnp.float32, 'cv_w_dw': _jnp.float32, 'cv_b_dw': _jnp.float32, 'cv_ln_g': _jnp.float32, 'cv_ln_b': _jnp.float32, 'cv_w_pw2': _jnp.float32, 'cv_b_pw2': _jnp.float32, 'xa_norm': _jnp.float32, 'xa_mem_norm': _jnp.float32, 'xa_w_q': _jnp.float32, 'xa_w_kv': _jnp.float32, 'xa_w_o': _jnp.float32, 'mlp_norm': _jnp.float32, 'mlp_w_up': _jnp.float32, 'mlp_w_down': _jnp.float32, 'final_norm': _jnp.float32}
MOMENT_SCALE = {'dn_norm': 1.835200e-01, 'dn_w_in': 8.884295e-02, 'dn_w_conv': 8.417713e-02, 'dn_a_log': 5.169589e-01, 'dn_dt_bias': 4.908560e-01, 'dn_out_norm': 3.013853e-01, 'dn_w_out': 1.000050e-01, 'cv_norm': 7.672462e-02, 'cv_w_pw1': 5.181241e-02, 'cv_b_pw1': 7.005779e-02, 'cv_w_dw': 6.760287e-02, 'cv_b_dw': 1.724007e-01, 'cv_ln_g': 9.011081e-02, 'cv_ln_b': 1.087124e-01, 'cv_w_pw2': 7.169738e-02, 'cv_b_pw2': 1.830115e-01, 'xa_norm': 1.540362e-02, 'xa_mem_norm': 2.263121e-02, 'xa_w_q': 1.550536e-02, 'xa_w_kv': 1.588887e-02, 'xa_w_o': 1.615875e-02, 'mlp_norm': 1.386024e-01, 'mlp_w_up': 7.081796e-02, 'mlp_w_down': 1.400493e-01, 'final_norm': 3.265024e+01}


def _to_microbatches(a, axis):
    t = _jnp.moveaxis(a, axis, 0)
    t = t.reshape((N_MICROBATCH, t.shape[0] // N_MICROBATCH) + t.shape[1:])
    return _jnp.moveaxis(t, 1, axis + 1)


def setup_inputs(seed: int = 0) -> dict:
    inp = _fwd_setup_inputs(seed)
    key = _jax.random.fold_in(_jax.random.key(seed), 7919)
    shape, _ = _output_shape()
    out = dict(inp)
    out["loss_target"] = _jax.random.normal(_jax.random.fold_in(key, 0), shape, _jnp.float32)
    for i, name in enumerate(TWIN_WEIGHTS):
        w = inp[name].astype(_jnp.float32)
        if MOMENT_SCALE is None:
            s = _jnp.sqrt(_jnp.mean(_jnp.square(w)) + 1e-30)
        else:
            s = MOMENT_SCALE[name]
        km, kv = _jax.random.split(_jax.random.fold_in(key, i + 1))
        out[name] = w
        out["m_" + name] = s * _jax.random.normal(km, w.shape, _jnp.float32)
        out["v_" + name] = (s * s) * _jax.random.uniform(kv, w.shape, _jnp.float32, 0.5, 1.5)
    if N_MICROBATCH > 1:
        for name, axis in PER_EXAMPLE_BATCH_AXIS.items():
            out[name] = _to_microbatches(out[name], axis)
    return {'x': out['x'], 'mem': out['mem'], 'dn_norm': out['dn_norm'], 'dn_w_in': out['dn_w_in'], 'dn_w_conv': out['dn_w_conv'], 'dn_a_log': out['dn_a_log'], 'dn_dt_bias': out['dn_dt_bias'], 'dn_out_norm': out['dn_out_norm'], 'dn_w_out': out['dn_w_out'], 'cv_norm': out['cv_norm'], 'cv_w_pw1': out['cv_w_pw1'], 'cv_b_pw1': out['cv_b_pw1'], 'cv_w_dw': out['cv_w_dw'], 'cv_b_dw': out['cv_b_dw'], 'cv_ln_g': out['cv_ln_g'], 'cv_ln_b': out['cv_ln_b'], 'cv_w_pw2': out['cv_w_pw2'], 'cv_b_pw2': out['cv_b_pw2'], 'xa_norm': out['xa_norm'], 'xa_mem_norm': out['xa_mem_norm'], 'xa_w_q': out['xa_w_q'], 'xa_w_kv': out['xa_w_kv'], 'xa_w_o': out['xa_w_o'], 'mlp_norm': out['mlp_norm'], 'mlp_w_up': out['mlp_w_up'], 'mlp_w_down': out['mlp_w_down'], 'final_norm': out['final_norm'], 'loss_target': out['loss_target'], 'm_dn_norm': out['m_dn_norm'], 'm_dn_w_in': out['m_dn_w_in'], 'm_dn_w_conv': out['m_dn_w_conv'], 'm_dn_a_log': out['m_dn_a_log'], 'm_dn_dt_bias': out['m_dn_dt_bias'], 'm_dn_out_norm': out['m_dn_out_norm'], 'm_dn_w_out': out['m_dn_w_out'], 'm_cv_norm': out['m_cv_norm'], 'm_cv_w_pw1': out['m_cv_w_pw1'], 'm_cv_b_pw1': out['m_cv_b_pw1'], 'm_cv_w_dw': out['m_cv_w_dw'], 'm_cv_b_dw': out['m_cv_b_dw'], 'm_cv_ln_g': out['m_cv_ln_g'], 'm_cv_ln_b': out['m_cv_ln_b'], 'm_cv_w_pw2': out['m_cv_w_pw2'], 'm_cv_b_pw2': out['m_cv_b_pw2'], 'm_xa_norm': out['m_xa_norm'], 'm_xa_mem_norm': out['m_xa_mem_norm'], 'm_xa_w_q': out['m_xa_w_q'], 'm_xa_w_kv': out['m_xa_w_kv'], 'm_xa_w_o': out['m_xa_w_o'], 'm_mlp_norm': out['m_mlp_norm'], 'm_mlp_w_up': out['m_mlp_w_up'], 'm_mlp_w_down': out['m_mlp_w_down'], 'm_final_norm': out['m_final_norm'], 'v_dn_norm': out['v_dn_norm'], 'v_dn_w_in': out['v_dn_w_in'], 'v_dn_w_conv': out['v_dn_w_conv'], 'v_dn_a_log': out['v_dn_a_log'], 'v_dn_dt_bias': out['v_dn_dt_bias'], 'v_dn_out_norm': out['v_dn_out_norm'], 'v_dn_w_out': out['v_dn_w_out'], 'v_cv_norm': out['v_cv_norm'], 'v_cv_w_pw1': out['v_cv_w_pw1'], 'v_cv_b_pw1': out['v_cv_b_pw1'], 'v_cv_w_dw': out['v_cv_w_dw'], 'v_cv_b_dw': out['v_cv_b_dw'], 'v_cv_ln_g': out['v_cv_ln_g'], 'v_cv_ln_b': out['v_cv_ln_b'], 'v_cv_w_pw2': out['v_cv_w_pw2'], 'v_cv_b_pw2': out['v_cv_b_pw2'], 'v_xa_norm': out['v_xa_norm'], 'v_xa_mem_norm': out['v_xa_mem_norm'], 'v_xa_w_q': out['v_xa_w_q'], 'v_xa_w_kv': out['v_xa_w_kv'], 'v_xa_w_o': out['v_xa_w_o'], 'v_mlp_norm': out['v_mlp_norm'], 'v_mlp_w_up': out['v_mlp_w_up'], 'v_mlp_w_down': out['v_mlp_w_down'], 'v_final_norm': out['v_final_norm']}


def _loss(weights, diff, rest, loss_target):
    with _jax.named_scope("forward"):
        args = {**rest, TWIN_DIFF_INPUT: diff, **{k: w.astype(_WEIGHT_DTYPES[k]) for k, w in weights.items()}}
        y = _forward(args)
    with _jax.named_scope("loss_head"):
        err = _jnp.square(y.astype(_jnp.float32) - loss_target)
        return 0.5 * _jnp.sum(_jnp.mean(err, axis=-1)) if err.ndim else 0.5 * err


def _adamw(w, g, m, v):
    m = ADAM_B1 * m + (1.0 - ADAM_B1) * g
    v = ADAM_B2 * v + (1.0 - ADAM_B2) * _jnp.square(g)
    m_hat = m / (1.0 - ADAM_B1 ** ADAM_STEP)
    v_hat = v / (1.0 - ADAM_B2 ** ADAM_STEP)
    delta = -ADAM_LR * (m_hat / (_jnp.sqrt(v_hat) + ADAM_EPS) + ADAM_WD * w)
    return delta, m, v


def reference(x, mem, dn_norm, dn_w_in, dn_w_conv, dn_a_log, dn_dt_bias, dn_out_norm, dn_w_out, cv_norm, cv_w_pw1, cv_b_pw1, cv_w_dw, cv_b_dw, cv_ln_g, cv_ln_b, cv_w_pw2, cv_b_pw2, xa_norm, xa_mem_norm, xa_w_q, xa_w_kv, xa_w_o, mlp_norm, mlp_w_up, mlp_w_down, final_norm, loss_target, m_dn_norm, m_dn_w_in, m_dn_w_conv, m_dn_a_log, m_dn_dt_bias, m_dn_out_norm, m_dn_w_out, m_cv_norm, m_cv_w_pw1, m_cv_b_pw1, m_cv_w_dw, m_cv_b_dw, m_cv_ln_g, m_cv_ln_b, m_cv_w_pw2, m_cv_b_pw2, m_xa_norm, m_xa_mem_norm, m_xa_w_q, m_xa_w_kv, m_xa_w_o, m_mlp_norm, m_mlp_w_up, m_mlp_w_down, m_final_norm, v_dn_norm, v_dn_w_in, v_dn_w_conv, v_dn_a_log, v_dn_dt_bias, v_dn_out_norm, v_dn_w_out, v_cv_norm, v_cv_w_pw1, v_cv_b_pw1, v_cv_w_dw, v_cv_b_dw, v_cv_ln_g, v_cv_ln_b, v_cv_w_pw2, v_cv_b_pw2, v_xa_norm, v_xa_mem_norm, v_xa_w_q, v_xa_w_kv, v_xa_w_o, v_mlp_norm, v_mlp_w_up, v_mlp_w_down, v_final_norm):
    given = dict(x=x, mem=mem, dn_norm=dn_norm, dn_w_in=dn_w_in, dn_w_conv=dn_w_conv, dn_a_log=dn_a_log, dn_dt_bias=dn_dt_bias, dn_out_norm=dn_out_norm, dn_w_out=dn_w_out, cv_norm=cv_norm, cv_w_pw1=cv_w_pw1, cv_b_pw1=cv_b_pw1, cv_w_dw=cv_w_dw, cv_b_dw=cv_b_dw, cv_ln_g=cv_ln_g, cv_ln_b=cv_ln_b, cv_w_pw2=cv_w_pw2, cv_b_pw2=cv_b_pw2, xa_norm=xa_norm, xa_mem_norm=xa_mem_norm, xa_w_q=xa_w_q, xa_w_kv=xa_w_kv, xa_w_o=xa_w_o, mlp_norm=mlp_norm, mlp_w_up=mlp_w_up, mlp_w_down=mlp_w_down, final_norm=final_norm, loss_target=loss_target, m_dn_norm=m_dn_norm, m_dn_w_in=m_dn_w_in, m_dn_w_conv=m_dn_w_conv, m_dn_a_log=m_dn_a_log, m_dn_dt_bias=m_dn_dt_bias, m_dn_out_norm=m_dn_out_norm, m_dn_w_out=m_dn_w_out, m_cv_norm=m_cv_norm, m_cv_w_pw1=m_cv_w_pw1, m_cv_b_pw1=m_cv_b_pw1, m_cv_w_dw=m_cv_w_dw, m_cv_b_dw=m_cv_b_dw, m_cv_ln_g=m_cv_ln_g, m_cv_ln_b=m_cv_ln_b, m_cv_w_pw2=m_cv_w_pw2, m_cv_b_pw2=m_cv_b_pw2, m_xa_norm=m_xa_norm, m_xa_mem_norm=m_xa_mem_norm, m_xa_w_q=m_xa_w_q, m_xa_w_kv=m_xa_w_kv, m_xa_w_o=m_xa_w_o, m_mlp_norm=m_mlp_norm, m_mlp_w_up=m_mlp_w_up, m_mlp_w_down=m_mlp_w_down, m_final_norm=m_final_norm, v_dn_norm=v_dn_norm, v_dn_w_in=v_dn_w_in, v_dn_w_conv=v_dn_w_conv, v_dn_a_log=v_dn_a_log, v_dn_dt_bias=v_dn_dt_bias, v_dn_out_norm=v_dn_out_norm, v_dn_w_out=v_dn_w_out, v_cv_norm=v_cv_norm, v_cv_w_pw1=v_cv_w_pw1, v_cv_b_pw1=v_cv_b_pw1, v_cv_w_dw=v_cv_w_dw, v_cv_b_dw=v_cv_b_dw, v_cv_ln_g=v_cv_ln_g, v_cv_ln_b=v_cv_ln_b, v_cv_w_pw2=v_cv_w_pw2, v_cv_b_pw2=v_cv_b_pw2, v_xa_norm=v_xa_norm, v_xa_mem_norm=v_xa_mem_norm, v_xa_w_q=v_xa_w_q, v_xa_w_kv=v_xa_w_kv, v_xa_w_o=v_xa_w_o, v_mlp_norm=v_mlp_norm, v_mlp_w_up=v_mlp_w_up, v_mlp_w_down=v_mlp_w_down, v_final_norm=v_final_norm)
    weights = {n: given[n] for n in TWIN_WEIGHTS}
    shared = {n: given[n] for n in SHARED_INPUTS}
    per_example = {n: given[n] for n in ['x', 'mem']}
    grad_fn = _jax.value_and_grad(_loss, argnums=(0, 1))

    def one_microbatch(ex, loss_target):
        ex = dict(ex)
        diff = ex.pop(TWIN_DIFF_INPUT)
        return grad_fn(weights, diff, {**shared, **ex}, loss_target)

    if N_MICROBATCH == 1:
        loss, (grad_w, grad_x) = one_microbatch(per_example, given["loss_target"])
    else:
        def body(carry, xs):
            loss_sum, grad_sum = carry
            l_k, (gw_k, gx_k) = one_microbatch(xs[0], xs[1])
            with _jax.named_scope("update"):
                return (loss_sum + l_k, _jax.tree.map(_jnp.add, grad_sum, gw_k)), gx_k

        init = (_jnp.zeros((), _jnp.float32), _jax.tree.map(_jnp.zeros_like, weights))
        (loss, grad_w), grad_x = _jax.lax.scan(body, init, (per_example, given["loss_target"]))
    with _jax.named_scope("update"):
        delta_w, new_m, new_v = {}, {}, {}
        for n in TWIN_WEIGHTS:
            delta_w[n], new_m[n], new_v[n] = _adamw(weights[n], grad_w[n], given["m_" + n], given["v_" + n])
    return (loss, grad_x, *[grad_w[n] for n in TWIN_WEIGHTS], *[delta_w[n] for n in TWIN_WEIGHTS],
            *[new_m[n] for n in TWIN_WEIGHTS], *[new_v[n] for n in TWIN_WEIGHTS])
```

```python
import functools

import jax
import jax.numpy as jnp
from jax import lax
from jax.experimental import pallas as pl
from jax.experimental.pallas import tpu as pltpu

F32 = jnp.float32
BF16 = jnp.bfloat16
HP = lax.Precision.HIGHEST
MESH_IDS = pl.DeviceIdType.MESH

N_DEV = 8
LANES = 128
RMS_EPS = 1e-6
LN_EPS = 1e-5
DN_HEAD_DIM = 128
DN_CONV = 4
DN_CHUNK = 64
CV_WIDTH = 31
XA_HEADS = 4
DN_HALO = 8
CV_HALO = 32
PACK_COLS = 1024
PACK_ROW_BLOCK = 384

ADAM_LR = 0.001
ADAM_B1 = 0.9
ADAM_B2 = 0.999
ADAM_EPS = 1e-08
ADAM_WD = 0.01
ADAM_STEP = 10

BIG = (("dn_w_in", 2), ("dn_w_out", 1), ("cv_w_pw1", 2), ("cv_w_pw2", 1), ("xa_w_q", 1), ("xa_w_kv", 2),
       ("xa_w_o", 1), ("mlp_w_up", 2), ("mlp_w_down", 1))
SMALL_SH = ("cv_norm", "cv_b_pw1", "cv_b_dw", "cv_ln_g", "cv_ln_b", "cv_b_pw2", "cv_w_dw", "dn_w_conv")
REPL = ("dn_norm", "dn_a_log", "dn_dt_bias", "dn_out_norm", "xa_norm", "xa_mem_norm", "mlp_norm", "final_norm")
WEIGHTS = ("dn_norm", "dn_w_in", "dn_w_conv", "dn_a_log", "dn_dt_bias", "dn_out_norm", "dn_w_out", "cv_norm",
           "cv_w_pw1", "cv_b_pw1", "cv_w_dw", "cv_b_dw", "cv_ln_g", "cv_ln_b", "cv_w_pw2", "cv_b_pw2", "xa_norm",
           "xa_mem_norm", "xa_w_q", "xa_w_kv", "xa_w_o", "mlp_norm", "mlp_w_up", "mlp_w_down", "final_norm")


def _bdot(a, b, dims):
    return lax.dot_general(a.astype(BF16), b.astype(BF16), (dims, ((), ())), preferred_element_type=F32)


def _hdot(a, b, dims):
    return lax.dot_general(a, b, (dims, ((), ())), precision=HP, preferred_element_type=F32)


@jax.custom_vjp
def _mm_nn(a, b):
    return _bdot(a, b, ((1,), (0,)))


def _mm_nn_fwd(a, b):
    return _mm_nn(a, b), (a, b)


def _mm_nn_bwd(res, ct):
    a, b = res
    return (_bdot(ct, b, ((1,), (1,))).astype(a.dtype), _bdot(a, ct, ((0,), (0,))).astype(b.dtype))


_mm_nn.defvjp(_mm_nn_fwd, _mm_nn_bwd)


@jax.custom_vjp
def _mm_nt(a, b):
    return _bdot(a, b, ((1,), (1,)))


def _mm_nt_fwd(a, b):
    return _mm_nt(a, b), (a, b)


def _mm_nt_bwd(res, ct):
    a, b = res
    return (_bdot(ct, b, ((1,), (0,))).astype(a.dtype), _bdot(ct, a, ((0,), (0,))).astype(b.dtype))


_mm_nt.defvjp(_mm_nt_fwd, _mm_nt_bwd)


@jax.custom_vjp
def _mm_tn(a, b):
    return _bdot(a, b, ((0,), (0,)))


def _mm_tn_fwd(a, b):
    return _mm_tn(a, b), (a, b)


def _mm_tn_bwd(res, ct):
    a, b = res
    return (_bdot(b, ct, ((1,), (1,))).astype(a.dtype), _bdot(a, ct, ((1,), (0,))).astype(b.dtype))


_mm_tn.defvjp(_mm_tn_fwd, _mm_tn_bwd)


def _sigmoid(x):
    return 0.5 * (jnp.tanh(0.5 * x) + 1.0)


def _silu(x):
    return x * _sigmoid(x)


def _softplus(x):
    return jnp.maximum(x, 0.0) + jnp.log(1.0 + jnp.exp(-jnp.abs(x)))


def _rms(x, g):
    r = lax.rsqrt(jnp.mean(x * x, axis=-1, keepdims=True) + RMS_EPS)
    return x * r * g


def _shift_rows(x, off):
    if off == 0:
        return x
    return pltpu.roll(x, x.shape[0] - off, 0)


def _neumann_inverse(lm):
    n = lm.shape[0]
    eye = (lax.broadcasted_iota(jnp.int32, (n, n), 0) == lax.broadcasted_iota(jnp.int32, (n, n), 1)).astype(F32)
    t = eye - lm
    p = lm
    steps = 0
    size = 2
    while size < n:
        size *= 2
        steps += 1
    for _ in range(steps):
        p = _hdot(p, p, ((1,), (0,)))
        t = t + _hdot(t, p, ((1,), (0,)))
    return t


@jax.custom_vjp
def _unit_lower_solve(lm, rhs):
    return _hdot(_neumann_inverse(lm), rhs, ((1,), (0,)))


def _uls_fwd(lm, rhs):
    t = _neumann_inverse(lm)
    sol = _hdot(t, rhs, ((1,), (0,)))
    return sol, (t, sol)


def _uls_bwd(res, ct):
    t, sol = res
    d_rhs = _hdot(t, ct, ((0,), (0,)))
    return -_hdot(d_rhs, sol, ((1,), (1,))), d_rhs


_unit_lower_solve.defvjp(_uls_fwd, _uls_bwd)


def _delta_chunk(q, k, v, gcol, bcol, s0):
    c = q.shape[0]
    ii = lax.broadcasted_iota(jnp.int32, (c, c), 0)
    jj = lax.broadcasted_iota(jnp.int32, (c, c), 1)
    eye = (ii == jj).astype(F32)
    causal = ii >= jj
    strict = ii > jj
    grow = jnp.sum(eye * gcol, axis=0, keepdims=True)
    gc = jnp.sum(jnp.where(causal, grow, 0.0), axis=1, keepdims=True)
    gc_row = jnp.sum(eye * gc, axis=0, keepdims=True)
    decay = jnp.exp(jnp.where(causal, gc - gc_row, -jnp.inf))
    kb = k * bcol
    lm = jnp.where(strict, _mm_nt(kb, k) * decay, 0.0)
    egc = jnp.exp(gc)
    rhs = jnp.concatenate([v * bcol, kb * egc], axis=-1)
    sol = _unit_lower_solve(lm, rhs)
    dv_ = v.shape[1]
    u, w = sol[:, :dv_], sol[:, dv_:]
    attn = _mm_nt(q, k) * decay
    qd = q * egc
    gl = jnp.sum(grow, axis=1, keepdims=True)
    kd = k * jnp.exp(gl - gc)
    v_new = u - _mm_nn(w, s0)
    o = _mm_nn(qd, s0) + _mm_nn(attn, v_new)
    s1 = s0 * jnp.exp(gl) + _mm_tn(kd, v_new)
    return o, s1


def _dn_point(cv, ba, alog, dt, heads):
    a = _silu(cv)
    d = cv.shape[1] // 3
    qs, ks = [], []
    for h in range(heads):
        qh = a[:, h * DN_HEAD_DIM:(h + 1) * DN_HEAD_DIM]
        qs.append(qh * lax.rsqrt(jnp.sum(qh * qh, axis=-1, keepdims=True) + 1e-6) * (DN_HEAD_DIM ** -0.5))
        kh = a[:, d + h * DN_HEAD_DIM:d + (h + 1) * DN_HEAD_DIM]
        ks.append(kh * lax.rsqrt(jnp.sum(kh * kh, axis=-1, keepdims=True) + 1e-6))
    q = jnp.concatenate(qs, axis=-1)
    k = jnp.concatenate(ks, axis=-1)
    v = a[:, 2 * d:]
    lane = lax.broadcasted_iota(jnp.int32, ba.shape, 1)
    beta = _sigmoid(ba)
    g = -jnp.exp(alog) * _softplus(ba + dt)
    gb = jnp.where(lane < heads, beta, jnp.where(lane < 2 * heads, g, 0.0))
    return q, k, v, gb


def _dn_post(o, z, onorm, heads):
    outs = []
    for h in range(heads):
        oh = o[:, h * DN_HEAD_DIM:(h + 1) * DN_HEAD_DIM]
        outs.append(oh * lax.rsqrt(jnp.mean(oh * oh, axis=-1, keepdims=True) + RMS_EPS) * onorm)
    return jnp.concatenate(outs, axis=-1) * _silu(z)


def _attn_tile(q, k, v):
    hd = q.shape[1] // XA_HEADS
    outs = []
    for h in range(XA_HEADS):
        sl = slice(h * hd, (h + 1) * hd)
        s = _mm_nt(q[:, sl], k[:, sl]) * (hd ** -0.5)
        m = lax.stop_gradient(jnp.max(s, axis=-1, keepdims=True))
        e = jnp.exp(s - m)
        p = e / jnp.sum(e, axis=-1, keepdims=True)
        outs.append(_mm_nn(p, v[:, sl]))
    return jnp.concatenate(outs, axis=-1)


def _ln_silu(c, g, b):
    mu = jnp.mean(c, axis=-1, keepdims=True)
    xc = c - mu
    y = xc * lax.rsqrt(jnp.mean(xc * xc, axis=-1, keepdims=True) + LN_EPS)
    return _silu(y * g + b)


def _causal_conv(xext, w, width, lead, ts):
    acc = None
    for j in range(width):
        term = _shift_rows(xext, lead + j)[:ts] * w[j:j + 1, :]
        acc = term if acc is None else acc + term
    return acc


def _colsum(x):
    return jnp.sum(x, axis=0, keepdims=True)


def _stack_rows(rows, n_rows):
    c = rows[0].shape[1]
    ridx = lax.broadcasted_iota(jnp.int32, (n_rows, c), 0)
    out = jnp.zeros((n_rows, c), F32)
    for j, r in enumerate(rows):
        out = out + jnp.where(ridx == j, r, 0.0)
    return out


def _matmul(a, b, mode, out_dtypes, *, name, epi=None, mn_extras=(), row_extras=(), tm=512, tn=1024, tk=1024):
    if mode == "nn":
        (m, k), (k2, n) = a.shape, b.shape
    elif mode == "nt":
        (m, k), (n, k2) = a.shape, b.shape
    else:
        (k, m), (k2, n) = a.shape, b.shape
    assert k == k2, (a.shape, b.shape, mode)
    tm, tn, tk = min(tm, m), min(tn, n), min(tk, k)
    assert m % tm == 0 and n % tn == 0 and k % tk == 0, (m, n, k, tm, tn, tk)
    nk = k // tk
    if mode == "tn":
        a_spec = pl.BlockSpec((tk, tm), lambda j, i, kk: (kk, i))
    else:
        a_spec = pl.BlockSpec((tm, tk), lambda j, i, kk: (i, kk))
    if mode == "nt":
        b_spec = pl.BlockSpec((tn, tk), lambda j, i, kk: (j, kk))
    else:
        b_spec = pl.BlockSpec((tk, tn), lambda j, i, kk: (kk, j))
    dims = {"nn": ((1,), (0,)), "nt": ((1,), (1,)), "tn": ((0,), (0,))}[mode]
    mn_spec = pl.BlockSpec((tm, tn), lambda j, i, kk: (i, j))
    row_spec = pl.BlockSpec((1, tn), lambda j, i, kk: (0, j))
    n_extra = len(mn_extras) + len(row_extras)
    n_out = len(out_dtypes)

    def body(a_ref, b_ref, *rest):
        extras = rest[:n_extra]
        outs = rest[n_extra:n_extra + n_out]
        acc = rest[-1]
        kk = pl.program_id(2)

        @pl.when(kk == 0)
        def _():
            acc[...] = jnp.zeros_like(acc)

        acc[...] += _bdot(a_ref[...], b_ref[...], dims)

        @pl.when(kk == nk - 1)
        def _():
            vals = (acc[...],) if epi is None else epi(acc[...], *[e[...] for e in extras])
            for o_ref, val in zip(outs, vals):
                o_ref[...] = val.astype(o_ref.dtype)

    res = pl.pallas_call(
        body, name=name,
        grid=(n // tn, m // tm, nk),
        in_specs=[a_spec, b_spec] + [mn_spec] * len(mn_extras) + [row_spec] * len(row_extras),
        out_specs=[mn_spec] * n_out,
        out_shape=[jax.ShapeDtypeStruct((m, n), dt) for dt in out_dtypes],
        scratch_shapes=[pltpu.VMEM((tm, tn), F32)],
        compiler_params=pltpu.CompilerParams(dimension_semantics=("parallel", "parallel", "arbitrary")),
    )(a, b, *mn_extras, *row_extras)
    return res[0] if n_out == 1 else res


def _rowwise(fn, *, n_rows, ts, name, rows=(), prevs=(), nexts=(), vecs=(), row_outs=(), acc_outs=()):
    ts = min(ts, n_rows)
    assert n_rows % ts == 0
    nblk = n_rows // ts
    in_specs, args = [], []
    for arr, cb, w in rows:
        in_specs.append(pl.BlockSpec((ts, w), functools.partial(lambda i, cb: (i, cb), cb=cb)))
        args.append(arr)
    for arr, cb, w, halo in prevs:
        per = ts // halo
        in_specs.append(pl.BlockSpec(
            (halo, w), functools.partial(lambda i, cb, per: (jnp.maximum(i * per - 1, 0), cb), cb=cb, per=per)))
        args.append(arr)
    for arr, cb, w, halo in nexts:
        per = ts // halo
        last_blk = n_rows // halo - 1
        in_specs.append(pl.BlockSpec(
            (halo, w), functools.partial(lambda i, cb, per, lb: (jnp.minimum((i + 1) * per, lb), cb),
                                         cb=cb, per=per, lb=last_blk)))
        args.append(arr)
    for arr in vecs:
        in_specs.append(pl.BlockSpec(arr.shape, functools.partial(lambda i, nd: (0,) * nd, nd=arr.ndim)))
        args.append(arr)
    out_specs, out_shape = [], []
    for w, dt in row_outs:
        out_specs.append(pl.BlockSpec((ts, w), lambda i: (i, 0)))
        out_shape.append(jax.ShapeDtypeStruct((n_rows, w), dt))
    for shp in acc_outs:
        out_specs.append(pl.BlockSpec(shp, functools.partial(lambda i, nd: (0,) * nd, nd=len(shp))))
        out_shape.append(jax.ShapeDtypeStruct(shp, F32))
    n_in, n_ro, n_acc = len(args), len(row_outs), len(acc_outs)
    n_tiles = n_in - len(vecs)

    def body(*refs):
        ins, ro, ac = refs[:n_in], refs[n_in:n_in + n_ro], refs[n_in + n_ro:]
        i = pl.program_id(0)
        rvals, avals = fn(i == 0, i == nblk - 1, *[r[...] for r in ins[:n_tiles]], *ins[n_tiles:])
        for r, val in zip(ro, rvals):
            r[...] = val.astype(r.dtype)
        if n_acc:
            @pl.when(i == 0)
            def _():
                for r in ac:
                    r[...] = jnp.zeros_like(r)

            for r, val in zip(ac, avals):
                r[...] += val

    res = pl.pallas_call(
        body, name=name, grid=(nblk,), in_specs=in_specs, out_specs=out_specs, out_shape=out_shape,
        compiler_params=pltpu.CompilerParams(dimension_semantics=("arbitrary",)),
    )(*args)
    return res


def _delta_fwd(q, k, v, gb, heads):
    s, hd = q.shape
    n = s // DN_CHUNK
    blk = pl.BlockSpec((DN_CHUNK, hd), lambda c: (c, 0))
    gspec = pl.BlockSpec((DN_CHUNK, LANES), lambda c: (c, 0))

    def body(q_ref, k_ref, v_ref, gb_ref, o_ref, st_ref, state):
        @pl.when(pl.program_id(0) == 0)
        def _():
            state[...] = jnp.zeros_like(state)

        gbv = gb_ref[...]
        for h in range(heads):
            sl = slice(h * DN_HEAD_DIM, (h + 1) * DN_HEAD_DIM)
            s0 = state[h]
            st_ref[0, h] = s0
            o, s1 = _delta_chunk(q_ref[:, sl], k_ref[:, sl], v_ref[:, sl],
                                 gbv[:, heads + h:heads + h + 1], gbv[:, h:h + 1], s0)
            o_ref[:, sl] = o
            state[h] = s1

    return pl.pallas_call(
        body, name="dn_delta_fwd", grid=(n,),
        in_specs=[blk, blk, blk, gspec],
        out_specs=[blk, pl.BlockSpec((1, heads, DN_HEAD_DIM, DN_HEAD_DIM), lambda c: (c, 0, 0, 0))],
        out_shape=[jax.ShapeDtypeStruct((s, hd), F32),
                   jax.ShapeDtypeStruct((n, heads, DN_HEAD_DIM, DN_HEAD_DIM), F32)],
        scratch_shapes=[pltpu.VMEM((heads, DN_HEAD_DIM, DN_HEAD_DIM), F32)],
        compiler_params=pltpu.CompilerParams(dimension_semantics=("arbitrary",)),
    )(q, k, v, gb)


def _delta_bwd(q, k, v, gb, states, do, heads):
    s, hd = q.shape
    n = s // DN_CHUNK
    blk = pl.BlockSpec((DN_CHUNK, hd), lambda c: (n - 1 - c, 0))
    gspec = pl.BlockSpec((DN_CHUNK, LANES), lambda c: (n - 1 - c, 0))
    sspec = pl.BlockSpec((1, heads, DN_HEAD_DIM, DN_HEAD_DIM), lambda c: (n - 1 - c, 0, 0, 0))

    def body(q_ref, k_ref, v_ref, gb_ref, st_ref, do_ref, dq_ref, dk_ref, dv_ref, dgb_ref, dstate):
        @pl.when(pl.program_id(0) == 0)
        def _():
            dstate[...] = jnp.zeros_like(dstate)

        gbv = gb_ref[...]
        lane = lax.broadcasted_iota(jnp.int32, gbv.shape, 1)
        dgb = jnp.zeros(gbv.shape, F32)
        for h in range(heads):
            sl = slice(h * DN_HEAD_DIM, (h + 1) * DN_HEAD_DIM)
            _, vjp = jax.vjp(_delta_chunk, q_ref[:, sl], k_ref[:, sl], v_ref[:, sl],
                             gbv[:, heads + h:heads + h + 1], gbv[:, h:h + 1], st_ref[0, h])
            dq, dk, dv, dg, db, ds0 = vjp((do_ref[:, sl], dstate[h]))
            dq_ref[:, sl] = dq
            dk_ref[:, sl] = dk
            dv_ref[:, sl] = dv
            dstate[h] = ds0
            dgb = dgb + jnp.where(lane == h, db, 0.0) + jnp.where(lane == heads + h, dg, 0.0)
        dgb_ref[...] = dgb

    return pl.pallas_call(
        body, name="dn_delta_bwd", grid=(n,),
        in_specs=[blk, blk, blk, gspec, sspec, blk],
        out_specs=[blk, blk, blk, gspec],
        out_shape=[jax.ShapeDtypeStruct((s, hd), F32)] * 3 + [jax.ShapeDtypeStruct((s, LANES), F32)],
        scratch_shapes=[pltpu.VMEM((heads, DN_HEAD_DIM, DN_HEAD_DIM), F32)],
        compiler_params=pltpu.CompilerParams(dimension_semantics=("arbitrary",)),
    )(q, k, v, gb, states, do)


def _dev_index(px, py, pc):
    return 4 * px + 2 * py + pc


def _all_gather(arrs, name):
    n = len(arrs)

    def body(*refs):
        xs, outs = refs[:n], refs[n:2 * n]
        send_sems, recv_sems, local_sems = refs[2 * n:]
        x, y, c = lax.axis_index("x"), lax.axis_index("y"), lax.axis_index("c")
        me, sibling = (x, y, c), (x, y, 1 - c)
        chips = [(1 - x, y), (x, 1 - y), (1 - x, 1 - y)]

        def copy(a, kk, block, to, src=None):
            dst = outs[a].at[_dev_index(*block)]
            return pltpu.make_async_remote_copy(
                src_ref=dst if src is None else src, dst_ref=dst,
                send_sem=send_sems.at[a * 7 + kk], recv_sem=recv_sems.at[a * 7 + kk],
                device_id=to, device_id_type=MESH_IDS)

        mine = [pltpu.make_async_copy(xs[a], outs[a].at[_dev_index(*me)], local_sems.at[a]) for a in range(n)]
        for cp in mine:
            cp.start()
        first = []
        for a in range(n):
            first.append(copy(a, 0, me, sibling, src=xs[a]))
            first += [copy(a, 1 + j, me, (*chip, c), src=xs[a]) for j, chip in enumerate(chips)]
        for cp in first:
            cp.start()
        passed = []
        for j, chip in enumerate(chips):
            for a in range(n):
                copy(a, 1 + j, (*chip, c), me).wait_recv()
                fwd = copy(a, 4 + j, (*chip, c), sibling)
                fwd.start()
                passed.append(fwd)
        for a in range(n):
            copy(a, 0, sibling, me).wait_recv()
        for j, chip in enumerate(chips):
            for a in range(n):
                copy(a, 4 + j, (*chip, 1 - c), me).wait_recv()
        for cp in first + passed:
            cp.wait_send()
        for cp in mine:
            cp.wait()

    hbm = pl.BlockSpec(memory_space=pltpu.HBM)
    res = pl.pallas_call(
        body, name=name,
        in_specs=[hbm] * n, out_specs=[hbm] * n,
        out_shape=[jax.ShapeDtypeStruct((N_DEV,) + a.shape, a.dtype) for a in arrs],
        scratch_shapes=[pltpu.SemaphoreType.DMA((7 * n,)), pltpu.SemaphoreType.DMA((7 * n,)),
                        pltpu.SemaphoreType.DMA((n,))],
    )(*arrs)
    return list(res)


def _all_to_all(arrs, name):
    n = len(arrs)
    flips = [(0, 0, 1), (1, 0, 0), (0, 1, 0), (1, 1, 0), (1, 0, 1), (0, 1, 1), (1, 1, 1)]

    def body(*refs):
        xs, outs = refs[:n], refs[n:2 * n]
        send_sems, recv_sems, local_sems = refs[2 * n:]
        x, y, c = lax.axis_index("x"), lax.axis_index("y"), lax.axis_index("c")
        me = (x, y, c)
        my_idx = _dev_index(*me)

        def peer(f):
            return tuple(1 - v if fl else v for v, fl in zip(me, f))

        mine = [pltpu.make_async_copy(xs[a].at[my_idx], outs[a].at[my_idx], local_sems.at[a]) for a in range(n)]
        for cp in mine:
            cp.start()
        sends = []
        for kk, f in enumerate(flips):
            p = peer(f)
            for a in range(n):
                sends.append(pltpu.make_async_remote_copy(
                    src_ref=xs[a].at[_dev_index(*p)], dst_ref=outs[a].at[my_idx],
                    send_sem=send_sems.at[a * 7 + kk], recv_sem=recv_sems.at[a * 7 + kk],
                    device_id=p, device_id_type=MESH_IDS))
        for cp in sends:
            cp.start()
        for kk, f in enumerate(flips):
            p = peer(f)
            for a in range(n):
                land = outs[a].at[_dev_index(*p)]
                pltpu.make_async_remote_copy(
                    src_ref=land, dst_ref=land, send_sem=send_sems.at[a * 7 + kk],
                    recv_sem=recv_sems.at[a * 7 + kk], device_id=p, device_id_type=MESH_IDS).wait_recv()
        for cp in sends:
            cp.wait_send()
        for cp in mine:
            cp.wait()

    hbm = pl.BlockSpec(memory_space=pltpu.HBM)
    res = pl.pallas_call(
        body, name=name,
        in_specs=[hbm] * n, out_specs=[hbm] * n,
        out_shape=[jax.ShapeDtypeStruct(a.shape, a.dtype) for a in arrs],
        scratch_shapes=[pltpu.SemaphoreType.DMA((7 * n,)), pltpu.SemaphoreType.DMA((7 * n,)),
                        pltpu.SemaphoreType.DMA((n,))],
    )(*arrs)
    return list(res)


def _slot_sum(g, name, tr):
    _, r, c = g.shape
    tr = min(tr, r)
    assert r % tr == 0

    def body(g_ref, o_ref):
        acc = g_ref[0].astype(F32)
        for s in range(1, N_DEV):
            acc = acc + g_ref[s].astype(F32)
        o_ref[...] = acc

    return pl.pallas_call(
        body, name=name, grid=(r // tr,),
        in_specs=[pl.BlockSpec((N_DEV, tr, c), lambda i: (0, i, 0))],
        out_specs=pl.BlockSpec((tr, c), lambda i: (i, 0)),
        out_shape=jax.ShapeDtypeStruct((r, c), F32),
        compiler_params=pltpu.CompilerParams(dimension_semantics=("parallel",)),
    )(g)


def _adamw(w, g, m, v, name, tr=256):
    r, c = w.shape
    tr = min(tr, r)
    assert r % tr == 0
    c1 = 1.0 / (1.0 - ADAM_B1 ** ADAM_STEP)
    c2 = 1.0 / (1.0 - ADAM_B2 ** ADAM_STEP)

    def body(w_ref, g_ref, m_ref, v_ref, d_ref, nm_ref, nv_ref):
        gg = g_ref[...]
        nm = ADAM_B1 * m_ref[...] + (1.0 - ADAM_B1) * gg
        nv = ADAM_B2 * v_ref[...] + (1.0 - ADAM_B2) * (gg * gg)
        d_ref[...] = -ADAM_LR * ((nm * c1) / (jnp.sqrt(nv * c2) + ADAM_EPS) + ADAM_WD * w_ref[...])
        nm_ref[...] = nm
        nv_ref[...] = nv

    spec = pl.BlockSpec((tr, c), lambda i: (i, 0))
    return pl.pallas_call(
        body, name=name, grid=(r // tr,), in_specs=[spec] * 4, out_specs=[spec] * 3,
        out_shape=[jax.ShapeDtypeStruct((r, c), F32)] * 3,
        compiler_params=pltpu.CompilerParams(dimension_semantics=("parallel",)),
    )(w, g, m, v)


def _rms_fwd(x, g, name, ts=512):
    s, d = x.shape

    def fn(first, last, xv, gv):
        return [_rms(xv, gv[...])], []

    return _rowwise(fn, n_rows=s, ts=ts, name=name, rows=[(x, 0, d)], vecs=[g], row_outs=[(d, BF16)])[0]


def _rms_bwd(x, dn, dres, g, name, ts=256):
    s, d = x.shape

    def fn(first, last, xv, dnv, drv, gv):
        _, vjp = jax.vjp(_rms, xv, gv[...])
        dx, dg = vjp(dnv.astype(F32))
        return [drv + dx], [dg]

    return _rowwise(fn, n_rows=s, ts=ts, name=name, rows=[(x, 0, d), (dn, 0, d), (dres, 0, d)], vecs=[g],
                    row_outs=[(d, F32)], acc_outs=[(1, d)])


def _loss_head(h, tgt, g, ts=256):
    s, d = h.shape

    def fn(first, last, hv, tv, gv):
        def f(hh, gg):
            e = _rms(hh, gg) - tv
            per_row = jnp.mean(e * e, axis=-1, keepdims=True)
            return 0.5 * jnp.sum(per_row, axis=0, keepdims=True)

        l, vjp = jax.vjp(f, hv, gv[...])
        dh, dg = vjp(jnp.ones((1, 1), F32))
        return [dh], [dg, jnp.zeros((1, LANES), F32) + l]

    return _rowwise(fn, n_rows=s, ts=ts, name="loss_head", rows=[(h, 0, d), (tgt, 0, d)], vecs=[g],
                    row_outs=[(d, F32)], acc_outs=[(1, d), (1, LANES)])


def _dn_pre_fwd(qkvz, ba, wconv, alog, dt, heads, ts=128):
    s = qkvz.shape[0]
    d3 = wconv.shape[1]
    d = d3 // 3

    def fn(first, last, xc, bav, xp, wv, av, dv):
        xext = jnp.concatenate([jnp.where(first, 0.0, xp), xc], axis=0)
        cv = _causal_conv(xext, wv, DN_CONV, DN_HALO - (DN_CONV - 1), xc.shape[0])
        return list(_dn_point(cv, bav, av[...], dv[...], heads)), []

    return _rowwise(fn, n_rows=s, ts=ts, name="dn_pre_fwd", rows=[(qkvz, 0, d3), (ba, 0, LANES)],
                    prevs=[(qkvz, 0, d3, DN_HALO)], vecs=[wconv, alog, dt],
                    row_outs=[(d, F32), (d, F32), (d, F32), (LANES, F32)])


def _dn_pre_bwd1(qkvz, ba, wconv, alog, dt, dq, dk, dv, dgb, heads, ts=128):
    s = qkvz.shape[0]
    d3 = wconv.shape[1]
    d = d3 // 3

    def fn(first, last, xc, bav, dqv, dkv, dvv, dgbv, xp, wv, av, dtv):
        xext = jnp.concatenate([jnp.where(first, 0.0, xp), xc], axis=0)
        cv = _causal_conv(xext, wv, DN_CONV, DN_HALO - (DN_CONV - 1), xc.shape[0])
        _, vjp = jax.vjp(functools.partial(_dn_point, heads=heads), cv, bav, av[...], dtv[...])
        dc, dba, da, ddt = vjp((dqv, dkv, dvv, dgbv))
        return [dc, dba], [da, ddt]

    return _rowwise(fn, n_rows=s, ts=ts, name="dn_pre_bwd1",
                    rows=[(qkvz, 0, d3), (ba, 0, LANES), (dq, 0, d), (dk, 0, d), (dv, 0, d), (dgb, 0, LANES)],
                    prevs=[(qkvz, 0, d3, DN_HALO)], vecs=[wconv, alog, dt],
                    row_outs=[(d3, F32), (LANES, BF16)], acc_outs=[(1, LANES), (1, LANES)])


def _dn_pre_bwd2(dc, qkvz, dz, wconv, ts=128):
    s = qkvz.shape[0]
    d3 = wconv.shape[1]
    d = d3 // 3

    def fn(first, last, dcc, xc, dzv, xp, dcn, wv):
        n = dcc.shape[0]
        dcext = jnp.concatenate([dcc, jnp.where(last, 0.0, dcn)], axis=0)
        xext = jnp.concatenate([jnp.where(first, 0.0, xp), xc], axis=0)
        dx = None
        dw = []
        for j in range(DN_CONV):
            term = _shift_rows(dcext, DN_CONV - 1 - j)[:n] * wv[j:j + 1, :]
            dx = term if dx is None else dx + term
            dw.append(_colsum(dcc * _shift_rows(xext, DN_HALO - (DN_CONV - 1) + j)[:n]))
        return [jnp.concatenate([dx, dzv], axis=-1)], [_stack_rows(dw, DN_CONV)]

    return _rowwise(fn, n_rows=s, ts=ts, name="dn_pre_bwd2",
                    rows=[(dc, 0, d3), (qkvz, 0, d3), (dz, 0, d)],
                    prevs=[(qkvz, 0, d3, DN_HALO)], nexts=[(dc, 0, d3, DN_HALO)], vecs=[wconv],
                    row_outs=[(4 * d, BF16)], acc_outs=[(DN_CONV, d3)])


def _dn_post_fwd(o, qkvz, onorm, heads, ts=256):
    s, d = o.shape

    def fn(first, last, ov, zv, nv):
        return [_dn_post(ov, zv, nv[...], heads)], []

    return _rowwise(fn, n_rows=s, ts=ts, name="dn_post_fwd", rows=[(o, 0, d), (qkvz, 3, d)], vecs=[onorm],
                    row_outs=[(d, BF16)])[0]


def _dn_post_bwd(o, qkvz, onorm, dog, heads, ts=256):
    s, d = o.shape

    def fn(first, last, ov, zv, dv, nv):
        _, vjp = jax.vjp(functools.partial(_dn_post, heads=heads), ov, zv, nv[...])
        do, dz, dn = vjp(dv.astype(F32))
        return [do, dz], [dn]

    return _rowwise(fn, n_rows=s, ts=ts, name="dn_post_bwd", rows=[(o, 0, d), (qkvz, 3, d), (dog, 0, d)],
                    vecs=[onorm], row_outs=[(d, F32), (d, F32)], acc_outs=[(1, DN_HEAD_DIM)])


def _cv_mid_fwd(u, wdw, bdw, lng, lnb, ts=256):
    s = u.shape[0]
    d = u.shape[1] // 2

    def fn(first, last, uc, up, wv, bv, gv, lbv):
        uext = jnp.concatenate([jnp.where(first, 0.0, up), uc], axis=0)
        glu = uext[:, :d] * _sigmoid(uext[:, d:])
        c = _causal_conv(glu, wv, CV_WIDTH, CV_HALO - (CV_WIDTH - 1), uc.shape[0]) + bv[...]
        return [c, _ln_silu(c, gv[...], lbv[...])], []

    return _rowwise(fn, n_rows=s, ts=ts, name="cv_mid_fwd", rows=[(u, 0, 2 * d)], prevs=[(u, 0, 2 * d, CV_HALO)],
                    vecs=[wdw, bdw, lng, lnb], row_outs=[(d, F32), (d, BF16)])


def _cv_mid_bwd1(c, ds, dhp, lng, lnb, ts=256):
    s, d = c.shape

    def fn(first, last, cv, dsv, dhv, gv, bv):
        _, vjp = jax.vjp(_ln_silu, cv, gv[...], bv[...])
        dc, dg, db = vjp(dsv.astype(F32))
        return [dc], [dg, db, _colsum(dc), _colsum(dhv)]

    return _rowwise(fn, n_rows=s, ts=ts, name="cv_mid_bwd1", rows=[(c, 0, d), (ds, 0, d), (dhp, 0, d)],
                    vecs=[lng, lnb], row_outs=[(d, F32)], acc_outs=[(1, d)] * 4)


def _cv_mid_bwd2(dc, u, wdw, ts=256):
    s, d = dc.shape

    def fn(first, last, dcc, uc, up, dcn, wv):
        n = dcc.shape[0]
        dcext = jnp.concatenate([dcc, jnp.where(last, 0.0, dcn)], axis=0)
        uext = jnp.concatenate([jnp.where(first, 0.0, up), uc], axis=0)
        glu = uext[:, :d] * _sigmoid(uext[:, d:])
        dglu = None
        dw = []
        for j in range(CV_WIDTH):
            term = _shift_rows(dcext, CV_WIDTH - 1 - j)[:n] * wv[j:j + 1, :]
            dglu = term if dglu is None else dglu + term
            dw.append(_colsum(dcc * _shift_rows(glu, CV_HALO - (CV_WIDTH - 1) + j)[:n]))
        u1, sg = uc[:, :d], _sigmoid(uc[:, d:])
        du = jnp.concatenate([dglu * sg, dglu * u1 * sg * (1.0 - sg)], axis=-1)
        return [du], [_stack_rows(dw, CV_HALO), _colsum(du)]

    return _rowwise(fn, n_rows=s, ts=ts, name="cv_mid_bwd2", rows=[(dc, 0, d), (u, 0, 2 * d)],
                    prevs=[(u, 0, 2 * d, CV_HALO)], nexts=[(dc, 0, d, CV_HALO)], vecs=[wdw],
                    row_outs=[(2 * d, BF16)], acc_outs=[(CV_HALO, d), (1, 2 * d)])


def _attn_fwd(q, k, v, name, ts=256):
    s, d = q.shape

    def fn(first, last, qv, kv, vv):
        return [_attn_tile(qv.astype(F32), kv[...].astype(F32), vv[...].astype(F32))], []

    return _rowwise(fn, n_rows=s, ts=ts, name=name, rows=[(q, 0, d)], vecs=[k, v], row_outs=[(d, BF16)])[0]


def _attn_bwd(q, k, v, do, name, ts=256):
    s, d = q.shape
    m = k.shape[0]

    def fn(first, last, qv, dov, kv, vv):
        _, vjp = jax.vjp(_attn_tile, qv.astype(F32), kv[...].astype(F32), vv[...].astype(F32))
        dq, dk, dv = vjp(dov.astype(F32))
        return [dq], [dk, dv]

    return _rowwise(fn, n_rows=s, ts=ts, name=name, rows=[(q, 0, d), (do, 0, d)], vecs=[k, v],
                    row_outs=[(d, BF16)], acc_outs=[(m, d), (m, d)])


def _pad_lanes(a, off=0):
    r, n = a.shape
    return jnp.pad(a, ((0, 0), (off, LANES - off - n)))


def _local_step(x, mem, tgt, w):
    s, d = x.shape
    heads = d // DN_HEAD_DIM
    g = {}

    def add_res(acc, res):
        return (res + acc,)

    w_in = w["dn_w_in"][0]
    assert w_in.shape[1] == 4 * d + 2 * heads
    w_qkvz = w_in[:, :4 * d]
    w_ba = _pad_lanes(w_in[:, 4 * d:])
    dn_norm = w["dn_norm"]
    alog = _pad_lanes(w["dn_a_log"], heads)
    dtb = _pad_lanes(w["dn_dt_bias"], heads)
    wconv = w["dn_w_conv"][0]
    n0 = _rms_fwd(x, dn_norm, "dn_rms")
    qkvz = _matmul(n0, w_qkvz, "nn", [F32], name="dn_in_proj")
    ba = _matmul(n0, w_ba, "nn", [F32], name="dn_in_proj_ba")
    q, k, v, gb = _dn_pre_fwd(qkvz, ba, wconv, alog, dtb, heads)
    o, states = _delta_fwd(q, k, v, gb, heads)
    og = _dn_post_fwd(o, qkvz, w["dn_out_norm"], heads)
    h1 = _matmul(og, w["dn_w_out"][0], "nn", [F32], name="dn_out_proj", epi=add_res, mn_extras=[x])

    def xattn_fwd(h, layer):
        nq = _rms_fwd(h, w["xa_norm"][layer:layer + 1], f"xa{layer}_rms")
        qx = _matmul(nq, w["xa_w_q"][layer], "nn", [BF16], name=f"xa{layer}_q")
        mn = _rms_fwd(mem, w["xa_mem_norm"][layer:layer + 1], f"xa{layer}_mem_rms")
        kv = _matmul(mn, w["xa_w_kv"][layer], "nn", [BF16], name=f"xa{layer}_kv")
        kx, vx = kv[:, :d], kv[:, d:]
        ox = _attn_fwd(qx, kx, vx, f"xa{layer}_attn")
        hn = _matmul(ox, w["xa_w_o"][layer], "nn", [F32], name=f"xa{layer}_o", epi=add_res, mn_extras=[h])
        return hn, (h, nq, qx, mn, kx, vx, ox)

    def mlp_fwd(h, layer):
        nm = _rms_fwd(h, w["mlp_norm"][layer:layer + 1], f"mlp{layer}_rms")

        def epi(acc):
            r = jnp.maximum(acc, 0.0)
            return acc, r * r

        u, a = _matmul(nm, w["mlp_w_up"][layer], "nn", [BF16, BF16], name=f"mlp{layer}_up", epi=epi)
        hn = _matmul(a, w["mlp_w_down"][layer], "nn", [F32], name=f"mlp{layer}_down", epi=add_res,
                     mn_extras=[h], tk=2048)
        return hn, (h, nm, u, a)

    h2, xa0 = xattn_fwd(h1, 0)
    h3, mlp0 = mlp_fwd(h2, 0)

    n1 = _rms_fwd(h3, w["cv_norm"], "cv_rms")
    u_cv = _matmul(n1, w["cv_w_pw1"][0], "nn", [F32], name="cv_pw1", epi=lambda acc, b: (acc + b,),
                   row_extras=[w["cv_b_pw1"]])
    wdw = jnp.pad(w["cv_w_dw"][0], ((0, CV_HALO - CV_WIDTH), (0, 0)))
    c_cv, s_cv = _cv_mid_fwd(u_cv, wdw, w["cv_b_dw"], w["cv_ln_g"], w["cv_ln_b"])
    h4 = _matmul(s_cv, w["cv_w_pw2"][0], "nn", [F32], name="cv_pw2",
                 epi=lambda acc, res, b: (res + acc + b,), mn_extras=[h3], row_extras=[w["cv_b_pw2"]])
    h5, xa1 = xattn_fwd(h4, 1)
    h6, mlp1 = mlp_fwd(h5, 1)

    fnorm = w["final_norm"].reshape(1, d)
    dh, g_fn, loss = _loss_head(h6, tgt, fnorm)
    g["final_norm"] = g_fn.reshape(d)

    def mlp_bwd(dh, layer, saved):
        h, nm, u, a = saved
        du = _matmul(dh, w["mlp_w_down"][layer], "nt", [BF16], name=f"mlp{layer}_down_dx",
                     epi=lambda acc, uu: (acc * 2.0 * jnp.maximum(uu.astype(F32), 0.0),), mn_extras=[u])
        gdown = _matmul(a, dh, "tn", [BF16], name=f"mlp{layer}_down_dw")
        dn = _matmul(du, w["mlp_w_up"][layer], "nt", [F32], name=f"mlp{layer}_up_dx", tk=2048)
        gup = _matmul(nm, du, "tn", [BF16], name=f"mlp{layer}_up_dw")
        dhn, gn = _rms_bwd(h, dn, dh, w["mlp_norm"][layer:layer + 1], f"mlp{layer}_rms_bwd")
        return dhn, gup, gdown, gn

    def xattn_bwd(dh, layer, saved):
        h, nq, qx, mn, kx, vx, ox = saved
        dox = _matmul(dh, w["xa_w_o"][layer], "nt", [BF16], name=f"xa{layer}_o_dx")
        go = _matmul(ox, dh, "tn", [BF16], name=f"xa{layer}_o_dw")
        dqx, dkx, dvx = _attn_bwd(qx, kx, vx, dox, f"xa{layer}_attn_bwd")
        dn = _matmul(dqx, w["xa_w_q"][layer], "nt", [F32], name=f"xa{layer}_q_dx")
        gq = _matmul(nq, dqx, "tn", [BF16], name=f"xa{layer}_q_dw")
        dkv = jnp.concatenate([dkx, dvx], axis=-1)
        gkv = _matmul(mn, dkv, "tn", [BF16], name=f"xa{layer}_kv_dw")
        dmn = _matmul(dkv, w["xa_w_kv"][layer], "nt", [F32], name=f"xa{layer}_kv_dx", tk=2048)
        _, gmem = _rms_bwd(mem, dmn, dmn, w["xa_mem_norm"][layer:layer + 1], f"xa{layer}_mem_rms_bwd")
        dhn, gn = _rms_bwd(h, dn, dh, w["xa_norm"][layer:layer + 1], f"xa{layer}_rms_bwd")
        return dhn, gq, gkv, go, gn, gmem

    dh, gup1, gdown1, gmn1 = mlp_bwd(dh, 1, mlp1)
    dh, gq1, gkv1, go1, gxn1, gmem1 = xattn_bwd(dh, 1, xa1)

    ds_cv = _matmul(dh, w["cv_w_pw2"][0], "nt", [BF16], name="cv_pw2_dx")
    g["cv_w_pw2"] = _matmul(s_cv, dh, "tn", [BF16], name="cv_pw2_dw")[None]
    dc_cv, g_lng, g_lnb, g_bdw, g_b2 = _cv_mid_bwd1(c_cv, ds_cv, dh, w["cv_ln_g"], w["cv_ln_b"])
    du_cv, g_wdw, g_b1 = _cv_mid_bwd2(dc_cv, u_cv, wdw)
    dn1 = _matmul(du_cv, w["cv_w_pw1"][0], "nt", [F32], name="cv_pw1_dx", tk=2048)
    g["cv_w_pw1"] = _matmul(n1, du_cv, "tn", [BF16], name="cv_pw1_dw")[None]
    dh, g_cvn = _rms_bwd(h3, dn1, dh, w["cv_norm"], "cv_rms_bwd")
    g.update(cv_ln_g=g_lng, cv_ln_b=g_lnb, cv_b_dw=g_bdw, cv_b_pw2=g_b2, cv_b_pw1=g_b1, cv_norm=g_cvn,
             cv_w_dw=g_wdw[:CV_WIDTH][None])

    dh, gup0, gdown0, gmn0 = mlp_bwd(dh, 0, mlp0)
    dh, gq0, gkv0, go0, gxn0, gmem0 = xattn_bwd(dh, 0, xa0)
    g["mlp_w_up"] = jnp.stack([gup0, gup1])
    g["mlp_w_down"] = jnp.stack([gdown0, gdown1])
    g["mlp_norm"] = jnp.concatenate([gmn0, gmn1], axis=0)
    g["xa_w_q"] = jnp.stack([gq0, gq1])
    g["xa_w_kv"] = jnp.stack([gkv0, gkv1])
    g["xa_w_o"] = jnp.stack([go0, go1])
    g["xa_norm"] = jnp.concatenate([gxn0, gxn1], axis=0)
    g["xa_mem_norm"] = jnp.concatenate([gmem0, gmem1], axis=0)

    dog = _matmul(dh, w["dn_w_out"][0], "nt", [BF16], name="dn_out_proj_dx")
    g["dn_w_out"] = _matmul(og, dh, "tn", [BF16], name="dn_out_proj_dw")[None]
    do, dz, g_on = _dn_post_bwd(o, qkvz, w["dn_out_norm"], dog, heads)
    dq, dk, dv, dgb = _delta_bwd(q, k, v, gb, states, do, heads)
    dc, dba, g_alog, g_dt = _dn_pre_bwd1(qkvz, ba, wconv, alog, dtb, dq, dk, dv, dgb, heads)
    dqkvz, g_wconv = _dn_pre_bwd2(dc, qkvz, dz, wconv)
    dn0a = _matmul(dba, w_ba, "nt", [F32], name="dn_in_proj_ba_dx")
    dn0 = _matmul(dqkvz, w_qkvz, "nt", [F32], name="dn_in_proj_dx", epi=add_res, mn_extras=[dn0a], tk=2048)
    g_qkvz = _matmul(n0, dqkvz, "tn", [BF16], name="dn_in_proj_dw")
    g_ba = _matmul(n0, dba, "tn", [BF16], name="dn_in_proj_ba_dw")
    g["dn_w_in"] = jnp.concatenate([g_qkvz, g_ba[:, :2 * heads]], axis=1)[None]
    grad_x, g_dnn = _rms_bwd(x, dn0, dh, dn_norm, "dn_rms_bwd")
    g.update(dn_norm=g_dnn, dn_out_norm=g_on, dn_w_conv=g_wconv[None],
             dn_a_log=g_alog[:, heads:2 * heads], dn_dt_bias=g_dt[:, heads:2 * heads])
    return loss, grad_x, g


def _round_up(n, m):
    return (n + m - 1) // m * m


def _pack_rows(parts, cols, row_mult):
    lead = parts[0].shape[:-1]
    flat, offs, off = [], [], 0
    for p in parts:
        n = _round_up(p.shape[-1], cols)
        flat.append(jnp.pad(p, [(0, 0)] * len(lead) + [(0, n - p.shape[-1])]))
        offs.append(off)
        off += n
    total = _round_up(off, cols * row_mult)
    if total > off:
        flat.append(jnp.zeros(lead + (total - off,), parts[0].dtype))
    return jnp.concatenate(flat, axis=-1).reshape(lead + (total // cols, cols)), offs


def _unpack(packed, offs, shapes):
    lead = packed.shape[:-2]
    flat = packed.reshape(lead + (-1,))
    out = []
    for off, shp in zip(offs, shapes):
        n = 1
        for v in shp:
            n *= v
        out.append(flat[..., off:off + n].reshape(lead + tuple(shp)))
    return out


def _full_from_gathered(gth, axis):
    _, l, r, c = gth.shape
    if axis == 1:
        return gth.transpose(1, 0, 2, 3).reshape(l, N_DEV * r, c)
    return gth.transpose(1, 2, 0, 3).reshape(l, r, N_DEV * c)


def _device_major(full, axis):
    l, r, c = full.shape
    if axis == 1:
        return full.reshape(l, N_DEV, r // N_DEV, c).transpose(1, 0, 2, 3)
    return full.reshape(l, r, N_DEV, c // N_DEV).transpose(2, 0, 1, 3)


def kernel(x, mem, dn_norm, dn_w_in, dn_w_conv, dn_a_log, dn_dt_bias, dn_out_norm, dn_w_out, cv_norm, cv_w_pw1, cv_b_pw1, cv_w_dw, cv_b_dw, cv_ln_g, cv_ln_b, cv_w_pw2, cv_b_pw2, xa_norm, xa_mem_norm, xa_w_q, xa_w_kv, xa_w_o, mlp_norm, mlp_w_up, mlp_w_down, final_norm, loss_target, m_dn_norm, m_dn_w_in, m_dn_w_conv, m_dn_a_log, m_dn_dt_bias, m_dn_out_norm, m_dn_w_out, m_cv_norm, m_cv_w_pw1, m_cv_b_pw1, m_cv_w_dw, m_cv_b_dw, m_cv_ln_g, m_cv_ln_b, m_cv_w_pw2, m_cv_b_pw2, m_xa_norm, m_xa_mem_norm, m_xa_w_q, m_xa_w_kv, m_xa_w_o, m_mlp_norm, m_mlp_w_up, m_mlp_w_down, m_final_norm, v_dn_norm, v_dn_w_in, v_dn_w_conv, v_dn_a_log, v_dn_dt_bias, v_dn_out_norm, v_dn_w_out, v_cv_norm, v_cv_w_pw1, v_cv_b_pw1, v_cv_w_dw, v_cv_b_dw, v_cv_ln_g, v_cv_ln_b, v_cv_w_pw2, v_cv_b_pw2, v_xa_norm, v_xa_mem_norm, v_xa_w_q, v_xa_w_kv, v_xa_w_o, v_mlp_norm, v_mlp_w_up, v_mlp_w_down, v_final_norm):
    wsh = dict(dn_norm=dn_norm, dn_w_in=dn_w_in, dn_w_conv=dn_w_conv, dn_a_log=dn_a_log, dn_dt_bias=dn_dt_bias, dn_out_norm=dn_out_norm, dn_w_out=dn_w_out, cv_norm=cv_norm, cv_w_pw1=cv_w_pw1, cv_b_pw1=cv_b_pw1, cv_w_dw=cv_w_dw, cv_b_dw=cv_b_dw, cv_ln_g=cv_ln_g, cv_ln_b=cv_ln_b, cv_w_pw2=cv_w_pw2, cv_b_pw2=cv_b_pw2, xa_norm=xa_norm, xa_mem_norm=xa_mem_norm, xa_w_q=xa_w_q, xa_w_kv=xa_w_kv, xa_w_o=xa_w_o, mlp_norm=mlp_norm, mlp_w_up=mlp_w_up, mlp_w_down=mlp_w_down, final_norm=final_norm)
    msh = dict(dn_norm=m_dn_norm, dn_w_in=m_dn_w_in, dn_w_conv=m_dn_w_conv, dn_a_log=m_dn_a_log, dn_dt_bias=m_dn_dt_bias, dn_out_norm=m_dn_out_norm, dn_w_out=m_dn_w_out, cv_norm=m_cv_norm, cv_w_pw1=m_cv_w_pw1, cv_b_pw1=m_cv_b_pw1, cv_w_dw=m_cv_w_dw, cv_b_dw=m_cv_b_dw, cv_ln_g=m_cv_ln_g, cv_ln_b=m_cv_ln_b, cv_w_pw2=m_cv_w_pw2, cv_b_pw2=m_cv_b_pw2, xa_norm=m_xa_norm, xa_mem_norm=m_xa_mem_norm, xa_w_q=m_xa_w_q, xa_w_kv=m_xa_w_kv, xa_w_o=m_xa_w_o, mlp_norm=m_mlp_norm, mlp_w_up=m_mlp_w_up, mlp_w_down=m_mlp_w_down, final_norm=m_final_norm)
    vsh = dict(dn_norm=v_dn_norm, dn_w_in=v_dn_w_in, dn_w_conv=v_dn_w_conv, dn_a_log=v_dn_a_log, dn_dt_bias=v_dn_dt_bias, dn_out_norm=v_dn_out_norm, dn_w_out=v_dn_w_out, cv_norm=v_cv_norm, cv_w_pw1=v_cv_w_pw1, cv_b_pw1=v_cv_b_pw1, cv_w_dw=v_cv_w_dw, cv_b_dw=v_cv_b_dw, cv_ln_g=v_cv_ln_g, cv_ln_b=v_cv_ln_b, cv_w_pw2=v_cv_w_pw2, cv_b_pw2=v_cv_b_pw2, xa_norm=v_xa_norm, xa_mem_norm=v_xa_mem_norm, xa_w_q=v_xa_w_q, xa_w_kv=v_xa_w_kv, xa_w_o=v_xa_w_o, mlp_norm=v_mlp_norm, mlp_w_up=v_mlp_w_up, mlp_w_down=v_mlp_w_down, final_norm=v_final_norm)

    big_names = [nm for nm, _ in BIG]
    big_axis = dict(BIG)

    big_pack, big_offs = _pack_rows([wsh[nm].astype(BF16).reshape(-1) for nm in big_names], PACK_COLS, PACK_ROW_BLOCK)
    small_pack, small_offs = _pack_rows([wsh[nm].reshape(-1) for nm in SMALL_SH], LANES, 8)
    big_g, small_g = _all_gather([big_pack, small_pack], "weights_all_gather")
    w = {}
    for nm, gth in zip(big_names, _unpack(big_g, big_offs, [wsh[nm].shape for nm in big_names])):
        w[nm] = _full_from_gathered(gth, big_axis[nm])
    for nm, gth in zip(SMALL_SH, _unpack(small_g, small_offs, [wsh[nm].shape for nm in SMALL_SH])):
        w[nm] = jnp.moveaxis(gth, 0, -2).reshape(gth.shape[1:-1] + (N_DEV * gth.shape[-1],))
    for nm in REPL:
        w[nm] = wsh[nm]

    loss_part, grad_x, g = _local_step(x[0], mem[0], loss_target[0], w)

    gbig_pack, _ = _pack_rows(
        [_device_major(g[nm], big_axis[nm]).astype(BF16).reshape(N_DEV, -1) for nm in big_names],
        PACK_COLS, PACK_ROW_BLOCK)
    gsmall_pack, _ = _pack_rows(
        [jnp.moveaxis(g[nm].reshape(g[nm].shape[:-1] + (N_DEV, -1)), -2, 0).reshape(N_DEV, -1) for nm in SMALL_SH],
        LANES, 8)
    rbig, rsmall = _all_to_all([gbig_pack, gsmall_pack], "grads_all_to_all")
    gbig_red = _slot_sum(rbig, "grads_big_sum", PACK_ROW_BLOCK)
    gsmall_red = _slot_sum(rsmall, "grads_small_sum", 512)
    repl_pack, repl_offs = _pack_rows([g[nm].reshape(-1) for nm in REPL], LANES, 8)
    (repl_all,) = _all_gather([repl_pack], "repl_grads_all_gather")
    repl_red = _slot_sum(repl_all, "repl_grads_sum", 512)

    gsh = {}
    for nm, val in zip(big_names, _unpack(gbig_red, big_offs, [wsh[nm].shape for nm in big_names])):
        gsh[nm] = val
    for nm, val in zip(SMALL_SH, _unpack(gsmall_red, small_offs, [wsh[nm].shape for nm in SMALL_SH])):
        gsh[nm] = val
    for nm, val in zip(REPL, _unpack(repl_red, repl_offs, [wsh[nm].shape for nm in REPL])):
        gsh[nm] = val

    delta, new_m, new_v = {}, {}, {}
    for nm in big_names:
        shp = wsh[nm].shape
        as2d = lambda a: a.reshape(-1, shp[-1])
        dl, nm_, nv_ = _adamw(as2d(wsh[nm]), as2d(gsh[nm]), as2d(msh[nm]), as2d(vsh[nm]), f"adamw_{nm}")
        delta[nm], new_m[nm], new_v[nm] = dl.reshape(shp), nm_.reshape(shp), nv_.reshape(shp)
    small_names = list(SMALL_SH) + list(REPL)
    packs = []
    for src in (wsh, gsh, msh, vsh):
        pk, sm_offs = _pack_rows([src[nm].reshape(-1) for nm in small_names], LANES, 8)
        packs.append(pk)
    outs = _adamw(*packs, "adamw_small")
    for dst, pk in zip((delta, new_m, new_v), outs):
        for nm, val in zip(small_names, _unpack(pk, sm_offs, [wsh[nm].shape for nm in small_names])):
            dst[nm] = val

    loss = lax.psum(loss_part[0, 0], ("x", "y", "c"))
    return (loss, grad_x[None], *[gsh[nm] for nm in WEIGHTS], *[delta[nm] for nm in WEIGHTS],
            *[new_m[nm] for nm in WEIGHTS], *[new_v[nm] for nm in WEIGHTS])
```

```python
import functools

import jax
import jax.numpy as jnp
from jax import lax
from jax.experimental import pallas as pl
from jax.experimental.pallas import tpu as pltpu

F32 = jnp.float32
BF16 = jnp.bfloat16
HP = lax.Precision.HIGHEST
MESH_IDS = pl.DeviceIdType.MESH

N_DEV = 8
LANES = 128
RMS_EPS = 1e-6
LN_EPS = 1e-5
DN_HEAD_DIM = 128
DN_CONV = 4
DN_CHUNK = 64
CV_WIDTH = 31
XA_HEADS = 4
DN_HALO = 8
CV_HALO = 32

ADAM_LR = 0.001
ADAM_B1 = 0.9
ADAM_B2 = 0.999
ADAM_EPS = 1e-08
ADAM_WD = 0.01
ADAM_STEP = 10

BIG = (("dn_w_in", 2), ("dn_w_out", 1), ("cv_w_pw1", 2), ("cv_w_pw2", 1), ("xa_w_q", 1), ("xa_w_kv", 2),
       ("xa_w_o", 1), ("mlp_w_up", 2), ("mlp_w_down", 1))
EXCHANGE_GROUPS = (
    (("dn_w_out", 0), ("dn_w_in", 0)),
    (("xa_w_q", 0), ("xa_w_o", 0), ("mlp_w_down", 0), ("xa_w_kv", 0), ("mlp_w_up", 0)),
    (("cv_w_pw2", 0), ("cv_w_pw1", 0)),
    (("xa_w_q", 1), ("xa_w_o", 1), ("mlp_w_down", 1), ("xa_w_kv", 1), ("mlp_w_up", 1)),
)
SMALL_SH = ("cv_norm", "cv_b_pw1", "cv_b_dw", "cv_ln_g", "cv_ln_b", "cv_b_pw2", "cv_w_dw", "dn_w_conv")
REPL = ("dn_norm", "dn_a_log", "dn_dt_bias", "dn_out_norm", "xa_norm", "xa_mem_norm", "mlp_norm", "final_norm")
WEIGHTS = ("dn_norm", "dn_w_in", "dn_w_conv", "dn_a_log", "dn_dt_bias", "dn_out_norm", "dn_w_out", "cv_norm",
           "cv_w_pw1", "cv_b_pw1", "cv_w_dw", "cv_b_dw", "cv_ln_g", "cv_ln_b", "cv_w_pw2", "cv_b_pw2", "xa_norm",
           "xa_mem_norm", "xa_w_q", "xa_w_kv", "xa_w_o", "mlp_norm", "mlp_w_up", "mlp_w_down", "final_norm")


def _dot_dims(mode, batched):
    o = 1 if batched else 0
    contract = {"nn": ((1 + o,), (o,)), "nt": ((1 + o,), (1 + o,)), "tn": ((o,), (o,))}[mode]
    return (contract, (((0,), (0,)) if batched else ((), ())))


def _bdot(a, b, mode):
    return lax.dot_general(a.astype(BF16), b.astype(BF16), _dot_dims(mode, a.ndim == 3),
                           preferred_element_type=F32)


@functools.partial(jax.custom_vjp, nondiff_argnums=(2,))
def _mm(a, b, mode):
    return _bdot(a, b, mode)


def _mm_fwd(a, b, mode):
    return _bdot(a, b, mode), (a, b)


def _mm_bwd(mode, res, ct):
    a, b = res
    if mode == "nn":
        da, db = _bdot(ct, b, "nt"), _bdot(a, ct, "tn")
    elif mode == "nt":
        da, db = _bdot(ct, b, "nn"), _bdot(ct, a, "tn")
    else:
        da, db = _bdot(b, ct, "nt"), _bdot(a, ct, "nn")
    return da.astype(a.dtype), db.astype(b.dtype)


_mm.defvjp(_mm_fwd, _mm_bwd)


def _sigmoid(x):
    return 0.5 * (jnp.tanh(0.5 * x) + 1.0)


def _silu(x):
    return x * _sigmoid(x)


def _softplus(x):
    return jnp.maximum(x, 0.0) + jnp.log(1.0 + jnp.exp(-jnp.abs(x)))


def _rms(x, g):
    r = lax.rsqrt(jnp.mean(x * x, axis=-1, keepdims=True) + RMS_EPS)
    return x * r * g


def _shift_rows(x, off):
    if off == 0:
        return x
    return pltpu.roll(x, x.shape[0] - off, 0)


def _series_dot(a, b, mode):
    return _bdot(a, b, mode)


def _chunk_masks(c):
    ii = lax.broadcasted_iota(jnp.int32, (c, c), 0)
    jj = lax.broadcasted_iota(jnp.int32, (c, c), 1)
    return (ii == jj).astype(F32), ii >= jj, ii > jj


def _neumann_inverse(lm):
    n = lm.shape[-1]
    t = -lm
    p = lm
    size = 2
    while size < n:
        size *= 2
        p = _series_dot(p, p, "nn")
        t = t + p + _series_dot(t, p, "nn")
    return t


def _apply_inverse(tm, rhs, mode):
    return rhs + _series_dot(tm, rhs, mode)


@jax.custom_vjp
def _unit_lower_solve(lm, rhs, tm):
    return _apply_inverse(tm, rhs, "nn")


def _uls_fwd(lm, rhs, tm):
    sol = _apply_inverse(tm, rhs, "nn")
    return sol, (tm, sol)


def _uls_bwd(res, ct):
    tm, sol = res
    d_rhs = _apply_inverse(tm, ct, "tn")
    return -_bdot(d_rhs, sol, "nt"), d_rhs, jnp.zeros_like(tm)


_unit_lower_solve.defvjp(_uls_fwd, _uls_bwd)


def _delta_chunk(q, k, v, gcol, bcol, s0, tm=None):
    c = q.shape[1]
    eye, causal, strict = _chunk_masks(c)
    grow = jnp.sum(eye * gcol, axis=1, keepdims=True)
    gc = jnp.sum(jnp.where(causal, grow, 0.0), axis=2, keepdims=True)
    gc_row = jnp.sum(eye * gc, axis=1, keepdims=True)
    decay = jnp.exp(jnp.where(causal, gc - gc_row, -jnp.inf))
    kb = k * bcol
    lm = jnp.where(strict, _mm(kb, k, "nt") * decay, 0.0)
    if tm is None:
        tm = _neumann_inverse(lax.stop_gradient(lm))
    egc = jnp.exp(gc)
    rhs = jnp.concatenate([v * bcol, kb * egc], axis=-1)
    sol = _unit_lower_solve(lm, rhs, tm)
    dv_ = v.shape[-1]
    u, w = sol[..., :dv_], sol[..., dv_:]
    attn = _mm(q, k, "nt") * decay
    qd = q * egc
    gl = jnp.sum(grow, axis=2, keepdims=True)
    kd = k * jnp.exp(gl - gc)
    v_new = u - _mm(w, s0, "nn")
    o = _mm(qd, s0, "nn") + _mm(attn, v_new, "nn")
    s1 = s0 * jnp.exp(gl) + _mm(kd, v_new, "tn")
    return o, s1, tm


def _dn_point(cv, ba, alog, dt, heads):
    a = _silu(cv)
    d = cv.shape[1] // 3
    qs, ks = [], []
    for h in range(heads):
        qh = a[:, h * DN_HEAD_DIM:(h + 1) * DN_HEAD_DIM]
        qs.append(qh * lax.rsqrt(jnp.sum(qh * qh, axis=-1, keepdims=True) + 1e-6) * (DN_HEAD_DIM ** -0.5))
        kh = a[:, d + h * DN_HEAD_DIM:d + (h + 1) * DN_HEAD_DIM]
        ks.append(kh * lax.rsqrt(jnp.sum(kh * kh, axis=-1, keepdims=True) + 1e-6))
    q = jnp.concatenate(qs, axis=-1)
    k = jnp.concatenate(ks, axis=-1)
    v = a[:, 2 * d:]
    lane = lax.broadcasted_iota(jnp.int32, ba.shape, 1)
    beta = _sigmoid(ba)
    g = -jnp.exp(alog) * _softplus(ba + dt)
    gb = jnp.where(lane < heads, beta, jnp.where(lane < 2 * heads, g, 0.0))
    return q, k, v, gb


def _dn_post(o, z, onorm, heads):
    outs = []
    for h in range(heads):
        oh = o[:, h * DN_HEAD_DIM:(h + 1) * DN_HEAD_DIM]
        outs.append(oh * lax.rsqrt(jnp.mean(oh * oh, axis=-1, keepdims=True) + RMS_EPS) * onorm)
    return jnp.concatenate(outs, axis=-1) * _silu(z)


def _attn_tile(q, k, v):
    hd = q.shape[1] // XA_HEADS
    outs = []
    for h in range(XA_HEADS):
        sl = slice(h * hd, (h + 1) * hd)
        s = _mm(q[:, sl], k[:, sl], "nt") * (hd ** -0.5)
        m = lax.stop_gradient(jnp.max(s, axis=-1, keepdims=True))
        e = jnp.exp(s - m)
        p = e / jnp.sum(e, axis=-1, keepdims=True)
        outs.append(_mm(p, v[:, sl], "nn"))
    return jnp.concatenate(outs, axis=-1)


def _ln_silu(c, g, b):
    mu = jnp.mean(c, axis=-1, keepdims=True)
    xc = c - mu
    y = xc * lax.rsqrt(jnp.mean(xc * xc, axis=-1, keepdims=True) + LN_EPS)
    return _silu(y * g + b)


def _causal_conv(xext, w, width, lead, ts):
    acc = None
    for j in range(width):
        term = _shift_rows(xext, lead + j)[:ts] * w[j:j + 1, :]
        acc = term if acc is None else acc + term
    return acc


def _colsum(x):
    return jnp.sum(x, axis=0, keepdims=True)


def _stack_rows(rows, n_rows):
    c = rows[0].shape[1]
    ridx = lax.broadcasted_iota(jnp.int32, (n_rows, c), 0)
    out = jnp.zeros((n_rows, c), F32)
    for j, r in enumerate(rows):
        out = out + jnp.where(ridx == j, r, 0.0)
    return out


def _matmul(a, b, mode, out_dtypes, *, name, epi=None, mn_extras=(), row_extras=(), out_dm=False,
            tm=512, tn=1024, tk=1024):
    b_dm = b.ndim == 3
    b_shape = (b.shape[1], N_DEV * b.shape[2]) if b_dm else b.shape
    if mode == "nn":
        (m, k), (k2, n) = a.shape, b_shape
    elif mode == "nt":
        (m, k), (n, k2) = a.shape, b_shape
    else:
        (k, m), (k2, n) = a.shape, b_shape
    assert k == k2, (a.shape, b.shape, mode)
    tm, tn, tk = min(tm, m), min(tn, n), min(tk, k)
    if b_dm:
        assert mode in ("nn", "nt")
        if mode == "nn":
            tn = b.shape[2]
        else:
            tk = b.shape[2]
    if out_dm:
        tn = n // N_DEV
    assert m % tm == 0 and n % tn == 0 and k % tk == 0, (m, n, k, tm, tn, tk)
    nk = k // tk
    if mode == "tn":
        a_spec = pl.BlockSpec((tk, tm), lambda j, i, kk: (kk, i))
    else:
        a_spec = pl.BlockSpec((tm, tk), lambda j, i, kk: (i, kk))
    if b_dm:
        b_spec = (pl.BlockSpec((None, tn, tk), lambda j, i, kk: (kk, j, 0)) if mode == "nt"
                  else pl.BlockSpec((None, tk, tn), lambda j, i, kk: (j, kk, 0)))
    else:
        b_spec = (pl.BlockSpec((tn, tk), lambda j, i, kk: (j, kk)) if mode == "nt"
                  else pl.BlockSpec((tk, tn), lambda j, i, kk: (kk, j)))
    mn_spec = pl.BlockSpec((tm, tn), lambda j, i, kk: (i, j))
    row_spec = pl.BlockSpec((1, tn), lambda j, i, kk: (0, j))
    n_extra = len(mn_extras) + len(row_extras)
    n_out = len(out_dtypes)
    in_specs = [a_spec, b_spec] + [mn_spec] * len(mn_extras) + [row_spec] * len(row_extras)
    args = [a, b, *mn_extras, *row_extras]
    if out_dm:
        out_specs = [pl.BlockSpec((None, tm, tn), lambda j, i, kk: (j, i, 0))] * n_out
        out_shape = [jax.ShapeDtypeStruct((N_DEV, m, tn), dt) for dt in out_dtypes]
    else:
        out_specs = [mn_spec] * n_out
        out_shape = [jax.ShapeDtypeStruct((m, n), dt) for dt in out_dtypes]
    n_in = len(args)

    def body(*refs):
        a_ref, b_ref = refs[0], refs[1]
        extras = refs[2:2 + n_extra]
        outs = refs[n_in:n_in + n_out]
        acc = refs[-1]
        kk = pl.program_id(2)

        @pl.when(kk == 0)
        def _():
            acc[...] = jnp.zeros_like(acc)

        acc[...] += _bdot(a_ref[...], b_ref[...], mode)

        @pl.when(kk == nk - 1)
        def _():
            vals = (acc[...],) if epi is None else epi(acc[...], *[e[...] for e in extras])
            for o_ref, val in zip(outs, vals):
                o_ref[...] = val.astype(o_ref.dtype)

    res = pl.pallas_call(
        body, name=name,
        grid=(n // tn, m // tm, nk),
        in_specs=in_specs, out_specs=out_specs, out_shape=out_shape,
        scratch_shapes=[pltpu.VMEM((tm, tn), F32)],
        compiler_params=pltpu.CompilerParams(dimension_semantics=("parallel", "parallel", "arbitrary")),
    )(*args)
    return res[0] if n_out == 1 else res


def _rowwise(fn, *, n_rows, ts, name, rows=(), prevs=(), nexts=(), vecs=(), row_outs=(), acc_outs=()):
    ts = min(ts, n_rows)
    assert n_rows % ts == 0
    nblk = n_rows // ts
    in_specs, args = [], []
    for arr, cb, w in rows:
        in_specs.append(pl.BlockSpec((ts, w), functools.partial(lambda i, cb: (i, cb), cb=cb)))
        args.append(arr)
    for arr, cb, w, halo in prevs:
        per = ts // halo
        in_specs.append(pl.BlockSpec(
            (halo, w), functools.partial(lambda i, cb, per: (jnp.maximum(i * per - 1, 0), cb), cb=cb, per=per)))
        args.append(arr)
    for arr, cb, w, halo in nexts:
        per = ts // halo
        last_blk = n_rows // halo - 1
        in_specs.append(pl.BlockSpec(
            (halo, w), functools.partial(lambda i, cb, per, lb: (jnp.minimum((i + 1) * per, lb), cb),
                                         cb=cb, per=per, lb=last_blk)))
        args.append(arr)
    for arr in vecs:
        in_specs.append(pl.BlockSpec(arr.shape, functools.partial(lambda i, nd: (0,) * nd, nd=arr.ndim)))
        args.append(arr)
    out_specs, out_shape = [], []
    for w, dt in row_outs:
        out_specs.append(pl.BlockSpec((ts, w), lambda i: (i, 0)))
        out_shape.append(jax.ShapeDtypeStruct((n_rows, w), dt))
    for shp in acc_outs:
        out_specs.append(pl.BlockSpec(shp, functools.partial(lambda i, nd: (0,) * nd, nd=len(shp))))
        out_shape.append(jax.ShapeDtypeStruct(shp, F32))
    n_in, n_ro, n_acc = len(args), len(row_outs), len(acc_outs)
    n_tiles = n_in - len(vecs)

    def body(*refs):
        ins, ro, ac = refs[:n_in], refs[n_in:n_in + n_ro], refs[n_in + n_ro:]
        i = pl.program_id(0)
        rvals, avals = fn(i == 0, i == nblk - 1, *[r[...] for r in ins[:n_tiles]], *ins[n_tiles:])
        for r, val in zip(ro, rvals):
            r[...] = val.astype(r.dtype)
        if n_acc:
            @pl.when(i == 0)
            def _():
                for r in ac:
                    r[...] = jnp.zeros_like(r)

            for r, val in zip(ac, avals):
                r[...] += val

    res = pl.pallas_call(
        body, name=name, grid=(nblk,), in_specs=in_specs, out_specs=out_specs, out_shape=out_shape,
        compiler_params=pltpu.CompilerParams(dimension_semantics=("arbitrary",)),
    )(*args)
    return res


def _delta_fwd(q, k, v, gb, heads):
    s, hd = q.shape
    n = s // DN_CHUNK
    blk = pl.BlockSpec((DN_CHUNK, hd), lambda c: (c, 0))
    gspec = pl.BlockSpec((DN_CHUNK, LANES), lambda c: (c, 0))

    def body(q_ref, k_ref, v_ref, gb_ref, o_ref, st_ref, tm_ref, state):
        @pl.when(pl.program_id(0) == 0)
        def _():
            state[...] = jnp.zeros_like(state)

        s0 = state[...]
        st_ref[0] = s0
        o, s1, tm = _delta_chunk(*_split_heads(q_ref, k_ref, v_ref, gb_ref[...], heads), s0)
        for h in range(heads):
            o_ref[:, h * DN_HEAD_DIM:(h + 1) * DN_HEAD_DIM] = o[h]
        state[...] = s1
        tm_ref[0] = tm

    return pl.pallas_call(
        body, name="dn_delta_fwd", grid=(n,),
        in_specs=[blk, blk, blk, gspec],
        out_specs=[blk, pl.BlockSpec((1, heads, DN_HEAD_DIM, DN_HEAD_DIM), lambda c: (c, 0, 0, 0)),
                   pl.BlockSpec((1, heads, DN_CHUNK, DN_CHUNK), lambda c: (c, 0, 0, 0))],
        out_shape=[jax.ShapeDtypeStruct((s, hd), F32),
                   jax.ShapeDtypeStruct((n, heads, DN_HEAD_DIM, DN_HEAD_DIM), F32),
                   jax.ShapeDtypeStruct((n, heads, DN_CHUNK, DN_CHUNK), F32)],
        scratch_shapes=[pltpu.VMEM((heads, DN_HEAD_DIM, DN_HEAD_DIM), F32)],
        compiler_params=pltpu.CompilerParams(dimension_semantics=("arbitrary",)),
    )(q, k, v, gb)


def _split_heads(q_ref, k_ref, v_ref, gbv, heads):
    def hs(ref):
        return jnp.stack([ref[:, h * DN_HEAD_DIM:(h + 1) * DN_HEAD_DIM] for h in range(heads)])

    gcol = jnp.stack([gbv[:, heads + h:heads + h + 1] for h in range(heads)])
    bcol = jnp.stack([gbv[:, h:h + 1] for h in range(heads)])
    return hs(q_ref), hs(k_ref), hs(v_ref), gcol, bcol


def _delta_bwd(q, k, v, gb, states, tms, do, heads):
    s, hd = q.shape
    n = s // DN_CHUNK
    blk = pl.BlockSpec((DN_CHUNK, hd), lambda c: (n - 1 - c, 0))
    gspec = pl.BlockSpec((DN_CHUNK, LANES), lambda c: (n - 1 - c, 0))
    sspec = pl.BlockSpec((1, heads, DN_HEAD_DIM, DN_HEAD_DIM), lambda c: (n - 1 - c, 0, 0, 0))
    tspec = pl.BlockSpec((1, heads, DN_CHUNK, DN_CHUNK), lambda c: (n - 1 - c, 0, 0, 0))

    def body(q_ref, k_ref, v_ref, gb_ref, st_ref, tm_ref, do_ref, dq_ref, dk_ref, dv_ref, dgb_ref, dstate):
        @pl.when(pl.program_id(0) == 0)
        def _():
            dstate[...] = jnp.zeros_like(dstate)

        gbv = gb_ref[...]
        tm = tm_ref[0]

        def chunk(qh, kh, vh, gcol, bcol, s0):
            return _delta_chunk(qh, kh, vh, gcol, bcol, s0, tm)[:2]

        _, vjp = jax.vjp(chunk, *_split_heads(q_ref, k_ref, v_ref, gbv, heads), st_ref[0])
        doh = jnp.stack([do_ref[:, h * DN_HEAD_DIM:(h + 1) * DN_HEAD_DIM] for h in range(heads)])
        dq, dk, dv, dg, db, ds0 = vjp((doh, dstate[...]))
        dstate[...] = ds0
        lane = lax.broadcasted_iota(jnp.int32, gbv.shape, 1)
        dgb = jnp.zeros(gbv.shape, F32)
        for h in range(heads):
            sl = slice(h * DN_HEAD_DIM, (h + 1) * DN_HEAD_DIM)
            dq_ref[:, sl] = dq[h]
            dk_ref[:, sl] = dk[h]
            dv_ref[:, sl] = dv[h]
            dgb = dgb + jnp.where(lane == h, db[h], 0.0) + jnp.where(lane == heads + h, dg[h], 0.0)
        dgb_ref[...] = dgb

    return pl.pallas_call(
        body, name="dn_delta_bwd", grid=(n,),
        in_specs=[blk, blk, blk, gspec, sspec, tspec, blk],
        out_specs=[blk, blk, blk, gspec],
        out_shape=[jax.ShapeDtypeStruct((s, hd), F32)] * 3 + [jax.ShapeDtypeStruct((s, LANES), F32)],
        scratch_shapes=[pltpu.VMEM((heads, DN_HEAD_DIM, DN_HEAD_DIM), F32)],
        compiler_params=pltpu.CompilerParams(dimension_semantics=("arbitrary",)),
    )(q, k, v, gb, states, tms, do)


def _dev_index(px, py, pc):
    return 4 * px + 2 * py + pc


def _all_gather(arrs, name):
    n = len(arrs)

    def body(*refs):
        xs, outs = refs[:n], refs[n:2 * n]
        send_sems, recv_sems, local_sems = refs[2 * n:]
        x, y, c = lax.axis_index("x"), lax.axis_index("y"), lax.axis_index("c")
        me, sibling = (x, y, c), (x, y, 1 - c)
        chips = [(1 - x, y), (x, 1 - y), (1 - x, 1 - y)]

        def copy(a, kk, block, to, src=None):
            dst = outs[a].at[_dev_index(*block)]
            return pltpu.make_async_remote_copy(
                src_ref=dst if src is None else src, dst_ref=dst,
                send_sem=send_sems.at[a * 7 + kk], recv_sem=recv_sems.at[a * 7 + kk],
                device_id=to, device_id_type=MESH_IDS)

        mine = [pltpu.make_async_copy(xs[a], outs[a].at[_dev_index(*me)], local_sems.at[a]) for a in range(n)]
        for cp in mine:
            cp.start()
        first = []
        for a in range(n):
            first.append(copy(a, 0, me, sibling, src=xs[a]))
            first += [copy(a, 1 + j, me, (*chip, c), src=xs[a]) for j, chip in enumerate(chips)]
        for cp in first:
            cp.start()
        passed = []
        for j, chip in enumerate(chips):
            for a in range(n):
                copy(a, 1 + j, (*chip, c), me).wait_recv()
                fwd = copy(a, 4 + j, (*chip, c), sibling)
                fwd.start()
                passed.append(fwd)
        for a in range(n):
            copy(a, 0, sibling, me).wait_recv()
        for j, chip in enumerate(chips):
            for a in range(n):
                copy(a, 4 + j, (*chip, 1 - c), me).wait_recv()
        for cp in first + passed:
            cp.wait_send()
        for cp in mine:
            cp.wait()

    hbm = pl.BlockSpec(memory_space=pltpu.HBM)
    res = pl.pallas_call(
        body, name=name,
        in_specs=[hbm] * n, out_specs=[hbm] * n,
        out_shape=[jax.ShapeDtypeStruct((N_DEV,) + a.shape, a.dtype) for a in arrs],
        scratch_shapes=[pltpu.SemaphoreType.DMA((7 * n,)), pltpu.SemaphoreType.DMA((7 * n,)),
                        pltpu.SemaphoreType.DMA((n,))],
    )(*arrs)
    return list(res)


def _all_to_all(arrs, name):
    n = len(arrs)
    flips = [(0, 0, 1), (1, 0, 0), (0, 1, 0), (1, 1, 0), (1, 0, 1), (0, 1, 1), (1, 1, 1)]

    def body(*refs):
        xs, outs = refs[:n], refs[n:2 * n]
        send_sems, recv_sems, local_sems = refs[2 * n:]
        x, y, c = lax.axis_index("x"), lax.axis_index("y"), lax.axis_index("c")
        me = (x, y, c)
        my_idx = _dev_index(*me)

        def peer(f):
            return tuple(1 - v if fl else v for v, fl in zip(me, f))

        mine = [pltpu.make_async_copy(xs[a].at[my_idx], outs[a].at[my_idx], local_sems.at[a]) for a in range(n)]
        for cp in mine:
            cp.start()
        sends = []
        for kk, f in enumerate(flips):
            p = peer(f)
            for a in range(n):
                sends.append(pltpu.make_async_remote_copy(
                    src_ref=xs[a].at[_dev_index(*p)], dst_ref=outs[a].at[my_idx],
                    send_sem=send_sems.at[a * 7 + kk], recv_sem=recv_sems.at[a * 7 + kk],
                    device_id=p, device_id_type=MESH_IDS))
        for cp in sends:
            cp.start()
        for kk, f in enumerate(flips):
            p = peer(f)
            for a in range(n):
                land = outs[a].at[_dev_index(*p)]
                pltpu.make_async_remote_copy(
                    src_ref=land, dst_ref=land, send_sem=send_sems.at[a * 7 + kk],
                    recv_sem=recv_sems.at[a * 7 + kk], device_id=p, device_id_type=MESH_IDS).wait_recv()
        for cp in sends:
            cp.wait_send()
        for cp in mine:
            cp.wait()

    hbm = pl.BlockSpec(memory_space=pltpu.HBM)
    res = pl.pallas_call(
        body, name=name,
        in_specs=[hbm] * n, out_specs=[hbm] * n,
        out_shape=[jax.ShapeDtypeStruct(a.shape, a.dtype) for a in arrs],
        scratch_shapes=[pltpu.SemaphoreType.DMA((7 * n,)), pltpu.SemaphoreType.DMA((7 * n,)),
                        pltpu.SemaphoreType.DMA((n,))],
    )(*arrs)
    return list(res)


def _slot_sum(g, name, tr):
    _, r, c = g.shape
    tr = min(tr, r)
    assert r % tr == 0

    def body(g_ref, o_ref):
        acc = g_ref[0].astype(F32)
        for s in range(1, N_DEV):
            acc = acc + g_ref[s].astype(F32)
        o_ref[...] = acc

    return pl.pallas_call(
        body, name=name, grid=(r // tr,),
        in_specs=[pl.BlockSpec((N_DEV, tr, c), lambda i: (0, i, 0))],
        out_specs=pl.BlockSpec((tr, c), lambda i: (i, 0)),
        out_shape=jax.ShapeDtypeStruct((r, c), F32),
        compiler_params=pltpu.CompilerParams(dimension_semantics=("parallel",)),
    )(g)


def _adam_update(w, gg, m, v):
    c1 = 1.0 / (1.0 - ADAM_B1 ** ADAM_STEP)
    c2 = 1.0 / (1.0 - ADAM_B2 ** ADAM_STEP)
    nm = ADAM_B1 * m + (1.0 - ADAM_B1) * gg
    nv = ADAM_B2 * v + (1.0 - ADAM_B2) * (gg * gg)
    return -ADAM_LR * ((nm * c1) / (jnp.sqrt(nv * c2) + ADAM_EPS) + ADAM_WD * w), nm, nv


def _adamw_reduce(recvs, w, m, v, name, tr=128):
    nl, r, c = w.shape
    assert len(recvs) == nl
    tr = min(tr, r)
    assert r % tr == 0
    nblk = r // tr

    def recv_spec(li):
        def index(l, i):
            return (0, jnp.where(l < li, 0, jnp.where(l > li, nblk - 1, i)), 0)
        return pl.BlockSpec((N_DEV, tr, c), index)

    def body(*refs):
        rrefs = refs[:nl]
        w_ref, m_ref, v_ref, g_ref, d_ref, nm_ref, nv_ref = refs[nl:]
        l = pl.program_id(0)
        gg = None
        for s in range(N_DEV):
            slot = rrefs[0][s].astype(F32)
            for li in range(1, nl):
                slot = jnp.where(l == li, rrefs[li][s].astype(F32), slot)
            gg = slot if gg is None else gg + slot
        g_ref[...] = gg
        d_ref[...], nm_ref[...], nv_ref[...] = _adam_update(w_ref[...], gg, m_ref[...], v_ref[...])

    spec = pl.BlockSpec((None, tr, c), lambda l, i: (l, i, 0))
    return pl.pallas_call(
        body, name=name, grid=(nl, nblk),
        in_specs=[recv_spec(li) for li in range(nl)] + [spec] * 3, out_specs=[spec] * 4,
        out_shape=[jax.ShapeDtypeStruct((nl, r, c), F32)] * 4,
        compiler_params=pltpu.CompilerParams(dimension_semantics=("arbitrary", "arbitrary")),
    )(*recvs, w, m, v)


def _adamw(w, g, m, v, name, tr=256):
    r, c = w.shape
    tr = min(tr, r)
    assert r % tr == 0

    def body(w_ref, g_ref, m_ref, v_ref, d_ref, nm_ref, nv_ref):
        d_ref[...], nm_ref[...], nv_ref[...] = _adam_update(w_ref[...], g_ref[...], m_ref[...], v_ref[...])

    spec = pl.BlockSpec((tr, c), lambda i: (i, 0))
    return pl.pallas_call(
        body, name=name, grid=(r // tr,), in_specs=[spec] * 4, out_specs=[spec] * 3,
        out_shape=[jax.ShapeDtypeStruct((r, c), F32)] * 3,
        compiler_params=pltpu.CompilerParams(dimension_semantics=("parallel",)),
    )(w, g, m, v)


def _rms_fwd(x, g, name, ts=512):
    s, d = x.shape

    def fn(first, last, xv, gv):
        return [_rms(xv, gv[...])], []

    return _rowwise(fn, n_rows=s, ts=ts, name=name, rows=[(x, 0, d)], vecs=[g], row_outs=[(d, BF16)])[0]


def _rms_bwd(x, dn, dres, g, name, ts=256):
    s, d = x.shape

    def fn(first, last, xv, dnv, drv, gv):
        _, vjp = jax.vjp(_rms, xv, gv[...])
        dx, dg = vjp(dnv.astype(F32))
        return [drv + dx], [dg]

    return _rowwise(fn, n_rows=s, ts=ts, name=name, rows=[(x, 0, d), (dn, 0, d), (dres, 0, d)], vecs=[g],
                    row_outs=[(d, F32)], acc_outs=[(1, d)])


def _loss_head(h, tgt, g, ts=256):
    s, d = h.shape

    def fn(first, last, hv, tv, gv):
        def f(hh, gg):
            e = _rms(hh, gg) - tv
            per_row = jnp.mean(e * e, axis=-1, keepdims=True)
            return 0.5 * jnp.sum(per_row, axis=0, keepdims=True)

        l, vjp = jax.vjp(f, hv, gv[...])
        dh, dg = vjp(jnp.ones((1, 1), F32))
        return [dh], [dg, jnp.zeros((1, LANES), F32) + l]

    return _rowwise(fn, n_rows=s, ts=ts, name="loss_head", rows=[(h, 0, d), (tgt, 0, d)], vecs=[g],
                    row_outs=[(d, F32)], acc_outs=[(1, d), (1, LANES)])


def _dn_pre_fwd(qkvz, ba, wconv, alog, dt, heads, ts=128):
    s = qkvz.shape[0]
    d3 = wconv.shape[1]
    d = d3 // 3

    def fn(first, last, xc, bav, xp, wv, av, dv):
        xext = jnp.concatenate([jnp.where(first, 0.0, xp), xc], axis=0)
        cv = _causal_conv(xext, wv, DN_CONV, DN_HALO - (DN_CONV - 1), xc.shape[0])
        return list(_dn_point(cv, bav, av[...], dv[...], heads)), []

    return _rowwise(fn, n_rows=s, ts=ts, name="dn_pre_fwd", rows=[(qkvz, 0, d3), (ba, 0, LANES)],
                    prevs=[(qkvz, 0, d3, DN_HALO)], vecs=[wconv, alog, dt],
                    row_outs=[(d, F32), (d, F32), (d, F32), (LANES, F32)])


def _dn_pre_bwd1(qkvz, ba, wconv, alog, dt, dq, dk, dv, dgb, heads, ts=128):
    s = qkvz.shape[0]
    d3 = wconv.shape[1]
    d = d3 // 3

    def fn(first, last, xc, bav, dqv, dkv, dvv, dgbv, xp, wv, av, dtv):
        xext = jnp.concatenate([jnp.where(first, 0.0, xp), xc], axis=0)
        cv = _causal_conv(xext, wv, DN_CONV, DN_HALO - (DN_CONV - 1), xc.shape[0])
        _, vjp = jax.vjp(functools.partial(_dn_point, heads=heads), cv, bav, av[...], dtv[...])
        dc, dba, da, ddt = vjp((dqv, dkv, dvv, dgbv))
        return [dc, dba], [da, ddt]

    return _rowwise(fn, n_rows=s, ts=ts, name="dn_pre_bwd1",
                    rows=[(qkvz, 0, d3), (ba, 0, LANES), (dq, 0, d), (dk, 0, d), (dv, 0, d), (dgb, 0, LANES)],
                    prevs=[(qkvz, 0, d3, DN_HALO)], vecs=[wconv, alog, dt],
                    row_outs=[(d3, F32), (LANES, BF16)], acc_outs=[(1, LANES), (1, LANES)])


def _dn_pre_bwd2(dc, qkvz, dz, wconv, ts=128):
    s = qkvz.shape[0]
    d3 = wconv.shape[1]
    d = d3 // 3

    def fn(first, last, dcc, xc, dzv, xp, dcn, wv):
        n = dcc.shape[0]
        dcext = jnp.concatenate([dcc, jnp.where(last, 0.0, dcn)], axis=0)
        xext = jnp.concatenate([jnp.where(first, 0.0, xp), xc], axis=0)
        dx = None
        dw = []
        for j in range(DN_CONV):
            term = _shift_rows(dcext, DN_CONV - 1 - j)[:n] * wv[j:j + 1, :]
            dx = term if dx is None else dx + term
            dw.append(_colsum(dcc * _shift_rows(xext, DN_HALO - (DN_CONV - 1) + j)[:n]))
        return [jnp.concatenate([dx, dzv], axis=-1)], [_stack_rows(dw, DN_CONV)]

    return _rowwise(fn, n_rows=s, ts=ts, name="dn_pre_bwd2",
                    rows=[(dc, 0, d3), (qkvz, 0, d3), (dz, 0, d)],
                    prevs=[(qkvz, 0, d3, DN_HALO)], nexts=[(dc, 0, d3, DN_HALO)], vecs=[wconv],
                    row_outs=[(4 * d, BF16)], acc_outs=[(DN_CONV, d3)])


def _dn_post_fwd(o, qkvz, onorm, heads, ts=256):
    s, d = o.shape

    def fn(first, last, ov, zv, nv):
        return [_dn_post(ov, zv, nv[...], heads)], []

    return _rowwise(fn, n_rows=s, ts=ts, name="dn_post_fwd", rows=[(o, 0, d), (qkvz, 3, d)], vecs=[onorm],
                    row_outs=[(d, BF16)])[0]


def _dn_post_bwd(o, qkvz, onorm, dog, heads, ts=256):
    s, d = o.shape

    def fn(first, last, ov, zv, dv, nv):
        _, vjp = jax.vjp(functools.partial(_dn_post, heads=heads), ov, zv, nv[...])
        do, dz, dn = vjp(dv.astype(F32))
        return [do, dz], [dn]

    return _rowwise(fn, n_rows=s, ts=ts, name="dn_post_bwd", rows=[(o, 0, d), (qkvz, 3, d), (dog, 0, d)],
                    vecs=[onorm], row_outs=[(d, F32), (d, F32)], acc_outs=[(1, DN_HEAD_DIM)])


def _cv_mid_fwd(u, wdw, bdw, lng, lnb, ts=256):
    s = u.shape[0]
    d = u.shape[1] // 2

    def fn(first, last, uc, up, wv, bv, gv, lbv):
        uext = jnp.concatenate([jnp.where(first, 0.0, up), uc], axis=0)
        glu = uext[:, :d] * _sigmoid(uext[:, d:])
        c = _causal_conv(glu, wv, CV_WIDTH, CV_HALO - (CV_WIDTH - 1), uc.shape[0]) + bv[...]
        return [c, _ln_silu(c, gv[...], lbv[...])], []

    return _rowwise(fn, n_rows=s, ts=ts, name="cv_mid_fwd", rows=[(u, 0, 2 * d)], prevs=[(u, 0, 2 * d, CV_HALO)],
                    vecs=[wdw, bdw, lng, lnb], row_outs=[(d, F32), (d, BF16)])


def _cv_mid_bwd1(c, ds, dhp, lng, lnb, ts=256):
    s, d = c.shape

    def fn(first, last, cv, dsv, dhv, gv, bv):
        _, vjp = jax.vjp(_ln_silu, cv, gv[...], bv[...])
        dc, dg, db = vjp(dsv.astype(F32))
        return [dc], [dg, db, _colsum(dc), _colsum(dhv)]

    return _rowwise(fn, n_rows=s, ts=ts, name="cv_mid_bwd1", rows=[(c, 0, d), (ds, 0, d), (dhp, 0, d)],
                    vecs=[lng, lnb], row_outs=[(d, F32)], acc_outs=[(1, d)] * 4)


def _cv_mid_bwd2(dc, u, wdw, ts=256):
    s, d = dc.shape

    def fn(first, last, dcc, uc, up, dcn, wv):
        n = dcc.shape[0]
        dcext = jnp.concatenate([dcc, jnp.where(last, 0.0, dcn)], axis=0)
        uext = jnp.concatenate([jnp.where(first, 0.0, up), uc], axis=0)
        glu = uext[:, :d] * _sigmoid(uext[:, d:])
        dglu = None
        dw = []
        for j in range(CV_WIDTH):
            term = _shift_rows(dcext, CV_WIDTH - 1 - j)[:n] * wv[j:j + 1, :]
            dglu = term if dglu is None else dglu + term
            dw.append(_colsum(dcc * _shift_rows(glu, CV_HALO - (CV_WIDTH - 1) + j)[:n]))
        u1, sg = uc[:, :d], _sigmoid(uc[:, d:])
        du = jnp.concatenate([dglu * sg, dglu * u1 * sg * (1.0 - sg)], axis=-1)
        return [du], [_stack_rows(dw, CV_HALO), _colsum(du)]

    return _rowwise(fn, n_rows=s, ts=ts, name="cv_mid_bwd2", rows=[(dc, 0, d), (u, 0, 2 * d)],
                    prevs=[(u, 0, 2 * d, CV_HALO)], nexts=[(dc, 0, d, CV_HALO)], vecs=[wdw],
                    row_outs=[(2 * d, BF16)], acc_outs=[(CV_HALO, d), (1, 2 * d)])


def _attn_fwd(q, k, v, name, ts=256):
    s, d = q.shape

    def fn(first, last, qv, kv, vv):
        return [_attn_tile(qv.astype(F32), kv[...].astype(F32), vv[...].astype(F32))], []

    return _rowwise(fn, n_rows=s, ts=ts, name=name, rows=[(q, 0, d)], vecs=[k, v], row_outs=[(d, BF16)])[0]


def _attn_bwd(q, k, v, do, name, ts=256):
    s, d = q.shape
    m = k.shape[0]

    def fn(first, last, qv, dov, kv, vv):
        _, vjp = jax.vjp(_attn_tile, qv.astype(F32), kv[...].astype(F32), vv[...].astype(F32))
        dq, dk, dv = vjp(dov.astype(F32))
        return [dq], [dk, dv]

    return _rowwise(fn, n_rows=s, ts=ts, name=name, rows=[(q, 0, d), (do, 0, d)], vecs=[k, v],
                    row_outs=[(d, BF16)], acc_outs=[(m, d), (m, d)])


def _pad_lanes(a, off=0):
    r, n = a.shape
    return jnp.pad(a, ((0, 0), (off, LANES - off - n)))


def _local_step(x, mem, tgt, w):
    s, d = x.shape
    heads = d // DN_HEAD_DIM
    g = {}

    def add_res(acc, res):
        return (res + acc,)

    w_in = w["dn_w_in"][0]
    assert w_in.shape[1] == 4 * d + 2 * heads
    w_qkvz = w_in[:, :4 * d]
    w_ba = _pad_lanes(w_in[:, 4 * d:])
    dn_norm = w["dn_norm"]
    alog = _pad_lanes(w["dn_a_log"], heads)
    dtb = _pad_lanes(w["dn_dt_bias"], heads)
    wconv = w["dn_w_conv"][0]
    n0 = _rms_fwd(x, dn_norm, "dn_rms")
    qkvz = _matmul(n0, w_qkvz, "nn", [F32], name="dn_in_proj")
    ba = _matmul(n0, w_ba, "nn", [F32], name="dn_in_proj_ba")
    q, k, v, gb = _dn_pre_fwd(qkvz, ba, wconv, alog, dtb, heads)
    o, states, tms = _delta_fwd(q, k, v, gb, heads)
    og = _dn_post_fwd(o, qkvz, w["dn_out_norm"], heads)
    h1 = _matmul(og, w["dn_w_out"][0], "nn", [F32], name="dn_out_proj", epi=add_res, mn_extras=[x])

    def xattn_fwd(h, layer):
        nq = _rms_fwd(h, w["xa_norm"][layer:layer + 1], f"xa{layer}_rms")
        qx = _matmul(nq, w["xa_w_q"][layer], "nn", [BF16], name=f"xa{layer}_q")
        mn = _rms_fwd(mem, w["xa_mem_norm"][layer:layer + 1], f"xa{layer}_mem_rms")
        kv = _matmul(mn, w["xa_w_kv"][layer], "nn", [BF16], name=f"xa{layer}_kv")
        kx, vx = kv[:, :d], kv[:, d:]
        ox = _attn_fwd(qx, kx, vx, f"xa{layer}_attn")
        hn = _matmul(ox, w["xa_w_o"][layer], "nn", [F32], name=f"xa{layer}_o", epi=add_res, mn_extras=[h])
        return hn, (h, nq, qx, mn, kx, vx, ox)

    def mlp_fwd(h, layer):
        nm = _rms_fwd(h, w["mlp_norm"][layer:layer + 1], f"mlp{layer}_rms")

        def epi(acc):
            r = jnp.maximum(acc, 0.0)
            return acc, r * r

        u, a = _matmul(nm, w["mlp_w_up"][layer], "nn", [BF16, BF16], name=f"mlp{layer}_up", epi=epi)
        hn = _matmul(a, w["mlp_w_down"][layer], "nn", [F32], name=f"mlp{layer}_down", epi=add_res,
                     mn_extras=[h], tk=2048)
        return hn, (h, nm, u, a)

    h2, xa0 = xattn_fwd(h1, 0)
    h3, mlp0 = mlp_fwd(h2, 0)

    n1 = _rms_fwd(h3, w["cv_norm"], "cv_rms")
    u_cv = _matmul(n1, w["cv_w_pw1"][0], "nn", [F32], name="cv_pw1", epi=lambda acc, b: (acc + b,),
                   row_extras=[w["cv_b_pw1"]])
    wdw = jnp.pad(w["cv_w_dw"][0], ((0, CV_HALO - CV_WIDTH), (0, 0)))
    c_cv, s_cv = _cv_mid_fwd(u_cv, wdw, w["cv_b_dw"], w["cv_ln_g"], w["cv_ln_b"])
    h4 = _matmul(s_cv, w["cv_w_pw2"][0], "nn", [F32], name="cv_pw2",
                 epi=lambda acc, res, b: (res + acc + b,), mn_extras=[h3], row_extras=[w["cv_b_pw2"]])
    h5, xa1 = xattn_fwd(h4, 1)
    h6, mlp1 = mlp_fwd(h5, 1)

    fnorm = w["final_norm"].reshape(1, d)
    dh, g_fn, loss = _loss_head(h6, tgt, fnorm)
    g["final_norm"] = g_fn.reshape(d)

    def mlp_bwd(dh, layer, saved):
        h, nm, u, a = saved
        du = _matmul(dh, w["mlp_w_down"][layer], "nt", [BF16], name=f"mlp{layer}_down_dx",
                     epi=lambda acc, uu: (acc * 2.0 * jnp.maximum(uu.astype(F32), 0.0),), mn_extras=[u])
        gdown = _matmul(a, dh, "tn", [BF16], name=f"mlp{layer}_down_dw")
        dn = _matmul(du, w["mlp_w_up"][layer], "nt", [F32], name=f"mlp{layer}_up_dx", tk=2048)
        gup = _matmul(nm, du, "tn", [BF16], name=f"mlp{layer}_up_dw", out_dm=True)
        dhn, gn = _rms_bwd(h, dn, dh, w["mlp_norm"][layer:layer + 1], f"mlp{layer}_rms_bwd")
        return dhn, gup, gdown, gn

    def xattn_bwd(dh, layer, saved):
        h, nq, qx, mn, kx, vx, ox = saved
        dox = _matmul(dh, w["xa_w_o"][layer], "nt", [BF16], name=f"xa{layer}_o_dx")
        go = _matmul(ox, dh, "tn", [BF16], name=f"xa{layer}_o_dw")
        dqx, dkx, dvx = _attn_bwd(qx, kx, vx, dox, f"xa{layer}_attn_bwd")
        dn = _matmul(dqx, w["xa_w_q"][layer], "nt", [F32], name=f"xa{layer}_q_dx")
        gq = _matmul(nq, dqx, "tn", [BF16], name=f"xa{layer}_q_dw")
        dkv = jnp.concatenate([dkx, dvx], axis=-1)
        gkv = _matmul(mn, dkv, "tn", [BF16], name=f"xa{layer}_kv_dw", out_dm=True)
        dmn = _matmul(dkv, w["xa_w_kv"][layer], "nt", [F32], name=f"xa{layer}_kv_dx", tk=2048)
        _, gmem = _rms_bwd(mem, dmn, dmn, w["xa_mem_norm"][layer:layer + 1], f"xa{layer}_mem_rms_bwd")
        dhn, gn = _rms_bwd(h, dn, dh, w["xa_norm"][layer:layer + 1], f"xa{layer}_rms_bwd")
        return dhn, gq, gkv, go, gn, gmem

    dh, gup1, gdown1, gmn1 = mlp_bwd(dh, 1, mlp1)
    dh, gq1, gkv1, go1, gxn1, gmem1 = xattn_bwd(dh, 1, xa1)

    ds_cv = _matmul(dh, w["cv_w_pw2"][0], "nt", [BF16], name="cv_pw2_dx")
    g["cv_w_pw2"] = [_matmul(s_cv, dh, "tn", [BF16], name="cv_pw2_dw")]
    dc_cv, g_lng, g_lnb, g_bdw, g_b2 = _cv_mid_bwd1(c_cv, ds_cv, dh, w["cv_ln_g"], w["cv_ln_b"])
    du_cv, g_wdw, g_b1 = _cv_mid_bwd2(dc_cv, u_cv, wdw)
    dn1 = _matmul(du_cv, w["cv_w_pw1"][0], "nt", [F32], name="cv_pw1_dx", tk=2048)
    g["cv_w_pw1"] = [_matmul(n1, du_cv, "tn", [BF16], name="cv_pw1_dw", out_dm=True)]
    dh, g_cvn = _rms_bwd(h3, dn1, dh, w["cv_norm"], "cv_rms_bwd")
    g.update(cv_ln_g=g_lng, cv_ln_b=g_lnb, cv_b_dw=g_bdw, cv_b_pw2=g_b2, cv_b_pw1=g_b1, cv_norm=g_cvn,
             cv_w_dw=g_wdw[:CV_WIDTH][None])

    dh, gup0, gdown0, gmn0 = mlp_bwd(dh, 0, mlp0)
    dh, gq0, gkv0, go0, gxn0, gmem0 = xattn_bwd(dh, 0, xa0)
    g["mlp_w_up"] = [gup0, gup1]
    g["mlp_w_down"] = [gdown0, gdown1]
    g["mlp_norm"] = jnp.concatenate([gmn0, gmn1], axis=0)
    g["xa_w_q"] = [gq0, gq1]
    g["xa_w_kv"] = [gkv0, gkv1]
    g["xa_w_o"] = [go0, go1]
    g["xa_norm"] = jnp.concatenate([gxn0, gxn1], axis=0)
    g["xa_mem_norm"] = jnp.concatenate([gmem0, gmem1], axis=0)

    dog = _matmul(dh, w["dn_w_out"][0], "nt", [BF16], name="dn_out_proj_dx")
    g["dn_w_out"] = [_matmul(og, dh, "tn", [BF16], name="dn_out_proj_dw")]
    do, dz, g_on = _dn_post_bwd(o, qkvz, w["dn_out_norm"], dog, heads)
    dq, dk, dv, dgb = _delta_bwd(q, k, v, gb, states, tms, do, heads)
    dc, dba, g_alog, g_dt = _dn_pre_bwd1(qkvz, ba, wconv, alog, dtb, dq, dk, dv, dgb, heads)
    dqkvz, g_wconv = _dn_pre_bwd2(dc, qkvz, dz, wconv)
    dn0a = _matmul(dba, w_ba, "nt", [F32], name="dn_in_proj_ba_dx")
    dn0 = _matmul(dqkvz, w_qkvz, "nt", [F32], name="dn_in_proj_dx", epi=add_res, mn_extras=[dn0a], tk=2048)
    g_qkvz = _matmul(n0, dqkvz, "tn", [BF16], name="dn_in_proj_dw")
    g_ba = _matmul(n0, dba, "tn", [BF16], name="dn_in_proj_ba_dw")
    g["dn_w_in"] = [jnp.concatenate([g_qkvz, g_ba[:, :2 * heads]], axis=1)]
    grad_x, g_dnn = _rms_bwd(x, dn0, dh, dn_norm, "dn_rms_bwd")
    g.update(dn_norm=g_dnn, dn_out_norm=g_on, dn_w_conv=g_wconv[None],
             dn_a_log=g_alog[:, heads:2 * heads], dn_dt_bias=g_dt[:, heads:2 * heads])
    return loss, grad_x, g


def _round_up(n, m):
    return (n + m - 1) // m * m


def _pack_rows(parts, cols, row_mult):
    lead = parts[0].shape[:-1]
    flat, offs, off = [], [], 0
    for p in parts:
        n = _round_up(p.shape[-1], cols)
        flat.append(jnp.pad(p, [(0, 0)] * len(lead) + [(0, n - p.shape[-1])]))
        offs.append(off)
        off += n
    total = _round_up(off, cols * row_mult)
    if total > off:
        flat.append(jnp.zeros(lead + (total - off,), parts[0].dtype))
    return jnp.concatenate(flat, axis=-1).reshape(lead + (total // cols, cols)), offs


def _unpack(packed, offs, shapes):
    lead = packed.shape[:-2]
    flat = packed.reshape(lead + (-1,))
    out = []
    for off, shp in zip(offs, shapes):
        n = 1
        for v in shp:
            n *= v
        out.append(flat[..., off:off + n].reshape(lead + tuple(shp)))
    return out


def kernel(x, mem, dn_norm, dn_w_in, dn_w_conv, dn_a_log, dn_dt_bias, dn_out_norm, dn_w_out, cv_norm, cv_w_pw1, cv_b_pw1, cv_w_dw, cv_b_dw, cv_ln_g, cv_ln_b, cv_w_pw2, cv_b_pw2, xa_norm, xa_mem_norm, xa_w_q, xa_w_kv, xa_w_o, mlp_norm, mlp_w_up, mlp_w_down, final_norm, loss_target, m_dn_norm, m_dn_w_in, m_dn_w_conv, m_dn_a_log, m_dn_dt_bias, m_dn_out_norm, m_dn_w_out, m_cv_norm, m_cv_w_pw1, m_cv_b_pw1, m_cv_w_dw, m_cv_b_dw, m_cv_ln_g, m_cv_ln_b, m_cv_w_pw2, m_cv_b_pw2, m_xa_norm, m_xa_mem_norm, m_xa_w_q, m_xa_w_kv, m_xa_w_o, m_mlp_norm, m_mlp_w_up, m_mlp_w_down, m_final_norm, v_dn_norm, v_dn_w_in, v_dn_w_conv, v_dn_a_log, v_dn_dt_bias, v_dn_out_norm, v_dn_w_out, v_cv_norm, v_cv_w_pw1, v_cv_b_pw1, v_cv_w_dw, v_cv_b_dw, v_cv_ln_g, v_cv_ln_b, v_cv_w_pw2, v_cv_b_pw2, v_xa_norm, v_xa_mem_norm, v_xa_w_q, v_xa_w_kv, v_xa_w_o, v_mlp_norm, v_mlp_w_up, v_mlp_w_down, v_final_norm):
    wsh = dict(dn_norm=dn_norm, dn_w_in=dn_w_in, dn_w_conv=dn_w_conv, dn_a_log=dn_a_log, dn_dt_bias=dn_dt_bias, dn_out_norm=dn_out_norm, dn_w_out=dn_w_out, cv_norm=cv_norm, cv_w_pw1=cv_w_pw1, cv_b_pw1=cv_b_pw1, cv_w_dw=cv_w_dw, cv_b_dw=cv_b_dw, cv_ln_g=cv_ln_g, cv_ln_b=cv_ln_b, cv_w_pw2=cv_w_pw2, cv_b_pw2=cv_b_pw2, xa_norm=xa_norm, xa_mem_norm=xa_mem_norm, xa_w_q=xa_w_q, xa_w_kv=xa_w_kv, xa_w_o=xa_w_o, mlp_norm=mlp_norm, mlp_w_up=mlp_w_up, mlp_w_down=mlp_w_down, final_norm=final_norm)
    msh = dict(dn_norm=m_dn_norm, dn_w_in=m_dn_w_in, dn_w_conv=m_dn_w_conv, dn_a_log=m_dn_a_log, dn_dt_bias=m_dn_dt_bias, dn_out_norm=m_dn_out_norm, dn_w_out=m_dn_w_out, cv_norm=m_cv_norm, cv_w_pw1=m_cv_w_pw1, cv_b_pw1=m_cv_b_pw1, cv_w_dw=m_cv_w_dw, cv_b_dw=m_cv_b_dw, cv_ln_g=m_cv_ln_g, cv_ln_b=m_cv_ln_b, cv_w_pw2=m_cv_w_pw2, cv_b_pw2=m_cv_b_pw2, xa_norm=m_xa_norm, xa_mem_norm=m_xa_mem_norm, xa_w_q=m_xa_w_q, xa_w_kv=m_xa_w_kv, xa_w_o=m_xa_w_o, mlp_norm=m_mlp_norm, mlp_w_up=m_mlp_w_up, mlp_w_down=m_mlp_w_down, final_norm=m_final_norm)
    vsh = dict(dn_norm=v_dn_norm, dn_w_in=v_dn_w_in, dn_w_conv=v_dn_w_conv, dn_a_log=v_dn_a_log, dn_dt_bias=v_dn_dt_bias, dn_out_norm=v_dn_out_norm, dn_w_out=v_dn_w_out, cv_norm=v_cv_norm, cv_w_pw1=v_cv_w_pw1, cv_b_pw1=v_cv_b_pw1, cv_w_dw=v_cv_w_dw, cv_b_dw=v_cv_b_dw, cv_ln_g=v_cv_ln_g, cv_ln_b=v_cv_ln_b, cv_w_pw2=v_cv_w_pw2, cv_b_pw2=v_cv_b_pw2, xa_norm=v_xa_norm, xa_mem_norm=v_xa_mem_norm, xa_w_q=v_xa_w_q, xa_w_kv=v_xa_w_kv, xa_w_o=v_xa_w_o, mlp_norm=v_mlp_norm, mlp_w_up=v_mlp_w_up, mlp_w_down=v_mlp_w_down, final_norm=v_final_norm)

    big_axis = dict(BIG)

    small_pack, small_offs = _pack_rows([wsh[nm].reshape(-1) for nm in SMALL_SH], LANES, 8)
    w = {nm: [None] * wsh[nm].shape[0] for nm in big_axis}
    for gi, group in enumerate(EXCHANGE_GROUPS):
        shards = [wsh[nm][layer].astype(BF16) for nm, layer in group]
        gathered = _all_gather(shards + ([small_pack] if gi == 0 else []), f"weights_all_gather_{gi}")
        for (nm, layer), gth in zip(group, gathered):
            if nm == "dn_w_in":
                w[nm][layer] = gth.transpose(1, 0, 2).reshape(gth.shape[1], N_DEV * gth.shape[2])
            elif big_axis[nm] == 1:
                w[nm][layer] = gth.reshape(N_DEV * gth.shape[1], gth.shape[2])
            else:
                w[nm][layer] = gth
        if gi == 0:
            small_g = gathered[-1]
    for nm, gth in zip(SMALL_SH, _unpack(small_g, small_offs, [wsh[nm].shape for nm in SMALL_SH])):
        w[nm] = jnp.moveaxis(gth, 0, -2).reshape(gth.shape[1:-1] + (N_DEV * gth.shape[-1],))
    for nm in REPL:
        w[nm] = wsh[nm]

    loss_part, grad_x, g = _local_step(x[0], mem[0], loss_target[0], w)

    gsmall_pack, _ = _pack_rows(
        [jnp.moveaxis(g[nm].reshape(g[nm].shape[:-1] + (N_DEV, -1)), -2, 0).reshape(N_DEV, -1) for nm in SMALL_SH],
        LANES, 8)
    recv = {nm: [None] * wsh[nm].shape[0] for nm in big_axis}
    for gi, group in reversed(list(enumerate(EXCHANGE_GROUPS))):
        blocks = []
        for nm, layer in group:
            gw = g[nm][layer]
            if nm == "dn_w_in":
                gw = gw.reshape(gw.shape[0], N_DEV, -1).transpose(1, 0, 2)
            elif big_axis[nm] == 1:
                gw = gw.reshape(N_DEV, gw.shape[0] // N_DEV, gw.shape[1])
            blocks.append(gw)
        landed = _all_to_all(blocks + ([gsmall_pack] if gi == 0 else []), f"grads_all_to_all_{gi}")
        for (nm, layer), r in zip(group, landed):
            recv[nm][layer] = r
        if gi == 0:
            rsmall = landed[-1]
    gsmall_red = _slot_sum(rsmall, "grads_small_sum", 512)
    repl_pack, repl_offs = _pack_rows([g[nm].reshape(-1) for nm in REPL], LANES, 8)
    (repl_all,) = _all_gather([repl_pack], "repl_grads_all_gather")
    repl_red = _slot_sum(repl_all, "repl_grads_sum", 512)

    gsh = {}
    for nm, val in zip(SMALL_SH, _unpack(gsmall_red, small_offs, [wsh[nm].shape for nm in SMALL_SH])):
        gsh[nm] = val
    for nm, val in zip(REPL, _unpack(repl_red, repl_offs, [wsh[nm].shape for nm in REPL])):
        gsh[nm] = val

    delta, new_m, new_v = {}, {}, {}
    for nm in big_axis:
        gsh[nm], delta[nm], new_m[nm], new_v[nm] = _adamw_reduce(recv[nm], wsh[nm], msh[nm], vsh[nm], f"adamw_{nm}")
    small_names = list(SMALL_SH) + list(REPL)
    packs = []
    for src in (wsh, gsh, msh, vsh):
        pk, sm_offs = _pack_rows([src[nm].reshape(-1) for nm in small_names], LANES, 8)
        packs.append(pk)
    outs = _adamw(*packs, "adamw_small")
    for dst, pk in zip((delta, new_m, new_v), outs):
        for nm, val in zip(small_names, _unpack(pk, sm_offs, [wsh[nm].shape for nm in small_names])):
            dst[nm] = val

    loss = lax.psum(loss_part[0, 0], ("x", "y", "c"))
    return (loss, grad_x[None], *[gsh[nm] for nm in WEIGHTS], *[delta[nm] for nm in WEIGHTS],
            *[new_m[nm] for nm in WEIGHTS], *[new_v[nm] for nm in WEIGHTS])
```

```python
import functools

import jax
import jax.numpy as jnp
from jax import lax
from jax.experimental import pallas as pl
from jax.experimental.pallas import tpu as pltpu

F32 = jnp.float32
BF16 = jnp.bfloat16
HP = lax.Precision.HIGHEST
MESH_IDS = pl.DeviceIdType.MESH

N_DEV = 8
LANES = 128
RMS_EPS = 1e-6
LN_EPS = 1e-5
DN_HEAD_DIM = 128
DN_CONV = 4
DN_CHUNK = 64
CV_WIDTH = 31
XA_HEADS = 4
DN_HALO = 8
CV_HALO = 32

ADAM_LR = 0.001
ADAM_B1 = 0.9
ADAM_B2 = 0.999
ADAM_EPS = 1e-08
ADAM_WD = 0.01
ADAM_STEP = 10

BIG = (("dn_w_in", 2), ("dn_w_out", 1), ("cv_w_pw1", 2), ("cv_w_pw2", 1), ("xa_w_q", 1), ("xa_w_kv", 2),
       ("xa_w_o", 1), ("mlp_w_up", 2), ("mlp_w_down", 1))
_LAYER_GROUP = ("xa_w_q", "xa_w_o", "mlp_w_down", "xa_w_kv", "mlp_w_up")
GATHER_GROUPS = (
    (("dn_w_in", 0),),
    (("dn_w_out", 0),) + tuple((nm, 0) for nm in _LAYER_GROUP),
    (("cv_w_pw2", 0), ("cv_w_pw1", 0)),
    tuple((nm, 1) for nm in _LAYER_GROUP),
)
SCATTER_GROUPS = (
    (("dn_w_out", 0), ("dn_w_in", 0)),
    tuple((nm, 0) for nm in _LAYER_GROUP),
    (("cv_w_pw2", 0), ("cv_w_pw1", 0)),
    tuple((nm, 1) for nm in _LAYER_GROUP),
)
SMALL_SH = ("cv_norm", "cv_b_pw1", "cv_b_dw", "cv_ln_g", "cv_ln_b", "cv_b_pw2", "cv_w_dw", "dn_w_conv")
REPL = ("dn_norm", "dn_a_log", "dn_dt_bias", "dn_out_norm", "xa_norm", "xa_mem_norm", "mlp_norm", "final_norm")
WEIGHTS = ("dn_norm", "dn_w_in", "dn_w_conv", "dn_a_log", "dn_dt_bias", "dn_out_norm", "dn_w_out", "cv_norm",
           "cv_w_pw1", "cv_b_pw1", "cv_w_dw", "cv_b_dw", "cv_ln_g", "cv_ln_b", "cv_w_pw2", "cv_b_pw2", "xa_norm",
           "xa_mem_norm", "xa_w_q", "xa_w_kv", "xa_w_o", "mlp_norm", "mlp_w_up", "mlp_w_down", "final_norm")


def _dot_dims(mode, batched):
    o = 1 if batched else 0
    contract = {"nn": ((1 + o,), (o,)), "nt": ((1 + o,), (1 + o,)), "tn": ((o,), (o,))}[mode]
    return (contract, (((0,), (0,)) if batched else ((), ())))


def _bdot(a, b, mode):
    return lax.dot_general(a.astype(BF16), b.astype(BF16), _dot_dims(mode, a.ndim == 3),
                           preferred_element_type=F32)


@functools.partial(jax.custom_vjp, nondiff_argnums=(2,))
def _mm(a, b, mode):
    return _bdot(a, b, mode)


def _mm_fwd(a, b, mode):
    return _bdot(a, b, mode), (a, b)


def _mm_bwd(mode, res, ct):
    a, b = res
    if mode == "nn":
        da, db = _bdot(ct, b, "nt"), _bdot(a, ct, "tn")
    elif mode == "nt":
        da, db = _bdot(ct, b, "nn"), _bdot(ct, a, "tn")
    else:
        da, db = _bdot(b, ct, "nt"), _bdot(a, ct, "nn")
    return da.astype(a.dtype), db.astype(b.dtype)


_mm.defvjp(_mm_fwd, _mm_bwd)


def _sigmoid(x):
    return 0.5 * (jnp.tanh(0.5 * x) + 1.0)


def _silu(x):
    return x * _sigmoid(x)


def _softplus(x):
    return jnp.maximum(x, 0.0) + jnp.log(1.0 + jnp.exp(-jnp.abs(x)))


def _rms(x, g):
    r = lax.rsqrt(jnp.mean(x * x, axis=-1, keepdims=True) + RMS_EPS)
    return x * r * g


def _shift_rows(x, off):
    if off == 0:
        return x
    return pltpu.roll(x, x.shape[0] - off, 0)


def _series_dot(a, b, mode):
    return _bdot(a, b, mode)


def _chunk_masks(c):
    ii = lax.broadcasted_iota(jnp.int32, (c, c), 0)
    jj = lax.broadcasted_iota(jnp.int32, (c, c), 1)
    return (ii == jj).astype(F32), ii >= jj, ii > jj


def _neumann_inverse(lm):
    n = lm.shape[-1]
    t = -lm
    p = lm
    size = 2
    while size < n:
        size *= 2
        p = _series_dot(p, p, "nn")
        t = t + p + _series_dot(t, p, "nn")
    return t


def _apply_inverse(tm, rhs, mode):
    return rhs + _series_dot(tm, rhs, mode)


@jax.custom_vjp
def _unit_lower_solve(lm, rhs, tm):
    return _apply_inverse(tm, rhs, "nn")


def _uls_fwd(lm, rhs, tm):
    sol = _apply_inverse(tm, rhs, "nn")
    return sol, (tm, sol)


def _uls_bwd(res, ct):
    tm, sol = res
    d_rhs = _apply_inverse(tm, ct, "tn")
    return -_bdot(d_rhs, sol, "nt"), d_rhs, jnp.zeros_like(tm)


_unit_lower_solve.defvjp(_uls_fwd, _uls_bwd)


def _delta_chunk(q, k, v, gcol, bcol, s0, tm=None):
    c = q.shape[1]
    eye, causal, strict = _chunk_masks(c)
    grow = jnp.sum(eye * gcol, axis=1, keepdims=True)
    gc = jnp.sum(jnp.where(causal, grow, 0.0), axis=2, keepdims=True)
    gc_row = jnp.sum(eye * gc, axis=1, keepdims=True)
    decay = jnp.exp(jnp.where(causal, gc - gc_row, -jnp.inf))
    kb = k * bcol
    lm = jnp.where(strict, _mm(kb, k, "nt") * decay, 0.0)
    if tm is None:
        tm = _neumann_inverse(lax.stop_gradient(lm))
    egc = jnp.exp(gc)
    rhs = jnp.concatenate([v * bcol, kb * egc], axis=-1)
    sol = _unit_lower_solve(lm, rhs, tm)
    dv_ = v.shape[-1]
    u, w = sol[..., :dv_], sol[..., dv_:]
    attn = _mm(q, k, "nt") * decay
    qd = q * egc
    gl = jnp.sum(grow, axis=2, keepdims=True)
    kd = k * jnp.exp(gl - gc)
    v_new = u - _mm(w, s0, "nn")
    o = _mm(qd, s0, "nn") + _mm(attn, v_new, "nn")
    s1 = s0 * jnp.exp(gl) + _mm(kd, v_new, "tn")
    return o, s1, tm


def _dn_point(cv, ba, alog, dt, heads):
    a = _silu(cv)
    d = cv.shape[1] // 3
    qs, ks = [], []
    for h in range(heads):
        qh = a[:, h * DN_HEAD_DIM:(h + 1) * DN_HEAD_DIM]
        qs.append(qh * lax.rsqrt(jnp.sum(qh * qh, axis=-1, keepdims=True) + 1e-6) * (DN_HEAD_DIM ** -0.5))
        kh = a[:, d + h * DN_HEAD_DIM:d + (h + 1) * DN_HEAD_DIM]
        ks.append(kh * lax.rsqrt(jnp.sum(kh * kh, axis=-1, keepdims=True) + 1e-6))
    q = jnp.concatenate(qs, axis=-1)
    k = jnp.concatenate(ks, axis=-1)
    v = a[:, 2 * d:]
    lane = lax.broadcasted_iota(jnp.int32, ba.shape, 1)
    beta = _sigmoid(ba)
    g = -jnp.exp(alog) * _softplus(ba + dt)
    gb = jnp.where(lane < heads, beta, jnp.where(lane < 2 * heads, g, 0.0))
    return q, k, v, gb


def _dn_post(o, z, onorm, heads):
    outs = []
    for h in range(heads):
        oh = o[:, h * DN_HEAD_DIM:(h + 1) * DN_HEAD_DIM]
        outs.append(oh * lax.rsqrt(jnp.mean(oh * oh, axis=-1, keepdims=True) + RMS_EPS) * onorm)
    return jnp.concatenate(outs, axis=-1) * _silu(z)


def _attn_tile(q, k, v):
    hd = q.shape[1] // XA_HEADS
    outs = []
    for h in range(XA_HEADS):
        sl = slice(h * hd, (h + 1) * hd)
        s = _mm(q[:, sl], k[:, sl], "nt") * (hd ** -0.5)
        m = lax.stop_gradient(jnp.max(s, axis=-1, keepdims=True))
        e = jnp.exp(s - m)
        p = e / jnp.sum(e, axis=-1, keepdims=True)
        outs.append(_mm(p, v[:, sl], "nn"))
    return jnp.concatenate(outs, axis=-1)


def _ln_silu(c, g, b):
    mu = jnp.mean(c, axis=-1, keepdims=True)
    xc = c - mu
    y = xc * lax.rsqrt(jnp.mean(xc * xc, axis=-1, keepdims=True) + LN_EPS)
    return _silu(y * g + b)


def _causal_conv(xext, w, width, lead, ts):
    acc = None
    for j in range(width):
        term = _shift_rows(xext, lead + j)[:ts] * w[j:j + 1, :]
        acc = term if acc is None else acc + term
    return acc


def _colsum(x):
    return jnp.sum(x, axis=0, keepdims=True)


def _stack_rows(rows, n_rows):
    c = rows[0].shape[1]
    ridx = lax.broadcasted_iota(jnp.int32, (n_rows, c), 0)
    out = jnp.zeros((n_rows, c), F32)
    for j, r in enumerate(rows):
        out = out + jnp.where(ridx == j, r, 0.0)
    return out


def _matmul(a, b, mode, out_dtypes, *, name, epi=None, mn_extras=(), row_extras=(), out_dm=False, after=(),
            tm=512, tn=1024, tk=1024):
    b_dm = b.ndim == 3
    b_shape = (b.shape[1], N_DEV * b.shape[2]) if b_dm else b.shape
    if mode == "nn":
        (m, k), (k2, n) = a.shape, b_shape
    elif mode == "nt":
        (m, k), (n, k2) = a.shape, b_shape
    else:
        (k, m), (k2, n) = a.shape, b_shape
    assert k == k2, (a.shape, b.shape, mode)
    tm, tn, tk = min(tm, m), min(tn, n), min(tk, k)
    if b_dm:
        assert mode in ("nn", "nt")
        if mode == "nn":
            tn = b.shape[2]
        else:
            tk = b.shape[2]
    if out_dm:
        tn = n // N_DEV
    assert m % tm == 0 and n % tn == 0 and k % tk == 0, (m, n, k, tm, tn, tk)
    nk = k // tk
    if mode == "tn":
        a_spec = pl.BlockSpec((tk, tm), lambda j, i, kk: (kk, i))
    else:
        a_spec = pl.BlockSpec((tm, tk), lambda j, i, kk: (i, kk))
    if b_dm:
        b_spec = (pl.BlockSpec((None, tn, tk), lambda j, i, kk: (kk, j, 0)) if mode == "nt"
                  else pl.BlockSpec((None, tk, tn), lambda j, i, kk: (j, kk, 0)))
    else:
        b_spec = (pl.BlockSpec((tn, tk), lambda j, i, kk: (j, kk)) if mode == "nt"
                  else pl.BlockSpec((tk, tn), lambda j, i, kk: (kk, j)))
    mn_spec = pl.BlockSpec((tm, tn), lambda j, i, kk: (i, j))
    row_spec = pl.BlockSpec((1, tn), lambda j, i, kk: (0, j))
    n_extra = len(mn_extras) + len(row_extras)
    n_out = len(out_dtypes)
    in_specs = ([a_spec, b_spec] + [mn_spec] * len(mn_extras) + [row_spec] * len(row_extras)
                + [_ANY_SPEC] * len(after))
    args = [a, b, *mn_extras, *row_extras, *after]
    if out_dm:
        out_specs = [pl.BlockSpec((None, tm, tn), lambda j, i, kk: (j, i, 0))] * n_out
        out_shape = [jax.ShapeDtypeStruct((N_DEV, m, tn), dt) for dt in out_dtypes]
    else:
        out_specs = [mn_spec] * n_out
        out_shape = [jax.ShapeDtypeStruct((m, n), dt) for dt in out_dtypes]
    n_in = len(args)

    def body(*refs):
        a_ref, b_ref = refs[0], refs[1]
        extras = refs[2:2 + n_extra]
        outs = refs[n_in:n_in + n_out]
        acc = refs[-1]
        kk = pl.program_id(2)

        @pl.when(kk == 0)
        def _():
            acc[...] = jnp.zeros_like(acc)

        acc[...] += _bdot(a_ref[...], b_ref[...], mode)

        @pl.when(kk == nk - 1)
        def _():
            vals = (acc[...],) if epi is None else epi(acc[...], *[e[...] for e in extras])
            for o_ref, val in zip(outs, vals):
                o_ref[...] = val.astype(o_ref.dtype)

    res = pl.pallas_call(
        body, name=name,
        grid=(n // tn, m // tm, nk),
        in_specs=in_specs, out_specs=out_specs, out_shape=out_shape,
        scratch_shapes=[pltpu.VMEM((tm, tn), F32)],
        compiler_params=pltpu.CompilerParams(dimension_semantics=("parallel", "parallel", "arbitrary")),
    )(*args)
    return res[0] if n_out == 1 else res


def _rowwise(fn, *, n_rows, ts, name, rows=(), prevs=(), nexts=(), vecs=(), row_outs=(), acc_outs=(), after=()):
    ts = min(ts, n_rows)
    assert n_rows % ts == 0
    nblk = n_rows // ts
    in_specs, args = [], []
    for arr, cb, w in rows:
        in_specs.append(pl.BlockSpec((ts, w), functools.partial(lambda i, cb: (i, cb), cb=cb)))
        args.append(arr)
    for arr, cb, w, halo in prevs:
        per = ts // halo
        in_specs.append(pl.BlockSpec(
            (halo, w), functools.partial(lambda i, cb, per: (jnp.maximum(i * per - 1, 0), cb), cb=cb, per=per)))
        args.append(arr)
    for arr, cb, w, halo in nexts:
        per = ts // halo
        last_blk = n_rows // halo - 1
        in_specs.append(pl.BlockSpec(
            (halo, w), functools.partial(lambda i, cb, per, lb: (jnp.minimum((i + 1) * per, lb), cb),
                                         cb=cb, per=per, lb=last_blk)))
        args.append(arr)
    for arr in vecs:
        in_specs.append(pl.BlockSpec(arr.shape, functools.partial(lambda i, nd: (0,) * nd, nd=arr.ndim)))
        args.append(arr)
    out_specs, out_shape = [], []
    for w, dt in row_outs:
        out_specs.append(pl.BlockSpec((ts, w), lambda i: (i, 0)))
        out_shape.append(jax.ShapeDtypeStruct((n_rows, w), dt))
    for shp in acc_outs:
        out_specs.append(pl.BlockSpec(shp, functools.partial(lambda i, nd: (0,) * nd, nd=len(shp))))
        out_shape.append(jax.ShapeDtypeStruct(shp, F32))
    n_used = len(args)
    n_tiles = n_used - len(vecs)
    in_specs += [_ANY_SPEC] * len(after)
    args += list(after)
    n_in, n_ro, n_acc = len(args), len(row_outs), len(acc_outs)

    def body(*refs):
        ins, ro, ac = refs[:n_used], refs[n_in:n_in + n_ro], refs[n_in + n_ro:]
        i = pl.program_id(0)
        rvals, avals = fn(i == 0, i == nblk - 1, *[r[...] for r in ins[:n_tiles]], *ins[n_tiles:])
        for r, val in zip(ro, rvals):
            r[...] = val.astype(r.dtype)
        if n_acc:
            @pl.when(i == 0)
            def _():
                for r in ac:
                    r[...] = jnp.zeros_like(r)

            for r, val in zip(ac, avals):
                r[...] += val

    res = pl.pallas_call(
        body, name=name, grid=(nblk,), in_specs=in_specs, out_specs=out_specs, out_shape=out_shape,
        compiler_params=pltpu.CompilerParams(dimension_semantics=("arbitrary",)),
    )(*args)
    return res


def _delta_fwd(q, k, v, gb, heads):
    s, hd = q.shape
    n = s // DN_CHUNK
    blk = pl.BlockSpec((DN_CHUNK, hd), lambda c: (c, 0))
    gspec = pl.BlockSpec((DN_CHUNK, LANES), lambda c: (c, 0))

    def body(q_ref, k_ref, v_ref, gb_ref, o_ref, st_ref, tm_ref, state):
        @pl.when(pl.program_id(0) == 0)
        def _():
            state[...] = jnp.zeros_like(state)

        s0 = state[...]
        st_ref[0] = s0
        o, s1, tm = _delta_chunk(*_split_heads(q_ref, k_ref, v_ref, gb_ref[...], heads), s0)
        for h in range(heads):
            o_ref[:, h * DN_HEAD_DIM:(h + 1) * DN_HEAD_DIM] = o[h]
        state[...] = s1
        tm_ref[0] = tm

    return pl.pallas_call(
        body, name="dn_delta_fwd", grid=(n,),
        in_specs=[blk, blk, blk, gspec],
        out_specs=[blk, pl.BlockSpec((1, heads, DN_HEAD_DIM, DN_HEAD_DIM), lambda c: (c, 0, 0, 0)),
                   pl.BlockSpec((1, heads, DN_CHUNK, DN_CHUNK), lambda c: (c, 0, 0, 0))],
        out_shape=[jax.ShapeDtypeStruct((s, hd), F32),
                   jax.ShapeDtypeStruct((n, heads, DN_HEAD_DIM, DN_HEAD_DIM), F32),
                   jax.ShapeDtypeStruct((n, heads, DN_CHUNK, DN_CHUNK), F32)],
        scratch_shapes=[pltpu.VMEM((heads, DN_HEAD_DIM, DN_HEAD_DIM), F32)],
        compiler_params=pltpu.CompilerParams(dimension_semantics=("arbitrary",)),
    )(q, k, v, gb)


def _split_heads(q_ref, k_ref, v_ref, gbv, heads):
    def hs(ref):
        return jnp.stack([ref[:, h * DN_HEAD_DIM:(h + 1) * DN_HEAD_DIM] for h in range(heads)])

    gcol = jnp.stack([gbv[:, heads + h:heads + h + 1] for h in range(heads)])
    bcol = jnp.stack([gbv[:, h:h + 1] for h in range(heads)])
    return hs(q_ref), hs(k_ref), hs(v_ref), gcol, bcol


def _delta_bwd(q, k, v, gb, states, tms, do, heads):
    s, hd = q.shape
    n = s // DN_CHUNK
    blk = pl.BlockSpec((DN_CHUNK, hd), lambda c: (n - 1 - c, 0))
    gspec = pl.BlockSpec((DN_CHUNK, LANES), lambda c: (n - 1 - c, 0))
    sspec = pl.BlockSpec((1, heads, DN_HEAD_DIM, DN_HEAD_DIM), lambda c: (n - 1 - c, 0, 0, 0))
    tspec = pl.BlockSpec((1, heads, DN_CHUNK, DN_CHUNK), lambda c: (n - 1 - c, 0, 0, 0))

    def body(q_ref, k_ref, v_ref, gb_ref, st_ref, tm_ref, do_ref, dq_ref, dk_ref, dv_ref, dgb_ref, dstate):
        @pl.when(pl.program_id(0) == 0)
        def _():
            dstate[...] = jnp.zeros_like(dstate)

        gbv = gb_ref[...]
        tm = tm_ref[0]

        def chunk(qh, kh, vh, gcol, bcol, s0):
            return _delta_chunk(qh, kh, vh, gcol, bcol, s0, tm)[:2]

        _, vjp = jax.vjp(chunk, *_split_heads(q_ref, k_ref, v_ref, gbv, heads), st_ref[0])
        doh = jnp.stack([do_ref[:, h * DN_HEAD_DIM:(h + 1) * DN_HEAD_DIM] for h in range(heads)])
        dq, dk, dv, dg, db, ds0 = vjp((doh, dstate[...]))
        dstate[...] = ds0
        lane = lax.broadcasted_iota(jnp.int32, gbv.shape, 1)
        dgb = jnp.zeros(gbv.shape, F32)
        for h in range(heads):
            sl = slice(h * DN_HEAD_DIM, (h + 1) * DN_HEAD_DIM)
            dq_ref[:, sl] = dq[h]
            dk_ref[:, sl] = dk[h]
            dv_ref[:, sl] = dv[h]
            dgb = dgb + jnp.where(lane == h, db[h], 0.0) + jnp.where(lane == heads + h, dg[h], 0.0)
        dgb_ref[...] = dgb

    return pl.pallas_call(
        body, name="dn_delta_bwd", grid=(n,),
        in_specs=[blk, blk, blk, gspec, sspec, tspec, blk],
        out_specs=[blk, blk, blk, gspec],
        out_shape=[jax.ShapeDtypeStruct((s, hd), F32)] * 3 + [jax.ShapeDtypeStruct((s, LANES), F32)],
        scratch_shapes=[pltpu.VMEM((heads, DN_HEAD_DIM, DN_HEAD_DIM), F32)],
        compiler_params=pltpu.CompilerParams(dimension_semantics=("arbitrary",)),
    )(q, k, v, gb, states, tms, do)


def _dev_index(px, py, pc):
    return 4 * px + 2 * py + pc


def _all_gather(arrs, name):
    n = len(arrs)

    def body(*refs):
        xs, outs = refs[:n], refs[n:2 * n]
        send_sems, recv_sems, local_sems = refs[2 * n:]
        x, y, c = lax.axis_index("x"), lax.axis_index("y"), lax.axis_index("c")
        me, sibling = (x, y, c), (x, y, 1 - c)
        chips = [(1 - x, y), (x, 1 - y), (1 - x, 1 - y)]

        def copy(a, kk, block, to, src=None):
            dst = outs[a].at[_dev_index(*block)]
            return pltpu.make_async_remote_copy(
                src_ref=dst if src is None else src, dst_ref=dst,
                send_sem=send_sems.at[a * 7 + kk], recv_sem=recv_sems.at[a * 7 + kk],
                device_id=to, device_id_type=MESH_IDS)

        mine = [pltpu.make_async_copy(xs[a], outs[a].at[_dev_index(*me)], local_sems.at[a]) for a in range(n)]
        for cp in mine:
            cp.start()
        first = []
        for a in range(n):
            first.append(copy(a, 0, me, sibling, src=xs[a]))
            first += [copy(a, 1 + j, me, (*chip, c), src=xs[a]) for j, chip in enumerate(chips)]
        for cp in first:
            cp.start()
        passed = []
        for j, chip in enumerate(chips):
            for a in range(n):
                copy(a, 1 + j, (*chip, c), me).wait_recv()
                fwd = copy(a, 4 + j, (*chip, c), sibling)
                fwd.start()
                passed.append(fwd)
        for a in range(n):
            copy(a, 0, sibling, me).wait_recv()
        for j, chip in enumerate(chips):
            for a in range(n):
                copy(a, 4 + j, (*chip, 1 - c), me).wait_recv()
        for cp in first + passed:
            cp.wait_send()
        for cp in mine:
            cp.wait()

    hbm = pl.BlockSpec(memory_space=pltpu.HBM)
    res = pl.pallas_call(
        body, name=name,
        in_specs=[hbm] * n, out_specs=[hbm] * n,
        out_shape=[jax.ShapeDtypeStruct((N_DEV,) + a.shape, a.dtype) for a in arrs],
        scratch_shapes=[pltpu.SemaphoreType.DMA((7 * n,)), pltpu.SemaphoreType.DMA((7 * n,)),
                        pltpu.SemaphoreType.DMA((n,))],
    )(*arrs)
    return list(res)


_FLIPS = ((0, 0, 1), (1, 0, 0), (0, 1, 0), (1, 1, 0), (1, 0, 1), (0, 1, 1), (1, 1, 1))
_HBM_SPEC = pl.BlockSpec(memory_space=pltpu.HBM)
_SEM_SPEC = pl.BlockSpec(memory_space=pltpu.SEMAPHORE)
_ANY_SPEC = pl.BlockSpec(memory_space=pl.ANY)
_DATAFLOW = pltpu.SideEffectType.DATAFLOW_SIDE_EFFECTING
TOKEN_SHAPE = (8, LANES)


def _mesh_me():
    return lax.axis_index("x"), lax.axis_index("y"), lax.axis_index("c")


def _flipped(me, f):
    return tuple(1 - v if fl else v for v, fl in zip(me, f))


def _place_own(srcs, scatter, name):
    n = len(srcs)

    def body(*refs):
        xs, outs, sems = refs[:n], refs[n:2 * n], refs[2 * n]
        idx = _dev_index(*_mesh_me())
        cps = [pltpu.make_async_copy(xs[a].at[idx] if scatter else xs[a], outs[a].at[idx], sems.at[a])
               for a in range(n)]
        for cp in cps:
            cp.start()
        for cp in cps:
            cp.wait()

    res = pl.pallas_call(
        body, name=name, in_specs=[_HBM_SPEC] * n, out_specs=[_HBM_SPEC] * n,
        out_shape=[jax.ShapeDtypeStruct(a.shape if scatter else (N_DEV,) + a.shape, a.dtype) for a in srcs],
        scratch_shapes=[pltpu.SemaphoreType.DMA((n,))],
    )(*srcs)
    return list(res)


def _exchange_copies(xs, lands, send_sems, recv_sems, scatter, landed):
    me = _mesh_me()
    cps = []
    for kk, f in enumerate(_FLIPS):
        p = _flipped(me, f)
        for a in range(len(xs)):
            cps.append(pltpu.make_async_remote_copy(
                src_ref=xs[a].at[_dev_index(*p)] if scatter else xs[a],
                dst_ref=lands[a].at[_dev_index(*(p if landed else me))],
                send_sem=send_sems.at[a * 7 + kk], recv_sem=recv_sems.at[a * 7 + kk],
                device_id=p, device_id_type=MESH_IDS))
    return cps


def _exchange_start(srcs, scatter, name, after=()):
    n = len(srcs)
    lands = _place_own(srcs, scatter, name + "_own")

    n_after = len(after)

    def body(*refs):
        xs, ls = refs[:n], refs[n:2 * n]
        send_sems, recv_sems = refs[2 * n + n_after], refs[2 * n + n_after + 1]
        token = refs[-1]
        for cp in _exchange_copies(xs, ls, send_sems, recv_sems, scatter, landed=False):
            cp.start()
        token[...] = jnp.zeros_like(token)

    operands = [pltpu.with_memory_space_constraint(a, pltpu.HBM) for a in list(srcs) + lands]
    res = pl.pallas_call(
        body, name=name,
        in_specs=[_HBM_SPEC] * (2 * n) + [_ANY_SPEC] * len(after),
        out_specs=[_SEM_SPEC, _SEM_SPEC] + [_HBM_SPEC] * (2 * n) + [pl.BlockSpec(memory_space=pltpu.VMEM)],
        out_shape=[pltpu.SemaphoreType.DMA((7 * n,)), pltpu.SemaphoreType.DMA((7 * n,))]
        + [pltpu.HBM(a.shape, a.dtype) for a in operands] + [jax.ShapeDtypeStruct(TOKEN_SHAPE, F32)],
        input_output_aliases={i: 2 + i for i in range(2 * n)},
        compiler_params=pltpu.CompilerParams(has_side_effects=_DATAFLOW),
    )(*operands, *after)
    return (res[0], res[1], list(res[2:2 + n]), list(res[2 + n:2 + 2 * n]), scatter, name), res[-1]


def _exchange_wait(handle, after):
    send_sems, recv_sems, srcs, lands, scatter, name = handle
    n = len(srcs)
    n_after = len(after)

    def body(*refs):
        xs, ls = refs[:n], refs[n:2 * n]
        send_sems_ref, recv_sems_ref = refs[2 * n], refs[2 * n + 1]
        for cp in _exchange_copies(xs, ls, send_sems_ref, recv_sems_ref, scatter, landed=True):
            cp.wait_send()
            cp.wait_recv()

    res = pl.pallas_call(
        body, name=name + "_wait",
        in_specs=[_HBM_SPEC] * (2 * n) + [_SEM_SPEC, _SEM_SPEC] + [_ANY_SPEC] * n_after,
        out_specs=[_HBM_SPEC] * (2 * n),
        out_shape=[pltpu.HBM(a.shape, a.dtype) for a in srcs + lands],
        input_output_aliases={i: i for i in range(2 * n)},
        compiler_params=pltpu.CompilerParams(has_side_effects=_DATAFLOW),
    )(*srcs, *lands, send_sems, recv_sems, *after)
    return list(res[n:])


def _slot_sum(g, name, tr):
    _, r, c = g.shape
    tr = min(tr, r)
    assert r % tr == 0

    def body(g_ref, o_ref):
        acc = g_ref[0].astype(F32)
        for s in range(1, N_DEV):
            acc = acc + g_ref[s].astype(F32)
        o_ref[...] = acc

    return pl.pallas_call(
        body, name=name, grid=(r // tr,),
        in_specs=[pl.BlockSpec((N_DEV, tr, c), lambda i: (0, i, 0))],
        out_specs=pl.BlockSpec((tr, c), lambda i: (i, 0)),
        out_shape=jax.ShapeDtypeStruct((r, c), F32),
        compiler_params=pltpu.CompilerParams(dimension_semantics=("parallel",)),
    )(g)


def _adam_update(w, gg, m, v):
    c1 = 1.0 / (1.0 - ADAM_B1 ** ADAM_STEP)
    c2 = 1.0 / (1.0 - ADAM_B2 ** ADAM_STEP)
    nm = ADAM_B1 * m + (1.0 - ADAM_B1) * gg
    nv = ADAM_B2 * v + (1.0 - ADAM_B2) * (gg * gg)
    return -ADAM_LR * ((nm * c1) / (jnp.sqrt(nv * c2) + ADAM_EPS) + ADAM_WD * w), nm, nv


def _adamw_reduce(recvs, w, m, v, name, tr=128):
    nl, r, c = w.shape
    assert len(recvs) == nl
    tr = min(tr, r)
    assert r % tr == 0
    nblk = r // tr

    def recv_spec(li):
        def index(l, i):
            return (0, jnp.where(l < li, 0, jnp.where(l > li, nblk - 1, i)), 0)
        return pl.BlockSpec((N_DEV, tr, c), index)

    def body(*refs):
        rrefs = refs[:nl]
        w_ref, m_ref, v_ref, g_ref, d_ref, nm_ref, nv_ref = refs[nl:]
        l = pl.program_id(0)
        gg = None
        for s in range(N_DEV):
            slot = rrefs[0][s].astype(F32)
            for li in range(1, nl):
                slot = jnp.where(l == li, rrefs[li][s].astype(F32), slot)
            gg = slot if gg is None else gg + slot
        g_ref[...] = gg
        d_ref[...], nm_ref[...], nv_ref[...] = _adam_update(w_ref[...], gg, m_ref[...], v_ref[...])

    spec = pl.BlockSpec((None, tr, c), lambda l, i: (l, i, 0))
    return pl.pallas_call(
        body, name=name, grid=(nl, nblk),
        in_specs=[recv_spec(li) for li in range(nl)] + [spec] * 3, out_specs=[spec] * 4,
        out_shape=[jax.ShapeDtypeStruct((nl, r, c), F32)] * 4,
        compiler_params=pltpu.CompilerParams(dimension_semantics=("arbitrary", "arbitrary")),
    )(*recvs, w, m, v)


def _adamw(w, g, m, v, name, tr=256):
    r, c = w.shape
    tr = min(tr, r)
    assert r % tr == 0

    def body(w_ref, g_ref, m_ref, v_ref, d_ref, nm_ref, nv_ref):
        d_ref[...], nm_ref[...], nv_ref[...] = _adam_update(w_ref[...], g_ref[...], m_ref[...], v_ref[...])

    spec = pl.BlockSpec((tr, c), lambda i: (i, 0))
    return pl.pallas_call(
        body, name=name, grid=(r // tr,), in_specs=[spec] * 4, out_specs=[spec] * 3,
        out_shape=[jax.ShapeDtypeStruct((r, c), F32)] * 3,
        compiler_params=pltpu.CompilerParams(dimension_semantics=("parallel",)),
    )(w, g, m, v)


def _rms_fwd(x, g, name, ts=512, after=()):
    s, d = x.shape

    def fn(first, last, xv, gv):
        return [_rms(xv, gv[...])], []

    return _rowwise(fn, n_rows=s, ts=ts, name=name, rows=[(x, 0, d)], vecs=[g], row_outs=[(d, BF16)],
                    after=after)[0]


def _rms_bwd(x, dn, dres, g, name, ts=256):
    s, d = x.shape

    def fn(first, last, xv, dnv, drv, gv):
        _, vjp = jax.vjp(_rms, xv, gv[...])
        dx, dg = vjp(dnv.astype(F32))
        return [drv + dx], [dg]

    return _rowwise(fn, n_rows=s, ts=ts, name=name, rows=[(x, 0, d), (dn, 0, d), (dres, 0, d)], vecs=[g],
                    row_outs=[(d, F32)], acc_outs=[(1, d)])


def _loss_head(h, tgt, g, ts=256):
    s, d = h.shape

    def fn(first, last, hv, tv, gv):
        def f(hh, gg):
            e = _rms(hh, gg) - tv
            per_row = jnp.mean(e * e, axis=-1, keepdims=True)
            return 0.5 * jnp.sum(per_row, axis=0, keepdims=True)

        l, vjp = jax.vjp(f, hv, gv[...])
        dh, dg = vjp(jnp.ones((1, 1), F32))
        return [dh], [dg, jnp.zeros((1, LANES), F32) + l]

    return _rowwise(fn, n_rows=s, ts=ts, name="loss_head", rows=[(h, 0, d), (tgt, 0, d)], vecs=[g],
                    row_outs=[(d, F32)], acc_outs=[(1, d), (1, LANES)])


def _dn_pre_fwd(qkvz, ba, wconv, alog, dt, heads, ts=128):
    s = qkvz.shape[0]
    d3 = wconv.shape[1]
    d = d3 // 3

    def fn(first, last, xc, bav, xp, wv, av, dv):
        xext = jnp.concatenate([jnp.where(first, 0.0, xp), xc], axis=0)
        cv = _causal_conv(xext, wv, DN_CONV, DN_HALO - (DN_CONV - 1), xc.shape[0])
        return list(_dn_point(cv, bav, av[...], dv[...], heads)), []

    return _rowwise(fn, n_rows=s, ts=ts, name="dn_pre_fwd", rows=[(qkvz, 0, d3), (ba, 0, LANES)],
                    prevs=[(qkvz, 0, d3, DN_HALO)], vecs=[wconv, alog, dt],
                    row_outs=[(d, F32), (d, F32), (d, F32), (LANES, F32)])


def _dn_pre_bwd1(qkvz, ba, wconv, alog, dt, dq, dk, dv, dgb, heads, ts=128):
    s = qkvz.shape[0]
    d3 = wconv.shape[1]
    d = d3 // 3

    def fn(first, last, xc, bav, dqv, dkv, dvv, dgbv, xp, wv, av, dtv):
        xext = jnp.concatenate([jnp.where(first, 0.0, xp), xc], axis=0)
        cv = _causal_conv(xext, wv, DN_CONV, DN_HALO - (DN_CONV - 1), xc.shape[0])
        _, vjp = jax.vjp(functools.partial(_dn_point, heads=heads), cv, bav, av[...], dtv[...])
        dc, dba, da, ddt = vjp((dqv, dkv, dvv, dgbv))
        return [dc, dba], [da, ddt]

    return _rowwise(fn, n_rows=s, ts=ts, name="dn_pre_bwd1",
                    rows=[(qkvz, 0, d3), (ba, 0, LANES), (dq, 0, d), (dk, 0, d), (dv, 0, d), (dgb, 0, LANES)],
                    prevs=[(qkvz, 0, d3, DN_HALO)], vecs=[wconv, alog, dt],
                    row_outs=[(d3, F32), (LANES, BF16)], acc_outs=[(1, LANES), (1, LANES)])


def _dn_pre_bwd2(dc, qkvz, dz, wconv, ts=128):
    s = qkvz.shape[0]
    d3 = wconv.shape[1]
    d = d3 // 3

    def fn(first, last, dcc, xc, dzv, xp, dcn, wv):
        n = dcc.shape[0]
        dcext = jnp.concatenate([dcc, jnp.where(last, 0.0, dcn)], axis=0)
        xext = jnp.concatenate([jnp.where(first, 0.0, xp), xc], axis=0)
        dx = None
        dw = []
        for j in range(DN_CONV):
            term = _shift_rows(dcext, DN_CONV - 1 - j)[:n] * wv[j:j + 1, :]
            dx = term if dx is None else dx + term
            dw.append(_colsum(dcc * _shift_rows(xext, DN_HALO - (DN_CONV - 1) + j)[:n]))
        return [jnp.concatenate([dx, dzv], axis=-1)], [_stack_rows(dw, DN_CONV)]

    return _rowwise(fn, n_rows=s, ts=ts, name="dn_pre_bwd2",
                    rows=[(dc, 0, d3), (qkvz, 0, d3), (dz, 0, d)],
                    prevs=[(qkvz, 0, d3, DN_HALO)], nexts=[(dc, 0, d3, DN_HALO)], vecs=[wconv],
                    row_outs=[(4 * d, BF16)], acc_outs=[(DN_CONV, d3)])


def _dn_post_fwd(o, qkvz, onorm, heads, ts=256):
    s, d = o.shape

    def fn(first, last, ov, zv, nv):
        return [_dn_post(ov, zv, nv[...], heads)], []

    return _rowwise(fn, n_rows=s, ts=ts, name="dn_post_fwd", rows=[(o, 0, d), (qkvz, 3, d)], vecs=[onorm],
                    row_outs=[(d, BF16)])[0]


def _dn_post_bwd(o, qkvz, onorm, dog, heads, ts=256):
    s, d = o.shape

    def fn(first, last, ov, zv, dv, nv):
        _, vjp = jax.vjp(functools.partial(_dn_post, heads=heads), ov, zv, nv[...])
        do, dz, dn = vjp(dv.astype(F32))
        return [do, dz], [dn]

    return _rowwise(fn, n_rows=s, ts=ts, name="dn_post_bwd", rows=[(o, 0, d), (qkvz, 3, d), (dog, 0, d)],
                    vecs=[onorm], row_outs=[(d, F32), (d, F32)], acc_outs=[(1, DN_HEAD_DIM)])


def _cv_mid_fwd(u, wdw, bdw, lng, lnb, ts=256):
    s = u.shape[0]
    d = u.shape[1] // 2

    def fn(first, last, uc, up, wv, bv, gv, lbv):
        uext = jnp.concatenate([jnp.where(first, 0.0, up), uc], axis=0)
        glu = uext[:, :d] * _sigmoid(uext[:, d:])
        c = _causal_conv(glu, wv, CV_WIDTH, CV_HALO - (CV_WIDTH - 1), uc.shape[0]) + bv[...]
        return [c, _ln_silu(c, gv[...], lbv[...])], []

    return _rowwise(fn, n_rows=s, ts=ts, name="cv_mid_fwd", rows=[(u, 0, 2 * d)], prevs=[(u, 0, 2 * d, CV_HALO)],
                    vecs=[wdw, bdw, lng, lnb], row_outs=[(d, F32), (d, BF16)])


def _cv_mid_bwd1(c, ds, dhp, lng, lnb, ts=256):
    s, d = c.shape

    def fn(first, last, cv, dsv, dhv, gv, bv):
        _, vjp = jax.vjp(_ln_silu, cv, gv[...], bv[...])
        dc, dg, db = vjp(dsv.astype(F32))
        return [dc], [dg, db, _colsum(dc), _colsum(dhv)]

    return _rowwise(fn, n_rows=s, ts=ts, name="cv_mid_bwd1", rows=[(c, 0, d), (ds, 0, d), (dhp, 0, d)],
                    vecs=[lng, lnb], row_outs=[(d, F32)], acc_outs=[(1, d)] * 4)


def _cv_mid_bwd2(dc, u, wdw, ts=256):
    s, d = dc.shape

    def fn(first, last, dcc, uc, up, dcn, wv):
        n = dcc.shape[0]
        dcext = jnp.concatenate([dcc, jnp.where(last, 0.0, dcn)], axis=0)
        uext = jnp.concatenate([jnp.where(first, 0.0, up), uc], axis=0)
        glu = uext[:, :d] * _sigmoid(uext[:, d:])
        dglu = None
        dw = []
        for j in range(CV_WIDTH):
            term = _shift_rows(dcext, CV_WIDTH - 1 - j)[:n] * wv[j:j + 1, :]
            dglu = term if dglu is None else dglu + term
            dw.append(_colsum(dcc * _shift_rows(glu, CV_HALO - (CV_WIDTH - 1) + j)[:n]))
        u1, sg = uc[:, :d], _sigmoid(uc[:, d:])
        du = jnp.concatenate([dglu * sg, dglu * u1 * sg * (1.0 - sg)], axis=-1)
        return [du], [_stack_rows(dw, CV_HALO), _colsum(du)]

    return _rowwise(fn, n_rows=s, ts=ts, name="cv_mid_bwd2", rows=[(dc, 0, d), (u, 0, 2 * d)],
                    prevs=[(u, 0, 2 * d, CV_HALO)], nexts=[(dc, 0, d, CV_HALO)], vecs=[wdw],
                    row_outs=[(2 * d, BF16)], acc_outs=[(CV_HALO, d), (1, 2 * d)])


def _attn_fwd(q, k, v, name, ts=256):
    s, d = q.shape

    def fn(first, last, qv, kv, vv):
        return [_attn_tile(qv.astype(F32), kv[...].astype(F32), vv[...].astype(F32))], []

    return _rowwise(fn, n_rows=s, ts=ts, name=name, rows=[(q, 0, d)], vecs=[k, v], row_outs=[(d, BF16)])[0]


def _attn_bwd(q, k, v, do, name, ts=256):
    s, d = q.shape
    m = k.shape[0]

    def fn(first, last, qv, dov, kv, vv):
        _, vjp = jax.vjp(_attn_tile, qv.astype(F32), kv[...].astype(F32), vv[...].astype(F32))
        dq, dk, dv = vjp(dov.astype(F32))
        return [dq], [dk, dv]

    return _rowwise(fn, n_rows=s, ts=ts, name=name, rows=[(q, 0, d), (do, 0, d)], vecs=[k, v],
                    row_outs=[(d, BF16)], acc_outs=[(m, d), (m, d)])


def _pad_lanes(a, off=0):
    r, n = a.shape
    return jnp.pad(a, ((0, 0), (off, LANES - off - n)))


def _local_step(x, mem, tgt, w, fetch=None, emit=None, first_after=()):
    s, d = x.shape
    heads = d // DN_HEAD_DIM
    g = {}
    if fetch is None:
        fetch = lambda group, after: None
    if emit is None:
        emit = lambda group, grads: ()

    def add_res(acc, res):
        return (res + acc,)

    w_in = w["dn_w_in"][0]
    assert w_in.shape[1] == 4 * d + 2 * heads
    w_qkvz = w_in[:, :4 * d]
    w_ba = _pad_lanes(w_in[:, 4 * d:])
    dn_norm = w["dn_norm"]
    alog = _pad_lanes(w["dn_a_log"], heads)
    dtb = _pad_lanes(w["dn_dt_bias"], heads)
    wconv = w["dn_w_conv"][0]
    n0 = _rms_fwd(x, dn_norm, "dn_rms", after=first_after)
    qkvz = _matmul(n0, w_qkvz, "nn", [F32], name="dn_in_proj")
    ba = _matmul(n0, w_ba, "nn", [F32], name="dn_in_proj_ba")
    q, k, v, gb = _dn_pre_fwd(qkvz, ba, wconv, alog, dtb, heads)
    o, states, tms = _delta_fwd(q, k, v, gb, heads)
    og = _dn_post_fwd(o, qkvz, w["dn_out_norm"], heads)
    fetch(1, [og])
    h1 = _matmul(og, w["dn_w_out"][0], "nn", [F32], name="dn_out_proj", epi=add_res, mn_extras=[x])

    def xattn_fwd(h, layer):
        nq = _rms_fwd(h, w["xa_norm"][layer:layer + 1], f"xa{layer}_rms")
        qx = _matmul(nq, w["xa_w_q"][layer], "nn", [BF16], name=f"xa{layer}_q")
        mn = _rms_fwd(mem, w["xa_mem_norm"][layer:layer + 1], f"xa{layer}_mem_rms")
        kv = _matmul(mn, w["xa_w_kv"][layer], "nn", [BF16], name=f"xa{layer}_kv")
        kx, vx = kv[:, :d], kv[:, d:]
        ox = _attn_fwd(qx, kx, vx, f"xa{layer}_attn")
        hn = _matmul(ox, w["xa_w_o"][layer], "nn", [F32], name=f"xa{layer}_o", epi=add_res, mn_extras=[h])
        return hn, (h, nq, qx, mn, kx, vx, ox)

    def mlp_fwd(h, layer):
        nm = _rms_fwd(h, w["mlp_norm"][layer:layer + 1], f"mlp{layer}_rms")

        def epi(acc):
            r = jnp.maximum(acc, 0.0)
            return acc, r * r

        u, a = _matmul(nm, w["mlp_w_up"][layer], "nn", [BF16, BF16], name=f"mlp{layer}_up", epi=epi)
        hn = _matmul(a, w["mlp_w_down"][layer], "nn", [F32], name=f"mlp{layer}_down", epi=add_res,
                     mn_extras=[h], tk=2048)
        return hn, (h, nm, u, a)

    h2, xa0 = xattn_fwd(h1, 0)
    h3, mlp0 = mlp_fwd(h2, 0)

    fetch(2, [h3])
    n1 = _rms_fwd(h3, w["cv_norm"], "cv_rms")
    u_cv = _matmul(n1, w["cv_w_pw1"][0], "nn", [F32], name="cv_pw1", epi=lambda acc, b: (acc + b,),
                   row_extras=[w["cv_b_pw1"]])
    wdw = jnp.pad(w["cv_w_dw"][0], ((0, CV_HALO - CV_WIDTH), (0, 0)))
    c_cv, s_cv = _cv_mid_fwd(u_cv, wdw, w["cv_b_dw"], w["cv_ln_g"], w["cv_ln_b"])
    h4 = _matmul(s_cv, w["cv_w_pw2"][0], "nn", [F32], name="cv_pw2",
                 epi=lambda acc, res, b: (res + acc + b,), mn_extras=[h3], row_extras=[w["cv_b_pw2"]])
    fetch(3, [h4])
    h5, xa1 = xattn_fwd(h4, 1)
    h6, mlp1 = mlp_fwd(h5, 1)

    fnorm = w["final_norm"].reshape(1, d)
    dh, g_fn, loss = _loss_head(h6, tgt, fnorm)
    g["final_norm"] = g_fn.reshape(d)

    def mlp_bwd(dh, layer, saved, after=()):
        h, nm, u, a = saved
        du = _matmul(dh, w["mlp_w_down"][layer], "nt", [BF16], name=f"mlp{layer}_down_dx", after=after,
                     epi=lambda acc, uu: (acc * 2.0 * jnp.maximum(uu.astype(F32), 0.0),), mn_extras=[u])
        gdown = _matmul(a, dh, "tn", [BF16], name=f"mlp{layer}_down_dw")
        dn = _matmul(du, w["mlp_w_up"][layer], "nt", [F32], name=f"mlp{layer}_up_dx", tk=2048)
        gup = _matmul(nm, du, "tn", [BF16], name=f"mlp{layer}_up_dw", out_dm=True)
        dhn, gn = _rms_bwd(h, dn, dh, w["mlp_norm"][layer:layer + 1], f"mlp{layer}_rms_bwd")
        return dhn, gup, gdown, gn

    def xattn_bwd(dh, layer, saved):
        h, nq, qx, mn, kx, vx, ox = saved
        dox = _matmul(dh, w["xa_w_o"][layer], "nt", [BF16], name=f"xa{layer}_o_dx")
        go = _matmul(ox, dh, "tn", [BF16], name=f"xa{layer}_o_dw")
        dqx, dkx, dvx = _attn_bwd(qx, kx, vx, dox, f"xa{layer}_attn_bwd")
        dn = _matmul(dqx, w["xa_w_q"][layer], "nt", [F32], name=f"xa{layer}_q_dx")
        gq = _matmul(nq, dqx, "tn", [BF16], name=f"xa{layer}_q_dw")
        dkv = jnp.concatenate([dkx, dvx], axis=-1)
        gkv = _matmul(mn, dkv, "tn", [BF16], name=f"xa{layer}_kv_dw", out_dm=True)
        dmn = _matmul(dkv, w["xa_w_kv"][layer], "nt", [F32], name=f"xa{layer}_kv_dx", tk=2048)
        _, gmem = _rms_bwd(mem, dmn, dmn, w["xa_mem_norm"][layer:layer + 1], f"xa{layer}_mem_rms_bwd")
        dhn, gn = _rms_bwd(h, dn, dh, w["xa_norm"][layer:layer + 1], f"xa{layer}_rms_bwd")
        return dhn, gq, gkv, go, gn, gmem

    dh, gup1, gdown1, gmn1 = mlp_bwd(dh, 1, mlp1)
    dh, gq1, gkv1, go1, gxn1, gmem1 = xattn_bwd(dh, 1, xa1)
    g.update(mlp_w_up=[None, gup1], mlp_w_down=[None, gdown1], xa_w_q=[None, gq1], xa_w_kv=[None, gkv1],
             xa_w_o=[None, go1])
    tok = emit(3, g)

    ds_cv = _matmul(dh, w["cv_w_pw2"][0], "nt", [BF16], name="cv_pw2_dx", after=tok)
    g["cv_w_pw2"] = [_matmul(s_cv, dh, "tn", [BF16], name="cv_pw2_dw")]
    dc_cv, g_lng, g_lnb, g_bdw, g_b2 = _cv_mid_bwd1(c_cv, ds_cv, dh, w["cv_ln_g"], w["cv_ln_b"])
    du_cv, g_wdw, g_b1 = _cv_mid_bwd2(dc_cv, u_cv, wdw)
    dn1 = _matmul(du_cv, w["cv_w_pw1"][0], "nt", [F32], name="cv_pw1_dx", tk=2048)
    g["cv_w_pw1"] = [_matmul(n1, du_cv, "tn", [BF16], name="cv_pw1_dw", out_dm=True)]
    dh, g_cvn = _rms_bwd(h3, dn1, dh, w["cv_norm"], "cv_rms_bwd")
    g.update(cv_ln_g=g_lng, cv_ln_b=g_lnb, cv_b_dw=g_bdw, cv_b_pw2=g_b2, cv_b_pw1=g_b1, cv_norm=g_cvn,
             cv_w_dw=g_wdw[:CV_WIDTH][None])

    tok = emit(2, g)
    dh, gup0, gdown0, gmn0 = mlp_bwd(dh, 0, mlp0, after=tok)
    dh, gq0, gkv0, go0, gxn0, gmem0 = xattn_bwd(dh, 0, xa0)
    g["mlp_w_up"][0] = gup0
    g["mlp_w_down"][0] = gdown0
    g["mlp_norm"] = jnp.concatenate([gmn0, gmn1], axis=0)
    g["xa_w_q"][0] = gq0
    g["xa_w_kv"][0] = gkv0
    g["xa_w_o"][0] = go0
    g["xa_norm"] = jnp.concatenate([gxn0, gxn1], axis=0)
    g["xa_mem_norm"] = jnp.concatenate([gmem0, gmem1], axis=0)
    tok = emit(1, g)

    dog = _matmul(dh, w["dn_w_out"][0], "nt", [BF16], name="dn_out_proj_dx", after=tok)
    g["dn_w_out"] = [_matmul(og, dh, "tn", [BF16], name="dn_out_proj_dw")]
    do, dz, g_on = _dn_post_bwd(o, qkvz, w["dn_out_norm"], dog, heads)
    dq, dk, dv, dgb = _delta_bwd(q, k, v, gb, states, tms, do, heads)
    dc, dba, g_alog, g_dt = _dn_pre_bwd1(qkvz, ba, wconv, alog, dtb, dq, dk, dv, dgb, heads)
    dqkvz, g_wconv = _dn_pre_bwd2(dc, qkvz, dz, wconv)
    g_qkvz = _matmul(n0, dqkvz, "tn", [BF16], name="dn_in_proj_dw")
    g_ba = _matmul(n0, dba, "tn", [BF16], name="dn_in_proj_ba_dw")
    g["dn_w_in"] = [jnp.concatenate([g_qkvz, g_ba[:, :2 * heads]], axis=1)]
    g["dn_w_conv"] = g_wconv[None]
    tok = emit(0, g)
    dn0a = _matmul(dba, w_ba, "nt", [F32], name="dn_in_proj_ba_dx", after=tok)
    dn0 = _matmul(dqkvz, w_qkvz, "nt", [F32], name="dn_in_proj_dx", epi=add_res, mn_extras=[dn0a], tk=2048)
    grad_x, g_dnn = _rms_bwd(x, dn0, dh, dn_norm, "dn_rms_bwd")
    g.update(dn_norm=g_dnn, dn_out_norm=g_on,
             dn_a_log=g_alog[:, heads:2 * heads], dn_dt_bias=g_dt[:, heads:2 * heads])
    return loss, grad_x, g


def _round_up(n, m):
    return (n + m - 1) // m * m


def _pack_rows(parts, cols, row_mult):
    lead = parts[0].shape[:-1]
    flat, offs, off = [], [], 0
    for p in parts:
        n = _round_up(p.shape[-1], cols)
        flat.append(jnp.pad(p, [(0, 0)] * len(lead) + [(0, n - p.shape[-1])]))
        offs.append(off)
        off += n
    total = _round_up(off, cols * row_mult)
    if total > off:
        flat.append(jnp.zeros(lead + (total - off,), parts[0].dtype))
    return jnp.concatenate(flat, axis=-1).reshape(lead + (total // cols, cols)), offs


def _unpack(packed, offs, shapes):
    lead = packed.shape[:-2]
    flat = packed.reshape(lead + (-1,))
    out = []
    for off, shp in zip(offs, shapes):
        n = 1
        for v in shp:
            n *= v
        out.append(flat[..., off:off + n].reshape(lead + tuple(shp)))
    return out


def kernel(x, mem, dn_norm, dn_w_in, dn_w_conv, dn_a_log, dn_dt_bias, dn_out_norm, dn_w_out, cv_norm, cv_w_pw1, cv_b_pw1, cv_w_dw, cv_b_dw, cv_ln_g, cv_ln_b, cv_w_pw2, cv_b_pw2, xa_norm, xa_mem_norm, xa_w_q, xa_w_kv, xa_w_o, mlp_norm, mlp_w_up, mlp_w_down, final_norm, loss_target, m_dn_norm, m_dn_w_in, m_dn_w_conv, m_dn_a_log, m_dn_dt_bias, m_dn_out_norm, m_dn_w_out, m_cv_norm, m_cv_w_pw1, m_cv_b_pw1, m_cv_w_dw, m_cv_b_dw, m_cv_ln_g, m_cv_ln_b, m_cv_w_pw2, m_cv_b_pw2, m_xa_norm, m_xa_mem_norm, m_xa_w_q, m_xa_w_kv, m_xa_w_o, m_mlp_norm, m_mlp_w_up, m_mlp_w_down, m_final_norm, v_dn_norm, v_dn_w_in, v_dn_w_conv, v_dn_a_log, v_dn_dt_bias, v_dn_out_norm, v_dn_w_out, v_cv_norm, v_cv_w_pw1, v_cv_b_pw1, v_cv_w_dw, v_cv_b_dw, v_cv_ln_g, v_cv_ln_b, v_cv_w_pw2, v_cv_b_pw2, v_xa_norm, v_xa_mem_norm, v_xa_w_q, v_xa_w_kv, v_xa_w_o, v_mlp_norm, v_mlp_w_up, v_mlp_w_down, v_final_norm):
    wsh = dict(dn_norm=dn_norm, dn_w_in=dn_w_in, dn_w_conv=dn_w_conv, dn_a_log=dn_a_log, dn_dt_bias=dn_dt_bias, dn_out_norm=dn_out_norm, dn_w_out=dn_w_out, cv_norm=cv_norm, cv_w_pw1=cv_w_pw1, cv_b_pw1=cv_b_pw1, cv_w_dw=cv_w_dw, cv_b_dw=cv_b_dw, cv_ln_g=cv_ln_g, cv_ln_b=cv_ln_b, cv_w_pw2=cv_w_pw2, cv_b_pw2=cv_b_pw2, xa_norm=xa_norm, xa_mem_norm=xa_mem_norm, xa_w_q=xa_w_q, xa_w_kv=xa_w_kv, xa_w_o=xa_w_o, mlp_norm=mlp_norm, mlp_w_up=mlp_w_up, mlp_w_down=mlp_w_down, final_norm=final_norm)
    msh = dict(dn_norm=m_dn_norm, dn_w_in=m_dn_w_in, dn_w_conv=m_dn_w_conv, dn_a_log=m_dn_a_log, dn_dt_bias=m_dn_dt_bias, dn_out_norm=m_dn_out_norm, dn_w_out=m_dn_w_out, cv_norm=m_cv_norm, cv_w_pw1=m_cv_w_pw1, cv_b_pw1=m_cv_b_pw1, cv_w_dw=m_cv_w_dw, cv_b_dw=m_cv_b_dw, cv_ln_g=m_cv_ln_g, cv_ln_b=m_cv_ln_b, cv_w_pw2=m_cv_w_pw2, cv_b_pw2=m_cv_b_pw2, xa_norm=m_xa_norm, xa_mem_norm=m_xa_mem_norm, xa_w_q=m_xa_w_q, xa_w_kv=m_xa_w_kv, xa_w_o=m_xa_w_o, mlp_norm=m_mlp_norm, mlp_w_up=m_mlp_w_up, mlp_w_down=m_mlp_w_down, final_norm=m_final_norm)
    vsh = dict(dn_norm=v_dn_norm, dn_w_in=v_dn_w_in, dn_w_conv=v_dn_w_conv, dn_a_log=v_dn_a_log, dn_dt_bias=v_dn_dt_bias, dn_out_norm=v_dn_out_norm, dn_w_out=v_dn_w_out, cv_norm=v_cv_norm, cv_w_pw1=v_cv_w_pw1, cv_b_pw1=v_cv_b_pw1, cv_w_dw=v_cv_w_dw, cv_b_dw=v_cv_b_dw, cv_ln_g=v_cv_ln_g, cv_ln_b=v_cv_ln_b, cv_w_pw2=v_cv_w_pw2, cv_b_pw2=v_cv_b_pw2, xa_norm=v_xa_norm, xa_mem_norm=v_xa_mem_norm, xa_w_q=v_xa_w_q, xa_w_kv=v_xa_w_kv, xa_w_o=v_xa_w_o, mlp_norm=v_mlp_norm, mlp_w_up=v_mlp_w_up, mlp_w_down=v_mlp_w_down, final_norm=v_final_norm)

    big_axis = dict(BIG)

    small_pack, small_offs = _pack_rows([wsh[nm].reshape(-1) for nm in SMALL_SH], LANES, 8)
    w = {nm: [None] * wsh[nm].shape[0] for nm in big_axis}

    def put_weights(group, gathered):
        for (nm, layer), gth in zip(group, gathered):
            if nm == "dn_w_in":
                w[nm][layer] = gth.transpose(1, 0, 2).reshape(gth.shape[1], N_DEV * gth.shape[2])
            elif big_axis[nm] == 1:
                w[nm][layer] = gth.reshape(N_DEV * gth.shape[1], gth.shape[2])
            else:
                w[nm][layer] = gth

    first = _all_gather([wsh[nm][layer].astype(BF16) for nm, layer in GATHER_GROUPS[0]] + [small_pack],
                        "weights_all_gather_0")
    put_weights(GATHER_GROUPS[0], first)
    gather_handles, tokens = {}, []
    for gi in range(1, len(GATHER_GROUPS)):
        shards = [wsh[nm][layer].astype(BF16) for nm, layer in GATHER_GROUPS[gi]]
        gather_handles[gi], tok = _exchange_start(shards, False, f"weights_gather_{gi}", after=[first[-1]] + tokens)
        tokens.append(tok)
    for nm, gth in zip(SMALL_SH, _unpack(first[-1], small_offs, [wsh[nm].shape for nm in SMALL_SH])):
        w[nm] = jnp.moveaxis(gth, 0, -2).reshape(gth.shape[1:-1] + (N_DEV * gth.shape[-1],))
    for nm in REPL:
        w[nm] = wsh[nm]

    def fetch(gi, after):
        put_weights(GATHER_GROUPS[gi], _exchange_wait(gather_handles[gi], after))

    scatter_handles = {}

    def emit(gi, g):
        blocks = []
        for nm, layer in SCATTER_GROUPS[gi]:
            gw = g[nm][layer]
            if nm == "dn_w_in":
                gw = gw.reshape(gw.shape[0], N_DEV, -1).transpose(1, 0, 2)
            elif big_axis[nm] == 1:
                gw = gw.reshape(N_DEV, gw.shape[0] // N_DEV, gw.shape[1])
            blocks.append(gw)
        if gi == 0:
            gsmall_pack, _ = _pack_rows(
                [jnp.moveaxis(g[nm].reshape(g[nm].shape[:-1] + (N_DEV, -1)), -2, 0).reshape(N_DEV, -1)
                 for nm in SMALL_SH], LANES, 8)
            blocks.append(gsmall_pack)
        scatter_handles[gi], tok = _exchange_start(blocks, True, f"grads_scatter_{gi}")
        return [tok]

    loss_part, grad_x, g = _local_step(x[0], mem[0], loss_target[0], w, fetch, emit, tokens)

    recv = {nm: [None] * wsh[nm].shape[0] for nm in big_axis}
    gsh, delta, new_m, new_v = {}, {}, {}, {}
    after = [grad_x]
    done = set()
    for gi in reversed(range(len(SCATTER_GROUPS))):
        landed = _exchange_wait(scatter_handles[gi], after)
        for (nm, layer), r in zip(SCATTER_GROUPS[gi], landed):
            recv[nm][layer] = r
        if gi == 0:
            rsmall = landed[-1]
        for nm in big_axis:
            if nm not in done and all(r is not None for r in recv[nm]):
                gsh[nm], delta[nm], new_m[nm], new_v[nm] = _adamw_reduce(
                    recv[nm], wsh[nm], msh[nm], vsh[nm], f"adamw_{nm}")
                done.add(nm)
                after = [delta[nm]]
    gsmall_red = _slot_sum(rsmall, "grads_small_sum", 512)
    repl_pack, repl_offs = _pack_rows([g[nm].reshape(-1) for nm in REPL], LANES, 8)
    (repl_all,) = _all_gather([repl_pack], "repl_grads_all_gather")
    repl_red = _slot_sum(repl_all, "repl_grads_sum", 512)
    for nm, val in zip(SMALL_SH, _unpack(gsmall_red, small_offs, [wsh[nm].shape for nm in SMALL_SH])):
        gsh[nm] = val
    for nm, val in zip(REPL, _unpack(repl_red, repl_offs, [wsh[nm].shape for nm in REPL])):
        gsh[nm] = val
    small_names = list(SMALL_SH) + list(REPL)
    packs = []
    for src in (wsh, gsh, msh, vsh):
        pk, sm_offs = _pack_rows([src[nm].reshape(-1) for nm in small_names], LANES, 8)
        packs.append(pk)
    outs = _adamw(*packs, "adamw_small")
    for dst, pk in zip((delta, new_m, new_v), outs):
        for nm, val in zip(small_names, _unpack(pk, sm_offs, [wsh[nm].shape for nm in small_names])):
            dst[nm] = val

    loss = lax.psum(loss_part[0, 0], ("x", "y", "c"))
    return (loss, grad_x[None], *[gsh[nm] for nm in WEIGHTS], *[delta[nm] for nm in WEIGHTS],
            *[new_m[nm] for nm in WEIGHTS], *[new_v[nm] for nm in WEIGHTS])
```

```python
import functools

import jax
import jax.numpy as jnp
from jax import lax
from jax.experimental import pallas as pl
from jax.experimental.pallas import tpu as pltpu

F32 = jnp.float32
BF16 = jnp.bfloat16
HP = lax.Precision.HIGHEST
MESH_IDS = pl.DeviceIdType.MESH

N_DEV = 8
LANES = 128
RMS_EPS = 1e-6
LN_EPS = 1e-5
DN_HEAD_DIM = 128
DN_CONV = 4
DN_CHUNK = 64
CV_WIDTH = 31
XA_HEADS = 4
DN_HALO = 8
CV_HALO = 32

ADAM_LR = 0.001
ADAM_B1 = 0.9
ADAM_B2 = 0.999
ADAM_EPS = 1e-08
ADAM_WD = 0.01
ADAM_STEP = 10

BIG = (("dn_w_in", 2), ("dn_w_out", 1), ("cv_w_pw1", 2), ("cv_w_pw2", 1), ("xa_w_q", 1), ("xa_w_kv", 2),
       ("xa_w_o", 1), ("mlp_w_up", 2), ("mlp_w_down", 1))
_LAYER_GROUP = ("xa_w_q", "xa_w_o", "mlp_w_down", "xa_w_kv", "mlp_w_up")
GATHER_GROUPS = (
    (("dn_w_in", 0),),
    (("dn_w_out", 0),) + tuple((nm, 0) for nm in _LAYER_GROUP),
    (("cv_w_pw2", 0), ("cv_w_pw1", 0)),
    tuple((nm, 1) for nm in _LAYER_GROUP),
)
SCATTER_GROUPS = (
    (("dn_w_out", 0), ("dn_w_in", 0)),
    tuple((nm, 0) for nm in _LAYER_GROUP),
    (("cv_w_pw2", 0), ("cv_w_pw1", 0)),
    tuple((nm, 1) for nm in _LAYER_GROUP),
)
SMALL_SH = ("cv_norm", "cv_b_pw1", "cv_b_dw", "cv_ln_g", "cv_ln_b", "cv_b_pw2", "cv_w_dw", "dn_w_conv")
REPL = ("dn_norm", "dn_a_log", "dn_dt_bias", "dn_out_norm", "xa_norm", "xa_mem_norm", "mlp_norm", "final_norm")
WEIGHTS = ("dn_norm", "dn_w_in", "dn_w_conv", "dn_a_log", "dn_dt_bias", "dn_out_norm", "dn_w_out", "cv_norm",
           "cv_w_pw1", "cv_b_pw1", "cv_w_dw", "cv_b_dw", "cv_ln_g", "cv_ln_b", "cv_w_pw2", "cv_b_pw2", "xa_norm",
           "xa_mem_norm", "xa_w_q", "xa_w_kv", "xa_w_o", "mlp_norm", "mlp_w_up", "mlp_w_down", "final_norm")


def _dot_dims(mode, batched):
    o = 1 if batched else 0
    contract = {"nn": ((1 + o,), (o,)), "nt": ((1 + o,), (1 + o,)), "tn": ((o,), (o,))}[mode]
    return (contract, (((0,), (0,)) if batched else ((), ())))


def _bdot(a, b, mode):
    return lax.dot_general(a.astype(BF16), b.astype(BF16), _dot_dims(mode, a.ndim == 3),
                           preferred_element_type=F32)


@functools.partial(jax.custom_vjp, nondiff_argnums=(2,))
def _mm(a, b, mode):
    return _bdot(a, b, mode)


def _mm_fwd(a, b, mode):
    return _bdot(a, b, mode), (a, b)


def _mm_bwd(mode, res, ct):
    a, b = res
    if mode == "nn":
        da, db = _bdot(ct, b, "nt"), _bdot(a, ct, "tn")
    elif mode == "nt":
        da, db = _bdot(ct, b, "nn"), _bdot(ct, a, "tn")
    else:
        da, db = _bdot(b, ct, "nt"), _bdot(a, ct, "nn")
    return da.astype(a.dtype), db.astype(b.dtype)


_mm.defvjp(_mm_fwd, _mm_bwd)


def _sigmoid(x):
    return 0.5 * (jnp.tanh(0.5 * x) + 1.0)


def _silu(x):
    return x * _sigmoid(x)


def _softplus(x):
    return jnp.maximum(x, 0.0) + jnp.log(1.0 + jnp.exp(-jnp.abs(x)))


def _rms(x, g):
    r = lax.rsqrt(jnp.mean(x * x, axis=-1, keepdims=True) + RMS_EPS)
    return x * r * g


def _shift_rows(x, off):
    if off == 0:
        return x
    return pltpu.roll(x, x.shape[0] - off, 0)


def _series_dot(a, b, mode):
    return _bdot(a, b, mode)


def _chunk_masks(c):
    ii = lax.broadcasted_iota(jnp.int32, (c, c), 0)
    jj = lax.broadcasted_iota(jnp.int32, (c, c), 1)
    return (ii == jj).astype(F32), ii >= jj, ii > jj


def _neumann_inverse(lm):
    n = lm.shape[-1]
    t = -lm
    p = lm
    size = 2
    while size < n:
        size *= 2
        p = _series_dot(p, p, "nn")
        t = t + p + _series_dot(t, p, "nn")
    return t


def _apply_inverse(tm, rhs, mode):
    return rhs + _series_dot(tm, rhs, mode)


@jax.custom_vjp
def _unit_lower_solve(lm, rhs, tm):
    return _apply_inverse(tm, rhs, "nn")


def _uls_fwd(lm, rhs, tm):
    sol = _apply_inverse(tm, rhs, "nn")
    return sol, (tm, sol)


def _uls_bwd(res, ct):
    tm, sol = res
    d_rhs = _apply_inverse(tm, ct, "tn")
    return -_bdot(d_rhs, sol, "nt"), d_rhs, jnp.zeros_like(tm)


_unit_lower_solve.defvjp(_uls_fwd, _uls_bwd)


def _delta_chunk(q, k, v, gcol, bcol, s0, tm=None):
    c = q.shape[1]
    eye, causal, strict = _chunk_masks(c)
    grow = jnp.sum(eye * gcol, axis=1, keepdims=True)
    gc = jnp.sum(jnp.where(causal, grow, 0.0), axis=2, keepdims=True)
    gc_row = jnp.sum(eye * gc, axis=1, keepdims=True)
    decay = jnp.exp(jnp.where(causal, gc - gc_row, -jnp.inf))
    kb = k * bcol
    lm = jnp.where(strict, _mm(kb, k, "nt") * decay, 0.0)
    if tm is None:
        tm = _neumann_inverse(lax.stop_gradient(lm))
    egc = jnp.exp(gc)
    rhs = jnp.concatenate([v * bcol, kb * egc], axis=-1)
    sol = _unit_lower_solve(lm, rhs, tm)
    dv_ = v.shape[-1]
    u, w = sol[..., :dv_], sol[..., dv_:]
    attn = _mm(q, k, "nt") * decay
    qd = q * egc
    gl = jnp.sum(grow, axis=2, keepdims=True)
    kd = k * jnp.exp(gl - gc)
    v_new = u - _mm(w, s0, "nn")
    o = _mm(qd, s0, "nn") + _mm(attn, v_new, "nn")
    s1 = s0 * jnp.exp(gl) + _mm(kd, v_new, "tn")
    return o, s1, tm


def _dn_point(cv, ba, alog, dt, heads):
    a = _silu(cv)
    d = cv.shape[1] // 3
    qs, ks = [], []
    for h in range(heads):
        qh = a[:, h * DN_HEAD_DIM:(h + 1) * DN_HEAD_DIM]
        qs.append(qh * lax.rsqrt(jnp.sum(qh * qh, axis=-1, keepdims=True) + 1e-6) * (DN_HEAD_DIM ** -0.5))
        kh = a[:, d + h * DN_HEAD_DIM:d + (h + 1) * DN_HEAD_DIM]
        ks.append(kh * lax.rsqrt(jnp.sum(kh * kh, axis=-1, keepdims=True) + 1e-6))
    q = jnp.concatenate(qs, axis=-1)
    k = jnp.concatenate(ks, axis=-1)
    v = a[:, 2 * d:]
    lane = lax.broadcasted_iota(jnp.int32, ba.shape, 1)
    beta = _sigmoid(ba)
    g = -jnp.exp(alog) * _softplus(ba + dt)
    gb = jnp.where(lane < heads, beta, jnp.where(lane < 2 * heads, g, 0.0))
    return q, k, v, gb


def _dn_post(o, z, onorm, heads):
    outs = []
    for h in range(heads):
        oh = o[:, h * DN_HEAD_DIM:(h + 1) * DN_HEAD_DIM]
        outs.append(oh * lax.rsqrt(jnp.mean(oh * oh, axis=-1, keepdims=True) + RMS_EPS) * onorm)
    return jnp.concatenate(outs, axis=-1) * _silu(z)


def _attn_tile(q, k, v):
    hd = q.shape[1] // XA_HEADS
    outs = []
    for h in range(XA_HEADS):
        sl = slice(h * hd, (h + 1) * hd)
        s = _mm(q[:, sl], k[:, sl], "nt") * (hd ** -0.5)
        m = lax.stop_gradient(jnp.max(s, axis=-1, keepdims=True))
        e = jnp.exp(s - m)
        p = e / jnp.sum(e, axis=-1, keepdims=True)
        outs.append(_mm(p, v[:, sl], "nn"))
    return jnp.concatenate(outs, axis=-1)


def _ln_silu(c, g, b):
    mu = jnp.mean(c, axis=-1, keepdims=True)
    xc = c - mu
    y = xc * lax.rsqrt(jnp.mean(xc * xc, axis=-1, keepdims=True) + LN_EPS)
    return _silu(y * g + b)


def _causal_conv(xext, w, width, lead, ts):
    acc = None
    for j in range(width):
        term = _shift_rows(xext, lead + j)[:ts] * w[j:j + 1, :]
        acc = term if acc is None else acc + term
    return acc


def _colsum(x):
    return jnp.sum(x, axis=0, keepdims=True)


def _stack_rows(rows, n_rows):
    c = rows[0].shape[1]
    ridx = lax.broadcasted_iota(jnp.int32, (n_rows, c), 0)
    out = jnp.zeros((n_rows, c), F32)
    for j, r in enumerate(rows):
        out = out + jnp.where(ridx == j, r, 0.0)
    return out


def _matmul(a, b, mode, out_dtypes, *, name, epi=None, mn_extras=(), row_extras=(), out_dm=False, after=(),
            tm=512, tn=1024, tk=1024):
    b_dm = b.ndim == 3
    b_shape = (b.shape[1], N_DEV * b.shape[2]) if b_dm else b.shape
    if mode == "nn":
        (m, k), (k2, n) = a.shape, b_shape
    elif mode == "nt":
        (m, k), (n, k2) = a.shape, b_shape
    else:
        (k, m), (k2, n) = a.shape, b_shape
    assert k == k2, (a.shape, b.shape, mode)
    tm, tn, tk = min(tm, m), min(tn, n), min(tk, k)
    if b_dm:
        assert mode in ("nn", "nt")
        if mode == "nn":
            tn = b.shape[2]
        else:
            tk = b.shape[2]
    if out_dm:
        tn = n // N_DEV
    assert m % tm == 0 and n % tn == 0 and k % tk == 0, (m, n, k, tm, tn, tk)
    nk = k // tk
    if mode == "tn":
        a_spec = pl.BlockSpec((tk, tm), lambda j, i, kk: (kk, i))
    else:
        a_spec = pl.BlockSpec((tm, tk), lambda j, i, kk: (i, kk))
    if b_dm:
        b_spec = (pl.BlockSpec((None, tn, tk), lambda j, i, kk: (kk, j, 0)) if mode == "nt"
                  else pl.BlockSpec((None, tk, tn), lambda j, i, kk: (j, kk, 0)))
    else:
        b_spec = (pl.BlockSpec((tn, tk), lambda j, i, kk: (j, kk)) if mode == "nt"
                  else pl.BlockSpec((tk, tn), lambda j, i, kk: (kk, j)))
    mn_spec = pl.BlockSpec((tm, tn), lambda j, i, kk: (i, j))
    row_spec = pl.BlockSpec((1, tn), lambda j, i, kk: (0, j))
    n_extra = len(mn_extras) + len(row_extras)
    n_out = len(out_dtypes)
    in_specs = ([a_spec, b_spec] + [mn_spec] * len(mn_extras) + [row_spec] * len(row_extras)
                + [_ANY_SPEC] * len(after))
    args = [a, b, *mn_extras, *row_extras, *after]
    if out_dm:
        out_specs = [pl.BlockSpec((None, tm, tn), lambda j, i, kk: (j, i, 0))] * n_out
        out_shape = [jax.ShapeDtypeStruct((N_DEV, m, tn), dt) for dt in out_dtypes]
    else:
        out_specs = [mn_spec] * n_out
        out_shape = [jax.ShapeDtypeStruct((m, n), dt) for dt in out_dtypes]
    n_in = len(args)

    def finish(acc_val, extras, outs):
        vals = (acc_val,) if epi is None else epi(acc_val, *[e[...] for e in extras])
        for o_ref, val in zip(outs, vals):
            o_ref[...] = val.astype(o_ref.dtype)

    def body_one_step(*refs):
        finish(_bdot(refs[0][...], refs[1][...], mode), refs[2:2 + n_extra], refs[n_in:n_in + n_out])

    def body(*refs):
        a_ref, b_ref = refs[0], refs[1]
        acc = refs[-1]
        kk = pl.program_id(2)

        @pl.when(kk == 0)
        def _():
            acc[...] = jnp.zeros_like(acc)

        acc[...] += _bdot(a_ref[...], b_ref[...], mode)

        @pl.when(kk == nk - 1)
        def _():
            finish(acc[...], refs[2:2 + n_extra], refs[n_in:n_in + n_out])

    res = pl.pallas_call(
        body_one_step if nk == 1 else body, name=name,
        grid=(n // tn, m // tm, nk),
        in_specs=in_specs, out_specs=out_specs, out_shape=out_shape,
        scratch_shapes=[] if nk == 1 else [pltpu.VMEM((tm, tn), F32)],
        compiler_params=pltpu.CompilerParams(dimension_semantics=("parallel", "parallel", "arbitrary")),
    )(*args)
    return res[0] if n_out == 1 else res


def _rowwise(fn, *, n_rows, ts, name, rows=(), prevs=(), nexts=(), vecs=(), row_outs=(), acc_outs=(), after=()):
    ts = min(ts, n_rows)
    assert n_rows % ts == 0
    nblk = n_rows // ts
    in_specs, args = [], []
    for arr, cb, w in rows:
        in_specs.append(pl.BlockSpec((ts, w), functools.partial(lambda i, cb: (i, cb), cb=cb)))
        args.append(arr)
    for arr, cb, w, halo in prevs:
        per = ts // halo
        in_specs.append(pl.BlockSpec(
            (halo, w), functools.partial(lambda i, cb, per: (jnp.maximum(i * per - 1, 0), cb), cb=cb, per=per)))
        args.append(arr)
    for arr, cb, w, halo in nexts:
        per = ts // halo
        last_blk = n_rows // halo - 1
        in_specs.append(pl.BlockSpec(
            (halo, w), functools.partial(lambda i, cb, per, lb: (jnp.minimum((i + 1) * per, lb), cb),
                                         cb=cb, per=per, lb=last_blk)))
        args.append(arr)
    for arr in vecs:
        in_specs.append(pl.BlockSpec(arr.shape, functools.partial(lambda i, nd: (0,) * nd, nd=arr.ndim)))
        args.append(arr)
    out_specs, out_shape = [], []
    for w, dt in row_outs:
        out_specs.append(pl.BlockSpec((ts, w), lambda i: (i, 0)))
        out_shape.append(jax.ShapeDtypeStruct((n_rows, w), dt))
    for shp in acc_outs:
        out_specs.append(pl.BlockSpec(shp, functools.partial(lambda i, nd: (0,) * nd, nd=len(shp))))
        out_shape.append(jax.ShapeDtypeStruct(shp, F32))
    n_used = len(args)
    n_tiles = n_used - len(vecs)
    in_specs += [_ANY_SPEC] * len(after)
    args += list(after)
    n_in, n_ro, n_acc = len(args), len(row_outs), len(acc_outs)

    def body(*refs):
        ins, ro, ac = refs[:n_used], refs[n_in:n_in + n_ro], refs[n_in + n_ro:]
        i = pl.program_id(0)
        rvals, avals = fn(i == 0, i == nblk - 1, *[r[...] for r in ins[:n_tiles]], *ins[n_tiles:])
        for r, val in zip(ro, rvals):
            r[...] = val.astype(r.dtype)
        if n_acc:
            @pl.when(i == 0)
            def _():
                for r in ac:
                    r[...] = jnp.zeros_like(r)

            for r, val in zip(ac, avals):
                r[...] += val

    res = pl.pallas_call(
        body, name=name, grid=(nblk,), in_specs=in_specs, out_specs=out_specs, out_shape=out_shape,
        compiler_params=pltpu.CompilerParams(dimension_semantics=("arbitrary",)),
    )(*args)
    return res


def _delta_fwd(q, k, v, gb, heads):
    s, hd = q.shape
    n = s // DN_CHUNK
    blk = pl.BlockSpec((DN_CHUNK, hd), lambda c: (c, 0))
    gspec = pl.BlockSpec((DN_CHUNK, LANES), lambda c: (c, 0))

    def body(q_ref, k_ref, v_ref, gb_ref, o_ref, st_ref, tm_ref, state):
        @pl.when(pl.program_id(0) == 0)
        def _():
            state[...] = jnp.zeros_like(state)

        s0 = state[...]
        st_ref[0] = s0
        o, s1, tm = _delta_chunk(*_split_heads(q_ref, k_ref, v_ref, gb_ref[...], heads), s0)
        for h in range(heads):
            o_ref[:, h * DN_HEAD_DIM:(h + 1) * DN_HEAD_DIM] = o[h]
        state[...] = s1
        tm_ref[0] = tm

    return pl.pallas_call(
        body, name="dn_delta_fwd", grid=(n,),
        in_specs=[blk, blk, blk, gspec],
        out_specs=[blk, pl.BlockSpec((1, heads, DN_HEAD_DIM, DN_HEAD_DIM), lambda c: (c, 0, 0, 0)),
                   pl.BlockSpec((1, heads, DN_CHUNK, DN_CHUNK), lambda c: (c, 0, 0, 0))],
        out_shape=[jax.ShapeDtypeStruct((s, hd), F32),
                   jax.ShapeDtypeStruct((n, heads, DN_HEAD_DIM, DN_HEAD_DIM), F32),
                   jax.ShapeDtypeStruct((n, heads, DN_CHUNK, DN_CHUNK), F32)],
        scratch_shapes=[pltpu.VMEM((heads, DN_HEAD_DIM, DN_HEAD_DIM), F32)],
        compiler_params=pltpu.CompilerParams(dimension_semantics=("arbitrary",)),
    )(q, k, v, gb)


def _split_heads(q_ref, k_ref, v_ref, gbv, heads):
    def hs(ref):
        return jnp.stack([ref[:, h * DN_HEAD_DIM:(h + 1) * DN_HEAD_DIM] for h in range(heads)])

    gcol = jnp.stack([gbv[:, heads + h:heads + h + 1] for h in range(heads)])
    bcol = jnp.stack([gbv[:, h:h + 1] for h in range(heads)])
    return hs(q_ref), hs(k_ref), hs(v_ref), gcol, bcol


def _delta_bwd(q, k, v, gb, states, tms, do, heads):
    s, hd = q.shape
    n = s // DN_CHUNK
    blk = pl.BlockSpec((DN_CHUNK, hd), lambda c: (n - 1 - c, 0))
    gspec = pl.BlockSpec((DN_CHUNK, LANES), lambda c: (n - 1 - c, 0))
    sspec = pl.BlockSpec((1, heads, DN_HEAD_DIM, DN_HEAD_DIM), lambda c: (n - 1 - c, 0, 0, 0))
    tspec = pl.BlockSpec((1, heads, DN_CHUNK, DN_CHUNK), lambda c: (n - 1 - c, 0, 0, 0))

    def body(q_ref, k_ref, v_ref, gb_ref, st_ref, tm_ref, do_ref, dq_ref, dk_ref, dv_ref, dgb_ref, dstate):
        @pl.when(pl.program_id(0) == 0)
        def _():
            dstate[...] = jnp.zeros_like(dstate)

        gbv = gb_ref[...]
        tm = tm_ref[0]

        def chunk(qh, kh, vh, gcol, bcol, s0):
            return _delta_chunk(qh, kh, vh, gcol, bcol, s0, tm)[:2]

        _, vjp = jax.vjp(chunk, *_split_heads(q_ref, k_ref, v_ref, gbv, heads), st_ref[0])
        doh = jnp.stack([do_ref[:, h * DN_HEAD_DIM:(h + 1) * DN_HEAD_DIM] for h in range(heads)])
        dq, dk, dv, dg, db, ds0 = vjp((doh, dstate[...]))
        dstate[...] = ds0
        lane = lax.broadcasted_iota(jnp.int32, gbv.shape, 1)
        dgb = jnp.zeros(gbv.shape, F32)
        for h in range(heads):
            sl = slice(h * DN_HEAD_DIM, (h + 1) * DN_HEAD_DIM)
            dq_ref[:, sl] = dq[h]
            dk_ref[:, sl] = dk[h]
            dv_ref[:, sl] = dv[h]
            dgb = dgb + jnp.where(lane == h, db[h], 0.0) + jnp.where(lane == heads + h, dg[h], 0.0)
        dgb_ref[...] = dgb

    return pl.pallas_call(
        body, name="dn_delta_bwd", grid=(n,),
        in_specs=[blk, blk, blk, gspec, sspec, tspec, blk],
        out_specs=[blk, blk, blk, gspec],
        out_shape=[jax.ShapeDtypeStruct((s, hd), F32)] * 3 + [jax.ShapeDtypeStruct((s, LANES), F32)],
        scratch_shapes=[pltpu.VMEM((heads, DN_HEAD_DIM, DN_HEAD_DIM), F32)],
        compiler_params=pltpu.CompilerParams(dimension_semantics=("arbitrary",)),
    )(q, k, v, gb, states, tms, do)


def _dev_index(px, py, pc):
    return 4 * px + 2 * py + pc


def _all_gather(arrs, name):
    n = len(arrs)

    def body(*refs):
        xs, outs = refs[:n], refs[n:2 * n]
        send_sems, recv_sems, local_sems = refs[2 * n:]
        x, y, c = lax.axis_index("x"), lax.axis_index("y"), lax.axis_index("c")
        me, sibling = (x, y, c), (x, y, 1 - c)
        chips = [(1 - x, y), (x, 1 - y), (1 - x, 1 - y)]

        def copy(a, kk, block, to, src=None):
            dst = outs[a].at[_dev_index(*block)]
            return pltpu.make_async_remote_copy(
                src_ref=dst if src is None else src, dst_ref=dst,
                send_sem=send_sems.at[a * 7 + kk], recv_sem=recv_sems.at[a * 7 + kk],
                device_id=to, device_id_type=MESH_IDS)

        mine = [pltpu.make_async_copy(xs[a], outs[a].at[_dev_index(*me)], local_sems.at[a]) for a in range(n)]
        for cp in mine:
            cp.start()
        first = []
        for a in range(n):
            first.append(copy(a, 0, me, sibling, src=xs[a]))
            first += [copy(a, 1 + j, me, (*chip, c), src=xs[a]) for j, chip in enumerate(chips)]
        for cp in first:
            cp.start()
        passed = []
        for j, chip in enumerate(chips):
            for a in range(n):
                copy(a, 1 + j, (*chip, c), me).wait_recv()
                fwd = copy(a, 4 + j, (*chip, c), sibling)
                fwd.start()
                passed.append(fwd)
        for a in range(n):
            copy(a, 0, sibling, me).wait_recv()
        for j, chip in enumerate(chips):
            for a in range(n):
                copy(a, 4 + j, (*chip, 1 - c), me).wait_recv()
        for cp in first + passed:
            cp.wait_send()
        for cp in mine:
            cp.wait()

    hbm = pl.BlockSpec(memory_space=pltpu.HBM)
    res = pl.pallas_call(
        body, name=name,
        in_specs=[hbm] * n, out_specs=[hbm] * n,
        out_shape=[jax.ShapeDtypeStruct((N_DEV,) + a.shape, a.dtype) for a in arrs],
        scratch_shapes=[pltpu.SemaphoreType.DMA((7 * n,)), pltpu.SemaphoreType.DMA((7 * n,)),
                        pltpu.SemaphoreType.DMA((n,))],
    )(*arrs)
    return list(res)


_FLIPS = ((0, 0, 1), (1, 0, 0), (0, 1, 0), (1, 1, 0), (1, 0, 1), (0, 1, 1), (1, 1, 1))
_HBM_SPEC = pl.BlockSpec(memory_space=pltpu.HBM)
_SEM_SPEC = pl.BlockSpec(memory_space=pltpu.SEMAPHORE)
_ANY_SPEC = pl.BlockSpec(memory_space=pl.ANY)
_DATAFLOW = pltpu.SideEffectType.DATAFLOW_SIDE_EFFECTING
TOKEN_SHAPE = (8, LANES)


def _mesh_me():
    return lax.axis_index("x"), lax.axis_index("y"), lax.axis_index("c")


def _flipped(me, f):
    return tuple(1 - v if fl else v for v, fl in zip(me, f))


def _exchange_copies(xs, lands, send_sems, recv_sems, scatter, landed):
    me = _mesh_me()
    cps = []
    for kk, f in enumerate(_FLIPS):
        p = _flipped(me, f)
        for a in range(len(xs)):
            cps.append(pltpu.make_async_remote_copy(
                src_ref=xs[a].at[_dev_index(*p)] if scatter else xs[a],
                dst_ref=lands[a].at[_dev_index(*(p if landed else me))],
                send_sem=send_sems.at[a * 7 + kk], recv_sem=recv_sems.at[a * 7 + kk],
                device_id=p, device_id_type=MESH_IDS))
    return cps


def _exchange_start(srcs, lands, scatter, name, after=()):
    n = len(srcs)

    n_after = len(after)

    def body(*refs):
        xs, ls = refs[:n], refs[n:2 * n]
        send_sems, recv_sems = refs[2 * n + n_after], refs[2 * n + n_after + 1]
        token = refs[-1]
        for cp in _exchange_copies(xs, ls, send_sems, recv_sems, scatter, landed=False):
            cp.start()
        token[...] = jnp.zeros_like(token)

    operands = [pltpu.with_memory_space_constraint(a, pltpu.HBM) for a in list(srcs) + list(lands)]
    res = pl.pallas_call(
        body, name=name,
        in_specs=[_HBM_SPEC] * (2 * n) + [_ANY_SPEC] * len(after),
        out_specs=[_SEM_SPEC, _SEM_SPEC] + [_HBM_SPEC] * (2 * n) + [pl.BlockSpec(memory_space=pltpu.VMEM)],
        out_shape=[pltpu.SemaphoreType.DMA((7 * n,)), pltpu.SemaphoreType.DMA((7 * n,))]
        + [pltpu.HBM(a.shape, a.dtype) for a in operands] + [jax.ShapeDtypeStruct(TOKEN_SHAPE, F32)],
        input_output_aliases={i: 2 + i for i in range(2 * n)},
        compiler_params=pltpu.CompilerParams(has_side_effects=_DATAFLOW),
    )(*operands, *after)
    return (res[0], res[1], list(res[2:2 + n]), list(res[2 + n:2 + 2 * n]), scatter, name), res[-1]


def _exchange_wait(handle, after):
    send_sems, recv_sems, srcs, lands, scatter, name = handle
    n = len(srcs)
    n_after = len(after)

    def body(*refs):
        xs, ls = refs[:n], refs[n:2 * n]
        send_sems_ref, recv_sems_ref = refs[2 * n], refs[2 * n + 1]
        for cp in _exchange_copies(xs, ls, send_sems_ref, recv_sems_ref, scatter, landed=True):
            cp.wait_send()
            cp.wait_recv()

    res = pl.pallas_call(
        body, name=name + "_wait",
        in_specs=[_HBM_SPEC] * (2 * n) + [_SEM_SPEC, _SEM_SPEC] + [_ANY_SPEC] * n_after,
        out_specs=[_HBM_SPEC] * (2 * n),
        out_shape=[pltpu.HBM(a.shape, a.dtype) for a in srcs + lands],
        input_output_aliases={i: i for i in range(2 * n)},
        compiler_params=pltpu.CompilerParams(has_side_effects=_DATAFLOW),
    )(*srcs, *lands, send_sems, recv_sems, *after)
    return list(res[:n]), list(res[n:])


def _slot_sum(g, name, tr):
    _, r, c = g.shape
    tr = min(tr, r)
    assert r % tr == 0

    def body(g_ref, o_ref):
        acc = g_ref[0].astype(F32)
        for s in range(1, N_DEV):
            acc = acc + g_ref[s].astype(F32)
        o_ref[...] = acc

    return pl.pallas_call(
        body, name=name, grid=(r // tr,),
        in_specs=[pl.BlockSpec((N_DEV, tr, c), lambda i: (0, i, 0))],
        out_specs=pl.BlockSpec((tr, c), lambda i: (i, 0)),
        out_shape=jax.ShapeDtypeStruct((r, c), F32),
        compiler_params=pltpu.CompilerParams(dimension_semantics=("parallel",)),
    )(g)


def _adam_update(w, gg, m, v):
    c1 = 1.0 / (1.0 - ADAM_B1 ** ADAM_STEP)
    c2 = 1.0 / (1.0 - ADAM_B2 ** ADAM_STEP)
    nm = ADAM_B1 * m + (1.0 - ADAM_B1) * gg
    nv = ADAM_B2 * v + (1.0 - ADAM_B2) * (gg * gg)
    return -ADAM_LR * ((nm * c1) / (jnp.sqrt(nv * c2) + ADAM_EPS) + ADAM_WD * w), nm, nv


def _adamw_reduce(me, recvs, owns, w, m, v, name, tr=128):
    nl, r, c = w.shape
    assert len(recvs) == nl and len(owns) == nl
    tr = min(tr, r)
    assert r % tr == 0
    nblk = r // tr

    def parked(li, l, i):
        return jnp.where(l < li, 0, jnp.where(l > li, nblk - 1, i))

    def recv_spec(li):
        return pl.BlockSpec((N_DEV, tr, c), lambda l, i, me_ref: (0, parked(li, l, i), 0))

    def own_spec(li):
        return pl.BlockSpec((None, tr, c), lambda l, i, me_ref: (me_ref[0], parked(li, l, i), 0))

    def body(me_ref, *refs):
        rrefs, orefs = refs[:nl], refs[nl:2 * nl]
        w_ref, m_ref, v_ref, g_ref, d_ref, nm_ref, nv_ref = refs[2 * nl:]
        l = pl.program_id(0)

        def of_layer(vals):
            out = vals[0]
            for li in range(1, nl):
                out = jnp.where(l == li, vals[li], out)
            return out

        own = of_layer([o[...].astype(F32) for o in orefs])
        gg = None
        for s in range(N_DEV):
            slot = jnp.where(me_ref[0] == s, own, of_layer([rr[s].astype(F32) for rr in rrefs]))
            gg = slot if gg is None else gg + slot
        g_ref[...] = gg
        d_ref[...], nm_ref[...], nv_ref[...] = _adam_update(w_ref[...], gg, m_ref[...], v_ref[...])

    spec = pl.BlockSpec((None, tr, c), lambda l, i, me_ref: (l, i, 0))
    return pl.pallas_call(
        body, name=name,
        grid_spec=pltpu.PrefetchScalarGridSpec(
            num_scalar_prefetch=1, grid=(nl, nblk),
            in_specs=[recv_spec(li) for li in range(nl)] + [own_spec(li) for li in range(nl)] + [spec] * 3,
            out_specs=[spec] * 4),
        out_shape=[jax.ShapeDtypeStruct((nl, r, c), F32)] * 4,
        compiler_params=pltpu.CompilerParams(dimension_semantics=("arbitrary", "arbitrary")),
    )(me, *recvs, *owns, w, m, v)


def _adamw(w, g, m, v, name, tr=256):
    r, c = w.shape
    tr = min(tr, r)
    assert r % tr == 0

    def body(w_ref, g_ref, m_ref, v_ref, d_ref, nm_ref, nv_ref):
        d_ref[...], nm_ref[...], nv_ref[...] = _adam_update(w_ref[...], g_ref[...], m_ref[...], v_ref[...])

    spec = pl.BlockSpec((tr, c), lambda i: (i, 0))
    return pl.pallas_call(
        body, name=name, grid=(r // tr,), in_specs=[spec] * 4, out_specs=[spec] * 3,
        out_shape=[jax.ShapeDtypeStruct((r, c), F32)] * 3,
        compiler_params=pltpu.CompilerParams(dimension_semantics=("parallel",)),
    )(w, g, m, v)


def _rms_fwd(x, g, name, ts=512, after=()):
    s, d = x.shape

    def fn(first, last, xv, gv):
        return [_rms(xv, gv[...])], []

    return _rowwise(fn, n_rows=s, ts=ts, name=name, rows=[(x, 0, d)], vecs=[g], row_outs=[(d, BF16)],
                    after=after)[0]


def _rms_bwd(x, dn, dres, g, name, ts=256):
    s, d = x.shape

    def fn(first, last, xv, dnv, drv, gv):
        _, vjp = jax.vjp(_rms, xv, gv[...])
        dx, dg = vjp(dnv.astype(F32))
        return [drv + dx], [dg]

    return _rowwise(fn, n_rows=s, ts=ts, name=name, rows=[(x, 0, d), (dn, 0, d), (dres, 0, d)], vecs=[g],
                    row_outs=[(d, F32)], acc_outs=[(1, d)])


def _loss_head(h, tgt, g, ts=256):
    s, d = h.shape

    def fn(first, last, hv, tv, gv):
        def f(hh, gg):
            e = _rms(hh, gg) - tv
            per_row = jnp.mean(e * e, axis=-1, keepdims=True)
            return 0.5 * jnp.sum(per_row, axis=0, keepdims=True)

        l, vjp = jax.vjp(f, hv, gv[...])
        dh, dg = vjp(jnp.ones((1, 1), F32))
        return [dh], [dg, jnp.zeros((1, LANES), F32) + l]

    return _rowwise(fn, n_rows=s, ts=ts, name="loss_head", rows=[(h, 0, d), (tgt, 0, d)], vecs=[g],
                    row_outs=[(d, F32)], acc_outs=[(1, d), (1, LANES)])


def _dn_pre_fwd(qkvz, ba, wconv, alog, dt, heads, ts=128):
    s = qkvz.shape[0]
    d3 = wconv.shape[1]
    d = d3 // 3

    def fn(first, last, xc, bav, xp, wv, av, dv):
        xext = jnp.concatenate([jnp.where(first, 0.0, xp), xc], axis=0)
        cv = _causal_conv(xext, wv, DN_CONV, DN_HALO - (DN_CONV - 1), xc.shape[0])
        return list(_dn_point(cv, bav, av[...], dv[...], heads)), []

    return _rowwise(fn, n_rows=s, ts=ts, name="dn_pre_fwd", rows=[(qkvz, 0, d3), (ba, 0, LANES)],
                    prevs=[(qkvz, 0, d3, DN_HALO)], vecs=[wconv, alog, dt],
                    row_outs=[(d, F32), (d, F32), (d, F32), (LANES, F32)])


def _dn_pre_bwd1(qkvz, ba, wconv, alog, dt, dq, dk, dv, dgb, heads, ts=128):
    s = qkvz.shape[0]
    d3 = wconv.shape[1]
    d = d3 // 3

    def fn(first, last, xc, bav, dqv, dkv, dvv, dgbv, xp, wv, av, dtv):
        xext = jnp.concatenate([jnp.where(first, 0.0, xp), xc], axis=0)
        cv = _causal_conv(xext, wv, DN_CONV, DN_HALO - (DN_CONV - 1), xc.shape[0])
        _, vjp = jax.vjp(functools.partial(_dn_point, heads=heads), cv, bav, av[...], dtv[...])
        dc, dba, da, ddt = vjp((dqv, dkv, dvv, dgbv))
        return [dc, dba], [da, ddt]

    return _rowwise(fn, n_rows=s, ts=ts, name="dn_pre_bwd1",
                    rows=[(qkvz, 0, d3), (ba, 0, LANES), (dq, 0, d), (dk, 0, d), (dv, 0, d), (dgb, 0, LANES)],
                    prevs=[(qkvz, 0, d3, DN_HALO)], vecs=[wconv, alog, dt],
                    row_outs=[(d3, F32), (LANES, BF16)], acc_outs=[(1, LANES), (1, LANES)])


def _dn_pre_bwd2(dc, qkvz, dz, wconv, ts=128):
    s = qkvz.shape[0]
    d3 = wconv.shape[1]
    d = d3 // 3

    def fn(first, last, dcc, xc, dzv, xp, dcn, wv):
        n = dcc.shape[0]
        dcext = jnp.concatenate([dcc, jnp.where(last, 0.0, dcn)], axis=0)
        xext = jnp.concatenate([jnp.where(first, 0.0, xp), xc], axis=0)
        dx = None
        dw = []
        for j in range(DN_CONV):
            term = _shift_rows(dcext, DN_CONV - 1 - j)[:n] * wv[j:j + 1, :]
            dx = term if dx is None else dx + term
            dw.append(_colsum(dcc * _shift_rows(xext, DN_HALO - (DN_CONV - 1) + j)[:n]))
        return [jnp.concatenate([dx, dzv], axis=-1)], [_stack_rows(dw, DN_CONV)]

    return _rowwise(fn, n_rows=s, ts=ts, name="dn_pre_bwd2",
                    rows=[(dc, 0, d3), (qkvz, 0, d3), (dz, 0, d)],
                    prevs=[(qkvz, 0, d3, DN_HALO)], nexts=[(dc, 0, d3, DN_HALO)], vecs=[wconv],
                    row_outs=[(4 * d, BF16)], acc_outs=[(DN_CONV, d3)])


def _dn_post_fwd(o, qkvz, onorm, heads, ts=256):
    s, d = o.shape

    def fn(first, last, ov, zv, nv):
        return [_dn_post(ov, zv, nv[...], heads)], []

    return _rowwise(fn, n_rows=s, ts=ts, name="dn_post_fwd", rows=[(o, 0, d), (qkvz, 3, d)], vecs=[onorm],
                    row_outs=[(d, BF16)])[0]


def _dn_post_bwd(o, qkvz, onorm, dog, heads, ts=256):
    s, d = o.shape

    def fn(first, last, ov, zv, dv, nv):
        _, vjp = jax.vjp(functools.partial(_dn_post, heads=heads), ov, zv, nv[...])
        do, dz, dn = vjp(dv.astype(F32))
        return [do, dz], [dn]

    return _rowwise(fn, n_rows=s, ts=ts, name="dn_post_bwd", rows=[(o, 0, d), (qkvz, 3, d), (dog, 0, d)],
                    vecs=[onorm], row_outs=[(d, F32), (d, F32)], acc_outs=[(1, DN_HEAD_DIM)])


def _cv_mid_fwd(u, wdw, bdw, lng, lnb, ts=256):
    s = u.shape[0]
    d = u.shape[1] // 2

    def fn(first, last, uc, up, wv, bv, gv, lbv):
        uext = jnp.concatenate([jnp.where(first, 0.0, up), uc], axis=0)
        glu = uext[:, :d] * _sigmoid(uext[:, d:])
        c = _causal_conv(glu, wv, CV_WIDTH, CV_HALO - (CV_WIDTH - 1), uc.shape[0]) + bv[...]
        return [c, _ln_silu(c, gv[...], lbv[...])], []

    return _rowwise(fn, n_rows=s, ts=ts, name="cv_mid_fwd", rows=[(u, 0, 2 * d)], prevs=[(u, 0, 2 * d, CV_HALO)],
                    vecs=[wdw, bdw, lng, lnb], row_outs=[(d, F32), (d, BF16)])


def _cv_mid_bwd1(c, ds, dhp, lng, lnb, ts=256):
    s, d = c.shape

    def fn(first, last, cv, dsv, dhv, gv, bv):
        _, vjp = jax.vjp(_ln_silu, cv, gv[...], bv[...])
        dc, dg, db = vjp(dsv.astype(F32))
        return [dc], [dg, db, _colsum(dc), _colsum(dhv)]

    return _rowwise(fn, n_rows=s, ts=ts, name="cv_mid_bwd1", rows=[(c, 0, d), (ds, 0, d), (dhp, 0, d)],
                    vecs=[lng, lnb], row_outs=[(d, F32)], acc_outs=[(1, d)] * 4)


def _cv_mid_bwd2(dc, u, wdw, ts=256):
    s, d = dc.shape

    def fn(first, last, dcc, uc, up, dcn, wv):
        n = dcc.shape[0]
        dcext = jnp.concatenate([dcc, jnp.where(last, 0.0, dcn)], axis=0)
        uext = jnp.concatenate([jnp.where(first, 0.0, up), uc], axis=0)
        glu = uext[:, :d] * _sigmoid(uext[:, d:])
        dglu = None
        dw = []
        for j in range(CV_WIDTH):
            term = _shift_rows(dcext, CV_WIDTH - 1 - j)[:n] * wv[j:j + 1, :]
            dglu = term if dglu is None else dglu + term
            dw.append(_colsum(dcc * _shift_rows(glu, CV_HALO - (CV_WIDTH - 1) + j)[:n]))
        u1, sg = uc[:, :d], _sigmoid(uc[:, d:])
        du = jnp.concatenate([dglu * sg, dglu * u1 * sg * (1.0 - sg)], axis=-1)
        return [du], [_stack_rows(dw, CV_HALO), _colsum(du)]

    return _rowwise(fn, n_rows=s, ts=ts, name="cv_mid_bwd2", rows=[(dc, 0, d), (u, 0, 2 * d)],
                    prevs=[(u, 0, 2 * d, CV_HALO)], nexts=[(dc, 0, d, CV_HALO)], vecs=[wdw],
                    row_outs=[(2 * d, BF16)], acc_outs=[(CV_HALO, d), (1, 2 * d)])


def _attn_fwd(q, k, v, name, ts=256):
    s, d = q.shape

    def fn(first, last, qv, kv, vv):
        return [_attn_tile(qv.astype(F32), kv[...].astype(F32), vv[...].astype(F32))], []

    return _rowwise(fn, n_rows=s, ts=ts, name=name, rows=[(q, 0, d)], vecs=[k, v], row_outs=[(d, BF16)])[0]


def _attn_bwd(q, k, v, do, name, ts=256):
    s, d = q.shape
    m = k.shape[0]

    def fn(first, last, qv, dov, kv, vv):
        _, vjp = jax.vjp(_attn_tile, qv.astype(F32), kv[...].astype(F32), vv[...].astype(F32))
        dq, dk, dv = vjp(dov.astype(F32))
        return [dq], [dk, dv]

    return _rowwise(fn, n_rows=s, ts=ts, name=name, rows=[(q, 0, d), (do, 0, d)], vecs=[k, v],
                    row_outs=[(d, BF16)], acc_outs=[(m, d), (m, d)])


def _pad_lanes(a, off=0):
    r, n = a.shape
    return jnp.pad(a, ((0, 0), (off, LANES - off - n)))


def _local_step(x, mem, tgt, w, fetch=None, emit=None, first_after=()):
    s, d = x.shape
    heads = d // DN_HEAD_DIM
    g = {}
    if fetch is None:
        fetch = lambda group, after: None
    if emit is None:
        emit = lambda group, grads: ()

    def add_res(acc, res):
        return (res + acc,)

    w_in = w["dn_w_in"][0]
    assert w_in.shape[1] == 4 * d + 2 * heads
    w_qkvz = w_in[:, :4 * d]
    w_ba = _pad_lanes(w_in[:, 4 * d:])
    dn_norm = w["dn_norm"]
    alog = _pad_lanes(w["dn_a_log"], heads)
    dtb = _pad_lanes(w["dn_dt_bias"], heads)
    wconv = w["dn_w_conv"][0]
    n0 = _rms_fwd(x, dn_norm, "dn_rms", after=first_after)
    qkvz = _matmul(n0, w_qkvz, "nn", [F32], name="dn_in_proj")
    ba = _matmul(n0, w_ba, "nn", [F32], name="dn_in_proj_ba")
    q, k, v, gb = _dn_pre_fwd(qkvz, ba, wconv, alog, dtb, heads)
    o, states, tms = _delta_fwd(q, k, v, gb, heads)
    og = _dn_post_fwd(o, qkvz, w["dn_out_norm"], heads)
    fetch(1, [og])
    h1 = _matmul(og, w["dn_w_out"][0], "nn", [F32], name="dn_out_proj", epi=add_res, mn_extras=[x])

    def xattn_fwd(h, layer):
        nq = _rms_fwd(h, w["xa_norm"][layer:layer + 1], f"xa{layer}_rms")
        qx = _matmul(nq, w["xa_w_q"][layer], "nn", [BF16], name=f"xa{layer}_q")
        mn = _rms_fwd(mem, w["xa_mem_norm"][layer:layer + 1], f"xa{layer}_mem_rms")
        kv = _matmul(mn, w["xa_w_kv"][layer], "nn", [BF16], name=f"xa{layer}_kv")
        kx, vx = kv[:, :d], kv[:, d:]
        ox = _attn_fwd(qx, kx, vx, f"xa{layer}_attn")
        hn = _matmul(ox, w["xa_w_o"][layer], "nn", [F32], name=f"xa{layer}_o", epi=add_res, mn_extras=[h])
        return hn, (h, nq, qx, mn, kx, vx, ox)

    mlp_tm = (512, 1024)

    def mlp_fwd(h, layer):
        nm = _rms_fwd(h, w["mlp_norm"][layer:layer + 1], f"mlp{layer}_rms")

        def epi(acc):
            r = jnp.maximum(acc, 0.0)
            return acc, r * r

        u, a = _matmul(nm, w["mlp_w_up"][layer], "nn", [BF16, BF16], name=f"mlp{layer}_up", epi=epi,
                       tm=mlp_tm[layer])
        hn = _matmul(a, w["mlp_w_down"][layer], "nn", [F32], name=f"mlp{layer}_down", epi=add_res,
                     mn_extras=[h], tk=2048, tm=mlp_tm[layer])
        return hn, (h, nm, u, a)

    h2, xa0 = xattn_fwd(h1, 0)
    h3, mlp0 = mlp_fwd(h2, 0)

    fetch(2, [h3])
    n1 = _rms_fwd(h3, w["cv_norm"], "cv_rms")
    u_cv = _matmul(n1, w["cv_w_pw1"][0], "nn", [F32], name="cv_pw1", epi=lambda acc, b: (acc + b,),
                   row_extras=[w["cv_b_pw1"]])
    wdw = jnp.pad(w["cv_w_dw"][0], ((0, CV_HALO - CV_WIDTH), (0, 0)))
    c_cv, s_cv = _cv_mid_fwd(u_cv, wdw, w["cv_b_dw"], w["cv_ln_g"], w["cv_ln_b"])
    h4 = _matmul(s_cv, w["cv_w_pw2"][0], "nn", [F32], name="cv_pw2",
                 epi=lambda acc, res, b: (res + acc + b,), mn_extras=[h3], row_extras=[w["cv_b_pw2"]])
    fetch(3, [h4])
    h5, xa1 = xattn_fwd(h4, 1)
    h6, mlp1 = mlp_fwd(h5, 1)

    fnorm = w["final_norm"].reshape(1, d)
    dh, g_fn, loss = _loss_head(h6, tgt, fnorm)
    g["final_norm"] = g_fn.reshape(d)

    def mlp_bwd(dh, layer, saved, after=()):
        h, nm, u, a = saved
        tm = mlp_tm[layer]
        du = _matmul(dh, w["mlp_w_down"][layer], "nt", [BF16], name=f"mlp{layer}_down_dx", after=after, tm=tm,
                     epi=lambda acc, uu: (acc * 2.0 * jnp.maximum(uu.astype(F32), 0.0),), mn_extras=[u])
        gdown = _matmul(a, dh, "tn", [BF16], name=f"mlp{layer}_down_dw", tm=tm)
        dn = _matmul(du, w["mlp_w_up"][layer], "nt", [F32], name=f"mlp{layer}_up_dx", tk=2048, tm=tm)
        gup = _matmul(nm, du, "tn", [BF16], name=f"mlp{layer}_up_dw", out_dm=True, tm=tm)
        dhn, gn = _rms_bwd(h, dn, dh, w["mlp_norm"][layer:layer + 1], f"mlp{layer}_rms_bwd")
        return dhn, gup, gdown, gn

    def xattn_bwd(dh, layer, saved):
        h, nq, qx, mn, kx, vx, ox = saved
        dox = _matmul(dh, w["xa_w_o"][layer], "nt", [BF16], name=f"xa{layer}_o_dx")
        go = _matmul(ox, dh, "tn", [BF16], name=f"xa{layer}_o_dw")
        dqx, dkx, dvx = _attn_bwd(qx, kx, vx, dox, f"xa{layer}_attn_bwd")
        dn = _matmul(dqx, w["xa_w_q"][layer], "nt", [F32], name=f"xa{layer}_q_dx")
        gq = _matmul(nq, dqx, "tn", [BF16], name=f"xa{layer}_q_dw")
        dkv = jnp.concatenate([dkx, dvx], axis=-1)
        gkv = _matmul(mn, dkv, "tn", [BF16], name=f"xa{layer}_kv_dw", out_dm=True)
        dmn = _matmul(dkv, w["xa_w_kv"][layer], "nt", [F32], name=f"xa{layer}_kv_dx", tk=2048)
        _, gmem = _rms_bwd(mem, dmn, dmn, w["xa_mem_norm"][layer:layer + 1], f"xa{layer}_mem_rms_bwd")
        dhn, gn = _rms_bwd(h, dn, dh, w["xa_norm"][layer:layer + 1], f"xa{layer}_rms_bwd")
        return dhn, gq, gkv, go, gn, gmem

    dh, gup1, gdown1, gmn1 = mlp_bwd(dh, 1, mlp1)
    dh, gq1, gkv1, go1, gxn1, gmem1 = xattn_bwd(dh, 1, xa1)
    g.update(mlp_w_up=[None, gup1], mlp_w_down=[None, gdown1], xa_w_q=[None, gq1], xa_w_kv=[None, gkv1],
             xa_w_o=[None, go1])
    tok = emit(3, g)

    ds_cv = _matmul(dh, w["cv_w_pw2"][0], "nt", [BF16], name="cv_pw2_dx", after=tok)
    g["cv_w_pw2"] = [_matmul(s_cv, dh, "tn", [BF16], name="cv_pw2_dw")]
    dc_cv, g_lng, g_lnb, g_bdw, g_b2 = _cv_mid_bwd1(c_cv, ds_cv, dh, w["cv_ln_g"], w["cv_ln_b"])
    du_cv, g_wdw, g_b1 = _cv_mid_bwd2(dc_cv, u_cv, wdw)
    dn1 = _matmul(du_cv, w["cv_w_pw1"][0], "nt", [F32], name="cv_pw1_dx", tk=2048)
    g["cv_w_pw1"] = [_matmul(n1, du_cv, "tn", [BF16], name="cv_pw1_dw", out_dm=True)]
    dh, g_cvn = _rms_bwd(h3, dn1, dh, w["cv_norm"], "cv_rms_bwd")
    g.update(cv_ln_g=g_lng, cv_ln_b=g_lnb, cv_b_dw=g_bdw, cv_b_pw2=g_b2, cv_b_pw1=g_b1, cv_norm=g_cvn,
             cv_w_dw=g_wdw[:CV_WIDTH][None])

    tok = emit(2, g)
    dh, gup0, gdown0, gmn0 = mlp_bwd(dh, 0, mlp0, after=tok)
    dh, gq0, gkv0, go0, gxn0, gmem0 = xattn_bwd(dh, 0, xa0)
    g["mlp_w_up"][0] = gup0
    g["mlp_w_down"][0] = gdown0
    g["mlp_norm"] = jnp.concatenate([gmn0, gmn1], axis=0)
    g["xa_w_q"][0] = gq0
    g["xa_w_kv"][0] = gkv0
    g["xa_w_o"][0] = go0
    g["xa_norm"] = jnp.concatenate([gxn0, gxn1], axis=0)
    g["xa_mem_norm"] = jnp.concatenate([gmem0, gmem1], axis=0)
    tok = emit(1, g)

    dog = _matmul(dh, w["dn_w_out"][0], "nt", [BF16], name="dn_out_proj_dx", after=tok)
    g["dn_w_out"] = [_matmul(og, dh, "tn", [BF16], name="dn_out_proj_dw")]
    do, dz, g_on = _dn_post_bwd(o, qkvz, w["dn_out_norm"], dog, heads)
    dq, dk, dv, dgb = _delta_bwd(q, k, v, gb, states, tms, do, heads)
    dc, dba, g_alog, g_dt = _dn_pre_bwd1(qkvz, ba, wconv, alog, dtb, dq, dk, dv, dgb, heads)
    dqkvz, g_wconv = _dn_pre_bwd2(dc, qkvz, dz, wconv)
    g_qkvz = _matmul(n0, dqkvz, "tn", [BF16], name="dn_in_proj_dw")
    g_ba = _matmul(n0, dba, "tn", [BF16], name="dn_in_proj_ba_dw")
    g["dn_w_in"] = [jnp.concatenate([g_qkvz, g_ba[:, :2 * heads]], axis=1)]
    g["dn_w_conv"] = g_wconv[None]
    tok = emit(0, g)
    dn0a = _matmul(dba, w_ba, "nt", [F32], name="dn_in_proj_ba_dx", after=tok)
    dn0 = _matmul(dqkvz, w_qkvz, "nt", [F32], name="dn_in_proj_dx", epi=add_res, mn_extras=[dn0a], tk=2048)
    grad_x, g_dnn = _rms_bwd(x, dn0, dh, dn_norm, "dn_rms_bwd")
    g.update(dn_norm=g_dnn, dn_out_norm=g_on,
             dn_a_log=g_alog[:, heads:2 * heads], dn_dt_bias=g_dt[:, heads:2 * heads])
    return loss, grad_x, g


def _round_up(n, m):
    return (n + m - 1) // m * m


def _pack_rows(parts, cols, row_mult):
    lead = parts[0].shape[:-1]
    flat, offs, off = [], [], 0
    for p in parts:
        n = _round_up(p.shape[-1], cols)
        flat.append(jnp.pad(p, [(0, 0)] * len(lead) + [(0, n - p.shape[-1])]))
        offs.append(off)
        off += n
    total = _round_up(off, cols * row_mult)
    if total > off:
        flat.append(jnp.zeros(lead + (total - off,), parts[0].dtype))
    return jnp.concatenate(flat, axis=-1).reshape(lead + (total // cols, cols)), offs


def _unpack(packed, offs, shapes):
    lead = packed.shape[:-2]
    flat = packed.reshape(lead + (-1,))
    out = []
    for off, shp in zip(offs, shapes):
        n = 1
        for v in shp:
            n *= v
        out.append(flat[..., off:off + n].reshape(lead + tuple(shp)))
    return out


def kernel(x, mem, dn_norm, dn_w_in, dn_w_conv, dn_a_log, dn_dt_bias, dn_out_norm, dn_w_out, cv_norm, cv_w_pw1, cv_b_pw1, cv_w_dw, cv_b_dw, cv_ln_g, cv_ln_b, cv_w_pw2, cv_b_pw2, xa_norm, xa_mem_norm, xa_w_q, xa_w_kv, xa_w_o, mlp_norm, mlp_w_up, mlp_w_down, final_norm, loss_target, m_dn_norm, m_dn_w_in, m_dn_w_conv, m_dn_a_log, m_dn_dt_bias, m_dn_out_norm, m_dn_w_out, m_cv_norm, m_cv_w_pw1, m_cv_b_pw1, m_cv_w_dw, m_cv_b_dw, m_cv_ln_g, m_cv_ln_b, m_cv_w_pw2, m_cv_b_pw2, m_xa_norm, m_xa_mem_norm, m_xa_w_q, m_xa_w_kv, m_xa_w_o, m_mlp_norm, m_mlp_w_up, m_mlp_w_down, m_final_norm, v_dn_norm, v_dn_w_in, v_dn_w_conv, v_dn_a_log, v_dn_dt_bias, v_dn_out_norm, v_dn_w_out, v_cv_norm, v_cv_w_pw1, v_cv_b_pw1, v_cv_w_dw, v_cv_b_dw, v_cv_ln_g, v_cv_ln_b, v_cv_w_pw2, v_cv_b_pw2, v_xa_norm, v_xa_mem_norm, v_xa_w_q, v_xa_w_kv, v_xa_w_o, v_mlp_norm, v_mlp_w_up, v_mlp_w_down, v_final_norm):
    wsh = dict(dn_norm=dn_norm, dn_w_in=dn_w_in, dn_w_conv=dn_w_conv, dn_a_log=dn_a_log, dn_dt_bias=dn_dt_bias, dn_out_norm=dn_out_norm, dn_w_out=dn_w_out, cv_norm=cv_norm, cv_w_pw1=cv_w_pw1, cv_b_pw1=cv_b_pw1, cv_w_dw=cv_w_dw, cv_b_dw=cv_b_dw, cv_ln_g=cv_ln_g, cv_ln_b=cv_ln_b, cv_w_pw2=cv_w_pw2, cv_b_pw2=cv_b_pw2, xa_norm=xa_norm, xa_mem_norm=xa_mem_norm, xa_w_q=xa_w_q, xa_w_kv=xa_w_kv, xa_w_o=xa_w_o, mlp_norm=mlp_norm, mlp_w_up=mlp_w_up, mlp_w_down=mlp_w_down, final_norm=final_norm)
    msh = dict(dn_norm=m_dn_norm, dn_w_in=m_dn_w_in, dn_w_conv=m_dn_w_conv, dn_a_log=m_dn_a_log, dn_dt_bias=m_dn_dt_bias, dn_out_norm=m_dn_out_norm, dn_w_out=m_dn_w_out, cv_norm=m_cv_norm, cv_w_pw1=m_cv_w_pw1, cv_b_pw1=m_cv_b_pw1, cv_w_dw=m_cv_w_dw, cv_b_dw=m_cv_b_dw, cv_ln_g=m_cv_ln_g, cv_ln_b=m_cv_ln_b, cv_w_pw2=m_cv_w_pw2, cv_b_pw2=m_cv_b_pw2, xa_norm=m_xa_norm, xa_mem_norm=m_xa_mem_norm, xa_w_q=m_xa_w_q, xa_w_kv=m_xa_w_kv, xa_w_o=m_xa_w_o, mlp_norm=m_mlp_norm, mlp_w_up=m_mlp_w_up, mlp_w_down=m_mlp_w_down, final_norm=m_final_norm)
    vsh = dict(dn_norm=v_dn_norm, dn_w_in=v_dn_w_in, dn_w_conv=v_dn_w_conv, dn_a_log=v_dn_a_log, dn_dt_bias=v_dn_dt_bias, dn_out_norm=v_dn_out_norm, dn_w_out=v_dn_w_out, cv_norm=v_cv_norm, cv_w_pw1=v_cv_w_pw1, cv_b_pw1=v_cv_b_pw1, cv_w_dw=v_cv_w_dw, cv_b_dw=v_cv_b_dw, cv_ln_g=v_cv_ln_g, cv_ln_b=v_cv_ln_b, cv_w_pw2=v_cv_w_pw2, cv_b_pw2=v_cv_b_pw2, xa_norm=v_xa_norm, xa_mem_norm=v_xa_mem_norm, xa_w_q=v_xa_w_q, xa_w_kv=v_xa_w_kv, xa_w_o=v_xa_w_o, mlp_norm=v_mlp_norm, mlp_w_up=v_mlp_w_up, mlp_w_down=v_mlp_w_down, final_norm=v_final_norm)

    big_axis = dict(BIG)

    small_pack, small_offs = _pack_rows([wsh[nm].reshape(-1) for nm in SMALL_SH], LANES, 8)
    w = {nm: [None] * wsh[nm].shape[0] for nm in big_axis}

    def put_weights(group, gathered):
        for (nm, layer), gth in zip(group, gathered):
            if nm == "dn_w_in":
                w[nm][layer] = gth.transpose(1, 0, 2).reshape(gth.shape[1], N_DEV * gth.shape[2])
            elif big_axis[nm] == 1:
                w[nm][layer] = gth.reshape(N_DEV * gth.shape[1], gth.shape[2])
            else:
                w[nm][layer] = gth

    first = _all_gather([wsh[nm][layer].astype(BF16) for nm, layer in GATHER_GROUPS[0]] + [small_pack],
                        "weights_all_gather_0")
    put_weights(GATHER_GROUPS[0], first)
    me = _dev_index(*_mesh_me())
    gather_handles, tokens = {}, []
    for gi in range(1, len(GATHER_GROUPS)):
        shards = [wsh[nm][layer].astype(BF16) for nm, layer in GATHER_GROUPS[gi]]
        lands = [lax.dynamic_update_slice(lax.empty((N_DEV,) + s.shape, s.dtype), s[None], (me, 0, 0))
                 for s in shards]
        gather_handles[gi], tok = _exchange_start(shards, lands, False, f"weights_gather_{gi}",
                                                  after=[first[-1]] + tokens)
        tokens.append(tok)
    for nm, gth in zip(SMALL_SH, _unpack(first[-1], small_offs, [wsh[nm].shape for nm in SMALL_SH])):
        w[nm] = jnp.moveaxis(gth, 0, -2).reshape(gth.shape[1:-1] + (N_DEV * gth.shape[-1],))
    for nm in REPL:
        w[nm] = wsh[nm]

    def fetch(gi, after):
        put_weights(GATHER_GROUPS[gi], _exchange_wait(gather_handles[gi], after)[1])

    scatter_handles = {}

    def emit(gi, g):
        blocks = []
        for nm, layer in SCATTER_GROUPS[gi]:
            gw = g[nm][layer]
            if nm == "dn_w_in":
                gw = gw.reshape(gw.shape[0], N_DEV, -1).transpose(1, 0, 2)
            elif big_axis[nm] == 1:
                gw = gw.reshape(N_DEV, gw.shape[0] // N_DEV, gw.shape[1])
            blocks.append(gw)
        if gi == 0:
            gsmall_pack, _ = _pack_rows(
                [jnp.moveaxis(g[nm].reshape(g[nm].shape[:-1] + (N_DEV, -1)), -2, 0).reshape(N_DEV, -1)
                 for nm in SMALL_SH], LANES, 8)
            blocks.append(gsmall_pack)
        lands = [lax.empty(b.shape, b.dtype) for b in blocks]
        scatter_handles[gi], tok = _exchange_start(blocks, lands, True, f"grads_scatter_{gi}")
        return [tok]

    loss_part, grad_x, g = _local_step(x[0], mem[0], loss_target[0], w, fetch, emit, tokens)

    recv = {nm: [None] * wsh[nm].shape[0] for nm in big_axis}
    sent = {nm: [None] * wsh[nm].shape[0] for nm in big_axis}
    gsh, delta, new_m, new_v = {}, {}, {}, {}
    after = [grad_x]
    done = set()
    me_arr = me.astype(jnp.int32).reshape(1)
    for gi in reversed(range(len(SCATTER_GROUPS))):
        sources, landed = _exchange_wait(scatter_handles[gi], after)
        for (nm, layer), src, r in zip(SCATTER_GROUPS[gi], sources, landed):
            sent[nm][layer], recv[nm][layer] = src, r
        if gi == 0:
            slot = lax.broadcasted_iota(jnp.int32, landed[-1].shape, 0)
            rsmall = jnp.where(slot == me, sources[-1], landed[-1])
        for nm in big_axis:
            if nm not in done and all(r is not None for r in recv[nm]):
                gsh[nm], delta[nm], new_m[nm], new_v[nm] = _adamw_reduce(
                    me_arr, recv[nm], sent[nm], wsh[nm], msh[nm], vsh[nm], f"adamw_{nm}")
                done.add(nm)
                after = [delta[nm]]
    gsmall_red = _slot_sum(rsmall, "grads_small_sum", 512)
    repl_pack, repl_offs = _pack_rows([g[nm].reshape(-1) for nm in REPL], LANES, 8)
    (repl_all,) = _all_gather([repl_pack], "repl_grads_all_gather")
    repl_red = _slot_sum(repl_all, "repl_grads_sum", 512)
    for nm, val in zip(SMALL_SH, _unpack(gsmall_red, small_offs, [wsh[nm].shape for nm in SMALL_SH])):
        gsh[nm] = val
    for nm, val in zip(REPL, _unpack(repl_red, repl_offs, [wsh[nm].shape for nm in REPL])):
        gsh[nm] = val
    small_names = list(SMALL_SH) + list(REPL)
    packs = []
    for src in (wsh, gsh, msh, vsh):
        pk, sm_offs = _pack_rows([src[nm].reshape(-1) for nm in small_names], LANES, 8)
        packs.append(pk)
    outs = _adamw(*packs, "adamw_small")
    for dst, pk in zip((delta, new_m, new_v), outs):
        for nm, val in zip(small_names, _unpack(pk, sm_offs, [wsh[nm].shape for nm in small_names])):
            dst[nm] = val

    loss = lax.psum(loss_part[0, 0], ("x", "y", "c"))
    return (loss, grad_x[None], *[gsh[nm] for nm in WEIGHTS], *[delta[nm] for nm in WEIGHTS],
            *[new_m[nm] for nm in WEIGHTS], *[new_v[nm] for nm in WEIGHTS])
```

```python
import functools

import jax
import jax.numpy as jnp
from jax import lax
from jax.experimental import pallas as pl
from jax.experimental.pallas import tpu as pltpu

F32 = jnp.float32
BF16 = jnp.bfloat16
HP = lax.Precision.HIGHEST
MESH_IDS = pl.DeviceIdType.MESH

N_DEV = 8
LANES = 128
RMS_EPS = 1e-6
LN_EPS = 1e-5
DN_HEAD_DIM = 128
DN_CONV = 4
DN_CHUNK = 64
CV_WIDTH = 31
XA_HEADS = 4
DN_HALO = 8
CV_HALO = 32

ADAM_LR = 0.001
ADAM_B1 = 0.9
ADAM_B2 = 0.999
ADAM_EPS = 1e-08
ADAM_WD = 0.01
ADAM_STEP = 10

BIG = (("dn_w_in", 2), ("dn_w_out", 1), ("cv_w_pw1", 2), ("cv_w_pw2", 1), ("xa_w_q", 1), ("xa_w_kv", 2),
       ("xa_w_o", 1), ("mlp_w_up", 2), ("mlp_w_down", 1))
_LAYER_GROUP = ("xa_w_q", "xa_w_o", "mlp_w_down", "xa_w_kv", "mlp_w_up")
GATHER_GROUPS = (
    (("dn_w_in", 0),),
    (("dn_w_out", 0),) + tuple((nm, 0) for nm in _LAYER_GROUP),
    (("cv_w_pw2", 0), ("cv_w_pw1", 0)),
    tuple((nm, 1) for nm in _LAYER_GROUP),
)
SCATTER_GROUPS = (
    (("dn_w_out", 0), ("dn_w_in", 0)),
    tuple((nm, 0) for nm in _LAYER_GROUP),
    (("cv_w_pw2", 0), ("cv_w_pw1", 0)),
    tuple((nm, 1) for nm in _LAYER_GROUP),
)
SMALL_SH = ("cv_norm", "cv_b_pw1", "cv_b_dw", "cv_ln_g", "cv_ln_b", "cv_b_pw2", "cv_w_dw", "dn_w_conv")
REPL = ("dn_norm", "dn_a_log", "dn_dt_bias", "dn_out_norm", "xa_norm", "xa_mem_norm", "mlp_norm", "final_norm")
WEIGHTS = ("dn_norm", "dn_w_in", "dn_w_conv", "dn_a_log", "dn_dt_bias", "dn_out_norm", "dn_w_out", "cv_norm",
           "cv_w_pw1", "cv_b_pw1", "cv_w_dw", "cv_b_dw", "cv_ln_g", "cv_ln_b", "cv_w_pw2", "cv_b_pw2", "xa_norm",
           "xa_mem_norm", "xa_w_q", "xa_w_kv", "xa_w_o", "mlp_norm", "mlp_w_up", "mlp_w_down", "final_norm")


def _dot_dims(mode, batched):
    o = 1 if batched else 0
    contract = {"nn": ((1 + o,), (o,)), "nt": ((1 + o,), (1 + o,)), "tn": ((o,), (o,))}[mode]
    return (contract, (((0,), (0,)) if batched else ((), ())))


def _bdot(a, b, mode):
    return lax.dot_general(a.astype(BF16), b.astype(BF16), _dot_dims(mode, a.ndim == 3),
                           preferred_element_type=F32)


@functools.partial(jax.custom_vjp, nondiff_argnums=(2,))
def _mm(a, b, mode):
    return _bdot(a, b, mode)


def _mm_fwd(a, b, mode):
    return _bdot(a, b, mode), (a, b)


def _mm_bwd(mode, res, ct):
    a, b = res
    if mode == "nn":
        da, db = _bdot(ct, b, "nt"), _bdot(a, ct, "tn")
    elif mode == "nt":
        da, db = _bdot(ct, b, "nn"), _bdot(ct, a, "tn")
    else:
        da, db = _bdot(b, ct, "nt"), _bdot(a, ct, "nn")
    return da.astype(a.dtype), db.astype(b.dtype)


_mm.defvjp(_mm_fwd, _mm_bwd)


def _sigmoid(x):
    return 0.5 * (jnp.tanh(0.5 * x) + 1.0)


def _silu(x):
    return x * _sigmoid(x)


def _softplus(x):
    return jnp.maximum(x, 0.0) + jnp.log(1.0 + jnp.exp(-jnp.abs(x)))


def _rms(x, g):
    r = lax.rsqrt(jnp.mean(x * x, axis=-1, keepdims=True) + RMS_EPS)
    return x * r * g


def _shift_rows(x, off):
    if off == 0:
        return x
    return pltpu.roll(x, x.shape[0] - off, 0)


def _series_dot(a, b, mode):
    return _bdot(a, b, mode)


def _chunk_masks(c):
    ii = lax.broadcasted_iota(jnp.int32, (c, c), 0)
    jj = lax.broadcasted_iota(jnp.int32, (c, c), 1)
    return (ii == jj).astype(F32), ii >= jj, ii > jj


def _neumann_inverse(lm):
    n = lm.shape[-1]
    t = -lm
    p = lm
    size = 2
    while size < n:
        size *= 2
        p = _series_dot(p, p, "nn")
        t = t + p + _series_dot(t, p, "nn")
    return t


def _apply_inverse(tm, rhs, mode):
    return rhs + _series_dot(tm, rhs, mode)


@jax.custom_vjp
def _unit_lower_solve(lm, rhs, tm):
    return _apply_inverse(tm, rhs, "nn")


def _uls_fwd(lm, rhs, tm):
    sol = _apply_inverse(tm, rhs, "nn")
    return sol, (tm, sol)


def _uls_bwd(res, ct):
    tm, sol = res
    d_rhs = _apply_inverse(tm, ct, "tn")
    return -_bdot(d_rhs, sol, "nt"), d_rhs, jnp.zeros_like(tm)


_unit_lower_solve.defvjp(_uls_fwd, _uls_bwd)


def _delta_chunk(q, k, v, gcol, bcol, s0, tm=None):
    c = q.shape[1]
    eye, causal, strict = _chunk_masks(c)
    grow = jnp.sum(eye * gcol, axis=1, keepdims=True)
    gc = jnp.sum(jnp.where(causal, grow, 0.0), axis=2, keepdims=True)
    gc_row = jnp.sum(eye * gc, axis=1, keepdims=True)
    decay = jnp.exp(jnp.where(causal, gc - gc_row, -jnp.inf))
    kb = k * bcol
    lm = jnp.where(strict, _mm(kb, k, "nt") * decay, 0.0)
    if tm is None:
        tm = _neumann_inverse(lax.stop_gradient(lm))
    egc = jnp.exp(gc)
    rhs = jnp.concatenate([v * bcol, kb * egc], axis=-1)
    sol = _unit_lower_solve(lm, rhs, tm)
    dv_ = v.shape[-1]
    u, w = sol[..., :dv_], sol[..., dv_:]
    attn = _mm(q, k, "nt") * decay
    qd = q * egc
    gl = jnp.sum(grow, axis=2, keepdims=True)
    kd = k * jnp.exp(gl - gc)
    v_new = u - _mm(w, s0, "nn")
    o = _mm(qd, s0, "nn") + _mm(attn, v_new, "nn")
    s1 = s0 * jnp.exp(gl) + _mm(kd, v_new, "tn")
    return o, s1, tm


def _dn_point(cv, ba, alog, dt, heads):
    a = _silu(cv)
    d = cv.shape[1] // 3
    qs, ks = [], []
    for h in range(heads):
        qh = a[:, h * DN_HEAD_DIM:(h + 1) * DN_HEAD_DIM]
        qs.append(qh * lax.rsqrt(jnp.sum(qh * qh, axis=-1, keepdims=True) + 1e-6) * (DN_HEAD_DIM ** -0.5))
        kh = a[:, d + h * DN_HEAD_DIM:d + (h + 1) * DN_HEAD_DIM]
        ks.append(kh * lax.rsqrt(jnp.sum(kh * kh, axis=-1, keepdims=True) + 1e-6))
    q = jnp.concatenate(qs, axis=-1)
    k = jnp.concatenate(ks, axis=-1)
    v = a[:, 2 * d:]
    lane = lax.broadcasted_iota(jnp.int32, ba.shape, 1)
    beta = _sigmoid(ba)
    g = -jnp.exp(alog) * _softplus(ba + dt)
    gb = jnp.where(lane < heads, beta, jnp.where(lane < 2 * heads, g, 0.0))
    return q, k, v, gb


def _dn_post(o, z, onorm, heads):
    outs = []
    for h in range(heads):
        oh = o[:, h * DN_HEAD_DIM:(h + 1) * DN_HEAD_DIM]
        outs.append(oh * lax.rsqrt(jnp.mean(oh * oh, axis=-1, keepdims=True) + RMS_EPS) * onorm)
    return jnp.concatenate(outs, axis=-1) * _silu(z)


def _attn_tile(q, k, v):
    hd = q.shape[1] // XA_HEADS
    outs = []
    for h in range(XA_HEADS):
        sl = slice(h * hd, (h + 1) * hd)
        s = _mm(q[:, sl], k[:, sl], "nt") * (hd ** -0.5)
        m = lax.stop_gradient(jnp.max(s, axis=-1, keepdims=True))
        e = jnp.exp(s - m)
        p = e / jnp.sum(e, axis=-1, keepdims=True)
        outs.append(_mm(p, v[:, sl], "nn"))
    return jnp.concatenate(outs, axis=-1)


def _ln_silu(c, g, b):
    mu = jnp.mean(c, axis=-1, keepdims=True)
    xc = c - mu
    y = xc * lax.rsqrt(jnp.mean(xc * xc, axis=-1, keepdims=True) + LN_EPS)
    return _silu(y * g + b)


def _causal_conv(xext, w, width, lead, ts):
    acc = None
    for j in range(width):
        term = _shift_rows(xext, lead + j)[:ts] * w[j:j + 1, :]
        acc = term if acc is None else acc + term
    return acc


def _colsum(x):
    return jnp.sum(x, axis=0, keepdims=True)


def _stack_rows(rows, n_rows):
    c = rows[0].shape[1]
    ridx = lax.broadcasted_iota(jnp.int32, (n_rows, c), 0)
    out = jnp.zeros((n_rows, c), F32)
    for j, r in enumerate(rows):
        out = out + jnp.where(ridx == j, r, 0.0)
    return out


def _matmul(a, b, mode, out_dtypes, *, name, epi=None, mn_extras=(), row_extras=(), out_dm=False, after=(),
            tm=1024, tn=1024, tk=1024):
    b_dm = b.ndim == 3
    b_shape = (b.shape[1], N_DEV * b.shape[2]) if b_dm else b.shape
    if mode == "nn":
        (m, k), (k2, n) = a.shape, b_shape
    elif mode == "nt":
        (m, k), (n, k2) = a.shape, b_shape
    else:
        (k, m), (k2, n) = a.shape, b_shape
    assert k == k2, (a.shape, b.shape, mode)
    tm, tn, tk = min(tm, m), min(tn, n), min(tk, k)
    cb, nb = 0, 1
    if b_dm:
        assert mode in ("nn", "nt")
        cb = b.shape[2]
        nb = max(1, (tn if mode == "nn" else tk) // cb)
        if mode == "nn":
            tn = nb * cb
        else:
            tk = nb * cb
    co, no = 0, 1
    if out_dm:
        co = n // N_DEV
        no = max(1, tn // co)
        tn = no * co
    assert m % tm == 0 and n % tn == 0 and k % tk == 0, (m, n, k, tm, tn, tk)
    nk = k // tk
    if mode == "tn":
        a_spec = pl.BlockSpec((tk, tm), lambda j, i, kk: (kk, i))
    else:
        a_spec = pl.BlockSpec((tm, tk), lambda j, i, kk: (i, kk))
    if b_dm:
        b_spec = (pl.BlockSpec((nb, tn, cb), lambda j, i, kk: (kk, j, 0)) if mode == "nt"
                  else pl.BlockSpec((nb, tk, cb), lambda j, i, kk: (j, kk, 0)))
    else:
        b_spec = (pl.BlockSpec((tn, tk), lambda j, i, kk: (j, kk)) if mode == "nt"
                  else pl.BlockSpec((tk, tn), lambda j, i, kk: (kk, j)))
    mn_spec = pl.BlockSpec((tm, tn), lambda j, i, kk: (i, j))
    row_spec = pl.BlockSpec((1, tn), lambda j, i, kk: (0, j))
    n_extra = len(mn_extras) + len(row_extras)
    n_out = len(out_dtypes)
    in_specs = ([a_spec, b_spec] + [mn_spec] * len(mn_extras) + [row_spec] * len(row_extras)
                + [_ANY_SPEC] * len(after))
    args = [a, b, *mn_extras, *row_extras, *after]
    if out_dm:
        out_specs = [pl.BlockSpec((no, tm, co), lambda j, i, kk: (j, i, 0))] * n_out
        out_shape = [jax.ShapeDtypeStruct((N_DEV, m, co), dt) for dt in out_dtypes]
    else:
        out_specs = [mn_spec] * n_out
        out_shape = [jax.ShapeDtypeStruct((m, n), dt) for dt in out_dtypes]
    n_in = len(args)

    def dot(a_ref, b_ref):
        if not b_dm:
            return _bdot(a_ref[...], b_ref[...], mode)
        if mode == "nn":
            parts = [_bdot(a_ref[...], b_ref[dd], "nn") for dd in range(nb)]
            return parts[0] if nb == 1 else jnp.concatenate(parts, axis=1)
        out = None
        for dd in range(nb):
            part = _bdot(a_ref[:, dd * cb:(dd + 1) * cb], b_ref[dd], "nt")
            out = part if out is None else out + part
        return out

    def finish(acc_val, extras, outs):
        vals = (acc_val,) if epi is None else epi(acc_val, *[e[...] for e in extras])
        for o_ref, val in zip(outs, vals):
            if out_dm:
                for dd in range(no):
                    o_ref[dd] = val[:, dd * co:(dd + 1) * co].astype(o_ref.dtype)
            else:
                o_ref[...] = val.astype(o_ref.dtype)

    def body_one_step(*refs):
        finish(dot(refs[0], refs[1]), refs[2:2 + n_extra], refs[n_in:n_in + n_out])

    def body(*refs):
        a_ref, b_ref = refs[0], refs[1]
        acc = refs[-1]
        kk = pl.program_id(2)

        @pl.when(kk == 0)
        def _():
            acc[...] = jnp.zeros_like(acc)

        acc[...] += dot(a_ref, b_ref)

        @pl.when(kk == nk - 1)
        def _():
            finish(acc[...], refs[2:2 + n_extra], refs[n_in:n_in + n_out])

    res = pl.pallas_call(
        body_one_step if nk == 1 else body, name=name,
        grid=(n // tn, m // tm, nk),
        in_specs=in_specs, out_specs=out_specs, out_shape=out_shape,
        scratch_shapes=[] if nk == 1 else [pltpu.VMEM((tm, tn), F32)],
        compiler_params=pltpu.CompilerParams(dimension_semantics=("parallel", "parallel", "arbitrary")),
    )(*args)
    return res[0] if n_out == 1 else res


def _rowwise(fn, *, n_rows, ts, name, rows=(), prevs=(), nexts=(), vecs=(), row_outs=(), acc_outs=(), after=()):
    ts = min(ts, n_rows)
    assert n_rows % ts == 0
    nblk = n_rows // ts
    in_specs, args = [], []
    for arr, cb, w in rows:
        in_specs.append(pl.BlockSpec((ts, w), functools.partial(lambda i, cb: (i, cb), cb=cb)))
        args.append(arr)
    for arr, cb, w, halo in prevs:
        per = ts // halo
        in_specs.append(pl.BlockSpec(
            (halo, w), functools.partial(lambda i, cb, per: (jnp.maximum(i * per - 1, 0), cb), cb=cb, per=per)))
        args.append(arr)
    for arr, cb, w, halo in nexts:
        per = ts // halo
        last_blk = n_rows // halo - 1
        in_specs.append(pl.BlockSpec(
            (halo, w), functools.partial(lambda i, cb, per, lb: (jnp.minimum((i + 1) * per, lb), cb),
                                         cb=cb, per=per, lb=last_blk)))
        args.append(arr)
    for arr in vecs:
        in_specs.append(pl.BlockSpec(arr.shape, functools.partial(lambda i, nd: (0,) * nd, nd=arr.ndim)))
        args.append(arr)
    out_specs, out_shape = [], []
    for w, dt in row_outs:
        out_specs.append(pl.BlockSpec((ts, w), lambda i: (i, 0)))
        out_shape.append(jax.ShapeDtypeStruct((n_rows, w), dt))
    for shp in acc_outs:
        out_specs.append(pl.BlockSpec(shp, functools.partial(lambda i, nd: (0,) * nd, nd=len(shp))))
        out_shape.append(jax.ShapeDtypeStruct(shp, F32))
    n_used = len(args)
    n_tiles = n_used - len(vecs)
    in_specs += [_ANY_SPEC] * len(after)
    args += list(after)
    n_in, n_ro, n_acc = len(args), len(row_outs), len(acc_outs)

    def body(*refs):
        ins, ro, ac = refs[:n_used], refs[n_in:n_in + n_ro], refs[n_in + n_ro:]
        i = pl.program_id(0)
        rvals, avals = fn(i == 0, i == nblk - 1, *[r[...] for r in ins[:n_tiles]], *ins[n_tiles:])
        for r, val in zip(ro, rvals):
            r[...] = val.astype(r.dtype)
        if n_acc:
            @pl.when(i == 0)
            def _():
                for r in ac:
                    r[...] = jnp.zeros_like(r)

            for r, val in zip(ac, avals):
                r[...] += val

    res = pl.pallas_call(
        body, name=name, grid=(nblk,), in_specs=in_specs, out_specs=out_specs, out_shape=out_shape,
        compiler_params=pltpu.CompilerParams(dimension_semantics=("arbitrary",)),
    )(*args)
    return res


def _delta_fwd(q, k, v, gb, heads):
    s, hd = q.shape
    n = s // DN_CHUNK
    blk = pl.BlockSpec((DN_CHUNK, hd), lambda c: (c, 0))
    gspec = pl.BlockSpec((DN_CHUNK, LANES), lambda c: (c, 0))

    def body(q_ref, k_ref, v_ref, gb_ref, o_ref, st_ref, tm_ref, state):
        @pl.when(pl.program_id(0) == 0)
        def _():
            state[...] = jnp.zeros_like(state)

        s0 = state[...]
        st_ref[0] = s0
        o, s1, tm = _delta_chunk(*_split_heads(q_ref, k_ref, v_ref, gb_ref[...], heads), s0)
        for h in range(heads):
            o_ref[:, h * DN_HEAD_DIM:(h + 1) * DN_HEAD_DIM] = o[h]
        state[...] = s1
        tm_ref[0] = tm

    return pl.pallas_call(
        body, name="dn_delta_fwd", grid=(n,),
        in_specs=[blk, blk, blk, gspec],
        out_specs=[blk, pl.BlockSpec((1, heads, DN_HEAD_DIM, DN_HEAD_DIM), lambda c: (c, 0, 0, 0)),
                   pl.BlockSpec((1, heads, DN_CHUNK, DN_CHUNK), lambda c: (c, 0, 0, 0))],
        out_shape=[jax.ShapeDtypeStruct((s, hd), F32),
                   jax.ShapeDtypeStruct((n, heads, DN_HEAD_DIM, DN_HEAD_DIM), F32),
                   jax.ShapeDtypeStruct((n, heads, DN_CHUNK, DN_CHUNK), F32)],
        scratch_shapes=[pltpu.VMEM((heads, DN_HEAD_DIM, DN_HEAD_DIM), F32)],
        compiler_params=pltpu.CompilerParams(dimension_semantics=("arbitrary",)),
    )(q, k, v, gb)


def _split_heads(q_ref, k_ref, v_ref, gbv, heads):
    def hs(ref):
        return jnp.stack([ref[:, h * DN_HEAD_DIM:(h + 1) * DN_HEAD_DIM] for h in range(heads)])

    gcol = jnp.stack([gbv[:, heads + h:heads + h + 1] for h in range(heads)])
    bcol = jnp.stack([gbv[:, h:h + 1] for h in range(heads)])
    return hs(q_ref), hs(k_ref), hs(v_ref), gcol, bcol


def _delta_bwd(q, k, v, gb, states, tms, do, heads):
    s, hd = q.shape
    n = s // DN_CHUNK
    blk = pl.BlockSpec((DN_CHUNK, hd), lambda c: (n - 1 - c, 0))
    gspec = pl.BlockSpec((DN_CHUNK, LANES), lambda c: (n - 1 - c, 0))
    sspec = pl.BlockSpec((1, heads, DN_HEAD_DIM, DN_HEAD_DIM), lambda c: (n - 1 - c, 0, 0, 0))
    tspec = pl.BlockSpec((1, heads, DN_CHUNK, DN_CHUNK), lambda c: (n - 1 - c, 0, 0, 0))

    def body(q_ref, k_ref, v_ref, gb_ref, st_ref, tm_ref, do_ref, dq_ref, dk_ref, dv_ref, dgb_ref, dstate):
        @pl.when(pl.program_id(0) == 0)
        def _():
            dstate[...] = jnp.zeros_like(dstate)

        gbv = gb_ref[...]
        tm = tm_ref[0]

        def chunk(qh, kh, vh, gcol, bcol, s0):
            return _delta_chunk(qh, kh, vh, gcol, bcol, s0, tm)[:2]

        _, vjp = jax.vjp(chunk, *_split_heads(q_ref, k_ref, v_ref, gbv, heads), st_ref[0])
        doh = jnp.stack([do_ref[:, h * DN_HEAD_DIM:(h + 1) * DN_HEAD_DIM] for h in range(heads)])
        dq, dk, dv, dg, db, ds0 = vjp((doh, dstate[...]))
        dstate[...] = ds0
        lane = lax.broadcasted_iota(jnp.int32, gbv.shape, 1)
        dgb = jnp.zeros(gbv.shape, F32)
        for h in range(heads):
            sl = slice(h * DN_HEAD_DIM, (h + 1) * DN_HEAD_DIM)
            dq_ref[:, sl] = dq[h]
            dk_ref[:, sl] = dk[h]
            dv_ref[:, sl] = dv[h]
            dgb = dgb + jnp.where(lane == h, db[h], 0.0) + jnp.where(lane == heads + h, dg[h], 0.0)
        dgb_ref[...] = dgb

    return pl.pallas_call(
        body, name="dn_delta_bwd", grid=(n,),
        in_specs=[blk, blk, blk, gspec, sspec, tspec, blk],
        out_specs=[blk, blk, blk, gspec],
        out_shape=[jax.ShapeDtypeStruct((s, hd), F32)] * 3 + [jax.ShapeDtypeStruct((s, LANES), F32)],
        scratch_shapes=[pltpu.VMEM((heads, DN_HEAD_DIM, DN_HEAD_DIM), F32)],
        compiler_params=pltpu.CompilerParams(dimension_semantics=("arbitrary",)),
    )(q, k, v, gb, states, tms, do)


def _dev_index(px, py, pc):
    return 4 * px + 2 * py + pc


def _all_gather(arrs, name):
    n = len(arrs)

    def body(*refs):
        xs, outs = refs[:n], refs[n:2 * n]
        send_sems, recv_sems, local_sems = refs[2 * n:]
        x, y, c = lax.axis_index("x"), lax.axis_index("y"), lax.axis_index("c")
        me, sibling = (x, y, c), (x, y, 1 - c)
        chips = [(1 - x, y), (x, 1 - y), (1 - x, 1 - y)]

        def copy(a, kk, block, to, src=None):
            dst = outs[a].at[_dev_index(*block)]
            return pltpu.make_async_remote_copy(
                src_ref=dst if src is None else src, dst_ref=dst,
                send_sem=send_sems.at[a * 7 + kk], recv_sem=recv_sems.at[a * 7 + kk],
                device_id=to, device_id_type=MESH_IDS)

        mine = [pltpu.make_async_copy(xs[a], outs[a].at[_dev_index(*me)], local_sems.at[a]) for a in range(n)]
        for cp in mine:
            cp.start()
        first = []
        for a in range(n):
            first.append(copy(a, 0, me, sibling, src=xs[a]))
            first += [copy(a, 1 + j, me, (*chip, c), src=xs[a]) for j, chip in enumerate(chips)]
        for cp in first:
            cp.start()
        passed = []
        for j, chip in enumerate(chips):
            for a in range(n):
                copy(a, 1 + j, (*chip, c), me).wait_recv()
                fwd = copy(a, 4 + j, (*chip, c), sibling)
                fwd.start()
                passed.append(fwd)
        for a in range(n):
            copy(a, 0, sibling, me).wait_recv()
        for j, chip in enumerate(chips):
            for a in range(n):
                copy(a, 4 + j, (*chip, 1 - c), me).wait_recv()
        for cp in first + passed:
            cp.wait_send()
        for cp in mine:
            cp.wait()

    hbm = pl.BlockSpec(memory_space=pltpu.HBM)
    res = pl.pallas_call(
        body, name=name,
        in_specs=[hbm] * n, out_specs=[hbm] * n,
        out_shape=[jax.ShapeDtypeStruct((N_DEV,) + a.shape, a.dtype) for a in arrs],
        scratch_shapes=[pltpu.SemaphoreType.DMA((7 * n,)), pltpu.SemaphoreType.DMA((7 * n,)),
                        pltpu.SemaphoreType.DMA((n,))],
    )(*arrs)
    return list(res)


_FLIPS = ((0, 0, 1), (1, 0, 0), (0, 1, 0), (1, 1, 0), (1, 0, 1), (0, 1, 1), (1, 1, 1))
_HBM_SPEC = pl.BlockSpec(memory_space=pltpu.HBM)
_SEM_SPEC = pl.BlockSpec(memory_space=pltpu.SEMAPHORE)
_ANY_SPEC = pl.BlockSpec(memory_space=pl.ANY)
_DATAFLOW = pltpu.SideEffectType.DATAFLOW_SIDE_EFFECTING
TOKEN_SHAPE = (8, LANES)


def _mesh_me():
    return lax.axis_index("x"), lax.axis_index("y"), lax.axis_index("c")


def _flipped(me, f):
    return tuple(1 - v if fl else v for v, fl in zip(me, f))


def _exchange_copies(xs, lands, send_sems, recv_sems, scatter, landed):
    me = _mesh_me()
    cps = []
    for kk, f in enumerate(_FLIPS):
        p = _flipped(me, f)
        for a in range(len(xs)):
            cps.append(pltpu.make_async_remote_copy(
                src_ref=xs[a].at[_dev_index(*p)] if scatter else xs[a],
                dst_ref=lands[a].at[_dev_index(*(p if landed else me))],
                send_sem=send_sems.at[a * 7 + kk], recv_sem=recv_sems.at[a * 7 + kk],
                device_id=p, device_id_type=MESH_IDS))
    return cps


def _exchange_start(srcs, lands, scatter, name, after=()):
    n = len(srcs)

    n_after = len(after)

    def body(*refs):
        xs, ls = refs[:n], refs[n:2 * n]
        send_sems, recv_sems = refs[2 * n + n_after], refs[2 * n + n_after + 1]
        token = refs[-1]
        for cp in _exchange_copies(xs, ls, send_sems, recv_sems, scatter, landed=False):
            cp.start()
        token[...] = jnp.zeros_like(token)

    operands = [pltpu.with_memory_space_constraint(a, pltpu.HBM) for a in list(srcs) + list(lands)]
    res = pl.pallas_call(
        body, name=name,
        in_specs=[_HBM_SPEC] * (2 * n) + [_ANY_SPEC] * len(after),
        out_specs=[_SEM_SPEC, _SEM_SPEC] + [_HBM_SPEC] * (2 * n) + [pl.BlockSpec(memory_space=pltpu.VMEM)],
        out_shape=[pltpu.SemaphoreType.DMA((7 * n,)), pltpu.SemaphoreType.DMA((7 * n,))]
        + [pltpu.HBM(a.shape, a.dtype) for a in operands] + [jax.ShapeDtypeStruct(TOKEN_SHAPE, F32)],
        input_output_aliases={i: 2 + i for i in range(2 * n)},
        compiler_params=pltpu.CompilerParams(has_side_effects=_DATAFLOW),
    )(*operands, *after)
    return (res[0], res[1], list(res[2:2 + n]), list(res[2 + n:2 + 2 * n]), scatter, name), res[-1]


def _exchange_wait(handle, after):
    send_sems, recv_sems, srcs, lands, scatter, name = handle
    n = len(srcs)
    n_after = len(after)

    def body(*refs):
        xs, ls = refs[:n], refs[n:2 * n]
        send_sems_ref, recv_sems_ref = refs[2 * n], refs[2 * n + 1]
        for cp in _exchange_copies(xs, ls, send_sems_ref, recv_sems_ref, scatter, landed=True):
            cp.wait_send()
            cp.wait_recv()

    res = pl.pallas_call(
        body, name=name + "_wait",
        in_specs=[_HBM_SPEC] * (2 * n) + [_SEM_SPEC, _SEM_SPEC] + [_ANY_SPEC] * n_after,
        out_specs=[_HBM_SPEC] * (2 * n),
        out_shape=[pltpu.HBM(a.shape, a.dtype) for a in srcs + lands],
        input_output_aliases={i: i for i in range(2 * n)},
        compiler_params=pltpu.CompilerParams(has_side_effects=_DATAFLOW),
    )(*srcs, *lands, send_sems, recv_sems, *after)
    return list(res[:n]), list(res[n:])


def _slot_sum(g, name, tr):
    _, r, c = g.shape
    tr = min(tr, r)
    assert r % tr == 0

    def body(g_ref, o_ref):
        acc = g_ref[0].astype(F32)
        for s in range(1, N_DEV):
            acc = acc + g_ref[s].astype(F32)
        o_ref[...] = acc

    return pl.pallas_call(
        body, name=name, grid=(r // tr,),
        in_specs=[pl.BlockSpec((N_DEV, tr, c), lambda i: (0, i, 0))],
        out_specs=pl.BlockSpec((tr, c), lambda i: (i, 0)),
        out_shape=jax.ShapeDtypeStruct((r, c), F32),
        compiler_params=pltpu.CompilerParams(dimension_semantics=("parallel",)),
    )(g)


def _adam_update(w, gg, m, v):
    c1 = 1.0 / (1.0 - ADAM_B1 ** ADAM_STEP)
    c2 = 1.0 / (1.0 - ADAM_B2 ** ADAM_STEP)
    nm = ADAM_B1 * m + (1.0 - ADAM_B1) * gg
    nv = ADAM_B2 * v + (1.0 - ADAM_B2) * (gg * gg)
    return -ADAM_LR * ((nm * c1) / (jnp.sqrt(nv * c2) + ADAM_EPS) + ADAM_WD * w), nm, nv


def _adamw_reduce(me, recvs, owns, w, m, v, name, tr=128):
    nl, r, c = w.shape
    assert len(recvs) == nl and len(owns) == nl
    tr = min(tr, r)
    assert r % tr == 0
    nblk = r // tr

    def parked(li, l, i):
        return jnp.where(l < li, 0, jnp.where(l > li, nblk - 1, i))

    def recv_spec(li):
        return pl.BlockSpec((N_DEV, tr, c), lambda l, i, me_ref: (0, parked(li, l, i), 0))

    def own_spec(li):
        return pl.BlockSpec((None, tr, c), lambda l, i, me_ref: (me_ref[0], parked(li, l, i), 0))

    def body(me_ref, *refs):
        rrefs, orefs = refs[:nl], refs[nl:2 * nl]
        w_ref, m_ref, v_ref, g_ref, d_ref, nm_ref, nv_ref = refs[2 * nl:]
        l = pl.program_id(0)

        def of_layer(vals):
            out = vals[0]
            for li in range(1, nl):
                out = jnp.where(l == li, vals[li], out)
            return out

        own = of_layer([o[...].astype(F32) for o in orefs])
        gg = None
        for s in range(N_DEV):
            slot = jnp.where(me_ref[0] == s, own, of_layer([rr[s].astype(F32) for rr in rrefs]))
            gg = slot if gg is None else gg + slot
        g_ref[...] = gg
        d_ref[...], nm_ref[...], nv_ref[...] = _adam_update(w_ref[...], gg, m_ref[...], v_ref[...])

    spec = pl.BlockSpec((None, tr, c), lambda l, i, me_ref: (l, i, 0))
    return pl.pallas_call(
        body, name=name,
        grid_spec=pltpu.PrefetchScalarGridSpec(
            num_scalar_prefetch=1, grid=(nl, nblk),
            in_specs=[recv_spec(li) for li in range(nl)] + [own_spec(li) for li in range(nl)] + [spec] * 3,
            out_specs=[spec] * 4),
        out_shape=[jax.ShapeDtypeStruct((nl, r, c), F32)] * 4,
        compiler_params=pltpu.CompilerParams(dimension_semantics=("arbitrary", "arbitrary")),
    )(me, *recvs, *owns, w, m, v)


def _adamw(w, g, m, v, name, tr=256):
    r, c = w.shape
    tr = min(tr, r)
    assert r % tr == 0

    def body(w_ref, g_ref, m_ref, v_ref, d_ref, nm_ref, nv_ref):
        d_ref[...], nm_ref[...], nv_ref[...] = _adam_update(w_ref[...], g_ref[...], m_ref[...], v_ref[...])

    spec = pl.BlockSpec((tr, c), lambda i: (i, 0))
    return pl.pallas_call(
        body, name=name, grid=(r // tr,), in_specs=[spec] * 4, out_specs=[spec] * 3,
        out_shape=[jax.ShapeDtypeStruct((r, c), F32)] * 3,
        compiler_params=pltpu.CompilerParams(dimension_semantics=("parallel",)),
    )(w, g, m, v)


def _rms_fwd(x, g, name, ts=512, after=()):
    s, d = x.shape

    def fn(first, last, xv, gv):
        return [_rms(xv, gv[...])], []

    return _rowwise(fn, n_rows=s, ts=ts, name=name, rows=[(x, 0, d)], vecs=[g], row_outs=[(d, BF16)],
                    after=after)[0]


def _rms_bwd(x, dn, dres, g, name, ts=256):
    s, d = x.shape

    def fn(first, last, xv, dnv, drv, gv):
        _, vjp = jax.vjp(_rms, xv, gv[...])
        dx, dg = vjp(dnv.astype(F32))
        return [drv + dx], [dg]

    return _rowwise(fn, n_rows=s, ts=ts, name=name, rows=[(x, 0, d), (dn, 0, d), (dres, 0, d)], vecs=[g],
                    row_outs=[(d, F32)], acc_outs=[(1, d)])


def _loss_head(h, tgt, g, ts=256):
    s, d = h.shape

    def fn(first, last, hv, tv, gv):
        def f(hh, gg):
            e = _rms(hh, gg) - tv
            per_row = jnp.mean(e * e, axis=-1, keepdims=True)
            return 0.5 * jnp.sum(per_row, axis=0, keepdims=True)

        l, vjp = jax.vjp(f, hv, gv[...])
        dh, dg = vjp(jnp.ones((1, 1), F32))
        return [dh], [dg, jnp.zeros((1, LANES), F32) + l]

    return _rowwise(fn, n_rows=s, ts=ts, name="loss_head", rows=[(h, 0, d), (tgt, 0, d)], vecs=[g],
                    row_outs=[(d, F32)], acc_outs=[(1, d), (1, LANES)])


def _dn_pre_fwd(qkvz, ba, wconv, alog, dt, heads, ts=128):
    s = qkvz.shape[0]
    d3 = wconv.shape[1]
    d = d3 // 3

    def fn(first, last, xc, bav, xp, wv, av, dv):
        xext = jnp.concatenate([jnp.where(first, 0.0, xp), xc], axis=0)
        cv = _causal_conv(xext, wv, DN_CONV, DN_HALO - (DN_CONV - 1), xc.shape[0])
        return list(_dn_point(cv, bav, av[...], dv[...], heads)), []

    return _rowwise(fn, n_rows=s, ts=ts, name="dn_pre_fwd", rows=[(qkvz, 0, d3), (ba, 0, LANES)],
                    prevs=[(qkvz, 0, d3, DN_HALO)], vecs=[wconv, alog, dt],
                    row_outs=[(d, F32), (d, F32), (d, F32), (LANES, F32)])


def _dn_pre_bwd1(qkvz, ba, wconv, alog, dt, dq, dk, dv, dgb, heads, ts=128):
    s = qkvz.shape[0]
    d3 = wconv.shape[1]
    d = d3 // 3

    def fn(first, last, xc, bav, dqv, dkv, dvv, dgbv, xp, wv, av, dtv):
        xext = jnp.concatenate([jnp.where(first, 0.0, xp), xc], axis=0)
        cv = _causal_conv(xext, wv, DN_CONV, DN_HALO - (DN_CONV - 1), xc.shape[0])
        _, vjp = jax.vjp(functools.partial(_dn_point, heads=heads), cv, bav, av[...], dtv[...])
        dc, dba, da, ddt = vjp((dqv, dkv, dvv, dgbv))
        return [dc, dba], [da, ddt]

    return _rowwise(fn, n_rows=s, ts=ts, name="dn_pre_bwd1",
                    rows=[(qkvz, 0, d3), (ba, 0, LANES), (dq, 0, d), (dk, 0, d), (dv, 0, d), (dgb, 0, LANES)],
                    prevs=[(qkvz, 0, d3, DN_HALO)], vecs=[wconv, alog, dt],
                    row_outs=[(d3, F32), (LANES, BF16)], acc_outs=[(1, LANES), (1, LANES)])


def _dn_pre_bwd2(dc, qkvz, dz, wconv, ts=128):
    s = qkvz.shape[0]
    d3 = wconv.shape[1]
    d = d3 // 3

    def fn(first, last, dcc, xc, dzv, xp, dcn, wv):
        n = dcc.shape[0]
        dcext = jnp.concatenate([dcc, jnp.where(last, 0.0, dcn)], axis=0)
        xext = jnp.concatenate([jnp.where(first, 0.0, xp), xc], axis=0)
        dx = None
        dw = []
        for j in range(DN_CONV):
            term = _shift_rows(dcext, DN_CONV - 1 - j)[:n] * wv[j:j + 1, :]
            dx = term if dx is None else dx + term
            dw.append(_colsum(dcc * _shift_rows(xext, DN_HALO - (DN_CONV - 1) + j)[:n]))
        return [jnp.concatenate([dx, dzv], axis=-1)], [_stack_rows(dw, DN_CONV)]

    return _rowwise(fn, n_rows=s, ts=ts, name="dn_pre_bwd2",
                    rows=[(dc, 0, d3), (qkvz, 0, d3), (dz, 0, d)],
                    prevs=[(qkvz, 0, d3, DN_HALO)], nexts=[(dc, 0, d3, DN_HALO)], vecs=[wconv],
                    row_outs=[(4 * d, BF16)], acc_outs=[(DN_CONV, d3)])


def _dn_post_fwd(o, qkvz, onorm, heads, ts=256):
    s, d = o.shape

    def fn(first, last, ov, zv, nv):
        return [_dn_post(ov, zv, nv[...], heads)], []

    return _rowwise(fn, n_rows=s, ts=ts, name="dn_post_fwd", rows=[(o, 0, d), (qkvz, 3, d)], vecs=[onorm],
                    row_outs=[(d, BF16)])[0]


def _dn_post_bwd(o, qkvz, onorm, dog, heads, ts=256):
    s, d = o.shape

    def fn(first, last, ov, zv, dv, nv):
        _, vjp = jax.vjp(functools.partial(_dn_post, heads=heads), ov, zv, nv[...])
        do, dz, dn = vjp(dv.astype(F32))
        return [do, dz], [dn]

    return _rowwise(fn, n_rows=s, ts=ts, name="dn_post_bwd", rows=[(o, 0, d), (qkvz, 3, d), (dog, 0, d)],
                    vecs=[onorm], row_outs=[(d, F32), (d, F32)], acc_outs=[(1, DN_HEAD_DIM)])


def _cv_mid_fwd(u, wdw, bdw, lng, lnb, ts=256):
    s = u.shape[0]
    d = u.shape[1] // 2

    def fn(first, last, uc, up, wv, bv, gv, lbv):
        uext = jnp.concatenate([jnp.where(first, 0.0, up), uc], axis=0)
        glu = uext[:, :d] * _sigmoid(uext[:, d:])
        c = _causal_conv(glu, wv, CV_WIDTH, CV_HALO - (CV_WIDTH - 1), uc.shape[0]) + bv[...]
        return [c, _ln_silu(c, gv[...], lbv[...])], []

    return _rowwise(fn, n_rows=s, ts=ts, name="cv_mid_fwd", rows=[(u, 0, 2 * d)], prevs=[(u, 0, 2 * d, CV_HALO)],
                    vecs=[wdw, bdw, lng, lnb], row_outs=[(d, F32), (d, BF16)])


def _cv_mid_bwd1(c, ds, dhp, lng, lnb, ts=256):
    s, d = c.shape

    def fn(first, last, cv, dsv, dhv, gv, bv):
        _, vjp = jax.vjp(_ln_silu, cv, gv[...], bv[...])
        dc, dg, db = vjp(dsv.astype(F32))
        return [dc], [dg, db, _colsum(dc), _colsum(dhv)]

    return _rowwise(fn, n_rows=s, ts=ts, name="cv_mid_bwd1", rows=[(c, 0, d), (ds, 0, d), (dhp, 0, d)],
                    vecs=[lng, lnb], row_outs=[(d, F32)], acc_outs=[(1, d)] * 4)


def _cv_mid_bwd2(dc, u, wdw, ts=256):
    s, d = dc.shape

    def fn(first, last, dcc, uc, up, dcn, wv):
        n = dcc.shape[0]
        dcext = jnp.concatenate([dcc, jnp.where(last, 0.0, dcn)], axis=0)
        uext = jnp.concatenate([jnp.where(first, 0.0, up), uc], axis=0)
        glu = uext[:, :d] * _sigmoid(uext[:, d:])
        dglu = None
        dw = []
        for j in range(CV_WIDTH):
            term = _shift_rows(dcext, CV_WIDTH - 1 - j)[:n] * wv[j:j + 1, :]
            dglu = term if dglu is None else dglu + term
            dw.append(_colsum(dcc * _shift_rows(glu, CV_HALO - (CV_WIDTH - 1) + j)[:n]))
        u1, sg = uc[:, :d], _sigmoid(uc[:, d:])
        du = jnp.concatenate([dglu * sg, dglu * u1 * sg * (1.0 - sg)], axis=-1)
        return [du], [_stack_rows(dw, CV_HALO), _colsum(du)]

    return _rowwise(fn, n_rows=s, ts=ts, name="cv_mid_bwd2", rows=[(dc, 0, d), (u, 0, 2 * d)],
                    prevs=[(u, 0, 2 * d, CV_HALO)], nexts=[(dc, 0, d, CV_HALO)], vecs=[wdw],
                    row_outs=[(2 * d, BF16)], acc_outs=[(CV_HALO, d), (1, 2 * d)])


def _attn_fwd(q, k, v, name, ts=256):
    s, d = q.shape

    def fn(first, last, qv, kv, vv):
        return [_attn_tile(qv.astype(F32), kv[...].astype(F32), vv[...].astype(F32))], []

    return _rowwise(fn, n_rows=s, ts=ts, name=name, rows=[(q, 0, d)], vecs=[k, v], row_outs=[(d, BF16)])[0]


def _attn_bwd(q, k, v, do, name, ts=256):
    s, d = q.shape
    m = k.shape[0]

    def fn(first, last, qv, dov, kv, vv):
        _, vjp = jax.vjp(_attn_tile, qv.astype(F32), kv[...].astype(F32), vv[...].astype(F32))
        dq, dk, dv = vjp(dov.astype(F32))
        return [dq], [dk, dv]

    return _rowwise(fn, n_rows=s, ts=ts, name=name, rows=[(q, 0, d), (do, 0, d)], vecs=[k, v],
                    row_outs=[(d, BF16)], acc_outs=[(m, d), (m, d)])


def _pad_lanes(a, off=0):
    r, n = a.shape
    return jnp.pad(a, ((0, 0), (off, LANES - off - n)))


def _local_step(x, mem, tgt, w, fetch=None, emit=None, first_after=()):
    s, d = x.shape
    heads = d // DN_HEAD_DIM
    g = {}
    if fetch is None:
        fetch = lambda group, after: None
    if emit is None:
        emit = lambda group, grads: ()

    def add_res(acc, res):
        return (res + acc,)

    w_in = w["dn_w_in"][0]
    assert w_in.shape[1] == 4 * d + 2 * heads
    w_qkvz = w_in[:, :4 * d]
    w_ba = _pad_lanes(w_in[:, 4 * d:])
    dn_norm = w["dn_norm"]
    alog = _pad_lanes(w["dn_a_log"], heads)
    dtb = _pad_lanes(w["dn_dt_bias"], heads)
    wconv = w["dn_w_conv"][0]
    n0 = _rms_fwd(x, dn_norm, "dn_rms", after=first_after)
    qkvz = _matmul(n0, w_qkvz, "nn", [F32], name="dn_in_proj")
    ba = _matmul(n0, w_ba, "nn", [F32], name="dn_in_proj_ba")
    q, k, v, gb = _dn_pre_fwd(qkvz, ba, wconv, alog, dtb, heads)
    o, states, tms = _delta_fwd(q, k, v, gb, heads)
    og = _dn_post_fwd(o, qkvz, w["dn_out_norm"], heads)
    fetch(1, [og])
    h1 = _matmul(og, w["dn_w_out"][0], "nn", [F32], name="dn_out_proj", epi=add_res, mn_extras=[x])

    def xattn_fwd(h, layer):
        nq = _rms_fwd(h, w["xa_norm"][layer:layer + 1], f"xa{layer}_rms")
        qx = _matmul(nq, w["xa_w_q"][layer], "nn", [BF16], name=f"xa{layer}_q")
        mn = _rms_fwd(mem, w["xa_mem_norm"][layer:layer + 1], f"xa{layer}_mem_rms")
        kv = _matmul(mn, w["xa_w_kv"][layer], "nn", [BF16], name=f"xa{layer}_kv")
        kx, vx = kv[:, :d], kv[:, d:]
        ox = _attn_fwd(qx, kx, vx, f"xa{layer}_attn")
        hn = _matmul(ox, w["xa_w_o"][layer], "nn", [F32], name=f"xa{layer}_o", epi=add_res, mn_extras=[h])
        return hn, (h, nq, qx, mn, kx, vx, ox)

    mlp_tm = (1024, 2048)

    def mlp_fwd(h, layer):
        nm = _rms_fwd(h, w["mlp_norm"][layer:layer + 1], f"mlp{layer}_rms")

        def epi(acc):
            r = jnp.maximum(acc, 0.0)
            return acc, r * r

        u, a = _matmul(nm, w["mlp_w_up"][layer], "nn", [BF16, BF16], name=f"mlp{layer}_up", epi=epi,
                       tm=mlp_tm[layer], tn=1024 * 1024 // mlp_tm[layer])
        hn = _matmul(a, w["mlp_w_down"][layer], "nn", [F32], name=f"mlp{layer}_down", epi=add_res,
                     mn_extras=[h], tk=2048)
        return hn, (h, nm, u, a)

    h2, xa0 = xattn_fwd(h1, 0)
    h3, mlp0 = mlp_fwd(h2, 0)

    fetch(2, [h3])
    n1 = _rms_fwd(h3, w["cv_norm"], "cv_rms")
    u_cv = _matmul(n1, w["cv_w_pw1"][0], "nn", [F32], name="cv_pw1", epi=lambda acc, b: (acc + b,),
                   row_extras=[w["cv_b_pw1"]])
    wdw = jnp.pad(w["cv_w_dw"][0], ((0, CV_HALO - CV_WIDTH), (0, 0)))
    c_cv, s_cv = _cv_mid_fwd(u_cv, wdw, w["cv_b_dw"], w["cv_ln_g"], w["cv_ln_b"])
    h4 = _matmul(s_cv, w["cv_w_pw2"][0], "nn", [F32], name="cv_pw2",
                 epi=lambda acc, res, b: (res + acc + b,), mn_extras=[h3], row_extras=[w["cv_b_pw2"]])
    fetch(3, [h4])
    h5, xa1 = xattn_fwd(h4, 1)
    h6, mlp1 = mlp_fwd(h5, 1)

    fnorm = w["final_norm"].reshape(1, d)
    dh, g_fn, loss = _loss_head(h6, tgt, fnorm)
    g["final_norm"] = g_fn.reshape(d)

    def mlp_bwd(dh, layer, saved, after=()):
        h, nm, u, a = saved
        tm = mlp_tm[layer]
        du = _matmul(dh, w["mlp_w_down"][layer], "nt", [BF16], name=f"mlp{layer}_down_dx", after=after,
                     epi=lambda acc, uu: (acc * 2.0 * jnp.maximum(uu.astype(F32), 0.0),), mn_extras=[u])
        gdown = _matmul(a, dh, "tn", [BF16], name=f"mlp{layer}_down_dw", tm=tm)
        dn = _matmul(du, w["mlp_w_up"][layer], "nt", [F32], name=f"mlp{layer}_up_dx", tm=tm)
        gup = _matmul(nm, du, "tn", [BF16], name=f"mlp{layer}_up_dw", out_dm=True, tk=tm)
        dhn, gn = _rms_bwd(h, dn, dh, w["mlp_norm"][layer:layer + 1], f"mlp{layer}_rms_bwd")
        return dhn, gup, gdown, gn

    def xattn_bwd(dh, layer, saved):
        h, nq, qx, mn, kx, vx, ox = saved
        dox = _matmul(dh, w["xa_w_o"][layer], "nt", [BF16], name=f"xa{layer}_o_dx")
        go = _matmul(ox, dh, "tn", [BF16], name=f"xa{layer}_o_dw")
        dqx, dkx, dvx = _attn_bwd(qx, kx, vx, dox, f"xa{layer}_attn_bwd")
        dn = _matmul(dqx, w["xa_w_q"][layer], "nt", [F32], name=f"xa{layer}_q_dx")
        gq = _matmul(nq, dqx, "tn", [BF16], name=f"xa{layer}_q_dw")
        dkv = jnp.concatenate([dkx, dvx], axis=-1)
        gkv = _matmul(mn, dkv, "tn", [BF16], name=f"xa{layer}_kv_dw", out_dm=True)
        dmn = _matmul(dkv, w["xa_w_kv"][layer], "nt", [F32], name=f"xa{layer}_kv_dx", tk=2048)
        _, gmem = _rms_bwd(mem, dmn, dmn, w["xa_mem_norm"][layer:layer + 1], f"xa{layer}_mem_rms_bwd")
        dhn, gn = _rms_bwd(h, dn, dh, w["xa_norm"][layer:layer + 1], f"xa{layer}_rms_bwd")
        return dhn, gq, gkv, go, gn, gmem

    dh, gup1, gdown1, gmn1 = mlp_bwd(dh, 1, mlp1)
    dh, gq1, gkv1, go1, gxn1, gmem1 = xattn_bwd(dh, 1, xa1)
    g.update(mlp_w_up=[None, gup1], mlp_w_down=[None, gdown1], xa_w_q=[None, gq1], xa_w_kv=[None, gkv1],
             xa_w_o=[None, go1])
    tok = emit(3, g)

    ds_cv = _matmul(dh, w["cv_w_pw2"][0], "nt", [BF16], name="cv_pw2_dx", after=tok)
    g["cv_w_pw2"] = [_matmul(s_cv, dh, "tn", [BF16], name="cv_pw2_dw")]
    dc_cv, g_lng, g_lnb, g_bdw, g_b2 = _cv_mid_bwd1(c_cv, ds_cv, dh, w["cv_ln_g"], w["cv_ln_b"])
    du_cv, g_wdw, g_b1 = _cv_mid_bwd2(dc_cv, u_cv, wdw)
    dn1 = _matmul(du_cv, w["cv_w_pw1"][0], "nt", [F32], name="cv_pw1_dx", tk=2048)
    g["cv_w_pw1"] = [_matmul(n1, du_cv, "tn", [BF16], name="cv_pw1_dw", out_dm=True)]
    dh, g_cvn = _rms_bwd(h3, dn1, dh, w["cv_norm"], "cv_rms_bwd")
    g.update(cv_ln_g=g_lng, cv_ln_b=g_lnb, cv_b_dw=g_bdw, cv_b_pw2=g_b2, cv_b_pw1=g_b1, cv_norm=g_cvn,
             cv_w_dw=g_wdw[:CV_WIDTH][None])

    tok = emit(2, g)
    dh, gup0, gdown0, gmn0 = mlp_bwd(dh, 0, mlp0, after=tok)
    dh, gq0, gkv0, go0, gxn0, gmem0 = xattn_bwd(dh, 0, xa0)
    g["mlp_w_up"][0] = gup0
    g["mlp_w_down"][0] = gdown0
    g["mlp_norm"] = jnp.concatenate([gmn0, gmn1], axis=0)
    g["xa_w_q"][0] = gq0
    g["xa_w_kv"][0] = gkv0
    g["xa_w_o"][0] = go0
    g["xa_norm"] = jnp.concatenate([gxn0, gxn1], axis=0)
    g["xa_mem_norm"] = jnp.concatenate([gmem0, gmem1], axis=0)
    tok = emit(1, g)

    dog = _matmul(dh, w["dn_w_out"][0], "nt", [BF16], name="dn_out_proj_dx", after=tok)
    g["dn_w_out"] = [_matmul(og, dh, "tn", [BF16], name="dn_out_proj_dw")]
    do, dz, g_on = _dn_post_bwd(o, qkvz, w["dn_out_norm"], dog, heads)
    dq, dk, dv, dgb = _delta_bwd(q, k, v, gb, states, tms, do, heads)
    dc, dba, g_alog, g_dt = _dn_pre_bwd1(qkvz, ba, wconv, alog, dtb, dq, dk, dv, dgb, heads)
    dqkvz, g_wconv = _dn_pre_bwd2(dc, qkvz, dz, wconv)
    g_qkvz = _matmul(n0, dqkvz, "tn", [BF16], name="dn_in_proj_dw")
    g_ba = _matmul(n0, dba, "tn", [BF16], name="dn_in_proj_ba_dw")
    g["dn_w_in"] = [jnp.concatenate([g_qkvz, g_ba[:, :2 * heads]], axis=1)]
    g["dn_w_conv"] = g_wconv[None]
    tok = emit(0, g)
    dn0a = _matmul(dba, w_ba, "nt", [F32], name="dn_in_proj_ba_dx", after=tok)
    dn0 = _matmul(dqkvz, w_qkvz, "nt", [F32], name="dn_in_proj_dx", epi=add_res, mn_extras=[dn0a], tk=2048)
    grad_x, g_dnn = _rms_bwd(x, dn0, dh, dn_norm, "dn_rms_bwd")
    g.update(dn_norm=g_dnn, dn_out_norm=g_on,
             dn_a_log=g_alog[:, heads:2 * heads], dn_dt_bias=g_dt[:, heads:2 * heads])
    return loss, grad_x, g


def _round_up(n, m):
    return (n + m - 1) // m * m


def _pack_rows(parts, cols, row_mult):
    lead = parts[0].shape[:-1]
    flat, offs, off = [], [], 0
    for p in parts:
        n = _round_up(p.shape[-1], cols)
        flat.append(jnp.pad(p, [(0, 0)] * len(lead) + [(0, n - p.shape[-1])]))
        offs.append(off)
        off += n
    total = _round_up(off, cols * row_mult)
    if total > off:
        flat.append(jnp.zeros(lead + (total - off,), parts[0].dtype))
    return jnp.concatenate(flat, axis=-1).reshape(lead + (total // cols, cols)), offs


def _unpack(packed, offs, shapes):
    lead = packed.shape[:-2]
    flat = packed.reshape(lead + (-1,))
    out = []
    for off, shp in zip(offs, shapes):
        n = 1
        for v in shp:
            n *= v
        out.append(flat[..., off:off + n].reshape(lead + tuple(shp)))
    return out


def kernel(x, mem, dn_norm, dn_w_in, dn_w_conv, dn_a_log, dn_dt_bias, dn_out_norm, dn_w_out, cv_norm, cv_w_pw1, cv_b_pw1, cv_w_dw, cv_b_dw, cv_ln_g, cv_ln_b, cv_w_pw2, cv_b_pw2, xa_norm, xa_mem_norm, xa_w_q, xa_w_kv, xa_w_o, mlp_norm, mlp_w_up, mlp_w_down, final_norm, loss_target, m_dn_norm, m_dn_w_in, m_dn_w_conv, m_dn_a_log, m_dn_dt_bias, m_dn_out_norm, m_dn_w_out, m_cv_norm, m_cv_w_pw1, m_cv_b_pw1, m_cv_w_dw, m_cv_b_dw, m_cv_ln_g, m_cv_ln_b, m_cv_w_pw2, m_cv_b_pw2, m_xa_norm, m_xa_mem_norm, m_xa_w_q, m_xa_w_kv, m_xa_w_o, m_mlp_norm, m_mlp_w_up, m_mlp_w_down, m_final_norm, v_dn_norm, v_dn_w_in, v_dn_w_conv, v_dn_a_log, v_dn_dt_bias, v_dn_out_norm, v_dn_w_out, v_cv_norm, v_cv_w_pw1, v_cv_b_pw1, v_cv_w_dw, v_cv_b_dw, v_cv_ln_g, v_cv_ln_b, v_cv_w_pw2, v_cv_b_pw2, v_xa_norm, v_xa_mem_norm, v_xa_w_q, v_xa_w_kv, v_xa_w_o, v_mlp_norm, v_mlp_w_up, v_mlp_w_down, v_final_norm):
    wsh = dict(dn_norm=dn_norm, dn_w_in=dn_w_in, dn_w_conv=dn_w_conv, dn_a_log=dn_a_log, dn_dt_bias=dn_dt_bias, dn_out_norm=dn_out_norm, dn_w_out=dn_w_out, cv_norm=cv_norm, cv_w_pw1=cv_w_pw1, cv_b_pw1=cv_b_pw1, cv_w_dw=cv_w_dw, cv_b_dw=cv_b_dw, cv_ln_g=cv_ln_g, cv_ln_b=cv_ln_b, cv_w_pw2=cv_w_pw2, cv_b_pw2=cv_b_pw2, xa_norm=xa_norm, xa_mem_norm=xa_mem_norm, xa_w_q=xa_w_q, xa_w_kv=xa_w_kv, xa_w_o=xa_w_o, mlp_norm=mlp_norm, mlp_w_up=mlp_w_up, mlp_w_down=mlp_w_down, final_norm=final_norm)
    msh = dict(dn_norm=m_dn_norm, dn_w_in=m_dn_w_in, dn_w_conv=m_dn_w_conv, dn_a_log=m_dn_a_log, dn_dt_bias=m_dn_dt_bias, dn_out_norm=m_dn_out_norm, dn_w_out=m_dn_w_out, cv_norm=m_cv_norm, cv_w_pw1=m_cv_w_pw1, cv_b_pw1=m_cv_b_pw1, cv_w_dw=m_cv_w_dw, cv_b_dw=m_cv_b_dw, cv_ln_g=m_cv_ln_g, cv_ln_b=m_cv_ln_b, cv_w_pw2=m_cv_w_pw2, cv_b_pw2=m_cv_b_pw2, xa_norm=m_xa_norm, xa_mem_norm=m_xa_mem_norm, xa_w_q=m_xa_w_q, xa_w_kv=m_xa_w_kv, xa_w_o=m_xa_w_o, mlp_norm=m_mlp_norm, mlp_w_up=m_mlp_w_up, mlp_w_down=m_mlp_w_down, final_norm=m_final_norm)
    vsh = dict(dn_norm=v_dn_norm, dn_w_in=v_dn_w_in, dn_w_conv=v_dn_w_conv, dn_a_log=v_dn_a_log, dn_dt_bias=v_dn_dt_bias, dn_out_norm=v_dn_out_norm, dn_w_out=v_dn_w_out, cv_norm=v_cv_norm, cv_w_pw1=v_cv_w_pw1, cv_b_pw1=v_cv_b_pw1, cv_w_dw=v_cv_w_dw, cv_b_dw=v_cv_b_dw, cv_ln_g=v_cv_ln_g, cv_ln_b=v_cv_ln_b, cv_w_pw2=v_cv_w_pw2, cv_b_pw2=v_cv_b_pw2, xa_norm=v_xa_norm, xa_mem_norm=v_xa_mem_norm, xa_w_q=v_xa_w_q, xa_w_kv=v_xa_w_kv, xa_w_o=v_xa_w_o, mlp_norm=v_mlp_norm, mlp_w_up=v_mlp_w_up, mlp_w_down=v_mlp_w_down, final_norm=v_final_norm)

    big_axis = dict(BIG)

    small_pack, small_offs = _pack_rows([wsh[nm].reshape(-1) for nm in SMALL_SH], LANES, 8)
    w = {nm: [None] * wsh[nm].shape[0] for nm in big_axis}

    def put_weights(group, gathered):
        for (nm, layer), gth in zip(group, gathered):
            if nm == "dn_w_in":
                w[nm][layer] = gth.transpose(1, 0, 2).reshape(gth.shape[1], N_DEV * gth.shape[2])
            elif big_axis[nm] == 1:
                w[nm][layer] = gth.reshape(N_DEV * gth.shape[1], gth.shape[2])
            else:
                w[nm][layer] = gth

    first = _all_gather([wsh[nm][layer].astype(BF16) for nm, layer in GATHER_GROUPS[0]] + [small_pack],
                        "weights_all_gather_0")
    put_weights(GATHER_GROUPS[0], first)
    me = _dev_index(*_mesh_me())
    gather_handles, tokens = {}, []
    for gi in range(1, len(GATHER_GROUPS)):
        shards = [wsh[nm][layer].astype(BF16) for nm, layer in GATHER_GROUPS[gi]]
        lands = [lax.dynamic_update_slice(lax.empty((N_DEV,) + s.shape, s.dtype), s[None], (me, 0, 0))
                 for s in shards]
        gather_handles[gi], tok = _exchange_start(shards, lands, False, f"weights_gather_{gi}",
                                                  after=[first[-1]] + tokens)
        tokens.append(tok)
    for nm, gth in zip(SMALL_SH, _unpack(first[-1], small_offs, [wsh[nm].shape for nm in SMALL_SH])):
        w[nm] = jnp.moveaxis(gth, 0, -2).reshape(gth.shape[1:-1] + (N_DEV * gth.shape[-1],))
    for nm in REPL:
        w[nm] = wsh[nm]

    def fetch(gi, after):
        put_weights(GATHER_GROUPS[gi], _exchange_wait(gather_handles[gi], after)[1])

    scatter_handles = {}

    def emit(gi, g):
        blocks = []
        for nm, layer in SCATTER_GROUPS[gi]:
            gw = g[nm][layer]
            if nm == "dn_w_in":
                gw = gw.reshape(gw.shape[0], N_DEV, -1).transpose(1, 0, 2)
            elif big_axis[nm] == 1:
                gw = gw.reshape(N_DEV, gw.shape[0] // N_DEV, gw.shape[1])
            blocks.append(gw)
        if gi == 0:
            gsmall_pack, _ = _pack_rows(
                [jnp.moveaxis(g[nm].reshape(g[nm].shape[:-1] + (N_DEV, -1)), -2, 0).reshape(N_DEV, -1)
                 for nm in SMALL_SH], LANES, 8)
            blocks.append(gsmall_pack)
        lands = [lax.empty(b.shape, b.dtype) for b in blocks]
        scatter_handles[gi], tok = _exchange_start(blocks, lands, True, f"grads_scatter_{gi}")
        return [tok]

    loss_part, grad_x, g = _local_step(x[0], mem[0], loss_target[0], w, fetch, emit, tokens)

    recv = {nm: [None] * wsh[nm].shape[0] for nm in big_axis}
    sent = {nm: [None] * wsh[nm].shape[0] for nm in big_axis}
    gsh, delta, new_m, new_v = {}, {}, {}, {}
    after = [grad_x]
    done = set()
    me_arr = me.astype(jnp.int32).reshape(1)
    for gi in reversed(range(len(SCATTER_GROUPS))):
        sources, landed = _exchange_wait(scatter_handles[gi], after)
        for (nm, layer), src, r in zip(SCATTER_GROUPS[gi], sources, landed):
            sent[nm][layer], recv[nm][layer] = src, r
        if gi == 0:
            slot = lax.broadcasted_iota(jnp.int32, landed[-1].shape, 0)
            rsmall = jnp.where(slot == me, sources[-1], landed[-1])
        for nm in big_axis:
            if nm not in done and all(r is not None for r in recv[nm]):
                gsh[nm], delta[nm], new_m[nm], new_v[nm] = _adamw_reduce(
                    me_arr, recv[nm], sent[nm], wsh[nm], msh[nm], vsh[nm], f"adamw_{nm}")
                done.add(nm)
                after = [delta[nm]]
    gsmall_red = _slot_sum(rsmall, "grads_small_sum", 512)
    repl_pack, repl_offs = _pack_rows([g[nm].reshape(-1) for nm in REPL], LANES, 8)
    (repl_all,) = _all_gather([repl_pack], "repl_grads_all_gather")
    repl_red = _slot_sum(repl_all, "repl_grads_sum", 512)
    for nm, val in zip(SMALL_SH, _unpack(gsmall_red, small_offs, [wsh[nm].shape for nm in SMALL_SH])):
        gsh[nm] = val
    for nm, val in zip(REPL, _unpack(repl_red, repl_offs, [wsh[nm].shape for nm in REPL])):
        gsh[nm] = val
    small_names = list(SMALL_SH) + list(REPL)
    packs = []
    for src in (wsh, gsh, msh, vsh):
        pk, sm_offs = _pack_rows([src[nm].reshape(-1) for nm in small_names], LANES, 8)
        packs.append(pk)
    outs = _adamw(*packs, "adamw_small")
    for dst, pk in zip((delta, new_m, new_v), outs):
        for nm, val in zip(small_names, _unpack(pk, sm_offs, [wsh[nm].shape for nm in small_names])):
            dst[nm] = val

    loss = lax.psum(loss_part[0, 0], ("x", "y", "c"))
    return (loss, grad_x[None], *[gsh[nm] for nm in WEIGHTS], *[delta[nm] for nm in WEIGHTS],
            *[new_m[nm] for nm in WEIGHTS], *[new_v[nm] for nm in WEIGHTS])
```

```python
import functools

import jax
import jax.numpy as jnp
from jax import lax
from jax.experimental import pallas as pl
from jax.experimental.pallas import tpu as pltpu

F32 = jnp.float32
BF16 = jnp.bfloat16
HP = lax.Precision.HIGHEST
MESH_IDS = pl.DeviceIdType.MESH

N_DEV = 8
LANES = 128
RMS_EPS = 1e-6
LN_EPS = 1e-5
DN_HEAD_DIM = 128
DN_CONV = 4
DN_CHUNK = 64
CV_WIDTH = 31
XA_HEADS = 4
EPI_SLAB = 256
DN_HALO = 8
CV_HALO = 32

ADAM_LR = 0.001
ADAM_B1 = 0.9
ADAM_B2 = 0.999
ADAM_EPS = 1e-08
ADAM_WD = 0.01
ADAM_STEP = 10

BIG = (("dn_w_in", 2), ("dn_w_out", 1), ("cv_w_pw1", 2), ("cv_w_pw2", 1), ("xa_w_q", 1), ("xa_w_kv", 2),
       ("xa_w_o", 1), ("mlp_w_up", 2), ("mlp_w_down", 1))
_LAYER_GROUP = ("xa_w_q", "xa_w_o", "mlp_w_down", "xa_w_kv", "mlp_w_up")
GATHER_GROUPS = (
    (("dn_w_in", 0),),
    (("dn_w_out", 0),) + tuple((nm, 0) for nm in _LAYER_GROUP),
    (("cv_w_pw2", 0), ("cv_w_pw1", 0)),
    tuple((nm, 1) for nm in _LAYER_GROUP),
)
SCATTER_GROUPS = (
    (("dn_w_out", 0), ("dn_w_in", 0)),
    tuple((nm, 0) for nm in _LAYER_GROUP),
    (("cv_w_pw2", 0), ("cv_w_pw1", 0)),
    tuple((nm, 1) for nm in _LAYER_GROUP),
)
SMALL_SH = ("cv_norm", "cv_b_pw1", "cv_b_dw", "cv_ln_g", "cv_ln_b", "cv_b_pw2", "cv_w_dw", "dn_w_conv")
REPL = ("dn_norm", "dn_a_log", "dn_dt_bias", "dn_out_norm", "xa_norm", "xa_mem_norm", "mlp_norm", "final_norm")
WEIGHTS = ("dn_norm", "dn_w_in", "dn_w_conv", "dn_a_log", "dn_dt_bias", "dn_out_norm", "dn_w_out", "cv_norm",
           "cv_w_pw1", "cv_b_pw1", "cv_w_dw", "cv_b_dw", "cv_ln_g", "cv_ln_b", "cv_w_pw2", "cv_b_pw2", "xa_norm",
           "xa_mem_norm", "xa_w_q", "xa_w_kv", "xa_w_o", "mlp_norm", "mlp_w_up", "mlp_w_down", "final_norm")


def _dot_dims(mode, batched):
    o = 1 if batched else 0
    contract = {"nn": ((1 + o,), (o,)), "nt": ((1 + o,), (1 + o,)), "tn": ((o,), (o,))}[mode]
    return (contract, (((0,), (0,)) if batched else ((), ())))


def _bdot(a, b, mode):
    return lax.dot_general(a.astype(BF16), b.astype(BF16), _dot_dims(mode, a.ndim == 3),
                           preferred_element_type=F32)


@functools.partial(jax.custom_vjp, nondiff_argnums=(2,))
def _mm(a, b, mode):
    return _bdot(a, b, mode)


def _mm_fwd(a, b, mode):
    return _bdot(a, b, mode), (a, b)


def _mm_bwd(mode, res, ct):
    a, b = res
    if mode == "nn":
        da, db = _bdot(ct, b, "nt"), _bdot(a, ct, "tn")
    elif mode == "nt":
        da, db = _bdot(ct, b, "nn"), _bdot(ct, a, "tn")
    else:
        da, db = _bdot(b, ct, "nt"), _bdot(a, ct, "nn")
    return da.astype(a.dtype), db.astype(b.dtype)


_mm.defvjp(_mm_fwd, _mm_bwd)


def _sigmoid(x):
    return 0.5 * (jnp.tanh(0.5 * x) + 1.0)


def _silu(x):
    return x * _sigmoid(x)


def _softplus(x):
    return jnp.maximum(x, 0.0) + jnp.log(1.0 + jnp.exp(-jnp.abs(x)))


def _rms(x, g):
    r = lax.rsqrt(jnp.mean(x * x, axis=-1, keepdims=True) + RMS_EPS)
    return x * r * g


def _shift_rows(x, off):
    if off == 0:
        return x
    return pltpu.roll(x, x.shape[0] - off, 0)


def _series_dot(a, b, mode):
    return _bdot(a, b, mode)


def _chunk_masks(c):
    ii = lax.broadcasted_iota(jnp.int32, (c, c), 0)
    jj = lax.broadcasted_iota(jnp.int32, (c, c), 1)
    return (ii == jj).astype(F32), ii >= jj, ii > jj


def _neumann_inverse(lm):
    n = lm.shape[-1]
    t = -lm
    p = lm
    size = 2
    while size < n:
        size *= 2
        p = _series_dot(p, p, "nn")
        t = t + p + _series_dot(t, p, "nn")
    return t


def _apply_inverse(tm, rhs, mode):
    return rhs + _series_dot(tm, rhs, mode)


@jax.custom_vjp
def _unit_lower_solve(lm, rhs, tm):
    return _apply_inverse(tm, rhs, "nn")


def _uls_fwd(lm, rhs, tm):
    sol = _apply_inverse(tm, rhs, "nn")
    return sol, (tm, sol)


def _uls_bwd(res, ct):
    tm, sol = res
    d_rhs = _apply_inverse(tm, ct, "tn")
    return -_bdot(d_rhs, sol, "nt"), d_rhs, jnp.zeros_like(tm)


_unit_lower_solve.defvjp(_uls_fwd, _uls_bwd)


def _delta_chunk(q, k, v, gcol, bcol, s0, tm=None):
    c = q.shape[1]
    eye, causal, strict = _chunk_masks(c)
    grow = jnp.sum(eye * gcol, axis=1, keepdims=True)
    gc = jnp.sum(jnp.where(causal, grow, 0.0), axis=2, keepdims=True)
    gc_row = jnp.sum(eye * gc, axis=1, keepdims=True)
    decay = jnp.exp(jnp.where(causal, gc - gc_row, -jnp.inf))
    kb = k * bcol
    lm = jnp.where(strict, _mm(kb, k, "nt") * decay, 0.0)
    if tm is None:
        tm = _neumann_inverse(lax.stop_gradient(lm))
    egc = jnp.exp(gc)
    rhs = jnp.concatenate([v * bcol, kb * egc], axis=-1)
    sol = _unit_lower_solve(lm, rhs, tm)
    dv_ = v.shape[-1]
    u, w = sol[..., :dv_], sol[..., dv_:]
    attn = _mm(q, k, "nt") * decay
    qd = q * egc
    gl = jnp.sum(grow, axis=2, keepdims=True)
    kd = k * jnp.exp(gl - gc)
    v_new = u - _mm(w, s0, "nn")
    o = _mm(qd, s0, "nn") + _mm(attn, v_new, "nn")
    s1 = s0 * jnp.exp(gl) + _mm(kd, v_new, "tn")
    return o, s1, tm


def _dn_point(cv, ba, alog, dt, heads):
    a = _silu(cv)
    d = cv.shape[1] // 3
    qs, ks = [], []
    for h in range(heads):
        qh = a[:, h * DN_HEAD_DIM:(h + 1) * DN_HEAD_DIM]
        qs.append(qh * lax.rsqrt(jnp.sum(qh * qh, axis=-1, keepdims=True) + 1e-6) * (DN_HEAD_DIM ** -0.5))
        kh = a[:, d + h * DN_HEAD_DIM:d + (h + 1) * DN_HEAD_DIM]
        ks.append(kh * lax.rsqrt(jnp.sum(kh * kh, axis=-1, keepdims=True) + 1e-6))
    q = jnp.concatenate(qs, axis=-1)
    k = jnp.concatenate(ks, axis=-1)
    v = a[:, 2 * d:]
    lane = lax.broadcasted_iota(jnp.int32, ba.shape, 1)
    beta = _sigmoid(ba)
    g = -jnp.exp(alog) * _softplus(ba + dt)
    gb = jnp.where(lane < heads, beta, jnp.where(lane < 2 * heads, g, 0.0))
    return q, k, v, gb


def _dn_post(o, z, onorm, heads):
    outs = []
    for h in range(heads):
        oh = o[:, h * DN_HEAD_DIM:(h + 1) * DN_HEAD_DIM]
        outs.append(oh * lax.rsqrt(jnp.mean(oh * oh, axis=-1, keepdims=True) + RMS_EPS) * onorm)
    return jnp.concatenate(outs, axis=-1) * _silu(z)


def _attn_tile(q, k, v):
    hd = q.shape[1] // XA_HEADS
    outs = []
    for h in range(XA_HEADS):
        sl = slice(h * hd, (h + 1) * hd)
        s = _mm(q[:, sl], k[:, sl], "nt") * (hd ** -0.5)
        m = lax.stop_gradient(jnp.max(s, axis=-1, keepdims=True))
        e = jnp.exp(s - m)
        p = e / jnp.sum(e, axis=-1, keepdims=True)
        outs.append(_mm(p, v[:, sl], "nn"))
    return jnp.concatenate(outs, axis=-1)


def _ln_silu(c, g, b):
    mu = jnp.mean(c, axis=-1, keepdims=True)
    xc = c - mu
    y = xc * lax.rsqrt(jnp.mean(xc * xc, axis=-1, keepdims=True) + LN_EPS)
    return _silu(y * g + b)


def _causal_conv(xext, w, width, lead, ts):
    acc = None
    for j in range(width):
        term = _shift_rows(xext, lead + j)[:ts] * w[j:j + 1, :]
        acc = term if acc is None else acc + term
    return acc


def _colsum(x):
    return jnp.sum(x, axis=0, keepdims=True)


def _stack_rows(rows, n_rows):
    c = rows[0].shape[1]
    ridx = lax.broadcasted_iota(jnp.int32, (n_rows, c), 0)
    out = jnp.zeros((n_rows, c), F32)
    for j, r in enumerate(rows):
        out = out + jnp.where(ridx == j, r, 0.0)
    return out


def _matmul(a, b, mode, out_dtypes, *, name, epi=None, mn_extras=(), row_extras=(), out_dm=False, after=(),
            n_rowsum=0, slab=0, tm=1024, tn=1024, tk=1024):
    b_dm = b.ndim == 3
    b_shape = (b.shape[1], N_DEV * b.shape[2]) if b_dm else b.shape
    if mode == "nn":
        (m, k), (k2, n) = a.shape, b_shape
    elif mode == "nt":
        (m, k), (n, k2) = a.shape, b_shape
    else:
        (k, m), (k2, n) = a.shape, b_shape
    assert k == k2, (a.shape, b.shape, mode)
    tm, tn, tk = min(tm, m), min(tn, n), min(tk, k)
    cb, nb = 0, 1
    if b_dm:
        assert mode in ("nn", "nt")
        cb = b.shape[2]
        nb = max(1, (tn if mode == "nn" else tk) // cb)
        if mode == "nn":
            tn = nb * cb
        else:
            tk = nb * cb
    co, no = 0, 1
    if out_dm:
        co = n // N_DEV
        no = max(1, tn // co)
        tn = no * co
    assert m % tm == 0 and n % tn == 0 and k % tk == 0, (m, n, k, tm, tn, tk)
    nk = k // tk
    if mode == "tn":
        a_spec = pl.BlockSpec((tk, tm), lambda j, i, kk: (kk, i))
    else:
        a_spec = pl.BlockSpec((tm, tk), lambda j, i, kk: (i, kk))
    if b_dm:
        b_spec = (pl.BlockSpec((nb, tn, cb), lambda j, i, kk: (kk, j, 0)) if mode == "nt"
                  else pl.BlockSpec((nb, tk, cb), lambda j, i, kk: (j, kk, 0)))
    else:
        b_spec = (pl.BlockSpec((tn, tk), lambda j, i, kk: (j, kk)) if mode == "nt"
                  else pl.BlockSpec((tk, tn), lambda j, i, kk: (kk, j)))
    mn_spec = pl.BlockSpec((tm, tn), lambda j, i, kk: (i, j))
    row_spec = pl.BlockSpec((1, tn), lambda j, i, kk: (0, j))
    n_extra = len(mn_extras) + len(row_extras)
    n_out = len(out_dtypes)
    in_specs = ([a_spec, b_spec] + [mn_spec] * len(mn_extras) + [row_spec] * len(row_extras)
                + [_ANY_SPEC] * len(after))
    args = [a, b, *mn_extras, *row_extras, *after]
    if out_dm:
        out_specs = [pl.BlockSpec((no, tm, co), lambda j, i, kk: (j, i, 0))] * n_out
        out_shape = [jax.ShapeDtypeStruct((N_DEV, m, co), dt) for dt in out_dtypes]
    else:
        out_specs = [mn_spec] * n_out
        out_shape = [jax.ShapeDtypeStruct((m, n), dt) for dt in out_dtypes]
    out_specs = out_specs + [row_spec] * n_rowsum
    out_shape = out_shape + [jax.ShapeDtypeStruct((1, n), F32)] * n_rowsum
    n_in = len(args)
    n_mn = len(mn_extras)
    step = min(slab, tm) if slab else tm
    assert tm % step == 0

    def dot(a_ref, b_ref):
        if not b_dm:
            return _bdot(a_ref[...], b_ref[...], mode)
        if mode == "nn":
            parts = [_bdot(a_ref[...], b_ref[dd], "nn") for dd in range(nb)]
            return parts[0] if nb == 1 else jnp.concatenate(parts, axis=1)
        out = None
        for dd in range(nb):
            part = _bdot(a_ref[:, dd * cb:(dd + 1) * cb], b_ref[dd], "nt")
            out = part if out is None else out + part
        return out

    def finish(acc_src, extras, outs):
        sums = [None] * n_rowsum
        for r0 in range(0, tm, step):
            rs = slice(r0, r0 + step)
            acc_val = acc_src[rs, :]
            if epi is None:
                vals = (acc_val,)
            else:
                vals = epi(acc_val, *[e[rs, :] for e in extras[:n_mn]], *[e[...] for e in extras[n_mn:]])
            for o_ref, val in zip(outs[:n_out], vals[:n_out]):
                if out_dm:
                    for dd in range(no):
                        o_ref[dd, rs, :] = val[:, dd * co:(dd + 1) * co].astype(o_ref.dtype)
                else:
                    o_ref[rs, :] = val.astype(o_ref.dtype)
            for q in range(n_rowsum):
                sums[q] = vals[n_out + q] if sums[q] is None else sums[q] + vals[n_out + q]
        for q in range(n_rowsum):
            s_ref = outs[n_out + q]

            @pl.when(pl.program_id(1) == 0)
            def _():
                s_ref[...] = sums[q]

            @pl.when(pl.program_id(1) > 0)
            def _():
                s_ref[...] += sums[q]

    def body_one_step(*refs):
        finish(dot(refs[0], refs[1]), refs[2:2 + n_extra], refs[n_in:])

    def body(*refs):
        a_ref, b_ref = refs[0], refs[1]
        acc = refs[-1]
        kk = pl.program_id(2)

        @pl.when(kk == 0)
        def _():
            acc[...] = jnp.zeros_like(acc)

        acc[...] += dot(a_ref, b_ref)

        @pl.when(kk == nk - 1)
        def _():
            finish(acc, refs[2:2 + n_extra], refs[n_in:-1])

    res = pl.pallas_call(
        body_one_step if nk == 1 else body, name=name,
        grid=(n // tn, m // tm, nk),
        in_specs=in_specs, out_specs=out_specs, out_shape=out_shape,
        scratch_shapes=[] if nk == 1 else [pltpu.VMEM((tm, tn), F32)],
        compiler_params=pltpu.CompilerParams(
            dimension_semantics=("parallel", "arbitrary" if n_rowsum else "parallel", "arbitrary")),
    )(*args)
    return res[0] if n_out + n_rowsum == 1 else res


def _rowwise(fn, *, n_rows, ts, name, rows=(), prevs=(), nexts=(), vecs=(), row_outs=(), acc_outs=(), after=()):
    ts = min(ts, n_rows)
    assert n_rows % ts == 0
    nblk = n_rows // ts
    in_specs, args = [], []
    for arr, cb, w in rows:
        in_specs.append(pl.BlockSpec((ts, w), functools.partial(lambda i, cb: (i, cb), cb=cb)))
        args.append(arr)
    for arr, cb, w, halo in prevs:
        per = ts // halo
        in_specs.append(pl.BlockSpec(
            (halo, w), functools.partial(lambda i, cb, per: (jnp.maximum(i * per - 1, 0), cb), cb=cb, per=per)))
        args.append(arr)
    for arr, cb, w, halo in nexts:
        per = ts // halo
        last_blk = n_rows // halo - 1
        in_specs.append(pl.BlockSpec(
            (halo, w), functools.partial(lambda i, cb, per, lb: (jnp.minimum((i + 1) * per, lb), cb),
                                         cb=cb, per=per, lb=last_blk)))
        args.append(arr)
    for arr in vecs:
        in_specs.append(pl.BlockSpec(arr.shape, functools.partial(lambda i, nd: (0,) * nd, nd=arr.ndim)))
        args.append(arr)
    out_specs, out_shape = [], []
    for w, dt in row_outs:
        out_specs.append(pl.BlockSpec((ts, w), lambda i: (i, 0)))
        out_shape.append(jax.ShapeDtypeStruct((n_rows, w), dt))
    for shp in acc_outs:
        out_specs.append(pl.BlockSpec(shp, functools.partial(lambda i, nd: (0,) * nd, nd=len(shp))))
        out_shape.append(jax.ShapeDtypeStruct(shp, F32))
    n_used = len(args)
    n_tiles = n_used - len(vecs)
    in_specs += [_ANY_SPEC] * len(after)
    args += list(after)
    n_in, n_ro, n_acc = len(args), len(row_outs), len(acc_outs)

    def body(*refs):
        ins, ro, ac = refs[:n_used], refs[n_in:n_in + n_ro], refs[n_in + n_ro:]
        i = pl.program_id(0)
        rvals, avals = fn(i == 0, i == nblk - 1, *[r[...] for r in ins[:n_tiles]], *ins[n_tiles:])
        for r, val in zip(ro, rvals):
            r[...] = val.astype(r.dtype)
        if n_acc:
            @pl.when(i == 0)
            def _():
                for r in ac:
                    r[...] = jnp.zeros_like(r)

            for r, val in zip(ac, avals):
                r[...] += val

    res = pl.pallas_call(
        body, name=name, grid=(nblk,), in_specs=in_specs, out_specs=out_specs, out_shape=out_shape,
        compiler_params=pltpu.CompilerParams(dimension_semantics=("arbitrary",)),
    )(*args)
    return res


def _delta_fwd(q, k, v, gb, heads):
    s, hd = q.shape
    n = s // DN_CHUNK
    blk = pl.BlockSpec((DN_CHUNK, hd), lambda c: (c, 0))
    gspec = pl.BlockSpec((DN_CHUNK, LANES), lambda c: (c, 0))

    def body(q_ref, k_ref, v_ref, gb_ref, o_ref, st_ref, tm_ref, state):
        @pl.when(pl.program_id(0) == 0)
        def _():
            state[...] = jnp.zeros_like(state)

        s0 = state[...]
        st_ref[0] = s0
        o, s1, tm = _delta_chunk(*_split_heads(q_ref, k_ref, v_ref, gb_ref[...], heads), s0)
        for h in range(heads):
            o_ref[:, h * DN_HEAD_DIM:(h + 1) * DN_HEAD_DIM] = o[h]
        state[...] = s1
        tm_ref[0] = tm

    return pl.pallas_call(
        body, name="dn_delta_fwd", grid=(n,),
        in_specs=[blk, blk, blk, gspec],
        out_specs=[blk, pl.BlockSpec((1, heads, DN_HEAD_DIM, DN_HEAD_DIM), lambda c: (c, 0, 0, 0)),
                   pl.BlockSpec((1, heads, DN_CHUNK, DN_CHUNK), lambda c: (c, 0, 0, 0))],
        out_shape=[jax.ShapeDtypeStruct((s, hd), F32),
                   jax.ShapeDtypeStruct((n, heads, DN_HEAD_DIM, DN_HEAD_DIM), F32),
                   jax.ShapeDtypeStruct((n, heads, DN_CHUNK, DN_CHUNK), F32)],
        scratch_shapes=[pltpu.VMEM((heads, DN_HEAD_DIM, DN_HEAD_DIM), F32)],
        compiler_params=pltpu.CompilerParams(dimension_semantics=("arbitrary",)),
    )(q, k, v, gb)


def _split_heads(q_ref, k_ref, v_ref, gbv, heads):
    def hs(ref):
        return jnp.stack([ref[:, h * DN_HEAD_DIM:(h + 1) * DN_HEAD_DIM] for h in range(heads)])

    gcol = jnp.stack([gbv[:, heads + h:heads + h + 1] for h in range(heads)])
    bcol = jnp.stack([gbv[:, h:h + 1] for h in range(heads)])
    return hs(q_ref), hs(k_ref), hs(v_ref), gcol, bcol


def _delta_bwd(q, k, v, gb, states, tms, do, heads):
    s, hd = q.shape
    n = s // DN_CHUNK
    blk = pl.BlockSpec((DN_CHUNK, hd), lambda c: (n - 1 - c, 0))
    gspec = pl.BlockSpec((DN_CHUNK, LANES), lambda c: (n - 1 - c, 0))
    sspec = pl.BlockSpec((1, heads, DN_HEAD_DIM, DN_HEAD_DIM), lambda c: (n - 1 - c, 0, 0, 0))
    tspec = pl.BlockSpec((1, heads, DN_CHUNK, DN_CHUNK), lambda c: (n - 1 - c, 0, 0, 0))

    def body(q_ref, k_ref, v_ref, gb_ref, st_ref, tm_ref, do_ref, dq_ref, dk_ref, dv_ref, dgb_ref, dstate):
        @pl.when(pl.program_id(0) == 0)
        def _():
            dstate[...] = jnp.zeros_like(dstate)

        gbv = gb_ref[...]
        tm = tm_ref[0]

        def chunk(qh, kh, vh, gcol, bcol, s0):
            return _delta_chunk(qh, kh, vh, gcol, bcol, s0, tm)[:2]

        _, vjp = jax.vjp(chunk, *_split_heads(q_ref, k_ref, v_ref, gbv, heads), st_ref[0])
        doh = jnp.stack([do_ref[:, h * DN_HEAD_DIM:(h + 1) * DN_HEAD_DIM] for h in range(heads)])
        dq, dk, dv, dg, db, ds0 = vjp((doh, dstate[...]))
        dstate[...] = ds0
        lane = lax.broadcasted_iota(jnp.int32, gbv.shape, 1)
        dgb = jnp.zeros(gbv.shape, F32)
        for h in range(heads):
            sl = slice(h * DN_HEAD_DIM, (h + 1) * DN_HEAD_DIM)
            dq_ref[:, sl] = dq[h]
            dk_ref[:, sl] = dk[h]
            dv_ref[:, sl] = dv[h]
            dgb = dgb + jnp.where(lane == h, db[h], 0.0) + jnp.where(lane == heads + h, dg[h], 0.0)
        dgb_ref[...] = dgb

    return pl.pallas_call(
        body, name="dn_delta_bwd", grid=(n,),
        in_specs=[blk, blk, blk, gspec, sspec, tspec, blk],
        out_specs=[blk, blk, blk, gspec],
        out_shape=[jax.ShapeDtypeStruct((s, hd), F32)] * 3 + [jax.ShapeDtypeStruct((s, LANES), F32)],
        scratch_shapes=[pltpu.VMEM((heads, DN_HEAD_DIM, DN_HEAD_DIM), F32)],
        compiler_params=pltpu.CompilerParams(dimension_semantics=("arbitrary",)),
    )(q, k, v, gb, states, tms, do)


def _dev_index(px, py, pc):
    return 4 * px + 2 * py + pc


def _all_gather(arrs, name):
    n = len(arrs)

    def body(*refs):
        xs, outs = refs[:n], refs[n:2 * n]
        send_sems, recv_sems, local_sems = refs[2 * n:]
        x, y, c = lax.axis_index("x"), lax.axis_index("y"), lax.axis_index("c")
        me, sibling = (x, y, c), (x, y, 1 - c)
        chips = [(1 - x, y), (x, 1 - y), (1 - x, 1 - y)]

        def copy(a, kk, block, to, src=None):
            dst = outs[a].at[_dev_index(*block)]
            return pltpu.make_async_remote_copy(
                src_ref=dst if src is None else src, dst_ref=dst,
                send_sem=send_sems.at[a * 7 + kk], recv_sem=recv_sems.at[a * 7 + kk],
                device_id=to, device_id_type=MESH_IDS)

        mine = [pltpu.make_async_copy(xs[a], outs[a].at[_dev_index(*me)], local_sems.at[a]) for a in range(n)]
        for cp in mine:
            cp.start()
        first = []
        for a in range(n):
            first.append(copy(a, 0, me, sibling, src=xs[a]))
            first += [copy(a, 1 + j, me, (*chip, c), src=xs[a]) for j, chip in enumerate(chips)]
        for cp in first:
            cp.start()
        passed = []
        for j, chip in enumerate(chips):
            for a in range(n):
                copy(a, 1 + j, (*chip, c), me).wait_recv()
                fwd = copy(a, 4 + j, (*chip, c), sibling)
                fwd.start()
                passed.append(fwd)
        for a in range(n):
            copy(a, 0, sibling, me).wait_recv()
        for j, chip in enumerate(chips):
            for a in range(n):
                copy(a, 4 + j, (*chip, 1 - c), me).wait_recv()
        for cp in first + passed:
            cp.wait_send()
        for cp in mine:
            cp.wait()

    hbm = pl.BlockSpec(memory_space=pltpu.HBM)
    res = pl.pallas_call(
        body, name=name,
        in_specs=[hbm] * n, out_specs=[hbm] * n,
        out_shape=[jax.ShapeDtypeStruct((N_DEV,) + a.shape, a.dtype) for a in arrs],
        scratch_shapes=[pltpu.SemaphoreType.DMA((7 * n,)), pltpu.SemaphoreType.DMA((7 * n,)),
                        pltpu.SemaphoreType.DMA((n,))],
    )(*arrs)
    return list(res)


_FLIPS = ((0, 0, 1), (1, 0, 0), (0, 1, 0), (1, 1, 0), (1, 0, 1), (0, 1, 1), (1, 1, 1))
_HBM_SPEC = pl.BlockSpec(memory_space=pltpu.HBM)
_SEM_SPEC = pl.BlockSpec(memory_space=pltpu.SEMAPHORE)
_ANY_SPEC = pl.BlockSpec(memory_space=pl.ANY)
_DATAFLOW = pltpu.SideEffectType.DATAFLOW_SIDE_EFFECTING
TOKEN_SHAPE = (8, LANES)


def _mesh_me():
    return lax.axis_index("x"), lax.axis_index("y"), lax.axis_index("c")


def _flipped(me, f):
    return tuple(1 - v if fl else v for v, fl in zip(me, f))


def _exchange_copies(xs, lands, send_sems, recv_sems, scatter, landed):
    me = _mesh_me()
    cps = []
    for kk, f in enumerate(_FLIPS):
        p = _flipped(me, f)
        for a in range(len(xs)):
            cps.append(pltpu.make_async_remote_copy(
                src_ref=xs[a].at[_dev_index(*p)] if scatter else xs[a],
                dst_ref=lands[a].at[_dev_index(*(p if landed else me))],
                send_sem=send_sems.at[a * 7 + kk], recv_sem=recv_sems.at[a * 7 + kk],
                device_id=p, device_id_type=MESH_IDS))
    return cps


def _exchange_start(srcs, lands, scatter, name, after=()):
    n = len(srcs)

    n_after = len(after)

    def body(*refs):
        xs, ls = refs[:n], refs[n:2 * n]
        send_sems, recv_sems = refs[2 * n + n_after], refs[2 * n + n_after + 1]
        token = refs[-1]
        for cp in _exchange_copies(xs, ls, send_sems, recv_sems, scatter, landed=False):
            cp.start()
        token[...] = jnp.zeros_like(token)

    operands = [pltpu.with_memory_space_constraint(a, pltpu.HBM) for a in list(srcs) + list(lands)]
    res = pl.pallas_call(
        body, name=name,
        in_specs=[_HBM_SPEC] * (2 * n) + [_ANY_SPEC] * len(after),
        out_specs=[_SEM_SPEC, _SEM_SPEC] + [_HBM_SPEC] * (2 * n) + [pl.BlockSpec(memory_space=pltpu.VMEM)],
        out_shape=[pltpu.SemaphoreType.DMA((7 * n,)), pltpu.SemaphoreType.DMA((7 * n,))]
        + [pltpu.HBM(a.shape, a.dtype) for a in operands] + [jax.ShapeDtypeStruct(TOKEN_SHAPE, F32)],
        input_output_aliases={i: 2 + i for i in range(2 * n)},
        compiler_params=pltpu.CompilerParams(has_side_effects=_DATAFLOW),
    )(*operands, *after)
    return (res[0], res[1], list(res[2:2 + n]), list(res[2 + n:2 + 2 * n]), scatter, name), res[-1]


def _exchange_wait(handle, after):
    send_sems, recv_sems, srcs, lands, scatter, name = handle
    n = len(srcs)
    n_after = len(after)

    def body(*refs):
        xs, ls = refs[:n], refs[n:2 * n]
        send_sems_ref, recv_sems_ref = refs[2 * n], refs[2 * n + 1]
        for cp in _exchange_copies(xs, ls, send_sems_ref, recv_sems_ref, scatter, landed=True):
            cp.wait_send()
            cp.wait_recv()

    res = pl.pallas_call(
        body, name=name + "_wait",
        in_specs=[_HBM_SPEC] * (2 * n) + [_SEM_SPEC, _SEM_SPEC] + [_ANY_SPEC] * n_after,
        out_specs=[_HBM_SPEC] * (2 * n),
        out_shape=[pltpu.HBM(a.shape, a.dtype) for a in srcs + lands],
        input_output_aliases={i: i for i in range(2 * n)},
        compiler_params=pltpu.CompilerParams(has_side_effects=_DATAFLOW),
    )(*srcs, *lands, send_sems, recv_sems, *after)
    return list(res[:n]), list(res[n:])


def _slot_sum(g, name, tr):
    _, r, c = g.shape
    tr = min(tr, r)
    assert r % tr == 0

    def body(g_ref, o_ref):
        acc = g_ref[0].astype(F32)
        for s in range(1, N_DEV):
            acc = acc + g_ref[s].astype(F32)
        o_ref[...] = acc

    return pl.pallas_call(
        body, name=name, grid=(r // tr,),
        in_specs=[pl.BlockSpec((N_DEV, tr, c), lambda i: (0, i, 0))],
        out_specs=pl.BlockSpec((tr, c), lambda i: (i, 0)),
        out_shape=jax.ShapeDtypeStruct((r, c), F32),
        compiler_params=pltpu.CompilerParams(dimension_semantics=("parallel",)),
    )(g)


def _adam_update(w, gg, m, v):
    c1 = 1.0 / (1.0 - ADAM_B1 ** ADAM_STEP)
    c2 = 1.0 / (1.0 - ADAM_B2 ** ADAM_STEP)
    nm = ADAM_B1 * m + (1.0 - ADAM_B1) * gg
    nv = ADAM_B2 * v + (1.0 - ADAM_B2) * (gg * gg)
    return -ADAM_LR * ((nm * c1) / (jnp.sqrt(nv * c2) + ADAM_EPS) + ADAM_WD * w), nm, nv


def _adamw_reduce(me, recvs, owns, w, m, v, name, tr=128):
    nl, r, c = w.shape
    assert len(recvs) == nl and len(owns) == nl
    tr = min(tr, r)
    assert r % tr == 0
    nblk = r // tr

    def parked(li, l, i):
        return jnp.where(l < li, 0, jnp.where(l > li, nblk - 1, i))

    def recv_spec(li):
        return pl.BlockSpec((N_DEV, tr, c), lambda l, i, me_ref: (0, parked(li, l, i), 0))

    def own_spec(li):
        return pl.BlockSpec((None, tr, c), lambda l, i, me_ref: (me_ref[0], parked(li, l, i), 0))

    def body(me_ref, *refs):
        rrefs, orefs = refs[:nl], refs[nl:2 * nl]
        w_ref, m_ref, v_ref, g_ref, d_ref, nm_ref, nv_ref = refs[2 * nl:]
        l = pl.program_id(0)

        def of_layer(vals):
            out = vals[0]
            for li in range(1, nl):
                out = jnp.where(l == li, vals[li], out)
            return out

        own = of_layer([o[...].astype(F32) for o in orefs])
        gg = None
        for s in range(N_DEV):
            slot = jnp.where(me_ref[0] == s, own, of_layer([rr[s].astype(F32) for rr in rrefs]))
            gg = slot if gg is None else gg + slot
        g_ref[...] = gg
        d_ref[...], nm_ref[...], nv_ref[...] = _adam_update(w_ref[...], gg, m_ref[...], v_ref[...])

    spec = pl.BlockSpec((None, tr, c), lambda l, i, me_ref: (l, i, 0))
    return pl.pallas_call(
        body, name=name,
        grid_spec=pltpu.PrefetchScalarGridSpec(
            num_scalar_prefetch=1, grid=(nl, nblk),
            in_specs=[recv_spec(li) for li in range(nl)] + [own_spec(li) for li in range(nl)] + [spec] * 3,
            out_specs=[spec] * 4),
        out_shape=[jax.ShapeDtypeStruct((nl, r, c), F32)] * 4,
        compiler_params=pltpu.CompilerParams(dimension_semantics=("arbitrary", "arbitrary")),
    )(me, *recvs, *owns, w, m, v)


def _adamw(w, g, m, v, name, tr=256):
    r, c = w.shape
    tr = min(tr, r)
    assert r % tr == 0

    def body(w_ref, g_ref, m_ref, v_ref, d_ref, nm_ref, nv_ref):
        d_ref[...], nm_ref[...], nv_ref[...] = _adam_update(w_ref[...], g_ref[...], m_ref[...], v_ref[...])

    spec = pl.BlockSpec((tr, c), lambda i: (i, 0))
    return pl.pallas_call(
        body, name=name, grid=(r // tr,), in_specs=[spec] * 4, out_specs=[spec] * 3,
        out_shape=[jax.ShapeDtypeStruct((r, c), F32)] * 3,
        compiler_params=pltpu.CompilerParams(dimension_semantics=("parallel",)),
    )(w, g, m, v)


def _rms_fwd(x, g, name, ts=512, after=()):
    s, d = x.shape

    def fn(first, last, xv, gv):
        return [_rms(xv, gv[...])], []

    return _rowwise(fn, n_rows=s, ts=ts, name=name, rows=[(x, 0, d)], vecs=[g], row_outs=[(d, BF16)],
                    after=after)[0]


def _rms_bwd(x, dn, dres, g, name, ts=256):
    s, d = x.shape

    def fn(first, last, xv, dnv, drv, gv):
        _, vjp = jax.vjp(_rms, xv, gv[...])
        dx, dg = vjp(dnv.astype(F32))
        return [drv + dx], [dg]

    return _rowwise(fn, n_rows=s, ts=ts, name=name, rows=[(x, 0, d), (dn, 0, d), (dres, 0, d)], vecs=[g],
                    row_outs=[(d, F32)], acc_outs=[(1, d)])


def _loss_head(h, tgt, g, ts=256):
    s, d = h.shape

    def fn(first, last, hv, tv, gv):
        def f(hh, gg):
            e = _rms(hh, gg) - tv
            per_row = jnp.mean(e * e, axis=-1, keepdims=True)
            return 0.5 * jnp.sum(per_row, axis=0, keepdims=True)

        l, vjp = jax.vjp(f, hv, gv[...])
        dh, dg = vjp(jnp.ones((1, 1), F32))
        return [dh], [dg, jnp.zeros((1, LANES), F32) + l]

    return _rowwise(fn, n_rows=s, ts=ts, name="loss_head", rows=[(h, 0, d), (tgt, 0, d)], vecs=[g],
                    row_outs=[(d, F32)], acc_outs=[(1, d), (1, LANES)])


def _dn_pre_fwd(qkvz, ba, wconv, alog, dt, heads, ts=128):
    s = qkvz.shape[0]
    d3 = wconv.shape[1]
    d = d3 // 3

    def fn(first, last, xc, bav, xp, wv, av, dv):
        xext = jnp.concatenate([jnp.where(first, 0.0, xp), xc], axis=0)
        cv = _causal_conv(xext, wv, DN_CONV, DN_HALO - (DN_CONV - 1), xc.shape[0])
        return list(_dn_point(cv, bav, av[...], dv[...], heads)), []

    return _rowwise(fn, n_rows=s, ts=ts, name="dn_pre_fwd", rows=[(qkvz, 0, d3), (ba, 0, LANES)],
                    prevs=[(qkvz, 0, d3, DN_HALO)], vecs=[wconv, alog, dt],
                    row_outs=[(d, F32), (d, F32), (d, F32), (LANES, F32)])


def _dn_pre_bwd1(qkvz, ba, wconv, alog, dt, dq, dk, dv, dgb, heads, ts=128):
    s = qkvz.shape[0]
    d3 = wconv.shape[1]
    d = d3 // 3

    def fn(first, last, xc, bav, dqv, dkv, dvv, dgbv, xp, wv, av, dtv):
        xext = jnp.concatenate([jnp.where(first, 0.0, xp), xc], axis=0)
        cv = _causal_conv(xext, wv, DN_CONV, DN_HALO - (DN_CONV - 1), xc.shape[0])
        _, vjp = jax.vjp(functools.partial(_dn_point, heads=heads), cv, bav, av[...], dtv[...])
        dc, dba, da, ddt = vjp((dqv, dkv, dvv, dgbv))
        return [dc, dba], [da, ddt]

    return _rowwise(fn, n_rows=s, ts=ts, name="dn_pre_bwd1",
                    rows=[(qkvz, 0, d3), (ba, 0, LANES), (dq, 0, d), (dk, 0, d), (dv, 0, d), (dgb, 0, LANES)],
                    prevs=[(qkvz, 0, d3, DN_HALO)], vecs=[wconv, alog, dt],
                    row_outs=[(d3, F32), (LANES, BF16)], acc_outs=[(1, LANES), (1, LANES)])


def _dn_pre_bwd2(dc, qkvz, dz, wconv, ts=128):
    s = qkvz.shape[0]
    d3 = wconv.shape[1]
    d = d3 // 3

    def fn(first, last, dcc, xc, dzv, xp, dcn, wv):
        n = dcc.shape[0]
        dcext = jnp.concatenate([dcc, jnp.where(last, 0.0, dcn)], axis=0)
        xext = jnp.concatenate([jnp.where(first, 0.0, xp), xc], axis=0)
        dx = None
        dw = []
        for j in range(DN_CONV):
            term = _shift_rows(dcext, DN_CONV - 1 - j)[:n] * wv[j:j + 1, :]
            dx = term if dx is None else dx + term
            dw.append(_colsum(dcc * _shift_rows(xext, DN_HALO - (DN_CONV - 1) + j)[:n]))
        return [jnp.concatenate([dx, dzv], axis=-1)], [_stack_rows(dw, DN_CONV)]

    return _rowwise(fn, n_rows=s, ts=ts, name="dn_pre_bwd2",
                    rows=[(dc, 0, d3), (qkvz, 0, d3), (dz, 0, d)],
                    prevs=[(qkvz, 0, d3, DN_HALO)], nexts=[(dc, 0, d3, DN_HALO)], vecs=[wconv],
                    row_outs=[(4 * d, BF16)], acc_outs=[(DN_CONV, d3)])


def _dn_post_fwd(o, qkvz, onorm, heads, ts=256):
    s, d = o.shape

    def fn(first, last, ov, zv, nv):
        return [_dn_post(ov, zv, nv[...], heads)], []

    return _rowwise(fn, n_rows=s, ts=ts, name="dn_post_fwd", rows=[(o, 0, d), (qkvz, 3, d)], vecs=[onorm],
                    row_outs=[(d, BF16)])[0]


def _dn_post_bwd(o, qkvz, onorm, dog, heads, ts=256):
    s, d = o.shape

    def fn(first, last, ov, zv, dv, nv):
        _, vjp = jax.vjp(functools.partial(_dn_post, heads=heads), ov, zv, nv[...])
        do, dz, dn = vjp(dv.astype(F32))
        return [do, dz], [dn]

    return _rowwise(fn, n_rows=s, ts=ts, name="dn_post_bwd", rows=[(o, 0, d), (qkvz, 3, d), (dog, 0, d)],
                    vecs=[onorm], row_outs=[(d, F32), (d, F32)], acc_outs=[(1, DN_HEAD_DIM)])


def _cv_mid_fwd(u, wdw, bdw, lng, lnb, ts=256):
    s = u.shape[0]
    d = u.shape[1] // 2

    def fn(first, last, uc, up, wv, bv, gv, lbv):
        uext = jnp.concatenate([jnp.where(first, 0.0, up), uc], axis=0)
        glu = uext[:, :d] * _sigmoid(uext[:, d:])
        c = _causal_conv(glu, wv, CV_WIDTH, CV_HALO - (CV_WIDTH - 1), uc.shape[0]) + bv[...]
        return [c, _ln_silu(c, gv[...], lbv[...])], []

    return _rowwise(fn, n_rows=s, ts=ts, name="cv_mid_fwd", rows=[(u, 0, 2 * d)], prevs=[(u, 0, 2 * d, CV_HALO)],
                    vecs=[wdw, bdw, lng, lnb], row_outs=[(d, F32), (d, BF16)])


def _cv_mid_bwd1(c, ds, dhp, lng, lnb, ts=256):
    s, d = c.shape

    def fn(first, last, cv, dsv, dhv, gv, bv):
        _, vjp = jax.vjp(_ln_silu, cv, gv[...], bv[...])
        dc, dg, db = vjp(dsv.astype(F32))
        return [dc], [dg, db, _colsum(dc), _colsum(dhv)]

    return _rowwise(fn, n_rows=s, ts=ts, name="cv_mid_bwd1", rows=[(c, 0, d), (ds, 0, d), (dhp, 0, d)],
                    vecs=[lng, lnb], row_outs=[(d, F32)], acc_outs=[(1, d)] * 4)


def _cv_mid_bwd2(dc, u, wdw, ts=256):
    s, d = dc.shape

    def fn(first, last, dcc, uc, up, dcn, wv):
        n = dcc.shape[0]
        dcext = jnp.concatenate([dcc, jnp.where(last, 0.0, dcn)], axis=0)
        uext = jnp.concatenate([jnp.where(first, 0.0, up), uc], axis=0)
        glu = uext[:, :d] * _sigmoid(uext[:, d:])
        dglu = None
        dw = []
        for j in range(CV_WIDTH):
            term = _shift_rows(dcext, CV_WIDTH - 1 - j)[:n] * wv[j:j + 1, :]
            dglu = term if dglu is None else dglu + term
            dw.append(_colsum(dcc * _shift_rows(glu, CV_HALO - (CV_WIDTH - 1) + j)[:n]))
        u1, sg = uc[:, :d], _sigmoid(uc[:, d:])
        du = jnp.concatenate([dglu * sg, dglu * u1 * sg * (1.0 - sg)], axis=-1)
        return [du], [_stack_rows(dw, CV_HALO), _colsum(du)]

    return _rowwise(fn, n_rows=s, ts=ts, name="cv_mid_bwd2", rows=[(dc, 0, d), (u, 0, 2 * d)],
                    prevs=[(u, 0, 2 * d, CV_HALO)], nexts=[(dc, 0, d, CV_HALO)], vecs=[wdw],
                    row_outs=[(2 * d, BF16)], acc_outs=[(CV_HALO, d), (1, 2 * d)])


def _attn_fwd(q, k, v, name, ts=256):
    s, d = q.shape

    def fn(first, last, qv, kv, vv):
        return [_attn_tile(qv.astype(F32), kv[...].astype(F32), vv[...].astype(F32))], []

    return _rowwise(fn, n_rows=s, ts=ts, name=name, rows=[(q, 0, d)], vecs=[k, v], row_outs=[(d, BF16)])[0]


def _attn_bwd(q, k, v, do, name, ts=256):
    s, d = q.shape
    m = k.shape[0]

    def fn(first, last, qv, dov, kv, vv):
        _, vjp = jax.vjp(_attn_tile, qv.astype(F32), kv[...].astype(F32), vv[...].astype(F32))
        dq, dk, dv = vjp(dov.astype(F32))
        return [dq], [dk, dv]

    return _rowwise(fn, n_rows=s, ts=ts, name=name, rows=[(q, 0, d), (do, 0, d)], vecs=[k, v],
                    row_outs=[(d, BF16)], acc_outs=[(m, d), (m, d)])


def _pad_lanes(a, off=0):
    r, n = a.shape
    return jnp.pad(a, ((0, 0), (off, LANES - off - n)))


def _local_step(x, mem, tgt, w, fetch=None, emit=None, first_after=()):
    s, d = x.shape
    heads = d // DN_HEAD_DIM
    g = {}
    if fetch is None:
        fetch = lambda group, after: None
    if emit is None:
        emit = lambda group, grads: ()

    def add_res(acc, res):
        return (res + acc,)

    def add_res_rms(acc, res, gain):
        h = res + acc
        return h, _rms(h, gain)

    def rms_bwd_epi(acc, hx, dres, gain):
        _, vjp = jax.vjp(_rms, hx, gain)
        dx, dg = vjp(acc)
        return dres + dx, dg

    w_in = w["dn_w_in"][0]
    assert w_in.shape[1] == 4 * d + 2 * heads
    w_qkvz = w_in[:, :4 * d]
    w_ba = _pad_lanes(w_in[:, 4 * d:])
    dn_norm = w["dn_norm"]
    alog = _pad_lanes(w["dn_a_log"], heads)
    dtb = _pad_lanes(w["dn_dt_bias"], heads)
    wconv = w["dn_w_conv"][0]
    n0 = _rms_fwd(x, dn_norm, "dn_rms", after=first_after)
    qkvz = _matmul(n0, w_qkvz, "nn", [F32], name="dn_in_proj")
    ba = _matmul(n0, w_ba, "nn", [F32], name="dn_in_proj_ba")
    q, k, v, gb = _dn_pre_fwd(qkvz, ba, wconv, alog, dtb, heads)
    o, states, tms = _delta_fwd(q, k, v, gb, heads)
    og = _dn_post_fwd(o, qkvz, w["dn_out_norm"], heads)
    fetch(1, [og])
    h1, nq0 = _matmul(og, w["dn_w_out"][0], "nn", [F32, BF16], name="dn_out_proj", epi=add_res_rms,
                      mn_extras=[x], row_extras=[w["xa_norm"][0:1]], slab=EPI_SLAB)

    def xattn_fwd(h, nq, layer, next_gain):
        qx = _matmul(nq, w["xa_w_q"][layer], "nn", [BF16], name=f"xa{layer}_q")
        mn = _rms_fwd(mem, w["xa_mem_norm"][layer:layer + 1], f"xa{layer}_mem_rms")
        kv = _matmul(mn, w["xa_w_kv"][layer], "nn", [BF16], name=f"xa{layer}_kv")
        kx, vx = kv[:, :d], kv[:, d:]
        ox = _attn_fwd(qx, kx, vx, f"xa{layer}_attn")
        hn, nn = _matmul(ox, w["xa_w_o"][layer], "nn", [F32, BF16], name=f"xa{layer}_o", epi=add_res_rms,
                         mn_extras=[h], row_extras=[next_gain], slab=EPI_SLAB)
        return hn, nn, (h, nq, qx, mn, kx, vx, ox)

    def mlp_fwd(h, nm, layer, next_gain):
        def epi(acc):
            r = jnp.maximum(acc, 0.0)
            return acc, r * r

        u, a = _matmul(nm, w["mlp_w_up"][layer], "nn", [BF16, BF16], name=f"mlp{layer}_up", epi=epi)
        if next_gain is None:
            hn, nn = _matmul(a, w["mlp_w_down"][layer], "nn", [F32], name=f"mlp{layer}_down", epi=add_res,
                             mn_extras=[h], tk=2048), None
        else:
            hn, nn = _matmul(a, w["mlp_w_down"][layer], "nn", [F32, BF16], name=f"mlp{layer}_down",
                             epi=add_res_rms, mn_extras=[h], row_extras=[next_gain], tk=2048, slab=EPI_SLAB)
        return hn, nn, (h, nm, u, a)

    h2, nm0, xa0 = xattn_fwd(h1, nq0, 0, w["mlp_norm"][0:1])
    fetch(2, [h2])
    h3, n1, mlp0 = mlp_fwd(h2, nm0, 0, w["cv_norm"])

    u_cv = _matmul(n1, w["cv_w_pw1"][0], "nn", [F32], name="cv_pw1", epi=lambda acc, b: (acc + b,),
                   row_extras=[w["cv_b_pw1"]])
    wdw = jnp.pad(w["cv_w_dw"][0], ((0, CV_HALO - CV_WIDTH), (0, 0)))
    c_cv, s_cv = _cv_mid_fwd(u_cv, wdw, w["cv_b_dw"], w["cv_ln_g"], w["cv_ln_b"])
    fetch(3, [s_cv])
    h4, nq1 = _matmul(s_cv, w["cv_w_pw2"][0], "nn", [F32, BF16], name="cv_pw2",
                      epi=lambda acc, res, b, gain: add_res_rms(acc + b, res, gain), mn_extras=[h3],
                      row_extras=[w["cv_b_pw2"], w["xa_norm"][1:2]], slab=EPI_SLAB)
    h5, nm1, xa1 = xattn_fwd(h4, nq1, 1, w["mlp_norm"][1:2])
    h6, _, mlp1 = mlp_fwd(h5, nm1, 1, None)

    fnorm = w["final_norm"].reshape(1, d)
    dh, g_fn, loss = _loss_head(h6, tgt, fnorm)
    g["final_norm"] = g_fn.reshape(d)

    def mlp_bwd(dh, layer, saved, after=()):
        h, nm, u, a = saved
        du = _matmul(dh, w["mlp_w_down"][layer], "nt", [BF16], name=f"mlp{layer}_down_dx", after=after,
                     epi=lambda acc, uu: (acc * 2.0 * jnp.maximum(uu.astype(F32), 0.0),), mn_extras=[u])
        gdown = _matmul(a, dh, "tn", [BF16], name=f"mlp{layer}_down_dw", tm=2048)
        dhn, gn = _matmul(du, w["mlp_w_up"][layer], "nt", [F32], name=f"mlp{layer}_up_dx", epi=rms_bwd_epi,
                          mn_extras=[h, dh], row_extras=[w["mlp_norm"][layer:layer + 1]], n_rowsum=1,
                          slab=EPI_SLAB)
        gup = _matmul(nm, du, "tn", [BF16], name=f"mlp{layer}_up_dw", out_dm=True, tk=2048)
        return dhn, gup, gdown, gn

    def xattn_bwd(dh, layer, saved):
        h, nq, qx, mn, kx, vx, ox = saved
        dox = _matmul(dh, w["xa_w_o"][layer], "nt", [BF16], name=f"xa{layer}_o_dx")
        go = _matmul(ox, dh, "tn", [BF16], name=f"xa{layer}_o_dw", tk=2048)
        dqx, dkx, dvx = _attn_bwd(qx, kx, vx, dox, f"xa{layer}_attn_bwd")
        dhn, gn = _matmul(dqx, w["xa_w_q"][layer], "nt", [F32], name=f"xa{layer}_q_dx", epi=rms_bwd_epi,
                          mn_extras=[h, dh], row_extras=[w["xa_norm"][layer:layer + 1]], n_rowsum=1,
                          slab=EPI_SLAB)
        gq = _matmul(nq, dqx, "tn", [BF16], name=f"xa{layer}_q_dw", tk=2048)
        dkv = jnp.concatenate([dkx, dvx], axis=-1)
        gkv = _matmul(mn, dkv, "tn", [BF16], name=f"xa{layer}_kv_dw", out_dm=True)
        dmn = _matmul(dkv, w["xa_w_kv"][layer], "nt", [F32], name=f"xa{layer}_kv_dx", tk=2048)
        _, gmem = _rms_bwd(mem, dmn, dmn, w["xa_mem_norm"][layer:layer + 1], f"xa{layer}_mem_rms_bwd")
        return dhn, gq, gkv, go, gn, gmem

    dh, gup1, gdown1, gmn1 = mlp_bwd(dh, 1, mlp1)
    dh, gq1, gkv1, go1, gxn1, gmem1 = xattn_bwd(dh, 1, xa1)
    g.update(mlp_w_up=[None, gup1], mlp_w_down=[None, gdown1], xa_w_q=[None, gq1], xa_w_kv=[None, gkv1],
             xa_w_o=[None, go1])
    tok = emit(3, g)

    ds_cv = _matmul(dh, w["cv_w_pw2"][0], "nt", [BF16], name="cv_pw2_dx", after=tok)
    g["cv_w_pw2"] = [_matmul(s_cv, dh, "tn", [BF16], name="cv_pw2_dw", tk=2048)]
    dc_cv, g_lng, g_lnb, g_bdw, g_b2 = _cv_mid_bwd1(c_cv, ds_cv, dh, w["cv_ln_g"], w["cv_ln_b"])
    du_cv, g_wdw, g_b1 = _cv_mid_bwd2(dc_cv, u_cv, wdw)
    g["cv_w_pw1"] = [_matmul(n1, du_cv, "tn", [BF16], name="cv_pw1_dw", out_dm=True, tk=2048)]
    dh, g_cvn = _matmul(du_cv, w["cv_w_pw1"][0], "nt", [F32], name="cv_pw1_dx", epi=rms_bwd_epi,
                        mn_extras=[h3, dh], row_extras=[w["cv_norm"]], n_rowsum=1, slab=EPI_SLAB)
    g.update(cv_ln_g=g_lng, cv_ln_b=g_lnb, cv_b_dw=g_bdw, cv_b_pw2=g_b2, cv_b_pw1=g_b1, cv_norm=g_cvn,
             cv_w_dw=g_wdw[:CV_WIDTH][None])

    tok = emit(2, g)
    dh, gup0, gdown0, gmn0 = mlp_bwd(dh, 0, mlp0, after=tok)
    dh, gq0, gkv0, go0, gxn0, gmem0 = xattn_bwd(dh, 0, xa0)
    g["mlp_w_up"][0] = gup0
    g["mlp_w_down"][0] = gdown0
    g["mlp_norm"] = jnp.concatenate([gmn0, gmn1], axis=0)
    g["xa_w_q"][0] = gq0
    g["xa_w_kv"][0] = gkv0
    g["xa_w_o"][0] = go0
    g["xa_norm"] = jnp.concatenate([gxn0, gxn1], axis=0)
    g["xa_mem_norm"] = jnp.concatenate([gmem0, gmem1], axis=0)
    tok = emit(1, g)

    dog = _matmul(dh, w["dn_w_out"][0], "nt", [BF16], name="dn_out_proj_dx", after=tok)
    g["dn_w_out"] = [_matmul(og, dh, "tn", [BF16], name="dn_out_proj_dw", tk=2048)]
    do, dz, g_on = _dn_post_bwd(o, qkvz, w["dn_out_norm"], dog, heads)
    dq, dk, dv, dgb = _delta_bwd(q, k, v, gb, states, tms, do, heads)
    dc, dba, g_alog, g_dt = _dn_pre_bwd1(qkvz, ba, wconv, alog, dtb, dq, dk, dv, dgb, heads)
    dqkvz, g_wconv = _dn_pre_bwd2(dc, qkvz, dz, wconv)
    g_qkvz = _matmul(n0, dqkvz, "tn", [BF16], name="dn_in_proj_dw", tk=2048)
    g_ba = _matmul(n0, dba, "tn", [BF16], name="dn_in_proj_ba_dw", tk=2048)
    g["dn_w_in"] = [jnp.concatenate([g_qkvz, g_ba[:, :2 * heads]], axis=1)]
    g["dn_w_conv"] = g_wconv[None]
    tok = emit(0, g)
    dn0a = _matmul(dba, w_ba, "nt", [F32], name="dn_in_proj_ba_dx", after=tok)
    grad_x, g_dnn = _matmul(dqkvz, w_qkvz, "nt", [F32], name="dn_in_proj_dx",
                            epi=lambda acc, part, hx, dres, gain: rms_bwd_epi(acc + part, hx, dres, gain),
                            mn_extras=[dn0a, x, dh], row_extras=[dn_norm], n_rowsum=1, slab=EPI_SLAB)
    g.update(dn_norm=g_dnn, dn_out_norm=g_on,
             dn_a_log=g_alog[:, heads:2 * heads], dn_dt_bias=g_dt[:, heads:2 * heads])
    return loss, grad_x, g


def _round_up(n, m):
    return (n + m - 1) // m * m


def _pack_rows(parts, cols, row_mult):
    lead = parts[0].shape[:-1]
    flat, offs, off = [], [], 0
    for p in parts:
        n = _round_up(p.shape[-1], cols)
        flat.append(jnp.pad(p, [(0, 0)] * len(lead) + [(0, n - p.shape[-1])]))
        offs.append(off)
        off += n
    total = _round_up(off, cols * row_mult)
    if total > off:
        flat.append(jnp.zeros(lead + (total - off,), parts[0].dtype))
    return jnp.concatenate(flat, axis=-1).reshape(lead + (total // cols, cols)), offs


def _unpack(packed, offs, shapes):
    lead = packed.shape[:-2]
    flat = packed.reshape(lead + (-1,))
    out = []
    for off, shp in zip(offs, shapes):
        n = 1
        for v in shp:
            n *= v
        out.append(flat[..., off:off + n].reshape(lead + tuple(shp)))
    return out


def kernel(x, mem, dn_norm, dn_w_in, dn_w_conv, dn_a_log, dn_dt_bias, dn_out_norm, dn_w_out, cv_norm, cv_w_pw1, cv_b_pw1, cv_w_dw, cv_b_dw, cv_ln_g, cv_ln_b, cv_w_pw2, cv_b_pw2, xa_norm, xa_mem_norm, xa_w_q, xa_w_kv, xa_w_o, mlp_norm, mlp_w_up, mlp_w_down, final_norm, loss_target, m_dn_norm, m_dn_w_in, m_dn_w_conv, m_dn_a_log, m_dn_dt_bias, m_dn_out_norm, m_dn_w_out, m_cv_norm, m_cv_w_pw1, m_cv_b_pw1, m_cv_w_dw, m_cv_b_dw, m_cv_ln_g, m_cv_ln_b, m_cv_w_pw2, m_cv_b_pw2, m_xa_norm, m_xa_mem_norm, m_xa_w_q, m_xa_w_kv, m_xa_w_o, m_mlp_norm, m_mlp_w_up, m_mlp_w_down, m_final_norm, v_dn_norm, v_dn_w_in, v_dn_w_conv, v_dn_a_log, v_dn_dt_bias, v_dn_out_norm, v_dn_w_out, v_cv_norm, v_cv_w_pw1, v_cv_b_pw1, v_cv_w_dw, v_cv_b_dw, v_cv_ln_g, v_cv_ln_b, v_cv_w_pw2, v_cv_b_pw2, v_xa_norm, v_xa_mem_norm, v_xa_w_q, v_xa_w_kv, v_xa_w_o, v_mlp_norm, v_mlp_w_up, v_mlp_w_down, v_final_norm):
    wsh = dict(dn_norm=dn_norm, dn_w_in=dn_w_in, dn_w_conv=dn_w_conv, dn_a_log=dn_a_log, dn_dt_bias=dn_dt_bias, dn_out_norm=dn_out_norm, dn_w_out=dn_w_out, cv_norm=cv_norm, cv_w_pw1=cv_w_pw1, cv_b_pw1=cv_b_pw1, cv_w_dw=cv_w_dw, cv_b_dw=cv_b_dw, cv_ln_g=cv_ln_g, cv_ln_b=cv_ln_b, cv_w_pw2=cv_w_pw2, cv_b_pw2=cv_b_pw2, xa_norm=xa_norm, xa_mem_norm=xa_mem_norm, xa_w_q=xa_w_q, xa_w_kv=xa_w_kv, xa_w_o=xa_w_o, mlp_norm=mlp_norm, mlp_w_up=mlp_w_up, mlp_w_down=mlp_w_down, final_norm=final_norm)
    msh = dict(dn_norm=m_dn_norm, dn_w_in=m_dn_w_in, dn_w_conv=m_dn_w_conv, dn_a_log=m_dn_a_log, dn_dt_bias=m_dn_dt_bias, dn_out_norm=m_dn_out_norm, dn_w_out=m_dn_w_out, cv_norm=m_cv_norm, cv_w_pw1=m_cv_w_pw1, cv_b_pw1=m_cv_b_pw1, cv_w_dw=m_cv_w_dw, cv_b_dw=m_cv_b_dw, cv_ln_g=m_cv_ln_g, cv_ln_b=m_cv_ln_b, cv_w_pw2=m_cv_w_pw2, cv_b_pw2=m_cv_b_pw2, xa_norm=m_xa_norm, xa_mem_norm=m_xa_mem_norm, xa_w_q=m_xa_w_q, xa_w_kv=m_xa_w_kv, xa_w_o=m_xa_w_o, mlp_norm=m_mlp_norm, mlp_w_up=m_mlp_w_up, mlp_w_down=m_mlp_w_down, final_norm=m_final_norm)
    vsh = dict(dn_norm=v_dn_norm, dn_w_in=v_dn_w_in, dn_w_conv=v_dn_w_conv, dn_a_log=v_dn_a_log, dn_dt_bias=v_dn_dt_bias, dn_out_norm=v_dn_out_norm, dn_w_out=v_dn_w_out, cv_norm=v_cv_norm, cv_w_pw1=v_cv_w_pw1, cv_b_pw1=v_cv_b_pw1, cv_w_dw=v_cv_w_dw, cv_b_dw=v_cv_b_dw, cv_ln_g=v_cv_ln_g, cv_ln_b=v_cv_ln_b, cv_w_pw2=v_cv_w_pw2, cv_b_pw2=v_cv_b_pw2, xa_norm=v_xa_norm, xa_mem_norm=v_xa_mem_norm, xa_w_q=v_xa_w_q, xa_w_kv=v_xa_w_kv, xa_w_o=v_xa_w_o, mlp_norm=v_mlp_norm, mlp_w_up=v_mlp_w_up, mlp_w_down=v_mlp_w_down, final_norm=v_final_norm)

    big_axis = dict(BIG)

    small_pack, small_offs = _pack_rows([wsh[nm].reshape(-1) for nm in SMALL_SH], LANES, 8)
    w = {nm: [None] * wsh[nm].shape[0] for nm in big_axis}

    def put_weights(group, gathered):
        for (nm, layer), gth in zip(group, gathered):
            if nm == "dn_w_in":
                w[nm][layer] = gth.transpose(1, 0, 2).reshape(gth.shape[1], N_DEV * gth.shape[2])
            elif big_axis[nm] == 1:
                w[nm][layer] = gth.reshape(N_DEV * gth.shape[1], gth.shape[2])
            else:
                w[nm][layer] = gth

    first = _all_gather([wsh[nm][layer].astype(BF16) for nm, layer in GATHER_GROUPS[0]] + [small_pack],
                        "weights_all_gather_0")
    put_weights(GATHER_GROUPS[0], first)
    me = _dev_index(*_mesh_me())
    gather_handles, tokens = {}, []
    for gi in range(1, len(GATHER_GROUPS)):
        shards = [wsh[nm][layer].astype(BF16) for nm, layer in GATHER_GROUPS[gi]]
        lands = [lax.dynamic_update_slice(lax.empty((N_DEV,) + s.shape, s.dtype), s[None], (me, 0, 0))
                 for s in shards]
        gather_handles[gi], tok = _exchange_start(shards, lands, False, f"weights_gather_{gi}",
                                                  after=[first[-1]] + tokens)
        tokens.append(tok)
    for nm, gth in zip(SMALL_SH, _unpack(first[-1], small_offs, [wsh[nm].shape for nm in SMALL_SH])):
        w[nm] = jnp.moveaxis(gth, 0, -2).reshape(gth.shape[1:-1] + (N_DEV * gth.shape[-1],))
    for nm in REPL:
        w[nm] = wsh[nm]

    def fetch(gi, after):
        put_weights(GATHER_GROUPS[gi], _exchange_wait(gather_handles[gi], after)[1])

    scatter_handles = {}

    def emit(gi, g):
        blocks = []
        for nm, layer in SCATTER_GROUPS[gi]:
            gw = g[nm][layer]
            if nm == "dn_w_in":
                gw = gw.reshape(gw.shape[0], N_DEV, -1).transpose(1, 0, 2)
            elif big_axis[nm] == 1:
                gw = gw.reshape(N_DEV, gw.shape[0] // N_DEV, gw.shape[1])
            blocks.append(gw)
        if gi == 0:
            gsmall_pack, _ = _pack_rows(
                [jnp.moveaxis(g[nm].reshape(g[nm].shape[:-1] + (N_DEV, -1)), -2, 0).reshape(N_DEV, -1)
                 for nm in SMALL_SH], LANES, 8)
            blocks.append(gsmall_pack)
        lands = [lax.empty(b.shape, b.dtype) for b in blocks]
        scatter_handles[gi], tok = _exchange_start(blocks, lands, True, f"grads_scatter_{gi}")
        return [tok]

    loss_part, grad_x, g = _local_step(x[0], mem[0], loss_target[0], w, fetch, emit, tokens)

    recv = {nm: [None] * wsh[nm].shape[0] for nm in big_axis}
    sent = {nm: [None] * wsh[nm].shape[0] for nm in big_axis}
    gsh, delta, new_m, new_v = {}, {}, {}, {}
    after = [grad_x]
    done = set()
    me_arr = me.astype(jnp.int32).reshape(1)
    for gi in reversed(range(len(SCATTER_GROUPS))):
        sources, landed = _exchange_wait(scatter_handles[gi], after)
        for (nm, layer), src, r in zip(SCATTER_GROUPS[gi], sources, landed):
            sent[nm][layer], recv[nm][layer] = src, r
        if gi == 0:
            slot = lax.broadcasted_iota(jnp.int32, landed[-1].shape, 0)
            rsmall = jnp.where(slot == me, sources[-1], landed[-1])
        for nm in big_axis:
            if nm not in done and all(r is not None for r in recv[nm]):
                gsh[nm], delta[nm], new_m[nm], new_v[nm] = _adamw_reduce(
                    me_arr, recv[nm], sent[nm], wsh[nm], msh[nm], vsh[nm], f"adamw_{nm}")
                done.add(nm)
                after = [delta[nm]]
    gsmall_red = _slot_sum(rsmall, "grads_small_sum", 512)
    repl_pack, repl_offs = _pack_rows([g[nm].reshape(-1) for nm in REPL], LANES, 8)
    (repl_all,) = _all_gather([repl_pack], "repl_grads_all_gather")
    repl_red = _slot_sum(repl_all, "repl_grads_sum", 512)
    for nm, val in zip(SMALL_SH, _unpack(gsmall_red, small_offs, [wsh[nm].shape for nm in SMALL_SH])):
        gsh[nm] = val
    for nm, val in zip(REPL, _unpack(repl_red, repl_offs, [wsh[nm].shape for nm in REPL])):
        gsh[nm] = val
    small_names = list(SMALL_SH) + list(REPL)
    packs = []
    for src in (wsh, gsh, msh, vsh):
        pk, sm_offs = _pack_rows([src[nm].reshape(-1) for nm in small_names], LANES, 8)
        packs.append(pk)
    outs = _adamw(*packs, "adamw_small")
    for dst, pk in zip((delta, new_m, new_v), outs):
        for nm, val in zip(small_names, _unpack(pk, sm_offs, [wsh[nm].shape for nm in small_names])):
            dst[nm] = val

    loss = lax.psum(loss_part[0, 0], ("x", "y", "c"))
    return (loss, grad_x[None], *[gsh[nm] for nm in WEIGHTS], *[delta[nm] for nm in WEIGHTS],
            *[new_m[nm] for nm in WEIGHTS], *[new_v[nm] for nm in WEIGHTS])
```

```python
import functools

import jax
import jax.numpy as jnp
from jax import lax
from jax.experimental import pallas as pl
from jax.experimental.pallas import tpu as pltpu

F32 = jnp.float32
BF16 = jnp.bfloat16
MESH_IDS = pl.DeviceIdType.MESH

N_DEV = 8
LANES = 128
RMS_EPS = 1e-6
LN_EPS = 1e-5
DN_HEAD_DIM = 128
DN_CONV = 4
DN_CHUNK = 64
CV_WIDTH = 31
XA_HEADS = 4
EPI_SLAB = 256
DN_HALO = 8
CV_HALO = 32

ADAM_LR = 0.001
ADAM_B1 = 0.9
ADAM_B2 = 0.999
ADAM_EPS = 1e-08
ADAM_WD = 0.01
ADAM_STEP = 10

BIG = (("dn_w_in", 2), ("dn_w_out", 1), ("cv_w_pw1", 2), ("cv_w_pw2", 1), ("xa_w_q", 1), ("xa_w_kv", 2),
       ("xa_w_o", 1), ("mlp_w_up", 2), ("mlp_w_down", 1))
_LAYER_GROUP = ("xa_w_q", "xa_w_o", "mlp_w_down", "xa_w_kv", "mlp_w_up")
GATHER_GROUPS = (
    (("dn_w_in", 0),),
    (("dn_w_out", 0),) + tuple((nm, 0) for nm in _LAYER_GROUP),
    (("cv_w_pw2", 0), ("cv_w_pw1", 0)),
    tuple((nm, 1) for nm in _LAYER_GROUP),
)
SCATTER_GROUPS = (
    (("dn_w_out", 0), ("dn_w_in", 0)),
    tuple((nm, 0) for nm in _LAYER_GROUP),
    (("cv_w_pw2", 0), ("cv_w_pw1", 0)),
    tuple((nm, 1) for nm in _LAYER_GROUP),
)
SMALL_SH = ("cv_norm", "cv_b_pw1", "cv_b_dw", "cv_ln_g", "cv_ln_b", "cv_b_pw2", "cv_w_dw", "dn_w_conv")
REPL = ("dn_norm", "dn_a_log", "dn_dt_bias", "dn_out_norm", "xa_norm", "xa_mem_norm", "mlp_norm", "final_norm")
WEIGHTS = ("dn_norm", "dn_w_in", "dn_w_conv", "dn_a_log", "dn_dt_bias", "dn_out_norm", "dn_w_out", "cv_norm",
           "cv_w_pw1", "cv_b_pw1", "cv_w_dw", "cv_b_dw", "cv_ln_g", "cv_ln_b", "cv_w_pw2", "cv_b_pw2", "xa_norm",
           "xa_mem_norm", "xa_w_q", "xa_w_kv", "xa_w_o", "mlp_norm", "mlp_w_up", "mlp_w_down", "final_norm")


def _dot_dims(mode, batched):
    o = 1 if batched else 0
    contract = {"nn": ((1 + o,), (o,)), "nt": ((1 + o,), (1 + o,)), "tn": ((o,), (o,))}[mode]
    return (contract, (((0,), (0,)) if batched else ((), ())))


def _bdot(a, b, mode):
    return lax.dot_general(a.astype(BF16), b.astype(BF16), _dot_dims(mode, a.ndim == 3),
                           preferred_element_type=F32)


@functools.partial(jax.custom_vjp, nondiff_argnums=(2,))
def _mm(a, b, mode):
    return _bdot(a, b, mode)


def _mm_fwd(a, b, mode):
    return _bdot(a, b, mode), (a, b)


def _mm_bwd(mode, res, ct):
    a, b = res
    if mode == "nn":
        da, db = _bdot(ct, b, "nt"), _bdot(a, ct, "tn")
    elif mode == "nt":
        da, db = _bdot(ct, b, "nn"), _bdot(ct, a, "tn")
    else:
        da, db = _bdot(b, ct, "nt"), _bdot(a, ct, "nn")
    return da.astype(a.dtype), db.astype(b.dtype)


_mm.defvjp(_mm_fwd, _mm_bwd)


def _sigmoid(x):
    return 0.5 * (jnp.tanh(0.5 * x) + 1.0)


def _silu(x):
    return x * _sigmoid(x)


def _softplus(x):
    return jnp.maximum(x, 0.0) + jnp.log(1.0 + jnp.exp(-jnp.abs(x)))


def _rms(x, g):
    r = lax.rsqrt(jnp.mean(x * x, axis=-1, keepdims=True) + RMS_EPS)
    return x * r * g


def _shift_rows(x, off):
    if off == 0:
        return x
    return pltpu.roll(x, x.shape[0] - off, 0)


def _series_dot(a, b, mode):
    return _bdot(a, b, mode)


def _chunk_masks(c):
    ii = lax.broadcasted_iota(jnp.int32, (c, c), 0)
    jj = lax.broadcasted_iota(jnp.int32, (c, c), 1)
    return (ii == jj).astype(F32), ii >= jj, ii > jj


def _neumann_inverse(lm):
    n = lm.shape[-1]
    t = -lm
    p = lm
    size = 2
    while size < n:
        size *= 2
        p = _series_dot(p, p, "nn")
        t = t + p + _series_dot(t, p, "nn")
    return t


def _apply_inverse(tm, rhs, mode):
    return rhs + _series_dot(tm, rhs, mode)


@jax.custom_vjp
def _unit_lower_solve(lm, rhs, tm):
    return _apply_inverse(tm, rhs, "nn")


def _uls_fwd(lm, rhs, tm):
    sol = _apply_inverse(tm, rhs, "nn")
    return sol, (tm, sol)


def _uls_bwd(res, ct):
    tm, sol = res
    d_rhs = _apply_inverse(tm, ct, "tn")
    return -_bdot(d_rhs, sol, "nt"), d_rhs, jnp.zeros_like(tm)


_unit_lower_solve.defvjp(_uls_fwd, _uls_bwd)


def _delta_chunk(q, k, v, gcol, bcol, s0, tm=None):
    c = q.shape[1]
    eye, causal, strict = _chunk_masks(c)
    grow = jnp.sum(eye * gcol, axis=1, keepdims=True)
    gc = jnp.sum(jnp.where(causal, grow, 0.0), axis=2, keepdims=True)
    gc_row = jnp.sum(eye * gc, axis=1, keepdims=True)
    decay = jnp.exp(jnp.where(causal, gc - gc_row, -jnp.inf))
    kb = k * bcol
    lm = jnp.where(strict, _mm(kb, k, "nt") * decay, 0.0)
    if tm is None:
        tm = _neumann_inverse(lax.stop_gradient(lm))
    egc = jnp.exp(gc)
    rhs = jnp.concatenate([v * bcol, kb * egc], axis=-1)
    sol = _unit_lower_solve(lm, rhs, tm)
    dv_ = v.shape[-1]
    u, w = sol[..., :dv_], sol[..., dv_:]
    attn = _mm(q, k, "nt") * decay
    qd = q * egc
    gl = jnp.sum(grow, axis=2, keepdims=True)
    kd = k * jnp.exp(gl - gc)
    v_new = u - _mm(w, s0, "nn")
    o = _mm(qd, s0, "nn") + _mm(attn, v_new, "nn")
    s1 = s0 * jnp.exp(gl) + _mm(kd, v_new, "tn")
    return o, s1, tm


def _dn_point(cv, ba, alog, dt, heads):
    a = _silu(cv)
    d = cv.shape[1] // 3
    qs, ks = [], []
    for h in range(heads):
        qh = a[:, h * DN_HEAD_DIM:(h + 1) * DN_HEAD_DIM]
        qs.append(qh * lax.rsqrt(jnp.sum(qh * qh, axis=-1, keepdims=True) + 1e-6) * (DN_HEAD_DIM ** -0.5))
        kh = a[:, d + h * DN_HEAD_DIM:d + (h + 1) * DN_HEAD_DIM]
        ks.append(kh * lax.rsqrt(jnp.sum(kh * kh, axis=-1, keepdims=True) + 1e-6))
    q = jnp.concatenate(qs, axis=-1)
    k = jnp.concatenate(ks, axis=-1)
    v = a[:, 2 * d:]
    lane = lax.broadcasted_iota(jnp.int32, ba.shape, 1)
    beta = _sigmoid(ba)
    g = -jnp.exp(alog) * _softplus(ba + dt)
    gb = jnp.where(lane < heads, beta, jnp.where(lane < 2 * heads, g, 0.0))
    return q, k, v, gb


def _attn_tile(q, k, v):
    hd = q.shape[1] // XA_HEADS
    outs = []
    for h in range(XA_HEADS):
        sl = slice(h * hd, (h + 1) * hd)
        s = _mm(q[:, sl], k[:, sl], "nt") * (hd ** -0.5)
        m = lax.stop_gradient(jnp.max(s, axis=-1, keepdims=True))
        e = jnp.exp(s - m)
        p = e / jnp.sum(e, axis=-1, keepdims=True)
        outs.append(_mm(p, v[:, sl], "nn"))
    return jnp.concatenate(outs, axis=-1)


def _ln_silu(c, g, b):
    mu = jnp.mean(c, axis=-1, keepdims=True)
    xc = c - mu
    y = xc * lax.rsqrt(jnp.mean(xc * xc, axis=-1, keepdims=True) + LN_EPS)
    return _silu(y * g + b)


def _causal_conv(xext, w, width, lead, ts):
    acc = None
    for j in range(width):
        term = _shift_rows(xext, lead + j)[:ts] * w[j:j + 1, :]
        acc = term if acc is None else acc + term
    return acc


def _colsum(x):
    return jnp.sum(x, axis=0, keepdims=True)


def _stack_rows(rows, n_rows):
    c = rows[0].shape[1]
    ridx = lax.broadcasted_iota(jnp.int32, (n_rows, c), 0)
    out = jnp.zeros((n_rows, c), F32)
    for j, r in enumerate(rows):
        out = out + jnp.where(ridx == j, r, 0.0)
    return out


def _matmul(a, b, mode, out_dtypes, *, name, epi=None, mn_extras=(), row_extras=(), out_dm=False, after=(),
            n_rowsum=0, slab=0, tm=1024, tn=1024, tk=1024):
    b_dm = b.ndim == 3
    b_shape = (b.shape[1], N_DEV * b.shape[2]) if b_dm else b.shape
    if mode == "nn":
        (m, k), (k2, n) = a.shape, b_shape
    elif mode == "nt":
        (m, k), (n, k2) = a.shape, b_shape
    else:
        (k, m), (k2, n) = a.shape, b_shape
    assert k == k2, (a.shape, b.shape, mode)
    tm, tn, tk = min(tm, m), min(tn, n), min(tk, k)
    cb, nb = 0, 1
    if b_dm:
        assert mode in ("nn", "nt")
        cb = b.shape[2]
        nb = max(1, (tn if mode == "nn" else tk) // cb)
        if mode == "nn":
            tn = nb * cb
        else:
            tk = nb * cb
    co, no = 0, 1
    if out_dm:
        co = n // N_DEV
        no = max(1, tn // co)
        tn = no * co
    assert m % tm == 0 and n % tn == 0 and k % tk == 0, (m, n, k, tm, tn, tk)
    nk = k // tk
    if mode == "tn":
        a_spec = pl.BlockSpec((tk, tm), lambda j, i, kk: (kk, i))
    else:
        a_spec = pl.BlockSpec((tm, tk), lambda j, i, kk: (i, kk))
    if b_dm:
        b_spec = (pl.BlockSpec((nb, tn, cb), lambda j, i, kk: (kk, j, 0)) if mode == "nt"
                  else pl.BlockSpec((nb, tk, cb), lambda j, i, kk: (j, kk, 0)))
    else:
        b_spec = (pl.BlockSpec((tn, tk), lambda j, i, kk: (j, kk)) if mode == "nt"
                  else pl.BlockSpec((tk, tn), lambda j, i, kk: (kk, j)))
    mn_spec = pl.BlockSpec((tm, tn), lambda j, i, kk: (i, j))
    row_spec = pl.BlockSpec((1, tn), lambda j, i, kk: (0, j))
    n_extra = len(mn_extras) + len(row_extras)
    n_out = len(out_dtypes)
    in_specs = ([a_spec, b_spec] + [mn_spec] * len(mn_extras) + [row_spec] * len(row_extras)
                + [_ANY_SPEC] * len(after))
    args = [a, b, *mn_extras, *row_extras, *after]
    if out_dm:
        out_specs = [pl.BlockSpec((no, tm, co), lambda j, i, kk: (j, i, 0))] * n_out
        out_shape = [jax.ShapeDtypeStruct((N_DEV, m, co), dt) for dt in out_dtypes]
    else:
        out_specs = [mn_spec] * n_out
        out_shape = [jax.ShapeDtypeStruct((m, n), dt) for dt in out_dtypes]
    out_specs = out_specs + [row_spec] * n_rowsum
    out_shape = out_shape + [jax.ShapeDtypeStruct((1, n), F32)] * n_rowsum
    n_in = len(args)
    n_mn = len(mn_extras)
    step = min(slab, tm) if slab else tm
    assert tm % step == 0

    def dot(a_ref, b_ref):
        if not b_dm:
            return _bdot(a_ref[...], b_ref[...], mode)
        if mode == "nn":
            parts = [_bdot(a_ref[...], b_ref[dd], "nn") for dd in range(nb)]
            return parts[0] if nb == 1 else jnp.concatenate(parts, axis=1)
        out = None
        for dd in range(nb):
            part = _bdot(a_ref[:, dd * cb:(dd + 1) * cb], b_ref[dd], "nt")
            out = part if out is None else out + part
        return out

    def finish(acc_src, extras, outs):
        sums = [None] * n_rowsum
        for r0 in range(0, tm, step):
            rs = slice(r0, r0 + step)
            acc_val = acc_src[rs, :]
            if epi is None:
                vals = (acc_val,)
            else:
                vals = epi(acc_val, *[e[rs, :] for e in extras[:n_mn]], *[e[...] for e in extras[n_mn:]])
            for o_ref, val in zip(outs[:n_out], vals[:n_out]):
                if out_dm:
                    for dd in range(no):
                        o_ref[dd, rs, :] = val[:, dd * co:(dd + 1) * co].astype(o_ref.dtype)
                else:
                    o_ref[rs, :] = val.astype(o_ref.dtype)
            for q in range(n_rowsum):
                sums[q] = vals[n_out + q] if sums[q] is None else sums[q] + vals[n_out + q]
        for q in range(n_rowsum):
            s_ref = outs[n_out + q]

            @pl.when(pl.program_id(1) == 0)
            def _():
                s_ref[...] = sums[q]

            @pl.when(pl.program_id(1) > 0)
            def _():
                s_ref[...] += sums[q]

    def body_one_step(*refs):
        finish(dot(refs[0], refs[1]), refs[2:2 + n_extra], refs[n_in:])

    def body(*refs):
        a_ref, b_ref = refs[0], refs[1]
        acc = refs[-1]
        kk = pl.program_id(2)

        @pl.when(kk == 0)
        def _():
            acc[...] = jnp.zeros_like(acc)

        acc[...] += dot(a_ref, b_ref)

        @pl.when(kk == nk - 1)
        def _():
            finish(acc, refs[2:2 + n_extra], refs[n_in:-1])

    res = pl.pallas_call(
        body_one_step if nk == 1 else body, name=name,
        grid=(n // tn, m // tm, nk),
        in_specs=in_specs, out_specs=out_specs, out_shape=out_shape,
        scratch_shapes=[] if nk == 1 else [pltpu.VMEM((tm, tn), F32)],
        compiler_params=pltpu.CompilerParams(
            dimension_semantics=("parallel", "arbitrary" if n_rowsum else "parallel", "arbitrary")),
    )(*args)
    return res[0] if n_out + n_rowsum == 1 else res


def _rowwise(fn, *, n_rows, ts, name, rows=(), prevs=(), nexts=(), vecs=(), row_outs=(), acc_outs=(), after=()):
    ts = min(ts, n_rows)
    assert n_rows % ts == 0
    nblk = n_rows // ts
    in_specs, args = [], []
    for arr, cb, w in rows:
        in_specs.append(pl.BlockSpec((ts, w), functools.partial(lambda i, cb: (i, cb), cb=cb)))
        args.append(arr)
    for arr, cb, w, halo in prevs:
        per = ts // halo
        in_specs.append(pl.BlockSpec(
            (halo, w), functools.partial(lambda i, cb, per: (jnp.maximum(i * per - 1, 0), cb), cb=cb, per=per)))
        args.append(arr)
    for arr, cb, w, halo in nexts:
        per = ts // halo
        last_blk = n_rows // halo - 1
        in_specs.append(pl.BlockSpec(
            (halo, w), functools.partial(lambda i, cb, per, lb: (jnp.minimum((i + 1) * per, lb), cb),
                                         cb=cb, per=per, lb=last_blk)))
        args.append(arr)
    for arr in vecs:
        in_specs.append(pl.BlockSpec(arr.shape, functools.partial(lambda i, nd: (0,) * nd, nd=arr.ndim)))
        args.append(arr)
    out_specs, out_shape = [], []
    for w, dt in row_outs:
        out_specs.append(pl.BlockSpec((ts, w), lambda i: (i, 0)))
        out_shape.append(jax.ShapeDtypeStruct((n_rows, w), dt))
    for shp in acc_outs:
        out_specs.append(pl.BlockSpec(shp, functools.partial(lambda i, nd: (0,) * nd, nd=len(shp))))
        out_shape.append(jax.ShapeDtypeStruct(shp, F32))
    n_used = len(args)
    n_tiles = n_used - len(vecs)
    in_specs += [_ANY_SPEC] * len(after)
    args += list(after)
    n_in, n_ro, n_acc = len(args), len(row_outs), len(acc_outs)

    def body(*refs):
        ins, ro, ac = refs[:n_used], refs[n_in:n_in + n_ro], refs[n_in + n_ro:]
        i = pl.program_id(0)
        rvals, avals = fn(i == 0, i == nblk - 1, *[r[...] for r in ins[:n_tiles]], *ins[n_tiles:])
        for r, val in zip(ro, rvals):
            r[...] = val.astype(r.dtype)
        if n_acc:
            @pl.when(i == 0)
            def _():
                for r in ac:
                    r[...] = jnp.zeros_like(r)

            for r, val in zip(ac, avals):
                r[...] += val

    res = pl.pallas_call(
        body, name=name, grid=(nblk,), in_specs=in_specs, out_specs=out_specs, out_shape=out_shape,
        compiler_params=pltpu.CompilerParams(dimension_semantics=("arbitrary",)),
    )(*args)
    return res


def _gated_out(o, z, onorm):
    return o * lax.rsqrt(jnp.mean(o * o, axis=-1, keepdims=True) + RMS_EPS) * onorm * _silu(z)


def _head_blocks(ref, heads, col0=0):
    return jnp.stack([ref[:, col0 + h * DN_HEAD_DIM:col0 + (h + 1) * DN_HEAD_DIM] for h in range(heads)])


def _split_heads(q_ref, k_ref, v_ref, gbv, heads):
    gcol = jnp.stack([gbv[:, heads + h:heads + h + 1] for h in range(heads)])
    bcol = jnp.stack([gbv[:, h:h + 1] for h in range(heads)])
    return _head_blocks(q_ref, heads), _head_blocks(k_ref, heads), _head_blocks(v_ref, heads), gcol, bcol


def _delta_fwd(q, k, v, gb, qkvz, onorm, heads):
    s, hd = q.shape
    n = s // DN_CHUNK
    blk = pl.BlockSpec((DN_CHUNK, hd), lambda c: (c, 0))
    gspec = pl.BlockSpec((DN_CHUNK, LANES), lambda c: (c, 0))

    def body(q_ref, k_ref, v_ref, gb_ref, z_ref, on_ref, og_ref, st_ref, tm_ref, state):
        @pl.when(pl.program_id(0) == 0)
        def _():
            state[...] = jnp.zeros_like(state)

        s0 = state[...]
        st_ref[0] = s0
        o, s1, tm = _delta_chunk(*_split_heads(q_ref, k_ref, v_ref, gb_ref[...], heads), s0)
        og = _gated_out(o, _head_blocks(z_ref, heads), on_ref[...])
        for h in range(heads):
            og_ref[:, h * DN_HEAD_DIM:(h + 1) * DN_HEAD_DIM] = og[h].astype(og_ref.dtype)
        state[...] = s1
        tm_ref[0] = tm

    return pl.pallas_call(
        body, name="dn_delta_fwd", grid=(n,),
        in_specs=[blk, blk, blk, gspec, pl.BlockSpec((DN_CHUNK, hd), lambda c: (c, 3)),
                  pl.BlockSpec(onorm.shape, lambda c: (0, 0))],
        out_specs=[blk, pl.BlockSpec((1, heads, DN_HEAD_DIM, DN_HEAD_DIM), lambda c: (c, 0, 0, 0)),
                   pl.BlockSpec((1, heads, DN_CHUNK, DN_CHUNK), lambda c: (c, 0, 0, 0))],
        out_shape=[jax.ShapeDtypeStruct((s, hd), BF16),
                   jax.ShapeDtypeStruct((n, heads, DN_HEAD_DIM, DN_HEAD_DIM), F32),
                   jax.ShapeDtypeStruct((n, heads, DN_CHUNK, DN_CHUNK), F32)],
        scratch_shapes=[pltpu.VMEM((heads, DN_HEAD_DIM, DN_HEAD_DIM), F32)],
        compiler_params=pltpu.CompilerParams(dimension_semantics=("arbitrary",)),
    )(q, k, v, gb, qkvz, onorm)


def _delta_bwd(q, k, v, gb, qkvz, onorm, states, tms, dog, heads):
    s, hd = q.shape
    n = s // DN_CHUNK
    blk = pl.BlockSpec((DN_CHUNK, hd), lambda c: (n - 1 - c, 0))
    gspec = pl.BlockSpec((DN_CHUNK, LANES), lambda c: (n - 1 - c, 0))
    sspec = pl.BlockSpec((1, heads, DN_HEAD_DIM, DN_HEAD_DIM), lambda c: (n - 1 - c, 0, 0, 0))
    tspec = pl.BlockSpec((1, heads, DN_CHUNK, DN_CHUNK), lambda c: (n - 1 - c, 0, 0, 0))
    nspec = pl.BlockSpec(onorm.shape, lambda c: (0, 0))

    def body(q_ref, k_ref, v_ref, gb_ref, z_ref, on_ref, st_ref, tm_ref, dog_ref,
             dq_ref, dk_ref, dv_ref, dgb_ref, dz_ref, don_ref, dstate):
        @pl.when(pl.program_id(0) == 0)
        def _():
            dstate[...] = jnp.zeros_like(dstate)
            don_ref[...] = jnp.zeros_like(don_ref)

        gbv = gb_ref[...]
        tm = tm_ref[0]

        def chunk(qh, kh, vh, gcol, bcol, s0, zh, on):
            o, s1, _ = _delta_chunk(qh, kh, vh, gcol, bcol, s0, tm)
            return _gated_out(o, zh, on), s1

        _, vjp = jax.vjp(chunk, *_split_heads(q_ref, k_ref, v_ref, gbv, heads), st_ref[0],
                         _head_blocks(z_ref, heads), on_ref[...])
        dq, dk, dv, dg, db, ds0, dz, don = vjp((_head_blocks(dog_ref, heads).astype(F32), dstate[...]))
        dstate[...] = ds0
        don_ref[...] += don
        lane = lax.broadcasted_iota(jnp.int32, gbv.shape, 1)
        dgb = jnp.zeros(gbv.shape, F32)
        for h in range(heads):
            sl = slice(h * DN_HEAD_DIM, (h + 1) * DN_HEAD_DIM)
            dq_ref[:, sl] = dq[h]
            dk_ref[:, sl] = dk[h]
            dv_ref[:, sl] = dv[h]
            dz_ref[:, sl] = dz[h]
            dgb = dgb + jnp.where(lane == h, db[h], 0.0) + jnp.where(lane == heads + h, dg[h], 0.0)
        dgb_ref[...] = dgb

    return pl.pallas_call(
        body, name="dn_delta_bwd", grid=(n,),
        in_specs=[blk, blk, blk, gspec, pl.BlockSpec((DN_CHUNK, hd), lambda c: (n - 1 - c, 3)), nspec,
                  sspec, tspec, blk],
        out_specs=[blk, blk, blk, gspec, blk, nspec],
        out_shape=[jax.ShapeDtypeStruct((s, hd), F32)] * 3 + [jax.ShapeDtypeStruct((s, LANES), F32),
                                                              jax.ShapeDtypeStruct((s, hd), F32),
                                                              jax.ShapeDtypeStruct(onorm.shape, F32)],
        scratch_shapes=[pltpu.VMEM((heads, DN_HEAD_DIM, DN_HEAD_DIM), F32)],
        compiler_params=pltpu.CompilerParams(dimension_semantics=("arbitrary",)),
    )(q, k, v, gb, qkvz, onorm, states, tms, dog)


def _dev_index(px, py, pc):
    return 4 * px + 2 * py + pc


def _all_gather(arrs, name):
    n = len(arrs)

    def body(*refs):
        xs, outs = refs[:n], refs[n:2 * n]
        send_sems, recv_sems, local_sems = refs[2 * n:]
        x, y, c = lax.axis_index("x"), lax.axis_index("y"), lax.axis_index("c")
        me, sibling = (x, y, c), (x, y, 1 - c)
        chips = [(1 - x, y), (x, 1 - y), (1 - x, 1 - y)]

        def copy(a, kk, block, to, src=None):
            dst = outs[a].at[_dev_index(*block)]
            return pltpu.make_async_remote_copy(
                src_ref=dst if src is None else src, dst_ref=dst,
                send_sem=send_sems.at[a * 7 + kk], recv_sem=recv_sems.at[a * 7 + kk],
                device_id=to, device_id_type=MESH_IDS)

        mine = [pltpu.make_async_copy(xs[a], outs[a].at[_dev_index(*me)], local_sems.at[a]) for a in range(n)]
        for cp in mine:
            cp.start()
        first = []
        for a in range(n):
            first.append(copy(a, 0, me, sibling, src=xs[a]))
            first += [copy(a, 1 + j, me, (*chip, c), src=xs[a]) for j, chip in enumerate(chips)]
        for cp in first:
            cp.start()
        passed = []
        for j, chip in enumerate(chips):
            for a in range(n):
                copy(a, 1 + j, (*chip, c), me).wait_recv()
                fwd = copy(a, 4 + j, (*chip, c), sibling)
                fwd.start()
                passed.append(fwd)
        for a in range(n):
            copy(a, 0, sibling, me).wait_recv()
        for j, chip in enumerate(chips):
            for a in range(n):
                copy(a, 4 + j, (*chip, 1 - c), me).wait_recv()
        for cp in first + passed:
            cp.wait_send()
        for cp in mine:
            cp.wait()

    hbm = pl.BlockSpec(memory_space=pltpu.HBM)
    res = pl.pallas_call(
        body, name=name,
        in_specs=[hbm] * n, out_specs=[hbm] * n,
        out_shape=[jax.ShapeDtypeStruct((N_DEV,) + a.shape, a.dtype) for a in arrs],
        scratch_shapes=[pltpu.SemaphoreType.DMA((7 * n,)), pltpu.SemaphoreType.DMA((7 * n,)),
                        pltpu.SemaphoreType.DMA((n,))],
    )(*arrs)
    return list(res)


_FLIPS = ((0, 0, 1), (1, 0, 0), (0, 1, 0), (1, 1, 0), (1, 0, 1), (0, 1, 1), (1, 1, 1))
_HBM_SPEC = pl.BlockSpec(memory_space=pltpu.HBM)
_SEM_SPEC = pl.BlockSpec(memory_space=pltpu.SEMAPHORE)
_ANY_SPEC = pl.BlockSpec(memory_space=pl.ANY)
_DATAFLOW = pltpu.SideEffectType.DATAFLOW_SIDE_EFFECTING
TOKEN_SHAPE = (8, LANES)


def _mesh_me():
    return lax.axis_index("x"), lax.axis_index("y"), lax.axis_index("c")


def _flipped(me, f):
    return tuple(1 - v if fl else v for v, fl in zip(me, f))


def _exchange_copies(xs, lands, send_sems, recv_sems, scatter, landed):
    me = _mesh_me()
    cps = []
    for kk, f in enumerate(_FLIPS):
        p = _flipped(me, f)
        for a in range(len(xs)):
            cps.append(pltpu.make_async_remote_copy(
                src_ref=xs[a].at[_dev_index(*p)] if scatter else xs[a],
                dst_ref=lands[a].at[_dev_index(*(p if landed else me))],
                send_sem=send_sems.at[a * 7 + kk], recv_sem=recv_sems.at[a * 7 + kk],
                device_id=p, device_id_type=MESH_IDS))
    return cps


def _exchange_start(srcs, lands, scatter, name, after=()):
    n = len(srcs)

    n_after = len(after)

    def body(*refs):
        xs, ls = refs[:n], refs[n:2 * n]
        send_sems, recv_sems = refs[2 * n + n_after], refs[2 * n + n_after + 1]
        token = refs[-1]
        for cp in _exchange_copies(xs, ls, send_sems, recv_sems, scatter, landed=False):
            cp.start()
        token[...] = jnp.zeros_like(token)

    operands = [pltpu.with_memory_space_constraint(a, pltpu.HBM) for a in list(srcs) + list(lands)]
    res = pl.pallas_call(
        body, name=name,
        in_specs=[_HBM_SPEC] * (2 * n) + [_ANY_SPEC] * len(after),
        out_specs=[_SEM_SPEC, _SEM_SPEC] + [_HBM_SPEC] * (2 * n) + [pl.BlockSpec(memory_space=pltpu.VMEM)],
        out_shape=[pltpu.SemaphoreType.DMA((7 * n,)), pltpu.SemaphoreType.DMA((7 * n,))]
        + [pltpu.HBM(a.shape, a.dtype) for a in operands] + [jax.ShapeDtypeStruct(TOKEN_SHAPE, F32)],
        input_output_aliases={i: 2 + i for i in range(2 * n)},
        compiler_params=pltpu.CompilerParams(has_side_effects=_DATAFLOW),
    )(*operands, *after)
    return (res[0], res[1], list(res[2:2 + n]), list(res[2 + n:2 + 2 * n]), scatter, name), res[-1]


def _exchange_wait(handle, after):
    send_sems, recv_sems, srcs, lands, scatter, name = handle
    n = len(srcs)
    n_after = len(after)

    def body(*refs):
        xs, ls = refs[:n], refs[n:2 * n]
        send_sems_ref, recv_sems_ref = refs[2 * n], refs[2 * n + 1]
        for cp in _exchange_copies(xs, ls, send_sems_ref, recv_sems_ref, scatter, landed=True):
            cp.wait_send()
            cp.wait_recv()

    res = pl.pallas_call(
        body, name=name + "_wait",
        in_specs=[_HBM_SPEC] * (2 * n) + [_SEM_SPEC, _SEM_SPEC] + [_ANY_SPEC] * n_after,
        out_specs=[_HBM_SPEC] * (2 * n),
        out_shape=[pltpu.HBM(a.shape, a.dtype) for a in srcs + lands],
        input_output_aliases={i: i for i in range(2 * n)},
        compiler_params=pltpu.CompilerParams(has_side_effects=_DATAFLOW),
    )(*srcs, *lands, send_sems, recv_sems, *after)
    return list(res[:n]), list(res[n:])


def _slot_sum(g, name, tr):
    _, r, c = g.shape
    tr = min(tr, r)
    assert r % tr == 0

    def body(g_ref, o_ref):
        acc = g_ref[0].astype(F32)
        for s in range(1, N_DEV):
            acc = acc + g_ref[s].astype(F32)
        o_ref[...] = acc

    return pl.pallas_call(
        body, name=name, grid=(r // tr,),
        in_specs=[pl.BlockSpec((N_DEV, tr, c), lambda i: (0, i, 0))],
        out_specs=pl.BlockSpec((tr, c), lambda i: (i, 0)),
        out_shape=jax.ShapeDtypeStruct((r, c), F32),
        compiler_params=pltpu.CompilerParams(dimension_semantics=("parallel",)),
    )(g)


def _adam_update(w, gg, m, v):
    c1 = 1.0 / (1.0 - ADAM_B1 ** ADAM_STEP)
    c2 = 1.0 / (1.0 - ADAM_B2 ** ADAM_STEP)
    nm = ADAM_B1 * m + (1.0 - ADAM_B1) * gg
    nv = ADAM_B2 * v + (1.0 - ADAM_B2) * (gg * gg)
    return -ADAM_LR * ((nm * c1) / (jnp.sqrt(nv * c2) + ADAM_EPS) + ADAM_WD * w), nm, nv


def _adamw_reduce(me, recvs, owns, w, m, v, name, tr=128):
    nl, r, c = w.shape
    assert len(recvs) == nl and len(owns) == nl
    tr = min(tr, r)
    assert r % tr == 0
    nblk = r // tr

    def parked(li, l, i):
        return jnp.where(l < li, 0, jnp.where(l > li, nblk - 1, i))

    def recv_spec(li):
        return pl.BlockSpec((N_DEV, tr, c), lambda l, i, me_ref: (0, parked(li, l, i), 0))

    def own_spec(li):
        return pl.BlockSpec((None, tr, c), lambda l, i, me_ref: (me_ref[0], parked(li, l, i), 0))

    def body(me_ref, *refs):
        rrefs, orefs = refs[:nl], refs[nl:2 * nl]
        w_ref, m_ref, v_ref, g_ref, d_ref, nm_ref, nv_ref = refs[2 * nl:]
        l = pl.program_id(0)

        def of_layer(vals):
            out = vals[0]
            for li in range(1, nl):
                out = jnp.where(l == li, vals[li], out)
            return out

        own = of_layer([o[...].astype(F32) for o in orefs])
        gg = None
        for s in range(N_DEV):
            slot = jnp.where(me_ref[0] == s, own, of_layer([rr[s].astype(F32) for rr in rrefs]))
            gg = slot if gg is None else gg + slot
        g_ref[...] = gg
        d_ref[...], nm_ref[...], nv_ref[...] = _adam_update(w_ref[...], gg, m_ref[...], v_ref[...])

    spec = pl.BlockSpec((None, tr, c), lambda l, i, me_ref: (l, i, 0))
    return pl.pallas_call(
        body, name=name,
        grid_spec=pltpu.PrefetchScalarGridSpec(
            num_scalar_prefetch=1, grid=(nl, nblk),
            in_specs=[recv_spec(li) for li in range(nl)] + [own_spec(li) for li in range(nl)] + [spec] * 3,
            out_specs=[spec] * 4),
        out_shape=[jax.ShapeDtypeStruct((nl, r, c), F32)] * 4,
        compiler_params=pltpu.CompilerParams(dimension_semantics=("arbitrary", "arbitrary")),
    )(me, *recvs, *owns, w, m, v)


def _adamw(w, g, m, v, name, tr=256):
    r, c = w.shape
    tr = min(tr, r)
    assert r % tr == 0

    def body(w_ref, g_ref, m_ref, v_ref, d_ref, nm_ref, nv_ref):
        d_ref[...], nm_ref[...], nv_ref[...] = _adam_update(w_ref[...], g_ref[...], m_ref[...], v_ref[...])

    spec = pl.BlockSpec((tr, c), lambda i: (i, 0))
    return pl.pallas_call(
        body, name=name, grid=(r // tr,), in_specs=[spec] * 4, out_specs=[spec] * 3,
        out_shape=[jax.ShapeDtypeStruct((r, c), F32)] * 3,
        compiler_params=pltpu.CompilerParams(dimension_semantics=("parallel",)),
    )(w, g, m, v)


def _rms_fwd(x, g, name, ts=512, after=()):
    s, d = x.shape

    def fn(first, last, xv, gv):
        return [_rms(xv, gv[...])], []

    return _rowwise(fn, n_rows=s, ts=ts, name=name, rows=[(x, 0, d)], vecs=[g], row_outs=[(d, BF16)],
                    after=after)[0]


def _rms_bwd(x, dn, dres, g, name, ts=256):
    s, d = x.shape

    def fn(first, last, xv, dnv, drv, gv):
        _, vjp = jax.vjp(_rms, xv, gv[...])
        dx, dg = vjp(dnv.astype(F32))
        return [drv + dx], [dg]

    return _rowwise(fn, n_rows=s, ts=ts, name=name, rows=[(x, 0, d), (dn, 0, d), (dres, 0, d)], vecs=[g],
                    row_outs=[(d, F32)], acc_outs=[(1, d)])


def _loss_head(h, tgt, g, ts=256):
    s, d = h.shape

    def fn(first, last, hv, tv, gv):
        def f(hh, gg):
            e = _rms(hh, gg) - tv
            per_row = jnp.mean(e * e, axis=-1, keepdims=True)
            return 0.5 * jnp.sum(per_row, axis=0, keepdims=True)

        l, vjp = jax.vjp(f, hv, gv[...])
        dh, dg = vjp(jnp.ones((1, 1), F32))
        return [dh], [dg, jnp.zeros((1, LANES), F32) + l]

    return _rowwise(fn, n_rows=s, ts=ts, name="loss_head", rows=[(h, 0, d), (tgt, 0, d)], vecs=[g],
                    row_outs=[(d, F32)], acc_outs=[(1, d), (1, LANES)])


def _dn_pre_fwd(qkvz, ba, wconv, alog, dt, heads, ts=128):
    s = qkvz.shape[0]
    d3 = wconv.shape[1]
    d = d3 // 3

    def fn(first, last, xc, bav, xp, wv, av, dv):
        xext = jnp.concatenate([jnp.where(first, 0.0, xp), xc], axis=0)
        cv = _causal_conv(xext, wv, DN_CONV, DN_HALO - (DN_CONV - 1), xc.shape[0])
        return list(_dn_point(cv, bav, av[...], dv[...], heads)), []

    return _rowwise(fn, n_rows=s, ts=ts, name="dn_pre_fwd", rows=[(qkvz, 0, d3), (ba, 0, LANES)],
                    prevs=[(qkvz, 0, d3, DN_HALO)], vecs=[wconv, alog, dt],
                    row_outs=[(d, F32), (d, F32), (d, F32), (LANES, F32)])


def _dn_pre_bwd(qkvz, ba, wconv, alog, dt, dq, dk, dv, dgb, dz, heads, ts=128):
    s = qkvz.shape[0]
    d3 = wconv.shape[1]
    d = d3 // 3
    lead = DN_HALO - (DN_CONV - 1)

    def fn(first, last, xc, bac, dqc, dkc, dvc, dgbc, dzc, xp, xn, ban, dqn, dkn, dvn, dgbn, wv, av, dtv):
        n = xc.shape[0]
        ext = lambda cur, nxt: jnp.concatenate([cur, nxt], axis=0)
        live = lambda nxt: jnp.where(last, 0.0, nxt)
        xall = jnp.concatenate([jnp.where(first, 0.0, xp), xc, live(xn)], axis=0)
        cv = _causal_conv(xall, wv, DN_CONV, lead, n + DN_HALO)
        (_, _, _, gbv), vjp = jax.vjp(lambda c, b: _dn_point(c, b, av[...], dtv[...], heads), cv, ext(bac, ban))
        dc, dba = vjp((ext(dqc, live(dqn)), ext(dkc, live(dkn)), ext(dvc, live(dvn)), ext(dgbc, live(dgbn))))
        dx = None
        dw = []
        for j in range(DN_CONV):
            term = _shift_rows(dc, DN_CONV - 1 - j)[:n] * wv[j:j + 1, :]
            dx = term if dx is None else dx + term
            dw.append(_colsum(dc[:n] * _shift_rows(xall, lead + j)[:n]))
        dba = dba[:n]
        return ([jnp.concatenate([dx, dzc], axis=-1), dba],
                [_stack_rows(dw, DN_CONV), _colsum(dgbc * gbv[:n]), _colsum(dba)])

    return _rowwise(fn, n_rows=s, ts=ts, name="dn_pre_bwd",
                    rows=[(qkvz, 0, d3), (ba, 0, LANES), (dq, 0, d), (dk, 0, d), (dv, 0, d), (dgb, 0, LANES),
                          (dz, 0, d)],
                    prevs=[(qkvz, 0, d3, DN_HALO)],
                    nexts=[(qkvz, 0, d3, DN_HALO), (ba, 0, LANES, DN_HALO), (dq, 0, d, DN_HALO),
                           (dk, 0, d, DN_HALO), (dv, 0, d, DN_HALO), (dgb, 0, LANES, DN_HALO)],
                    vecs=[wconv, alog, dt],
                    row_outs=[(4 * d, BF16), (LANES, BF16)], acc_outs=[(DN_CONV, d3), (1, LANES), (1, LANES)])


def _cv_mid_fwd(u, wdw, bdw, lng, lnb, ts=256):
    s = u.shape[0]
    d = u.shape[1] // 2

    def fn(first, last, uc, up, wv, bv, gv, lbv):
        uext = jnp.concatenate([jnp.where(first, 0.0, up), uc], axis=0)
        glu = uext[:, :d] * _sigmoid(uext[:, d:])
        c = _causal_conv(glu, wv, CV_WIDTH, CV_HALO - (CV_WIDTH - 1), uc.shape[0]) + bv[...]
        return [c, _ln_silu(c, gv[...], lbv[...])], []

    return _rowwise(fn, n_rows=s, ts=ts, name="cv_mid_fwd", rows=[(u, 0, 2 * d)], prevs=[(u, 0, 2 * d, CV_HALO)],
                    vecs=[wdw, bdw, lng, lnb], row_outs=[(d, F32), (d, BF16)])


def _cv_mid_bwd1(c, ds, dhp, lng, lnb, ts=256):
    s, d = c.shape

    def fn(first, last, cv, dsv, dhv, gv, bv):
        _, vjp = jax.vjp(_ln_silu, cv, gv[...], bv[...])
        dc, dg, db = vjp(dsv.astype(F32))
        return [dc], [dg, db, _colsum(dc), _colsum(dhv)]

    return _rowwise(fn, n_rows=s, ts=ts, name="cv_mid_bwd1", rows=[(c, 0, d), (ds, 0, d), (dhp, 0, d)],
                    vecs=[lng, lnb], row_outs=[(d, F32)], acc_outs=[(1, d)] * 4)


def _cv_mid_bwd2(dc, u, wdw, ts=256):
    s, d = dc.shape

    def fn(first, last, dcc, uc, up, dcn, wv):
        n = dcc.shape[0]
        dcext = jnp.concatenate([dcc, jnp.where(last, 0.0, dcn)], axis=0)
        uext = jnp.concatenate([jnp.where(first, 0.0, up), uc], axis=0)
        glu = uext[:, :d] * _sigmoid(uext[:, d:])
        dglu = None
        dw = []
        for j in range(CV_WIDTH):
            term = _shift_rows(dcext, CV_WIDTH - 1 - j)[:n] * wv[j:j + 1, :]
            dglu = term if dglu is None else dglu + term
            dw.append(_colsum(dcc * _shift_rows(glu, CV_HALO - (CV_WIDTH - 1) + j)[:n]))
        u1, sg = uc[:, :d], _sigmoid(uc[:, d:])
        du = jnp.concatenate([dglu * sg, dglu * u1 * sg * (1.0 - sg)], axis=-1)
        return [du], [_stack_rows(dw, CV_HALO), _colsum(du)]

    return _rowwise(fn, n_rows=s, ts=ts, name="cv_mid_bwd2", rows=[(dc, 0, d), (u, 0, 2 * d)],
                    prevs=[(u, 0, 2 * d, CV_HALO)], nexts=[(dc, 0, d, CV_HALO)], vecs=[wdw],
                    row_outs=[(2 * d, BF16)], acc_outs=[(CV_HALO, d), (1, 2 * d)])


def _attn_fwd(q, k, v, name, ts=512):
    s, d = q.shape

    def fn(first, last, qv, kv, vv):
        return [_attn_tile(qv.astype(F32), kv[...].astype(F32), vv[...].astype(F32))], []

    return _rowwise(fn, n_rows=s, ts=ts, name=name, rows=[(q, 0, d)], vecs=[k, v], row_outs=[(d, BF16)])[0]


def _attn_bwd(q, k, v, do, name, ts=512):
    s, d = q.shape
    m = k.shape[0]

    def fn(first, last, qv, dov, kv, vv):
        _, vjp = jax.vjp(_attn_tile, qv.astype(F32), kv[...].astype(F32), vv[...].astype(F32))
        dq, dk, dv = vjp(dov.astype(F32))
        return [dq], [dk, dv]

    return _rowwise(fn, n_rows=s, ts=ts, name=name, rows=[(q, 0, d), (do, 0, d)], vecs=[k, v],
                    row_outs=[(d, BF16)], acc_outs=[(m, d), (m, d)])


def _pad_lanes(a, off=0):
    r, n = a.shape
    return jnp.pad(a, ((0, 0), (off, LANES - off - n)))


def _local_step(x, mem, tgt, w, fetch=None, emit=None, first_after=()):
    s, d = x.shape
    heads = d // DN_HEAD_DIM
    g = {}
    if fetch is None:
        fetch = lambda group, after: None
    if emit is None:
        emit = lambda group, grads: ()

    def add_res(acc, res):
        return (res + acc,)

    def add_res_rms(acc, res, gain):
        h = res + acc
        return h, _rms(h, gain)

    def rms_bwd_epi(acc, hx, dres, gain):
        _, vjp = jax.vjp(_rms, hx, gain)
        dx, dg = vjp(acc)
        return dres + dx, dg

    w_in = w["dn_w_in"][0]
    assert w_in.shape[1] == 4 * d + 2 * heads
    w_qkvz = w_in[:, :4 * d]
    w_ba = _pad_lanes(w_in[:, 4 * d:])
    dn_norm = w["dn_norm"]
    alog = _pad_lanes(w["dn_a_log"], heads)
    dtb = _pad_lanes(w["dn_dt_bias"], heads)
    wconv = w["dn_w_conv"][0]
    n0 = _rms_fwd(x, dn_norm, "dn_rms", after=first_after)
    qkvz = _matmul(n0, w_qkvz, "nn", [F32], name="dn_in_proj")
    ba = _matmul(n0, w_ba, "nn", [F32], name="dn_in_proj_ba")
    q, k, v, gb = _dn_pre_fwd(qkvz, ba, wconv, alog, dtb, heads)
    og, states, tms = _delta_fwd(q, k, v, gb, qkvz, w["dn_out_norm"], heads)
    fetch(1, [og])
    h1, nq0 = _matmul(og, w["dn_w_out"][0], "nn", [F32, BF16], name="dn_out_proj", epi=add_res_rms,
                      mn_extras=[x], row_extras=[w["xa_norm"][0:1]], slab=EPI_SLAB)

    def xattn_fwd(h, nq, layer, next_gain):
        qx = _matmul(nq, w["xa_w_q"][layer], "nn", [BF16], name=f"xa{layer}_q")
        mn = _rms_fwd(mem, w["xa_mem_norm"][layer:layer + 1], f"xa{layer}_mem_rms")
        kv = _matmul(mn, w["xa_w_kv"][layer], "nn", [BF16], name=f"xa{layer}_kv")
        kx, vx = kv[:, :d], kv[:, d:]
        ox = _attn_fwd(qx, kx, vx, f"xa{layer}_attn")
        hn, nn = _matmul(ox, w["xa_w_o"][layer], "nn", [F32, BF16], name=f"xa{layer}_o", epi=add_res_rms,
                         mn_extras=[h], row_extras=[next_gain], slab=EPI_SLAB)
        return hn, nn, (h, nq, qx, mn, kx, vx, ox)

    def mlp_fwd(h, nm, layer, next_gain):
        def epi(acc):
            r = jnp.maximum(acc, 0.0)
            return acc, r * r

        u, a = _matmul(nm, w["mlp_w_up"][layer], "nn", [BF16, BF16], name=f"mlp{layer}_up", epi=epi)
        if next_gain is None:
            hn, nn = _matmul(a, w["mlp_w_down"][layer], "nn", [F32], name=f"mlp{layer}_down", epi=add_res,
                             mn_extras=[h], tk=2048), None
        else:
            hn, nn = _matmul(a, w["mlp_w_down"][layer], "nn", [F32, BF16], name=f"mlp{layer}_down",
                             epi=add_res_rms, mn_extras=[h], row_extras=[next_gain], tk=2048, slab=EPI_SLAB)
        return hn, nn, (h, nm, u, a)

    h2, nm0, xa0 = xattn_fwd(h1, nq0, 0, w["mlp_norm"][0:1])
    fetch(2, [h2])
    h3, n1, mlp0 = mlp_fwd(h2, nm0, 0, w["cv_norm"])

    u_cv = _matmul(n1, w["cv_w_pw1"][0], "nn", [F32], name="cv_pw1", epi=lambda acc, b: (acc + b,),
                   row_extras=[w["cv_b_pw1"]])
    wdw = jnp.pad(w["cv_w_dw"][0], ((0, CV_HALO - CV_WIDTH), (0, 0)))
    c_cv, s_cv = _cv_mid_fwd(u_cv, wdw, w["cv_b_dw"], w["cv_ln_g"], w["cv_ln_b"])
    fetch(3, [s_cv])
    h4, nq1 = _matmul(s_cv, w["cv_w_pw2"][0], "nn", [F32, BF16], name="cv_pw2",
                      epi=lambda acc, res, b, gain: add_res_rms(acc + b, res, gain), mn_extras=[h3],
                      row_extras=[w["cv_b_pw2"], w["xa_norm"][1:2]], slab=EPI_SLAB)
    h5, nm1, xa1 = xattn_fwd(h4, nq1, 1, w["mlp_norm"][1:2])
    h6, _, mlp1 = mlp_fwd(h5, nm1, 1, None)

    fnorm = w["final_norm"].reshape(1, d)
    dh, g_fn, loss = _loss_head(h6, tgt, fnorm)
    g["final_norm"] = g_fn.reshape(d)

    def mlp_bwd(dh, layer, saved, after=()):
        h, nm, u, a = saved
        du = _matmul(dh, w["mlp_w_down"][layer], "nt", [BF16], name=f"mlp{layer}_down_dx", after=after,
                     epi=lambda acc, uu: (acc * 2.0 * jnp.maximum(uu.astype(F32), 0.0),), mn_extras=[u])
        gdown = _matmul(a, dh, "tn", [BF16], name=f"mlp{layer}_down_dw", tm=2048)
        dhn, gn = _matmul(du, w["mlp_w_up"][layer], "nt", [F32], name=f"mlp{layer}_up_dx", epi=rms_bwd_epi,
                          mn_extras=[h, dh], row_extras=[w["mlp_norm"][layer:layer + 1]], n_rowsum=1,
                          slab=EPI_SLAB)
        gup = _matmul(nm, du, "tn", [BF16], name=f"mlp{layer}_up_dw", out_dm=True, tk=2048)
        return dhn, gup, gdown, gn

    def xattn_bwd(dh, layer, saved):
        h, nq, qx, mn, kx, vx, ox = saved
        dox = _matmul(dh, w["xa_w_o"][layer], "nt", [BF16], name=f"xa{layer}_o_dx")
        go = _matmul(ox, dh, "tn", [BF16], name=f"xa{layer}_o_dw", tk=2048)
        dqx, dkx, dvx = _attn_bwd(qx, kx, vx, dox, f"xa{layer}_attn_bwd")
        dhn, gn = _matmul(dqx, w["xa_w_q"][layer], "nt", [F32], name=f"xa{layer}_q_dx", epi=rms_bwd_epi,
                          mn_extras=[h, dh], row_extras=[w["xa_norm"][layer:layer + 1]], n_rowsum=1,
                          slab=EPI_SLAB)
        gq = _matmul(nq, dqx, "tn", [BF16], name=f"xa{layer}_q_dw", tk=2048)
        dkv = jnp.concatenate([dkx, dvx], axis=-1)
        gkv = _matmul(mn, dkv, "tn", [BF16], name=f"xa{layer}_kv_dw", out_dm=True)
        dmn = _matmul(dkv, w["xa_w_kv"][layer], "nt", [F32], name=f"xa{layer}_kv_dx", tk=2048)
        _, gmem = _rms_bwd(mem, dmn, dmn, w["xa_mem_norm"][layer:layer + 1], f"xa{layer}_mem_rms_bwd")
        return dhn, gq, gkv, go, gn, gmem

    dh, gup1, gdown1, gmn1 = mlp_bwd(dh, 1, mlp1)
    dh, gq1, gkv1, go1, gxn1, gmem1 = xattn_bwd(dh, 1, xa1)
    g.update(mlp_w_up=[None, gup1], mlp_w_down=[None, gdown1], xa_w_q=[None, gq1], xa_w_kv=[None, gkv1],
             xa_w_o=[None, go1])
    tok = emit(3, g)

    ds_cv = _matmul(dh, w["cv_w_pw2"][0], "nt", [BF16], name="cv_pw2_dx", after=tok)
    g["cv_w_pw2"] = [_matmul(s_cv, dh, "tn", [BF16], name="cv_pw2_dw", tk=2048)]
    dc_cv, g_lng, g_lnb, g_bdw, g_b2 = _cv_mid_bwd1(c_cv, ds_cv, dh, w["cv_ln_g"], w["cv_ln_b"])
    du_cv, g_wdw, g_b1 = _cv_mid_bwd2(dc_cv, u_cv, wdw)
    g["cv_w_pw1"] = [_matmul(n1, du_cv, "tn", [BF16], name="cv_pw1_dw", out_dm=True, tk=2048)]
    dh, g_cvn = _matmul(du_cv, w["cv_w_pw1"][0], "nt", [F32], name="cv_pw1_dx", epi=rms_bwd_epi,
                        mn_extras=[h3, dh], row_extras=[w["cv_norm"]], n_rowsum=1, slab=EPI_SLAB)
    g.update(cv_ln_g=g_lng, cv_ln_b=g_lnb, cv_b_dw=g_bdw, cv_b_pw2=g_b2, cv_b_pw1=g_b1, cv_norm=g_cvn,
             cv_w_dw=g_wdw[:CV_WIDTH][None])

    tok = emit(2, g)
    dh, gup0, gdown0, gmn0 = mlp_bwd(dh, 0, mlp0, after=tok)
    dh, gq0, gkv0, go0, gxn0, gmem0 = xattn_bwd(dh, 0, xa0)
    g["mlp_w_up"][0] = gup0
    g["mlp_w_down"][0] = gdown0
    g["mlp_norm"] = jnp.concatenate([gmn0, gmn1], axis=0)
    g["xa_w_q"][0] = gq0
    g["xa_w_kv"][0] = gkv0
    g["xa_w_o"][0] = go0
    g["xa_norm"] = jnp.concatenate([gxn0, gxn1], axis=0)
    g["xa_mem_norm"] = jnp.concatenate([gmem0, gmem1], axis=0)
    tok = emit(1, g)

    dog = _matmul(dh, w["dn_w_out"][0], "nt", [BF16], name="dn_out_proj_dx", after=tok)
    g["dn_w_out"] = [_matmul(og, dh, "tn", [BF16], name="dn_out_proj_dw", tk=2048)]
    dq, dk, dv, dgb, dz, g_on = _delta_bwd(q, k, v, gb, qkvz, w["dn_out_norm"], states, tms, dog, heads)
    dqkvz, dba, g_wconv, g_alog, g_dt = _dn_pre_bwd(qkvz, ba, wconv, alog, dtb, dq, dk, dv, dgb, dz, heads)
    g_qkvz = _matmul(n0, dqkvz, "tn", [BF16], name="dn_in_proj_dw", tk=2048)
    g_ba = _matmul(n0, dba, "tn", [BF16], name="dn_in_proj_ba_dw", tk=2048)
    g["dn_w_in"] = [jnp.concatenate([g_qkvz, g_ba[:, :2 * heads]], axis=1)]
    g["dn_w_conv"] = g_wconv[None]
    tok = emit(0, g)
    dn0a = _matmul(dba, w_ba, "nt", [F32], name="dn_in_proj_ba_dx", after=tok)
    grad_x, g_dnn = _matmul(dqkvz, w_qkvz, "nt", [F32], name="dn_in_proj_dx",
                            epi=lambda acc, part, hx, dres, gain: rms_bwd_epi(acc + part, hx, dres, gain),
                            mn_extras=[dn0a, x, dh], row_extras=[dn_norm], n_rowsum=1, slab=EPI_SLAB)
    g.update(dn_norm=g_dnn, dn_out_norm=g_on,
             dn_a_log=g_alog[:, heads:2 * heads], dn_dt_bias=g_dt[:, heads:2 * heads])
    return loss, grad_x, g


def _round_up(n, m):
    return (n + m - 1) // m * m


def _pack_rows(parts, cols, row_mult):
    lead = parts[0].shape[:-1]
    flat, offs, off = [], [], 0
    for p in parts:
        n = _round_up(p.shape[-1], cols)
        flat.append(jnp.pad(p, [(0, 0)] * len(lead) + [(0, n - p.shape[-1])]))
        offs.append(off)
        off += n
    total = _round_up(off, cols * row_mult)
    if total > off:
        flat.append(jnp.zeros(lead + (total - off,), parts[0].dtype))
    return jnp.concatenate(flat, axis=-1).reshape(lead + (total // cols, cols)), offs


def _unpack(packed, offs, shapes):
    lead = packed.shape[:-2]
    flat = packed.reshape(lead + (-1,))
    out = []
    for off, shp in zip(offs, shapes):
        n = 1
        for v in shp:
            n *= v
        out.append(flat[..., off:off + n].reshape(lead + tuple(shp)))
    return out


def kernel(x, mem, dn_norm, dn_w_in, dn_w_conv, dn_a_log, dn_dt_bias, dn_out_norm, dn_w_out, cv_norm, cv_w_pw1, cv_b_pw1, cv_w_dw, cv_b_dw, cv_ln_g, cv_ln_b, cv_w_pw2, cv_b_pw2, xa_norm, xa_mem_norm, xa_w_q, xa_w_kv, xa_w_o, mlp_norm, mlp_w_up, mlp_w_down, final_norm, loss_target, m_dn_norm, m_dn_w_in, m_dn_w_conv, m_dn_a_log, m_dn_dt_bias, m_dn_out_norm, m_dn_w_out, m_cv_norm, m_cv_w_pw1, m_cv_b_pw1, m_cv_w_dw, m_cv_b_dw, m_cv_ln_g, m_cv_ln_b, m_cv_w_pw2, m_cv_b_pw2, m_xa_norm, m_xa_mem_norm, m_xa_w_q, m_xa_w_kv, m_xa_w_o, m_mlp_norm, m_mlp_w_up, m_mlp_w_down, m_final_norm, v_dn_norm, v_dn_w_in, v_dn_w_conv, v_dn_a_log, v_dn_dt_bias, v_dn_out_norm, v_dn_w_out, v_cv_norm, v_cv_w_pw1, v_cv_b_pw1, v_cv_w_dw, v_cv_b_dw, v_cv_ln_g, v_cv_ln_b, v_cv_w_pw2, v_cv_b_pw2, v_xa_norm, v_xa_mem_norm, v_xa_w_q, v_xa_w_kv, v_xa_w_o, v_mlp_norm, v_mlp_w_up, v_mlp_w_down, v_final_norm):
    wsh = dict(dn_norm=dn_norm, dn_w_in=dn_w_in, dn_w_conv=dn_w_conv, dn_a_log=dn_a_log, dn_dt_bias=dn_dt_bias, dn_out_norm=dn_out_norm, dn_w_out=dn_w_out, cv_norm=cv_norm, cv_w_pw1=cv_w_pw1, cv_b_pw1=cv_b_pw1, cv_w_dw=cv_w_dw, cv_b_dw=cv_b_dw, cv_ln_g=cv_ln_g, cv_ln_b=cv_ln_b, cv_w_pw2=cv_w_pw2, cv_b_pw2=cv_b_pw2, xa_norm=xa_norm, xa_mem_norm=xa_mem_norm, xa_w_q=xa_w_q, xa_w_kv=xa_w_kv, xa_w_o=xa_w_o, mlp_norm=mlp_norm, mlp_w_up=mlp_w_up, mlp_w_down=mlp_w_down, final_norm=final_norm)
    msh = dict(dn_norm=m_dn_norm, dn_w_in=m_dn_w_in, dn_w_conv=m_dn_w_conv, dn_a_log=m_dn_a_log, dn_dt_bias=m_dn_dt_bias, dn_out_norm=m_dn_out_norm, dn_w_out=m_dn_w_out, cv_norm=m_cv_norm, cv_w_pw1=m_cv_w_pw1, cv_b_pw1=m_cv_b_pw1, cv_w_dw=m_cv_w_dw, cv_b_dw=m_cv_b_dw, cv_ln_g=m_cv_ln_g, cv_ln_b=m_cv_ln_b, cv_w_pw2=m_cv_w_pw2, cv_b_pw2=m_cv_b_pw2, xa_norm=m_xa_norm, xa_mem_norm=m_xa_mem_norm, xa_w_q=m_xa_w_q, xa_w_kv=m_xa_w_kv, xa_w_o=m_xa_w_o, mlp_norm=m_mlp_norm, mlp_w_up=m_mlp_w_up, mlp_w_down=m_mlp_w_down, final_norm=m_final_norm)
    vsh = dict(dn_norm=v_dn_norm, dn_w_in=v_dn_w_in, dn_w_conv=v_dn_w_conv, dn_a_log=v_dn_a_log, dn_dt_bias=v_dn_dt_bias, dn_out_norm=v_dn_out_norm, dn_w_out=v_dn_w_out, cv_norm=v_cv_norm, cv_w_pw1=v_cv_w_pw1, cv_b_pw1=v_cv_b_pw1, cv_w_dw=v_cv_w_dw, cv_b_dw=v_cv_b_dw, cv_ln_g=v_cv_ln_g, cv_ln_b=v_cv_ln_b, cv_w_pw2=v_cv_w_pw2, cv_b_pw2=v_cv_b_pw2, xa_norm=v_xa_norm, xa_mem_norm=v_xa_mem_norm, xa_w_q=v_xa_w_q, xa_w_kv=v_xa_w_kv, xa_w_o=v_xa_w_o, mlp_norm=v_mlp_norm, mlp_w_up=v_mlp_w_up, mlp_w_down=v_mlp_w_down, final_norm=v_final_norm)

    big_axis = dict(BIG)

    small_pack, small_offs = _pack_rows([wsh[nm].reshape(-1) for nm in SMALL_SH], LANES, 8)
    w = {nm: [None] * wsh[nm].shape[0] for nm in big_axis}

    def put_weights(group, gathered):
        for (nm, layer), gth in zip(group, gathered):
            if nm == "dn_w_in":
                w[nm][layer] = gth.transpose(1, 0, 2).reshape(gth.shape[1], N_DEV * gth.shape[2])
            elif big_axis[nm] == 1:
                w[nm][layer] = gth.reshape(N_DEV * gth.shape[1], gth.shape[2])
            else:
                w[nm][layer] = gth

    first = _all_gather([wsh[nm][layer].astype(BF16) for nm, layer in GATHER_GROUPS[0]] + [small_pack],
                        "weights_all_gather_0")
    put_weights(GATHER_GROUPS[0], first)
    me = _dev_index(*_mesh_me())
    gather_handles, tokens = {}, []
    for gi in range(1, len(GATHER_GROUPS)):
        shards = [wsh[nm][layer].astype(BF16) for nm, layer in GATHER_GROUPS[gi]]
        lands = [lax.dynamic_update_slice(lax.empty((N_DEV,) + s.shape, s.dtype), s[None], (me, 0, 0))
                 for s in shards]
        gather_handles[gi], tok = _exchange_start(shards, lands, False, f"weights_gather_{gi}",
                                                  after=[first[-1]] + tokens)
        tokens.append(tok)
    for nm, gth in zip(SMALL_SH, _unpack(first[-1], small_offs, [wsh[nm].shape for nm in SMALL_SH])):
        w[nm] = jnp.moveaxis(gth, 0, -2).reshape(gth.shape[1:-1] + (N_DEV * gth.shape[-1],))
    for nm in REPL:
        w[nm] = wsh[nm]

    def fetch(gi, after):
        put_weights(GATHER_GROUPS[gi], _exchange_wait(gather_handles[gi], after)[1])

    scatter_handles = {}

    def emit(gi, g):
        blocks = []
        for nm, layer in SCATTER_GROUPS[gi]:
            gw = g[nm][layer]
            if nm == "dn_w_in":
                gw = gw.reshape(gw.shape[0], N_DEV, -1).transpose(1, 0, 2)
            elif big_axis[nm] == 1:
                gw = gw.reshape(N_DEV, gw.shape[0] // N_DEV, gw.shape[1])
            blocks.append(gw)
        if gi == 0:
            gsmall_pack, _ = _pack_rows(
                [jnp.moveaxis(g[nm].reshape(g[nm].shape[:-1] + (N_DEV, -1)), -2, 0).reshape(N_DEV, -1)
                 for nm in SMALL_SH], LANES, 8)
            blocks.append(gsmall_pack)
        lands = [lax.empty(b.shape, b.dtype) for b in blocks]
        scatter_handles[gi], tok = _exchange_start(blocks, lands, True, f"grads_scatter_{gi}")
        return [tok]

    loss_part, grad_x, g = _local_step(x[0], mem[0], loss_target[0], w, fetch, emit, tokens)

    recv = {nm: [None] * wsh[nm].shape[0] for nm in big_axis}
    sent = {nm: [None] * wsh[nm].shape[0] for nm in big_axis}
    gsh, delta, new_m, new_v = {}, {}, {}, {}
    after = [grad_x]
    done = set()
    me_arr = me.astype(jnp.int32).reshape(1)
    for gi in reversed(range(len(SCATTER_GROUPS))):
        sources, landed = _exchange_wait(scatter_handles[gi], after)
        for (nm, layer), src, r in zip(SCATTER_GROUPS[gi], sources, landed):
            sent[nm][layer], recv[nm][layer] = src, r
        if gi == 0:
            slot = lax.broadcasted_iota(jnp.int32, landed[-1].shape, 0)
            rsmall = jnp.where(slot == me, sources[-1], landed[-1])
        for nm in big_axis:
            if nm not in done and all(r is not None for r in recv[nm]):
                gsh[nm], delta[nm], new_m[nm], new_v[nm] = _adamw_reduce(
                    me_arr, recv[nm], sent[nm], wsh[nm], msh[nm], vsh[nm], f"adamw_{nm}")
                done.add(nm)
                after = [delta[nm]]
    gsmall_red = _slot_sum(rsmall, "grads_small_sum", 512)
    repl_pack, repl_offs = _pack_rows([g[nm].reshape(-1) for nm in REPL], LANES, 8)
    (repl_all,) = _all_gather([repl_pack], "repl_grads_all_gather")
    repl_red = _slot_sum(repl_all, "repl_grads_sum", 512)
    for nm, val in zip(SMALL_SH, _unpack(gsmall_red, small_offs, [wsh[nm].shape for nm in SMALL_SH])):
        gsh[nm] = val
    for nm, val in zip(REPL, _unpack(repl_red, repl_offs, [wsh[nm].shape for nm in REPL])):
        gsh[nm] = val
    small_names = list(SMALL_SH) + list(REPL)
    packs = []
    for src in (wsh, gsh, msh, vsh):
        pk, sm_offs = _pack_rows([src[nm].reshape(-1) for nm in small_names], LANES, 8)
        packs.append(pk)
    outs = _adamw(*packs, "adamw_small")
    for dst, pk in zip((delta, new_m, new_v), outs):
        for nm, val in zip(small_names, _unpack(pk, sm_offs, [wsh[nm].shape for nm in small_names])):
            dst[nm] = val

    loss = lax.psum(loss_part[0, 0], ("x", "y", "c"))
    return (loss, grad_x[None], *[gsh[nm] for nm in WEIGHTS], *[delta[nm] for nm in WEIGHTS],
            *[new_m[nm] for nm in WEIGHTS], *[new_v[nm] for nm in WEIGHTS])
```

```python
import functools

import jax
import jax.numpy as jnp
from jax import lax
from jax.experimental import pallas as pl
from jax.experimental.pallas import tpu as pltpu

F32 = jnp.float32
BF16 = jnp.bfloat16
MESH_IDS = pl.DeviceIdType.MESH

N_DEV = 8
LANES = 128
RMS_EPS = 1e-6
LN_EPS = 1e-5
DN_HEAD_DIM = 128
DN_CONV = 4
DN_CHUNK = 64
CV_WIDTH = 31
XA_HEADS = 4
EPI_SLAB = 256
DN_HALO = 8
CV_HALO = 32

ADAM_LR = 0.001
ADAM_B1 = 0.9
ADAM_B2 = 0.999
ADAM_EPS = 1e-08
ADAM_WD = 0.01
ADAM_STEP = 10

BIG = (("dn_w_in", 2), ("dn_w_out", 1), ("cv_w_pw1", 2), ("cv_w_pw2", 1), ("xa_w_q", 1), ("xa_w_kv", 2),
       ("xa_w_o", 1), ("mlp_w_up", 2), ("mlp_w_down", 1))
_LAYER_GROUP = ("xa_w_q", "xa_w_o", "mlp_w_down", "xa_w_kv", "mlp_w_up")
GATHER_GROUPS = (
    (("dn_w_in", 0),),
    (("dn_w_out", 0),) + tuple((nm, 0) for nm in _LAYER_GROUP),
    (("cv_w_pw2", 0), ("cv_w_pw1", 0)),
    tuple((nm, 1) for nm in _LAYER_GROUP),
)
SCATTER_GROUPS = (
    (("dn_w_out", 0), ("dn_w_in", 0)),
    tuple((nm, 0) for nm in _LAYER_GROUP),
    (("cv_w_pw2", 0), ("cv_w_pw1", 0)),
    tuple((nm, 1) for nm in _LAYER_GROUP),
)
SMALL_SH = ("cv_norm", "cv_b_pw1", "cv_b_dw", "cv_ln_g", "cv_ln_b", "cv_b_pw2", "cv_w_dw", "dn_w_conv")
REPL = ("dn_norm", "dn_a_log", "dn_dt_bias", "dn_out_norm", "xa_norm", "xa_mem_norm", "mlp_norm", "final_norm")
WEIGHTS = ("dn_norm", "dn_w_in", "dn_w_conv", "dn_a_log", "dn_dt_bias", "dn_out_norm", "dn_w_out", "cv_norm",
           "cv_w_pw1", "cv_b_pw1", "cv_w_dw", "cv_b_dw", "cv_ln_g", "cv_ln_b", "cv_w_pw2", "cv_b_pw2", "xa_norm",
           "xa_mem_norm", "xa_w_q", "xa_w_kv", "xa_w_o", "mlp_norm", "mlp_w_up", "mlp_w_down", "final_norm")


def _dot_dims(mode, batched):
    o = 1 if batched else 0
    contract = {"nn": ((1 + o,), (o,)), "nt": ((1 + o,), (1 + o,)), "tn": ((o,), (o,))}[mode]
    return (contract, (((0,), (0,)) if batched else ((), ())))


def _bdot(a, b, mode):
    return lax.dot_general(a.astype(BF16), b.astype(BF16), _dot_dims(mode, a.ndim == 3),
                           preferred_element_type=F32)


@functools.partial(jax.custom_vjp, nondiff_argnums=(2,))
def _mm(a, b, mode):
    return _bdot(a, b, mode)


def _mm_fwd(a, b, mode):
    return _bdot(a, b, mode), (a, b)


def _mm_bwd(mode, res, ct):
    a, b = res
    if mode == "nn":
        da, db = _bdot(ct, b, "nt"), _bdot(a, ct, "tn")
    elif mode == "nt":
        da, db = _bdot(ct, b, "nn"), _bdot(ct, a, "tn")
    else:
        da, db = _bdot(b, ct, "nt"), _bdot(a, ct, "nn")
    return da.astype(a.dtype), db.astype(b.dtype)


_mm.defvjp(_mm_fwd, _mm_bwd)


def _sigmoid(x):
    return 0.5 * (jnp.tanh(0.5 * x) + 1.0)


def _silu(x):
    return x * _sigmoid(x)


def _softplus(x):
    return jnp.maximum(x, 0.0) + jnp.log(1.0 + jnp.exp(-jnp.abs(x)))


def _rms(x, g):
    r = lax.rsqrt(jnp.mean(x * x, axis=-1, keepdims=True) + RMS_EPS)
    return x * r * g


def _shift_rows(x, off):
    if off == 0:
        return x
    return pltpu.roll(x, x.shape[0] - off, 0)


def _series_dot(a, b, mode):
    return _bdot(a, b, mode)


def _chunk_masks(c):
    ii = lax.broadcasted_iota(jnp.int32, (c, c), 0)
    jj = lax.broadcasted_iota(jnp.int32, (c, c), 1)
    return (ii == jj).astype(F32), ii >= jj, ii > jj


def _neumann_inverse(lm):
    n = lm.shape[-1]
    t = -lm
    p = lm
    size = 2
    while size < n:
        size *= 2
        p = _series_dot(p, p, "nn")
        t = t + p + _series_dot(t, p, "nn")
    return t


def _apply_inverse(tm, rhs, mode):
    return rhs + _series_dot(tm, rhs, mode)


@jax.custom_vjp
def _unit_lower_solve(lm, rhs, tm):
    return _apply_inverse(tm, rhs, "nn")


def _uls_fwd(lm, rhs, tm):
    sol = _apply_inverse(tm, rhs, "nn")
    return sol, (tm, sol)


def _uls_bwd(res, ct):
    tm, sol = res
    d_rhs = _apply_inverse(tm, ct, "tn")
    return -_bdot(d_rhs, sol, "nt"), d_rhs, jnp.zeros_like(tm)


_unit_lower_solve.defvjp(_uls_fwd, _uls_bwd)


def _delta_chunk(q, k, v, gcol, bcol, s0, tm=None):
    c = q.shape[1]
    eye, causal, strict = _chunk_masks(c)
    grow = jnp.sum(eye * gcol, axis=1, keepdims=True)
    gc = jnp.sum(jnp.where(causal, grow, 0.0), axis=2, keepdims=True)
    gc_row = jnp.sum(eye * gc, axis=1, keepdims=True)
    decay = jnp.exp(jnp.where(causal, gc - gc_row, -jnp.inf))
    kb = k * bcol
    lm = jnp.where(strict, _mm(kb, k, "nt") * decay, 0.0)
    if tm is None:
        tm = _neumann_inverse(lax.stop_gradient(lm))
    egc = jnp.exp(gc)
    rhs = jnp.concatenate([v * bcol, kb * egc], axis=-1)
    sol = _unit_lower_solve(lm, rhs, tm)
    dv_ = v.shape[-1]
    u, w = sol[..., :dv_], sol[..., dv_:]
    attn = _mm(q, k, "nt") * decay
    qd = q * egc
    gl = jnp.sum(grow, axis=2, keepdims=True)
    kd = k * jnp.exp(gl - gc)
    v_new = u - _mm(w, s0, "nn")
    o = _mm(qd, s0, "nn") + _mm(attn, v_new, "nn")
    s1 = s0 * jnp.exp(gl) + _mm(kd, v_new, "tn")
    return o, s1, tm


def _dn_point(cv, ba, alog, dt, heads):
    a = _silu(cv)
    d = cv.shape[1] // 3
    qs, ks = [], []
    for h in range(heads):
        qh = a[:, h * DN_HEAD_DIM:(h + 1) * DN_HEAD_DIM]
        qs.append(qh * lax.rsqrt(jnp.sum(qh * qh, axis=-1, keepdims=True) + 1e-6) * (DN_HEAD_DIM ** -0.5))
        kh = a[:, d + h * DN_HEAD_DIM:d + (h + 1) * DN_HEAD_DIM]
        ks.append(kh * lax.rsqrt(jnp.sum(kh * kh, axis=-1, keepdims=True) + 1e-6))
    q = jnp.concatenate(qs, axis=-1)
    k = jnp.concatenate(ks, axis=-1)
    v = a[:, 2 * d:]
    lane = lax.broadcasted_iota(jnp.int32, ba.shape, 1)
    beta = _sigmoid(ba)
    g = -jnp.exp(alog) * _softplus(ba + dt)
    gb = jnp.where(lane < heads, beta, jnp.where(lane < 2 * heads, g, 0.0))
    return q, k, v, gb


def _attn_tile(q, k, v):
    hd = q.shape[1] // XA_HEADS
    outs = []
    for h in range(XA_HEADS):
        sl = slice(h * hd, (h + 1) * hd)
        s = _mm(q[:, sl], k[:, sl], "nt") * (hd ** -0.5)
        m = lax.stop_gradient(jnp.max(s, axis=-1, keepdims=True))
        e = jnp.exp(s - m)
        p = e / jnp.sum(e, axis=-1, keepdims=True)
        outs.append(_mm(p, v[:, sl], "nn"))
    return jnp.concatenate(outs, axis=-1)


def _ln_silu(c, g, b):
    mu = jnp.mean(c, axis=-1, keepdims=True)
    xc = c - mu
    y = xc * lax.rsqrt(jnp.mean(xc * xc, axis=-1, keepdims=True) + LN_EPS)
    return _silu(y * g + b)


def _causal_conv(xext, w, width, lead, ts):
    acc = None
    for j in range(width):
        term = _shift_rows(xext, lead + j)[:ts] * w[j:j + 1, :]
        acc = term if acc is None else acc + term
    return acc


def _colsum(x):
    return jnp.sum(x, axis=0, keepdims=True)


def _stack_rows(rows, n_rows):
    c = rows[0].shape[1]
    ridx = lax.broadcasted_iota(jnp.int32, (n_rows, c), 0)
    out = jnp.zeros((n_rows, c), F32)
    for j, r in enumerate(rows):
        out = out + jnp.where(ridx == j, r, 0.0)
    return out


def _matmul(a, b, mode, out_dtypes, *, name, epi=None, mn_extras=(), row_extras=(), out_dm=False, after=(),
            n_rowsum=0, slab=0, b_rows=None, tm=1024, tn=1024, tk=1024):
    b_dm = b.ndim == 3
    b_shape = (b.shape[1], N_DEV * b.shape[2]) if b_dm else b.shape
    if b_rows is not None:
        assert not b_dm and b_rows <= b.shape[0]
        b_shape = (b_rows, b.shape[1])
    if mode == "nn":
        (m, k), (k2, n) = a.shape, b_shape
    elif mode == "nt":
        (m, k), (n, k2) = a.shape, b_shape
    else:
        (k, m), (k2, n) = a.shape, b_shape
    assert k == k2, (a.shape, b.shape, mode)
    tm, tn, tk = min(tm, m), min(tn, n), min(tk, k)
    cb, nb = 0, 1
    if b_dm:
        assert mode in ("nn", "nt")
        cb = b.shape[2]
        nb = max(1, (tn if mode == "nn" else tk) // cb)
        if mode == "nn":
            tn = nb * cb
        else:
            tk = nb * cb
    co, no = 0, 1
    if out_dm:
        co = n // N_DEV
        no = max(1, tn // co)
        tn = no * co
    assert m % tm == 0 and n % tn == 0 and k % tk == 0, (m, n, k, tm, tn, tk)
    nk = k // tk
    if mode == "tn":
        a_spec = pl.BlockSpec((tk, tm), lambda j, i, kk: (kk, i))
    else:
        a_spec = pl.BlockSpec((tm, tk), lambda j, i, kk: (i, kk))
    if b_dm:
        b_spec = (pl.BlockSpec((nb, tn, cb), lambda j, i, kk: (kk, j, 0)) if mode == "nt"
                  else pl.BlockSpec((nb, tk, cb), lambda j, i, kk: (j, kk, 0)))
    else:
        b_spec = (pl.BlockSpec((tn, tk), lambda j, i, kk: (j, kk)) if mode == "nt"
                  else pl.BlockSpec((tk, tn), lambda j, i, kk: (kk, j)))
    mn_spec = pl.BlockSpec((tm, tn), lambda j, i, kk: (i, j))
    row_spec = pl.BlockSpec((1, tn), lambda j, i, kk: (0, j))
    n_extra = len(mn_extras) + len(row_extras)
    n_out = len(out_dtypes)
    in_specs = ([a_spec, b_spec] + [mn_spec] * len(mn_extras) + [row_spec] * len(row_extras)
                + [_ANY_SPEC] * len(after))
    args = [a, b, *mn_extras, *row_extras, *after]
    if out_dm:
        out_specs = [pl.BlockSpec((no, tm, co), lambda j, i, kk: (j, i, 0))] * n_out
        out_shape = [jax.ShapeDtypeStruct((N_DEV, m, co), dt) for dt in out_dtypes]
    else:
        out_specs = [mn_spec] * n_out
        out_shape = [jax.ShapeDtypeStruct((m, n), dt) for dt in out_dtypes]
    out_specs = out_specs + [row_spec] * n_rowsum
    out_shape = out_shape + [jax.ShapeDtypeStruct((1, n), F32)] * n_rowsum
    n_in = len(args)
    n_mn = len(mn_extras)
    step = min(slab, tm) if slab else tm
    assert tm % step == 0

    def dot(a_ref, b_ref):
        if not b_dm:
            return _bdot(a_ref[...], b_ref[...], mode)
        if mode == "nn":
            parts = [_bdot(a_ref[...], b_ref[dd], "nn") for dd in range(nb)]
            return parts[0] if nb == 1 else jnp.concatenate(parts, axis=1)
        out = None
        for dd in range(nb):
            part = _bdot(a_ref[:, dd * cb:(dd + 1) * cb], b_ref[dd], "nt")
            out = part if out is None else out + part
        return out

    def finish(acc_src, extras, outs):
        sums = [None] * n_rowsum
        for r0 in range(0, tm, step):
            rs = slice(r0, r0 + step)
            acc_val = acc_src[rs, :]
            if epi is None:
                vals = (acc_val,)
            else:
                vals = epi(acc_val, *[e[rs, :] for e in extras[:n_mn]], *[e[...] for e in extras[n_mn:]])
            for o_ref, val in zip(outs[:n_out], vals[:n_out]):
                if out_dm:
                    for dd in range(no):
                        o_ref[dd, rs, :] = val[:, dd * co:(dd + 1) * co].astype(o_ref.dtype)
                else:
                    o_ref[rs, :] = val.astype(o_ref.dtype)
            for q in range(n_rowsum):
                sums[q] = vals[n_out + q] if sums[q] is None else sums[q] + vals[n_out + q]
        for q in range(n_rowsum):
            s_ref = outs[n_out + q]

            @pl.when(pl.program_id(1) == 0)
            def _():
                s_ref[...] = sums[q]

            @pl.when(pl.program_id(1) > 0)
            def _():
                s_ref[...] += sums[q]

    def body_one_step(*refs):
        finish(dot(refs[0], refs[1]), refs[2:2 + n_extra], refs[n_in:])

    def body(*refs):
        a_ref, b_ref = refs[0], refs[1]
        acc = refs[-1]
        kk = pl.program_id(2)

        @pl.when(kk == 0)
        def _():
            acc[...] = jnp.zeros_like(acc)

        acc[...] += dot(a_ref, b_ref)

        @pl.when(kk == nk - 1)
        def _():
            finish(acc, refs[2:2 + n_extra], refs[n_in:-1])

    res = pl.pallas_call(
        body_one_step if nk == 1 else body, name=name,
        grid=(n // tn, m // tm, nk),
        in_specs=in_specs, out_specs=out_specs, out_shape=out_shape,
        scratch_shapes=[] if nk == 1 else [pltpu.VMEM((tm, tn), F32)],
        compiler_params=pltpu.CompilerParams(
            dimension_semantics=("parallel", "arbitrary" if n_rowsum else "parallel", "arbitrary")),
    )(*args)
    return res[0] if n_out + n_rowsum == 1 else res


def _rowwise(fn, *, n_rows, ts, name, rows=(), prevs=(), nexts=(), vecs=(), row_outs=(), acc_outs=(), after=()):
    ts = min(ts, n_rows)
    assert n_rows % ts == 0
    nblk = n_rows // ts
    in_specs, args = [], []
    for arr, cb, w in rows:
        in_specs.append(pl.BlockSpec((ts, w), functools.partial(lambda i, cb: (i, cb), cb=cb)))
        args.append(arr)
    for arr, cb, w, halo in prevs:
        per = ts // halo
        in_specs.append(pl.BlockSpec(
            (halo, w), functools.partial(lambda i, cb, per: (jnp.maximum(i * per - 1, 0), cb), cb=cb, per=per)))
        args.append(arr)
    for arr, cb, w, halo in nexts:
        per = ts // halo
        last_blk = n_rows // halo - 1
        in_specs.append(pl.BlockSpec(
            (halo, w), functools.partial(lambda i, cb, per, lb: (jnp.minimum((i + 1) * per, lb), cb),
                                         cb=cb, per=per, lb=last_blk)))
        args.append(arr)
    for arr in vecs:
        in_specs.append(pl.BlockSpec(arr.shape, functools.partial(lambda i, nd: (0,) * nd, nd=arr.ndim)))
        args.append(arr)
    out_specs, out_shape = [], []
    for w, dt in row_outs:
        out_specs.append(pl.BlockSpec((ts, w), lambda i: (i, 0)))
        out_shape.append(jax.ShapeDtypeStruct((n_rows, w), dt))
    for shp in acc_outs:
        out_specs.append(pl.BlockSpec(shp, functools.partial(lambda i, nd: (0,) * nd, nd=len(shp))))
        out_shape.append(jax.ShapeDtypeStruct(shp, F32))
    n_used = len(args)
    n_tiles = n_used - len(vecs)
    in_specs += [_ANY_SPEC] * len(after)
    args += list(after)
    n_in, n_ro, n_acc = len(args), len(row_outs), len(acc_outs)

    def body(*refs):
        ins, ro, ac = refs[:n_used], refs[n_in:n_in + n_ro], refs[n_in + n_ro:]
        i = pl.program_id(0)
        rvals, avals = fn(i == 0, i == nblk - 1, *[r[...] for r in ins[:n_tiles]], *ins[n_tiles:])
        for r, val in zip(ro, rvals):
            r[...] = val.astype(r.dtype)
        if n_acc:
            @pl.when(i == 0)
            def _():
                for r in ac:
                    r[...] = jnp.zeros_like(r)

            for r, val in zip(ac, avals):
                r[...] += val

    res = pl.pallas_call(
        body, name=name, grid=(nblk,), in_specs=in_specs, out_specs=out_specs, out_shape=out_shape,
        compiler_params=pltpu.CompilerParams(dimension_semantics=("arbitrary",)),
    )(*args)
    return res


def _gated_out(o, z, onorm):
    return o * lax.rsqrt(jnp.mean(o * o, axis=-1, keepdims=True) + RMS_EPS) * onorm * _silu(z)


def _head_blocks(ref, heads, col0=0):
    return jnp.stack([ref[:, col0 + h * DN_HEAD_DIM:col0 + (h + 1) * DN_HEAD_DIM] for h in range(heads)])


def _split_heads(q_ref, k_ref, v_ref, gbv, heads):
    gcol = jnp.stack([gbv[:, heads + h:heads + h + 1] for h in range(heads)])
    bcol = jnp.stack([gbv[:, h:h + 1] for h in range(heads)])
    return _head_blocks(q_ref, heads), _head_blocks(k_ref, heads), _head_blocks(v_ref, heads), gcol, bcol


def _delta_fwd(q, k, v, gb, qkvz, onorm, heads):
    s, hd = q.shape
    n = s // DN_CHUNK
    blk = pl.BlockSpec((DN_CHUNK, hd), lambda c: (c, 0))
    gspec = pl.BlockSpec((DN_CHUNK, LANES), lambda c: (c, 0))

    def body(q_ref, k_ref, v_ref, gb_ref, z_ref, on_ref, og_ref, st_ref, tm_ref, state):
        @pl.when(pl.program_id(0) == 0)
        def _():
            state[...] = jnp.zeros_like(state)

        s0 = state[...]
        st_ref[0] = s0
        o, s1, tm = _delta_chunk(*_split_heads(q_ref, k_ref, v_ref, gb_ref[...], heads), s0)
        og = _gated_out(o, _head_blocks(z_ref, heads), on_ref[...])
        for h in range(heads):
            og_ref[:, h * DN_HEAD_DIM:(h + 1) * DN_HEAD_DIM] = og[h].astype(og_ref.dtype)
        state[...] = s1
        tm_ref[0] = tm

    return pl.pallas_call(
        body, name="dn_delta_fwd", grid=(n,),
        in_specs=[blk, blk, blk, gspec, pl.BlockSpec((DN_CHUNK, hd), lambda c: (c, 3)),
                  pl.BlockSpec(onorm.shape, lambda c: (0, 0))],
        out_specs=[blk, pl.BlockSpec((1, heads, DN_HEAD_DIM, DN_HEAD_DIM), lambda c: (c, 0, 0, 0)),
                   pl.BlockSpec((1, heads, DN_CHUNK, DN_CHUNK), lambda c: (c, 0, 0, 0))],
        out_shape=[jax.ShapeDtypeStruct((s, hd), BF16),
                   jax.ShapeDtypeStruct((n, heads, DN_HEAD_DIM, DN_HEAD_DIM), F32),
                   jax.ShapeDtypeStruct((n, heads, DN_CHUNK, DN_CHUNK), F32)],
        scratch_shapes=[pltpu.VMEM((heads, DN_HEAD_DIM, DN_HEAD_DIM), F32)],
        compiler_params=pltpu.CompilerParams(dimension_semantics=("arbitrary",)),
    )(q, k, v, gb, qkvz, onorm)


def _delta_bwd(q, k, v, gb, qkvz, onorm, states, tms, dog, heads):
    s, hd = q.shape
    n = s // DN_CHUNK
    blk = pl.BlockSpec((DN_CHUNK, hd), lambda c: (n - 1 - c, 0))
    gspec = pl.BlockSpec((DN_CHUNK, LANES), lambda c: (n - 1 - c, 0))
    sspec = pl.BlockSpec((1, heads, DN_HEAD_DIM, DN_HEAD_DIM), lambda c: (n - 1 - c, 0, 0, 0))
    tspec = pl.BlockSpec((1, heads, DN_CHUNK, DN_CHUNK), lambda c: (n - 1 - c, 0, 0, 0))
    nspec = pl.BlockSpec(onorm.shape, lambda c: (0, 0))

    def body(q_ref, k_ref, v_ref, gb_ref, z_ref, on_ref, st_ref, tm_ref, dog_ref,
             dq_ref, dk_ref, dv_ref, dgb_ref, dz_ref, don_ref, dstate):
        @pl.when(pl.program_id(0) == 0)
        def _():
            dstate[...] = jnp.zeros_like(dstate)
            don_ref[...] = jnp.zeros_like(don_ref)

        gbv = gb_ref[...]
        tm = tm_ref[0]

        def chunk(qh, kh, vh, gcol, bcol, s0, zh, on):
            o, s1, _ = _delta_chunk(qh, kh, vh, gcol, bcol, s0, tm)
            return _gated_out(o, zh, on), s1

        _, vjp = jax.vjp(chunk, *_split_heads(q_ref, k_ref, v_ref, gbv, heads), st_ref[0],
                         _head_blocks(z_ref, heads), on_ref[...])
        dq, dk, dv, dg, db, ds0, dz, don = vjp((_head_blocks(dog_ref, heads).astype(F32), dstate[...]))
        dstate[...] = ds0
        don_ref[...] += don
        lane = lax.broadcasted_iota(jnp.int32, gbv.shape, 1)
        dgb = jnp.zeros(gbv.shape, F32)
        for h in range(heads):
            sl = slice(h * DN_HEAD_DIM, (h + 1) * DN_HEAD_DIM)
            dq_ref[:, sl] = dq[h]
            dk_ref[:, sl] = dk[h]
            dv_ref[:, sl] = dv[h]
            dz_ref[:, sl] = dz[h]
            dgb = dgb + jnp.where(lane == h, db[h], 0.0) + jnp.where(lane == heads + h, dg[h], 0.0)
        dgb_ref[...] = dgb

    return pl.pallas_call(
        body, name="dn_delta_bwd", grid=(n,),
        in_specs=[blk, blk, blk, gspec, pl.BlockSpec((DN_CHUNK, hd), lambda c: (n - 1 - c, 3)), nspec,
                  sspec, tspec, blk],
        out_specs=[blk, blk, blk, gspec, blk, nspec],
        out_shape=[jax.ShapeDtypeStruct((s, hd), F32)] * 3 + [jax.ShapeDtypeStruct((s, LANES), F32),
                                                              jax.ShapeDtypeStruct((s, hd), F32),
                                                              jax.ShapeDtypeStruct(onorm.shape, F32)],
        scratch_shapes=[pltpu.VMEM((heads, DN_HEAD_DIM, DN_HEAD_DIM), F32)],
        compiler_params=pltpu.CompilerParams(dimension_semantics=("arbitrary",)),
    )(q, k, v, gb, qkvz, onorm, states, tms, dog)


def _dev_index(px, py, pc):
    return 4 * px + 2 * py + pc


def _all_gather(arrs, name):
    n = len(arrs)

    def body(*refs):
        xs, outs = refs[:n], refs[n:2 * n]
        send_sems, recv_sems, local_sems = refs[2 * n:]
        x, y, c = lax.axis_index("x"), lax.axis_index("y"), lax.axis_index("c")
        me, sibling = (x, y, c), (x, y, 1 - c)
        chips = [(1 - x, y), (x, 1 - y), (1 - x, 1 - y)]

        def copy(a, kk, block, to, src=None):
            dst = outs[a].at[_dev_index(*block)]
            return pltpu.make_async_remote_copy(
                src_ref=dst if src is None else src, dst_ref=dst,
                send_sem=send_sems.at[a * 7 + kk], recv_sem=recv_sems.at[a * 7 + kk],
                device_id=to, device_id_type=MESH_IDS)

        mine = [pltpu.make_async_copy(xs[a], outs[a].at[_dev_index(*me)], local_sems.at[a]) for a in range(n)]
        for cp in mine:
            cp.start()
        first = []
        for a in range(n):
            first.append(copy(a, 0, me, sibling, src=xs[a]))
            first += [copy(a, 1 + j, me, (*chip, c), src=xs[a]) for j, chip in enumerate(chips)]
        for cp in first:
            cp.start()
        passed = []
        for j, chip in enumerate(chips):
            for a in range(n):
                copy(a, 1 + j, (*chip, c), me).wait_recv()
                fwd = copy(a, 4 + j, (*chip, c), sibling)
                fwd.start()
                passed.append(fwd)
        for a in range(n):
            copy(a, 0, sibling, me).wait_recv()
        for j, chip in enumerate(chips):
            for a in range(n):
                copy(a, 4 + j, (*chip, 1 - c), me).wait_recv()
        for cp in first + passed:
            cp.wait_send()
        for cp in mine:
            cp.wait()

    hbm = pl.BlockSpec(memory_space=pltpu.HBM)
    res = pl.pallas_call(
        body, name=name,
        in_specs=[hbm] * n, out_specs=[hbm] * n,
        out_shape=[jax.ShapeDtypeStruct((N_DEV,) + a.shape, a.dtype) for a in arrs],
        scratch_shapes=[pltpu.SemaphoreType.DMA((7 * n,)), pltpu.SemaphoreType.DMA((7 * n,)),
                        pltpu.SemaphoreType.DMA((n,))],
    )(*arrs)
    return list(res)


_FLIPS = ((0, 0, 1), (1, 0, 0), (0, 1, 0), (1, 1, 0), (1, 0, 1), (0, 1, 1), (1, 1, 1))
_HBM_SPEC = pl.BlockSpec(memory_space=pltpu.HBM)
_SEM_SPEC = pl.BlockSpec(memory_space=pltpu.SEMAPHORE)
_ANY_SPEC = pl.BlockSpec(memory_space=pl.ANY)
_DATAFLOW = pltpu.SideEffectType.DATAFLOW_SIDE_EFFECTING
TOKEN_SHAPE = (8, LANES)


def _mesh_me():
    return lax.axis_index("x"), lax.axis_index("y"), lax.axis_index("c")


def _flipped(me, f):
    return tuple(1 - v if fl else v for v, fl in zip(me, f))


def _exchange_copies(xs, lands, send_sems, recv_sems, scatter, landed):
    me = _mesh_me()
    cps = []
    for kk, f in enumerate(_FLIPS):
        p = _flipped(me, f)
        for a in range(len(xs)):
            cps.append(pltpu.make_async_remote_copy(
                src_ref=xs[a].at[_dev_index(*p)] if scatter else xs[a],
                dst_ref=lands[a].at[_dev_index(*(p if landed else me))],
                send_sem=send_sems.at[a * 7 + kk], recv_sem=recv_sems.at[a * 7 + kk],
                device_id=p, device_id_type=MESH_IDS))
    return cps


def _exchange_start(srcs, lands, scatter, name, after=()):
    n = len(srcs)

    n_after = len(after)

    def body(*refs):
        xs, ls = refs[:n], refs[n:2 * n]
        send_sems, recv_sems = refs[2 * n + n_after], refs[2 * n + n_after + 1]
        token = refs[-1]
        for cp in _exchange_copies(xs, ls, send_sems, recv_sems, scatter, landed=False):
            cp.start()
        token[...] = jnp.zeros_like(token)

    operands = [pltpu.with_memory_space_constraint(a, pltpu.HBM) for a in list(srcs) + list(lands)]
    res = pl.pallas_call(
        body, name=name,
        in_specs=[_HBM_SPEC] * (2 * n) + [_ANY_SPEC] * len(after),
        out_specs=[_SEM_SPEC, _SEM_SPEC] + [_HBM_SPEC] * (2 * n) + [pl.BlockSpec(memory_space=pltpu.VMEM)],
        out_shape=[pltpu.SemaphoreType.DMA((7 * n,)), pltpu.SemaphoreType.DMA((7 * n,))]
        + [pltpu.HBM(a.shape, a.dtype) for a in operands] + [jax.ShapeDtypeStruct(TOKEN_SHAPE, F32)],
        input_output_aliases={i: 2 + i for i in range(2 * n)},
        compiler_params=pltpu.CompilerParams(has_side_effects=_DATAFLOW),
    )(*operands, *after)
    return (res[0], res[1], list(res[2:2 + n]), list(res[2 + n:2 + 2 * n]), scatter, name), res[-1]


def _exchange_wait(handle, after):
    send_sems, recv_sems, srcs, lands, scatter, name = handle
    n = len(srcs)
    n_after = len(after)

    def body(*refs):
        xs, ls = refs[:n], refs[n:2 * n]
        send_sems_ref, recv_sems_ref = refs[2 * n], refs[2 * n + 1]
        for cp in _exchange_copies(xs, ls, send_sems_ref, recv_sems_ref, scatter, landed=True):
            cp.wait_send()
            cp.wait_recv()

    res = pl.pallas_call(
        body, name=name + "_wait",
        in_specs=[_HBM_SPEC] * (2 * n) + [_SEM_SPEC, _SEM_SPEC] + [_ANY_SPEC] * n_after,
        out_specs=[_HBM_SPEC] * (2 * n),
        out_shape=[pltpu.HBM(a.shape, a.dtype) for a in srcs + lands],
        input_output_aliases={i: i for i in range(2 * n)},
        compiler_params=pltpu.CompilerParams(has_side_effects=_DATAFLOW),
    )(*srcs, *lands, send_sems, recv_sems, *after)
    return list(res[:n]), list(res[n:])


def _slot_sum(g, name, tr):
    _, r, c = g.shape
    tr = min(tr, r)
    assert r % tr == 0

    def body(g_ref, o_ref):
        acc = g_ref[0].astype(F32)
        for s in range(1, N_DEV):
            acc = acc + g_ref[s].astype(F32)
        o_ref[...] = acc

    return pl.pallas_call(
        body, name=name, grid=(r // tr,),
        in_specs=[pl.BlockSpec((N_DEV, tr, c), lambda i: (0, i, 0))],
        out_specs=pl.BlockSpec((tr, c), lambda i: (i, 0)),
        out_shape=jax.ShapeDtypeStruct((r, c), F32),
        compiler_params=pltpu.CompilerParams(dimension_semantics=("parallel",)),
    )(g)


def _adam_update(w, gg, m, v):
    c1 = 1.0 / (1.0 - ADAM_B1 ** ADAM_STEP)
    c2 = 1.0 / (1.0 - ADAM_B2 ** ADAM_STEP)
    nm = ADAM_B1 * m + (1.0 - ADAM_B1) * gg
    nv = ADAM_B2 * v + (1.0 - ADAM_B2) * (gg * gg)
    return -ADAM_LR * ((nm * c1) / (jnp.sqrt(nv * c2) + ADAM_EPS) + ADAM_WD * w), nm, nv


def _adamw_reduce(me, recvs, owns, w, m, v, name, tr=128):
    nl, r, c = w.shape
    assert len(recvs) == nl and len(owns) == nl
    tr = min(tr, r)
    if r % tr == 0:
        tc, nblk = c, r // tr
        at = lambda i: (i, 0)
    else:
        tr, tc = r, min(c, 4 * LANES)
        assert c % tc == 0
        nblk = c // tc
        at = lambda i: (0, i)

    def parked(li, l, i):
        return jnp.where(l < li, 0, jnp.where(l > li, nblk - 1, i))

    def recv_spec(li):
        return pl.BlockSpec((N_DEV, tr, tc), lambda l, i, me_ref: (0, *at(parked(li, l, i))))

    def own_spec(li):
        return pl.BlockSpec((None, tr, tc), lambda l, i, me_ref: (me_ref[0], *at(parked(li, l, i))))

    def body(me_ref, *refs):
        rrefs, orefs = refs[:nl], refs[nl:2 * nl]
        w_ref, m_ref, v_ref, g_ref, d_ref, nm_ref, nv_ref = refs[2 * nl:]
        l = pl.program_id(0)

        def of_layer(vals):
            out = vals[0]
            for li in range(1, nl):
                out = jnp.where(l == li, vals[li], out)
            return out

        own = of_layer([o[...].astype(F32) for o in orefs])
        gg = None
        for s in range(N_DEV):
            slot = jnp.where(me_ref[0] == s, own, of_layer([rr[s].astype(F32) for rr in rrefs]))
            gg = slot if gg is None else gg + slot
        g_ref[...] = gg
        d_ref[...], nm_ref[...], nv_ref[...] = _adam_update(w_ref[...], gg, m_ref[...], v_ref[...])

    spec = pl.BlockSpec((None, tr, tc), lambda l, i, me_ref: (l, *at(i)))
    return pl.pallas_call(
        body, name=name,
        grid_spec=pltpu.PrefetchScalarGridSpec(
            num_scalar_prefetch=1, grid=(nl, nblk),
            in_specs=[recv_spec(li) for li in range(nl)] + [own_spec(li) for li in range(nl)] + [spec] * 3,
            out_specs=[spec] * 4),
        out_shape=[jax.ShapeDtypeStruct((nl, r, c), F32)] * 4,
        compiler_params=pltpu.CompilerParams(dimension_semantics=("arbitrary", "arbitrary")),
    )(me, *recvs, *owns, w, m, v)


def _adamw(w, g, m, v, name, tr=256):
    r, c = w.shape
    tr = min(tr, r)
    assert r % tr == 0

    def body(w_ref, g_ref, m_ref, v_ref, d_ref, nm_ref, nv_ref):
        d_ref[...], nm_ref[...], nv_ref[...] = _adam_update(w_ref[...], g_ref[...], m_ref[...], v_ref[...])

    spec = pl.BlockSpec((tr, c), lambda i: (i, 0))
    return pl.pallas_call(
        body, name=name, grid=(r // tr,), in_specs=[spec] * 4, out_specs=[spec] * 3,
        out_shape=[jax.ShapeDtypeStruct((r, c), F32)] * 3,
        compiler_params=pltpu.CompilerParams(dimension_semantics=("parallel",)),
    )(w, g, m, v)


def _rms_fwd(x, g, name, ts=512, after=()):
    s, d = x.shape

    def fn(first, last, xv, gv):
        return [_rms(xv, gv[...])], []

    return _rowwise(fn, n_rows=s, ts=ts, name=name, rows=[(x, 0, d)], vecs=[g], row_outs=[(d, BF16)],
                    after=after)[0]


def _rms_bwd(x, dn, dres, g, name, ts=256):
    s, d = x.shape

    def fn(first, last, xv, dnv, drv, gv):
        _, vjp = jax.vjp(_rms, xv, gv[...])
        dx, dg = vjp(dnv.astype(F32))
        return [drv + dx], [dg]

    return _rowwise(fn, n_rows=s, ts=ts, name=name, rows=[(x, 0, d), (dn, 0, d), (dres, 0, d)], vecs=[g],
                    row_outs=[(d, F32)], acc_outs=[(1, d)])


def _loss_head(h, tgt, g, ts=256):
    s, d = h.shape

    def fn(first, last, hv, tv, gv):
        def f(hh, gg):
            e = _rms(hh, gg) - tv
            per_row = jnp.mean(e * e, axis=-1, keepdims=True)
            return 0.5 * jnp.sum(per_row, axis=0, keepdims=True)

        l, vjp = jax.vjp(f, hv, gv[...])
        dh, dg = vjp(jnp.ones((1, 1), F32))
        return [dh], [dg, jnp.zeros((1, LANES), F32) + l]

    return _rowwise(fn, n_rows=s, ts=ts, name="loss_head", rows=[(h, 0, d), (tgt, 0, d)], vecs=[g],
                    row_outs=[(d, F32)], acc_outs=[(1, d), (1, LANES)])


def _dn_pre_fwd(qkvz, ba, wconv, alog, dt, heads, ts=128):
    s = qkvz.shape[0]
    d3 = wconv.shape[1]
    d = d3 // 3

    def fn(first, last, xc, bav, xp, wv, av, dv):
        xext = jnp.concatenate([jnp.where(first, 0.0, xp), xc], axis=0)
        cv = _causal_conv(xext, wv, DN_CONV, DN_HALO - (DN_CONV - 1), xc.shape[0])
        return list(_dn_point(cv, bav, av[...], dv[...], heads)), []

    return _rowwise(fn, n_rows=s, ts=ts, name="dn_pre_fwd", rows=[(qkvz, 0, d3), (ba, 0, LANES)],
                    prevs=[(qkvz, 0, d3, DN_HALO)], vecs=[wconv, alog, dt],
                    row_outs=[(d, F32), (d, F32), (d, F32), (LANES, F32)])


def _dn_pre_bwd(qkvz, ba, wconv, alog, dt, dq, dk, dv, dgb, dz, heads, ts=128):
    s = qkvz.shape[0]
    d3 = wconv.shape[1]
    d = d3 // 3
    lead = DN_HALO - (DN_CONV - 1)

    def fn(first, last, xc, bac, dqc, dkc, dvc, dgbc, dzc, xp, xn, ban, dqn, dkn, dvn, dgbn, wv, av, dtv):
        n = xc.shape[0]
        ext = lambda cur, nxt: jnp.concatenate([cur, nxt], axis=0)
        live = lambda nxt: jnp.where(last, 0.0, nxt)
        xall = jnp.concatenate([jnp.where(first, 0.0, xp), xc, live(xn)], axis=0)
        cv = _causal_conv(xall, wv, DN_CONV, lead, n + DN_HALO)
        (_, _, _, gbv), vjp = jax.vjp(lambda c, b: _dn_point(c, b, av[...], dtv[...], heads), cv, ext(bac, ban))
        dc, dba = vjp((ext(dqc, live(dqn)), ext(dkc, live(dkn)), ext(dvc, live(dvn)), ext(dgbc, live(dgbn))))
        dx = None
        dw = []
        for j in range(DN_CONV):
            term = _shift_rows(dc, DN_CONV - 1 - j)[:n] * wv[j:j + 1, :]
            dx = term if dx is None else dx + term
            dw.append(_colsum(dc[:n] * _shift_rows(xall, lead + j)[:n]))
        dba = dba[:n]
        return ([jnp.concatenate([dx, dzc], axis=-1), dba],
                [_stack_rows(dw, DN_CONV), _colsum(dgbc * gbv[:n]), _colsum(dba)])

    return _rowwise(fn, n_rows=s, ts=ts, name="dn_pre_bwd",
                    rows=[(qkvz, 0, d3), (ba, 0, LANES), (dq, 0, d), (dk, 0, d), (dv, 0, d), (dgb, 0, LANES),
                          (dz, 0, d)],
                    prevs=[(qkvz, 0, d3, DN_HALO)],
                    nexts=[(qkvz, 0, d3, DN_HALO), (ba, 0, LANES, DN_HALO), (dq, 0, d, DN_HALO),
                           (dk, 0, d, DN_HALO), (dv, 0, d, DN_HALO), (dgb, 0, LANES, DN_HALO)],
                    vecs=[wconv, alog, dt],
                    row_outs=[(4 * d, BF16), (LANES, BF16)], acc_outs=[(DN_CONV, d3), (1, LANES), (1, LANES)])


def _cv_mid_fwd(u, wdw, bdw, lng, lnb, ts=256):
    s = u.shape[0]
    d = u.shape[1] // 2

    def fn(first, last, uc, up, wv, bv, gv, lbv):
        uext = jnp.concatenate([jnp.where(first, 0.0, up), uc], axis=0)
        glu = uext[:, :d] * _sigmoid(uext[:, d:])
        c = _causal_conv(glu, wv, CV_WIDTH, CV_HALO - (CV_WIDTH - 1), uc.shape[0]) + bv[...]
        return [c, _ln_silu(c, gv[...], lbv[...])], []

    return _rowwise(fn, n_rows=s, ts=ts, name="cv_mid_fwd", rows=[(u, 0, 2 * d)], prevs=[(u, 0, 2 * d, CV_HALO)],
                    vecs=[wdw, bdw, lng, lnb], row_outs=[(d, F32), (d, BF16)])


def _cv_mid_bwd1(c, ds, dhp, lng, lnb, ts=256):
    s, d = c.shape

    def fn(first, last, cv, dsv, dhv, gv, bv):
        _, vjp = jax.vjp(_ln_silu, cv, gv[...], bv[...])
        dc, dg, db = vjp(dsv.astype(F32))
        return [dc], [dg, db, _colsum(dc), _colsum(dhv)]

    return _rowwise(fn, n_rows=s, ts=ts, name="cv_mid_bwd1", rows=[(c, 0, d), (ds, 0, d), (dhp, 0, d)],
                    vecs=[lng, lnb], row_outs=[(d, F32)], acc_outs=[(1, d)] * 4)


def _cv_mid_bwd2(dc, u, wdw, ts=256):
    s, d = dc.shape

    def fn(first, last, dcc, uc, up, dcn, wv):
        n = dcc.shape[0]
        dcext = jnp.concatenate([dcc, jnp.where(last, 0.0, dcn)], axis=0)
        uext = jnp.concatenate([jnp.where(first, 0.0, up), uc], axis=0)
        glu = uext[:, :d] * _sigmoid(uext[:, d:])
        dglu = None
        dw = []
        for j in range(CV_WIDTH):
            term = _shift_rows(dcext, CV_WIDTH - 1 - j)[:n] * wv[j:j + 1, :]
            dglu = term if dglu is None else dglu + term
            dw.append(_colsum(dcc * _shift_rows(glu, CV_HALO - (CV_WIDTH - 1) + j)[:n]))
        u1, sg = uc[:, :d], _sigmoid(uc[:, d:])
        du = jnp.concatenate([dglu * sg, dglu * u1 * sg * (1.0 - sg)], axis=-1)
        return [du], [_stack_rows(dw, CV_HALO), _colsum(du)]

    return _rowwise(fn, n_rows=s, ts=ts, name="cv_mid_bwd2", rows=[(dc, 0, d), (u, 0, 2 * d)],
                    prevs=[(u, 0, 2 * d, CV_HALO)], nexts=[(dc, 0, d, CV_HALO)], vecs=[wdw],
                    row_outs=[(2 * d, BF16)], acc_outs=[(CV_HALO, d), (1, 2 * d)])


def _attn_fwd(q, k, v, name, ts=512):
    s, d = q.shape

    def fn(first, last, qv, kv, vv):
        return [_attn_tile(qv.astype(F32), kv[...].astype(F32), vv[...].astype(F32))], []

    return _rowwise(fn, n_rows=s, ts=ts, name=name, rows=[(q, 0, d)], vecs=[k, v], row_outs=[(d, BF16)])[0]


def _attn_bwd(q, k, v, do, name, ts=512):
    s, d = q.shape
    m = k.shape[0]

    def fn(first, last, qv, dov, kv, vv):
        _, vjp = jax.vjp(_attn_tile, qv.astype(F32), kv[...].astype(F32), vv[...].astype(F32))
        dq, dk, dv = vjp(dov.astype(F32))
        return [dq], [dk, dv]

    return _rowwise(fn, n_rows=s, ts=ts, name=name, rows=[(q, 0, d), (do, 0, d)], vecs=[k, v],
                    row_outs=[(d, BF16)], acc_outs=[(m, d), (m, d)])


def _pad_lanes(a, off=0):
    r, n = a.shape
    return jnp.pad(a, ((0, 0), (off, LANES - off - n)))


def _local_step(x, mem, tgt, w, fetch=None, emit=None, first_after=()):
    s, d = x.shape
    heads = d // DN_HEAD_DIM
    g = {}
    if fetch is None:
        fetch = lambda group, after: None
    if emit is None:
        emit = lambda group, grads: ()

    def add_res(acc, res):
        return (res + acc,)

    def add_res_rms(acc, res, gain):
        h = res + acc
        return h, _rms(h, gain)

    def rms_bwd_epi(acc, hx, dres, gain):
        _, vjp = jax.vjp(_rms, hx, gain)
        dx, dg = vjp(acc)
        return dres + dx, dg

    w_int = w["dn_w_in"][0]
    assert w_int.shape[0] == 4 * d + 2 * heads
    w_bat = jnp.pad(w_int[4 * d:], ((0, LANES - 2 * heads), (0, 0)))
    dn_norm = w["dn_norm"]
    alog = _pad_lanes(w["dn_a_log"], heads)
    dtb = _pad_lanes(w["dn_dt_bias"], heads)
    wconv = w["dn_w_conv"][0]
    n0 = _rms_fwd(x, dn_norm, "dn_rms", after=first_after)
    qkvz = _matmul(n0, w_int, "nt", [F32], name="dn_in_proj", b_rows=4 * d)
    ba = _matmul(n0, w_bat, "nt", [F32], name="dn_in_proj_ba")
    q, k, v, gb = _dn_pre_fwd(qkvz, ba, wconv, alog, dtb, heads)
    og, states, tms = _delta_fwd(q, k, v, gb, qkvz, w["dn_out_norm"], heads)
    fetch(1, [og])
    h1, nq0 = _matmul(og, w["dn_w_out"][0], "nn", [F32, BF16], name="dn_out_proj", epi=add_res_rms,
                      mn_extras=[x], row_extras=[w["xa_norm"][0:1]], slab=EPI_SLAB)

    def xattn_fwd(h, nq, layer, next_gain):
        qx = _matmul(nq, w["xa_w_q"][layer], "nn", [BF16], name=f"xa{layer}_q")
        mn = _rms_fwd(mem, w["xa_mem_norm"][layer:layer + 1], f"xa{layer}_mem_rms")
        kv = _matmul(mn, w["xa_w_kv"][layer], "nn", [BF16], name=f"xa{layer}_kv")
        kx, vx = kv[:, :d], kv[:, d:]
        ox = _attn_fwd(qx, kx, vx, f"xa{layer}_attn")
        hn, nn = _matmul(ox, w["xa_w_o"][layer], "nn", [F32, BF16], name=f"xa{layer}_o", epi=add_res_rms,
                         mn_extras=[h], row_extras=[next_gain], slab=EPI_SLAB)
        return hn, nn, (h, nq, qx, mn, kx, vx, ox)

    def mlp_fwd(h, nm, layer, next_gain):
        def epi(acc):
            r = jnp.maximum(acc, 0.0)
            return acc, r * r

        u, a = _matmul(nm, w["mlp_w_up"][layer], "nn", [BF16, BF16], name=f"mlp{layer}_up", epi=epi)
        if next_gain is None:
            hn, nn = _matmul(a, w["mlp_w_down"][layer], "nn", [F32], name=f"mlp{layer}_down", epi=add_res,
                             mn_extras=[h], tk=2048), None
        else:
            hn, nn = _matmul(a, w["mlp_w_down"][layer], "nn", [F32, BF16], name=f"mlp{layer}_down",
                             epi=add_res_rms, mn_extras=[h], row_extras=[next_gain], tk=2048, slab=EPI_SLAB)
        return hn, nn, (h, nm, u, a)

    h2, nm0, xa0 = xattn_fwd(h1, nq0, 0, w["mlp_norm"][0:1])
    fetch(2, [h2])
    h3, n1, mlp0 = mlp_fwd(h2, nm0, 0, w["cv_norm"])

    u_cv = _matmul(n1, w["cv_w_pw1"][0], "nn", [F32], name="cv_pw1", epi=lambda acc, b: (acc + b,),
                   row_extras=[w["cv_b_pw1"]])
    wdw = jnp.pad(w["cv_w_dw"][0], ((0, CV_HALO - CV_WIDTH), (0, 0)))
    c_cv, s_cv = _cv_mid_fwd(u_cv, wdw, w["cv_b_dw"], w["cv_ln_g"], w["cv_ln_b"])
    h4, nq1 = _matmul(s_cv, w["cv_w_pw2"][0], "nn", [F32, BF16], name="cv_pw2",
                      epi=lambda acc, res, b, gain: add_res_rms(acc + b, res, gain), mn_extras=[h3],
                      row_extras=[w["cv_b_pw2"], w["xa_norm"][1:2]], slab=EPI_SLAB)
    fetch(3, [h4])
    h5, nm1, xa1 = xattn_fwd(h4, nq1, 1, w["mlp_norm"][1:2])
    h6, _, mlp1 = mlp_fwd(h5, nm1, 1, None)

    fnorm = w["final_norm"].reshape(1, d)
    dh, g_fn, loss = _loss_head(h6, tgt, fnorm)
    g["final_norm"] = g_fn.reshape(d)

    def mlp_bwd(dh, layer, saved, after=()):
        h, nm, u, a = saved
        du = _matmul(dh, w["mlp_w_down"][layer], "nt", [BF16], name=f"mlp{layer}_down_dx", after=after,
                     epi=lambda acc, uu: (acc * 2.0 * jnp.maximum(uu.astype(F32), 0.0),), mn_extras=[u])
        gdown = _matmul(a, dh, "tn", [BF16], name=f"mlp{layer}_down_dw", tm=2048)
        dhn, gn = _matmul(du, w["mlp_w_up"][layer], "nt", [F32], name=f"mlp{layer}_up_dx", epi=rms_bwd_epi,
                          mn_extras=[h, dh], row_extras=[w["mlp_norm"][layer:layer + 1]], n_rowsum=1,
                          slab=EPI_SLAB)
        gup = _matmul(nm, du, "tn", [BF16], name=f"mlp{layer}_up_dw", out_dm=True, tk=2048)
        return dhn, gup, gdown, gn

    def xattn_bwd(dh, layer, saved):
        h, nq, qx, mn, kx, vx, ox = saved
        dox = _matmul(dh, w["xa_w_o"][layer], "nt", [BF16], name=f"xa{layer}_o_dx")
        go = _matmul(ox, dh, "tn", [BF16], name=f"xa{layer}_o_dw", tk=2048)
        dqx, dkx, dvx = _attn_bwd(qx, kx, vx, dox, f"xa{layer}_attn_bwd")
        dhn, gn = _matmul(dqx, w["xa_w_q"][layer], "nt", [F32], name=f"xa{layer}_q_dx", epi=rms_bwd_epi,
                          mn_extras=[h, dh], row_extras=[w["xa_norm"][layer:layer + 1]], n_rowsum=1,
                          slab=EPI_SLAB)
        gq = _matmul(nq, dqx, "tn", [BF16], name=f"xa{layer}_q_dw", tk=2048)
        dkv = jnp.concatenate([dkx, dvx], axis=-1)
        gkv = _matmul(mn, dkv, "tn", [BF16], name=f"xa{layer}_kv_dw", out_dm=True)
        dmn = _matmul(dkv, w["xa_w_kv"][layer], "nt", [F32], name=f"xa{layer}_kv_dx", tk=2048)
        _, gmem = _rms_bwd(mem, dmn, dmn, w["xa_mem_norm"][layer:layer + 1], f"xa{layer}_mem_rms_bwd")
        return dhn, gq, gkv, go, gn, gmem

    dh, gup1, gdown1, gmn1 = mlp_bwd(dh, 1, mlp1)
    dh, gq1, gkv1, go1, gxn1, gmem1 = xattn_bwd(dh, 1, xa1)
    g.update(mlp_w_up=[None, gup1], mlp_w_down=[None, gdown1], xa_w_q=[None, gq1], xa_w_kv=[None, gkv1],
             xa_w_o=[None, go1])
    tok = emit(3, g)

    ds_cv = _matmul(dh, w["cv_w_pw2"][0], "nt", [BF16], name="cv_pw2_dx", after=tok)
    g["cv_w_pw2"] = [_matmul(s_cv, dh, "tn", [BF16], name="cv_pw2_dw", tk=2048)]
    dc_cv, g_lng, g_lnb, g_bdw, g_b2 = _cv_mid_bwd1(c_cv, ds_cv, dh, w["cv_ln_g"], w["cv_ln_b"])
    du_cv, g_wdw, g_b1 = _cv_mid_bwd2(dc_cv, u_cv, wdw)
    g["cv_w_pw1"] = [_matmul(n1, du_cv, "tn", [BF16], name="cv_pw1_dw", out_dm=True, tk=2048)]
    dh, g_cvn = _matmul(du_cv, w["cv_w_pw1"][0], "nt", [F32], name="cv_pw1_dx", epi=rms_bwd_epi,
                        mn_extras=[h3, dh], row_extras=[w["cv_norm"]], n_rowsum=1, slab=EPI_SLAB)
    g.update(cv_ln_g=g_lng, cv_ln_b=g_lnb, cv_b_dw=g_bdw, cv_b_pw2=g_b2, cv_b_pw1=g_b1, cv_norm=g_cvn,
             cv_w_dw=g_wdw[:CV_WIDTH][None])

    tok = emit(2, g)
    dh, gup0, gdown0, gmn0 = mlp_bwd(dh, 0, mlp0, after=tok)
    dh, gq0, gkv0, go0, gxn0, gmem0 = xattn_bwd(dh, 0, xa0)
    g["mlp_w_up"][0] = gup0
    g["mlp_w_down"][0] = gdown0
    g["mlp_norm"] = jnp.concatenate([gmn0, gmn1], axis=0)
    g["xa_w_q"][0] = gq0
    g["xa_w_kv"][0] = gkv0
    g["xa_w_o"][0] = go0
    g["xa_norm"] = jnp.concatenate([gxn0, gxn1], axis=0)
    g["xa_mem_norm"] = jnp.concatenate([gmem0, gmem1], axis=0)
    tok = emit(1, g)

    dog = _matmul(dh, w["dn_w_out"][0], "nt", [BF16], name="dn_out_proj_dx", after=tok)
    g["dn_w_out"] = [_matmul(og, dh, "tn", [BF16], name="dn_out_proj_dw", tk=2048)]
    dq, dk, dv, dgb, dz, g_on = _delta_bwd(q, k, v, gb, qkvz, w["dn_out_norm"], states, tms, dog, heads)
    dqkvz, dba, g_wconv, g_alog, g_dt = _dn_pre_bwd(qkvz, ba, wconv, alog, dtb, dq, dk, dv, dgb, dz, heads)
    g_qkvzt = _matmul(dqkvz, n0, "tn", [BF16], name="dn_in_proj_dw", tk=2048)
    g_bat = _matmul(dba, n0, "tn", [BF16], name="dn_in_proj_ba_dw", tk=2048)
    g["dn_w_in"] = [jnp.concatenate([g_qkvzt, g_bat[:2 * heads]], axis=0)]
    g["dn_w_conv"] = g_wconv[None]
    tok = emit(0, g)
    dn0a = _matmul(dba, w_bat, "nn", [F32], name="dn_in_proj_ba_dx", after=tok)
    grad_x, g_dnn = _matmul(dqkvz, w_int, "nn", [F32], name="dn_in_proj_dx", b_rows=4 * d,
                            epi=lambda acc, part, hx, dres, gain: rms_bwd_epi(acc + part, hx, dres, gain),
                            mn_extras=[dn0a, x, dh], row_extras=[dn_norm], n_rowsum=1, slab=EPI_SLAB)
    g.update(dn_norm=g_dnn, dn_out_norm=g_on,
             dn_a_log=g_alog[:, heads:2 * heads], dn_dt_bias=g_dt[:, heads:2 * heads])
    return loss, grad_x, g


def _round_up(n, m):
    return (n + m - 1) // m * m


def _pack_rows(parts, cols, row_mult):
    lead = parts[0].shape[:-1]
    flat, offs, off = [], [], 0
    for p in parts:
        n = _round_up(p.shape[-1], cols)
        flat.append(jnp.pad(p, [(0, 0)] * len(lead) + [(0, n - p.shape[-1])]))
        offs.append(off)
        off += n
    total = _round_up(off, cols * row_mult)
    if total > off:
        flat.append(jnp.zeros(lead + (total - off,), parts[0].dtype))
    return jnp.concatenate(flat, axis=-1).reshape(lead + (total // cols, cols)), offs


def _unpack(packed, offs, shapes):
    lead = packed.shape[:-2]
    flat = packed.reshape(lead + (-1,))
    out = []
    for off, shp in zip(offs, shapes):
        n = 1
        for v in shp:
            n *= v
        out.append(flat[..., off:off + n].reshape(lead + tuple(shp)))
    return out


def kernel(x, mem, dn_norm, dn_w_in, dn_w_conv, dn_a_log, dn_dt_bias, dn_out_norm, dn_w_out, cv_norm, cv_w_pw1, cv_b_pw1, cv_w_dw, cv_b_dw, cv_ln_g, cv_ln_b, cv_w_pw2, cv_b_pw2, xa_norm, xa_mem_norm, xa_w_q, xa_w_kv, xa_w_o, mlp_norm, mlp_w_up, mlp_w_down, final_norm, loss_target, m_dn_norm, m_dn_w_in, m_dn_w_conv, m_dn_a_log, m_dn_dt_bias, m_dn_out_norm, m_dn_w_out, m_cv_norm, m_cv_w_pw1, m_cv_b_pw1, m_cv_w_dw, m_cv_b_dw, m_cv_ln_g, m_cv_ln_b, m_cv_w_pw2, m_cv_b_pw2, m_xa_norm, m_xa_mem_norm, m_xa_w_q, m_xa_w_kv, m_xa_w_o, m_mlp_norm, m_mlp_w_up, m_mlp_w_down, m_final_norm, v_dn_norm, v_dn_w_in, v_dn_w_conv, v_dn_a_log, v_dn_dt_bias, v_dn_out_norm, v_dn_w_out, v_cv_norm, v_cv_w_pw1, v_cv_b_pw1, v_cv_w_dw, v_cv_b_dw, v_cv_ln_g, v_cv_ln_b, v_cv_w_pw2, v_cv_b_pw2, v_xa_norm, v_xa_mem_norm, v_xa_w_q, v_xa_w_kv, v_xa_w_o, v_mlp_norm, v_mlp_w_up, v_mlp_w_down, v_final_norm):
    wsh = dict(dn_norm=dn_norm, dn_w_in=dn_w_in, dn_w_conv=dn_w_conv, dn_a_log=dn_a_log, dn_dt_bias=dn_dt_bias, dn_out_norm=dn_out_norm, dn_w_out=dn_w_out, cv_norm=cv_norm, cv_w_pw1=cv_w_pw1, cv_b_pw1=cv_b_pw1, cv_w_dw=cv_w_dw, cv_b_dw=cv_b_dw, cv_ln_g=cv_ln_g, cv_ln_b=cv_ln_b, cv_w_pw2=cv_w_pw2, cv_b_pw2=cv_b_pw2, xa_norm=xa_norm, xa_mem_norm=xa_mem_norm, xa_w_q=xa_w_q, xa_w_kv=xa_w_kv, xa_w_o=xa_w_o, mlp_norm=mlp_norm, mlp_w_up=mlp_w_up, mlp_w_down=mlp_w_down, final_norm=final_norm)
    msh = dict(dn_norm=m_dn_norm, dn_w_in=m_dn_w_in, dn_w_conv=m_dn_w_conv, dn_a_log=m_dn_a_log, dn_dt_bias=m_dn_dt_bias, dn_out_norm=m_dn_out_norm, dn_w_out=m_dn_w_out, cv_norm=m_cv_norm, cv_w_pw1=m_cv_w_pw1, cv_b_pw1=m_cv_b_pw1, cv_w_dw=m_cv_w_dw, cv_b_dw=m_cv_b_dw, cv_ln_g=m_cv_ln_g, cv_ln_b=m_cv_ln_b, cv_w_pw2=m_cv_w_pw2, cv_b_pw2=m_cv_b_pw2, xa_norm=m_xa_norm, xa_mem_norm=m_xa_mem_norm, xa_w_q=m_xa_w_q, xa_w_kv=m_xa_w_kv, xa_w_o=m_xa_w_o, mlp_norm=m_mlp_norm, mlp_w_up=m_mlp_w_up, mlp_w_down=m_mlp_w_down, final_norm=m_final_norm)
    vsh = dict(dn_norm=v_dn_norm, dn_w_in=v_dn_w_in, dn_w_conv=v_dn_w_conv, dn_a_log=v_dn_a_log, dn_dt_bias=v_dn_dt_bias, dn_out_norm=v_dn_out_norm, dn_w_out=v_dn_w_out, cv_norm=v_cv_norm, cv_w_pw1=v_cv_w_pw1, cv_b_pw1=v_cv_b_pw1, cv_w_dw=v_cv_w_dw, cv_b_dw=v_cv_b_dw, cv_ln_g=v_cv_ln_g, cv_ln_b=v_cv_ln_b, cv_w_pw2=v_cv_w_pw2, cv_b_pw2=v_cv_b_pw2, xa_norm=v_xa_norm, xa_mem_norm=v_xa_mem_norm, xa_w_q=v_xa_w_q, xa_w_kv=v_xa_w_kv, xa_w_o=v_xa_w_o, mlp_norm=v_mlp_norm, mlp_w_up=v_mlp_w_up, mlp_w_down=v_mlp_w_down, final_norm=v_final_norm)

    big_axis = dict(BIG)
    for src in (wsh, msh, vsh):
        src["dn_w_in"] = jnp.swapaxes(src["dn_w_in"], 1, 2)
    big_axis["dn_w_in"] = 1

    small_pack, small_offs = _pack_rows([wsh[nm].reshape(-1) for nm in SMALL_SH], LANES, 8)
    w = {nm: [None] * wsh[nm].shape[0] for nm in big_axis}

    def put_weights(group, gathered):
        for (nm, layer), gth in zip(group, gathered):
            if big_axis[nm] == 1:
                w[nm][layer] = gth.reshape(N_DEV * gth.shape[1], gth.shape[2])
            else:
                w[nm][layer] = gth

    first = _all_gather([wsh[nm][layer].astype(BF16) for nm, layer in GATHER_GROUPS[0]] + [small_pack],
                        "weights_all_gather_0")
    put_weights(GATHER_GROUPS[0], first)
    me = _dev_index(*_mesh_me())
    gather_handles, tokens = {}, []
    for gi in range(1, len(GATHER_GROUPS)):
        shards = [wsh[nm][layer].astype(BF16) for nm, layer in GATHER_GROUPS[gi]]
        lands = [lax.dynamic_update_slice(lax.empty((N_DEV,) + s.shape, s.dtype), s[None], (me, 0, 0))
                 for s in shards]
        gather_handles[gi], tok = _exchange_start(shards, lands, False, f"weights_gather_{gi}",
                                                  after=[first[-1]] + tokens)
        tokens.append(tok)
    for nm, gth in zip(SMALL_SH, _unpack(first[-1], small_offs, [wsh[nm].shape for nm in SMALL_SH])):
        w[nm] = jnp.moveaxis(gth, 0, -2).reshape(gth.shape[1:-1] + (N_DEV * gth.shape[-1],))
    for nm in REPL:
        w[nm] = wsh[nm]

    def fetch(gi, after):
        put_weights(GATHER_GROUPS[gi], _exchange_wait(gather_handles[gi], after)[1])

    scatter_handles = {}

    def emit(gi, g):
        blocks = []
        for nm, layer in SCATTER_GROUPS[gi]:
            gw = g[nm][layer]
            if big_axis[nm] == 1:
                gw = gw.reshape(N_DEV, gw.shape[0] // N_DEV, gw.shape[1])
            blocks.append(gw)
        if gi == 0:
            gsmall_pack, _ = _pack_rows(
                [jnp.moveaxis(g[nm].reshape(g[nm].shape[:-1] + (N_DEV, -1)), -2, 0).reshape(N_DEV, -1)
                 for nm in SMALL_SH], LANES, 8)
            blocks.append(gsmall_pack)
        lands = [lax.empty(b.shape, b.dtype) for b in blocks]
        scatter_handles[gi], tok = _exchange_start(blocks, lands, True, f"grads_scatter_{gi}")
        return [tok]

    loss_part, grad_x, g = _local_step(x[0], mem[0], loss_target[0], w, fetch, emit, tokens)

    recv = {nm: [None] * wsh[nm].shape[0] for nm in big_axis}
    sent = {nm: [None] * wsh[nm].shape[0] for nm in big_axis}
    gsh, delta, new_m, new_v = {}, {}, {}, {}
    after = [grad_x]
    done = set()
    me_arr = me.astype(jnp.int32).reshape(1)
    def small_adamw(names, name):
        packs = []
        for src in (wsh, gsh, msh, vsh):
            pk, offs = _pack_rows([src[nm].reshape(-1) for nm in names], LANES, 8)
            packs.append(pk)
        outs = _adamw(*packs, name)
        for dst, pk in zip((delta, new_m, new_v), outs):
            for nm, val in zip(names, _unpack(pk, offs, [wsh[nm].shape for nm in names])):
                dst[nm] = val
        return outs[0]

    for gi in reversed(range(len(SCATTER_GROUPS))):
        if gi == 0:
            repl_pack, repl_offs = _pack_rows([g[nm].reshape(-1) for nm in REPL] + [loss_part[:, :1].reshape(-1)],
                                              LANES, 8)
            (repl_all,) = _all_gather([repl_pack], "repl_grads_all_gather")
            repl_red = _slot_sum(repl_all, "repl_grads_sum", 512)
            *repl_vals, loss_sum = _unpack(repl_red, repl_offs, [wsh[nm].shape for nm in REPL] + [(1,)])
            for nm, val in zip(REPL, repl_vals):
                gsh[nm] = val
            after = after + [small_adamw(list(REPL), "adamw_repl")]
        sources, landed = _exchange_wait(scatter_handles[gi], after)
        for (nm, layer), src, r in zip(SCATTER_GROUPS[gi], sources, landed):
            sent[nm][layer], recv[nm][layer] = src, r
        if gi == 0:
            slot = lax.broadcasted_iota(jnp.int32, landed[-1].shape, 0)
            rsmall = jnp.where(slot == me, sources[-1], landed[-1])
        for nm in big_axis:
            if nm not in done and all(r is not None for r in recv[nm]):
                gsh[nm], delta[nm], new_m[nm], new_v[nm] = _adamw_reduce(
                    me_arr, recv[nm], sent[nm], wsh[nm], msh[nm], vsh[nm], f"adamw_{nm}")
                done.add(nm)
                after = [delta[nm]]
    gsmall_red = _slot_sum(rsmall, "grads_small_sum", 512)
    for nm, val in zip(SMALL_SH, _unpack(gsmall_red, small_offs, [wsh[nm].shape for nm in SMALL_SH])):
        gsh[nm] = val
    small_adamw(list(SMALL_SH), "adamw_small")
    for dst in (gsh, delta, new_m, new_v):
        dst["dn_w_in"] = jnp.swapaxes(dst["dn_w_in"], 1, 2)
    return (loss_sum.reshape(()), grad_x[None], *[gsh[nm] for nm in WEIGHTS], *[delta[nm] for nm in WEIGHTS],
            *[new_m[nm] for nm in WEIGHTS], *[new_v[nm] for nm in WEIGHTS])
```

```python
import functools

import jax
import jax.numpy as jnp
from jax import lax
from jax.experimental import pallas as pl
from jax.experimental.pallas import tpu as pltpu

F32 = jnp.float32
BF16 = jnp.bfloat16
MESH_IDS = pl.DeviceIdType.MESH

N_DEV = 8
LANES = 128
RMS_EPS = 1e-6
LN_EPS = 1e-5
DN_HEAD_DIM = 128
DN_CONV = 4
DN_CHUNK = 64
CV_WIDTH = 31
XA_HEADS = 4
EPI_SLAB = 256
DN_HALO = 8
CV_HALO = 32

ADAM_LR = 0.001
ADAM_B1 = 0.9
ADAM_B2 = 0.999
ADAM_EPS = 1e-08
ADAM_WD = 0.01
ADAM_STEP = 10

BIG = (("dn_w_in", 2), ("dn_w_out", 1), ("cv_w_pw1", 2), ("cv_w_pw2", 1), ("xa_w_q", 1), ("xa_w_kv", 2),
       ("xa_w_o", 1), ("mlp_w_up", 2), ("mlp_w_down", 1))
_LAYER_GROUP = ("xa_w_q", "xa_w_o", "mlp_w_down", "xa_w_kv", "mlp_w_up")
GATHER_GROUPS = (
    (("dn_w_in", 0),),
    (("dn_w_out", 0),) + tuple((nm, 0) for nm in _LAYER_GROUP),
    (("cv_w_pw2", 0), ("cv_w_pw1", 0)),
    tuple((nm, 1) for nm in _LAYER_GROUP),
)
SCATTER_GROUPS = (
    (("dn_w_out", 0), ("dn_w_in", 0)),
    tuple((nm, 0) for nm in _LAYER_GROUP),
    (("cv_w_pw2", 0), ("cv_w_pw1", 0)),
    tuple((nm, 1) for nm in _LAYER_GROUP),
)
SMALL_SH = ("cv_norm", "cv_b_pw1", "cv_b_dw", "cv_ln_g", "cv_ln_b", "cv_b_pw2", "cv_w_dw", "dn_w_conv")
REPL = ("dn_norm", "dn_a_log", "dn_dt_bias", "dn_out_norm", "xa_norm", "xa_mem_norm", "mlp_norm", "final_norm")
WEIGHTS = ("dn_norm", "dn_w_in", "dn_w_conv", "dn_a_log", "dn_dt_bias", "dn_out_norm", "dn_w_out", "cv_norm",
           "cv_w_pw1", "cv_b_pw1", "cv_w_dw", "cv_b_dw", "cv_ln_g", "cv_ln_b", "cv_w_pw2", "cv_b_pw2", "xa_norm",
           "xa_mem_norm", "xa_w_q", "xa_w_kv", "xa_w_o", "mlp_norm", "mlp_w_up", "mlp_w_down", "final_norm")


def _dot_dims(mode, batched):
    o = 1 if batched else 0
    contract = {"nn": ((1 + o,), (o,)), "nt": ((1 + o,), (1 + o,)), "tn": ((o,), (o,))}[mode]
    return (contract, (((0,), (0,)) if batched else ((), ())))


def _bdot(a, b, mode):
    return lax.dot_general(a.astype(BF16), b.astype(BF16), _dot_dims(mode, a.ndim == 3),
                           preferred_element_type=F32)


@functools.partial(jax.custom_vjp, nondiff_argnums=(2,))
def _mm(a, b, mode):
    return _bdot(a, b, mode)


def _mm_fwd(a, b, mode):
    return _bdot(a, b, mode), (a, b)


def _mm_bwd(mode, res, ct):
    a, b = res
    if mode == "nn":
        da, db = _bdot(ct, b, "nt"), _bdot(a, ct, "tn")
    elif mode == "nt":
        da, db = _bdot(ct, b, "nn"), _bdot(ct, a, "tn")
    else:
        da, db = _bdot(b, ct, "nt"), _bdot(a, ct, "nn")
    return da.astype(a.dtype), db.astype(b.dtype)


_mm.defvjp(_mm_fwd, _mm_bwd)


def _sigmoid(x):
    return 0.5 * (jnp.tanh(0.5 * x) + 1.0)


def _silu(x):
    return x * _sigmoid(x)


def _softplus(x):
    return jnp.maximum(x, 0.0) + jnp.log(1.0 + jnp.exp(-jnp.abs(x)))


def _rms(x, g):
    r = lax.rsqrt(jnp.mean(x * x, axis=-1, keepdims=True) + RMS_EPS)
    return x * r * g


def _shift_rows(x, off):
    if off == 0:
        return x
    return pltpu.roll(x, x.shape[0] - off, 0)


def _series_dot(a, b, mode):
    return _bdot(a, b, mode)


def _chunk_masks(c):
    ii = lax.broadcasted_iota(jnp.int32, (c, c), 0)
    jj = lax.broadcasted_iota(jnp.int32, (c, c), 1)
    return (ii == jj).astype(F32), ii >= jj, ii > jj


def _neumann_inverse(lm):
    n = lm.shape[-1]
    t = -lm
    p = lm
    size = 2
    while size < n:
        size *= 2
        p = _series_dot(p, p, "nn")
        t = t + p + _series_dot(t, p, "nn")
    return t


def _apply_inverse(tm, rhs, mode):
    return rhs + _series_dot(tm, rhs, mode)


@jax.custom_vjp
def _unit_lower_solve(lm, rhs, tm):
    return _apply_inverse(tm, rhs, "nn")


def _uls_fwd(lm, rhs, tm):
    sol = _apply_inverse(tm, rhs, "nn")
    return sol, (tm, sol)


def _uls_bwd(res, ct):
    tm, sol = res
    d_rhs = _apply_inverse(tm, ct, "tn")
    return -_bdot(d_rhs, sol, "nt"), d_rhs, jnp.zeros_like(tm)


_unit_lower_solve.defvjp(_uls_fwd, _uls_bwd)


def _delta_chunk(q, k, v, gcol, bcol, s0, tm=None):
    c = q.shape[1]
    eye, causal, strict = _chunk_masks(c)
    grow = jnp.sum(eye * gcol, axis=1, keepdims=True)
    gc = jnp.sum(jnp.where(causal, grow, 0.0), axis=2, keepdims=True)
    gc_row = jnp.sum(eye * gc, axis=1, keepdims=True)
    decay = jnp.exp(jnp.where(causal, gc - gc_row, -jnp.inf))
    kb = k * bcol
    lm = jnp.where(strict, _mm(kb, k, "nt") * decay, 0.0)
    if tm is None:
        tm = _neumann_inverse(lax.stop_gradient(lm))
    egc = jnp.exp(gc)
    rhs = jnp.concatenate([v * bcol, kb * egc], axis=-1)
    sol = _unit_lower_solve(lm, rhs, tm)
    dv_ = v.shape[-1]
    u, w = sol[..., :dv_], sol[..., dv_:]
    attn = _mm(q, k, "nt") * decay
    qd = q * egc
    gl = jnp.sum(grow, axis=2, keepdims=True)
    kd = k * jnp.exp(gl - gc)
    v_new = u - _mm(w, s0, "nn")
    o = _mm(qd, s0, "nn") + _mm(attn, v_new, "nn")
    s1 = s0 * jnp.exp(gl) + _mm(kd, v_new, "tn")
    return o, s1, tm


def _dn_point(cv, ba, alog, dt, heads):
    a = _silu(cv)
    d = cv.shape[1] // 3
    qs, ks = [], []
    for h in range(heads):
        qh = a[:, h * DN_HEAD_DIM:(h + 1) * DN_HEAD_DIM]
        qs.append(qh * lax.rsqrt(jnp.sum(qh * qh, axis=-1, keepdims=True) + 1e-6) * (DN_HEAD_DIM ** -0.5))
        kh = a[:, d + h * DN_HEAD_DIM:d + (h + 1) * DN_HEAD_DIM]
        ks.append(kh * lax.rsqrt(jnp.sum(kh * kh, axis=-1, keepdims=True) + 1e-6))
    q = jnp.concatenate(qs, axis=-1)
    k = jnp.concatenate(ks, axis=-1)
    v = a[:, 2 * d:]
    lane = lax.broadcasted_iota(jnp.int32, ba.shape, 1)
    beta = _sigmoid(ba)
    g = -jnp.exp(alog) * _softplus(ba + dt)
    gb = jnp.where(lane < heads, beta, jnp.where(lane < 2 * heads, g, 0.0))
    return q, k, v, gb


def _attn_tile(q, k, v):
    hd = q.shape[1] // XA_HEADS
    outs = []
    for h in range(XA_HEADS):
        sl = slice(h * hd, (h + 1) * hd)
        s = _mm(q[:, sl], k[:, sl], "nt") * (hd ** -0.5)
        m = lax.stop_gradient(jnp.max(s, axis=-1, keepdims=True))
        e = jnp.exp(s - m)
        p = e / jnp.sum(e, axis=-1, keepdims=True)
        outs.append(_mm(p, v[:, sl], "nn"))
    return jnp.concatenate(outs, axis=-1)


def _ln_silu(c, g, b):
    mu = jnp.mean(c, axis=-1, keepdims=True)
    xc = c - mu
    y = xc * lax.rsqrt(jnp.mean(xc * xc, axis=-1, keepdims=True) + LN_EPS)
    return _silu(y * g + b)


def _causal_conv(xext, w, width, lead, ts):
    acc = None
    for j in range(width):
        term = _shift_rows(xext, lead + j)[:ts] * w[j:j + 1, :]
        acc = term if acc is None else acc + term
    return acc


def _colsum(x):
    return jnp.sum(x, axis=0, keepdims=True)


def _stack_rows(rows, n_rows):
    c = rows[0].shape[1]
    ridx = lax.broadcasted_iota(jnp.int32, (n_rows, c), 0)
    out = jnp.zeros((n_rows, c), F32)
    for j, r in enumerate(rows):
        out = out + jnp.where(ridx == j, r, 0.0)
    return out


def _matmul(a, b, mode, out_dtypes, *, name, epi=None, mn_extras=(), row_extras=(), out_dm=False, after=(),
            n_rowsum=0, slab=0, b_rows=None, a_pre=None, tm=1024, tn=1024, tk=1024):
    b_dm = b.ndim == 3
    b_shape = (b.shape[1], N_DEV * b.shape[2]) if b_dm else b.shape
    if b_rows is not None:
        assert not b_dm and b_rows <= b.shape[0]
        b_shape = (b_rows, b.shape[1])
    if mode == "nn":
        (m, k), (k2, n) = a.shape, b_shape
    elif mode == "nt":
        (m, k), (n, k2) = a.shape, b_shape
    else:
        (k, m), (k2, n) = a.shape, b_shape
    assert k == k2, (a.shape, b.shape, mode)
    tm, tn, tk = min(tm, m), min(tn, n), min(tk, k)
    cb, nb = 0, 1
    if b_dm:
        assert mode in ("nn", "nt")
        cb = b.shape[2]
        nb = max(1, (tn if mode == "nn" else tk) // cb)
        if mode == "nn":
            tn = nb * cb
        else:
            tk = nb * cb
    co, no = 0, 1
    if out_dm:
        co = n // N_DEV
        no = max(1, tn // co)
        tn = no * co
    assert m % tm == 0 and n % tn == 0 and k % tk == 0, (m, n, k, tm, tn, tk)
    nk = k // tk
    if mode == "tn":
        a_spec = pl.BlockSpec((tk, tm), lambda j, i, kk: (kk, i))
    else:
        a_spec = pl.BlockSpec((tm, tk), lambda j, i, kk: (i, kk))
    if b_dm:
        b_spec = (pl.BlockSpec((nb, tn, cb), lambda j, i, kk: (kk, j, 0)) if mode == "nt"
                  else pl.BlockSpec((nb, tk, cb), lambda j, i, kk: (j, kk, 0)))
    else:
        b_spec = (pl.BlockSpec((tn, tk), lambda j, i, kk: (j, kk)) if mode == "nt"
                  else pl.BlockSpec((tk, tn), lambda j, i, kk: (kk, j)))
    mn_spec = pl.BlockSpec((tm, tn), lambda j, i, kk: (i, j))
    row_spec = pl.BlockSpec((1, tn), lambda j, i, kk: (0, j))
    n_extra = len(mn_extras) + len(row_extras)
    n_out = len(out_dtypes)
    in_specs = ([a_spec, b_spec] + [mn_spec] * len(mn_extras) + [row_spec] * len(row_extras)
                + [_ANY_SPEC] * len(after))
    args = [a, b, *mn_extras, *row_extras, *after]
    if out_dm:
        out_specs = [pl.BlockSpec((no, tm, co), lambda j, i, kk: (j, i, 0))] * n_out
        out_shape = [jax.ShapeDtypeStruct((N_DEV, m, co), dt) for dt in out_dtypes]
    else:
        out_specs = [mn_spec] * n_out
        out_shape = [jax.ShapeDtypeStruct((m, n), dt) for dt in out_dtypes]
    out_specs = out_specs + [row_spec] * n_rowsum
    out_shape = out_shape + [jax.ShapeDtypeStruct((1, n), F32)] * n_rowsum
    n_in = len(args)
    n_mn = len(mn_extras)
    step = min(slab, tm) if slab else tm
    assert tm % step == 0

    def dot(a_ref, b_ref):
        a_val = a_ref[...] if a_pre is None else a_pre(a_ref[...])
        if not b_dm:
            return _bdot(a_val, b_ref[...], mode)
        if mode == "nn":
            parts = [_bdot(a_val, b_ref[dd], "nn") for dd in range(nb)]
            return parts[0] if nb == 1 else jnp.concatenate(parts, axis=1)
        out = None
        for dd in range(nb):
            part = _bdot(a_val[:, dd * cb:(dd + 1) * cb], b_ref[dd], "nt")
            out = part if out is None else out + part
        return out

    def finish(acc_src, extras, outs):
        sums = [None] * n_rowsum
        for r0 in range(0, tm, step):
            rs = slice(r0, r0 + step)
            acc_val = acc_src[rs, :]
            if epi is None:
                vals = (acc_val,)
            else:
                vals = epi(acc_val, *[e[rs, :] for e in extras[:n_mn]], *[e[...] for e in extras[n_mn:]])
            for o_ref, val in zip(outs[:n_out], vals[:n_out]):
                if out_dm:
                    for dd in range(no):
                        o_ref[dd, rs, :] = val[:, dd * co:(dd + 1) * co].astype(o_ref.dtype)
                else:
                    o_ref[rs, :] = val.astype(o_ref.dtype)
            for q in range(n_rowsum):
                sums[q] = vals[n_out + q] if sums[q] is None else sums[q] + vals[n_out + q]
        for q in range(n_rowsum):
            s_ref = outs[n_out + q]

            @pl.when(pl.program_id(1) == 0)
            def _():
                s_ref[...] = sums[q]

            @pl.when(pl.program_id(1) > 0)
            def _():
                s_ref[...] += sums[q]

    def body_one_step(*refs):
        finish(dot(refs[0], refs[1]), refs[2:2 + n_extra], refs[n_in:])

    def body(*refs):
        a_ref, b_ref = refs[0], refs[1]
        acc = refs[-1]
        kk = pl.program_id(2)

        @pl.when(kk == 0)
        def _():
            acc[...] = jnp.zeros_like(acc)

        acc[...] += dot(a_ref, b_ref)

        @pl.when(kk == nk - 1)
        def _():
            finish(acc, refs[2:2 + n_extra], refs[n_in:-1])

    res = pl.pallas_call(
        body_one_step if nk == 1 else body, name=name,
        grid=(n // tn, m // tm, nk),
        in_specs=in_specs, out_specs=out_specs, out_shape=out_shape,
        scratch_shapes=[] if nk == 1 else [pltpu.VMEM((tm, tn), F32)],
        compiler_params=pltpu.CompilerParams(
            dimension_semantics=("parallel", "arbitrary" if n_rowsum else "parallel", "arbitrary")),
    )(*args)
    return res[0] if n_out + n_rowsum == 1 else res


def _rowwise(fn, *, n_rows, ts, name, rows=(), prevs=(), nexts=(), vecs=(), row_outs=(), acc_outs=(), after=()):
    ts = min(ts, n_rows)
    assert n_rows % ts == 0
    nblk = n_rows // ts
    in_specs, args = [], []
    for arr, cb, w in rows:
        in_specs.append(pl.BlockSpec((ts, w), functools.partial(lambda i, cb: (i, cb), cb=cb)))
        args.append(arr)
    for arr, cb, w, halo in prevs:
        per = ts // halo
        in_specs.append(pl.BlockSpec(
            (halo, w), functools.partial(lambda i, cb, per: (jnp.maximum(i * per - 1, 0), cb), cb=cb, per=per)))
        args.append(arr)
    for arr, cb, w, halo in nexts:
        per = ts // halo
        last_blk = n_rows // halo - 1
        in_specs.append(pl.BlockSpec(
            (halo, w), functools.partial(lambda i, cb, per, lb: (jnp.minimum((i + 1) * per, lb), cb),
                                         cb=cb, per=per, lb=last_blk)))
        args.append(arr)
    for arr in vecs:
        in_specs.append(pl.BlockSpec(arr.shape, functools.partial(lambda i, nd: (0,) * nd, nd=arr.ndim)))
        args.append(arr)
    out_specs, out_shape = [], []
    for w, dt in row_outs:
        out_specs.append(pl.BlockSpec((ts, w), lambda i: (i, 0)))
        out_shape.append(jax.ShapeDtypeStruct((n_rows, w), dt))
    for shp in acc_outs:
        out_specs.append(pl.BlockSpec(shp, functools.partial(lambda i, nd: (0,) * nd, nd=len(shp))))
        out_shape.append(jax.ShapeDtypeStruct(shp, F32))
    n_used = len(args)
    n_tiles = n_used - len(vecs)
    in_specs += [_ANY_SPEC] * len(after)
    args += list(after)
    n_in, n_ro, n_acc = len(args), len(row_outs), len(acc_outs)

    def body(*refs):
        ins, ro, ac = refs[:n_used], refs[n_in:n_in + n_ro], refs[n_in + n_ro:]
        i = pl.program_id(0)
        rvals, avals = fn(i == 0, i == nblk - 1, *[r[...] for r in ins[:n_tiles]], *ins[n_tiles:])
        for r, val in zip(ro, rvals):
            r[...] = val.astype(r.dtype)
        if n_acc:
            @pl.when(i == 0)
            def _():
                for r in ac:
                    r[...] = jnp.zeros_like(r)

            for r, val in zip(ac, avals):
                r[...] += val

    res = pl.pallas_call(
        body, name=name, grid=(nblk,), in_specs=in_specs, out_specs=out_specs, out_shape=out_shape,
        compiler_params=pltpu.CompilerParams(dimension_semantics=("arbitrary",)),
    )(*args)
    return res


def _gated_out(o, z, onorm):
    return o * lax.rsqrt(jnp.mean(o * o, axis=-1, keepdims=True) + RMS_EPS) * onorm * _silu(z)


def _head_blocks(ref, heads, col0=0):
    return jnp.stack([ref[:, col0 + h * DN_HEAD_DIM:col0 + (h + 1) * DN_HEAD_DIM] for h in range(heads)])


def _split_heads(q_ref, k_ref, v_ref, gbv, heads):
    gcol = jnp.stack([gbv[:, heads + h:heads + h + 1] for h in range(heads)])
    bcol = jnp.stack([gbv[:, h:h + 1] for h in range(heads)])
    return _head_blocks(q_ref, heads), _head_blocks(k_ref, heads), _head_blocks(v_ref, heads), gcol, bcol


def _delta_fwd(q, k, v, gb, qkvz, onorm, heads):
    s, hd = q.shape
    n = s // DN_CHUNK
    blk = pl.BlockSpec((DN_CHUNK, hd), lambda c: (c, 0))
    gspec = pl.BlockSpec((DN_CHUNK, LANES), lambda c: (c, 0))

    def body(q_ref, k_ref, v_ref, gb_ref, z_ref, on_ref, og_ref, st_ref, tm_ref, state):
        @pl.when(pl.program_id(0) == 0)
        def _():
            state[...] = jnp.zeros_like(state)

        s0 = state[...]
        st_ref[0] = s0
        o, s1, tm = _delta_chunk(*_split_heads(q_ref, k_ref, v_ref, gb_ref[...], heads), s0)
        og = _gated_out(o, _head_blocks(z_ref, heads), on_ref[...])
        for h in range(heads):
            og_ref[:, h * DN_HEAD_DIM:(h + 1) * DN_HEAD_DIM] = og[h].astype(og_ref.dtype)
        state[...] = s1
        tm_ref[0] = tm

    return pl.pallas_call(
        body, name="dn_delta_fwd", grid=(n,),
        in_specs=[blk, blk, blk, gspec, pl.BlockSpec((DN_CHUNK, hd), lambda c: (c, 3)),
                  pl.BlockSpec(onorm.shape, lambda c: (0, 0))],
        out_specs=[blk, pl.BlockSpec((1, heads, DN_HEAD_DIM, DN_HEAD_DIM), lambda c: (c, 0, 0, 0)),
                   pl.BlockSpec((1, heads, DN_CHUNK, DN_CHUNK), lambda c: (c, 0, 0, 0))],
        out_shape=[jax.ShapeDtypeStruct((s, hd), BF16),
                   jax.ShapeDtypeStruct((n, heads, DN_HEAD_DIM, DN_HEAD_DIM), F32),
                   jax.ShapeDtypeStruct((n, heads, DN_CHUNK, DN_CHUNK), F32)],
        scratch_shapes=[pltpu.VMEM((heads, DN_HEAD_DIM, DN_HEAD_DIM), F32)],
        compiler_params=pltpu.CompilerParams(dimension_semantics=("arbitrary",)),
    )(q, k, v, gb, qkvz, onorm)


def _delta_bwd(q, k, v, gb, qkvz, onorm, states, tms, dog, heads):
    s, hd = q.shape
    n = s // DN_CHUNK
    blk = pl.BlockSpec((DN_CHUNK, hd), lambda c: (n - 1 - c, 0))
    gspec = pl.BlockSpec((DN_CHUNK, LANES), lambda c: (n - 1 - c, 0))
    sspec = pl.BlockSpec((1, heads, DN_HEAD_DIM, DN_HEAD_DIM), lambda c: (n - 1 - c, 0, 0, 0))
    tspec = pl.BlockSpec((1, heads, DN_CHUNK, DN_CHUNK), lambda c: (n - 1 - c, 0, 0, 0))
    nspec = pl.BlockSpec(onorm.shape, lambda c: (0, 0))

    def body(q_ref, k_ref, v_ref, gb_ref, z_ref, on_ref, st_ref, tm_ref, dog_ref,
             dq_ref, dk_ref, dv_ref, dgb_ref, dz_ref, don_ref, dstate):
        @pl.when(pl.program_id(0) == 0)
        def _():
            dstate[...] = jnp.zeros_like(dstate)
            don_ref[...] = jnp.zeros_like(don_ref)

        gbv = gb_ref[...]
        tm = tm_ref[0]

        def chunk(qh, kh, vh, gcol, bcol, s0, zh, on):
            o, s1, _ = _delta_chunk(qh, kh, vh, gcol, bcol, s0, tm)
            return _gated_out(o, zh, on), s1

        _, vjp = jax.vjp(chunk, *_split_heads(q_ref, k_ref, v_ref, gbv, heads), st_ref[0],
                         _head_blocks(z_ref, heads), on_ref[...])
        dq, dk, dv, dg, db, ds0, dz, don = vjp((_head_blocks(dog_ref, heads).astype(F32), dstate[...]))
        dstate[...] = ds0
        don_ref[...] += don
        lane = lax.broadcasted_iota(jnp.int32, gbv.shape, 1)
        dgb = jnp.zeros(gbv.shape, F32)
        for h in range(heads):
            sl = slice(h * DN_HEAD_DIM, (h + 1) * DN_HEAD_DIM)
            dq_ref[:, sl] = dq[h]
            dk_ref[:, sl] = dk[h]
            dv_ref[:, sl] = dv[h]
            dz_ref[:, sl] = dz[h]
            dgb = dgb + jnp.where(lane == h, db[h], 0.0) + jnp.where(lane == heads + h, dg[h], 0.0)
        dgb_ref[...] = dgb

    return pl.pallas_call(
        body, name="dn_delta_bwd", grid=(n,),
        in_specs=[blk, blk, blk, gspec, pl.BlockSpec((DN_CHUNK, hd), lambda c: (n - 1 - c, 3)), nspec,
                  sspec, tspec, blk],
        out_specs=[blk, blk, blk, gspec, blk, nspec],
        out_shape=[jax.ShapeDtypeStruct((s, hd), F32)] * 3 + [jax.ShapeDtypeStruct((s, LANES), F32),
                                                              jax.ShapeDtypeStruct((s, hd), F32),
                                                              jax.ShapeDtypeStruct(onorm.shape, F32)],
        scratch_shapes=[pltpu.VMEM((heads, DN_HEAD_DIM, DN_HEAD_DIM), F32)],
        compiler_params=pltpu.CompilerParams(dimension_semantics=("arbitrary",)),
    )(q, k, v, gb, qkvz, onorm, states, tms, dog)


def _dev_index(px, py, pc):
    return 4 * px + 2 * py + pc


def _all_gather(arrs, name):
    n = len(arrs)

    def body(*refs):
        xs, outs = refs[:n], refs[n:2 * n]
        send_sems, recv_sems, local_sems = refs[2 * n:]
        x, y, c = lax.axis_index("x"), lax.axis_index("y"), lax.axis_index("c")
        me, sibling = (x, y, c), (x, y, 1 - c)
        chips = [(1 - x, y), (x, 1 - y), (1 - x, 1 - y)]

        def copy(a, kk, block, to, src=None):
            dst = outs[a].at[_dev_index(*block)]
            return pltpu.make_async_remote_copy(
                src_ref=dst if src is None else src, dst_ref=dst,
                send_sem=send_sems.at[a * 7 + kk], recv_sem=recv_sems.at[a * 7 + kk],
                device_id=to, device_id_type=MESH_IDS)

        mine = [pltpu.make_async_copy(xs[a], outs[a].at[_dev_index(*me)], local_sems.at[a]) for a in range(n)]
        for cp in mine:
            cp.start()
        first = []
        for a in range(n):
            first.append(copy(a, 0, me, sibling, src=xs[a]))
            first += [copy(a, 1 + j, me, (*chip, c), src=xs[a]) for j, chip in enumerate(chips)]
        for cp in first:
            cp.start()
        passed = []
        for j, chip in enumerate(chips):
            for a in range(n):
                copy(a, 1 + j, (*chip, c), me).wait_recv()
                fwd = copy(a, 4 + j, (*chip, c), sibling)
                fwd.start()
                passed.append(fwd)
        for a in range(n):
            copy(a, 0, sibling, me).wait_recv()
        for j, chip in enumerate(chips):
            for a in range(n):
                copy(a, 4 + j, (*chip, 1 - c), me).wait_recv()
        for cp in first + passed:
            cp.wait_send()
        for cp in mine:
            cp.wait()

    hbm = pl.BlockSpec(memory_space=pltpu.HBM)
    res = pl.pallas_call(
        body, name=name,
        in_specs=[hbm] * n, out_specs=[hbm] * n,
        out_shape=[jax.ShapeDtypeStruct((N_DEV,) + a.shape, a.dtype) for a in arrs],
        scratch_shapes=[pltpu.SemaphoreType.DMA((7 * n,)), pltpu.SemaphoreType.DMA((7 * n,)),
                        pltpu.SemaphoreType.DMA((n,))],
    )(*arrs)
    return list(res)


_FLIPS = ((0, 0, 1), (1, 0, 0), (0, 1, 0), (1, 1, 0), (1, 0, 1), (0, 1, 1), (1, 1, 1))
_HBM_SPEC = pl.BlockSpec(memory_space=pltpu.HBM)
_SEM_SPEC = pl.BlockSpec(memory_space=pltpu.SEMAPHORE)
_ANY_SPEC = pl.BlockSpec(memory_space=pl.ANY)
_DATAFLOW = pltpu.SideEffectType.DATAFLOW_SIDE_EFFECTING
TOKEN_SHAPE = (8, LANES)


def _mesh_me():
    return lax.axis_index("x"), lax.axis_index("y"), lax.axis_index("c")


def _flipped(me, f):
    return tuple(1 - v if fl else v for v, fl in zip(me, f))


def _exchange_copies(xs, lands, send_sems, recv_sems, scatter, landed):
    me = _mesh_me()
    cps = []
    for kk, f in enumerate(_FLIPS):
        p = _flipped(me, f)
        for a in range(len(xs)):
            cps.append(pltpu.make_async_remote_copy(
                src_ref=xs[a].at[_dev_index(*p)] if scatter else xs[a],
                dst_ref=lands[a].at[_dev_index(*(p if landed else me))],
                send_sem=send_sems.at[a * 7 + kk], recv_sem=recv_sems.at[a * 7 + kk],
                device_id=p, device_id_type=MESH_IDS))
    return cps


def _exchange_start(srcs, lands, scatter, name, after=()):
    n = len(srcs)

    n_after = len(after)

    def body(*refs):
        xs, ls = refs[:n], refs[n:2 * n]
        send_sems, recv_sems = refs[2 * n + n_after], refs[2 * n + n_after + 1]
        token = refs[-1]
        for cp in _exchange_copies(xs, ls, send_sems, recv_sems, scatter, landed=False):
            cp.start()
        token[...] = jnp.zeros_like(token)

    operands = [pltpu.with_memory_space_constraint(a, pltpu.HBM) for a in list(srcs) + list(lands)]
    res = pl.pallas_call(
        body, name=name,
        in_specs=[_HBM_SPEC] * (2 * n) + [_ANY_SPEC] * len(after),
        out_specs=[_SEM_SPEC, _SEM_SPEC] + [_HBM_SPEC] * (2 * n) + [pl.BlockSpec(memory_space=pltpu.VMEM)],
        out_shape=[pltpu.SemaphoreType.DMA((7 * n,)), pltpu.SemaphoreType.DMA((7 * n,))]
        + [pltpu.HBM(a.shape, a.dtype) for a in operands] + [jax.ShapeDtypeStruct(TOKEN_SHAPE, F32)],
        input_output_aliases={i: 2 + i for i in range(2 * n)},
        compiler_params=pltpu.CompilerParams(has_side_effects=_DATAFLOW),
    )(*operands, *after)
    return (res[0], res[1], list(res[2:2 + n]), list(res[2 + n:2 + 2 * n]), scatter, name), res[-1]


def _exchange_wait(handle, after):
    send_sems, recv_sems, srcs, lands, scatter, name = handle
    n = len(srcs)
    n_after = len(after)

    def body(*refs):
        xs, ls = refs[:n], refs[n:2 * n]
        send_sems_ref, recv_sems_ref = refs[2 * n], refs[2 * n + 1]
        for cp in _exchange_copies(xs, ls, send_sems_ref, recv_sems_ref, scatter, landed=True):
            cp.wait_send()
            cp.wait_recv()

    res = pl.pallas_call(
        body, name=name + "_wait",
        in_specs=[_HBM_SPEC] * (2 * n) + [_SEM_SPEC, _SEM_SPEC] + [_ANY_SPEC] * n_after,
        out_specs=[_HBM_SPEC] * (2 * n),
        out_shape=[pltpu.HBM(a.shape, a.dtype) for a in srcs + lands],
        input_output_aliases={i: i for i in range(2 * n)},
        compiler_params=pltpu.CompilerParams(has_side_effects=_DATAFLOW),
    )(*srcs, *lands, send_sems, recv_sems, *after)
    return list(res[:n]), list(res[n:])


def _slot_sum(g, name, tr):
    _, r, c = g.shape
    tr = min(tr, r)
    assert r % tr == 0

    def body(g_ref, o_ref):
        acc = g_ref[0].astype(F32)
        for s in range(1, N_DEV):
            acc = acc + g_ref[s].astype(F32)
        o_ref[...] = acc

    return pl.pallas_call(
        body, name=name, grid=(r // tr,),
        in_specs=[pl.BlockSpec((N_DEV, tr, c), lambda i: (0, i, 0))],
        out_specs=pl.BlockSpec((tr, c), lambda i: (i, 0)),
        out_shape=jax.ShapeDtypeStruct((r, c), F32),
        compiler_params=pltpu.CompilerParams(dimension_semantics=("parallel",)),
    )(g)


def _adam_update(w, gg, m, v):
    c1 = 1.0 / (1.0 - ADAM_B1 ** ADAM_STEP)
    c2 = 1.0 / (1.0 - ADAM_B2 ** ADAM_STEP)
    nm = ADAM_B1 * m + (1.0 - ADAM_B1) * gg
    nv = ADAM_B2 * v + (1.0 - ADAM_B2) * (gg * gg)
    return -ADAM_LR * ((nm * c1) / (jnp.sqrt(nv * c2) + ADAM_EPS) + ADAM_WD * w), nm, nv


def _adamw_reduce(me, recvs, owns, w, m, v, name, tr=256):
    nl, r, c = w.shape
    assert len(recvs) == nl and len(owns) == nl
    tr = min(tr, r)
    if r % tr == 0:
        tc, nblk = c, r // tr
        at = lambda i: (i, 0)
    else:
        tr, tc = r, min(c, 4 * LANES)
        assert c % tc == 0
        nblk = c // tc
        at = lambda i: (0, i)

    def parked(li, l, i):
        return jnp.where(l < li, 0, jnp.where(l > li, nblk - 1, i))

    def recv_spec(li):
        return pl.BlockSpec((N_DEV, tr, tc), lambda l, i, me_ref: (0, *at(parked(li, l, i))))

    def own_spec(li):
        return pl.BlockSpec((None, tr, tc), lambda l, i, me_ref: (me_ref[0], *at(parked(li, l, i))))

    def body(me_ref, *refs):
        rrefs, orefs = refs[:nl], refs[nl:2 * nl]
        w_ref, m_ref, v_ref, g_ref, d_ref, nm_ref, nv_ref = refs[2 * nl:]
        l = pl.program_id(0)

        def of_layer(vals):
            out = vals[0]
            for li in range(1, nl):
                out = jnp.where(l == li, vals[li], out)
            return out

        own = of_layer([o[...].astype(F32) for o in orefs])
        gg = None
        for s in range(N_DEV):
            slot = jnp.where(me_ref[0] == s, own, of_layer([rr[s].astype(F32) for rr in rrefs]))
            gg = slot if gg is None else gg + slot
        g_ref[...] = gg
        d_ref[...], nm_ref[...], nv_ref[...] = _adam_update(w_ref[...], gg, m_ref[...], v_ref[...])

    spec = pl.BlockSpec((None, tr, tc), lambda l, i, me_ref: (l, *at(i)))
    return pl.pallas_call(
        body, name=name,
        grid_spec=pltpu.PrefetchScalarGridSpec(
            num_scalar_prefetch=1, grid=(nl, nblk),
            in_specs=[recv_spec(li) for li in range(nl)] + [own_spec(li) for li in range(nl)] + [spec] * 3,
            out_specs=[spec] * 4),
        out_shape=[jax.ShapeDtypeStruct((nl, r, c), F32)] * 4,
        compiler_params=pltpu.CompilerParams(dimension_semantics=("arbitrary", "arbitrary")),
    )(me, *recvs, *owns, w, m, v)


def _adamw(w, g, m, v, name, tr=256):
    r, c = w.shape
    tr = min(tr, r)
    assert r % tr == 0

    def body(w_ref, g_ref, m_ref, v_ref, d_ref, nm_ref, nv_ref):
        d_ref[...], nm_ref[...], nv_ref[...] = _adam_update(w_ref[...], g_ref[...], m_ref[...], v_ref[...])

    spec = pl.BlockSpec((tr, c), lambda i: (i, 0))
    return pl.pallas_call(
        body, name=name, grid=(r // tr,), in_specs=[spec] * 4, out_specs=[spec] * 3,
        out_shape=[jax.ShapeDtypeStruct((r, c), F32)] * 3,
        compiler_params=pltpu.CompilerParams(dimension_semantics=("parallel",)),
    )(w, g, m, v)


def _rms_fwd(x, g, name, ts=512, after=()):
    s, d = x.shape

    def fn(first, last, xv, gv):
        return [_rms(xv, gv[...])], []

    return _rowwise(fn, n_rows=s, ts=ts, name=name, rows=[(x, 0, d)], vecs=[g], row_outs=[(d, BF16)],
                    after=after)[0]


def _rms_bwd(x, dn, dres, g, name, ts=256):
    s, d = x.shape

    def fn(first, last, xv, dnv, drv, gv):
        _, vjp = jax.vjp(_rms, xv, gv[...])
        dx, dg = vjp(dnv.astype(F32))
        return [drv + dx], [dg]

    return _rowwise(fn, n_rows=s, ts=ts, name=name, rows=[(x, 0, d), (dn, 0, d), (dres, 0, d)], vecs=[g],
                    row_outs=[(d, F32)], acc_outs=[(1, d)])


def _dn_pre_fwd(qkvz, ba, wconv, alog, dt, heads, ts=128):
    s = qkvz.shape[0]
    d3 = wconv.shape[1]
    d = d3 // 3

    def fn(first, last, xc, bav, xp, wv, av, dv):
        xext = jnp.concatenate([jnp.where(first, 0.0, xp), xc], axis=0)
        cv = _causal_conv(xext, wv, DN_CONV, DN_HALO - (DN_CONV - 1), xc.shape[0])
        return list(_dn_point(cv, bav, av[...], dv[...], heads)), []

    return _rowwise(fn, n_rows=s, ts=ts, name="dn_pre_fwd", rows=[(qkvz, 0, d3), (ba, 0, LANES)],
                    prevs=[(qkvz, 0, d3, DN_HALO)], vecs=[wconv, alog, dt],
                    row_outs=[(d, F32), (d, F32), (d, F32), (LANES, F32)])


def _dn_pre_bwd(qkvz, ba, wconv, alog, dt, dq, dk, dv, dgb, dz, heads, ts=128):
    s = qkvz.shape[0]
    d3 = wconv.shape[1]
    d = d3 // 3
    lead = DN_HALO - (DN_CONV - 1)

    def fn(first, last, xc, bac, dqc, dkc, dvc, dgbc, dzc, xp, xn, ban, dqn, dkn, dvn, dgbn, wv, av, dtv):
        n = xc.shape[0]
        ext = lambda cur, nxt: jnp.concatenate([cur, nxt], axis=0)
        live = lambda nxt: jnp.where(last, 0.0, nxt)
        xall = jnp.concatenate([jnp.where(first, 0.0, xp), xc, live(xn)], axis=0)
        cv = _causal_conv(xall, wv, DN_CONV, lead, n + DN_HALO)
        (_, _, _, gbv), vjp = jax.vjp(lambda c, b: _dn_point(c, b, av[...], dtv[...], heads), cv, ext(bac, ban))
        dc, dba = vjp((ext(dqc, live(dqn)), ext(dkc, live(dkn)), ext(dvc, live(dvn)), ext(dgbc, live(dgbn))))
        dx = None
        dw = []
        for j in range(DN_CONV):
            term = _shift_rows(dc, DN_CONV - 1 - j)[:n] * wv[j:j + 1, :]
            dx = term if dx is None else dx + term
            dw.append(_colsum(dc[:n] * _shift_rows(xall, lead + j)[:n]))
        dba = dba[:n]
        return ([jnp.concatenate([dx, dzc], axis=-1), dba],
                [_stack_rows(dw, DN_CONV), _colsum(dgbc * gbv[:n]), _colsum(dba)])

    return _rowwise(fn, n_rows=s, ts=ts, name="dn_pre_bwd",
                    rows=[(qkvz, 0, d3), (ba, 0, LANES), (dq, 0, d), (dk, 0, d), (dv, 0, d), (dgb, 0, LANES),
                          (dz, 0, d)],
                    prevs=[(qkvz, 0, d3, DN_HALO)],
                    nexts=[(qkvz, 0, d3, DN_HALO), (ba, 0, LANES, DN_HALO), (dq, 0, d, DN_HALO),
                           (dk, 0, d, DN_HALO), (dv, 0, d, DN_HALO), (dgb, 0, LANES, DN_HALO)],
                    vecs=[wconv, alog, dt],
                    row_outs=[(4 * d, BF16), (LANES, BF16)], acc_outs=[(DN_CONV, d3), (1, LANES), (1, LANES)])


def _cv_mid_fwd(u, wdw, bdw, lng, lnb, ts=256):
    s = u.shape[0]
    d = u.shape[1] // 2

    def fn(first, last, uc, up, wv, bv, gv, lbv):
        uext = jnp.concatenate([jnp.where(first, 0.0, up), uc], axis=0)
        glu = uext[:, :d] * _sigmoid(uext[:, d:])
        c = _causal_conv(glu, wv, CV_WIDTH, CV_HALO - (CV_WIDTH - 1), uc.shape[0]) + bv[...]
        return [c, _ln_silu(c, gv[...], lbv[...])], []

    return _rowwise(fn, n_rows=s, ts=ts, name="cv_mid_fwd", rows=[(u, 0, 2 * d)], prevs=[(u, 0, 2 * d, CV_HALO)],
                    vecs=[wdw, bdw, lng, lnb], row_outs=[(d, F32), (d, BF16)])


def _cv_mid_bwd2(dc, u, wdw, ts=256):
    s, d = dc.shape

    def fn(first, last, dcc, uc, up, dcn, wv):
        n = dcc.shape[0]
        dcext = jnp.concatenate([dcc, jnp.where(last, 0.0, dcn)], axis=0)
        uext = jnp.concatenate([jnp.where(first, 0.0, up), uc], axis=0)
        glu = uext[:, :d] * _sigmoid(uext[:, d:])
        dglu = None
        dw = []
        for j in range(CV_WIDTH):
            term = _shift_rows(dcext, CV_WIDTH - 1 - j)[:n] * wv[j:j + 1, :]
            dglu = term if dglu is None else dglu + term
            dw.append(_colsum(dcc * _shift_rows(glu, CV_HALO - (CV_WIDTH - 1) + j)[:n]))
        u1, sg = uc[:, :d], _sigmoid(uc[:, d:])
        du = jnp.concatenate([dglu * sg, dglu * u1 * sg * (1.0 - sg)], axis=-1)
        return [du], [_stack_rows(dw, CV_HALO), _colsum(du)]

    return _rowwise(fn, n_rows=s, ts=ts, name="cv_mid_bwd2", rows=[(dc, 0, d), (u, 0, 2 * d)],
                    prevs=[(u, 0, 2 * d, CV_HALO)], nexts=[(dc, 0, d, CV_HALO)], vecs=[wdw],
                    row_outs=[(2 * d, BF16)], acc_outs=[(CV_HALO, d), (1, 2 * d)])


def _attn_fwd(q, k, v, name, ts=512):
    s, d = q.shape

    def fn(first, last, qv, kv, vv):
        return [_attn_tile(qv.astype(F32), kv[...].astype(F32), vv[...].astype(F32))], []

    return _rowwise(fn, n_rows=s, ts=ts, name=name, rows=[(q, 0, d)], vecs=[k, v], row_outs=[(d, BF16)])[0]


def _attn_bwd(q, k, v, do, name, ts=512):
    s, d = q.shape
    m = k.shape[0]

    def fn(first, last, qv, dov, kv, vv):
        _, vjp = jax.vjp(_attn_tile, qv.astype(F32), kv[...].astype(F32), vv[...].astype(F32))
        dq, dk, dv = vjp(dov.astype(F32))
        return [dq], [dk, dv]

    return _rowwise(fn, n_rows=s, ts=ts, name=name, rows=[(q, 0, d), (do, 0, d)], vecs=[k, v],
                    row_outs=[(d, BF16)], acc_outs=[(m, d), (m, d)])


def _pad_lanes(a, off=0):
    r, n = a.shape
    return jnp.pad(a, ((0, 0), (off, LANES - off - n)))


def _local_step(x, mem, tgt, w, fetch=None, emit=None, first_after=()):
    s, d = x.shape
    heads = d // DN_HEAD_DIM
    g = {}
    if fetch is None:
        fetch = lambda group, after: None
    if emit is None:
        emit = lambda group, grads: ()

    def add_res(acc, res):
        return (res + acc,)

    def add_res_rms(acc, res, gain):
        h = res + acc
        return h, _rms(h, gain)

    def rms_bwd_epi(acc, hx, dres, gain):
        _, vjp = jax.vjp(_rms, hx, gain)
        dx, dg = vjp(acc)
        return dres + dx, dg

    w_int = w["dn_w_in"][0]
    assert w_int.shape[0] == 4 * d + 2 * heads
    w_bat = jnp.pad(w_int[4 * d:], ((0, LANES - 2 * heads), (0, 0)))
    dn_norm = w["dn_norm"]
    alog = _pad_lanes(w["dn_a_log"], heads)
    dtb = _pad_lanes(w["dn_dt_bias"], heads)
    wconv = w["dn_w_conv"][0]
    n0 = _rms_fwd(x, dn_norm, "dn_rms", after=first_after)
    qkvz = _matmul(n0, w_int, "nt", [F32], name="dn_in_proj", b_rows=4 * d)
    ba = _matmul(n0, w_bat, "nt", [F32], name="dn_in_proj_ba")
    q, k, v, gb = _dn_pre_fwd(qkvz, ba, wconv, alog, dtb, heads)
    og, states, tms = _delta_fwd(q, k, v, gb, qkvz, w["dn_out_norm"], heads)
    fetch(1, [og])
    h1, nq0 = _matmul(og, w["dn_w_out"][0], "nn", [F32, BF16], name="dn_out_proj", epi=add_res_rms,
                      mn_extras=[x], row_extras=[w["xa_norm"][0:1]], slab=EPI_SLAB)

    def xattn_fwd(h, nq, layer, next_gain):
        qx = _matmul(nq, w["xa_w_q"][layer], "nn", [BF16], name=f"xa{layer}_q")
        mn = _rms_fwd(mem, w["xa_mem_norm"][layer:layer + 1], f"xa{layer}_mem_rms")
        kv = _matmul(mn, w["xa_w_kv"][layer], "nn", [BF16], name=f"xa{layer}_kv")
        kx, vx = kv[:, :d], kv[:, d:]
        ox = _attn_fwd(qx, kx, vx, f"xa{layer}_attn")
        hn, nn = _matmul(ox, w["xa_w_o"][layer], "nn", [F32, BF16], name=f"xa{layer}_o", epi=add_res_rms,
                         mn_extras=[h], row_extras=[next_gain], slab=EPI_SLAB)
        return hn, nn, (h, nq, qx, mn, kx, vx, ox)

    def sq_relu(t):
        r = jnp.maximum(t.astype(F32), 0.0)
        return r * r

    mlp_store_act = (True, False)

    def loss_epi(acc, res, target, gain):
        def cols(hh, gg):
            e = _rms(hh, gg) - target
            return _colsum(e * e) * (0.5 / d)

        per_col, vjp = jax.vjp(cols, res + acc, gain)
        dhx, dgain = vjp(jnp.ones_like(per_col))
        return dhx, dgain, per_col

    def mlp_fwd(h, nm, layer, next_gain):
        if mlp_store_act[layer]:
            u, a = _matmul(nm, w["mlp_w_up"][layer], "nn", [BF16, BF16], name=f"mlp{layer}_up",
                           epi=lambda acc: (acc, sq_relu(acc)))
            pre = None
        else:
            u = _matmul(nm, w["mlp_w_up"][layer], "nn", [BF16], name=f"mlp{layer}_up")
            a, pre = u, sq_relu
        if next_gain is None:
            hn, *nn = _matmul(a, w["mlp_w_down"][layer], "nn", [F32], name=f"mlp{layer}_down_loss",
                              epi=loss_epi, mn_extras=[h, tgt], row_extras=[w["final_norm"].reshape(1, d)],
                              n_rowsum=2, tk=2048, slab=EPI_SLAB, a_pre=pre)
        else:
            hn, nn = _matmul(a, w["mlp_w_down"][layer], "nn", [F32, BF16], name=f"mlp{layer}_down",
                             epi=add_res_rms, mn_extras=[h], row_extras=[next_gain], tk=2048, slab=EPI_SLAB,
                             a_pre=pre)
        return hn, nn, (h, nm, u, a, pre)

    h2, nm0, xa0 = xattn_fwd(h1, nq0, 0, w["mlp_norm"][0:1])
    fetch(2, [h2])
    h3, n1, mlp0 = mlp_fwd(h2, nm0, 0, w["cv_norm"])

    u_cv = _matmul(n1, w["cv_w_pw1"][0], "nn", [F32], name="cv_pw1", epi=lambda acc, b: (acc + b,),
                   row_extras=[w["cv_b_pw1"]])
    wdw = jnp.pad(w["cv_w_dw"][0], ((0, CV_HALO - CV_WIDTH), (0, 0)))
    c_cv, s_cv = _cv_mid_fwd(u_cv, wdw, w["cv_b_dw"], w["cv_ln_g"], w["cv_ln_b"])
    h4, nq1 = _matmul(s_cv, w["cv_w_pw2"][0], "nn", [F32, BF16], name="cv_pw2",
                      epi=lambda acc, res, b, gain: add_res_rms(acc + b, res, gain), mn_extras=[h3],
                      row_extras=[w["cv_b_pw2"], w["xa_norm"][1:2]], slab=EPI_SLAB)
    fetch(3, [h4])
    h5, nm1, xa1 = xattn_fwd(h4, nq1, 1, w["mlp_norm"][1:2])
    dh, (g_fn, loss_cols), mlp1 = mlp_fwd(h5, nm1, 1, None)
    g["final_norm"] = g_fn.reshape(d)
    loss = jnp.sum(loss_cols, axis=1, keepdims=True)

    def mlp_bwd(dh, layer, saved, after=()):
        h, nm, u, a, pre = saved
        du = _matmul(dh, w["mlp_w_down"][layer], "nt", [BF16], name=f"mlp{layer}_down_dx", after=after,
                     epi=lambda acc, uu: (acc * 2.0 * jnp.maximum(uu.astype(F32), 0.0),), mn_extras=[u])
        gdown = _matmul(a, dh, "tn", [BF16], name=f"mlp{layer}_down_dw", tm=2048, a_pre=pre)
        dhn, gn = _matmul(du, w["mlp_w_up"][layer], "nt", [F32], name=f"mlp{layer}_up_dx", epi=rms_bwd_epi,
                          mn_extras=[h, dh], row_extras=[w["mlp_norm"][layer:layer + 1]], n_rowsum=1,
                          slab=EPI_SLAB)
        gup = _matmul(nm, du, "tn", [BF16], name=f"mlp{layer}_up_dw", out_dm=True, tk=2048)
        return dhn, gup, gdown, gn

    def xattn_bwd(dh, layer, saved):
        h, nq, qx, mn, kx, vx, ox = saved
        dox = _matmul(dh, w["xa_w_o"][layer], "nt", [BF16], name=f"xa{layer}_o_dx")
        go = _matmul(ox, dh, "tn", [BF16], name=f"xa{layer}_o_dw", tk=2048)
        dqx, dkx, dvx = _attn_bwd(qx, kx, vx, dox, f"xa{layer}_attn_bwd")

        def epi(acc, hx, dres, gain):
            dhx, dg = rms_bwd_epi(acc, hx, dres, gain)
            return dhx, dg, _colsum(dhx)

        dhn, gn, dh_cols = _matmul(dqx, w["xa_w_q"][layer], "nt", [F32], name=f"xa{layer}_q_dx", epi=epi,
                                   mn_extras=[h, dh], row_extras=[w["xa_norm"][layer:layer + 1]], n_rowsum=2,
                                   slab=EPI_SLAB)
        gq = _matmul(nq, dqx, "tn", [BF16], name=f"xa{layer}_q_dw", tk=2048)
        dkv = jnp.concatenate([dkx, dvx], axis=-1)
        gkv = _matmul(mn, dkv, "tn", [BF16], name=f"xa{layer}_kv_dw", out_dm=True)
        dmn = _matmul(dkv, w["xa_w_kv"][layer], "nt", [F32], name=f"xa{layer}_kv_dx", tk=2048)
        _, gmem = _rms_bwd(mem, dmn, dmn, w["xa_mem_norm"][layer:layer + 1], f"xa{layer}_mem_rms_bwd")
        return dhn, gq, gkv, go, gn, gmem, dh_cols

    dh, gup1, gdown1, gmn1 = mlp_bwd(dh, 1, mlp1)
    dh, gq1, gkv1, go1, gxn1, gmem1, g_b2 = xattn_bwd(dh, 1, xa1)
    g.update(mlp_w_up=[None, gup1], mlp_w_down=[None, gdown1], xa_w_q=[None, gq1], xa_w_kv=[None, gkv1],
             xa_w_o=[None, go1])
    tok = emit(3, g)

    def ln_bwd_epi(acc, cx, gain, bias):
        _, vjp = jax.vjp(_ln_silu, cx, gain, bias)
        dc, dg, db = vjp(acc)
        return dc, dg, db, _colsum(dc)

    dc_cv, g_lng, g_lnb, g_bdw = _matmul(dh, w["cv_w_pw2"][0], "nt", [F32], name="cv_pw2_dx", after=tok,
                                        epi=ln_bwd_epi, mn_extras=[c_cv],
                                        row_extras=[w["cv_ln_g"], w["cv_ln_b"]], n_rowsum=3, slab=EPI_SLAB)
    g["cv_w_pw2"] = [_matmul(s_cv, dh, "tn", [BF16], name="cv_pw2_dw", tk=2048)]
    du_cv, g_wdw, g_b1 = _cv_mid_bwd2(dc_cv, u_cv, wdw)
    g["cv_w_pw1"] = [_matmul(n1, du_cv, "tn", [BF16], name="cv_pw1_dw", out_dm=True, tk=2048)]
    dh, g_cvn = _matmul(du_cv, w["cv_w_pw1"][0], "nt", [F32], name="cv_pw1_dx", epi=rms_bwd_epi,
                        mn_extras=[h3, dh], row_extras=[w["cv_norm"]], n_rowsum=1, slab=EPI_SLAB)
    g.update(cv_ln_g=g_lng, cv_ln_b=g_lnb, cv_b_dw=g_bdw, cv_b_pw2=g_b2, cv_b_pw1=g_b1, cv_norm=g_cvn,
             cv_w_dw=g_wdw[:CV_WIDTH][None])

    tok = emit(2, g)
    dh, gup0, gdown0, gmn0 = mlp_bwd(dh, 0, mlp0, after=tok)
    dh, gq0, gkv0, go0, gxn0, gmem0, _ = xattn_bwd(dh, 0, xa0)
    g["mlp_w_up"][0] = gup0
    g["mlp_w_down"][0] = gdown0
    g["mlp_norm"] = jnp.concatenate([gmn0, gmn1], axis=0)
    g["xa_w_q"][0] = gq0
    g["xa_w_kv"][0] = gkv0
    g["xa_w_o"][0] = go0
    g["xa_norm"] = jnp.concatenate([gxn0, gxn1], axis=0)
    g["xa_mem_norm"] = jnp.concatenate([gmem0, gmem1], axis=0)
    tok = emit(1, g)

    dog = _matmul(dh, w["dn_w_out"][0], "nt", [BF16], name="dn_out_proj_dx", after=tok)
    g["dn_w_out"] = [_matmul(og, dh, "tn", [BF16], name="dn_out_proj_dw", tk=2048)]
    dq, dk, dv, dgb, dz, g_on = _delta_bwd(q, k, v, gb, qkvz, w["dn_out_norm"], states, tms, dog, heads)
    dqkvz, dba, g_wconv, g_alog, g_dt = _dn_pre_bwd(qkvz, ba, wconv, alog, dtb, dq, dk, dv, dgb, dz, heads)
    g_qkvzt = _matmul(dqkvz, n0, "tn", [BF16], name="dn_in_proj_dw", tk=2048)
    g_bat = _matmul(dba, n0, "tn", [BF16], name="dn_in_proj_ba_dw", tk=2048)
    g["dn_w_in"] = [jnp.concatenate([g_qkvzt, g_bat[:2 * heads]], axis=0)]
    g["dn_w_conv"] = g_wconv[None]
    tok = emit(0, g)
    dn0a = _matmul(dba, w_bat, "nn", [F32], name="dn_in_proj_ba_dx", after=tok)
    grad_x, g_dnn = _matmul(dqkvz, w_int, "nn", [F32], name="dn_in_proj_dx", b_rows=4 * d,
                            epi=lambda acc, part, hx, dres, gain: rms_bwd_epi(acc + part, hx, dres, gain),
                            mn_extras=[dn0a, x, dh], row_extras=[dn_norm], n_rowsum=1, slab=EPI_SLAB)
    g.update(dn_norm=g_dnn, dn_out_norm=g_on,
             dn_a_log=g_alog[:, heads:2 * heads], dn_dt_bias=g_dt[:, heads:2 * heads])
    return loss, grad_x, g


def _round_up(n, m):
    return (n + m - 1) // m * m


def _pack_rows(parts, cols, row_mult):
    lead = parts[0].shape[:-1]
    flat, offs, off = [], [], 0
    for p in parts:
        n = _round_up(p.shape[-1], cols)
        flat.append(jnp.pad(p, [(0, 0)] * len(lead) + [(0, n - p.shape[-1])]))
        offs.append(off)
        off += n
    total = _round_up(off, cols * row_mult)
    if total > off:
        flat.append(jnp.zeros(lead + (total - off,), parts[0].dtype))
    return jnp.concatenate(flat, axis=-1).reshape(lead + (total // cols, cols)), offs


def _unpack(packed, offs, shapes):
    lead = packed.shape[:-2]
    flat = packed.reshape(lead + (-1,))
    out = []
    for off, shp in zip(offs, shapes):
        n = 1
        for v in shp:
            n *= v
        out.append(flat[..., off:off + n].reshape(lead + tuple(shp)))
    return out


def kernel(x, mem, dn_norm, dn_w_in, dn_w_conv, dn_a_log, dn_dt_bias, dn_out_norm, dn_w_out, cv_norm, cv_w_pw1, cv_b_pw1, cv_w_dw, cv_b_dw, cv_ln_g, cv_ln_b, cv_w_pw2, cv_b_pw2, xa_norm, xa_mem_norm, xa_w_q, xa_w_kv, xa_w_o, mlp_norm, mlp_w_up, mlp_w_down, final_norm, loss_target, m_dn_norm, m_dn_w_in, m_dn_w_conv, m_dn_a_log, m_dn_dt_bias, m_dn_out_norm, m_dn_w_out, m_cv_norm, m_cv_w_pw1, m_cv_b_pw1, m_cv_w_dw, m_cv_b_dw, m_cv_ln_g, m_cv_ln_b, m_cv_w_pw2, m_cv_b_pw2, m_xa_norm, m_xa_mem_norm, m_xa_w_q, m_xa_w_kv, m_xa_w_o, m_mlp_norm, m_mlp_w_up, m_mlp_w_down, m_final_norm, v_dn_norm, v_dn_w_in, v_dn_w_conv, v_dn_a_log, v_dn_dt_bias, v_dn_out_norm, v_dn_w_out, v_cv_norm, v_cv_w_pw1, v_cv_b_pw1, v_cv_w_dw, v_cv_b_dw, v_cv_ln_g, v_cv_ln_b, v_cv_w_pw2, v_cv_b_pw2, v_xa_norm, v_xa_mem_norm, v_xa_w_q, v_xa_w_kv, v_xa_w_o, v_mlp_norm, v_mlp_w_up, v_mlp_w_down, v_final_norm):
    wsh = dict(dn_norm=dn_norm, dn_w_in=dn_w_in, dn_w_conv=dn_w_conv, dn_a_log=dn_a_log, dn_dt_bias=dn_dt_bias, dn_out_norm=dn_out_norm, dn_w_out=dn_w_out, cv_norm=cv_norm, cv_w_pw1=cv_w_pw1, cv_b_pw1=cv_b_pw1, cv_w_dw=cv_w_dw, cv_b_dw=cv_b_dw, cv_ln_g=cv_ln_g, cv_ln_b=cv_ln_b, cv_w_pw2=cv_w_pw2, cv_b_pw2=cv_b_pw2, xa_norm=xa_norm, xa_mem_norm=xa_mem_norm, xa_w_q=xa_w_q, xa_w_kv=xa_w_kv, xa_w_o=xa_w_o, mlp_norm=mlp_norm, mlp_w_up=mlp_w_up, mlp_w_down=mlp_w_down, final_norm=final_norm)
    msh = dict(dn_norm=m_dn_norm, dn_w_in=m_dn_w_in, dn_w_conv=m_dn_w_conv, dn_a_log=m_dn_a_log, dn_dt_bias=m_dn_dt_bias, dn_out_norm=m_dn_out_norm, dn_w_out=m_dn_w_out, cv_norm=m_cv_norm, cv_w_pw1=m_cv_w_pw1, cv_b_pw1=m_cv_b_pw1, cv_w_dw=m_cv_w_dw, cv_b_dw=m_cv_b_dw, cv_ln_g=m_cv_ln_g, cv_ln_b=m_cv_ln_b, cv_w_pw2=m_cv_w_pw2, cv_b_pw2=m_cv_b_pw2, xa_norm=m_xa_norm, xa_mem_norm=m_xa_mem_norm, xa_w_q=m_xa_w_q, xa_w_kv=m_xa_w_kv, xa_w_o=m_xa_w_o, mlp_norm=m_mlp_norm, mlp_w_up=m_mlp_w_up, mlp_w_down=m_mlp_w_down, final_norm=m_final_norm)
    vsh = dict(dn_norm=v_dn_norm, dn_w_in=v_dn_w_in, dn_w_conv=v_dn_w_conv, dn_a_log=v_dn_a_log, dn_dt_bias=v_dn_dt_bias, dn_out_norm=v_dn_out_norm, dn_w_out=v_dn_w_out, cv_norm=v_cv_norm, cv_w_pw1=v_cv_w_pw1, cv_b_pw1=v_cv_b_pw1, cv_w_dw=v_cv_w_dw, cv_b_dw=v_cv_b_dw, cv_ln_g=v_cv_ln_g, cv_ln_b=v_cv_ln_b, cv_w_pw2=v_cv_w_pw2, cv_b_pw2=v_cv_b_pw2, xa_norm=v_xa_norm, xa_mem_norm=v_xa_mem_norm, xa_w_q=v_xa_w_q, xa_w_kv=v_xa_w_kv, xa_w_o=v_xa_w_o, mlp_norm=v_mlp_norm, mlp_w_up=v_mlp_w_up, mlp_w_down=v_mlp_w_down, final_norm=v_final_norm)

    big_axis = dict(BIG)
    for src in (wsh, msh, vsh):
        src["dn_w_in"] = jnp.swapaxes(src["dn_w_in"], 1, 2)
    big_axis["dn_w_in"] = 1

    small_pack, small_offs = _pack_rows([wsh[nm].reshape(-1) for nm in SMALL_SH], LANES, 8)
    w = {nm: [None] * wsh[nm].shape[0] for nm in big_axis}

    def put_weights(group, gathered):
        for (nm, layer), gth in zip(group, gathered):
            if big_axis[nm] == 1:
                w[nm][layer] = gth.reshape(N_DEV * gth.shape[1], gth.shape[2])
            else:
                w[nm][layer] = gth

    first = _all_gather([wsh[nm][layer].astype(BF16) for nm, layer in GATHER_GROUPS[0]] + [small_pack],
                        "weights_all_gather_0")
    put_weights(GATHER_GROUPS[0], first)
    me = _dev_index(*_mesh_me())
    gather_handles, tokens = {}, []
    for gi in range(1, len(GATHER_GROUPS)):
        shards = [wsh[nm][layer].astype(BF16) for nm, layer in GATHER_GROUPS[gi]]
        lands = [lax.dynamic_update_slice(lax.empty((N_DEV,) + s.shape, s.dtype), s[None], (me, 0, 0))
                 for s in shards]
        gather_handles[gi], tok = _exchange_start(shards, lands, False, f"weights_gather_{gi}",
                                                  after=[first[-1]] + tokens)
        tokens.append(tok)
    for nm, gth in zip(SMALL_SH, _unpack(first[-1], small_offs, [wsh[nm].shape for nm in SMALL_SH])):
        w[nm] = jnp.moveaxis(gth, 0, -2).reshape(gth.shape[1:-1] + (N_DEV * gth.shape[-1],))
    for nm in REPL:
        w[nm] = wsh[nm]

    def fetch(gi, after):
        put_weights(GATHER_GROUPS[gi], _exchange_wait(gather_handles[gi], after)[1])

    scatter_handles = {}

    def emit(gi, g):
        blocks = []
        for nm, layer in SCATTER_GROUPS[gi]:
            gw = g[nm][layer]
            if big_axis[nm] == 1:
                gw = gw.reshape(N_DEV, gw.shape[0] // N_DEV, gw.shape[1])
            blocks.append(gw)
        if gi == 0:
            gsmall_pack, _ = _pack_rows(
                [jnp.moveaxis(g[nm].reshape(g[nm].shape[:-1] + (N_DEV, -1)), -2, 0).reshape(N_DEV, -1)
                 for nm in SMALL_SH], LANES, 8)
            blocks.append(gsmall_pack)
        lands = [lax.empty(b.shape, b.dtype) for b in blocks]
        scatter_handles[gi], tok = _exchange_start(blocks, lands, True, f"grads_scatter_{gi}")
        return [tok]

    loss_part, grad_x, g = _local_step(x[0], mem[0], loss_target[0], w, fetch, emit, tokens)

    recv = {nm: [None] * wsh[nm].shape[0] for nm in big_axis}
    sent = {nm: [None] * wsh[nm].shape[0] for nm in big_axis}
    gsh, delta, new_m, new_v = {}, {}, {}, {}
    after = [grad_x]
    done = set()
    me_arr = me.astype(jnp.int32).reshape(1)
    def small_adamw(names, name):
        packs = []
        for src in (wsh, gsh, msh, vsh):
            pk, offs = _pack_rows([src[nm].reshape(-1) for nm in names], LANES, 8)
            packs.append(pk)
        outs = _adamw(*packs, name)
        for dst, pk in zip((delta, new_m, new_v), outs):
            for nm, val in zip(names, _unpack(pk, offs, [wsh[nm].shape for nm in names])):
                dst[nm] = val
        return outs[0]

    for gi in reversed(range(len(SCATTER_GROUPS))):
        if gi == 0:
            repl_pack, repl_offs = _pack_rows([g[nm].reshape(-1) for nm in REPL] + [loss_part[:, :1].reshape(-1)],
                                              LANES, 8)
            (repl_all,) = _all_gather([repl_pack], "repl_grads_all_gather")
            repl_red = _slot_sum(repl_all, "repl_grads_sum", 512)
            *repl_vals, loss_sum = _unpack(repl_red, repl_offs, [wsh[nm].shape for nm in REPL] + [(1,)])
            for nm, val in zip(REPL, repl_vals):
                gsh[nm] = val
            after = after + [small_adamw(list(REPL), "adamw_repl")]
        sources, landed = _exchange_wait(scatter_handles[gi], after)
        for (nm, layer), src, r in zip(SCATTER_GROUPS[gi], sources, landed):
            sent[nm][layer], recv[nm][layer] = src, r
        if gi == 0:
            slot = lax.broadcasted_iota(jnp.int32, landed[-1].shape, 0)
            rsmall = jnp.where(slot == me, sources[-1], landed[-1])
        for nm in big_axis:
            if nm not in done and all(r is not None for r in recv[nm]):
                gsh[nm], delta[nm], new_m[nm], new_v[nm] = _adamw_reduce(
                    me_arr, recv[nm], sent[nm], wsh[nm], msh[nm], vsh[nm], f"adamw_{nm}")
                done.add(nm)
                after = [delta[nm]]
    gsmall_red = _slot_sum(rsmall, "grads_small_sum", 512)
    for nm, val in zip(SMALL_SH, _unpack(gsmall_red, small_offs, [wsh[nm].shape for nm in SMALL_SH])):
        gsh[nm] = val
    small_adamw(list(SMALL_SH), "adamw_small")
    for dst in (gsh, delta, new_m, new_v):
        dst["dn_w_in"] = jnp.swapaxes(dst["dn_w_in"], 1, 2)
    return (loss_sum.reshape(()), grad_x[None], *[gsh[nm] for nm in WEIGHTS], *[delta[nm] for nm in WEIGHTS],
            *[new_m[nm] for nm in WEIGHTS], *[new_v[nm] for nm in WEIGHTS])
```

```python
import functools

import jax
import jax.numpy as jnp
from jax import lax
from jax.experimental import pallas as pl
from jax.experimental.pallas import tpu as pltpu

F32 = jnp.float32
BF16 = jnp.bfloat16
MESH_IDS = pl.DeviceIdType.MESH

N_DEV = 8
LANES = 128
RMS_EPS = 1e-6
LN_EPS = 1e-5
DN_HEAD_DIM = 128
DN_CONV = 4
DN_CHUNK = 64
CV_WIDTH = 31
XA_HEADS = 4
EPI_SLAB = 256
DN_HALO = 8
CV_HALO = 32

ADAM_LR = 0.001
ADAM_B1 = 0.9
ADAM_B2 = 0.999
ADAM_EPS = 1e-08
ADAM_WD = 0.01
ADAM_STEP = 10

BIG = (("dn_w_in", 2), ("dn_w_out", 1), ("cv_w_pw1", 2), ("cv_w_pw2", 1), ("xa_w_q", 1), ("xa_w_kv", 2),
       ("xa_w_o", 1), ("mlp_w_up", 2), ("mlp_w_down", 1))
_LAYER_GROUP = ("xa_w_q", "xa_w_o", "mlp_w_down", "xa_w_kv", "mlp_w_up")
GATHER_GROUPS = (
    (("dn_w_in", 0),),
    (("dn_w_out", 0),) + tuple((nm, 0) for nm in _LAYER_GROUP),
    (("cv_w_pw2", 0), ("cv_w_pw1", 0)),
    tuple((nm, 1) for nm in _LAYER_GROUP),
)
SCATTER_GROUPS = (
    (("dn_w_out", 0), ("dn_w_in", 0)),
    tuple((nm, 0) for nm in _LAYER_GROUP),
    (("cv_w_pw2", 0), ("cv_w_pw1", 0)),
    tuple((nm, 1) for nm in _LAYER_GROUP),
)
SMALL_SH = ("cv_norm", "cv_b_pw1", "cv_b_dw", "cv_ln_g", "cv_ln_b", "cv_b_pw2", "cv_w_dw", "dn_w_conv")
REPL = ("dn_norm", "dn_a_log", "dn_dt_bias", "dn_out_norm", "xa_norm", "xa_mem_norm", "mlp_norm", "final_norm")
WEIGHTS = ("dn_norm", "dn_w_in", "dn_w_conv", "dn_a_log", "dn_dt_bias", "dn_out_norm", "dn_w_out", "cv_norm",
           "cv_w_pw1", "cv_b_pw1", "cv_w_dw", "cv_b_dw", "cv_ln_g", "cv_ln_b", "cv_w_pw2", "cv_b_pw2", "xa_norm",
           "xa_mem_norm", "xa_w_q", "xa_w_kv", "xa_w_o", "mlp_norm", "mlp_w_up", "mlp_w_down", "final_norm")


def _dot_dims(mode, batched):
    o = 1 if batched else 0
    contract = {"nn": ((1 + o,), (o,)), "nt": ((1 + o,), (1 + o,)), "tn": ((o,), (o,))}[mode]
    return (contract, (((0,), (0,)) if batched else ((), ())))


def _bdot(a, b, mode):
    return lax.dot_general(a.astype(BF16), b.astype(BF16), _dot_dims(mode, a.ndim == 3),
                           preferred_element_type=F32)


@functools.partial(jax.custom_vjp, nondiff_argnums=(2,))
def _mm(a, b, mode):
    return _bdot(a, b, mode)


def _mm_fwd(a, b, mode):
    return _bdot(a, b, mode), (a, b)


def _mm_bwd(mode, res, ct):
    a, b = res
    if mode == "nn":
        da, db = _bdot(ct, b, "nt"), _bdot(a, ct, "tn")
    elif mode == "nt":
        da, db = _bdot(ct, b, "nn"), _bdot(ct, a, "tn")
    else:
        da, db = _bdot(b, ct, "nt"), _bdot(a, ct, "nn")
    return da.astype(a.dtype), db.astype(b.dtype)


_mm.defvjp(_mm_fwd, _mm_bwd)


def _sigmoid(x):
    return 0.5 * (jnp.tanh(0.5 * x) + 1.0)


def _silu(x):
    return x * _sigmoid(x)


def _softplus(x):
    return jnp.maximum(x, 0.0) + jnp.log(1.0 + jnp.exp(-jnp.abs(x)))


def _rms(x, g):
    r = lax.rsqrt(jnp.mean(x * x, axis=-1, keepdims=True) + RMS_EPS)
    return x * r * g


def _shift_rows(x, off):
    if off == 0:
        return x
    return pltpu.roll(x, x.shape[0] - off, 0)


def _series_dot(a, b, mode):
    return _bdot(a, b, mode)


def _lane_is_a(shape, half, period=None):
    lane = lax.broadcasted_iota(jnp.int32, shape, len(shape) - 1)
    if period is not None:
        assert period & (period - 1) == 0
        lane = jnp.bitwise_and(lane, period - 1)
    return lane < half


def _block_diag(xp, half, period=None):
    is_a = _lane_is_a(xp.shape, half, period)
    return jnp.concatenate([jnp.where(is_a, xp, 0.0), jnp.where(is_a, 0.0, xp)], axis=-2)


def _diag_blocks(full, half, period=None):
    r = full.shape[-2] // 2
    return jnp.where(_lane_is_a(full[:, :r].shape, half, period), full[:, :r], full[:, r:])


def _pair(a_col, b_col, width, half, period=None):
    is_a = _lane_is_a(a_col.shape[:-1] + (width,), half, period)
    return jnp.where(is_a, a_col, b_col)


def _neumann_inverse(lm):
    c = lm.shape[-2]
    t = -lm
    p = lm
    size = 2
    while size < c:
        size *= 2
        p = _series_dot(p, _block_diag(p, c), "nn")
        t = t + p + _series_dot(t, _block_diag(p, c), "nn")
    return t


@jax.custom_vjp
def _unit_lower_solve(lm, rhs, tm):
    return rhs + _series_dot(tm, _block_diag(rhs, DN_HEAD_DIM, 2 * DN_HEAD_DIM), "nn")


def _uls_fwd(lm, rhs, tm):
    sol = _unit_lower_solve(lm, rhs, tm)
    return sol, (tm, sol)


def _uls_bwd(res, ct):
    tm, sol = res
    d_rhs = ct + _diag_blocks(_series_dot(tm, ct, "tn"), DN_HEAD_DIM, 2 * DN_HEAD_DIM)
    d_lm = -_bdot(d_rhs, _block_diag(sol, DN_HEAD_DIM, 2 * DN_HEAD_DIM), "nt")
    return d_lm, d_rhs, jnp.zeros_like(tm)


_unit_lower_solve.defvjp(_uls_fwd, _uls_bwd)


def _delta_chunk(q, k, v, g_a, g_b, b_a, b_b, s0, tm=None):
    c, hd = q.shape[-2], q.shape[-1] // 2
    ii = lax.broadcasted_iota(jnp.int32, (c, 2 * c), 0)
    jj = jnp.bitwise_and(lax.broadcasted_iota(jnp.int32, (c, 2 * c), 1), c - 1)
    eye, causal, strict = (ii == jj).astype(F32), ii >= jj, ii > jj
    in_a = _lane_is_a((c, 2 * c), c)
    grow = jnp.sum(eye * _pair(g_a, g_b, 2 * c, c), axis=-2, keepdims=True)
    run = jnp.where(causal, grow, 0.0)
    gc_a = jnp.sum(jnp.where(in_a, run, 0.0), axis=-1, keepdims=True)
    gc_b = jnp.sum(jnp.where(in_a, 0.0, run), axis=-1, keepdims=True)
    gc_col = _pair(gc_a, gc_b, 2 * c, c)
    gc_row = jnp.sum(eye * gc_col, axis=-2, keepdims=True)
    decay = jnp.exp(jnp.where(causal, gc_col - gc_row, -jnp.inf))
    beta = _pair(b_a, b_b, 2 * hd, hd)
    kb = k * beta
    k_bd = _block_diag(k, hd)
    lm = jnp.where(strict, _mm(kb, k_bd, "nt") * decay, 0.0)
    if tm is None:
        tm = _neumann_inverse(lax.stop_gradient(lm))
    gc = _pair(gc_a, gc_b, 2 * hd, hd)
    egc = jnp.exp(gc)
    sol = _unit_lower_solve(lm, jnp.concatenate([v * beta, kb * egc], axis=-1), tm)
    u, w = sol[..., :2 * hd], sol[..., 2 * hd:]
    attn = _mm(q, k_bd, "nt") * decay
    qd = q * egc
    row_a = _lane_is_a((1, 2 * c), c)
    gl = _pair(jnp.sum(jnp.where(row_a, grow, 0.0), axis=-1, keepdims=True),
               jnp.sum(jnp.where(row_a, 0.0, grow), axis=-1, keepdims=True), 2 * hd, hd)
    kd = k * jnp.exp(gl - gc)
    s_bd = _block_diag(s0, hd)
    v_new = u - _mm(w, s_bd, "nn")
    o = _mm(qd, s_bd, "nn") + _mm(attn, _block_diag(v_new, hd), "nn")
    s1 = s0 * jnp.exp(gl) + _diag_blocks(_mm(kd, v_new, "tn"), hd)
    return o, s1, tm


def _dn_point(cv, ba, alog, dt, heads):
    a = _silu(cv)
    d = cv.shape[1] // 3
    qs, ks = [], []
    for h in range(heads):
        qh = a[:, h * DN_HEAD_DIM:(h + 1) * DN_HEAD_DIM]
        qs.append(qh * lax.rsqrt(jnp.sum(qh * qh, axis=-1, keepdims=True) + 1e-6) * (DN_HEAD_DIM ** -0.5))
        kh = a[:, d + h * DN_HEAD_DIM:d + (h + 1) * DN_HEAD_DIM]
        ks.append(kh * lax.rsqrt(jnp.sum(kh * kh, axis=-1, keepdims=True) + 1e-6))
    q = jnp.concatenate(qs, axis=-1)
    k = jnp.concatenate(ks, axis=-1)
    v = a[:, 2 * d:]
    lane = lax.broadcasted_iota(jnp.int32, ba.shape, 1)
    beta = _sigmoid(ba)
    g = -jnp.exp(alog) * _softplus(ba + dt)
    gb = jnp.where(lane < heads, beta, jnp.where(lane < 2 * heads, g, 0.0))
    return q, k, v, gb


def _attn_tile(q, k, v):
    hd = q.shape[1] // XA_HEADS
    outs = []
    for h in range(XA_HEADS):
        sl = slice(h * hd, (h + 1) * hd)
        s = _mm(q[:, sl], k[:, sl], "nt") * (hd ** -0.5)
        m = lax.stop_gradient(jnp.max(s, axis=-1, keepdims=True))
        e = jnp.exp(s - m)
        p = e / jnp.sum(e, axis=-1, keepdims=True)
        outs.append(_mm(p, v[:, sl], "nn"))
    return jnp.concatenate(outs, axis=-1)


def _ln_silu(c, g, b):
    mu = jnp.mean(c, axis=-1, keepdims=True)
    xc = c - mu
    y = xc * lax.rsqrt(jnp.mean(xc * xc, axis=-1, keepdims=True) + LN_EPS)
    return _silu(y * g + b)


def _causal_conv(xext, w, width, lead, ts):
    acc = None
    for j in range(width):
        term = _shift_rows(xext, lead + j)[:ts] * w[j:j + 1, :]
        acc = term if acc is None else acc + term
    return acc


def _colsum(x):
    return jnp.sum(x, axis=0, keepdims=True)


def _stack_rows(rows, n_rows):
    c = rows[0].shape[1]
    ridx = lax.broadcasted_iota(jnp.int32, (n_rows, c), 0)
    out = jnp.zeros((n_rows, c), F32)
    for j, r in enumerate(rows):
        out = out + jnp.where(ridx == j, r, 0.0)
    return out


def _matmul(a, b, mode, out_dtypes, *, name, epi=None, mn_extras=(), row_extras=(), out_dm=False, after=(),
            n_rowsum=0, slab=0, b_rows=None, a_pre=None, tm=1024, tn=1024, tk=1024):
    b_dm = b.ndim == 3
    b_shape = (b.shape[1], N_DEV * b.shape[2]) if b_dm else b.shape
    if b_rows is not None:
        assert not b_dm and b_rows <= b.shape[0]
        b_shape = (b_rows, b.shape[1])
    if mode == "nn":
        (m, k), (k2, n) = a.shape, b_shape
    elif mode == "nt":
        (m, k), (n, k2) = a.shape, b_shape
    else:
        (k, m), (k2, n) = a.shape, b_shape
    assert k == k2, (a.shape, b.shape, mode)
    tm, tn, tk = min(tm, m), min(tn, n), min(tk, k)
    cb, nb = 0, 1
    if b_dm:
        assert mode in ("nn", "nt")
        cb = b.shape[2]
        nb = max(1, (tn if mode == "nn" else tk) // cb)
        if mode == "nn":
            tn = nb * cb
        else:
            tk = nb * cb
    co, no = 0, 1
    if out_dm:
        co = n // N_DEV
        no = max(1, tn // co)
        tn = no * co
    assert m % tm == 0 and n % tn == 0 and k % tk == 0, (m, n, k, tm, tn, tk)
    nk = k // tk
    if mode == "tn":
        a_spec = pl.BlockSpec((tk, tm), lambda j, i, kk: (kk, i))
    else:
        a_spec = pl.BlockSpec((tm, tk), lambda j, i, kk: (i, kk))
    if b_dm:
        b_spec = (pl.BlockSpec((nb, tn, cb), lambda j, i, kk: (kk, j, 0)) if mode == "nt"
                  else pl.BlockSpec((nb, tk, cb), lambda j, i, kk: (j, kk, 0)))
    else:
        b_spec = (pl.BlockSpec((tn, tk), lambda j, i, kk: (j, kk)) if mode == "nt"
                  else pl.BlockSpec((tk, tn), lambda j, i, kk: (kk, j)))
    mn_spec = pl.BlockSpec((tm, tn), lambda j, i, kk: (i, j))
    row_spec = pl.BlockSpec((1, tn), lambda j, i, kk: (0, j))
    n_extra = len(mn_extras) + len(row_extras)
    n_out = len(out_dtypes)
    in_specs = ([a_spec, b_spec] + [mn_spec] * len(mn_extras) + [row_spec] * len(row_extras)
                + [_ANY_SPEC] * len(after))
    args = [a, b, *mn_extras, *row_extras, *after]
    if out_dm:
        out_specs = [pl.BlockSpec((no, tm, co), lambda j, i, kk: (j, i, 0))] * n_out
        out_shape = [jax.ShapeDtypeStruct((N_DEV, m, co), dt) for dt in out_dtypes]
    else:
        out_specs = [mn_spec] * n_out
        out_shape = [jax.ShapeDtypeStruct((m, n), dt) for dt in out_dtypes]
    out_specs = out_specs + [row_spec] * n_rowsum
    out_shape = out_shape + [jax.ShapeDtypeStruct((1, n), F32)] * n_rowsum
    n_in = len(args)
    n_mn = len(mn_extras)
    step = min(slab, tm) if slab else tm
    assert tm % step == 0

    def dot(a_ref, b_ref):
        a_val = a_ref[...] if a_pre is None else a_pre(a_ref[...])
        if not b_dm:
            return _bdot(a_val, b_ref[...], mode)
        if mode == "nn":
            parts = [_bdot(a_val, b_ref[dd], "nn") for dd in range(nb)]
            return parts[0] if nb == 1 else jnp.concatenate(parts, axis=1)
        out = None
        for dd in range(nb):
            part = _bdot(a_val[:, dd * cb:(dd + 1) * cb], b_ref[dd], "nt")
            out = part if out is None else out + part
        return out

    def finish(acc_src, extras, outs):
        sums = [None] * n_rowsum
        for r0 in range(0, tm, step):
            rs = slice(r0, r0 + step)
            acc_val = acc_src[rs, :]
            if epi is None:
                vals = (acc_val,)
            else:
                vals = epi(acc_val, *[e[rs, :] for e in extras[:n_mn]], *[e[...] for e in extras[n_mn:]])
            for o_ref, val in zip(outs[:n_out], vals[:n_out]):
                if out_dm:
                    for dd in range(no):
                        o_ref[dd, rs, :] = val[:, dd * co:(dd + 1) * co].astype(o_ref.dtype)
                else:
                    o_ref[rs, :] = val.astype(o_ref.dtype)
            for q in range(n_rowsum):
                sums[q] = vals[n_out + q] if sums[q] is None else sums[q] + vals[n_out + q]
        for q in range(n_rowsum):
            s_ref = outs[n_out + q]

            @pl.when(pl.program_id(1) == 0)
            def _():
                s_ref[...] = sums[q]

            @pl.when(pl.program_id(1) > 0)
            def _():
                s_ref[...] += sums[q]

    def body_one_step(*refs):
        finish(dot(refs[0], refs[1]), refs[2:2 + n_extra], refs[n_in:])

    def body(*refs):
        a_ref, b_ref = refs[0], refs[1]
        acc = refs[-1]
        kk = pl.program_id(2)

        @pl.when(kk == 0)
        def _():
            acc[...] = jnp.zeros_like(acc)

        acc[...] += dot(a_ref, b_ref)

        @pl.when(kk == nk - 1)
        def _():
            finish(acc, refs[2:2 + n_extra], refs[n_in:-1])

    res = pl.pallas_call(
        body_one_step if nk == 1 else body, name=name,
        grid=(n // tn, m // tm, nk),
        in_specs=in_specs, out_specs=out_specs, out_shape=out_shape,
        scratch_shapes=[] if nk == 1 else [pltpu.VMEM((tm, tn), F32)],
        compiler_params=pltpu.CompilerParams(
            dimension_semantics=("parallel", "arbitrary" if n_rowsum else "parallel", "arbitrary")),
    )(*args)
    return res[0] if n_out + n_rowsum == 1 else res


def _rowwise(fn, *, n_rows, ts, name, rows=(), prevs=(), nexts=(), vecs=(), row_outs=(), acc_outs=(), after=()):
    ts = min(ts, n_rows)
    assert n_rows % ts == 0
    nblk = n_rows // ts
    in_specs, args = [], []
    for arr, cb, w in rows:
        in_specs.append(pl.BlockSpec((ts, w), functools.partial(lambda i, cb: (i, cb), cb=cb)))
        args.append(arr)
    for arr, cb, w, halo in prevs:
        per = ts // halo
        in_specs.append(pl.BlockSpec(
            (halo, w), functools.partial(lambda i, cb, per: (jnp.maximum(i * per - 1, 0), cb), cb=cb, per=per)))
        args.append(arr)
    for arr, cb, w, halo in nexts:
        per = ts // halo
        last_blk = n_rows // halo - 1
        in_specs.append(pl.BlockSpec(
            (halo, w), functools.partial(lambda i, cb, per, lb: (jnp.minimum((i + 1) * per, lb), cb),
                                         cb=cb, per=per, lb=last_blk)))
        args.append(arr)
    for arr in vecs:
        in_specs.append(pl.BlockSpec(arr.shape, functools.partial(lambda i, nd: (0,) * nd, nd=arr.ndim)))
        args.append(arr)
    out_specs, out_shape = [], []
    for w, dt in row_outs:
        out_specs.append(pl.BlockSpec((ts, w), lambda i: (i, 0)))
        out_shape.append(jax.ShapeDtypeStruct((n_rows, w), dt))
    for shp in acc_outs:
        out_specs.append(pl.BlockSpec(shp, functools.partial(lambda i, nd: (0,) * nd, nd=len(shp))))
        out_shape.append(jax.ShapeDtypeStruct(shp, F32))
    n_used = len(args)
    n_tiles = n_used - len(vecs)
    in_specs += [_ANY_SPEC] * len(after)
    args += list(after)
    n_in, n_ro, n_acc = len(args), len(row_outs), len(acc_outs)

    def body(*refs):
        ins, ro, ac = refs[:n_used], refs[n_in:n_in + n_ro], refs[n_in + n_ro:]
        i = pl.program_id(0)
        rvals, avals = fn(i == 0, i == nblk - 1, *[r[...] for r in ins[:n_tiles]], *ins[n_tiles:])
        for r, val in zip(ro, rvals):
            r[...] = val.astype(r.dtype)
        if n_acc:
            @pl.when(i == 0)
            def _():
                for r in ac:
                    r[...] = jnp.zeros_like(r)

            for r, val in zip(ac, avals):
                r[...] += val

    res = pl.pallas_call(
        body, name=name, grid=(nblk,), in_specs=in_specs, out_specs=out_specs, out_shape=out_shape,
        compiler_params=pltpu.CompilerParams(dimension_semantics=("arbitrary",)),
    )(*args)
    return res


def _gated_out(o, z, onorm):
    hd = o.shape[-1] // 2
    halves = [oh * lax.rsqrt(jnp.mean(oh * oh, axis=-1, keepdims=True) + RMS_EPS) * onorm
              for oh in (o[..., :hd], o[..., hd:])]
    return jnp.concatenate(halves, axis=-1) * _silu(z)


def _pair_blocks(ref, heads):
    w = 2 * DN_HEAD_DIM
    return jnp.stack([ref[:, p * w:(p + 1) * w] for p in range(heads // 2)])


def _split_pairs(q_ref, k_ref, v_ref, gbv, heads):
    def cols(first):
        return jnp.stack([gbv[:, first + 2 * p:first + 2 * p + 1] for p in range(heads // 2)])

    return (_pair_blocks(q_ref, heads), _pair_blocks(k_ref, heads), _pair_blocks(v_ref, heads),
            cols(heads), cols(heads + 1), cols(0), cols(1))


def _delta_fwd(q, k, v, gb, qkvz, onorm, heads):
    s, hd = q.shape
    n = s // DN_CHUNK
    assert heads % 2 == 0
    pairs, pw = heads // 2, 2 * DN_HEAD_DIM
    blk = pl.BlockSpec((DN_CHUNK, hd), lambda c: (c, 0))
    gspec = pl.BlockSpec((DN_CHUNK, LANES), lambda c: (c, 0))

    def body(q_ref, k_ref, v_ref, gb_ref, z_ref, on_ref, og_ref, st_ref, tm_ref, state):
        @pl.when(pl.program_id(0) == 0)
        def _():
            state[...] = jnp.zeros_like(state)

        s0 = state[...]
        st_ref[0] = s0
        o, s1, tm = _delta_chunk(*_split_pairs(q_ref, k_ref, v_ref, gb_ref[...], heads), s0)
        og = _gated_out(o, _pair_blocks(z_ref, heads), on_ref[...])
        for p in range(pairs):
            og_ref[:, p * pw:(p + 1) * pw] = og[p].astype(og_ref.dtype)
        state[...] = s1
        tm_ref[0] = tm

    return pl.pallas_call(
        body, name="dn_delta_fwd", grid=(n,),
        in_specs=[blk, blk, blk, gspec, pl.BlockSpec((DN_CHUNK, hd), lambda c: (c, 3)),
                  pl.BlockSpec(onorm.shape, lambda c: (0, 0))],
        out_specs=[blk, pl.BlockSpec((1, pairs, DN_HEAD_DIM, pw), lambda c: (c, 0, 0, 0)),
                   pl.BlockSpec((1, pairs, DN_CHUNK, 2 * DN_CHUNK), lambda c: (c, 0, 0, 0))],
        out_shape=[jax.ShapeDtypeStruct((s, hd), BF16),
                   jax.ShapeDtypeStruct((n, pairs, DN_HEAD_DIM, pw), F32),
                   jax.ShapeDtypeStruct((n, pairs, DN_CHUNK, 2 * DN_CHUNK), F32)],
        scratch_shapes=[pltpu.VMEM((pairs, DN_HEAD_DIM, pw), F32)],
        compiler_params=pltpu.CompilerParams(dimension_semantics=("arbitrary",)),
    )(q, k, v, gb, qkvz, onorm)


def _delta_bwd(q, k, v, gb, qkvz, onorm, states, tms, dog, heads):
    s, hd = q.shape
    n = s // DN_CHUNK
    pairs, pw = heads // 2, 2 * DN_HEAD_DIM
    blk = pl.BlockSpec((DN_CHUNK, hd), lambda c: (n - 1 - c, 0))
    gspec = pl.BlockSpec((DN_CHUNK, LANES), lambda c: (n - 1 - c, 0))
    sspec = pl.BlockSpec((1, pairs, DN_HEAD_DIM, pw), lambda c: (n - 1 - c, 0, 0, 0))
    tspec = pl.BlockSpec((1, pairs, DN_CHUNK, 2 * DN_CHUNK), lambda c: (n - 1 - c, 0, 0, 0))
    nspec = pl.BlockSpec(onorm.shape, lambda c: (0, 0))

    def body(q_ref, k_ref, v_ref, gb_ref, z_ref, on_ref, st_ref, tm_ref, dog_ref,
             dq_ref, dk_ref, dv_ref, dgb_ref, dz_ref, don_ref, dstate):
        @pl.when(pl.program_id(0) == 0)
        def _():
            dstate[...] = jnp.zeros_like(dstate)
            don_ref[...] = jnp.zeros_like(don_ref)

        gbv = gb_ref[...]
        tm = tm_ref[0]

        def chunk(q2, k2, v2, g_a, g_b, b_a, b_b, s0, z2, on):
            o, s1, _ = _delta_chunk(q2, k2, v2, g_a, g_b, b_a, b_b, s0, tm)
            return _gated_out(o, z2, on), s1

        _, vjp = jax.vjp(chunk, *_split_pairs(q_ref, k_ref, v_ref, gbv, heads), st_ref[0],
                         _pair_blocks(z_ref, heads), on_ref[...])
        dq, dk, dv, dg_a, dg_b, db_a, db_b, ds0, dz, don = vjp(
            (_pair_blocks(dog_ref, heads).astype(F32), dstate[...]))
        dstate[...] = ds0
        don_ref[...] += don
        lane = lax.broadcasted_iota(jnp.int32, gbv.shape, 1)
        dgb = jnp.zeros(gbv.shape, F32)
        for p in range(pairs):
            sl = slice(p * pw, (p + 1) * pw)
            dq_ref[:, sl] = dq[p]
            dk_ref[:, sl] = dk[p]
            dv_ref[:, sl] = dv[p]
            dz_ref[:, sl] = dz[p]
            for first, col in ((2 * p, db_a), (2 * p + 1, db_b), (heads + 2 * p, dg_a), (heads + 2 * p + 1, dg_b)):
                dgb = dgb + jnp.where(lane == first, col[p], 0.0)
        dgb_ref[...] = dgb

    return pl.pallas_call(
        body, name="dn_delta_bwd", grid=(n,),
        in_specs=[blk, blk, blk, gspec, pl.BlockSpec((DN_CHUNK, hd), lambda c: (n - 1 - c, 3)), nspec,
                  sspec, tspec, blk],
        out_specs=[blk, blk, blk, gspec, blk, nspec],
        out_shape=[jax.ShapeDtypeStruct((s, hd), F32)] * 3 + [jax.ShapeDtypeStruct((s, LANES), F32),
                                                              jax.ShapeDtypeStruct((s, hd), F32),
                                                              jax.ShapeDtypeStruct(onorm.shape, F32)],
        scratch_shapes=[pltpu.VMEM((pairs, DN_HEAD_DIM, pw), F32)],
        compiler_params=pltpu.CompilerParams(dimension_semantics=("arbitrary",)),
    )(q, k, v, gb, qkvz, onorm, states, tms, dog)


def _dev_index(px, py, pc):
    return 4 * px + 2 * py + pc


def _all_gather(arrs, name):
    n = len(arrs)

    def body(*refs):
        xs, outs = refs[:n], refs[n:2 * n]
        send_sems, recv_sems, local_sems = refs[2 * n:]
        x, y, c = lax.axis_index("x"), lax.axis_index("y"), lax.axis_index("c")
        me, sibling = (x, y, c), (x, y, 1 - c)
        chips = [(1 - x, y), (x, 1 - y), (1 - x, 1 - y)]

        def copy(a, kk, block, to, src=None):
            dst = outs[a].at[_dev_index(*block)]
            return pltpu.make_async_remote_copy(
                src_ref=dst if src is None else src, dst_ref=dst,
                send_sem=send_sems.at[a * 7 + kk], recv_sem=recv_sems.at[a * 7 + kk],
                device_id=to, device_id_type=MESH_IDS)

        mine = [pltpu.make_async_copy(xs[a], outs[a].at[_dev_index(*me)], local_sems.at[a]) for a in range(n)]
        for cp in mine:
            cp.start()
        first = []
        for a in range(n):
            first.append(copy(a, 0, me, sibling, src=xs[a]))
            first += [copy(a, 1 + j, me, (*chip, c), src=xs[a]) for j, chip in enumerate(chips)]
        for cp in first:
            cp.start()
        passed = []
        for j, chip in enumerate(chips):
            for a in range(n):
                copy(a, 1 + j, (*chip, c), me).wait_recv()
                fwd = copy(a, 4 + j, (*chip, c), sibling)
                fwd.start()
                passed.append(fwd)
        for a in range(n):
            copy(a, 0, sibling, me).wait_recv()
        for j, chip in enumerate(chips):
            for a in range(n):
                copy(a, 4 + j, (*chip, 1 - c), me).wait_recv()
        for cp in first + passed:
            cp.wait_send()
        for cp in mine:
            cp.wait()

    hbm = pl.BlockSpec(memory_space=pltpu.HBM)
    res = pl.pallas_call(
        body, name=name,
        in_specs=[hbm] * n, out_specs=[hbm] * n,
        out_shape=[jax.ShapeDtypeStruct((N_DEV,) + a.shape, a.dtype) for a in arrs],
        scratch_shapes=[pltpu.SemaphoreType.DMA((7 * n,)), pltpu.SemaphoreType.DMA((7 * n,)),
                        pltpu.SemaphoreType.DMA((n,))],
    )(*arrs)
    return list(res)


_FLIPS = ((0, 0, 1), (1, 0, 0), (0, 1, 0), (1, 1, 0), (1, 0, 1), (0, 1, 1), (1, 1, 1))
_HBM_SPEC = pl.BlockSpec(memory_space=pltpu.HBM)
_SEM_SPEC = pl.BlockSpec(memory_space=pltpu.SEMAPHORE)
_ANY_SPEC = pl.BlockSpec(memory_space=pl.ANY)
_DATAFLOW = pltpu.SideEffectType.DATAFLOW_SIDE_EFFECTING
TOKEN_SHAPE = (8, LANES)


def _mesh_me():
    return lax.axis_index("x"), lax.axis_index("y"), lax.axis_index("c")


def _flipped(me, f):
    return tuple(1 - v if fl else v for v, fl in zip(me, f))


def _exchange_copies(xs, lands, send_sems, recv_sems, scatter, landed):
    me = _mesh_me()
    cps = []
    for kk, f in enumerate(_FLIPS):
        p = _flipped(me, f)
        for a in range(len(xs)):
            cps.append(pltpu.make_async_remote_copy(
                src_ref=xs[a].at[_dev_index(*p)] if scatter else xs[a],
                dst_ref=lands[a].at[_dev_index(*(p if landed else me))],
                send_sem=send_sems.at[a * 7 + kk], recv_sem=recv_sems.at[a * 7 + kk],
                device_id=p, device_id_type=MESH_IDS))
    return cps


def _exchange_start(srcs, lands, scatter, name, after=()):
    n = len(srcs)

    n_after = len(after)

    def body(*refs):
        xs, ls = refs[:n], refs[n:2 * n]
        send_sems, recv_sems = refs[2 * n + n_after], refs[2 * n + n_after + 1]
        token = refs[-1]
        for cp in _exchange_copies(xs, ls, send_sems, recv_sems, scatter, landed=False):
            cp.start()
        token[...] = jnp.zeros_like(token)

    operands = [pltpu.with_memory_space_constraint(a, pltpu.HBM) for a in list(srcs) + list(lands)]
    res = pl.pallas_call(
        body, name=name,
        in_specs=[_HBM_SPEC] * (2 * n) + [_ANY_SPEC] * len(after),
        out_specs=[_SEM_SPEC, _SEM_SPEC] + [_HBM_SPEC] * (2 * n) + [pl.BlockSpec(memory_space=pltpu.VMEM)],
        out_shape=[pltpu.SemaphoreType.DMA((7 * n,)), pltpu.SemaphoreType.DMA((7 * n,))]
        + [pltpu.HBM(a.shape, a.dtype) for a in operands] + [jax.ShapeDtypeStruct(TOKEN_SHAPE, F32)],
        input_output_aliases={i: 2 + i for i in range(2 * n)},
        compiler_params=pltpu.CompilerParams(has_side_effects=_DATAFLOW),
    )(*operands, *after)
    return (res[0], res[1], list(res[2:2 + n]), list(res[2 + n:2 + 2 * n]), scatter, name), res[-1]


def _exchange_wait(handle, after):
    send_sems, recv_sems, srcs, lands, scatter, name = handle
    n = len(srcs)
    n_after = len(after)

    def body(*refs):
        xs, ls = refs[:n], refs[n:2 * n]
        send_sems_ref, recv_sems_ref = refs[2 * n], refs[2 * n + 1]
        for cp in _exchange_copies(xs, ls, send_sems_ref, recv_sems_ref, scatter, landed=True):
            cp.wait_send()
            cp.wait_recv()

    res = pl.pallas_call(
        body, name=name + "_wait",
        in_specs=[_HBM_SPEC] * (2 * n) + [_SEM_SPEC, _SEM_SPEC] + [_ANY_SPEC] * n_after,
        out_specs=[_HBM_SPEC] * (2 * n),
        out_shape=[pltpu.HBM(a.shape, a.dtype) for a in srcs + lands],
        input_output_aliases={i: i for i in range(2 * n)},
        compiler_params=pltpu.CompilerParams(has_side_effects=_DATAFLOW),
    )(*srcs, *lands, send_sems, recv_sems, *after)
    return list(res[:n]), list(res[n:])


def _slot_sum(g, name, tr):
    _, r, c = g.shape
    tr = min(tr, r)
    assert r % tr == 0

    def body(g_ref, o_ref):
        acc = g_ref[0].astype(F32)
        for s in range(1, N_DEV):
            acc = acc + g_ref[s].astype(F32)
        o_ref[...] = acc

    return pl.pallas_call(
        body, name=name, grid=(r // tr,),
        in_specs=[pl.BlockSpec((N_DEV, tr, c), lambda i: (0, i, 0))],
        out_specs=pl.BlockSpec((tr, c), lambda i: (i, 0)),
        out_shape=jax.ShapeDtypeStruct((r, c), F32),
        compiler_params=pltpu.CompilerParams(dimension_semantics=("parallel",)),
    )(g)


def _adam_update(w, gg, m, v):
    c1 = 1.0 / (1.0 - ADAM_B1 ** ADAM_STEP)
    c2 = 1.0 / (1.0 - ADAM_B2 ** ADAM_STEP)
    nm = ADAM_B1 * m + (1.0 - ADAM_B1) * gg
    nv = ADAM_B2 * v + (1.0 - ADAM_B2) * (gg * gg)
    return -ADAM_LR * ((nm * c1) / (jnp.sqrt(nv * c2) + ADAM_EPS) + ADAM_WD * w), nm, nv


def _adamw_reduce(me, recvs, owns, w, m, v, name, tr=256):
    nl, r, c = w.shape
    assert len(recvs) == nl and len(owns) == nl
    tr = min(tr, r)
    if r % tr == 0:
        tc, nblk = c, r // tr
        at = lambda i: (i, 0)
    else:
        tr, tc = r, min(c, 4 * LANES)
        assert c % tc == 0
        nblk = c // tc
        at = lambda i: (0, i)

    def parked(li, l, i):
        return jnp.where(l < li, 0, jnp.where(l > li, nblk - 1, i))

    def recv_spec(li):
        return pl.BlockSpec((N_DEV, tr, tc), lambda l, i, me_ref: (0, *at(parked(li, l, i))))

    def own_spec(li):
        return pl.BlockSpec((None, tr, tc), lambda l, i, me_ref: (me_ref[0], *at(parked(li, l, i))))

    def body(me_ref, *refs):
        rrefs, orefs = refs[:nl], refs[nl:2 * nl]
        w_ref, m_ref, v_ref, g_ref, d_ref, nm_ref, nv_ref = refs[2 * nl:]
        l = pl.program_id(0)

        def of_layer(vals):
            out = vals[0]
            for li in range(1, nl):
                out = jnp.where(l == li, vals[li], out)
            return out

        own = of_layer([o[...].astype(F32) for o in orefs])
        gg = None
        for s in range(N_DEV):
            slot = jnp.where(me_ref[0] == s, own, of_layer([rr[s].astype(F32) for rr in rrefs]))
            gg = slot if gg is None else gg + slot
        g_ref[...] = gg
        d_ref[...], nm_ref[...], nv_ref[...] = _adam_update(w_ref[...], gg, m_ref[...], v_ref[...])

    spec = pl.BlockSpec((None, tr, tc), lambda l, i, me_ref: (l, *at(i)))
    return pl.pallas_call(
        body, name=name,
        grid_spec=pltpu.PrefetchScalarGridSpec(
            num_scalar_prefetch=1, grid=(nl, nblk),
            in_specs=[recv_spec(li) for li in range(nl)] + [own_spec(li) for li in range(nl)] + [spec] * 3,
            out_specs=[spec] * 4),
        out_shape=[jax.ShapeDtypeStruct((nl, r, c), F32)] * 4,
        compiler_params=pltpu.CompilerParams(dimension_semantics=("arbitrary", "arbitrary")),
    )(me, *recvs, *owns, w, m, v)


def _adamw(w, g, m, v, name, tr=256):
    r, c = w.shape
    tr = min(tr, r)
    assert r % tr == 0

    def body(w_ref, g_ref, m_ref, v_ref, d_ref, nm_ref, nv_ref):
        d_ref[...], nm_ref[...], nv_ref[...] = _adam_update(w_ref[...], g_ref[...], m_ref[...], v_ref[...])

    spec = pl.BlockSpec((tr, c), lambda i: (i, 0))
    return pl.pallas_call(
        body, name=name, grid=(r // tr,), in_specs=[spec] * 4, out_specs=[spec] * 3,
        out_shape=[jax.ShapeDtypeStruct((r, c), F32)] * 3,
        compiler_params=pltpu.CompilerParams(dimension_semantics=("parallel",)),
    )(w, g, m, v)


def _rms_fwd(x, g, name, ts=512, after=()):
    s, d = x.shape

    def fn(first, last, xv, gv):
        return [_rms(xv, gv[...])], []

    return _rowwise(fn, n_rows=s, ts=ts, name=name, rows=[(x, 0, d)], vecs=[g], row_outs=[(d, BF16)],
                    after=after)[0]


def _rms_bwd(x, dn, dres, g, name, ts=256):
    s, d = x.shape

    def fn(first, last, xv, dnv, drv, gv):
        _, vjp = jax.vjp(_rms, xv, gv[...])
        dx, dg = vjp(dnv.astype(F32))
        return [drv + dx], [dg]

    return _rowwise(fn, n_rows=s, ts=ts, name=name, rows=[(x, 0, d), (dn, 0, d), (dres, 0, d)], vecs=[g],
                    row_outs=[(d, F32)], acc_outs=[(1, d)])


def _dn_pre_fwd(qkvz, ba, wconv, alog, dt, heads, ts=128):
    s = qkvz.shape[0]
    d3 = wconv.shape[1]
    d = d3 // 3

    def fn(first, last, xc, bav, xp, wv, av, dv):
        xext = jnp.concatenate([jnp.where(first, 0.0, xp), xc], axis=0)
        cv = _causal_conv(xext, wv, DN_CONV, DN_HALO - (DN_CONV - 1), xc.shape[0])
        return list(_dn_point(cv, bav, av[...], dv[...], heads)), []

    return _rowwise(fn, n_rows=s, ts=ts, name="dn_pre_fwd", rows=[(qkvz, 0, d3), (ba, 0, LANES)],
                    prevs=[(qkvz, 0, d3, DN_HALO)], vecs=[wconv, alog, dt],
                    row_outs=[(d, F32), (d, F32), (d, F32), (LANES, F32)])


def _dn_pre_bwd(qkvz, ba, wconv, alog, dt, dq, dk, dv, dgb, dz, heads, ts=128):
    s = qkvz.shape[0]
    d3 = wconv.shape[1]
    d = d3 // 3
    lead = DN_HALO - (DN_CONV - 1)

    def fn(first, last, xc, bac, dqc, dkc, dvc, dgbc, dzc, xp, xn, ban, dqn, dkn, dvn, dgbn, wv, av, dtv):
        n = xc.shape[0]
        ext = lambda cur, nxt: jnp.concatenate([cur, nxt], axis=0)
        live = lambda nxt: jnp.where(last, 0.0, nxt)
        xall = jnp.concatenate([jnp.where(first, 0.0, xp), xc, live(xn)], axis=0)
        cv = _causal_conv(xall, wv, DN_CONV, lead, n + DN_HALO)
        (_, _, _, gbv), vjp = jax.vjp(lambda c, b: _dn_point(c, b, av[...], dtv[...], heads), cv, ext(bac, ban))
        dc, dba = vjp((ext(dqc, live(dqn)), ext(dkc, live(dkn)), ext(dvc, live(dvn)), ext(dgbc, live(dgbn))))
        dx = None
        dw = []
        for j in range(DN_CONV):
            term = _shift_rows(dc, DN_CONV - 1 - j)[:n] * wv[j:j + 1, :]
            dx = term if dx is None else dx + term
            dw.append(_colsum(dc[:n] * _shift_rows(xall, lead + j)[:n]))
        dba = dba[:n]
        return ([jnp.concatenate([dx, dzc], axis=-1), dba],
                [_stack_rows(dw, DN_CONV), _colsum(dgbc * gbv[:n]), _colsum(dba)])

    return _rowwise(fn, n_rows=s, ts=ts, name="dn_pre_bwd",
                    rows=[(qkvz, 0, d3), (ba, 0, LANES), (dq, 0, d), (dk, 0, d), (dv, 0, d), (dgb, 0, LANES),
                          (dz, 0, d)],
                    prevs=[(qkvz, 0, d3, DN_HALO)],
                    nexts=[(qkvz, 0, d3, DN_HALO), (ba, 0, LANES, DN_HALO), (dq, 0, d, DN_HALO),
                           (dk, 0, d, DN_HALO), (dv, 0, d, DN_HALO), (dgb, 0, LANES, DN_HALO)],
                    vecs=[wconv, alog, dt],
                    row_outs=[(4 * d, BF16), (LANES, BF16)], acc_outs=[(DN_CONV, d3), (1, LANES), (1, LANES)])


def _cv_mid_fwd(u, wdw, bdw, lng, lnb, ts=256):
    s = u.shape[0]
    d = u.shape[1] // 2

    def fn(first, last, uc, up, wv, bv, gv, lbv):
        uext = jnp.concatenate([jnp.where(first, 0.0, up), uc], axis=0)
        glu = uext[:, :d] * _sigmoid(uext[:, d:])
        c = _causal_conv(glu, wv, CV_WIDTH, CV_HALO - (CV_WIDTH - 1), uc.shape[0]) + bv[...]
        return [c, _ln_silu(c, gv[...], lbv[...])], []

    return _rowwise(fn, n_rows=s, ts=ts, name="cv_mid_fwd", rows=[(u, 0, 2 * d)], prevs=[(u, 0, 2 * d, CV_HALO)],
                    vecs=[wdw, bdw, lng, lnb], row_outs=[(d, F32), (d, BF16)])


def _cv_mid_bwd2(dc, u, wdw, ts=256):
    s, d = dc.shape

    def fn(first, last, dcc, uc, up, dcn, wv):
        n = dcc.shape[0]
        dcext = jnp.concatenate([dcc, jnp.where(last, 0.0, dcn)], axis=0)
        uext = jnp.concatenate([jnp.where(first, 0.0, up), uc], axis=0)
        glu = uext[:, :d] * _sigmoid(uext[:, d:])
        dglu = None
        dw = []
        for j in range(CV_WIDTH):
            term = _shift_rows(dcext, CV_WIDTH - 1 - j)[:n] * wv[j:j + 1, :]
            dglu = term if dglu is None else dglu + term
            dw.append(_colsum(dcc * _shift_rows(glu, CV_HALO - (CV_WIDTH - 1) + j)[:n]))
        u1, sg = uc[:, :d], _sigmoid(uc[:, d:])
        du = jnp.concatenate([dglu * sg, dglu * u1 * sg * (1.0 - sg)], axis=-1)
        return [du], [_stack_rows(dw, CV_HALO), _colsum(du)]

    return _rowwise(fn, n_rows=s, ts=ts, name="cv_mid_bwd2", rows=[(dc, 0, d), (u, 0, 2 * d)],
                    prevs=[(u, 0, 2 * d, CV_HALO)], nexts=[(dc, 0, d, CV_HALO)], vecs=[wdw],
                    row_outs=[(2 * d, BF16)], acc_outs=[(CV_HALO, d), (1, 2 * d)])


def _attn_fwd(q, k, v, name, ts=512):
    s, d = q.shape

    def fn(first, last, qv, kv, vv):
        return [_attn_tile(qv.astype(F32), kv[...].astype(F32), vv[...].astype(F32))], []

    return _rowwise(fn, n_rows=s, ts=ts, name=name, rows=[(q, 0, d)], vecs=[k, v], row_outs=[(d, BF16)])[0]


def _attn_bwd(q, k, v, do, name, ts=512):
    s, d = q.shape
    m = k.shape[0]

    def fn(first, last, qv, dov, kv, vv):
        _, vjp = jax.vjp(_attn_tile, qv.astype(F32), kv[...].astype(F32), vv[...].astype(F32))
        dq, dk, dv = vjp(dov.astype(F32))
        return [dq], [dk, dv]

    return _rowwise(fn, n_rows=s, ts=ts, name=name, rows=[(q, 0, d), (do, 0, d)], vecs=[k, v],
                    row_outs=[(d, BF16)], acc_outs=[(m, d), (m, d)])


def _pad_lanes(a, off=0):
    r, n = a.shape
    return jnp.pad(a, ((0, 0), (off, LANES - off - n)))


def _local_step(x, mem, tgt, w, fetch=None, emit=None, first_after=()):
    s, d = x.shape
    heads = d // DN_HEAD_DIM
    g = {}
    if fetch is None:
        fetch = lambda group, after: None
    if emit is None:
        emit = lambda group, grads: ()

    def add_res(acc, res):
        return (res + acc,)

    def add_res_rms(acc, res, gain):
        h = res + acc
        return h, _rms(h, gain)

    def rms_bwd_epi(acc, hx, dres, gain):
        _, vjp = jax.vjp(_rms, hx, gain)
        dx, dg = vjp(acc)
        return dres + dx, dg

    w_int = w["dn_w_in"][0]
    assert w_int.shape[0] == 4 * d + 2 * heads
    w_bat = jnp.pad(w_int[4 * d:], ((0, LANES - 2 * heads), (0, 0)))
    dn_norm = w["dn_norm"]
    alog = _pad_lanes(w["dn_a_log"], heads)
    dtb = _pad_lanes(w["dn_dt_bias"], heads)
    wconv = w["dn_w_conv"][0]
    n0 = _rms_fwd(x, dn_norm, "dn_rms", after=first_after)
    qkvz = _matmul(n0, w_int, "nt", [F32], name="dn_in_proj", b_rows=4 * d)
    ba = _matmul(n0, w_bat, "nt", [F32], name="dn_in_proj_ba")
    q, k, v, gb = _dn_pre_fwd(qkvz, ba, wconv, alog, dtb, heads)
    og, states, tms = _delta_fwd(q, k, v, gb, qkvz, w["dn_out_norm"], heads)
    fetch(1, [og])
    h1, nq0 = _matmul(og, w["dn_w_out"][0], "nn", [F32, BF16], name="dn_out_proj", epi=add_res_rms,
                      mn_extras=[x], row_extras=[w["xa_norm"][0:1]], slab=EPI_SLAB)

    def xattn_fwd(h, nq, layer, next_gain):
        qx = _matmul(nq, w["xa_w_q"][layer], "nn", [BF16], name=f"xa{layer}_q")
        mn = _rms_fwd(mem, w["xa_mem_norm"][layer:layer + 1], f"xa{layer}_mem_rms")
        kv = _matmul(mn, w["xa_w_kv"][layer], "nn", [BF16], name=f"xa{layer}_kv")
        kx, vx = kv[:, :d], kv[:, d:]
        ox = _attn_fwd(qx, kx, vx, f"xa{layer}_attn")
        hn, nn = _matmul(ox, w["xa_w_o"][layer], "nn", [F32, BF16], name=f"xa{layer}_o", epi=add_res_rms,
                         mn_extras=[h], row_extras=[next_gain], slab=EPI_SLAB)
        return hn, nn, (h, nq, qx, mn, kx, vx, ox)

    def sq_relu(t):
        r = jnp.maximum(t.astype(F32), 0.0)
        return r * r

    mlp_store_act = (False, False)

    def loss_epi(acc, res, target, gain):
        def cols(hh, gg):
            e = _rms(hh, gg) - target
            return _colsum(e * e) * (0.5 / d)

        per_col, vjp = jax.vjp(cols, res + acc, gain)
        dhx, dgain = vjp(jnp.ones_like(per_col))
        return dhx, dgain, per_col

    def mlp_fwd(h, nm, layer, next_gain):
        if mlp_store_act[layer]:
            u, a = _matmul(nm, w["mlp_w_up"][layer], "nn", [BF16, BF16], name=f"mlp{layer}_up",
                           epi=lambda acc: (acc, sq_relu(acc)))
            pre = None
        else:
            u = _matmul(nm, w["mlp_w_up"][layer], "nn", [BF16], name=f"mlp{layer}_up")
            a, pre = u, sq_relu
        if next_gain is None:
            hn, *nn = _matmul(a, w["mlp_w_down"][layer], "nn", [F32], name=f"mlp{layer}_down_loss",
                              epi=loss_epi, mn_extras=[h, tgt], row_extras=[w["final_norm"].reshape(1, d)],
                              n_rowsum=2, tk=2048, slab=EPI_SLAB, a_pre=pre)
        else:
            hn, nn = _matmul(a, w["mlp_w_down"][layer], "nn", [F32, BF16], name=f"mlp{layer}_down",
                             epi=add_res_rms, mn_extras=[h], row_extras=[next_gain], tk=2048, slab=EPI_SLAB,
                             a_pre=pre)
        return hn, nn, (h, nm, u, a, pre)

    h2, nm0, xa0 = xattn_fwd(h1, nq0, 0, w["mlp_norm"][0:1])
    fetch(2, [h2])
    h3, n1, mlp0 = mlp_fwd(h2, nm0, 0, w["cv_norm"])

    u_cv = _matmul(n1, w["cv_w_pw1"][0], "nn", [F32], name="cv_pw1", epi=lambda acc, b: (acc + b,),
                   row_extras=[w["cv_b_pw1"]])
    wdw = jnp.pad(w["cv_w_dw"][0], ((0, CV_HALO - CV_WIDTH), (0, 0)))
    c_cv, s_cv = _cv_mid_fwd(u_cv, wdw, w["cv_b_dw"], w["cv_ln_g"], w["cv_ln_b"])
    h4, nq1 = _matmul(s_cv, w["cv_w_pw2"][0], "nn", [F32, BF16], name="cv_pw2",
                      epi=lambda acc, res, b, gain: add_res_rms(acc + b, res, gain), mn_extras=[h3],
                      row_extras=[w["cv_b_pw2"], w["xa_norm"][1:2]], slab=EPI_SLAB)
    fetch(3, [h4])
    h5, nm1, xa1 = xattn_fwd(h4, nq1, 1, w["mlp_norm"][1:2])
    dh, (g_fn, loss_cols), mlp1 = mlp_fwd(h5, nm1, 1, None)
    g["final_norm"] = g_fn.reshape(d)
    loss = jnp.sum(loss_cols, axis=1, keepdims=True)

    def mlp_bwd(dh, layer, saved, after=()):
        h, nm, u, a, pre = saved
        du = _matmul(dh, w["mlp_w_down"][layer], "nt", [BF16], name=f"mlp{layer}_down_dx", after=after,
                     epi=lambda acc, uu: (acc * 2.0 * jnp.maximum(uu.astype(F32), 0.0),), mn_extras=[u])
        gdown = _matmul(a, dh, "tn", [BF16], name=f"mlp{layer}_down_dw", tm=2048, a_pre=pre)
        dhn, gn = _matmul(du, w["mlp_w_up"][layer], "nt", [F32], name=f"mlp{layer}_up_dx", epi=rms_bwd_epi,
                          mn_extras=[h, dh], row_extras=[w["mlp_norm"][layer:layer + 1]], n_rowsum=1,
                          slab=EPI_SLAB)
        gup = _matmul(nm, du, "tn", [BF16], name=f"mlp{layer}_up_dw", out_dm=True, tk=2048)
        return dhn, gup, gdown, gn

    def xattn_bwd(dh, layer, saved):
        h, nq, qx, mn, kx, vx, ox = saved
        dox = _matmul(dh, w["xa_w_o"][layer], "nt", [BF16], name=f"xa{layer}_o_dx")
        go = _matmul(ox, dh, "tn", [BF16], name=f"xa{layer}_o_dw", tk=2048)
        dqx, dkx, dvx = _attn_bwd(qx, kx, vx, dox, f"xa{layer}_attn_bwd")

        def epi(acc, hx, dres, gain):
            dhx, dg = rms_bwd_epi(acc, hx, dres, gain)
            return dhx, dg, _colsum(dhx)

        dhn, gn, dh_cols = _matmul(dqx, w["xa_w_q"][layer], "nt", [F32], name=f"xa{layer}_q_dx", epi=epi,
                                   mn_extras=[h, dh], row_extras=[w["xa_norm"][layer:layer + 1]], n_rowsum=2,
                                   slab=EPI_SLAB)
        gq = _matmul(nq, dqx, "tn", [BF16], name=f"xa{layer}_q_dw", tk=2048)
        dkv = jnp.concatenate([dkx, dvx], axis=-1)
        gkv = _matmul(mn, dkv, "tn", [BF16], name=f"xa{layer}_kv_dw", out_dm=True)
        dmn = _matmul(dkv, w["xa_w_kv"][layer], "nt", [F32], name=f"xa{layer}_kv_dx", tk=2048)
        _, gmem = _rms_bwd(mem, dmn, dmn, w["xa_mem_norm"][layer:layer + 1], f"xa{layer}_mem_rms_bwd")
        return dhn, gq, gkv, go, gn, gmem, dh_cols

    dh, gup1, gdown1, gmn1 = mlp_bwd(dh, 1, mlp1)
    dh, gq1, gkv1, go1, gxn1, gmem1, g_b2 = xattn_bwd(dh, 1, xa1)
    g.update(mlp_w_up=[None, gup1], mlp_w_down=[None, gdown1], xa_w_q=[None, gq1], xa_w_kv=[None, gkv1],
             xa_w_o=[None, go1])
    tok = emit(3, g)

    def ln_bwd_epi(acc, cx, gain, bias):
        _, vjp = jax.vjp(_ln_silu, cx, gain, bias)
        dc, dg, db = vjp(acc)
        return dc, dg, db, _colsum(dc)

    dc_cv, g_lng, g_lnb, g_bdw = _matmul(dh, w["cv_w_pw2"][0], "nt", [F32], name="cv_pw2_dx", after=tok,
                                        epi=ln_bwd_epi, mn_extras=[c_cv],
                                        row_extras=[w["cv_ln_g"], w["cv_ln_b"]], n_rowsum=3, slab=EPI_SLAB)
    g["cv_w_pw2"] = [_matmul(s_cv, dh, "tn", [BF16], name="cv_pw2_dw", tk=2048)]
    du_cv, g_wdw, g_b1 = _cv_mid_bwd2(dc_cv, u_cv, wdw)
    g["cv_w_pw1"] = [_matmul(n1, du_cv, "tn", [BF16], name="cv_pw1_dw", out_dm=True, tk=2048)]
    dh, g_cvn = _matmul(du_cv, w["cv_w_pw1"][0], "nt", [F32], name="cv_pw1_dx", epi=rms_bwd_epi,
                        mn_extras=[h3, dh], row_extras=[w["cv_norm"]], n_rowsum=1, slab=EPI_SLAB)
    g.update(cv_ln_g=g_lng, cv_ln_b=g_lnb, cv_b_dw=g_bdw, cv_b_pw2=g_b2, cv_b_pw1=g_b1, cv_norm=g_cvn,
             cv_w_dw=g_wdw[:CV_WIDTH][None])

    tok = emit(2, g)
    dh, gup0, gdown0, gmn0 = mlp_bwd(dh, 0, mlp0, after=tok)
    dh, gq0, gkv0, go0, gxn0, gmem0, _ = xattn_bwd(dh, 0, xa0)
    g["mlp_w_up"][0] = gup0
    g["mlp_w_down"][0] = gdown0
    g["mlp_norm"] = jnp.concatenate([gmn0, gmn1], axis=0)
    g["xa_w_q"][0] = gq0
    g["xa_w_kv"][0] = gkv0
    g["xa_w_o"][0] = go0
    g["xa_norm"] = jnp.concatenate([gxn0, gxn1], axis=0)
    g["xa_mem_norm"] = jnp.concatenate([gmem0, gmem1], axis=0)
    tok = emit(1, g)

    dog = _matmul(dh, w["dn_w_out"][0], "nt", [BF16], name="dn_out_proj_dx", after=tok)
    g["dn_w_out"] = [_matmul(og, dh, "tn", [BF16], name="dn_out_proj_dw", tk=2048)]
    dq, dk, dv, dgb, dz, g_on = _delta_bwd(q, k, v, gb, qkvz, w["dn_out_norm"], states, tms, dog, heads)
    dqkvz, dba, g_wconv, g_alog, g_dt = _dn_pre_bwd(qkvz, ba, wconv, alog, dtb, dq, dk, dv, dgb, dz, heads)
    g_qkvzt = _matmul(dqkvz, n0, "tn", [BF16], name="dn_in_proj_dw", tk=2048)
    g_bat = _matmul(dba, n0, "tn", [BF16], name="dn_in_proj_ba_dw", tk=2048)
    g["dn_w_in"] = [jnp.concatenate([g_qkvzt, g_bat[:2 * heads]], axis=0)]
    g["dn_w_conv"] = g_wconv[None]
    tok = emit(0, g)
    dn0a = _matmul(dba, w_bat, "nn", [F32], name="dn_in_proj_ba_dx", after=tok)
    grad_x, g_dnn = _matmul(dqkvz, w_int, "nn", [F32], name="dn_in_proj_dx", b_rows=4 * d,
                            epi=lambda acc, part, hx, dres, gain: rms_bwd_epi(acc + part, hx, dres, gain),
                            mn_extras=[dn0a, x, dh], row_extras=[dn_norm], n_rowsum=1, slab=EPI_SLAB)
    g.update(dn_norm=g_dnn, dn_out_norm=g_on,
             dn_a_log=g_alog[:, heads:2 * heads], dn_dt_bias=g_dt[:, heads:2 * heads])
    return loss, grad_x, g


def _round_up(n, m):
    return (n + m - 1) // m * m


def _pack_rows(parts, cols, row_mult):
    lead = parts[0].shape[:-1]
    flat, offs, off = [], [], 0
    for p in parts:
        n = _round_up(p.shape[-1], cols)
        flat.append(jnp.pad(p, [(0, 0)] * len(lead) + [(0, n - p.shape[-1])]))
        offs.append(off)
        off += n
    total = _round_up(off, cols * row_mult)
    if total > off:
        flat.append(jnp.zeros(lead + (total - off,), parts[0].dtype))
    return jnp.concatenate(flat, axis=-1).reshape(lead + (total // cols, cols)), offs


def _unpack(packed, offs, shapes):
    lead = packed.shape[:-2]
    flat = packed.reshape(lead + (-1,))
    out = []
    for off, shp in zip(offs, shapes):
        n = 1
        for v in shp:
            n *= v
        out.append(flat[..., off:off + n].reshape(lead + tuple(shp)))
    return out


def kernel(x, mem, dn_norm, dn_w_in, dn_w_conv, dn_a_log, dn_dt_bias, dn_out_norm, dn_w_out, cv_norm, cv_w_pw1, cv_b_pw1, cv_w_dw, cv_b_dw, cv_ln_g, cv_ln_b, cv_w_pw2, cv_b_pw2, xa_norm, xa_mem_norm, xa_w_q, xa_w_kv, xa_w_o, mlp_norm, mlp_w_up, mlp_w_down, final_norm, loss_target, m_dn_norm, m_dn_w_in, m_dn_w_conv, m_dn_a_log, m_dn_dt_bias, m_dn_out_norm, m_dn_w_out, m_cv_norm, m_cv_w_pw1, m_cv_b_pw1, m_cv_w_dw, m_cv_b_dw, m_cv_ln_g, m_cv_ln_b, m_cv_w_pw2, m_cv_b_pw2, m_xa_norm, m_xa_mem_norm, m_xa_w_q, m_xa_w_kv, m_xa_w_o, m_mlp_norm, m_mlp_w_up, m_mlp_w_down, m_final_norm, v_dn_norm, v_dn_w_in, v_dn_w_conv, v_dn_a_log, v_dn_dt_bias, v_dn_out_norm, v_dn_w_out, v_cv_norm, v_cv_w_pw1, v_cv_b_pw1, v_cv_w_dw, v_cv_b_dw, v_cv_ln_g, v_cv_ln_b, v_cv_w_pw2, v_cv_b_pw2, v_xa_norm, v_xa_mem_norm, v_xa_w_q, v_xa_w_kv, v_xa_w_o, v_mlp_norm, v_mlp_w_up, v_mlp_w_down, v_final_norm):
    wsh = dict(dn_norm=dn_norm, dn_w_in=dn_w_in, dn_w_conv=dn_w_conv, dn_a_log=dn_a_log, dn_dt_bias=dn_dt_bias, dn_out_norm=dn_out_norm, dn_w_out=dn_w_out, cv_norm=cv_norm, cv_w_pw1=cv_w_pw1, cv_b_pw1=cv_b_pw1, cv_w_dw=cv_w_dw, cv_b_dw=cv_b_dw, cv_ln_g=cv_ln_g, cv_ln_b=cv_ln_b, cv_w_pw2=cv_w_pw2, cv_b_pw2=cv_b_pw2, xa_norm=xa_norm, xa_mem_norm=xa_mem_norm, xa_w_q=xa_w_q, xa_w_kv=xa_w_kv, xa_w_o=xa_w_o, mlp_norm=mlp_norm, mlp_w_up=mlp_w_up, mlp_w_down=mlp_w_down, final_norm=final_norm)
    msh = dict(dn_norm=m_dn_norm, dn_w_in=m_dn_w_in, dn_w_conv=m_dn_w_conv, dn_a_log=m_dn_a_log, dn_dt_bias=m_dn_dt_bias, dn_out_norm=m_dn_out_norm, dn_w_out=m_dn_w_out, cv_norm=m_cv_norm, cv_w_pw1=m_cv_w_pw1, cv_b_pw1=m_cv_b_pw1, cv_w_dw=m_cv_w_dw, cv_b_dw=m_cv_b_dw, cv_ln_g=m_cv_ln_g, cv_ln_b=m_cv_ln_b, cv_w_pw2=m_cv_w_pw2, cv_b_pw2=m_cv_b_pw2, xa_norm=m_xa_norm, xa_mem_norm=m_xa_mem_norm, xa_w_q=m_xa_w_q, xa_w_kv=m_xa_w_kv, xa_w_o=m_xa_w_o, mlp_norm=m_mlp_norm, mlp_w_up=m_mlp_w_up, mlp_w_down=m_mlp_w_down, final_norm=m_final_norm)
    vsh = dict(dn_norm=v_dn_norm, dn_w_in=v_dn_w_in, dn_w_conv=v_dn_w_conv, dn_a_log=v_dn_a_log, dn_dt_bias=v_dn_dt_bias, dn_out_norm=v_dn_out_norm, dn_w_out=v_dn_w_out, cv_norm=v_cv_norm, cv_w_pw1=v_cv_w_pw1, cv_b_pw1=v_cv_b_pw1, cv_w_dw=v_cv_w_dw, cv_b_dw=v_cv_b_dw, cv_ln_g=v_cv_ln_g, cv_ln_b=v_cv_ln_b, cv_w_pw2=v_cv_w_pw2, cv_b_pw2=v_cv_b_pw2, xa_norm=v_xa_norm, xa_mem_norm=v_xa_mem_norm, xa_w_q=v_xa_w_q, xa_w_kv=v_xa_w_kv, xa_w_o=v_xa_w_o, mlp_norm=v_mlp_norm, mlp_w_up=v_mlp_w_up, mlp_w_down=v_mlp_w_down, final_norm=v_final_norm)

    big_axis = dict(BIG)
    for src in (wsh, msh, vsh):
        src["dn_w_in"] = jnp.swapaxes(src["dn_w_in"], 1, 2)
    big_axis["dn_w_in"] = 1

    small_pack, small_offs = _pack_rows([wsh[nm].reshape(-1) for nm in SMALL_SH], LANES, 8)
    w = {nm: [None] * wsh[nm].shape[0] for nm in big_axis}

    def put_weights(group, gathered):
        for (nm, layer), gth in zip(group, gathered):
            if big_axis[nm] == 1:
                w[nm][layer] = gth.reshape(N_DEV * gth.shape[1], gth.shape[2])
            else:
                w[nm][layer] = gth

    first = _all_gather([wsh[nm][layer].astype(BF16) for nm, layer in GATHER_GROUPS[0]] + [small_pack],
                        "weights_all_gather_0")
    put_weights(GATHER_GROUPS[0], first)
    me = _dev_index(*_mesh_me())
    gather_handles, tokens = {}, []
    for gi in range(1, len(GATHER_GROUPS)):
        shards = [wsh[nm][layer].astype(BF16) for nm, layer in GATHER_GROUPS[gi]]
        lands = [lax.dynamic_update_slice(lax.empty((N_DEV,) + s.shape, s.dtype), s[None], (me, 0, 0))
                 for s in shards]
        gather_handles[gi], tok = _exchange_start(shards, lands, False, f"weights_gather_{gi}",
                                                  after=[first[-1]] + tokens)
        tokens.append(tok)
    for nm, gth in zip(SMALL_SH, _unpack(first[-1], small_offs, [wsh[nm].shape for nm in SMALL_SH])):
        w[nm] = jnp.moveaxis(gth, 0, -2).reshape(gth.shape[1:-1] + (N_DEV * gth.shape[-1],))
    for nm in REPL:
        w[nm] = wsh[nm]

    def fetch(gi, after):
        put_weights(GATHER_GROUPS[gi], _exchange_wait(gather_handles[gi], after)[1])

    scatter_handles = {}

    def emit(gi, g):
        blocks = []
        for nm, layer in SCATTER_GROUPS[gi]:
            gw = g[nm][layer]
            if big_axis[nm] == 1:
                gw = gw.reshape(N_DEV, gw.shape[0] // N_DEV, gw.shape[1])
            blocks.append(gw)
        if gi == 0:
            gsmall_pack, _ = _pack_rows(
                [jnp.moveaxis(g[nm].reshape(g[nm].shape[:-1] + (N_DEV, -1)), -2, 0).reshape(N_DEV, -1)
                 for nm in SMALL_SH], LANES, 8)
            blocks.append(gsmall_pack)
        lands = [lax.empty(b.shape, b.dtype) for b in blocks]
        scatter_handles[gi], tok = _exchange_start(blocks, lands, True, f"grads_scatter_{gi}")
        return [tok]

    loss_part, grad_x, g = _local_step(x[0], mem[0], loss_target[0], w, fetch, emit, tokens)

    recv = {nm: [None] * wsh[nm].shape[0] for nm in big_axis}
    sent = {nm: [None] * wsh[nm].shape[0] for nm in big_axis}
    gsh, delta, new_m, new_v = {}, {}, {}, {}
    after = [grad_x]
    done = set()
    me_arr = me.astype(jnp.int32).reshape(1)
    def small_adamw(names, name):
        packs = []
        for src in (wsh, gsh, msh, vsh):
            pk, offs = _pack_rows([src[nm].reshape(-1) for nm in names], LANES, 8)
            packs.append(pk)
        outs = _adamw(*packs, name)
        for dst, pk in zip((delta, new_m, new_v), outs):
            for nm, val in zip(names, _unpack(pk, offs, [wsh[nm].shape for nm in names])):
                dst[nm] = val
        return outs[0]

    for gi in reversed(range(len(SCATTER_GROUPS))):
        if gi == 0:
            repl_pack, repl_offs = _pack_rows([g[nm].reshape(-1) for nm in REPL] + [loss_part[:, :1].reshape(-1)],
                                              LANES, 8)
            (repl_all,) = _all_gather([repl_pack], "repl_grads_all_gather")
            repl_red = _slot_sum(repl_all, "repl_grads_sum", 512)
            *repl_vals, loss_sum = _unpack(repl_red, repl_offs, [wsh[nm].shape for nm in REPL] + [(1,)])
            for nm, val in zip(REPL, repl_vals):
                gsh[nm] = val
            after = after + [small_adamw(list(REPL), "adamw_repl")]
        sources, landed = _exchange_wait(scatter_handles[gi], after)
        for (nm, layer), src, r in zip(SCATTER_GROUPS[gi], sources, landed):
            sent[nm][layer], recv[nm][layer] = src, r
        if gi == 0:
            slot = lax.broadcasted_iota(jnp.int32, landed[-1].shape, 0)
            rsmall = jnp.where(slot == me, sources[-1], landed[-1])
        for nm in big_axis:
            if nm not in done and all(r is not None for r in recv[nm]):
                gsh[nm], delta[nm], new_m[nm], new_v[nm] = _adamw_reduce(
                    me_arr, recv[nm], sent[nm], wsh[nm], msh[nm], vsh[nm], f"adamw_{nm}")
                done.add(nm)
                after = [delta[nm]]
    gsmall_red = _slot_sum(rsmall, "grads_small_sum", 512)
    for nm, val in zip(SMALL_SH, _unpack(gsmall_red, small_offs, [wsh[nm].shape for nm in SMALL_SH])):
        gsh[nm] = val
    small_adamw(list(SMALL_SH), "adamw_small")
    for dst in (gsh, delta, new_m, new_v):
        dst["dn_w_in"] = jnp.swapaxes(dst["dn_w_in"], 1, 2)
    return (loss_sum.reshape(()), grad_x[None], *[gsh[nm] for nm in WEIGHTS], *[delta[nm] for nm in WEIGHTS],
            *[new_m[nm] for nm in WEIGHTS], *[new_v[nm] for nm in WEIGHTS])
```

```python
import functools

import jax
import jax.numpy as jnp
from jax import lax
from jax.experimental import pallas as pl
from jax.experimental.pallas import tpu as pltpu

F32 = jnp.float32
BF16 = jnp.bfloat16
MESH_IDS = pl.DeviceIdType.MESH

N_DEV = 8
LANES = 128
RMS_EPS = 1e-6
LN_EPS = 1e-5
DN_HEAD_DIM = 128
DN_CONV = 4
DN_CHUNK = 64
CV_WIDTH = 31
XA_HEADS = 4
EPI_SLAB = 256
DN_HALO = 8
CV_HALO = 32

ADAM_LR = 0.001
ADAM_B1 = 0.9
ADAM_B2 = 0.999
ADAM_EPS = 1e-08
ADAM_WD = 0.01
ADAM_STEP = 10

BIG = (("dn_w_in", 2), ("dn_w_out", 1), ("cv_w_pw1", 2), ("cv_w_pw2", 1), ("xa_w_q", 1), ("xa_w_kv", 2),
       ("xa_w_o", 1), ("mlp_w_up", 2), ("mlp_w_down", 1))
_LAYER_GROUP = ("xa_w_q", "xa_w_o", "mlp_w_down", "xa_w_kv", "mlp_w_up")
GATHER_GROUPS = (
    (("dn_w_in", 0),),
    (("dn_w_out", 0),) + tuple((nm, 0) for nm in _LAYER_GROUP),
    (("cv_w_pw2", 0), ("cv_w_pw1", 0)),
    tuple((nm, 1) for nm in _LAYER_GROUP),
)
SCATTER_GROUPS = (
    (("dn_w_out", 0), ("dn_w_in", 0)),
    tuple((nm, 0) for nm in _LAYER_GROUP),
    (("cv_w_pw2", 0), ("cv_w_pw1", 0)),
    tuple((nm, 1) for nm in _LAYER_GROUP),
)
SMALL_SH = ("cv_norm", "cv_b_pw1", "cv_b_dw", "cv_ln_g", "cv_ln_b", "cv_b_pw2", "cv_w_dw", "dn_w_conv")
REPL = ("dn_norm", "dn_a_log", "dn_dt_bias", "dn_out_norm", "xa_norm", "xa_mem_norm", "mlp_norm", "final_norm")
WEIGHTS = ("dn_norm", "dn_w_in", "dn_w_conv", "dn_a_log", "dn_dt_bias", "dn_out_norm", "dn_w_out", "cv_norm",
           "cv_w_pw1", "cv_b_pw1", "cv_w_dw", "cv_b_dw", "cv_ln_g", "cv_ln_b", "cv_w_pw2", "cv_b_pw2", "xa_norm",
           "xa_mem_norm", "xa_w_q", "xa_w_kv", "xa_w_o", "mlp_norm", "mlp_w_up", "mlp_w_down", "final_norm")


def _dot_dims(mode, batched):
    o = 1 if batched else 0
    contract = {"nn": ((1 + o,), (o,)), "nt": ((1 + o,), (1 + o,)), "tn": ((o,), (o,))}[mode]
    return (contract, (((0,), (0,)) if batched else ((), ())))


def _bdot(a, b, mode):
    return lax.dot_general(a.astype(BF16), b.astype(BF16), _dot_dims(mode, a.ndim == 3),
                           preferred_element_type=F32)


@functools.partial(jax.custom_vjp, nondiff_argnums=(2,))
def _mm(a, b, mode):
    return _bdot(a, b, mode)


def _mm_fwd(a, b, mode):
    return _bdot(a, b, mode), (a, b)


def _mm_bwd(mode, res, ct):
    a, b = res
    if mode == "nn":
        da, db = _bdot(ct, b, "nt"), _bdot(a, ct, "tn")
    elif mode == "nt":
        da, db = _bdot(ct, b, "nn"), _bdot(ct, a, "tn")
    else:
        da, db = _bdot(b, ct, "nt"), _bdot(a, ct, "nn")
    return da.astype(a.dtype), db.astype(b.dtype)


_mm.defvjp(_mm_fwd, _mm_bwd)


def _sigmoid(x):
    return 0.5 * (jnp.tanh(0.5 * x) + 1.0)


def _silu(x):
    return x * _sigmoid(x)


def _softplus(x):
    return jnp.maximum(x, 0.0) + jnp.log(1.0 + jnp.exp(-jnp.abs(x)))


def _rms(x, g):
    r = lax.rsqrt(jnp.mean(x * x, axis=-1, keepdims=True) + RMS_EPS)
    return x * r * g


def _shift_rows(x, off):
    if off == 0:
        return x
    return pltpu.roll(x, x.shape[0] - off, 0)


def _series_dot(a, b, mode):
    return _bdot(a, b, mode)


def _chunk_masks(c):
    ii = lax.broadcasted_iota(jnp.int32, (c, c), 0)
    jj = lax.broadcasted_iota(jnp.int32, (c, c), 1)
    return (ii == jj).astype(F32), ii >= jj, ii > jj


def _neumann_inverse(lm):
    n = lm.shape[-1]
    t = -lm
    p = lm
    size = 2
    while size < n:
        size *= 2
        p = _series_dot(p, p, "nn")
        t = t + p + _series_dot(t, p, "nn")
    return t


def _apply_inverse(tm, rhs, mode):
    return rhs + _series_dot(tm, rhs, mode)


@jax.custom_vjp
def _unit_lower_solve(lm, rhs, tm):
    return _apply_inverse(tm, rhs, "nn")


def _uls_fwd(lm, rhs, tm):
    sol = _apply_inverse(tm, rhs, "nn")
    return sol, (tm, sol)


def _uls_bwd(res, ct):
    tm, sol = res
    d_rhs = _apply_inverse(tm, ct, "tn")
    return -_bdot(d_rhs, sol, "nt"), d_rhs, jnp.zeros_like(tm)


_unit_lower_solve.defvjp(_uls_fwd, _uls_bwd)


def _delta_chunk(q, k, v, gcol, bcol, s0, tm=None):
    c = q.shape[1]
    eye, causal, strict = _chunk_masks(c)
    grow = jnp.sum(eye * gcol, axis=1, keepdims=True)
    gc = jnp.sum(jnp.where(causal, grow, 0.0), axis=2, keepdims=True)
    gc_row = jnp.sum(eye * gc, axis=1, keepdims=True)
    decay = jnp.exp(jnp.where(causal, gc - gc_row, -jnp.inf))
    kb = k * bcol
    on_k = _mm(jnp.concatenate([kb, q], axis=1), k, "nt")
    lm = jnp.where(strict, on_k[:, :c] * decay, 0.0)
    attn = on_k[:, c:] * decay
    if tm is None:
        tm = _neumann_inverse(lax.stop_gradient(lm))
    egc = jnp.exp(gc)
    rhs = jnp.concatenate([v * bcol, kb * egc], axis=-1)
    sol = _unit_lower_solve(lm, rhs, tm)
    dv_ = v.shape[-1]
    u, w = sol[..., :dv_], sol[..., dv_:]
    gl = jnp.sum(grow, axis=2, keepdims=True)
    kd = k * jnp.exp(gl - gc)
    on_s = _mm(jnp.concatenate([w, q * egc], axis=1), s0, "nn")
    v_new = u - on_s[:, :c]
    o = on_s[:, c:] + _mm(attn, v_new, "nn")
    s1 = s0 * jnp.exp(gl) + _mm(kd, v_new, "tn")
    return o, s1, tm


def _dn_point(cv, ba, alog, dt, heads):
    a = _silu(cv)
    d = cv.shape[1] // 3
    qs, ks = [], []
    for h in range(heads):
        qh = a[:, h * DN_HEAD_DIM:(h + 1) * DN_HEAD_DIM]
        qs.append(qh * lax.rsqrt(jnp.sum(qh * qh, axis=-1, keepdims=True) + 1e-6) * (DN_HEAD_DIM ** -0.5))
        kh = a[:, d + h * DN_HEAD_DIM:d + (h + 1) * DN_HEAD_DIM]
        ks.append(kh * lax.rsqrt(jnp.sum(kh * kh, axis=-1, keepdims=True) + 1e-6))
    q = jnp.concatenate(qs, axis=-1)
    k = jnp.concatenate(ks, axis=-1)
    v = a[:, 2 * d:]
    lane = lax.broadcasted_iota(jnp.int32, ba.shape, 1)
    beta = _sigmoid(ba)
    g = -jnp.exp(alog) * _softplus(ba + dt)
    gb = jnp.where(lane < heads, beta, jnp.where(lane < 2 * heads, g, 0.0))
    return q, k, v, gb


def _attn_tile(q, k, v):
    hd = q.shape[1] // XA_HEADS
    outs = []
    for h in range(XA_HEADS):
        sl = slice(h * hd, (h + 1) * hd)
        s = _mm(q[:, sl], k[:, sl], "nt") * (hd ** -0.5)
        m = lax.stop_gradient(jnp.max(s, axis=-1, keepdims=True))
        e = jnp.exp(s - m)
        p = e / jnp.sum(e, axis=-1, keepdims=True)
        outs.append(_mm(p, v[:, sl], "nn"))
    return jnp.concatenate(outs, axis=-1)


def _ln_silu(c, g, b):
    mu = jnp.mean(c, axis=-1, keepdims=True)
    xc = c - mu
    y = xc * lax.rsqrt(jnp.mean(xc * xc, axis=-1, keepdims=True) + LN_EPS)
    return _silu(y * g + b)


def _causal_conv(xext, w, width, lead, ts):
    acc = None
    for j in range(width):
        term = _shift_rows(xext, lead + j)[:ts] * w[j:j + 1, :]
        acc = term if acc is None else acc + term
    return acc


def _colsum(x):
    return jnp.sum(x, axis=0, keepdims=True)


def _stack_rows(rows, n_rows):
    c = rows[0].shape[1]
    ridx = lax.broadcasted_iota(jnp.int32, (n_rows, c), 0)
    out = jnp.zeros((n_rows, c), F32)
    for j, r in enumerate(rows):
        out = out + jnp.where(ridx == j, r, 0.0)
    return out


def _matmul(a, b, mode, out_dtypes, *, name, epi=None, mn_extras=(), row_extras=(), out_dm=False, after=(),
            n_rowsum=0, slab=0, b_rows=None, a_pre=None, tm=1024, tn=1024, tk=1024):
    b_dm = b.ndim == 3
    b_shape = (b.shape[1], N_DEV * b.shape[2]) if b_dm else b.shape
    if b_rows is not None:
        assert not b_dm and b_rows <= b.shape[0]
        b_shape = (b_rows, b.shape[1])
    if mode == "nn":
        (m, k), (k2, n) = a.shape, b_shape
    elif mode == "nt":
        (m, k), (n, k2) = a.shape, b_shape
    else:
        (k, m), (k2, n) = a.shape, b_shape
    assert k == k2, (a.shape, b.shape, mode)
    tm, tn, tk = min(tm, m), min(tn, n), min(tk, k)
    cb, nb = 0, 1
    if b_dm:
        assert mode in ("nn", "nt")
        cb = b.shape[2]
        nb = max(1, (tn if mode == "nn" else tk) // cb)
        if mode == "nn":
            tn = nb * cb
        else:
            tk = nb * cb
    co, no = 0, 1
    if out_dm:
        co = n // N_DEV
        no = max(1, tn // co)
        tn = no * co
    assert m % tm == 0 and n % tn == 0 and k % tk == 0, (m, n, k, tm, tn, tk)
    nk = k // tk
    if mode == "tn":
        a_spec = pl.BlockSpec((tk, tm), lambda j, i, kk: (kk, i))
    else:
        a_spec = pl.BlockSpec((tm, tk), lambda j, i, kk: (i, kk))
    if b_dm:
        b_spec = (pl.BlockSpec((nb, tn, cb), lambda j, i, kk: (kk, j, 0)) if mode == "nt"
                  else pl.BlockSpec((nb, tk, cb), lambda j, i, kk: (j, kk, 0)))
    else:
        b_spec = (pl.BlockSpec((tn, tk), lambda j, i, kk: (j, kk)) if mode == "nt"
                  else pl.BlockSpec((tk, tn), lambda j, i, kk: (kk, j)))
    mn_spec = pl.BlockSpec((tm, tn), lambda j, i, kk: (i, j))
    row_spec = pl.BlockSpec((1, tn), lambda j, i, kk: (0, j))
    n_extra = len(mn_extras) + len(row_extras)
    n_out = len(out_dtypes)
    in_specs = ([a_spec, b_spec] + [mn_spec] * len(mn_extras) + [row_spec] * len(row_extras)
                + [_ANY_SPEC] * len(after))
    args = [a, b, *mn_extras, *row_extras, *after]
    if out_dm:
        out_specs = [pl.BlockSpec((no, tm, co), lambda j, i, kk: (j, i, 0))] * n_out
        out_shape = [jax.ShapeDtypeStruct((N_DEV, m, co), dt) for dt in out_dtypes]
    else:
        out_specs = [mn_spec] * n_out
        out_shape = [jax.ShapeDtypeStruct((m, n), dt) for dt in out_dtypes]
    out_specs = out_specs + [row_spec] * n_rowsum
    out_shape = out_shape + [jax.ShapeDtypeStruct((1, n), F32)] * n_rowsum
    n_in = len(args)
    n_mn = len(mn_extras)
    step = min(slab, tm) if slab else tm
    assert tm % step == 0

    def dot(a_ref, b_ref):
        a_val = a_ref[...] if a_pre is None else a_pre(a_ref[...])
        if not b_dm:
            return _bdot(a_val, b_ref[...], mode)
        if mode == "nn":
            parts = [_bdot(a_val, b_ref[dd], "nn") for dd in range(nb)]
            return parts[0] if nb == 1 else jnp.concatenate(parts, axis=1)
        out = None
        for dd in range(nb):
            part = _bdot(a_val[:, dd * cb:(dd + 1) * cb], b_ref[dd], "nt")
            out = part if out is None else out + part
        return out

    def finish(acc_src, extras, outs):
        sums = [None] * n_rowsum
        for r0 in range(0, tm, step):
            rs = slice(r0, r0 + step)
            acc_val = acc_src[rs, :]
            if epi is None:
                vals = (acc_val,)
            else:
                vals = epi(acc_val, *[e[rs, :] for e in extras[:n_mn]], *[e[...] for e in extras[n_mn:]])
            for o_ref, val in zip(outs[:n_out], vals[:n_out]):
                if out_dm:
                    for dd in range(no):
                        o_ref[dd, rs, :] = val[:, dd * co:(dd + 1) * co].astype(o_ref.dtype)
                else:
                    o_ref[rs, :] = val.astype(o_ref.dtype)
            for q in range(n_rowsum):
                sums[q] = vals[n_out + q] if sums[q] is None else sums[q] + vals[n_out + q]
        for q in range(n_rowsum):
            s_ref = outs[n_out + q]

            @pl.when(pl.program_id(1) == 0)
            def _():
                s_ref[...] = sums[q]

            @pl.when(pl.program_id(1) > 0)
            def _():
                s_ref[...] += sums[q]

    def body_one_step(*refs):
        finish(dot(refs[0], refs[1]), refs[2:2 + n_extra], refs[n_in:])

    def body(*refs):
        a_ref, b_ref = refs[0], refs[1]
        acc = refs[-1]
        kk = pl.program_id(2)

        @pl.when(kk == 0)
        def _():
            acc[...] = jnp.zeros_like(acc)

        acc[...] += dot(a_ref, b_ref)

        @pl.when(kk == nk - 1)
        def _():
            finish(acc, refs[2:2 + n_extra], refs[n_in:-1])

    res = pl.pallas_call(
        body_one_step if nk == 1 else body, name=name,
        grid=(n // tn, m // tm, nk),
        in_specs=in_specs, out_specs=out_specs, out_shape=out_shape,
        scratch_shapes=[] if nk == 1 else [pltpu.VMEM((tm, tn), F32)],
        compiler_params=pltpu.CompilerParams(
            dimension_semantics=("parallel", "arbitrary" if n_rowsum else "parallel", "arbitrary")),
    )(*args)
    return res[0] if n_out + n_rowsum == 1 else res


def _rowwise(fn, *, n_rows, ts, name, rows=(), prevs=(), nexts=(), vecs=(), row_outs=(), acc_outs=(), after=()):
    ts = min(ts, n_rows)
    assert n_rows % ts == 0
    nblk = n_rows // ts
    in_specs, args = [], []
    for arr, cb, w in rows:
        in_specs.append(pl.BlockSpec((ts, w), functools.partial(lambda i, cb: (i, cb), cb=cb)))
        args.append(arr)
    for arr, cb, w, halo in prevs:
        per = ts // halo
        in_specs.append(pl.BlockSpec(
            (halo, w), functools.partial(lambda i, cb, per: (jnp.maximum(i * per - 1, 0), cb), cb=cb, per=per)))
        args.append(arr)
    for arr, cb, w, halo in nexts:
        per = ts // halo
        last_blk = n_rows // halo - 1
        in_specs.append(pl.BlockSpec(
            (halo, w), functools.partial(lambda i, cb, per, lb: (jnp.minimum((i + 1) * per, lb), cb),
                                         cb=cb, per=per, lb=last_blk)))
        args.append(arr)
    for arr in vecs:
        in_specs.append(pl.BlockSpec(arr.shape, functools.partial(lambda i, nd: (0,) * nd, nd=arr.ndim)))
        args.append(arr)
    out_specs, out_shape = [], []
    for w, dt in row_outs:
        out_specs.append(pl.BlockSpec((ts, w), lambda i: (i, 0)))
        out_shape.append(jax.ShapeDtypeStruct((n_rows, w), dt))
    for shp in acc_outs:
        out_specs.append(pl.BlockSpec(shp, functools.partial(lambda i, nd: (0,) * nd, nd=len(shp))))
        out_shape.append(jax.ShapeDtypeStruct(shp, F32))
    n_used = len(args)
    n_tiles = n_used - len(vecs)
    in_specs += [_ANY_SPEC] * len(after)
    args += list(after)
    n_in, n_ro, n_acc = len(args), len(row_outs), len(acc_outs)

    def body(*refs):
        ins, ro, ac = refs[:n_used], refs[n_in:n_in + n_ro], refs[n_in + n_ro:]
        i = pl.program_id(0)
        rvals, avals = fn(i == 0, i == nblk - 1, *[r[...] for r in ins[:n_tiles]], *ins[n_tiles:])
        for r, val in zip(ro, rvals):
            r[...] = val.astype(r.dtype)
        if n_acc:
            @pl.when(i == 0)
            def _():
                for r in ac:
                    r[...] = jnp.zeros_like(r)

            for r, val in zip(ac, avals):
                r[...] += val

    res = pl.pallas_call(
        body, name=name, grid=(nblk,), in_specs=in_specs, out_specs=out_specs, out_shape=out_shape,
        compiler_params=pltpu.CompilerParams(dimension_semantics=("arbitrary",)),
    )(*args)
    return res


def _gated_out(o, z, onorm):
    return o * lax.rsqrt(jnp.mean(o * o, axis=-1, keepdims=True) + RMS_EPS) * onorm * _silu(z)


def _head_blocks(ref, heads, col0=0):
    return jnp.stack([ref[:, col0 + h * DN_HEAD_DIM:col0 + (h + 1) * DN_HEAD_DIM] for h in range(heads)])


def _split_heads(q_ref, k_ref, v_ref, gbv, heads):
    gcol = jnp.stack([gbv[:, heads + h:heads + h + 1] for h in range(heads)])
    bcol = jnp.stack([gbv[:, h:h + 1] for h in range(heads)])
    return _head_blocks(q_ref, heads), _head_blocks(k_ref, heads), _head_blocks(v_ref, heads), gcol, bcol


def _delta_fwd(q, k, v, gb, qkvz, onorm, heads):
    s, hd = q.shape
    n = s // DN_CHUNK
    blk = pl.BlockSpec((DN_CHUNK, hd), lambda c: (c, 0))
    gspec = pl.BlockSpec((DN_CHUNK, LANES), lambda c: (c, 0))

    def body(q_ref, k_ref, v_ref, gb_ref, z_ref, on_ref, og_ref, st_ref, tm_ref, state):
        @pl.when(pl.program_id(0) == 0)
        def _():
            state[...] = jnp.zeros_like(state)

        s0 = state[...]
        st_ref[0] = s0
        o, s1, tm = _delta_chunk(*_split_heads(q_ref, k_ref, v_ref, gb_ref[...], heads), s0)
        og = _gated_out(o, _head_blocks(z_ref, heads), on_ref[...])
        for h in range(heads):
            og_ref[:, h * DN_HEAD_DIM:(h + 1) * DN_HEAD_DIM] = og[h].astype(og_ref.dtype)
        state[...] = s1
        tm_ref[0] = tm

    return pl.pallas_call(
        body, name="dn_delta_fwd", grid=(n,),
        in_specs=[blk, blk, blk, gspec, pl.BlockSpec((DN_CHUNK, hd), lambda c: (c, 3)),
                  pl.BlockSpec(onorm.shape, lambda c: (0, 0))],
        out_specs=[blk, pl.BlockSpec((1, heads, DN_HEAD_DIM, DN_HEAD_DIM), lambda c: (c, 0, 0, 0)),
                   pl.BlockSpec((1, heads, DN_CHUNK, DN_CHUNK), lambda c: (c, 0, 0, 0))],
        out_shape=[jax.ShapeDtypeStruct((s, hd), BF16),
                   jax.ShapeDtypeStruct((n, heads, DN_HEAD_DIM, DN_HEAD_DIM), F32),
                   jax.ShapeDtypeStruct((n, heads, DN_CHUNK, DN_CHUNK), F32)],
        scratch_shapes=[pltpu.VMEM((heads, DN_HEAD_DIM, DN_HEAD_DIM), F32)],
        compiler_params=pltpu.CompilerParams(dimension_semantics=("arbitrary",)),
    )(q, k, v, gb, qkvz, onorm)


def _delta_bwd(q, k, v, gb, qkvz, onorm, states, tms, dog, heads):
    s, hd = q.shape
    n = s // DN_CHUNK
    blk = pl.BlockSpec((DN_CHUNK, hd), lambda c: (n - 1 - c, 0))
    gspec = pl.BlockSpec((DN_CHUNK, LANES), lambda c: (n - 1 - c, 0))
    sspec = pl.BlockSpec((1, heads, DN_HEAD_DIM, DN_HEAD_DIM), lambda c: (n - 1 - c, 0, 0, 0))
    tspec = pl.BlockSpec((1, heads, DN_CHUNK, DN_CHUNK), lambda c: (n - 1 - c, 0, 0, 0))
    nspec = pl.BlockSpec(onorm.shape, lambda c: (0, 0))

    def body(q_ref, k_ref, v_ref, gb_ref, z_ref, on_ref, st_ref, tm_ref, dog_ref,
             dq_ref, dk_ref, dv_ref, dgb_ref, dz_ref, don_ref, dstate):
        @pl.when(pl.program_id(0) == 0)
        def _():
            dstate[...] = jnp.zeros_like(dstate)
            don_ref[...] = jnp.zeros_like(don_ref)

        gbv = gb_ref[...]
        tm = tm_ref[0]

        def chunk(qh, kh, vh, gcol, bcol, s0, zh, on):
            o, s1, _ = _delta_chunk(qh, kh, vh, gcol, bcol, s0, tm)
            return _gated_out(o, zh, on), s1

        _, vjp = jax.vjp(chunk, *_split_heads(q_ref, k_ref, v_ref, gbv, heads), st_ref[0],
                         _head_blocks(z_ref, heads), on_ref[...])
        dq, dk, dv, dg, db, ds0, dz, don = vjp((_head_blocks(dog_ref, heads).astype(F32), dstate[...]))
        dstate[...] = ds0
        don_ref[...] += don
        lane = lax.broadcasted_iota(jnp.int32, gbv.shape, 1)
        dgb = jnp.zeros(gbv.shape, F32)
        for h in range(heads):
            sl = slice(h * DN_HEAD_DIM, (h + 1) * DN_HEAD_DIM)
            dq_ref[:, sl] = dq[h]
            dk_ref[:, sl] = dk[h]
            dv_ref[:, sl] = dv[h]
            dz_ref[:, sl] = dz[h]
            dgb = dgb + jnp.where(lane == h, db[h], 0.0) + jnp.where(lane == heads + h, dg[h], 0.0)
        dgb_ref[...] = dgb

    return pl.pallas_call(
        body, name="dn_delta_bwd", grid=(n,),
        in_specs=[blk, blk, blk, gspec, pl.BlockSpec((DN_CHUNK, hd), lambda c: (n - 1 - c, 3)), nspec,
                  sspec, tspec, blk],
        out_specs=[blk, blk, blk, gspec, blk, nspec],
        out_shape=[jax.ShapeDtypeStruct((s, hd), F32)] * 3 + [jax.ShapeDtypeStruct((s, LANES), F32),
                                                              jax.ShapeDtypeStruct((s, hd), F32),
                                                              jax.ShapeDtypeStruct(onorm.shape, F32)],
        scratch_shapes=[pltpu.VMEM((heads, DN_HEAD_DIM, DN_HEAD_DIM), F32)],
        compiler_params=pltpu.CompilerParams(dimension_semantics=("arbitrary",)),
    )(q, k, v, gb, qkvz, onorm, states, tms, dog)


def _dev_index(px, py, pc):
    return 4 * px + 2 * py + pc


def _all_gather(arrs, name):
    n = len(arrs)

    def body(*refs):
        xs, outs = refs[:n], refs[n:2 * n]
        send_sems, recv_sems, local_sems = refs[2 * n:]
        x, y, c = lax.axis_index("x"), lax.axis_index("y"), lax.axis_index("c")
        me, sibling = (x, y, c), (x, y, 1 - c)
        chips = [(1 - x, y), (x, 1 - y), (1 - x, 1 - y)]

        def copy(a, kk, block, to, src=None):
            dst = outs[a].at[_dev_index(*block)]
            return pltpu.make_async_remote_copy(
                src_ref=dst if src is None else src, dst_ref=dst,
                send_sem=send_sems.at[a * 7 + kk], recv_sem=recv_sems.at[a * 7 + kk],
                device_id=to, device_id_type=MESH_IDS)

        mine = [pltpu.make_async_copy(xs[a], outs[a].at[_dev_index(*me)], local_sems.at[a]) for a in range(n)]
        for cp in mine:
            cp.start()
        first = []
        for a in range(n):
            first.append(copy(a, 0, me, sibling, src=xs[a]))
            first += [copy(a, 1 + j, me, (*chip, c), src=xs[a]) for j, chip in enumerate(chips)]
        for cp in first:
            cp.start()
        passed = []
        for j, chip in enumerate(chips):
            for a in range(n):
                copy(a, 1 + j, (*chip, c), me).wait_recv()
                fwd = copy(a, 4 + j, (*chip, c), sibling)
                fwd.start()
                passed.append(fwd)
        for a in range(n):
            copy(a, 0, sibling, me).wait_recv()
        for j, chip in enumerate(chips):
            for a in range(n):
                copy(a, 4 + j, (*chip, 1 - c), me).wait_recv()
        for cp in first + passed:
            cp.wait_send()
        for cp in mine:
            cp.wait()

    hbm = pl.BlockSpec(memory_space=pltpu.HBM)
    res = pl.pallas_call(
        body, name=name,
        in_specs=[hbm] * n, out_specs=[hbm] * n,
        out_shape=[jax.ShapeDtypeStruct((N_DEV,) + a.shape, a.dtype) for a in arrs],
        scratch_shapes=[pltpu.SemaphoreType.DMA((7 * n,)), pltpu.SemaphoreType.DMA((7 * n,)),
                        pltpu.SemaphoreType.DMA((n,))],
    )(*arrs)
    return list(res)


_FLIPS = ((0, 0, 1), (1, 0, 0), (0, 1, 0), (1, 1, 0), (1, 0, 1), (0, 1, 1), (1, 1, 1))
_HBM_SPEC = pl.BlockSpec(memory_space=pltpu.HBM)
_SEM_SPEC = pl.BlockSpec(memory_space=pltpu.SEMAPHORE)
_ANY_SPEC = pl.BlockSpec(memory_space=pl.ANY)
_DATAFLOW = pltpu.SideEffectType.DATAFLOW_SIDE_EFFECTING
TOKEN_SHAPE = (8, LANES)


def _mesh_me():
    return lax.axis_index("x"), lax.axis_index("y"), lax.axis_index("c")


def _flipped(me, f):
    return tuple(1 - v if fl else v for v, fl in zip(me, f))


def _exchange_copies(xs, lands, send_sems, recv_sems, scatter, landed):
    me = _mesh_me()
    cps = []
    for kk, f in enumerate(_FLIPS):
        p = _flipped(me, f)
        for a in range(len(xs)):
            cps.append(pltpu.make_async_remote_copy(
                src_ref=xs[a].at[_dev_index(*p)] if scatter else xs[a],
                dst_ref=lands[a].at[_dev_index(*(p if landed else me))],
                send_sem=send_sems.at[a * 7 + kk], recv_sem=recv_sems.at[a * 7 + kk],
                device_id=p, device_id_type=MESH_IDS))
    return cps


def _exchange_start(srcs, lands, scatter, name, after=()):
    n = len(srcs)

    n_after = len(after)

    def body(*refs):
        xs, ls = refs[:n], refs[n:2 * n]
        send_sems, recv_sems = refs[2 * n + n_after], refs[2 * n + n_after + 1]
        token = refs[-1]
        for cp in _exchange_copies(xs, ls, send_sems, recv_sems, scatter, landed=False):
            cp.start()
        token[...] = jnp.zeros_like(token)

    operands = [pltpu.with_memory_space_constraint(a, pltpu.HBM) for a in list(srcs) + list(lands)]
    res = pl.pallas_call(
        body, name=name,
        in_specs=[_HBM_SPEC] * (2 * n) + [_ANY_SPEC] * len(after),
        out_specs=[_SEM_SPEC, _SEM_SPEC] + [_HBM_SPEC] * (2 * n) + [pl.BlockSpec(memory_space=pltpu.VMEM)],
        out_shape=[pltpu.SemaphoreType.DMA((7 * n,)), pltpu.SemaphoreType.DMA((7 * n,))]
        + [pltpu.HBM(a.shape, a.dtype) for a in operands] + [jax.ShapeDtypeStruct(TOKEN_SHAPE, F32)],
        input_output_aliases={i: 2 + i for i in range(2 * n)},
        compiler_params=pltpu.CompilerParams(has_side_effects=_DATAFLOW),
    )(*operands, *after)
    return (res[0], res[1], list(res[2:2 + n]), list(res[2 + n:2 + 2 * n]), scatter, name), res[-1]


def _exchange_wait(handle, after):
    send_sems, recv_sems, srcs, lands, scatter, name = handle
    n = len(srcs)
    n_after = len(after)

    def body(*refs):
        xs, ls = refs[:n], refs[n:2 * n]
        send_sems_ref, recv_sems_ref = refs[2 * n], refs[2 * n + 1]
        for cp in _exchange_copies(xs, ls, send_sems_ref, recv_sems_ref, scatter, landed=True):
            cp.wait_send()
            cp.wait_recv()

    res = pl.pallas_call(
        body, name=name + "_wait",
        in_specs=[_HBM_SPEC] * (2 * n) + [_SEM_SPEC, _SEM_SPEC] + [_ANY_SPEC] * n_after,
        out_specs=[_HBM_SPEC] * (2 * n),
        out_shape=[pltpu.HBM(a.shape, a.dtype) for a in srcs + lands],
        input_output_aliases={i: i for i in range(2 * n)},
        compiler_params=pltpu.CompilerParams(has_side_effects=_DATAFLOW),
    )(*srcs, *lands, send_sems, recv_sems, *after)
    return list(res[:n]), list(res[n:])


def _slot_sum(g, name, tr):
    _, r, c = g.shape
    tr = min(tr, r)
    assert r % tr == 0

    def body(g_ref, o_ref):
        acc = g_ref[0].astype(F32)
        for s in range(1, N_DEV):
            acc = acc + g_ref[s].astype(F32)
        o_ref[...] = acc

    return pl.pallas_call(
        body, name=name, grid=(r // tr,),
        in_specs=[pl.BlockSpec((N_DEV, tr, c), lambda i: (0, i, 0))],
        out_specs=pl.BlockSpec((tr, c), lambda i: (i, 0)),
        out_shape=jax.ShapeDtypeStruct((r, c), F32),
        compiler_params=pltpu.CompilerParams(dimension_semantics=("parallel",)),
    )(g)


def _adam_update(w, gg, m, v):
    c1 = 1.0 / (1.0 - ADAM_B1 ** ADAM_STEP)
    c2 = 1.0 / (1.0 - ADAM_B2 ** ADAM_STEP)
    nm = ADAM_B1 * m + (1.0 - ADAM_B1) * gg
    nv = ADAM_B2 * v + (1.0 - ADAM_B2) * (gg * gg)
    return -ADAM_LR * ((nm * c1) / (jnp.sqrt(nv * c2) + ADAM_EPS) + ADAM_WD * w), nm, nv


def _adamw_reduce(me, recvs, owns, w, m, v, name, tr=256):
    nl, r, c = w.shape
    assert len(recvs) == nl and len(owns) == nl
    tr = min(tr, r)
    if r % tr == 0:
        tc, nblk = c, r // tr
        at = lambda i: (i, 0)
    else:
        tr, tc = r, min(c, 4 * LANES)
        assert c % tc == 0
        nblk = c // tc
        at = lambda i: (0, i)

    def parked(li, l, i):
        return jnp.where(l < li, 0, jnp.where(l > li, nblk - 1, i))

    def recv_spec(li):
        return pl.BlockSpec((N_DEV, tr, tc), lambda l, i, me_ref: (0, *at(parked(li, l, i))))

    def own_spec(li):
        return pl.BlockSpec((None, tr, tc), lambda l, i, me_ref: (me_ref[0], *at(parked(li, l, i))))

    def body(me_ref, *refs):
        rrefs, orefs = refs[:nl], refs[nl:2 * nl]
        w_ref, m_ref, v_ref, g_ref, d_ref, nm_ref, nv_ref = refs[2 * nl:]
        l = pl.program_id(0)

        def of_layer(vals):
            out = vals[0]
            for li in range(1, nl):
                out = jnp.where(l == li, vals[li], out)
            return out

        own = of_layer([o[...].astype(F32) for o in orefs])
        gg = None
        for s in range(N_DEV):
            slot = jnp.where(me_ref[0] == s, own, of_layer([rr[s].astype(F32) for rr in rrefs]))
            gg = slot if gg is None else gg + slot
        g_ref[...] = gg
        d_ref[...], nm_ref[...], nv_ref[...] = _adam_update(w_ref[...], gg, m_ref[...], v_ref[...])

    spec = pl.BlockSpec((None, tr, tc), lambda l, i, me_ref: (l, *at(i)))
    return pl.pallas_call(
        body, name=name,
        grid_spec=pltpu.PrefetchScalarGridSpec(
            num_scalar_prefetch=1, grid=(nl, nblk),
            in_specs=[recv_spec(li) for li in range(nl)] + [own_spec(li) for li in range(nl)] + [spec] * 3,
            out_specs=[spec] * 4),
        out_shape=[jax.ShapeDtypeStruct((nl, r, c), F32)] * 4,
        compiler_params=pltpu.CompilerParams(dimension_semantics=("arbitrary", "arbitrary")),
    )(me, *recvs, *owns, w, m, v)


def _adamw(w, g, m, v, name, tr=256):
    r, c = w.shape
    tr = min(tr, r)
    assert r % tr == 0

    def body(w_ref, g_ref, m_ref, v_ref, d_ref, nm_ref, nv_ref):
        d_ref[...], nm_ref[...], nv_ref[...] = _adam_update(w_ref[...], g_ref[...], m_ref[...], v_ref[...])

    spec = pl.BlockSpec((tr, c), lambda i: (i, 0))
    return pl.pallas_call(
        body, name=name, grid=(r // tr,), in_specs=[spec] * 4, out_specs=[spec] * 3,
        out_shape=[jax.ShapeDtypeStruct((r, c), F32)] * 3,
        compiler_params=pltpu.CompilerParams(dimension_semantics=("parallel",)),
    )(w, g, m, v)


def _rms_fwd(x, g, name, ts=512, after=()):
    s, d = x.shape

    def fn(first, last, xv, gv):
        return [_rms(xv, gv[...])], []

    return _rowwise(fn, n_rows=s, ts=ts, name=name, rows=[(x, 0, d)], vecs=[g], row_outs=[(d, BF16)],
                    after=after)[0]


def _rms_bwd(x, dn, dres, g, name, ts=256):
    s, d = x.shape

    def fn(first, last, xv, dnv, drv, gv):
        _, vjp = jax.vjp(_rms, xv, gv[...])
        dx, dg = vjp(dnv.astype(F32))
        return [drv + dx], [dg]

    return _rowwise(fn, n_rows=s, ts=ts, name=name, rows=[(x, 0, d), (dn, 0, d), (dres, 0, d)], vecs=[g],
                    row_outs=[(d, F32)], acc_outs=[(1, d)])


def _dn_pre_fwd(qkvz, ba, wconv, alog, dt, heads, ts=128):
    s = qkvz.shape[0]
    d3 = wconv.shape[1]
    d = d3 // 3

    def fn(first, last, xc, bav, xp, wv, av, dv):
        xext = jnp.concatenate([jnp.where(first, 0.0, xp), xc], axis=0)
        cv = _causal_conv(xext, wv, DN_CONV, DN_HALO - (DN_CONV - 1), xc.shape[0])
        return list(_dn_point(cv, bav, av[...], dv[...], heads)), []

    return _rowwise(fn, n_rows=s, ts=ts, name="dn_pre_fwd", rows=[(qkvz, 0, d3), (ba, 0, LANES)],
                    prevs=[(qkvz, 0, d3, DN_HALO)], vecs=[wconv, alog, dt],
                    row_outs=[(d, F32), (d, F32), (d, F32), (LANES, F32)])


def _dn_pre_bwd(qkvz, ba, wconv, alog, dt, dq, dk, dv, dgb, dz, heads, ts=128):
    s = qkvz.shape[0]
    d3 = wconv.shape[1]
    d = d3 // 3
    lead = DN_HALO - (DN_CONV - 1)

    def fn(first, last, xc, bac, dqc, dkc, dvc, dgbc, dzc, xp, xn, ban, dqn, dkn, dvn, dgbn, wv, av, dtv):
        n = xc.shape[0]
        ext = lambda cur, nxt: jnp.concatenate([cur, nxt], axis=0)
        live = lambda nxt: jnp.where(last, 0.0, nxt)
        xall = jnp.concatenate([jnp.where(first, 0.0, xp), xc, live(xn)], axis=0)
        cv = _causal_conv(xall, wv, DN_CONV, lead, n + DN_HALO)
        (_, _, _, gbv), vjp = jax.vjp(lambda c, b: _dn_point(c, b, av[...], dtv[...], heads), cv, ext(bac, ban))
        dc, dba = vjp((ext(dqc, live(dqn)), ext(dkc, live(dkn)), ext(dvc, live(dvn)), ext(dgbc, live(dgbn))))
        dx = None
        dw = []
        for j in range(DN_CONV):
            term = _shift_rows(dc, DN_CONV - 1 - j)[:n] * wv[j:j + 1, :]
            dx = term if dx is None else dx + term
            dw.append(_colsum(dc[:n] * _shift_rows(xall, lead + j)[:n]))
        dba = dba[:n]
        return ([jnp.concatenate([dx, dzc], axis=-1), dba],
                [_stack_rows(dw, DN_CONV), _colsum(dgbc * gbv[:n]), _colsum(dba)])

    return _rowwise(fn, n_rows=s, ts=ts, name="dn_pre_bwd",
                    rows=[(qkvz, 0, d3), (ba, 0, LANES), (dq, 0, d), (dk, 0, d), (dv, 0, d), (dgb, 0, LANES),
                          (dz, 0, d)],
                    prevs=[(qkvz, 0, d3, DN_HALO)],
                    nexts=[(qkvz, 0, d3, DN_HALO), (ba, 0, LANES, DN_HALO), (dq, 0, d, DN_HALO),
                           (dk, 0, d, DN_HALO), (dv, 0, d, DN_HALO), (dgb, 0, LANES, DN_HALO)],
                    vecs=[wconv, alog, dt],
                    row_outs=[(4 * d, BF16), (LANES, BF16)], acc_outs=[(DN_CONV, d3), (1, LANES), (1, LANES)])


def _cv_mid_fwd(u, wdw, bdw, lng, lnb, ts=256):
    s = u.shape[0]
    d = u.shape[1] // 2

    def fn(first, last, uc, up, wv, bv, gv, lbv):
        uext = jnp.concatenate([jnp.where(first, 0.0, up), uc], axis=0)
        glu = uext[:, :d] * _sigmoid(uext[:, d:])
        c = _causal_conv(glu, wv, CV_WIDTH, CV_HALO - (CV_WIDTH - 1), uc.shape[0]) + bv[...]
        return [c, _ln_silu(c, gv[...], lbv[...])], []

    return _rowwise(fn, n_rows=s, ts=ts, name="cv_mid_fwd", rows=[(u, 0, 2 * d)], prevs=[(u, 0, 2 * d, CV_HALO)],
                    vecs=[wdw, bdw, lng, lnb], row_outs=[(d, F32), (d, BF16)])


def _cv_mid_bwd2(dc, u, wdw, ts=256):
    s, d = dc.shape

    def fn(first, last, dcc, uc, up, dcn, wv):
        n = dcc.shape[0]
        dcext = jnp.concatenate([dcc, jnp.where(last, 0.0, dcn)], axis=0)
        uext = jnp.concatenate([jnp.where(first, 0.0, up), uc], axis=0)
        glu = uext[:, :d] * _sigmoid(uext[:, d:])
        dglu = None
        dw = []
        for j in range(CV_WIDTH):
            term = _shift_rows(dcext, CV_WIDTH - 1 - j)[:n] * wv[j:j + 1, :]
            dglu = term if dglu is None else dglu + term
            dw.append(_colsum(dcc * _shift_rows(glu, CV_HALO - (CV_WIDTH - 1) + j)[:n]))
        u1, sg = uc[:, :d], _sigmoid(uc[:, d:])
        du = jnp.concatenate([dglu * sg, dglu * u1 * sg * (1.0 - sg)], axis=-1)
        return [du], [_stack_rows(dw, CV_HALO), _colsum(du)]

    return _rowwise(fn, n_rows=s, ts=ts, name="cv_mid_bwd2", rows=[(dc, 0, d), (u, 0, 2 * d)],
                    prevs=[(u, 0, 2 * d, CV_HALO)], nexts=[(dc, 0, d, CV_HALO)], vecs=[wdw],
                    row_outs=[(2 * d, BF16)], acc_outs=[(CV_HALO, d), (1, 2 * d)])


def _attn_fwd(q, k, v, name, ts=512):
    s, d = q.shape

    def fn(first, last, qv, kv, vv):
        return [_attn_tile(qv.astype(F32), kv[...].astype(F32), vv[...].astype(F32))], []

    return _rowwise(fn, n_rows=s, ts=ts, name=name, rows=[(q, 0, d)], vecs=[k, v], row_outs=[(d, BF16)])[0]


def _attn_bwd(q, k, v, do, name, ts=512):
    s, d = q.shape
    m = k.shape[0]

    def fn(first, last, qv, dov, kv, vv):
        _, vjp = jax.vjp(_attn_tile, qv.astype(F32), kv[...].astype(F32), vv[...].astype(F32))
        dq, dk, dv = vjp(dov.astype(F32))
        return [dq], [dk, dv]

    return _rowwise(fn, n_rows=s, ts=ts, name=name, rows=[(q, 0, d), (do, 0, d)], vecs=[k, v],
                    row_outs=[(d, BF16)], acc_outs=[(m, d), (m, d)])


def _pad_lanes(a, off=0):
    r, n = a.shape
    return jnp.pad(a, ((0, 0), (off, LANES - off - n)))


def _local_step(x, mem, tgt, w, fetch=None, emit=None, first_after=()):
    s, d = x.shape
    heads = d // DN_HEAD_DIM
    g = {}
    if fetch is None:
        fetch = lambda group, after: None
    if emit is None:
        emit = lambda group, grads: ()

    def add_res(acc, res):
        return (res + acc,)

    def add_res_rms(acc, res, gain):
        h = res + acc
        return h, _rms(h, gain)

    def rms_bwd_epi(acc, hx, dres, gain):
        _, vjp = jax.vjp(_rms, hx, gain)
        dx, dg = vjp(acc)
        return dres + dx, dg

    w_int = w["dn_w_in"][0]
    assert w_int.shape[0] == 4 * d + 2 * heads
    w_bat = jnp.pad(w_int[4 * d:], ((0, LANES - 2 * heads), (0, 0)))
    dn_norm = w["dn_norm"]
    alog = _pad_lanes(w["dn_a_log"], heads)
    dtb = _pad_lanes(w["dn_dt_bias"], heads)
    wconv = w["dn_w_conv"][0]
    n0 = _rms_fwd(x, dn_norm, "dn_rms", after=first_after)
    qkvz = _matmul(n0, w_int, "nt", [F32], name="dn_in_proj", b_rows=4 * d)
    ba = _matmul(n0, w_bat, "nt", [F32], name="dn_in_proj_ba")
    q, k, v, gb = _dn_pre_fwd(qkvz, ba, wconv, alog, dtb, heads)
    og, states, tms = _delta_fwd(q, k, v, gb, qkvz, w["dn_out_norm"], heads)
    fetch(1, [og])
    h1, nq0 = _matmul(og, w["dn_w_out"][0], "nn", [F32, BF16], name="dn_out_proj", epi=add_res_rms,
                      mn_extras=[x], row_extras=[w["xa_norm"][0:1]], slab=EPI_SLAB)

    def xattn_fwd(h, nq, layer, next_gain):
        qx = _matmul(nq, w["xa_w_q"][layer], "nn", [BF16], name=f"xa{layer}_q")
        mn = _rms_fwd(mem, w["xa_mem_norm"][layer:layer + 1], f"xa{layer}_mem_rms")
        kv = _matmul(mn, w["xa_w_kv"][layer], "nn", [BF16], name=f"xa{layer}_kv")
        kx, vx = kv[:, :d], kv[:, d:]
        ox = _attn_fwd(qx, kx, vx, f"xa{layer}_attn")
        hn, nn = _matmul(ox, w["xa_w_o"][layer], "nn", [F32, BF16], name=f"xa{layer}_o", epi=add_res_rms,
                         mn_extras=[h], row_extras=[next_gain], slab=EPI_SLAB)
        return hn, nn, (h, nq, qx, mn, kx, vx, ox)

    def sq_relu(t):
        r = jnp.maximum(t.astype(F32), 0.0)
        return r * r

    mlp_store_act = (False, False)

    def loss_epi(acc, res, target, gain):
        def cols(hh, gg):
            e = _rms(hh, gg) - target
            return _colsum(e * e) * (0.5 / d)

        per_col, vjp = jax.vjp(cols, res + acc, gain)
        dhx, dgain = vjp(jnp.ones_like(per_col))
        return dhx, dgain, per_col

    def mlp_fwd(h, nm, layer, next_gain):
        if mlp_store_act[layer]:
            u, a = _matmul(nm, w["mlp_w_up"][layer], "nn", [BF16, BF16], name=f"mlp{layer}_up",
                           epi=lambda acc: (acc, sq_relu(acc)))
            pre = None
        else:
            u = _matmul(nm, w["mlp_w_up"][layer], "nn", [BF16], name=f"mlp{layer}_up")
            a, pre = u, sq_relu
        if next_gain is None:
            hn, *nn = _matmul(a, w["mlp_w_down"][layer], "nn", [F32], name=f"mlp{layer}_down_loss",
                              epi=loss_epi, mn_extras=[h, tgt], row_extras=[w["final_norm"].reshape(1, d)],
                              n_rowsum=2, tk=2048, slab=EPI_SLAB, a_pre=pre)
        else:
            hn, nn = _matmul(a, w["mlp_w_down"][layer], "nn", [F32, BF16], name=f"mlp{layer}_down",
                             epi=add_res_rms, mn_extras=[h], row_extras=[next_gain], tk=2048, slab=EPI_SLAB,
                             a_pre=pre)
        return hn, nn, (h, nm, u, a, pre)

    h2, nm0, xa0 = xattn_fwd(h1, nq0, 0, w["mlp_norm"][0:1])
    fetch(2, [h2])
    h3, n1, mlp0 = mlp_fwd(h2, nm0, 0, w["cv_norm"])

    u_cv = _matmul(n1, w["cv_w_pw1"][0], "nn", [F32], name="cv_pw1", epi=lambda acc, b: (acc + b,),
                   row_extras=[w["cv_b_pw1"]])
    wdw = jnp.pad(w["cv_w_dw"][0], ((0, CV_HALO - CV_WIDTH), (0, 0)))
    c_cv, s_cv = _cv_mid_fwd(u_cv, wdw, w["cv_b_dw"], w["cv_ln_g"], w["cv_ln_b"])
    h4, nq1 = _matmul(s_cv, w["cv_w_pw2"][0], "nn", [F32, BF16], name="cv_pw2",
                      epi=lambda acc, res, b, gain: add_res_rms(acc + b, res, gain), mn_extras=[h3],
                      row_extras=[w["cv_b_pw2"], w["xa_norm"][1:2]], slab=EPI_SLAB)
    fetch(3, [h4])
    h5, nm1, xa1 = xattn_fwd(h4, nq1, 1, w["mlp_norm"][1:2])
    dh, (g_fn, loss_cols), mlp1 = mlp_fwd(h5, nm1, 1, None)
    g["final_norm"] = g_fn.reshape(d)
    loss = jnp.sum(loss_cols, axis=1, keepdims=True)

    def mlp_bwd(dh, layer, saved, after=()):
        h, nm, u, a, pre = saved
        du = _matmul(dh, w["mlp_w_down"][layer], "nt", [BF16], name=f"mlp{layer}_down_dx", after=after,
                     epi=lambda acc, uu: (acc * 2.0 * jnp.maximum(uu.astype(F32), 0.0),), mn_extras=[u])
        gdown = _matmul(a, dh, "tn", [BF16], name=f"mlp{layer}_down_dw", tm=2048, a_pre=pre)
        dhn, gn = _matmul(du, w["mlp_w_up"][layer], "nt", [F32], name=f"mlp{layer}_up_dx", epi=rms_bwd_epi,
                          mn_extras=[h, dh], row_extras=[w["mlp_norm"][layer:layer + 1]], n_rowsum=1,
                          slab=EPI_SLAB)
        gup = _matmul(nm, du, "tn", [BF16], name=f"mlp{layer}_up_dw", out_dm=True, tk=2048)
        return dhn, gup, gdown, gn

    def xattn_bwd(dh, layer, saved):
        h, nq, qx, mn, kx, vx, ox = saved
        dox = _matmul(dh, w["xa_w_o"][layer], "nt", [BF16], name=f"xa{layer}_o_dx")
        go = _matmul(ox, dh, "tn", [BF16], name=f"xa{layer}_o_dw", tk=2048)
        dqx, dkx, dvx = _attn_bwd(qx, kx, vx, dox, f"xa{layer}_attn_bwd")

        def epi(acc, hx, dres, gain):
            dhx, dg = rms_bwd_epi(acc, hx, dres, gain)
            return dhx, dg, _colsum(dhx)

        dhn, gn, dh_cols = _matmul(dqx, w["xa_w_q"][layer], "nt", [F32], name=f"xa{layer}_q_dx", epi=epi,
                                   mn_extras=[h, dh], row_extras=[w["xa_norm"][layer:layer + 1]], n_rowsum=2,
                                   slab=EPI_SLAB)
        gq = _matmul(nq, dqx, "tn", [BF16], name=f"xa{layer}_q_dw", tk=2048)
        dkv = jnp.concatenate([dkx, dvx], axis=-1)
        gkv = _matmul(mn, dkv, "tn", [BF16], name=f"xa{layer}_kv_dw", out_dm=True)
        dmn = _matmul(dkv, w["xa_w_kv"][layer], "nt", [F32], name=f"xa{layer}_kv_dx", tk=2048)
        _, gmem = _rms_bwd(mem, dmn, dmn, w["xa_mem_norm"][layer:layer + 1], f"xa{layer}_mem_rms_bwd")
        return dhn, gq, gkv, go, gn, gmem, dh_cols

    dh, gup1, gdown1, gmn1 = mlp_bwd(dh, 1, mlp1)
    dh, gq1, gkv1, go1, gxn1, gmem1, g_b2 = xattn_bwd(dh, 1, xa1)
    g.update(mlp_w_up=[None, gup1], mlp_w_down=[None, gdown1], xa_w_q=[None, gq1], xa_w_kv=[None, gkv1],
             xa_w_o=[None, go1])
    tok = emit(3, g)

    def ln_bwd_epi(acc, cx, gain, bias):
        _, vjp = jax.vjp(_ln_silu, cx, gain, bias)
        dc, dg, db = vjp(acc)
        return dc, dg, db, _colsum(dc)

    dc_cv, g_lng, g_lnb, g_bdw = _matmul(dh, w["cv_w_pw2"][0], "nt", [F32], name="cv_pw2_dx", after=tok,
                                        epi=ln_bwd_epi, mn_extras=[c_cv],
                                        row_extras=[w["cv_ln_g"], w["cv_ln_b"]], n_rowsum=3, slab=EPI_SLAB)
    g["cv_w_pw2"] = [_matmul(s_cv, dh, "tn", [BF16], name="cv_pw2_dw", tk=2048)]
    du_cv, g_wdw, g_b1 = _cv_mid_bwd2(dc_cv, u_cv, wdw)
    g["cv_w_pw1"] = [_matmul(n1, du_cv, "tn", [BF16], name="cv_pw1_dw", out_dm=True, tk=2048)]
    dh, g_cvn = _matmul(du_cv, w["cv_w_pw1"][0], "nt", [F32], name="cv_pw1_dx", epi=rms_bwd_epi,
                        mn_extras=[h3, dh], row_extras=[w["cv_norm"]], n_rowsum=1, slab=EPI_SLAB)
    g.update(cv_ln_g=g_lng, cv_ln_b=g_lnb, cv_b_dw=g_bdw, cv_b_pw2=g_b2, cv_b_pw1=g_b1, cv_norm=g_cvn,
             cv_w_dw=g_wdw[:CV_WIDTH][None])

    tok = emit(2, g)
    dh, gup0, gdown0, gmn0 = mlp_bwd(dh, 0, mlp0, after=tok)
    dh, gq0, gkv0, go0, gxn0, gmem0, _ = xattn_bwd(dh, 0, xa0)
    g["mlp_w_up"][0] = gup0
    g["mlp_w_down"][0] = gdown0
    g["mlp_norm"] = jnp.concatenate([gmn0, gmn1], axis=0)
    g["xa_w_q"][0] = gq0
    g["xa_w_kv"][0] = gkv0
    g["xa_w_o"][0] = go0
    g["xa_norm"] = jnp.concatenate([gxn0, gxn1], axis=0)
    g["xa_mem_norm"] = jnp.concatenate([gmem0, gmem1], axis=0)
    tok = emit(1, g)

    dog = _matmul(dh, w["dn_w_out"][0], "nt", [BF16], name="dn_out_proj_dx", after=tok)
    g["dn_w_out"] = [_matmul(og, dh, "tn", [BF16], name="dn_out_proj_dw", tk=2048)]
    dq, dk, dv, dgb, dz, g_on = _delta_bwd(q, k, v, gb, qkvz, w["dn_out_norm"], states, tms, dog, heads)
    dqkvz, dba, g_wconv, g_alog, g_dt = _dn_pre_bwd(qkvz, ba, wconv, alog, dtb, dq, dk, dv, dgb, dz, heads)
    g_qkvzt = _matmul(dqkvz, n0, "tn", [BF16], name="dn_in_proj_dw", tk=2048)
    g_bat = _matmul(dba, n0, "tn", [BF16], name="dn_in_proj_ba_dw", tk=2048)
    g["dn_w_in"] = [jnp.concatenate([g_qkvzt, g_bat[:2 * heads]], axis=0)]
    g["dn_w_conv"] = g_wconv[None]
    tok = emit(0, g)
    dn0a = _matmul(dba, w_bat, "nn", [F32], name="dn_in_proj_ba_dx", after=tok)
    grad_x, g_dnn = _matmul(dqkvz, w_int, "nn", [F32], name="dn_in_proj_dx", b_rows=4 * d,
                            epi=lambda acc, part, hx, dres, gain: rms_bwd_epi(acc + part, hx, dres, gain),
                            mn_extras=[dn0a, x, dh], row_extras=[dn_norm], n_rowsum=1, slab=EPI_SLAB)
    g.update(dn_norm=g_dnn, dn_out_norm=g_on,
             dn_a_log=g_alog[:, heads:2 * heads], dn_dt_bias=g_dt[:, heads:2 * heads])
    return loss, grad_x, g


def _round_up(n, m):
    return (n + m - 1) // m * m


def _pack_rows(parts, cols, row_mult):
    lead = parts[0].shape[:-1]
    flat, offs, off = [], [], 0
    for p in parts:
        n = _round_up(p.shape[-1], cols)
        flat.append(jnp.pad(p, [(0, 0)] * len(lead) + [(0, n - p.shape[-1])]))
        offs.append(off)
        off += n
    total = _round_up(off, cols * row_mult)
    if total > off:
        flat.append(jnp.zeros(lead + (total - off,), parts[0].dtype))
    return jnp.concatenate(flat, axis=-1).reshape(lead + (total // cols, cols)), offs


def _unpack(packed, offs, shapes):
    lead = packed.shape[:-2]
    flat = packed.reshape(lead + (-1,))
    out = []
    for off, shp in zip(offs, shapes):
        n = 1
        for v in shp:
            n *= v
        out.append(flat[..., off:off + n].reshape(lead + tuple(shp)))
    return out


def kernel(x, mem, dn_norm, dn_w_in, dn_w_conv, dn_a_log, dn_dt_bias, dn_out_norm, dn_w_out, cv_norm, cv_w_pw1, cv_b_pw1, cv_w_dw, cv_b_dw, cv_ln_g, cv_ln_b, cv_w_pw2, cv_b_pw2, xa_norm, xa_mem_norm, xa_w_q, xa_w_kv, xa_w_o, mlp_norm, mlp_w_up, mlp_w_down, final_norm, loss_target, m_dn_norm, m_dn_w_in, m_dn_w_conv, m_dn_a_log, m_dn_dt_bias, m_dn_out_norm, m_dn_w_out, m_cv_norm, m_cv_w_pw1, m_cv_b_pw1, m_cv_w_dw, m_cv_b_dw, m_cv_ln_g, m_cv_ln_b, m_cv_w_pw2, m_cv_b_pw2, m_xa_norm, m_xa_mem_norm, m_xa_w_q, m_xa_w_kv, m_xa_w_o, m_mlp_norm, m_mlp_w_up, m_mlp_w_down, m_final_norm, v_dn_norm, v_dn_w_in, v_dn_w_conv, v_dn_a_log, v_dn_dt_bias, v_dn_out_norm, v_dn_w_out, v_cv_norm, v_cv_w_pw1, v_cv_b_pw1, v_cv_w_dw, v_cv_b_dw, v_cv_ln_g, v_cv_ln_b, v_cv_w_pw2, v_cv_b_pw2, v_xa_norm, v_xa_mem_norm, v_xa_w_q, v_xa_w_kv, v_xa_w_o, v_mlp_norm, v_mlp_w_up, v_mlp_w_down, v_final_norm):
    wsh = dict(dn_norm=dn_norm, dn_w_in=dn_w_in, dn_w_conv=dn_w_conv, dn_a_log=dn_a_log, dn_dt_bias=dn_dt_bias, dn_out_norm=dn_out_norm, dn_w_out=dn_w_out, cv_norm=cv_norm, cv_w_pw1=cv_w_pw1, cv_b_pw1=cv_b_pw1, cv_w_dw=cv_w_dw, cv_b_dw=cv_b_dw, cv_ln_g=cv_ln_g, cv_ln_b=cv_ln_b, cv_w_pw2=cv_w_pw2, cv_b_pw2=cv_b_pw2, xa_norm=xa_norm, xa_mem_norm=xa_mem_norm, xa_w_q=xa_w_q, xa_w_kv=xa_w_kv, xa_w_o=xa_w_o, mlp_norm=mlp_norm, mlp_w_up=mlp_w_up, mlp_w_down=mlp_w_down, final_norm=final_norm)
    msh = dict(dn_norm=m_dn_norm, dn_w_in=m_dn_w_in, dn_w_conv=m_dn_w_conv, dn_a_log=m_dn_a_log, dn_dt_bias=m_dn_dt_bias, dn_out_norm=m_dn_out_norm, dn_w_out=m_dn_w_out, cv_norm=m_cv_norm, cv_w_pw1=m_cv_w_pw1, cv_b_pw1=m_cv_b_pw1, cv_w_dw=m_cv_w_dw, cv_b_dw=m_cv_b_dw, cv_ln_g=m_cv_ln_g, cv_ln_b=m_cv_ln_b, cv_w_pw2=m_cv_w_pw2, cv_b_pw2=m_cv_b_pw2, xa_norm=m_xa_norm, xa_mem_norm=m_xa_mem_norm, xa_w_q=m_xa_w_q, xa_w_kv=m_xa_w_kv, xa_w_o=m_xa_w_o, mlp_norm=m_mlp_norm, mlp_w_up=m_mlp_w_up, mlp_w_down=m_mlp_w_down, final_norm=m_final_norm)
    vsh = dict(dn_norm=v_dn_norm, dn_w_in=v_dn_w_in, dn_w_conv=v_dn_w_conv, dn_a_log=v_dn_a_log, dn_dt_bias=v_dn_dt_bias, dn_out_norm=v_dn_out_norm, dn_w_out=v_dn_w_out, cv_norm=v_cv_norm, cv_w_pw1=v_cv_w_pw1, cv_b_pw1=v_cv_b_pw1, cv_w_dw=v_cv_w_dw, cv_b_dw=v_cv_b_dw, cv_ln_g=v_cv_ln_g, cv_ln_b=v_cv_ln_b, cv_w_pw2=v_cv_w_pw2, cv_b_pw2=v_cv_b_pw2, xa_norm=v_xa_norm, xa_mem_norm=v_xa_mem_norm, xa_w_q=v_xa_w_q, xa_w_kv=v_xa_w_kv, xa_w_o=v_xa_w_o, mlp_norm=v_mlp_norm, mlp_w_up=v_mlp_w_up, mlp_w_down=v_mlp_w_down, final_norm=v_final_norm)

    big_axis = dict(BIG)
    for src in (wsh, msh, vsh):
        src["dn_w_in"] = jnp.swapaxes(src["dn_w_in"], 1, 2)
    big_axis["dn_w_in"] = 1

    small_pack, small_offs = _pack_rows([wsh[nm].reshape(-1) for nm in SMALL_SH], LANES, 8)
    w = {nm: [None] * wsh[nm].shape[0] for nm in big_axis}

    def put_weights(group, gathered):
        for (nm, layer), gth in zip(group, gathered):
            if big_axis[nm] == 1:
                w[nm][layer] = gth.reshape(N_DEV * gth.shape[1], gth.shape[2])
            else:
                w[nm][layer] = gth

    first = _all_gather([wsh[nm][layer].astype(BF16) for nm, layer in GATHER_GROUPS[0]] + [small_pack],
                        "weights_all_gather_0")
    put_weights(GATHER_GROUPS[0], first)
    me = _dev_index(*_mesh_me())
    gather_handles, tokens = {}, []
    for gi in range(1, len(GATHER_GROUPS)):
        shards = [wsh[nm][layer].astype(BF16) for nm, layer in GATHER_GROUPS[gi]]
        lands = [lax.dynamic_update_slice(lax.empty((N_DEV,) + s.shape, s.dtype), s[None], (me, 0, 0))
                 for s in shards]
        gather_handles[gi], tok = _exchange_start(shards, lands, False, f"weights_gather_{gi}",
                                                  after=[first[-1]] + tokens)
        tokens.append(tok)
    for nm, gth in zip(SMALL_SH, _unpack(first[-1], small_offs, [wsh[nm].shape for nm in SMALL_SH])):
        w[nm] = jnp.moveaxis(gth, 0, -2).reshape(gth.shape[1:-1] + (N_DEV * gth.shape[-1],))
    for nm in REPL:
        w[nm] = wsh[nm]

    def fetch(gi, after):
        put_weights(GATHER_GROUPS[gi], _exchange_wait(gather_handles[gi], after)[1])

    scatter_handles = {}

    def emit(gi, g):
        blocks = []
        for nm, layer in SCATTER_GROUPS[gi]:
            gw = g[nm][layer]
            if big_axis[nm] == 1:
                gw = gw.reshape(N_DEV, gw.shape[0] // N_DEV, gw.shape[1])
            blocks.append(gw)
        if gi == 0:
            gsmall_pack, _ = _pack_rows(
                [jnp.moveaxis(g[nm].reshape(g[nm].shape[:-1] + (N_DEV, -1)), -2, 0).reshape(N_DEV, -1)
                 for nm in SMALL_SH], LANES, 8)
            blocks.append(gsmall_pack)
        lands = [lax.empty(b.shape, b.dtype) for b in blocks]
        scatter_handles[gi], tok = _exchange_start(blocks, lands, True, f"grads_scatter_{gi}")
        return [tok]

    loss_part, grad_x, g = _local_step(x[0], mem[0], loss_target[0], w, fetch, emit, tokens)

    recv = {nm: [None] * wsh[nm].shape[0] for nm in big_axis}
    sent = {nm: [None] * wsh[nm].shape[0] for nm in big_axis}
    gsh, delta, new_m, new_v = {}, {}, {}, {}
    after = [grad_x]
    done = set()
    me_arr = me.astype(jnp.int32).reshape(1)
    def small_adamw(names, name):
        packs = []
        for src in (wsh, gsh, msh, vsh):
            pk, offs = _pack_rows([src[nm].reshape(-1) for nm in names], LANES, 8)
            packs.append(pk)
        outs = _adamw(*packs, name)
        for dst, pk in zip((delta, new_m, new_v), outs):
            for nm, val in zip(names, _unpack(pk, offs, [wsh[nm].shape for nm in names])):
                dst[nm] = val
        return outs[0]

    for gi in reversed(range(len(SCATTER_GROUPS))):
        if gi == 0:
            repl_pack, repl_offs = _pack_rows([g[nm].reshape(-1) for nm in REPL] + [loss_part[:, :1].reshape(-1)],
                                              LANES, 8)
            (repl_all,) = _all_gather([repl_pack], "repl_grads_all_gather")
            repl_red = _slot_sum(repl_all, "repl_grads_sum", 512)
            *repl_vals, loss_sum = _unpack(repl_red, repl_offs, [wsh[nm].shape for nm in REPL] + [(1,)])
            for nm, val in zip(REPL, repl_vals):
                gsh[nm] = val
            after = after + [small_adamw(list(REPL), "adamw_repl")]
        sources, landed = _exchange_wait(scatter_handles[gi], after)
        for (nm, layer), src, r in zip(SCATTER_GROUPS[gi], sources, landed):
            sent[nm][layer], recv[nm][layer] = src, r
        if gi == 0:
            slot = lax.broadcasted_iota(jnp.int32, landed[-1].shape, 0)
            rsmall = jnp.where(slot == me, sources[-1], landed[-1])
        for nm in big_axis:
            if nm not in done and all(r is not None for r in recv[nm]):
                gsh[nm], delta[nm], new_m[nm], new_v[nm] = _adamw_reduce(
                    me_arr, recv[nm], sent[nm], wsh[nm], msh[nm], vsh[nm], f"adamw_{nm}")
                done.add(nm)
                after = [delta[nm]]
    gsmall_red = _slot_sum(rsmall, "grads_small_sum", 512)
    for nm, val in zip(SMALL_SH, _unpack(gsmall_red, small_offs, [wsh[nm].shape for nm in SMALL_SH])):
        gsh[nm] = val
    small_adamw(list(SMALL_SH), "adamw_small")
    for dst in (gsh, delta, new_m, new_v):
        dst["dn_w_in"] = jnp.swapaxes(dst["dn_w_in"], 1, 2)
    return (loss_sum.reshape(()), grad_x[None], *[gsh[nm] for nm in WEIGHTS], *[delta[nm] for nm in WEIGHTS],
            *[new_m[nm] for nm in WEIGHTS], *[new_v[nm] for nm in WEIGHTS])
```

```python
import functools

import jax
import jax.numpy as jnp
from jax import lax
from jax.experimental import pallas as pl
from jax.experimental.pallas import tpu as pltpu

F32 = jnp.float32
BF16 = jnp.bfloat16
MESH_IDS = pl.DeviceIdType.MESH

N_DEV = 8
LANES = 128
RMS_EPS = 1e-6
LN_EPS = 1e-5
DN_HEAD_DIM = 128
DN_CONV = 4
DN_CHUNK = 64
CV_WIDTH = 31
XA_HEADS = 4
MM_TILE = 1024
MM_DEEP = 2048
EPI_SLAB = 256
SLOT_SUM_ROWS = 512
DN_HALO = 8
CV_HALO = 32

ADAM_LR = 0.001
ADAM_B1 = 0.9
ADAM_B2 = 0.999
ADAM_EPS = 1e-08
ADAM_WD = 0.01
ADAM_STEP = 10

BIG = (("dn_w_in", 2), ("dn_w_out", 1), ("cv_w_pw1", 2), ("cv_w_pw2", 1), ("xa_w_q", 1), ("xa_w_kv", 2),
       ("xa_w_o", 1), ("mlp_w_up", 2), ("mlp_w_down", 1))
_LAYER_GROUP = ("xa_w_q", "xa_w_o", "mlp_w_down", "xa_w_kv", "mlp_w_up")
GATHER_GROUPS = (
    (("dn_w_in", 0),),
    (("dn_w_out", 0),) + tuple((nm, 0) for nm in _LAYER_GROUP),
    (("cv_w_pw2", 0), ("cv_w_pw1", 0)),
    tuple((nm, 1) for nm in _LAYER_GROUP),
)
SCATTER_GROUPS = (
    (("dn_w_out", 0), ("dn_w_in", 0)),
    tuple((nm, 0) for nm in _LAYER_GROUP),
    (("cv_w_pw2", 0), ("cv_w_pw1", 0)),
    tuple((nm, 1) for nm in _LAYER_GROUP),
)
SMALL_SH = ("cv_norm", "cv_b_pw1", "cv_b_dw", "cv_ln_g", "cv_ln_b", "cv_b_pw2", "cv_w_dw", "dn_w_conv")
REPL = ("dn_norm", "dn_a_log", "dn_dt_bias", "dn_out_norm", "xa_norm", "xa_mem_norm", "mlp_norm", "final_norm")
WEIGHTS = ("dn_norm", "dn_w_in", "dn_w_conv", "dn_a_log", "dn_dt_bias", "dn_out_norm", "dn_w_out", "cv_norm",
           "cv_w_pw1", "cv_b_pw1", "cv_w_dw", "cv_b_dw", "cv_ln_g", "cv_ln_b", "cv_w_pw2", "cv_b_pw2", "xa_norm",
           "xa_mem_norm", "xa_w_q", "xa_w_kv", "xa_w_o", "mlp_norm", "mlp_w_up", "mlp_w_down", "final_norm")


def _dot_dims(mode, batched):
    o = 1 if batched else 0
    contract = {"nn": ((1 + o,), (o,)), "nt": ((1 + o,), (1 + o,)), "tn": ((o,), (o,))}[mode]
    return (contract, (((0,), (0,)) if batched else ((), ())))


def _bdot(a, b, mode):
    return lax.dot_general(a.astype(BF16), b.astype(BF16), _dot_dims(mode, a.ndim == 3),
                           preferred_element_type=F32)


@functools.partial(jax.custom_vjp, nondiff_argnums=(2,))
def _mm(a, b, mode):
    return _bdot(a, b, mode)


def _mm_fwd(a, b, mode):
    return _bdot(a, b, mode), (a, b)


def _mm_bwd(mode, res, ct):
    a, b = res
    if mode == "nn":
        da, db = _bdot(ct, b, "nt"), _bdot(a, ct, "tn")
    elif mode == "nt":
        da, db = _bdot(ct, b, "nn"), _bdot(ct, a, "tn")
    else:
        da, db = _bdot(b, ct, "nt"), _bdot(a, ct, "nn")
    return da.astype(a.dtype), db.astype(b.dtype)


_mm.defvjp(_mm_fwd, _mm_bwd)


def _sigmoid(x):
    return 0.5 * (jnp.tanh(0.5 * x) + 1.0)


def _silu(x):
    return x * _sigmoid(x)


def _softplus(x):
    return jnp.maximum(x, 0.0) + jnp.log(1.0 + jnp.exp(-jnp.abs(x)))


def _rms(x, g):
    r = lax.rsqrt(jnp.mean(x * x, axis=-1, keepdims=True) + RMS_EPS)
    return x * r * g


def _shift_rows(x, off):
    if off == 0:
        return x
    return pltpu.roll(x, x.shape[0] - off, 0)


def _series_dot(a, b, mode):
    return _bdot(a, b, mode)


def _chunk_masks(c):
    ii = lax.broadcasted_iota(jnp.int32, (c, c), 0)
    jj = lax.broadcasted_iota(jnp.int32, (c, c), 1)
    return (ii == jj).astype(F32), ii >= jj, ii > jj


def _neumann_inverse(lm):
    n = lm.shape[-1]
    t = -lm
    p = lm
    size = 2
    while size < n:
        size *= 2
        p = _series_dot(p, p, "nn")
        t = t + p + _series_dot(t, p, "nn")
    return t


def _apply_inverse(tm, rhs, mode):
    return rhs + _series_dot(tm, rhs, mode)


@jax.custom_vjp
def _unit_lower_solve(lm, rhs, tm):
    return _apply_inverse(tm, rhs, "nn")


def _uls_fwd(lm, rhs, tm):
    sol = _apply_inverse(tm, rhs, "nn")
    return sol, (tm, sol)


def _uls_bwd(res, ct):
    tm, sol = res
    d_rhs = _apply_inverse(tm, ct, "tn")
    return -_bdot(d_rhs, sol, "nt"), d_rhs, jnp.zeros_like(tm)


_unit_lower_solve.defvjp(_uls_fwd, _uls_bwd)


def _delta_chunk(q, k, v, gcol, bcol, s0, tm=None):
    c = q.shape[1]
    eye, causal, strict = _chunk_masks(c)
    grow = jnp.sum(eye * gcol, axis=1, keepdims=True)
    gc = jnp.sum(jnp.where(causal, grow, 0.0), axis=2, keepdims=True)
    gc_row = jnp.sum(eye * gc, axis=1, keepdims=True)
    decay = jnp.exp(jnp.where(causal, gc - gc_row, -jnp.inf))
    kb = k * bcol
    on_k = _mm(jnp.concatenate([kb, q], axis=1), k, "nt")
    lm = jnp.where(strict, on_k[:, :c] * decay, 0.0)
    attn = on_k[:, c:] * decay
    if tm is None:
        tm = _neumann_inverse(lax.stop_gradient(lm))
    egc = jnp.exp(gc)
    rhs = jnp.concatenate([v * bcol, kb * egc], axis=-1)
    sol = _unit_lower_solve(lm, rhs, tm)
    dv_ = v.shape[-1]
    u, w = sol[..., :dv_], sol[..., dv_:]
    gl = jnp.sum(grow, axis=2, keepdims=True)
    kd = k * jnp.exp(gl - gc)
    on_s = _mm(jnp.concatenate([w, q * egc], axis=1), s0, "nn")
    v_new = u - on_s[:, :c]
    o = on_s[:, c:] + _mm(attn, v_new, "nn")
    s1 = s0 * jnp.exp(gl) + _mm(kd, v_new, "tn")
    return o, s1, tm


def _dn_point(cv, ba, alog, dt, heads):
    a = _silu(cv)
    d = cv.shape[1] // 3
    qs, ks = [], []
    for h in range(heads):
        qh = a[:, h * DN_HEAD_DIM:(h + 1) * DN_HEAD_DIM]
        qs.append(qh * lax.rsqrt(jnp.sum(qh * qh, axis=-1, keepdims=True) + 1e-6) * (DN_HEAD_DIM ** -0.5))
        kh = a[:, d + h * DN_HEAD_DIM:d + (h + 1) * DN_HEAD_DIM]
        ks.append(kh * lax.rsqrt(jnp.sum(kh * kh, axis=-1, keepdims=True) + 1e-6))
    q = jnp.concatenate(qs, axis=-1)
    k = jnp.concatenate(ks, axis=-1)
    v = a[:, 2 * d:]
    lane = lax.broadcasted_iota(jnp.int32, ba.shape, 1)
    beta = _sigmoid(ba)
    g = -jnp.exp(alog) * _softplus(ba + dt)
    gb = jnp.where(lane < heads, beta, jnp.where(lane < 2 * heads, g, 0.0))
    return q, k, v, gb


def _attn_tile(q, k, v):
    hd = q.shape[1] // XA_HEADS
    outs = []
    for h in range(XA_HEADS):
        sl = slice(h * hd, (h + 1) * hd)
        s = _mm(q[:, sl], k[:, sl], "nt") * (hd ** -0.5)
        m = lax.stop_gradient(jnp.max(s, axis=-1, keepdims=True))
        e = jnp.exp(s - m)
        p = e / jnp.sum(e, axis=-1, keepdims=True)
        outs.append(_mm(p, v[:, sl], "nn"))
    return jnp.concatenate(outs, axis=-1)


def _ln_silu(c, g, b):
    mu = jnp.mean(c, axis=-1, keepdims=True)
    xc = c - mu
    y = xc * lax.rsqrt(jnp.mean(xc * xc, axis=-1, keepdims=True) + LN_EPS)
    return _silu(y * g + b)


def _causal_conv(xext, w, width, lead, ts):
    acc = None
    for j in range(width):
        term = _shift_rows(xext, lead + j)[:ts] * w[j:j + 1, :]
        acc = term if acc is None else acc + term
    return acc


def _colsum(x):
    return jnp.sum(x, axis=0, keepdims=True)


def _stack_rows(rows, n_rows):
    c = rows[0].shape[1]
    ridx = lax.broadcasted_iota(jnp.int32, (n_rows, c), 0)
    out = jnp.zeros((n_rows, c), F32)
    for j, r in enumerate(rows):
        out = out + jnp.where(ridx == j, r, 0.0)
    return out


def _matmul(a, b, mode, out_dtypes, *, name, epi=None, mn_extras=(), row_extras=(), out_dm=False, after=(),
            n_rowsum=0, slab=0, b_rows=None, a_pre=None, tm=MM_TILE, tn=MM_TILE, tk=MM_TILE):
    b_dm = b.ndim == 3
    b_shape = (b.shape[1], N_DEV * b.shape[2]) if b_dm else b.shape
    if b_rows is not None:
        assert not b_dm and b_rows <= b.shape[0]
        b_shape = (b_rows, b.shape[1])
    if mode == "nn":
        (m, k), (k2, n) = a.shape, b_shape
    elif mode == "nt":
        (m, k), (n, k2) = a.shape, b_shape
    else:
        (k, m), (k2, n) = a.shape, b_shape
    assert k == k2, (a.shape, b.shape, mode)
    tm, tn, tk = min(tm, m), min(tn, n), min(tk, k)
    cb, nb = 0, 1
    if b_dm:
        assert mode in ("nn", "nt")
        cb = b.shape[2]
        nb = max(1, (tn if mode == "nn" else tk) // cb)
        if mode == "nn":
            tn = nb * cb
        else:
            tk = nb * cb
    co, no = 0, 1
    if out_dm:
        co = n // N_DEV
        no = max(1, tn // co)
        tn = no * co
    assert m % tm == 0 and n % tn == 0 and k % tk == 0, (m, n, k, tm, tn, tk)
    nk = k // tk
    if mode == "tn":
        a_spec = pl.BlockSpec((tk, tm), lambda j, i, kk: (kk, i))
    else:
        a_spec = pl.BlockSpec((tm, tk), lambda j, i, kk: (i, kk))
    if b_dm:
        b_spec = (pl.BlockSpec((nb, tn, cb), lambda j, i, kk: (kk, j, 0)) if mode == "nt"
                  else pl.BlockSpec((nb, tk, cb), lambda j, i, kk: (j, kk, 0)))
    else:
        b_spec = (pl.BlockSpec((tn, tk), lambda j, i, kk: (j, kk)) if mode == "nt"
                  else pl.BlockSpec((tk, tn), lambda j, i, kk: (kk, j)))
    mn_spec = pl.BlockSpec((tm, tn), lambda j, i, kk: (i, j))
    row_spec = pl.BlockSpec((1, tn), lambda j, i, kk: (0, j))
    n_extra = len(mn_extras) + len(row_extras)
    n_out = len(out_dtypes)
    in_specs = ([a_spec, b_spec] + [mn_spec] * len(mn_extras) + [row_spec] * len(row_extras)
                + [_ANY_SPEC] * len(after))
    args = [a, b, *mn_extras, *row_extras, *after]
    if out_dm:
        out_specs = [pl.BlockSpec((no, tm, co), lambda j, i, kk: (j, i, 0))] * n_out
        out_shape = [jax.ShapeDtypeStruct((N_DEV, m, co), dt) for dt in out_dtypes]
    else:
        out_specs = [mn_spec] * n_out
        out_shape = [jax.ShapeDtypeStruct((m, n), dt) for dt in out_dtypes]
    out_specs = out_specs + [row_spec] * n_rowsum
    out_shape = out_shape + [jax.ShapeDtypeStruct((1, n), F32)] * n_rowsum
    n_in = len(args)
    n_mn = len(mn_extras)
    step = min(slab, tm) if slab else tm
    assert tm % step == 0

    def dot(a_ref, b_ref):
        a_val = a_ref[...] if a_pre is None else a_pre(a_ref[...])
        if not b_dm:
            return _bdot(a_val, b_ref[...], mode)
        if mode == "nn":
            parts = [_bdot(a_val, b_ref[dd], "nn") for dd in range(nb)]
            return parts[0] if nb == 1 else jnp.concatenate(parts, axis=1)
        out = None
        for dd in range(nb):
            part = _bdot(a_val[:, dd * cb:(dd + 1) * cb], b_ref[dd], "nt")
            out = part if out is None else out + part
        return out

    def finish(acc_src, extras, outs):
        sums = [None] * n_rowsum
        for r0 in range(0, tm, step):
            rs = slice(r0, r0 + step)
            acc_val = acc_src[rs, :]
            if epi is None:
                vals = (acc_val,)
            else:
                vals = epi(acc_val, *[e[rs, :] for e in extras[:n_mn]], *[e[...] for e in extras[n_mn:]])
            for o_ref, val in zip(outs[:n_out], vals[:n_out]):
                if out_dm:
                    for dd in range(no):
                        o_ref[dd, rs, :] = val[:, dd * co:(dd + 1) * co].astype(o_ref.dtype)
                else:
                    o_ref[rs, :] = val.astype(o_ref.dtype)
            for q in range(n_rowsum):
                sums[q] = vals[n_out + q] if sums[q] is None else sums[q] + vals[n_out + q]
        for q in range(n_rowsum):
            s_ref = outs[n_out + q]

            @pl.when(pl.program_id(1) == 0)
            def _():
                s_ref[...] = sums[q]

            @pl.when(pl.program_id(1) > 0)
            def _():
                s_ref[...] += sums[q]

    def body_one_step(*refs):
        finish(dot(refs[0], refs[1]), refs[2:2 + n_extra], refs[n_in:])

    def body(*refs):
        a_ref, b_ref = refs[0], refs[1]
        acc = refs[-1]
        kk = pl.program_id(2)

        @pl.when(kk == 0)
        def _():
            acc[...] = jnp.zeros_like(acc)

        acc[...] += dot(a_ref, b_ref)

        @pl.when(kk == nk - 1)
        def _():
            finish(acc, refs[2:2 + n_extra], refs[n_in:-1])

    res = pl.pallas_call(
        body_one_step if nk == 1 else body, name=name,
        grid=(n // tn, m // tm, nk),
        in_specs=in_specs, out_specs=out_specs, out_shape=out_shape,
        scratch_shapes=[] if nk == 1 else [pltpu.VMEM((tm, tn), F32)],
        compiler_params=pltpu.CompilerParams(
            dimension_semantics=("parallel", "arbitrary" if n_rowsum else "parallel", "arbitrary")),
    )(*args)
    return res[0] if n_out + n_rowsum == 1 else res


def _rowwise(fn, *, n_rows, ts, name, rows=(), prevs=(), nexts=(), vecs=(), row_outs=(), acc_outs=(), after=()):
    ts = min(ts, n_rows)
    assert n_rows % ts == 0
    nblk = n_rows // ts
    in_specs, args = [], []
    for arr, cb, w in rows:
        in_specs.append(pl.BlockSpec((ts, w), functools.partial(lambda i, cb: (i, cb), cb=cb)))
        args.append(arr)
    for arr, cb, w, halo in prevs:
        per = ts // halo
        in_specs.append(pl.BlockSpec(
            (halo, w), functools.partial(lambda i, cb, per: (jnp.maximum(i * per - 1, 0), cb), cb=cb, per=per)))
        args.append(arr)
    for arr, cb, w, halo in nexts:
        per = ts // halo
        last_blk = n_rows // halo - 1
        in_specs.append(pl.BlockSpec(
            (halo, w), functools.partial(lambda i, cb, per, lb: (jnp.minimum((i + 1) * per, lb), cb),
                                         cb=cb, per=per, lb=last_blk)))
        args.append(arr)
    for arr in vecs:
        in_specs.append(pl.BlockSpec(arr.shape, functools.partial(lambda i, nd: (0,) * nd, nd=arr.ndim)))
        args.append(arr)
    out_specs, out_shape = [], []
    for w, dt in row_outs:
        out_specs.append(pl.BlockSpec((ts, w), lambda i: (i, 0)))
        out_shape.append(jax.ShapeDtypeStruct((n_rows, w), dt))
    for shp in acc_outs:
        out_specs.append(pl.BlockSpec(shp, functools.partial(lambda i, nd: (0,) * nd, nd=len(shp))))
        out_shape.append(jax.ShapeDtypeStruct(shp, F32))
    n_used = len(args)
    n_tiles = n_used - len(vecs)
    in_specs += [_ANY_SPEC] * len(after)
    args += list(after)
    n_in, n_ro, n_acc = len(args), len(row_outs), len(acc_outs)

    def body(*refs):
        ins, ro, ac = refs[:n_used], refs[n_in:n_in + n_ro], refs[n_in + n_ro:]
        i = pl.program_id(0)
        rvals, avals = fn(i == 0, i == nblk - 1, *[r[...] for r in ins[:n_tiles]], *ins[n_tiles:])
        for r, val in zip(ro, rvals):
            r[...] = val.astype(r.dtype)
        if n_acc:
            @pl.when(i == 0)
            def _():
                for r in ac:
                    r[...] = jnp.zeros_like(r)

            for r, val in zip(ac, avals):
                r[...] += val

    res = pl.pallas_call(
        body, name=name, grid=(nblk,), in_specs=in_specs, out_specs=out_specs, out_shape=out_shape,
        compiler_params=pltpu.CompilerParams(dimension_semantics=("arbitrary",)),
    )(*args)
    return res


def _gated_out(o, z, onorm):
    return o * lax.rsqrt(jnp.mean(o * o, axis=-1, keepdims=True) + RMS_EPS) * onorm * _silu(z)


def _head_blocks(ref, heads, col0=0):
    return jnp.stack([ref[:, col0 + h * DN_HEAD_DIM:col0 + (h + 1) * DN_HEAD_DIM] for h in range(heads)])


def _split_heads(q_ref, k_ref, v_ref, gbv, heads):
    gcol = jnp.stack([gbv[:, heads + h:heads + h + 1] for h in range(heads)])
    bcol = jnp.stack([gbv[:, h:h + 1] for h in range(heads)])
    return _head_blocks(q_ref, heads), _head_blocks(k_ref, heads), _head_blocks(v_ref, heads), gcol, bcol


def _delta_fwd(q, k, v, gb, qkvz, onorm, heads):
    s, hd = q.shape
    n = s // DN_CHUNK
    blk = pl.BlockSpec((DN_CHUNK, hd), lambda c: (c, 0))
    gspec = pl.BlockSpec((DN_CHUNK, LANES), lambda c: (c, 0))

    def body(q_ref, k_ref, v_ref, gb_ref, z_ref, on_ref, og_ref, st_ref, tm_ref, state):
        @pl.when(pl.program_id(0) == 0)
        def _():
            state[...] = jnp.zeros_like(state)

        s0 = state[...]
        st_ref[0] = s0
        o, s1, tm = _delta_chunk(*_split_heads(q_ref, k_ref, v_ref, gb_ref[...], heads), s0)
        og = _gated_out(o, _head_blocks(z_ref, heads), on_ref[...])
        for h in range(heads):
            og_ref[:, h * DN_HEAD_DIM:(h + 1) * DN_HEAD_DIM] = og[h].astype(og_ref.dtype)
        state[...] = s1
        tm_ref[0] = tm

    return pl.pallas_call(
        body, name="dn_delta_fwd", grid=(n,),
        in_specs=[blk, blk, blk, gspec, pl.BlockSpec((DN_CHUNK, hd), lambda c: (c, 3)),
                  pl.BlockSpec(onorm.shape, lambda c: (0, 0))],
        out_specs=[blk, pl.BlockSpec((1, heads, DN_HEAD_DIM, DN_HEAD_DIM), lambda c: (c, 0, 0, 0)),
                   pl.BlockSpec((1, heads, DN_CHUNK, DN_CHUNK), lambda c: (c, 0, 0, 0))],
        out_shape=[jax.ShapeDtypeStruct((s, hd), BF16),
                   jax.ShapeDtypeStruct((n, heads, DN_HEAD_DIM, DN_HEAD_DIM), F32),
                   jax.ShapeDtypeStruct((n, heads, DN_CHUNK, DN_CHUNK), F32)],
        scratch_shapes=[pltpu.VMEM((heads, DN_HEAD_DIM, DN_HEAD_DIM), F32)],
        compiler_params=pltpu.CompilerParams(dimension_semantics=("arbitrary",)),
    )(q, k, v, gb, qkvz, onorm)


def _delta_bwd(q, k, v, gb, qkvz, onorm, states, tms, dog, heads):
    s, hd = q.shape
    n = s // DN_CHUNK
    blk = pl.BlockSpec((DN_CHUNK, hd), lambda c: (n - 1 - c, 0))
    gspec = pl.BlockSpec((DN_CHUNK, LANES), lambda c: (n - 1 - c, 0))
    sspec = pl.BlockSpec((1, heads, DN_HEAD_DIM, DN_HEAD_DIM), lambda c: (n - 1 - c, 0, 0, 0))
    tspec = pl.BlockSpec((1, heads, DN_CHUNK, DN_CHUNK), lambda c: (n - 1 - c, 0, 0, 0))
    nspec = pl.BlockSpec(onorm.shape, lambda c: (0, 0))

    def body(q_ref, k_ref, v_ref, gb_ref, z_ref, on_ref, st_ref, tm_ref, dog_ref,
             dq_ref, dk_ref, dv_ref, dgb_ref, dz_ref, don_ref, dstate):
        @pl.when(pl.program_id(0) == 0)
        def _():
            dstate[...] = jnp.zeros_like(dstate)
            don_ref[...] = jnp.zeros_like(don_ref)

        gbv = gb_ref[...]
        tm = tm_ref[0]

        def chunk(qh, kh, vh, gcol, bcol, s0, zh, on):
            o, s1, _ = _delta_chunk(qh, kh, vh, gcol, bcol, s0, tm)
            return _gated_out(o, zh, on), s1

        _, vjp = jax.vjp(chunk, *_split_heads(q_ref, k_ref, v_ref, gbv, heads), st_ref[0],
                         _head_blocks(z_ref, heads), on_ref[...])
        dq, dk, dv, dg, db, ds0, dz, don = vjp((_head_blocks(dog_ref, heads).astype(F32), dstate[...]))
        dstate[...] = ds0
        don_ref[...] += don
        lane = lax.broadcasted_iota(jnp.int32, gbv.shape, 1)
        dgb = jnp.zeros(gbv.shape, F32)
        for h in range(heads):
            sl = slice(h * DN_HEAD_DIM, (h + 1) * DN_HEAD_DIM)
            dq_ref[:, sl] = dq[h]
            dk_ref[:, sl] = dk[h]
            dv_ref[:, sl] = dv[h]
            dz_ref[:, sl] = dz[h]
            dgb = dgb + jnp.where(lane == h, db[h], 0.0) + jnp.where(lane == heads + h, dg[h], 0.0)
        dgb_ref[...] = dgb

    return pl.pallas_call(
        body, name="dn_delta_bwd", grid=(n,),
        in_specs=[blk, blk, blk, gspec, pl.BlockSpec((DN_CHUNK, hd), lambda c: (n - 1 - c, 3)), nspec,
                  sspec, tspec, blk],
        out_specs=[blk, blk, blk, gspec, blk, nspec],
        out_shape=[jax.ShapeDtypeStruct((s, hd), F32)] * 3 + [jax.ShapeDtypeStruct((s, LANES), F32),
                                                              jax.ShapeDtypeStruct((s, hd), F32),
                                                              jax.ShapeDtypeStruct(onorm.shape, F32)],
        scratch_shapes=[pltpu.VMEM((heads, DN_HEAD_DIM, DN_HEAD_DIM), F32)],
        compiler_params=pltpu.CompilerParams(dimension_semantics=("arbitrary",)),
    )(q, k, v, gb, qkvz, onorm, states, tms, dog)


def _dev_index(px, py, pc):
    return 4 * px + 2 * py + pc


def _all_gather(arrs, name):
    n = len(arrs)

    def body(*refs):
        xs, outs = refs[:n], refs[n:2 * n]
        send_sems, recv_sems, local_sems = refs[2 * n:]
        x, y, c = lax.axis_index("x"), lax.axis_index("y"), lax.axis_index("c")
        me, sibling = (x, y, c), (x, y, 1 - c)
        chips = [(1 - x, y), (x, 1 - y), (1 - x, 1 - y)]

        def copy(a, kk, block, to, src=None):
            dst = outs[a].at[_dev_index(*block)]
            return pltpu.make_async_remote_copy(
                src_ref=dst if src is None else src, dst_ref=dst,
                send_sem=send_sems.at[a * 7 + kk], recv_sem=recv_sems.at[a * 7 + kk],
                device_id=to, device_id_type=MESH_IDS)

        mine = [pltpu.make_async_copy(xs[a], outs[a].at[_dev_index(*me)], local_sems.at[a]) for a in range(n)]
        for cp in mine:
            cp.start()
        first = []
        for a in range(n):
            first.append(copy(a, 0, me, sibling, src=xs[a]))
            first += [copy(a, 1 + j, me, (*chip, c), src=xs[a]) for j, chip in enumerate(chips)]
        for cp in first:
            cp.start()
        passed = []
        for j, chip in enumerate(chips):
            for a in range(n):
                copy(a, 1 + j, (*chip, c), me).wait_recv()
                fwd = copy(a, 4 + j, (*chip, c), sibling)
                fwd.start()
                passed.append(fwd)
        for a in range(n):
            copy(a, 0, sibling, me).wait_recv()
        for j, chip in enumerate(chips):
            for a in range(n):
                copy(a, 4 + j, (*chip, 1 - c), me).wait_recv()
        for cp in first + passed:
            cp.wait_send()
        for cp in mine:
            cp.wait()

    hbm = pl.BlockSpec(memory_space=pltpu.HBM)
    res = pl.pallas_call(
        body, name=name,
        in_specs=[hbm] * n, out_specs=[hbm] * n,
        out_shape=[jax.ShapeDtypeStruct((N_DEV,) + a.shape, a.dtype) for a in arrs],
        scratch_shapes=[pltpu.SemaphoreType.DMA((7 * n,)), pltpu.SemaphoreType.DMA((7 * n,)),
                        pltpu.SemaphoreType.DMA((n,))],
    )(*arrs)
    return list(res)


_FLIPS = ((0, 0, 1), (1, 0, 0), (0, 1, 0), (1, 1, 0), (1, 0, 1), (0, 1, 1), (1, 1, 1))
_HBM_SPEC = pl.BlockSpec(memory_space=pltpu.HBM)
_SEM_SPEC = pl.BlockSpec(memory_space=pltpu.SEMAPHORE)
_ANY_SPEC = pl.BlockSpec(memory_space=pl.ANY)
_DATAFLOW = pltpu.SideEffectType.DATAFLOW_SIDE_EFFECTING
TOKEN_SHAPE = (8, LANES)


def _mesh_me():
    return lax.axis_index("x"), lax.axis_index("y"), lax.axis_index("c")


def _flipped(me, f):
    return tuple(1 - v if fl else v for v, fl in zip(me, f))


def _exchange_copies(xs, lands, send_sems, recv_sems, scatter, landed):
    me = _mesh_me()
    cps = []
    for kk, f in enumerate(_FLIPS):
        p = _flipped(me, f)
        for a in range(len(xs)):
            cps.append(pltpu.make_async_remote_copy(
                src_ref=xs[a].at[_dev_index(*p)] if scatter else xs[a],
                dst_ref=lands[a].at[_dev_index(*(p if landed else me))],
                send_sem=send_sems.at[a * 7 + kk], recv_sem=recv_sems.at[a * 7 + kk],
                device_id=p, device_id_type=MESH_IDS))
    return cps


def _exchange_start(srcs, lands, scatter, name, after=()):
    n = len(srcs)

    n_after = len(after)

    def body(*refs):
        xs, ls = refs[:n], refs[n:2 * n]
        send_sems, recv_sems = refs[2 * n + n_after], refs[2 * n + n_after + 1]
        token = refs[-1]
        for cp in _exchange_copies(xs, ls, send_sems, recv_sems, scatter, landed=False):
            cp.start()
        token[...] = jnp.zeros_like(token)

    operands = [pltpu.with_memory_space_constraint(a, pltpu.HBM) for a in list(srcs) + list(lands)]
    res = pl.pallas_call(
        body, name=name,
        in_specs=[_HBM_SPEC] * (2 * n) + [_ANY_SPEC] * len(after),
        out_specs=[_SEM_SPEC, _SEM_SPEC] + [_HBM_SPEC] * (2 * n) + [pl.BlockSpec(memory_space=pltpu.VMEM)],
        out_shape=[pltpu.SemaphoreType.DMA((7 * n,)), pltpu.SemaphoreType.DMA((7 * n,))]
        + [pltpu.HBM(a.shape, a.dtype) for a in operands] + [jax.ShapeDtypeStruct(TOKEN_SHAPE, F32)],
        input_output_aliases={i: 2 + i for i in range(2 * n)},
        compiler_params=pltpu.CompilerParams(has_side_effects=_DATAFLOW),
    )(*operands, *after)
    return (res[0], res[1], list(res[2:2 + n]), list(res[2 + n:2 + 2 * n]), scatter, name), res[-1]


def _exchange_wait(handle, after):
    send_sems, recv_sems, srcs, lands, scatter, name = handle
    n = len(srcs)
    n_after = len(after)

    def body(*refs):
        xs, ls = refs[:n], refs[n:2 * n]
        send_sems_ref, recv_sems_ref = refs[2 * n], refs[2 * n + 1]
        for cp in _exchange_copies(xs, ls, send_sems_ref, recv_sems_ref, scatter, landed=True):
            cp.wait_send()
            cp.wait_recv()

    res = pl.pallas_call(
        body, name=name + "_wait",
        in_specs=[_HBM_SPEC] * (2 * n) + [_SEM_SPEC, _SEM_SPEC] + [_ANY_SPEC] * n_after,
        out_specs=[_HBM_SPEC] * (2 * n),
        out_shape=[pltpu.HBM(a.shape, a.dtype) for a in srcs + lands],
        input_output_aliases={i: i for i in range(2 * n)},
        compiler_params=pltpu.CompilerParams(has_side_effects=_DATAFLOW),
    )(*srcs, *lands, send_sems, recv_sems, *after)
    return list(res[:n]), list(res[n:])


def _slot_sum(g, name, tr):
    _, r, c = g.shape
    tr = min(tr, r)
    assert r % tr == 0

    def body(g_ref, o_ref):
        acc = g_ref[0].astype(F32)
        for s in range(1, N_DEV):
            acc = acc + g_ref[s].astype(F32)
        o_ref[...] = acc

    return pl.pallas_call(
        body, name=name, grid=(r // tr,),
        in_specs=[pl.BlockSpec((N_DEV, tr, c), lambda i: (0, i, 0))],
        out_specs=pl.BlockSpec((tr, c), lambda i: (i, 0)),
        out_shape=jax.ShapeDtypeStruct((r, c), F32),
        compiler_params=pltpu.CompilerParams(dimension_semantics=("parallel",)),
    )(g)


def _adam_update(w, gg, m, v):
    c1 = 1.0 / (1.0 - ADAM_B1 ** ADAM_STEP)
    c2 = 1.0 / (1.0 - ADAM_B2 ** ADAM_STEP)
    nm = ADAM_B1 * m + (1.0 - ADAM_B1) * gg
    nv = ADAM_B2 * v + (1.0 - ADAM_B2) * (gg * gg)
    return -ADAM_LR * ((nm * c1) / (jnp.sqrt(nv * c2) + ADAM_EPS) + ADAM_WD * w), nm, nv


def _adamw_reduce(me, recvs, owns, w, m, v, name, tr=256):
    nl, r, c = w.shape
    assert len(recvs) == nl and len(owns) == nl
    tr = min(tr, r)
    if r % tr == 0:
        tc, nblk = c, r // tr
        at = lambda i: (i, 0)
    else:
        tr, tc = r, min(c, 4 * LANES)
        assert c % tc == 0
        nblk = c // tc
        at = lambda i: (0, i)

    def parked(li, l, i):
        return jnp.where(l < li, 0, jnp.where(l > li, nblk - 1, i))

    def recv_spec(li):
        return pl.BlockSpec((N_DEV, tr, tc), lambda l, i, me_ref: (0, *at(parked(li, l, i))))

    def own_spec(li):
        return pl.BlockSpec((None, tr, tc), lambda l, i, me_ref: (me_ref[0], *at(parked(li, l, i))))

    def body(me_ref, *refs):
        rrefs, orefs = refs[:nl], refs[nl:2 * nl]
        w_ref, m_ref, v_ref, g_ref, d_ref, nm_ref, nv_ref = refs[2 * nl:]
        l = pl.program_id(0)

        def of_layer(vals):
            out = vals[0]
            for li in range(1, nl):
                out = jnp.where(l == li, vals[li], out)
            return out

        own = of_layer([o[...].astype(F32) for o in orefs])
        gg = None
        for s in range(N_DEV):
            slot = jnp.where(me_ref[0] == s, own, of_layer([rr[s].astype(F32) for rr in rrefs]))
            gg = slot if gg is None else gg + slot
        g_ref[...] = gg
        d_ref[...], nm_ref[...], nv_ref[...] = _adam_update(w_ref[...], gg, m_ref[...], v_ref[...])

    spec = pl.BlockSpec((None, tr, tc), lambda l, i, me_ref: (l, *at(i)))
    return pl.pallas_call(
        body, name=name,
        grid_spec=pltpu.PrefetchScalarGridSpec(
            num_scalar_prefetch=1, grid=(nl, nblk),
            in_specs=[recv_spec(li) for li in range(nl)] + [own_spec(li) for li in range(nl)] + [spec] * 3,
            out_specs=[spec] * 4),
        out_shape=[jax.ShapeDtypeStruct((nl, r, c), F32)] * 4,
        compiler_params=pltpu.CompilerParams(dimension_semantics=("arbitrary", "arbitrary")),
    )(me, *recvs, *owns, w, m, v)


def _adamw(w, g, m, v, name, tr=256):
    r, c = w.shape
    tr = min(tr, r)
    assert r % tr == 0

    def body(w_ref, g_ref, m_ref, v_ref, d_ref, nm_ref, nv_ref):
        d_ref[...], nm_ref[...], nv_ref[...] = _adam_update(w_ref[...], g_ref[...], m_ref[...], v_ref[...])

    spec = pl.BlockSpec((tr, c), lambda i: (i, 0))
    return pl.pallas_call(
        body, name=name, grid=(r // tr,), in_specs=[spec] * 4, out_specs=[spec] * 3,
        out_shape=[jax.ShapeDtypeStruct((r, c), F32)] * 3,
        compiler_params=pltpu.CompilerParams(dimension_semantics=("parallel",)),
    )(w, g, m, v)


def _rms_fwd(x, g, name, ts=512, after=()):
    s, d = x.shape

    def fn(first, last, xv, gv):
        return [_rms(xv, gv[...])], []

    return _rowwise(fn, n_rows=s, ts=ts, name=name, rows=[(x, 0, d)], vecs=[g], row_outs=[(d, BF16)],
                    after=after)[0]


def _rms_bwd(x, dn, dres, g, name, ts=256):
    s, d = x.shape

    def fn(first, last, xv, dnv, drv, gv):
        _, vjp = jax.vjp(_rms, xv, gv[...])
        dx, dg = vjp(dnv.astype(F32))
        return [drv + dx], [dg]

    return _rowwise(fn, n_rows=s, ts=ts, name=name, rows=[(x, 0, d), (dn, 0, d), (dres, 0, d)], vecs=[g],
                    row_outs=[(d, F32)], acc_outs=[(1, d)])


def _dn_pre_fwd(qkvz, ba, wconv, alog, dt, heads, ts=256):
    s = qkvz.shape[0]
    d3 = wconv.shape[1]
    d = d3 // 3

    def fn(first, last, xc, bav, xp, wv, av, dv):
        xext = jnp.concatenate([jnp.where(first, 0.0, xp), xc], axis=0)
        cv = _causal_conv(xext, wv, DN_CONV, DN_HALO - (DN_CONV - 1), xc.shape[0])
        return list(_dn_point(cv, bav, av[...], dv[...], heads)), []

    return _rowwise(fn, n_rows=s, ts=ts, name="dn_pre_fwd", rows=[(qkvz, 0, d3), (ba, 0, LANES)],
                    prevs=[(qkvz, 0, d3, DN_HALO)], vecs=[wconv, alog, dt],
                    row_outs=[(d, F32), (d, F32), (d, F32), (LANES, F32)])


def _dn_pre_bwd(qkvz, ba, wconv, alog, dt, dq, dk, dv, dgb, dz, heads, ts=128):
    s = qkvz.shape[0]
    d3 = wconv.shape[1]
    d = d3 // 3
    lead = DN_HALO - (DN_CONV - 1)

    def fn(first, last, xc, bac, dqc, dkc, dvc, dgbc, dzc, xp, xn, ban, dqn, dkn, dvn, dgbn, wv, av, dtv):
        n = xc.shape[0]
        ext = lambda cur, nxt: jnp.concatenate([cur, nxt], axis=0)
        live = lambda nxt: jnp.where(last, 0.0, nxt)
        xall = jnp.concatenate([jnp.where(first, 0.0, xp), xc, live(xn)], axis=0)
        cv = _causal_conv(xall, wv, DN_CONV, lead, n + DN_HALO)
        (_, _, _, gbv), vjp = jax.vjp(lambda c, b: _dn_point(c, b, av[...], dtv[...], heads), cv, ext(bac, ban))
        dc, dba = vjp((ext(dqc, live(dqn)), ext(dkc, live(dkn)), ext(dvc, live(dvn)), ext(dgbc, live(dgbn))))
        dx = None
        dw = []
        for j in range(DN_CONV):
            term = _shift_rows(dc, DN_CONV - 1 - j)[:n] * wv[j:j + 1, :]
            dx = term if dx is None else dx + term
            dw.append(_colsum(dc[:n] * _shift_rows(xall, lead + j)[:n]))
        dba = dba[:n]
        return ([jnp.concatenate([dx, dzc], axis=-1), dba],
                [_stack_rows(dw, DN_CONV), _colsum(dgbc * gbv[:n]), _colsum(dba)])

    return _rowwise(fn, n_rows=s, ts=ts, name="dn_pre_bwd",
                    rows=[(qkvz, 0, d3), (ba, 0, LANES), (dq, 0, d), (dk, 0, d), (dv, 0, d), (dgb, 0, LANES),
                          (dz, 0, d)],
                    prevs=[(qkvz, 0, d3, DN_HALO)],
                    nexts=[(qkvz, 0, d3, DN_HALO), (ba, 0, LANES, DN_HALO), (dq, 0, d, DN_HALO),
                           (dk, 0, d, DN_HALO), (dv, 0, d, DN_HALO), (dgb, 0, LANES, DN_HALO)],
                    vecs=[wconv, alog, dt],
                    row_outs=[(4 * d, BF16), (LANES, BF16)], acc_outs=[(DN_CONV, d3), (1, LANES), (1, LANES)])


def _cv_mid_fwd(u, wdw, bdw, lng, lnb, ts=256):
    s = u.shape[0]
    d = u.shape[1] // 2

    def fn(first, last, uc, up, wv, bv, gv, lbv):
        uext = jnp.concatenate([jnp.where(first, 0.0, up), uc], axis=0)
        glu = uext[:, :d] * _sigmoid(uext[:, d:])
        c = _causal_conv(glu, wv, CV_WIDTH, CV_HALO - (CV_WIDTH - 1), uc.shape[0]) + bv[...]
        return [c, _ln_silu(c, gv[...], lbv[...])], []

    return _rowwise(fn, n_rows=s, ts=ts, name="cv_mid_fwd", rows=[(u, 0, 2 * d)], prevs=[(u, 0, 2 * d, CV_HALO)],
                    vecs=[wdw, bdw, lng, lnb], row_outs=[(d, F32), (d, BF16)])


def _cv_mid_bwd2(dc, u, wdw, ts=256):
    s, d = dc.shape

    def fn(first, last, dcc, uc, up, dcn, wv):
        n = dcc.shape[0]
        dcext = jnp.concatenate([dcc, jnp.where(last, 0.0, dcn)], axis=0)
        uext = jnp.concatenate([jnp.where(first, 0.0, up), uc], axis=0)
        glu = uext[:, :d] * _sigmoid(uext[:, d:])
        dglu = None
        dw = []
        for j in range(CV_WIDTH):
            term = _shift_rows(dcext, CV_WIDTH - 1 - j)[:n] * wv[j:j + 1, :]
            dglu = term if dglu is None else dglu + term
            dw.append(_colsum(dcc * _shift_rows(glu, CV_HALO - (CV_WIDTH - 1) + j)[:n]))
        u1, sg = uc[:, :d], _sigmoid(uc[:, d:])
        du = jnp.concatenate([dglu * sg, dglu * u1 * sg * (1.0 - sg)], axis=-1)
        return [du], [_stack_rows(dw, CV_HALO), _colsum(du)]

    return _rowwise(fn, n_rows=s, ts=ts, name="cv_mid_bwd2", rows=[(dc, 0, d), (u, 0, 2 * d)],
                    prevs=[(u, 0, 2 * d, CV_HALO)], nexts=[(dc, 0, d, CV_HALO)], vecs=[wdw],
                    row_outs=[(2 * d, BF16)], acc_outs=[(CV_HALO, d), (1, 2 * d)])


def _attn_fwd(q, k, v, name, ts=512):
    s, d = q.shape

    def fn(first, last, qv, kv, vv):
        return [_attn_tile(qv.astype(F32), kv[...].astype(F32), vv[...].astype(F32))], []

    return _rowwise(fn, n_rows=s, ts=ts, name=name, rows=[(q, 0, d)], vecs=[k, v], row_outs=[(d, BF16)])[0]


def _attn_bwd(q, k, v, do, name, ts=512):
    s, d = q.shape
    m = k.shape[0]

    def fn(first, last, qv, dov, kv, vv):
        _, vjp = jax.vjp(_attn_tile, qv.astype(F32), kv[...].astype(F32), vv[...].astype(F32))
        dq, dk, dv = vjp(dov.astype(F32))
        return [dq], [dk, dv]

    return _rowwise(fn, n_rows=s, ts=ts, name=name, rows=[(q, 0, d), (do, 0, d)], vecs=[k, v],
                    row_outs=[(d, BF16)], acc_outs=[(m, d), (m, d)])


def _pad_lanes(a, off=0):
    r, n = a.shape
    return jnp.pad(a, ((0, 0), (off, LANES - off - n)))


def _local_step(x, mem, tgt, w, fetch=None, emit=None, first_after=()):
    s, d = x.shape
    heads = d // DN_HEAD_DIM
    g = {}
    if fetch is None:
        fetch = lambda group, after: None
    if emit is None:
        emit = lambda group, grads: ()

    def add_res(acc, res):
        return (res + acc,)

    def add_res_rms(acc, res, gain):
        h = res + acc
        return h, _rms(h, gain)

    def rms_bwd_epi(acc, hx, dres, gain):
        _, vjp = jax.vjp(_rms, hx, gain)
        dx, dg = vjp(acc)
        return dres + dx, dg

    w_int = w["dn_w_in"][0]
    assert w_int.shape[0] == 4 * d + 2 * heads
    w_bat = jnp.pad(w_int[4 * d:], ((0, LANES - 2 * heads), (0, 0)))
    dn_norm = w["dn_norm"]
    alog = _pad_lanes(w["dn_a_log"], heads)
    dtb = _pad_lanes(w["dn_dt_bias"], heads)
    wconv = w["dn_w_conv"][0]
    n0 = _rms_fwd(x, dn_norm, "dn_rms", after=first_after)
    qkvz = _matmul(n0, w_int, "nt", [F32], name="dn_in_proj", b_rows=4 * d)
    ba = _matmul(n0, w_bat, "nt", [F32], name="dn_in_proj_ba")
    q, k, v, gb = _dn_pre_fwd(qkvz, ba, wconv, alog, dtb, heads)
    og, states, tms = _delta_fwd(q, k, v, gb, qkvz, w["dn_out_norm"], heads)
    fetch(1, [og])
    h1, nq0 = _matmul(og, w["dn_w_out"][0], "nn", [F32, BF16], name="dn_out_proj", epi=add_res_rms,
                      mn_extras=[x], row_extras=[w["xa_norm"][0:1]], slab=EPI_SLAB)

    def xattn_fwd(h, nq, layer, next_gain):
        qx = _matmul(nq, w["xa_w_q"][layer], "nn", [BF16], name=f"xa{layer}_q")
        mn = _rms_fwd(mem, w["xa_mem_norm"][layer:layer + 1], f"xa{layer}_mem_rms")
        kv = _matmul(mn, w["xa_w_kv"][layer], "nn", [BF16], name=f"xa{layer}_kv")
        kx, vx = kv[:, :d], kv[:, d:]
        ox = _attn_fwd(qx, kx, vx, f"xa{layer}_attn")
        hn, nn = _matmul(ox, w["xa_w_o"][layer], "nn", [F32, BF16], name=f"xa{layer}_o", epi=add_res_rms,
                         mn_extras=[h], row_extras=[next_gain], slab=EPI_SLAB)
        return hn, nn, (h, nq, qx, mn, kx, vx, ox)

    def sq_relu(t):
        r = jnp.maximum(t.astype(F32), 0.0)
        return r * r

    def loss_epi(acc, res, target, gain):
        def cols(hh, gg):
            e = _rms(hh, gg) - target
            return _colsum(e * e) * (0.5 / d)

        per_col, vjp = jax.vjp(cols, res + acc, gain)
        dhx, dgain = vjp(jnp.ones_like(per_col))
        return dhx, dgain, per_col

    def mlp_fwd(h, nm, layer, next_gain):
        u = _matmul(nm, w["mlp_w_up"][layer], "nn", [BF16], name=f"mlp{layer}_up")
        if next_gain is None:
            hn, *nn = _matmul(u, w["mlp_w_down"][layer], "nn", [F32], name=f"mlp{layer}_down_loss",
                              epi=loss_epi, mn_extras=[h, tgt], row_extras=[w["final_norm"].reshape(1, d)],
                              n_rowsum=2, tk=MM_DEEP, slab=EPI_SLAB, a_pre=sq_relu)
        else:
            hn, nn = _matmul(u, w["mlp_w_down"][layer], "nn", [F32, BF16], name=f"mlp{layer}_down",
                             epi=add_res_rms, mn_extras=[h], row_extras=[next_gain], tk=MM_DEEP, slab=EPI_SLAB,
                             a_pre=sq_relu)
        return hn, nn, (h, nm, u)

    h2, nm0, xa0 = xattn_fwd(h1, nq0, 0, w["mlp_norm"][0:1])
    fetch(2, [h2])
    h3, n1, mlp0 = mlp_fwd(h2, nm0, 0, w["cv_norm"])

    u_cv = _matmul(n1, w["cv_w_pw1"][0], "nn", [F32], name="cv_pw1", epi=lambda acc, b: (acc + b,),
                   row_extras=[w["cv_b_pw1"]])
    wdw = jnp.pad(w["cv_w_dw"][0], ((0, CV_HALO - CV_WIDTH), (0, 0)))
    c_cv, s_cv = _cv_mid_fwd(u_cv, wdw, w["cv_b_dw"], w["cv_ln_g"], w["cv_ln_b"])
    h4, nq1 = _matmul(s_cv, w["cv_w_pw2"][0], "nn", [F32, BF16], name="cv_pw2",
                      epi=lambda acc, res, b, gain: add_res_rms(acc + b, res, gain), mn_extras=[h3],
                      row_extras=[w["cv_b_pw2"], w["xa_norm"][1:2]], slab=EPI_SLAB)
    fetch(3, [h4])
    h5, nm1, xa1 = xattn_fwd(h4, nq1, 1, w["mlp_norm"][1:2])
    dh, (g_fn, loss_cols), mlp1 = mlp_fwd(h5, nm1, 1, None)
    g["final_norm"] = g_fn.reshape(d)
    loss = jnp.sum(loss_cols, axis=1, keepdims=True)

    def mlp_bwd(dh, layer, saved, after=()):
        h, nm, u = saved
        du = _matmul(dh, w["mlp_w_down"][layer], "nt", [BF16], name=f"mlp{layer}_down_dx", after=after,
                     epi=lambda acc, uu: (acc * 2.0 * jnp.maximum(uu.astype(F32), 0.0),), mn_extras=[u])
        gdown = _matmul(u, dh, "tn", [BF16], name=f"mlp{layer}_down_dw", tm=MM_DEEP, a_pre=sq_relu)
        dhn, gn = _matmul(du, w["mlp_w_up"][layer], "nt", [F32], name=f"mlp{layer}_up_dx", epi=rms_bwd_epi,
                          mn_extras=[h, dh], row_extras=[w["mlp_norm"][layer:layer + 1]], n_rowsum=1,
                          slab=EPI_SLAB)
        gup = _matmul(nm, du, "tn", [BF16], name=f"mlp{layer}_up_dw", out_dm=True, tk=MM_DEEP)
        return dhn, gup, gdown, gn

    def xattn_bwd(dh, layer, saved):
        h, nq, qx, mn, kx, vx, ox = saved
        dox = _matmul(dh, w["xa_w_o"][layer], "nt", [BF16], name=f"xa{layer}_o_dx")
        go = _matmul(ox, dh, "tn", [BF16], name=f"xa{layer}_o_dw", tk=MM_DEEP)
        dqx, dkx, dvx = _attn_bwd(qx, kx, vx, dox, f"xa{layer}_attn_bwd")

        def epi(acc, hx, dres, gain):
            dhx, dg = rms_bwd_epi(acc, hx, dres, gain)
            return dhx, dg, _colsum(dhx)

        dhn, gn, dh_cols = _matmul(dqx, w["xa_w_q"][layer], "nt", [F32], name=f"xa{layer}_q_dx", epi=epi,
                                   mn_extras=[h, dh], row_extras=[w["xa_norm"][layer:layer + 1]], n_rowsum=2,
                                   slab=EPI_SLAB)
        gq = _matmul(nq, dqx, "tn", [BF16], name=f"xa{layer}_q_dw", tk=MM_DEEP)
        dkv = jnp.concatenate([dkx, dvx], axis=-1)
        gkv = _matmul(mn, dkv, "tn", [BF16], name=f"xa{layer}_kv_dw", out_dm=True)
        dmn = _matmul(dkv, w["xa_w_kv"][layer], "nt", [F32], name=f"xa{layer}_kv_dx", tk=MM_DEEP)
        _, gmem = _rms_bwd(mem, dmn, dmn, w["xa_mem_norm"][layer:layer + 1], f"xa{layer}_mem_rms_bwd")
        return dhn, gq, gkv, go, gn, gmem, dh_cols

    dh, gup1, gdown1, gmn1 = mlp_bwd(dh, 1, mlp1)
    dh, gq1, gkv1, go1, gxn1, gmem1, g_b2 = xattn_bwd(dh, 1, xa1)
    g.update(mlp_w_up=[None, gup1], mlp_w_down=[None, gdown1], xa_w_q=[None, gq1], xa_w_kv=[None, gkv1],
             xa_w_o=[None, go1])
    tok = emit(3, g)

    def ln_bwd_epi(acc, cx, gain, bias):
        _, vjp = jax.vjp(_ln_silu, cx, gain, bias)
        dc, dg, db = vjp(acc)
        return dc, dg, db, _colsum(dc)

    dc_cv, g_lng, g_lnb, g_bdw = _matmul(dh, w["cv_w_pw2"][0], "nt", [F32], name="cv_pw2_dx", after=tok,
                                        epi=ln_bwd_epi, mn_extras=[c_cv],
                                        row_extras=[w["cv_ln_g"], w["cv_ln_b"]], n_rowsum=3, slab=EPI_SLAB)
    g["cv_w_pw2"] = [_matmul(s_cv, dh, "tn", [BF16], name="cv_pw2_dw", tk=MM_DEEP)]
    du_cv, g_wdw, g_b1 = _cv_mid_bwd2(dc_cv, u_cv, wdw)
    g["cv_w_pw1"] = [_matmul(n1, du_cv, "tn", [BF16], name="cv_pw1_dw", out_dm=True, tk=MM_DEEP)]
    dh, g_cvn = _matmul(du_cv, w["cv_w_pw1"][0], "nt", [F32], name="cv_pw1_dx", epi=rms_bwd_epi,
                        mn_extras=[h3, dh], row_extras=[w["cv_norm"]], n_rowsum=1, slab=EPI_SLAB)
    g.update(cv_ln_g=g_lng, cv_ln_b=g_lnb, cv_b_dw=g_bdw, cv_b_pw2=g_b2, cv_b_pw1=g_b1, cv_norm=g_cvn,
             cv_w_dw=g_wdw[:CV_WIDTH][None])

    tok = emit(2, g)
    dh, gup0, gdown0, gmn0 = mlp_bwd(dh, 0, mlp0, after=tok)
    dh, gq0, gkv0, go0, gxn0, gmem0, _ = xattn_bwd(dh, 0, xa0)
    g["mlp_w_up"][0] = gup0
    g["mlp_w_down"][0] = gdown0
    g["mlp_norm"] = jnp.concatenate([gmn0, gmn1], axis=0)
    g["xa_w_q"][0] = gq0
    g["xa_w_kv"][0] = gkv0
    g["xa_w_o"][0] = go0
    g["xa_norm"] = jnp.concatenate([gxn0, gxn1], axis=0)
    g["xa_mem_norm"] = jnp.concatenate([gmem0, gmem1], axis=0)
    tok = emit(1, g)

    dog = _matmul(dh, w["dn_w_out"][0], "nt", [BF16], name="dn_out_proj_dx", after=tok)
    g["dn_w_out"] = [_matmul(og, dh, "tn", [BF16], name="dn_out_proj_dw", tk=MM_DEEP)]
    dq, dk, dv, dgb, dz, g_on = _delta_bwd(q, k, v, gb, qkvz, w["dn_out_norm"], states, tms, dog, heads)
    dqkvz, dba, g_wconv, g_alog, g_dt = _dn_pre_bwd(qkvz, ba, wconv, alog, dtb, dq, dk, dv, dgb, dz, heads)
    g_qkvzt = _matmul(dqkvz, n0, "tn", [BF16], name="dn_in_proj_dw", tk=MM_DEEP)
    g_bat = _matmul(dba, n0, "tn", [BF16], name="dn_in_proj_ba_dw", tk=MM_DEEP)
    g["dn_w_in"] = [jnp.concatenate([g_qkvzt, g_bat[:2 * heads]], axis=0)]
    g["dn_w_conv"] = g_wconv[None]
    tok = emit(0, g)
    dn0a = _matmul(dba, w_bat, "nn", [F32], name="dn_in_proj_ba_dx", after=tok)
    grad_x, g_dnn = _matmul(dqkvz, w_int, "nn", [F32], name="dn_in_proj_dx", b_rows=4 * d,
                            epi=lambda acc, part, hx, dres, gain: rms_bwd_epi(acc + part, hx, dres, gain),
                            mn_extras=[dn0a, x, dh], row_extras=[dn_norm], n_rowsum=1, slab=EPI_SLAB)
    g.update(dn_norm=g_dnn, dn_out_norm=g_on,
             dn_a_log=g_alog[:, heads:2 * heads], dn_dt_bias=g_dt[:, heads:2 * heads])
    return loss, grad_x, g


def _round_up(n, m):
    return (n + m - 1) // m * m


def _pack_rows(parts, cols, row_mult):
    lead = parts[0].shape[:-1]
    flat, offs, off = [], [], 0
    for p in parts:
        n = _round_up(p.shape[-1], cols)
        flat.append(jnp.pad(p, [(0, 0)] * len(lead) + [(0, n - p.shape[-1])]))
        offs.append(off)
        off += n
    total = _round_up(off, cols * row_mult)
    if total > off:
        flat.append(jnp.zeros(lead + (total - off,), parts[0].dtype))
    return jnp.concatenate(flat, axis=-1).reshape(lead + (total // cols, cols)), offs


def _unpack(packed, offs, shapes):
    lead = packed.shape[:-2]
    flat = packed.reshape(lead + (-1,))
    out = []
    for off, shp in zip(offs, shapes):
        n = 1
        for v in shp:
            n *= v
        out.append(flat[..., off:off + n].reshape(lead + tuple(shp)))
    return out


def kernel(x, mem, dn_norm, dn_w_in, dn_w_conv, dn_a_log, dn_dt_bias, dn_out_norm, dn_w_out, cv_norm, cv_w_pw1, cv_b_pw1, cv_w_dw, cv_b_dw, cv_ln_g, cv_ln_b, cv_w_pw2, cv_b_pw2, xa_norm, xa_mem_norm, xa_w_q, xa_w_kv, xa_w_o, mlp_norm, mlp_w_up, mlp_w_down, final_norm, loss_target, m_dn_norm, m_dn_w_in, m_dn_w_conv, m_dn_a_log, m_dn_dt_bias, m_dn_out_norm, m_dn_w_out, m_cv_norm, m_cv_w_pw1, m_cv_b_pw1, m_cv_w_dw, m_cv_b_dw, m_cv_ln_g, m_cv_ln_b, m_cv_w_pw2, m_cv_b_pw2, m_xa_norm, m_xa_mem_norm, m_xa_w_q, m_xa_w_kv, m_xa_w_o, m_mlp_norm, m_mlp_w_up, m_mlp_w_down, m_final_norm, v_dn_norm, v_dn_w_in, v_dn_w_conv, v_dn_a_log, v_dn_dt_bias, v_dn_out_norm, v_dn_w_out, v_cv_norm, v_cv_w_pw1, v_cv_b_pw1, v_cv_w_dw, v_cv_b_dw, v_cv_ln_g, v_cv_ln_b, v_cv_w_pw2, v_cv_b_pw2, v_xa_norm, v_xa_mem_norm, v_xa_w_q, v_xa_w_kv, v_xa_w_o, v_mlp_norm, v_mlp_w_up, v_mlp_w_down, v_final_norm):
    wsh = dict(dn_norm=dn_norm, dn_w_in=dn_w_in, dn_w_conv=dn_w_conv, dn_a_log=dn_a_log, dn_dt_bias=dn_dt_bias, dn_out_norm=dn_out_norm, dn_w_out=dn_w_out, cv_norm=cv_norm, cv_w_pw1=cv_w_pw1, cv_b_pw1=cv_b_pw1, cv_w_dw=cv_w_dw, cv_b_dw=cv_b_dw, cv_ln_g=cv_ln_g, cv_ln_b=cv_ln_b, cv_w_pw2=cv_w_pw2, cv_b_pw2=cv_b_pw2, xa_norm=xa_norm, xa_mem_norm=xa_mem_norm, xa_w_q=xa_w_q, xa_w_kv=xa_w_kv, xa_w_o=xa_w_o, mlp_norm=mlp_norm, mlp_w_up=mlp_w_up, mlp_w_down=mlp_w_down, final_norm=final_norm)
    msh = dict(dn_norm=m_dn_norm, dn_w_in=m_dn_w_in, dn_w_conv=m_dn_w_conv, dn_a_log=m_dn_a_log, dn_dt_bias=m_dn_dt_bias, dn_out_norm=m_dn_out_norm, dn_w_out=m_dn_w_out, cv_norm=m_cv_norm, cv_w_pw1=m_cv_w_pw1, cv_b_pw1=m_cv_b_pw1, cv_w_dw=m_cv_w_dw, cv_b_dw=m_cv_b_dw, cv_ln_g=m_cv_ln_g, cv_ln_b=m_cv_ln_b, cv_w_pw2=m_cv_w_pw2, cv_b_pw2=m_cv_b_pw2, xa_norm=m_xa_norm, xa_mem_norm=m_xa_mem_norm, xa_w_q=m_xa_w_q, xa_w_kv=m_xa_w_kv, xa_w_o=m_xa_w_o, mlp_norm=m_mlp_norm, mlp_w_up=m_mlp_w_up, mlp_w_down=m_mlp_w_down, final_norm=m_final_norm)
    vsh = dict(dn_norm=v_dn_norm, dn_w_in=v_dn_w_in, dn_w_conv=v_dn_w_conv, dn_a_log=v_dn_a_log, dn_dt_bias=v_dn_dt_bias, dn_out_norm=v_dn_out_norm, dn_w_out=v_dn_w_out, cv_norm=v_cv_norm, cv_w_pw1=v_cv_w_pw1, cv_b_pw1=v_cv_b_pw1, cv_w_dw=v_cv_w_dw, cv_b_dw=v_cv_b_dw, cv_ln_g=v_cv_ln_g, cv_ln_b=v_cv_ln_b, cv_w_pw2=v_cv_w_pw2, cv_b_pw2=v_cv_b_pw2, xa_norm=v_xa_norm, xa_mem_norm=v_xa_mem_norm, xa_w_q=v_xa_w_q, xa_w_kv=v_xa_w_kv, xa_w_o=v_xa_w_o, mlp_norm=v_mlp_norm, mlp_w_up=v_mlp_w_up, mlp_w_down=v_mlp_w_down, final_norm=v_final_norm)

    big_axis = dict(BIG)
    for src in (wsh, msh, vsh):
        src["dn_w_in"] = jnp.swapaxes(src["dn_w_in"], 1, 2)
    big_axis["dn_w_in"] = 1

    small_pack, small_offs = _pack_rows([wsh[nm].reshape(-1) for nm in SMALL_SH], LANES, 8)
    w = {nm: [None] * wsh[nm].shape[0] for nm in big_axis}

    def put_weights(group, gathered):
        for (nm, layer), gth in zip(group, gathered):
            if big_axis[nm] == 1:
                w[nm][layer] = gth.reshape(N_DEV * gth.shape[1], gth.shape[2])
            else:
                w[nm][layer] = gth

    first = _all_gather([wsh[nm][layer].astype(BF16) for nm, layer in GATHER_GROUPS[0]] + [small_pack],
                        "weights_all_gather_0")
    put_weights(GATHER_GROUPS[0], first)
    me = _dev_index(*_mesh_me())
    gather_handles, tokens = {}, []
    for gi in range(1, len(GATHER_GROUPS)):
        shards = [wsh[nm][layer].astype(BF16) for nm, layer in GATHER_GROUPS[gi]]
        lands = [lax.dynamic_update_slice(lax.empty((N_DEV,) + s.shape, s.dtype), s[None], (me, 0, 0))
                 for s in shards]
        gather_handles[gi], tok = _exchange_start(shards, lands, False, f"weights_gather_{gi}",
                                                  after=[first[-1]] + tokens)
        tokens.append(tok)
    for nm, gth in zip(SMALL_SH, _unpack(first[-1], small_offs, [wsh[nm].shape for nm in SMALL_SH])):
        w[nm] = jnp.moveaxis(gth, 0, -2).reshape(gth.shape[1:-1] + (N_DEV * gth.shape[-1],))
    for nm in REPL:
        w[nm] = wsh[nm]

    def fetch(gi, after):
        put_weights(GATHER_GROUPS[gi], _exchange_wait(gather_handles[gi], after)[1])

    scatter_handles = {}

    def emit(gi, g):
        blocks = []
        for nm, layer in SCATTER_GROUPS[gi]:
            gw = g[nm][layer]
            if big_axis[nm] == 1:
                gw = gw.reshape(N_DEV, gw.shape[0] // N_DEV, gw.shape[1])
            blocks.append(gw)
        if gi == 0:
            gsmall_pack, _ = _pack_rows(
                [jnp.moveaxis(g[nm].reshape(g[nm].shape[:-1] + (N_DEV, -1)), -2, 0).reshape(N_DEV, -1)
                 for nm in SMALL_SH], LANES, 8)
            blocks.append(gsmall_pack)
        lands = [lax.empty(b.shape, b.dtype) for b in blocks]
        scatter_handles[gi], tok = _exchange_start(blocks, lands, True, f"grads_scatter_{gi}")
        return [tok]

    loss_part, grad_x, g = _local_step(x[0], mem[0], loss_target[0], w, fetch, emit, tokens)

    recv = {nm: [None] * wsh[nm].shape[0] for nm in big_axis}
    sent = {nm: [None] * wsh[nm].shape[0] for nm in big_axis}
    gsh, delta, new_m, new_v = {}, {}, {}, {}
    after = [grad_x]
    done = set()
    me_arr = me.astype(jnp.int32).reshape(1)
    def small_adamw(names, name):
        packs = []
        for src in (wsh, gsh, msh, vsh):
            pk, offs = _pack_rows([src[nm].reshape(-1) for nm in names], LANES, 8)
            packs.append(pk)
        outs = _adamw(*packs, name)
        for dst, pk in zip((delta, new_m, new_v), outs):
            for nm, val in zip(names, _unpack(pk, offs, [wsh[nm].shape for nm in names])):
                dst[nm] = val
        return outs[0]

    for gi in reversed(range(len(SCATTER_GROUPS))):
        if gi == 0:
            repl_pack, repl_offs = _pack_rows([g[nm].reshape(-1) for nm in REPL] + [loss_part[:, :1].reshape(-1)],
                                              LANES, 8)
            (repl_all,) = _all_gather([repl_pack], "repl_grads_all_gather")
            repl_red = _slot_sum(repl_all, "repl_grads_sum", SLOT_SUM_ROWS)
            *repl_vals, loss_sum = _unpack(repl_red, repl_offs, [wsh[nm].shape for nm in REPL] + [(1,)])
            for nm, val in zip(REPL, repl_vals):
                gsh[nm] = val
            after = after + [small_adamw(list(REPL), "adamw_repl")]
        sources, landed = _exchange_wait(scatter_handles[gi], after)
        for (nm, layer), src, r in zip(SCATTER_GROUPS[gi], sources, landed):
            sent[nm][layer], recv[nm][layer] = src, r
        if gi == 0:
            slot = lax.broadcasted_iota(jnp.int32, landed[-1].shape, 0)
            rsmall = jnp.where(slot == me, sources[-1], landed[-1])
        for nm in big_axis:
            if nm not in done and all(r is not None for r in recv[nm]):
                gsh[nm], delta[nm], new_m[nm], new_v[nm] = _adamw_reduce(
                    me_arr, recv[nm], sent[nm], wsh[nm], msh[nm], vsh[nm], f"adamw_{nm}")
                done.add(nm)
                after = [delta[nm]]
    gsmall_red = _slot_sum(rsmall, "grads_small_sum", SLOT_SUM_ROWS)
    for nm, val in zip(SMALL_SH, _unpack(gsmall_red, small_offs, [wsh[nm].shape for nm in SMALL_SH])):
        gsh[nm] = val
    small_adamw(list(SMALL_SH), "adamw_small")
    for dst in (gsh, delta, new_m, new_v):
        dst["dn_w_in"] = jnp.swapaxes(dst["dn_w_in"], 1, 2)
    return (loss_sum.reshape(()), grad_x[None], *[gsh[nm] for nm in WEIGHTS], *[delta[nm] for nm in WEIGHTS],
            *[new_m[nm] for nm in WEIGHTS], *[new_v[nm] for nm in WEIGHTS])
```

```python
import functools

import jax
import jax.numpy as jnp
from jax import lax
from jax.experimental import pallas as pl
from jax.experimental.pallas import tpu as pltpu

F32 = jnp.float32
BF16 = jnp.bfloat16
MESH_IDS = pl.DeviceIdType.MESH

N_DEV = 8
LANES = 128
RMS_EPS = 1e-6
LN_EPS = 1e-5
DN_HEAD_DIM = 128
DN_CONV = 4
DN_CHUNK = 64
CV_WIDTH = 31
XA_HEADS = 4
MM_TILE = 1024
MM_DEEP = 2048
EPI_SLAB = 256
SLOT_SUM_ROWS = 512
DN_HALO = 8
CV_HALO = 32

ADAM_LR = 0.001
ADAM_B1 = 0.9
ADAM_B2 = 0.999
ADAM_EPS = 1e-08
ADAM_WD = 0.01
ADAM_STEP = 10

BIG = (("dn_w_in", 2), ("dn_w_out", 1), ("cv_w_pw1", 2), ("cv_w_pw2", 1), ("xa_w_q", 1), ("xa_w_kv", 2),
       ("xa_w_o", 1), ("mlp_w_up", 2), ("mlp_w_down", 1))
_LAYER_GROUP = ("xa_w_q", "xa_w_o", "mlp_w_down", "xa_w_kv", "mlp_w_up")
GATHER_GROUPS = (
    (("dn_w_in", 0),),
    (("dn_w_out", 0),) + tuple((nm, 0) for nm in _LAYER_GROUP),
    (("cv_w_pw2", 0), ("cv_w_pw1", 0)),
    tuple((nm, 1) for nm in _LAYER_GROUP),
)
SCATTER_GROUPS = (
    (("dn_w_out", 0), ("dn_w_in", 0)),
    tuple((nm, 0) for nm in _LAYER_GROUP),
    (("cv_w_pw2", 0), ("cv_w_pw1", 0)),
    tuple((nm, 1) for nm in _LAYER_GROUP),
)
SMALL_SH = ("cv_norm", "cv_b_pw1", "cv_b_dw", "cv_ln_g", "cv_ln_b", "cv_b_pw2", "cv_w_dw", "dn_w_conv")
REPL = ("dn_norm", "dn_a_log", "dn_dt_bias", "dn_out_norm", "xa_norm", "xa_mem_norm", "mlp_norm", "final_norm")
WEIGHTS = ("dn_norm", "dn_w_in", "dn_w_conv", "dn_a_log", "dn_dt_bias", "dn_out_norm", "dn_w_out", "cv_norm",
           "cv_w_pw1", "cv_b_pw1", "cv_w_dw", "cv_b_dw", "cv_ln_g", "cv_ln_b", "cv_w_pw2", "cv_b_pw2", "xa_norm",
           "xa_mem_norm", "xa_w_q", "xa_w_kv", "xa_w_o", "mlp_norm", "mlp_w_up", "mlp_w_down", "final_norm")


def _dot_dims(mode, batched):
    o = 1 if batched else 0
    contract = {"nn": ((1 + o,), (o,)), "nt": ((1 + o,), (1 + o,)), "tn": ((o,), (o,))}[mode]
    return (contract, (((0,), (0,)) if batched else ((), ())))


def _bdot(a, b, mode):
    return lax.dot_general(a.astype(BF16), b.astype(BF16), _dot_dims(mode, a.ndim == 3),
                           preferred_element_type=F32)


@functools.partial(jax.custom_vjp, nondiff_argnums=(2,))
def _mm(a, b, mode):
    return _bdot(a, b, mode)


def _mm_fwd(a, b, mode):
    return _bdot(a, b, mode), (a, b)


def _mm_bwd(mode, res, ct):
    a, b = res
    if mode == "nn":
        da, db = _bdot(ct, b, "nt"), _bdot(a, ct, "tn")
    elif mode == "nt":
        da, db = _bdot(ct, b, "nn"), _bdot(ct, a, "tn")
    else:
        da, db = _bdot(b, ct, "nt"), _bdot(a, ct, "nn")
    return da.astype(a.dtype), db.astype(b.dtype)


_mm.defvjp(_mm_fwd, _mm_bwd)


def _sigmoid(x):
    return 0.5 * (jnp.tanh(0.5 * x) + 1.0)


def _silu(x):
    return x * _sigmoid(x)


def _softplus(x):
    return jnp.maximum(x, 0.0) + jnp.log(1.0 + jnp.exp(-jnp.abs(x)))


def _rms(x, g):
    r = lax.rsqrt(jnp.mean(x * x, axis=-1, keepdims=True) + RMS_EPS)
    return x * r * g


def _shift_rows(x, off):
    if off == 0:
        return x
    return pltpu.roll(x, x.shape[0] - off, 0)


def _series_dot(a, b, mode):
    return _bdot(a, b, mode)


def _chunk_masks(c):
    ii = lax.broadcasted_iota(jnp.int32, (c, c), 0)
    jj = lax.broadcasted_iota(jnp.int32, (c, c), 1)
    return (ii == jj).astype(F32), ii >= jj, ii > jj


def _neumann_inverse(lm):
    n = lm.shape[-1]
    t = -lm
    p = lm
    size = 2
    while size < n:
        size *= 2
        p = _series_dot(p, p, "nn")
        t = t + p + _series_dot(t, p, "nn")
    return t


def _apply_inverse(tm, rhs, mode):
    return rhs + _series_dot(tm, rhs, mode)


@jax.custom_vjp
def _unit_lower_solve(lm, rhs, tm):
    return _apply_inverse(tm, rhs, "nn")


def _uls_fwd(lm, rhs, tm):
    sol = _apply_inverse(tm, rhs, "nn")
    return sol, (tm, sol)


def _uls_bwd(res, ct):
    tm, sol = res
    d_rhs = _apply_inverse(tm, ct, "tn")
    return -_bdot(d_rhs, sol, "nt"), d_rhs, jnp.zeros_like(tm)


_unit_lower_solve.defvjp(_uls_fwd, _uls_bwd)


def _delta_chunk(q, k, v, gcol, bcol, s0, tm=None):
    c = q.shape[1]
    eye, causal, strict = _chunk_masks(c)
    grow = jnp.sum(eye * gcol, axis=1, keepdims=True)
    gc = jnp.sum(jnp.where(causal, grow, 0.0), axis=2, keepdims=True)
    gc_row = jnp.sum(eye * gc, axis=1, keepdims=True)
    decay = jnp.exp(jnp.where(causal, gc - gc_row, -jnp.inf))
    kb = k * bcol
    on_k = _mm(jnp.concatenate([kb, q], axis=1), k, "nt")
    lm = jnp.where(strict, on_k[:, :c] * decay, 0.0)
    attn = on_k[:, c:] * decay
    if tm is None:
        tm = _neumann_inverse(lax.stop_gradient(lm))
    egc = jnp.exp(gc)
    rhs = jnp.concatenate([v * bcol, kb * egc], axis=-1)
    sol = _unit_lower_solve(lm, rhs, tm)
    dv_ = v.shape[-1]
    u, w = sol[..., :dv_], sol[..., dv_:]
    gl = jnp.sum(grow, axis=2, keepdims=True)
    kd = k * jnp.exp(gl - gc)
    on_s = _mm(jnp.concatenate([w, q * egc], axis=1), s0, "nn")
    v_new = u - on_s[:, :c]
    o = on_s[:, c:] + _mm(attn, v_new, "nn")
    s1 = s0 * jnp.exp(gl) + _mm(kd, v_new, "tn")
    return o, s1, tm


def _dn_point(cv, ba, alog, dt, heads):
    a = _silu(cv)
    d = cv.shape[1] // 3
    qs, ks = [], []
    for h in range(heads):
        qh = a[:, h * DN_HEAD_DIM:(h + 1) * DN_HEAD_DIM]
        qs.append(qh * lax.rsqrt(jnp.sum(qh * qh, axis=-1, keepdims=True) + 1e-6) * (DN_HEAD_DIM ** -0.5))
        kh = a[:, d + h * DN_HEAD_DIM:d + (h + 1) * DN_HEAD_DIM]
        ks.append(kh * lax.rsqrt(jnp.sum(kh * kh, axis=-1, keepdims=True) + 1e-6))
    q = jnp.concatenate(qs, axis=-1)
    k = jnp.concatenate(ks, axis=-1)
    v = a[:, 2 * d:]
    lane = lax.broadcasted_iota(jnp.int32, ba.shape, 1)
    beta = _sigmoid(ba)
    g = -jnp.exp(alog) * _softplus(ba + dt)
    gb = jnp.where(lane < heads, beta, jnp.where(lane < 2 * heads, g, 0.0))
    return q, k, v, gb


def _attn_tile(q, k, v):
    hd = q.shape[1] // XA_HEADS
    outs = []
    for h in range(XA_HEADS):
        sl = slice(h * hd, (h + 1) * hd)
        s = _mm(q[:, sl], k[:, sl], "nt") * (hd ** -0.5)
        m = lax.stop_gradient(jnp.max(s, axis=-1, keepdims=True))
        e = jnp.exp(s - m)
        p = e / jnp.sum(e, axis=-1, keepdims=True)
        outs.append(_mm(p, v[:, sl], "nn"))
    return jnp.concatenate(outs, axis=-1)


def _ln_silu(c, g, b):
    mu = jnp.mean(c, axis=-1, keepdims=True)
    xc = c - mu
    y = xc * lax.rsqrt(jnp.mean(xc * xc, axis=-1, keepdims=True) + LN_EPS)
    return _silu(y * g + b)


def _causal_conv(xext, w, width, lead, ts):
    acc = None
    for j in range(width):
        term = _shift_rows(xext, lead + j)[:ts] * w[j:j + 1, :]
        acc = term if acc is None else acc + term
    return acc


def _colsum(x):
    return jnp.sum(x, axis=0, keepdims=True)


def _stack_rows(rows, n_rows):
    c = rows[0].shape[1]
    ridx = lax.broadcasted_iota(jnp.int32, (n_rows, c), 0)
    out = jnp.zeros((n_rows, c), F32)
    for j, r in enumerate(rows):
        out = out + jnp.where(ridx == j, r, 0.0)
    return out


def _matmul(a, b, mode, out_dtypes, *, name, epi=None, mn_extras=(), row_extras=(), out_dm=False, after=(),
            n_rowsum=0, slab=0, b_rows=None, a_pre=None, tm=MM_TILE, tn=MM_TILE, tk=MM_TILE):
    b_dm = b.ndim == 3
    b_shape = (b.shape[1], N_DEV * b.shape[2]) if b_dm else b.shape
    if b_rows is not None:
        assert not b_dm and b_rows <= b.shape[0]
        b_shape = (b_rows, b.shape[1])
    if mode == "nn":
        (m, k), (k2, n) = a.shape, b_shape
    elif mode == "nt":
        (m, k), (n, k2) = a.shape, b_shape
    else:
        (k, m), (k2, n) = a.shape, b_shape
    assert k == k2, (a.shape, b.shape, mode)
    tm, tn, tk = min(tm, m), min(tn, n), min(tk, k)
    cb, nb = 0, 1
    if b_dm:
        assert mode in ("nn", "nt")
        cb = b.shape[2]
        nb = max(1, (tn if mode == "nn" else tk) // cb)
        if mode == "nn":
            tn = nb * cb
        else:
            tk = nb * cb
    co, no = 0, 1
    if out_dm:
        co = n // N_DEV
        no = max(1, tn // co)
        tn = no * co
    assert m % tm == 0 and n % tn == 0 and k % tk == 0, (m, n, k, tm, tn, tk)
    nk = k // tk
    if mode == "tn":
        a_spec = pl.BlockSpec((tk, tm), lambda j, i, kk: (kk, i))
    else:
        a_spec = pl.BlockSpec((tm, tk), lambda j, i, kk: (i, kk))
    if b_dm:
        b_spec = (pl.BlockSpec((nb, tn, cb), lambda j, i, kk: (kk, j, 0)) if mode == "nt"
                  else pl.BlockSpec((nb, tk, cb), lambda j, i, kk: (j, kk, 0)))
    else:
        b_spec = (pl.BlockSpec((tn, tk), lambda j, i, kk: (j, kk)) if mode == "nt"
                  else pl.BlockSpec((tk, tn), lambda j, i, kk: (kk, j)))
    mn_spec = pl.BlockSpec((tm, tn), lambda j, i, kk: (i, j))
    row_spec = pl.BlockSpec((1, tn), lambda j, i, kk: (0, j))
    n_extra = len(mn_extras) + len(row_extras)
    n_out = len(out_dtypes)
    in_specs = ([a_spec, b_spec] + [mn_spec] * len(mn_extras) + [row_spec] * len(row_extras)
                + [_ANY_SPEC] * len(after))
    args = [a, b, *mn_extras, *row_extras, *after]
    if out_dm:
        out_specs = [pl.BlockSpec((no, tm, co), lambda j, i, kk: (j, i, 0))] * n_out
        out_shape = [jax.ShapeDtypeStruct((N_DEV, m, co), dt) for dt in out_dtypes]
    else:
        out_specs = [mn_spec] * n_out
        out_shape = [jax.ShapeDtypeStruct((m, n), dt) for dt in out_dtypes]
    out_specs = out_specs + [row_spec] * n_rowsum
    out_shape = out_shape + [jax.ShapeDtypeStruct((1, n), F32)] * n_rowsum
    n_in = len(args)
    n_mn = len(mn_extras)
    step = min(slab, tm) if slab else tm
    assert tm % step == 0

    def dot(a_ref, b_ref):
        a_val = a_ref[...] if a_pre is None else a_pre(a_ref[...])
        if not b_dm:
            return _bdot(a_val, b_ref[...], mode)
        if mode == "nn":
            parts = [_bdot(a_val, b_ref[dd], "nn") for dd in range(nb)]
            return parts[0] if nb == 1 else jnp.concatenate(parts, axis=1)
        out = None
        for dd in range(nb):
            part = _bdot(a_val[:, dd * cb:(dd + 1) * cb], b_ref[dd], "nt")
            out = part if out is None else out + part
        return out

    def finish(acc_src, extras, outs):
        sums = [None] * n_rowsum
        for r0 in range(0, tm, step):
            rs = slice(r0, r0 + step)
            acc_val = acc_src[rs, :]
            if epi is None:
                vals = (acc_val,)
            else:
                vals = epi(acc_val, *[e[rs, :] for e in extras[:n_mn]], *[e[...] for e in extras[n_mn:]])
            for o_ref, val in zip(outs[:n_out], vals[:n_out]):
                if out_dm:
                    for dd in range(no):
                        o_ref[dd, rs, :] = val[:, dd * co:(dd + 1) * co].astype(o_ref.dtype)
                else:
                    o_ref[rs, :] = val.astype(o_ref.dtype)
            for q in range(n_rowsum):
                sums[q] = vals[n_out + q] if sums[q] is None else sums[q] + vals[n_out + q]
        for q in range(n_rowsum):
            s_ref = outs[n_out + q]

            @pl.when(pl.program_id(1) == 0)
            def _():
                s_ref[...] = sums[q]

            @pl.when(pl.program_id(1) > 0)
            def _():
                s_ref[...] += sums[q]

    def body_one_step(*refs):
        finish(dot(refs[0], refs[1]), refs[2:2 + n_extra], refs[n_in:])

    def body(*refs):
        a_ref, b_ref = refs[0], refs[1]
        acc = refs[-1]
        kk = pl.program_id(2)

        @pl.when(kk == 0)
        def _():
            acc[...] = jnp.zeros_like(acc)

        acc[...] += dot(a_ref, b_ref)

        @pl.when(kk == nk - 1)
        def _():
            finish(acc, refs[2:2 + n_extra], refs[n_in:-1])

    res = pl.pallas_call(
        body_one_step if nk == 1 else body, name=name,
        grid=(n // tn, m // tm, nk),
        in_specs=in_specs, out_specs=out_specs, out_shape=out_shape,
        scratch_shapes=[] if nk == 1 else [pltpu.VMEM((tm, tn), F32)],
        compiler_params=pltpu.CompilerParams(
            dimension_semantics=("parallel", "arbitrary" if n_rowsum else "parallel", "arbitrary")),
    )(*args)
    return res[0] if n_out + n_rowsum == 1 else res


def _rowwise(fn, *, n_rows, ts, name, rows=(), prevs=(), nexts=(), vecs=(), row_outs=(), acc_outs=(), after=()):
    ts = min(ts, n_rows)
    assert n_rows % ts == 0
    nblk = n_rows // ts
    in_specs, args = [], []
    for arr, cb, w in rows:
        in_specs.append(pl.BlockSpec((ts, w), functools.partial(lambda i, cb: (i, cb), cb=cb)))
        args.append(arr)
    for arr, cb, w, halo in prevs:
        per = ts // halo
        in_specs.append(pl.BlockSpec(
            (halo, w), functools.partial(lambda i, cb, per: (jnp.maximum(i * per - 1, 0), cb), cb=cb, per=per)))
        args.append(arr)
    for arr, cb, w, halo in nexts:
        per = ts // halo
        last_blk = n_rows // halo - 1
        in_specs.append(pl.BlockSpec(
            (halo, w), functools.partial(lambda i, cb, per, lb: (jnp.minimum((i + 1) * per, lb), cb),
                                         cb=cb, per=per, lb=last_blk)))
        args.append(arr)
    for arr in vecs:
        in_specs.append(pl.BlockSpec(arr.shape, functools.partial(lambda i, nd: (0,) * nd, nd=arr.ndim)))
        args.append(arr)
    out_specs, out_shape = [], []
    for w, dt in row_outs:
        out_specs.append(pl.BlockSpec((ts, w), lambda i: (i, 0)))
        out_shape.append(jax.ShapeDtypeStruct((n_rows, w), dt))
    for shp in acc_outs:
        out_specs.append(pl.BlockSpec(shp, functools.partial(lambda i, nd: (0,) * nd, nd=len(shp))))
        out_shape.append(jax.ShapeDtypeStruct(shp, F32))
    n_used = len(args)
    n_tiles = n_used - len(vecs)
    in_specs += [_ANY_SPEC] * len(after)
    args += list(after)
    n_in, n_ro, n_acc = len(args), len(row_outs), len(acc_outs)

    def body(*refs):
        ins, ro, ac = refs[:n_used], refs[n_in:n_in + n_ro], refs[n_in + n_ro:]
        i = pl.program_id(0)
        rvals, avals = fn(i == 0, i == nblk - 1, *[r[...] for r in ins[:n_tiles]], *ins[n_tiles:])
        for r, val in zip(ro, rvals):
            r[...] = val.astype(r.dtype)
        if n_acc:
            @pl.when(i == 0)
            def _():
                for r in ac:
                    r[...] = jnp.zeros_like(r)

            for r, val in zip(ac, avals):
                r[...] += val

    res = pl.pallas_call(
        body, name=name, grid=(nblk,), in_specs=in_specs, out_specs=out_specs, out_shape=out_shape,
        compiler_params=pltpu.CompilerParams(dimension_semantics=("arbitrary",)),
    )(*args)
    return res


def _gated_out(o, z, onorm):
    return o * lax.rsqrt(jnp.mean(o * o, axis=-1, keepdims=True) + RMS_EPS) * onorm * _silu(z)


def _head_blocks(ref, heads, col0=0):
    return jnp.stack([ref[:, col0 + h * DN_HEAD_DIM:col0 + (h + 1) * DN_HEAD_DIM] for h in range(heads)])


def _split_heads(q_ref, k_ref, v_ref, gbv, heads):
    gcol = jnp.stack([gbv[:, heads + h:heads + h + 1] for h in range(heads)])
    bcol = jnp.stack([gbv[:, h:h + 1] for h in range(heads)])
    return _head_blocks(q_ref, heads), _head_blocks(k_ref, heads), _head_blocks(v_ref, heads), gcol, bcol


def _delta_fwd(q, k, v, gb, qkvz, onorm, heads):
    s, hd = q.shape
    n = s // DN_CHUNK
    blk = pl.BlockSpec((DN_CHUNK, hd), lambda c: (c, 0))
    gspec = pl.BlockSpec((DN_CHUNK, LANES), lambda c: (c, 0))

    def body(q_ref, k_ref, v_ref, gb_ref, z_ref, on_ref, og_ref, st_ref, tm_ref, state):
        @pl.when(pl.program_id(0) == 0)
        def _():
            state[...] = jnp.zeros_like(state)

        s0 = state[...]
        st_ref[0] = s0
        o, s1, tm = _delta_chunk(*_split_heads(q_ref, k_ref, v_ref, gb_ref[...], heads), s0)
        og = _gated_out(o, _head_blocks(z_ref, heads), on_ref[...])
        for h in range(heads):
            og_ref[:, h * DN_HEAD_DIM:(h + 1) * DN_HEAD_DIM] = og[h].astype(og_ref.dtype)
        state[...] = s1
        tm_ref[0] = tm

    return pl.pallas_call(
        body, name="dn_delta_fwd", grid=(n,),
        in_specs=[blk, blk, blk, gspec, pl.BlockSpec((DN_CHUNK, hd), lambda c: (c, 3)),
                  pl.BlockSpec(onorm.shape, lambda c: (0, 0))],
        out_specs=[blk, pl.BlockSpec((1, heads, DN_HEAD_DIM, DN_HEAD_DIM), lambda c: (c, 0, 0, 0)),
                   pl.BlockSpec((1, heads, DN_CHUNK, DN_CHUNK), lambda c: (c, 0, 0, 0))],
        out_shape=[jax.ShapeDtypeStruct((s, hd), BF16),
                   jax.ShapeDtypeStruct((n, heads, DN_HEAD_DIM, DN_HEAD_DIM), F32),
                   jax.ShapeDtypeStruct((n, heads, DN_CHUNK, DN_CHUNK), F32)],
        scratch_shapes=[pltpu.VMEM((heads, DN_HEAD_DIM, DN_HEAD_DIM), F32)],
        compiler_params=pltpu.CompilerParams(dimension_semantics=("arbitrary",)),
    )(q, k, v, gb, qkvz, onorm)


def _delta_bwd(q, k, v, gb, qkvz, onorm, states, tms, dog, heads):
    s, hd = q.shape
    n = s // DN_CHUNK
    blk = pl.BlockSpec((DN_CHUNK, hd), lambda c: (n - 1 - c, 0))
    gspec = pl.BlockSpec((DN_CHUNK, LANES), lambda c: (n - 1 - c, 0))
    sspec = pl.BlockSpec((1, heads, DN_HEAD_DIM, DN_HEAD_DIM), lambda c: (n - 1 - c, 0, 0, 0))
    tspec = pl.BlockSpec((1, heads, DN_CHUNK, DN_CHUNK), lambda c: (n - 1 - c, 0, 0, 0))
    nspec = pl.BlockSpec(onorm.shape, lambda c: (0, 0))

    def body(q_ref, k_ref, v_ref, gb_ref, z_ref, on_ref, st_ref, tm_ref, dog_ref,
             dq_ref, dk_ref, dv_ref, dgb_ref, dz_ref, don_ref, dstate):
        @pl.when(pl.program_id(0) == 0)
        def _():
            dstate[...] = jnp.zeros_like(dstate)
            don_ref[...] = jnp.zeros_like(don_ref)

        gbv = gb_ref[...]
        tm = tm_ref[0]

        def chunk(qh, kh, vh, gcol, bcol, s0, zh, on):
            o, s1, _ = _delta_chunk(qh, kh, vh, gcol, bcol, s0, tm)
            return _gated_out(o, zh, on), s1

        _, vjp = jax.vjp(chunk, *_split_heads(q_ref, k_ref, v_ref, gbv, heads), st_ref[0],
                         _head_blocks(z_ref, heads), on_ref[...])
        dq, dk, dv, dg, db, ds0, dz, don = vjp((_head_blocks(dog_ref, heads).astype(F32), dstate[...]))
        dstate[...] = ds0
        don_ref[...] += don
        lane = lax.broadcasted_iota(jnp.int32, gbv.shape, 1)
        dgb = jnp.zeros(gbv.shape, F32)
        for h in range(heads):
            sl = slice(h * DN_HEAD_DIM, (h + 1) * DN_HEAD_DIM)
            dq_ref[:, sl] = dq[h]
            dk_ref[:, sl] = dk[h]
            dv_ref[:, sl] = dv[h]
            dz_ref[:, sl] = dz[h]
            dgb = dgb + jnp.where(lane == h, db[h], 0.0) + jnp.where(lane == heads + h, dg[h], 0.0)
        dgb_ref[...] = dgb

    return pl.pallas_call(
        body, name="dn_delta_bwd", grid=(n,),
        in_specs=[blk, blk, blk, gspec, pl.BlockSpec((DN_CHUNK, hd), lambda c: (n - 1 - c, 3)), nspec,
                  sspec, tspec, blk],
        out_specs=[blk, blk, blk, gspec, blk, nspec],
        out_shape=[jax.ShapeDtypeStruct((s, hd), F32)] * 3 + [jax.ShapeDtypeStruct((s, LANES), F32),
                                                              jax.ShapeDtypeStruct((s, hd), F32),
                                                              jax.ShapeDtypeStruct(onorm.shape, F32)],
        scratch_shapes=[pltpu.VMEM((heads, DN_HEAD_DIM, DN_HEAD_DIM), F32)],
        compiler_params=pltpu.CompilerParams(dimension_semantics=("arbitrary",)),
    )(q, k, v, gb, qkvz, onorm, states, tms, dog)


def _dev_index(px, py, pc):
    return 4 * px + 2 * py + pc


def _all_gather(arrs, name):
    n = len(arrs)

    def body(*refs):
        xs, outs = refs[:n], refs[n:2 * n]
        send_sems, recv_sems, local_sems = refs[2 * n:]
        x, y, c = lax.axis_index("x"), lax.axis_index("y"), lax.axis_index("c")
        me, sibling = (x, y, c), (x, y, 1 - c)
        chips = [(1 - x, y), (x, 1 - y), (1 - x, 1 - y)]

        def copy(a, kk, block, to, src=None):
            dst = outs[a].at[_dev_index(*block)]
            return pltpu.make_async_remote_copy(
                src_ref=dst if src is None else src, dst_ref=dst,
                send_sem=send_sems.at[a * 7 + kk], recv_sem=recv_sems.at[a * 7 + kk],
                device_id=to, device_id_type=MESH_IDS)

        mine = [pltpu.make_async_copy(xs[a], outs[a].at[_dev_index(*me)], local_sems.at[a]) for a in range(n)]
        for cp in mine:
            cp.start()
        first = []
        for a in range(n):
            first.append(copy(a, 0, me, sibling, src=xs[a]))
            first += [copy(a, 1 + j, me, (*chip, c), src=xs[a]) for j, chip in enumerate(chips)]
        for cp in first:
            cp.start()
        passed = []
        for j, chip in enumerate(chips):
            for a in range(n):
                copy(a, 1 + j, (*chip, c), me).wait_recv()
                fwd = copy(a, 4 + j, (*chip, c), sibling)
                fwd.start()
                passed.append(fwd)
        for a in range(n):
            copy(a, 0, sibling, me).wait_recv()
        for j, chip in enumerate(chips):
            for a in range(n):
                copy(a, 4 + j, (*chip, 1 - c), me).wait_recv()
        for cp in first + passed:
            cp.wait_send()
        for cp in mine:
            cp.wait()

    hbm = pl.BlockSpec(memory_space=pltpu.HBM)
    res = pl.pallas_call(
        body, name=name,
        in_specs=[hbm] * n, out_specs=[hbm] * n,
        out_shape=[jax.ShapeDtypeStruct((N_DEV,) + a.shape, a.dtype) for a in arrs],
        scratch_shapes=[pltpu.SemaphoreType.DMA((7 * n,)), pltpu.SemaphoreType.DMA((7 * n,)),
                        pltpu.SemaphoreType.DMA((n,))],
    )(*arrs)
    return list(res)


_FLIPS = ((0, 0, 1), (1, 0, 0), (0, 1, 0), (1, 1, 0), (1, 0, 1), (0, 1, 1), (1, 1, 1))
_HBM_SPEC = pl.BlockSpec(memory_space=pltpu.HBM)
_SEM_SPEC = pl.BlockSpec(memory_space=pltpu.SEMAPHORE)
_ANY_SPEC = pl.BlockSpec(memory_space=pl.ANY)
_DATAFLOW = pltpu.SideEffectType.DATAFLOW_SIDE_EFFECTING
TOKEN_SHAPE = (8, LANES)


def _mesh_me():
    return lax.axis_index("x"), lax.axis_index("y"), lax.axis_index("c")


def _flipped(me, f):
    return tuple(1 - v if fl else v for v, fl in zip(me, f))


def _exchange_copies(xs, lands, send_sems, recv_sems, scatter, landed):
    me = _mesh_me()
    cps = []
    for kk, f in enumerate(_FLIPS):
        p = _flipped(me, f)
        for a in range(len(xs)):
            cps.append(pltpu.make_async_remote_copy(
                src_ref=xs[a].at[_dev_index(*p)] if scatter else xs[a],
                dst_ref=lands[a].at[_dev_index(*(p if landed else me))],
                send_sem=send_sems.at[a * 7 + kk], recv_sem=recv_sems.at[a * 7 + kk],
                device_id=p, device_id_type=MESH_IDS))
    return cps


def _exchange_start(srcs, lands, scatter, name, after=()):
    n = len(srcs)

    n_after = len(after)

    def body(*refs):
        xs, ls = refs[:n], refs[n:2 * n]
        send_sems, recv_sems = refs[2 * n + n_after], refs[2 * n + n_after + 1]
        token = refs[-1]
        for cp in _exchange_copies(xs, ls, send_sems, recv_sems, scatter, landed=False):
            cp.start()
        token[...] = jnp.zeros_like(token)

    operands = [pltpu.with_memory_space_constraint(a, pltpu.HBM) for a in list(srcs) + list(lands)]
    res = pl.pallas_call(
        body, name=name,
        in_specs=[_HBM_SPEC] * (2 * n) + [_ANY_SPEC] * len(after),
        out_specs=[_SEM_SPEC, _SEM_SPEC] + [_HBM_SPEC] * (2 * n) + [pl.BlockSpec(memory_space=pltpu.VMEM)],
        out_shape=[pltpu.SemaphoreType.DMA((7 * n,)), pltpu.SemaphoreType.DMA((7 * n,))]
        + [pltpu.HBM(a.shape, a.dtype) for a in operands] + [jax.ShapeDtypeStruct(TOKEN_SHAPE, F32)],
        input_output_aliases={i: 2 + i for i in range(2 * n)},
        compiler_params=pltpu.CompilerParams(has_side_effects=_DATAFLOW),
    )(*operands, *after)
    return (res[0], res[1], list(res[2:2 + n]), list(res[2 + n:2 + 2 * n]), scatter, name), res[-1]


def _exchange_wait(handle, after):
    send_sems, recv_sems, srcs, lands, scatter, name = handle
    n = len(srcs)
    n_after = len(after)

    def body(*refs):
        xs, ls = refs[:n], refs[n:2 * n]
        send_sems_ref, recv_sems_ref = refs[2 * n], refs[2 * n + 1]
        for cp in _exchange_copies(xs, ls, send_sems_ref, recv_sems_ref, scatter, landed=True):
            cp.wait_send()
            cp.wait_recv()

    res = pl.pallas_call(
        body, name=name + "_wait",
        in_specs=[_HBM_SPEC] * (2 * n) + [_SEM_SPEC, _SEM_SPEC] + [_ANY_SPEC] * n_after,
        out_specs=[_HBM_SPEC] * (2 * n),
        out_shape=[pltpu.HBM(a.shape, a.dtype) for a in srcs + lands],
        input_output_aliases={i: i for i in range(2 * n)},
        compiler_params=pltpu.CompilerParams(has_side_effects=_DATAFLOW),
    )(*srcs, *lands, send_sems, recv_sems, *after)
    return list(res[:n]), list(res[n:])


def _slot_sum(g, name, tr):
    _, r, c = g.shape
    tr = min(tr, r)
    assert r % tr == 0

    def body(g_ref, o_ref):
        acc = g_ref[0].astype(F32)
        for s in range(1, N_DEV):
            acc = acc + g_ref[s].astype(F32)
        o_ref[...] = acc

    return pl.pallas_call(
        body, name=name, grid=(r // tr,),
        in_specs=[pl.BlockSpec((N_DEV, tr, c), lambda i: (0, i, 0))],
        out_specs=pl.BlockSpec((tr, c), lambda i: (i, 0)),
        out_shape=jax.ShapeDtypeStruct((r, c), F32),
        compiler_params=pltpu.CompilerParams(dimension_semantics=("parallel",)),
    )(g)


def _adam_update(w, gg, m, v):
    c1 = 1.0 / (1.0 - ADAM_B1 ** ADAM_STEP)
    c2 = 1.0 / (1.0 - ADAM_B2 ** ADAM_STEP)
    nm = ADAM_B1 * m + (1.0 - ADAM_B1) * gg
    nv = ADAM_B2 * v + (1.0 - ADAM_B2) * (gg * gg)
    return -ADAM_LR * ((nm * c1) / (jnp.sqrt(nv * c2) + ADAM_EPS) + ADAM_WD * w), nm, nv


def _adamw_reduce(me, recvs, owns, w, m, v, name, tr=256):
    nl, r, c = w.shape
    assert len(recvs) == nl and len(owns) == nl
    tr = min(tr, r)
    if r % tr == 0:
        tc, nblk = c, r // tr
        at = lambda i: (i, 0)
    else:
        tr, tc = r, min(c, 4 * LANES)
        assert c % tc == 0
        nblk = c // tc
        at = lambda i: (0, i)

    def parked(li, l, i):
        return jnp.where(l < li, 0, jnp.where(l > li, nblk - 1, i))

    def recv_spec(li):
        return pl.BlockSpec((N_DEV, tr, tc), lambda l, i, me_ref: (0, *at(parked(li, l, i))))

    def own_spec(li):
        return pl.BlockSpec((None, tr, tc), lambda l, i, me_ref: (me_ref[0], *at(parked(li, l, i))))

    def body(me_ref, *refs):
        rrefs, orefs = refs[:nl], refs[nl:2 * nl]
        w_ref, m_ref, v_ref, g_ref, d_ref, nm_ref, nv_ref = refs[2 * nl:]
        l = pl.program_id(0)

        def of_layer(vals):
            out = vals[0]
            for li in range(1, nl):
                out = jnp.where(l == li, vals[li], out)
            return out

        own = of_layer([o[...].astype(F32) for o in orefs])
        gg = None
        for s in range(N_DEV):
            slot = jnp.where(me_ref[0] == s, own, of_layer([rr[s].astype(F32) for rr in rrefs]))
            gg = slot if gg is None else gg + slot
        g_ref[...] = gg
        d_ref[...], nm_ref[...], nv_ref[...] = _adam_update(w_ref[...], gg, m_ref[...], v_ref[...])

    spec = pl.BlockSpec((None, tr, tc), lambda l, i, me_ref: (l, *at(i)))
    return pl.pallas_call(
        body, name=name,
        grid_spec=pltpu.PrefetchScalarGridSpec(
            num_scalar_prefetch=1, grid=(nl, nblk),
            in_specs=[recv_spec(li) for li in range(nl)] + [own_spec(li) for li in range(nl)] + [spec] * 3,
            out_specs=[spec] * 4),
        out_shape=[jax.ShapeDtypeStruct((nl, r, c), F32)] * 4,
        compiler_params=pltpu.CompilerParams(dimension_semantics=("arbitrary", "arbitrary")),
    )(me, *recvs, *owns, w, m, v)


def _adamw(w, g, m, v, name, tr=256):
    r, c = w.shape
    tr = min(tr, r)
    assert r % tr == 0

    def body(w_ref, g_ref, m_ref, v_ref, d_ref, nm_ref, nv_ref):
        d_ref[...], nm_ref[...], nv_ref[...] = _adam_update(w_ref[...], g_ref[...], m_ref[...], v_ref[...])

    spec = pl.BlockSpec((tr, c), lambda i: (i, 0))
    return pl.pallas_call(
        body, name=name, grid=(r // tr,), in_specs=[spec] * 4, out_specs=[spec] * 3,
        out_shape=[jax.ShapeDtypeStruct((r, c), F32)] * 3,
        compiler_params=pltpu.CompilerParams(dimension_semantics=("parallel",)),
    )(w, g, m, v)


def _rms_fwd(x, g, name, ts=512, after=()):
    s, d = x.shape

    def fn(first, last, xv, gv):
        return [_rms(xv, gv[...])], []

    return _rowwise(fn, n_rows=s, ts=ts, name=name, rows=[(x, 0, d)], vecs=[g], row_outs=[(d, BF16)],
                    after=after)[0]


def _rms_bwd(x, dn, dres, g, name, ts=256):
    s, d = x.shape

    def fn(first, last, xv, dnv, drv, gv):
        _, vjp = jax.vjp(_rms, xv, gv[...])
        dx, dg = vjp(dnv.astype(F32))
        return [drv + dx], [dg]

    return _rowwise(fn, n_rows=s, ts=ts, name=name, rows=[(x, 0, d), (dn, 0, d), (dres, 0, d)], vecs=[g],
                    row_outs=[(d, F32)], acc_outs=[(1, d)])


def _dn_pre_fwd(qkvz, ba, wconv, alog, dt, heads, ts=256):
    s = qkvz.shape[0]
    d3 = wconv.shape[1]
    d = d3 // 3

    def fn(first, last, xc, bav, xp, wv, av, dv):
        xext = jnp.concatenate([jnp.where(first, 0.0, xp), xc], axis=0)
        cv = _causal_conv(xext, wv, DN_CONV, DN_HALO - (DN_CONV - 1), xc.shape[0])
        return list(_dn_point(cv, bav, av[...], dv[...], heads)), []

    return _rowwise(fn, n_rows=s, ts=ts, name="dn_pre_fwd", rows=[(qkvz, 0, d3), (ba, 0, LANES)],
                    prevs=[(qkvz, 0, d3, DN_HALO)], vecs=[wconv, alog, dt],
                    row_outs=[(d, F32), (d, F32), (d, F32), (LANES, F32)])


def _dn_pre_bwd(qkvz, ba, wconv, alog, dt, dq, dk, dv, dgb, dz, heads, ts=256):
    s = qkvz.shape[0]
    d3 = wconv.shape[1]
    d = d3 // 3
    lead = DN_HALO - (DN_CONV - 1)

    def fn(first, last, xc, bac, dqc, dkc, dvc, dgbc, dzc, xp, xn, ban, dqn, dkn, dvn, dgbn, wv, av, dtv):
        n = xc.shape[0]
        ext = lambda cur, nxt: jnp.concatenate([cur, nxt], axis=0)
        live = lambda nxt: jnp.where(last, 0.0, nxt)
        xall = jnp.concatenate([jnp.where(first, 0.0, xp), xc, live(xn)], axis=0)
        cv = _causal_conv(xall, wv, DN_CONV, lead, n + DN_HALO)
        (_, _, _, gbv), vjp = jax.vjp(lambda c, b: _dn_point(c, b, av[...], dtv[...], heads), cv, ext(bac, ban))
        dc, dba = vjp((ext(dqc, live(dqn)), ext(dkc, live(dkn)), ext(dvc, live(dvn)), ext(dgbc, live(dgbn))))
        dx = None
        dw = []
        for j in range(DN_CONV):
            term = _shift_rows(dc, DN_CONV - 1 - j)[:n] * wv[j:j + 1, :]
            dx = term if dx is None else dx + term
            dw.append(_colsum(dc[:n] * _shift_rows(xall, lead + j)[:n]))
        dba = dba[:n]
        return ([jnp.concatenate([dx, dzc], axis=-1), dba],
                [_stack_rows(dw, DN_CONV), _colsum(dgbc * gbv[:n]), _colsum(dba)])

    return _rowwise(fn, n_rows=s, ts=ts, name="dn_pre_bwd",
                    rows=[(qkvz, 0, d3), (ba, 0, LANES), (dq, 0, d), (dk, 0, d), (dv, 0, d), (dgb, 0, LANES),
                          (dz, 0, d)],
                    prevs=[(qkvz, 0, d3, DN_HALO)],
                    nexts=[(qkvz, 0, d3, DN_HALO), (ba, 0, LANES, DN_HALO), (dq, 0, d, DN_HALO),
                           (dk, 0, d, DN_HALO), (dv, 0, d, DN_HALO), (dgb, 0, LANES, DN_HALO)],
                    vecs=[wconv, alog, dt],
                    row_outs=[(4 * d, BF16), (LANES, BF16)], acc_outs=[(DN_CONV, d3), (1, LANES), (1, LANES)])


def _cv_mid_fwd(u, wdw, bdw, lng, lnb, ts=512):
    s = u.shape[0]
    d = u.shape[1] // 2

    def fn(first, last, uc, up, wv, bv, gv, lbv):
        uext = jnp.concatenate([jnp.where(first, 0.0, up), uc], axis=0)
        glu = uext[:, :d] * _sigmoid(uext[:, d:])
        c = _causal_conv(glu, wv, CV_WIDTH, CV_HALO - (CV_WIDTH - 1), uc.shape[0]) + bv[...]
        return [c, _ln_silu(c, gv[...], lbv[...])], []

    return _rowwise(fn, n_rows=s, ts=ts, name="cv_mid_fwd", rows=[(u, 0, 2 * d)], prevs=[(u, 0, 2 * d, CV_HALO)],
                    vecs=[wdw, bdw, lng, lnb], row_outs=[(d, F32), (d, BF16)])


def _cv_mid_bwd2(dc, u, wdw, ts=512):
    s, d = dc.shape

    def fn(first, last, dcc, uc, up, dcn, wv):
        n = dcc.shape[0]
        dcext = jnp.concatenate([dcc, jnp.where(last, 0.0, dcn)], axis=0)
        uext = jnp.concatenate([jnp.where(first, 0.0, up), uc], axis=0)
        glu = uext[:, :d] * _sigmoid(uext[:, d:])
        dglu = None
        dw = []
        for j in range(CV_WIDTH):
            term = _shift_rows(dcext, CV_WIDTH - 1 - j)[:n] * wv[j:j + 1, :]
            dglu = term if dglu is None else dglu + term
            dw.append(_colsum(dcc * _shift_rows(glu, CV_HALO - (CV_WIDTH - 1) + j)[:n]))
        u1, sg = uc[:, :d], _sigmoid(uc[:, d:])
        du = jnp.concatenate([dglu * sg, dglu * u1 * sg * (1.0 - sg)], axis=-1)
        return [du], [_stack_rows(dw, CV_HALO), _colsum(du)]

    return _rowwise(fn, n_rows=s, ts=ts, name="cv_mid_bwd2", rows=[(dc, 0, d), (u, 0, 2 * d)],
                    prevs=[(u, 0, 2 * d, CV_HALO)], nexts=[(dc, 0, d, CV_HALO)], vecs=[wdw],
                    row_outs=[(2 * d, BF16)], acc_outs=[(CV_HALO, d), (1, 2 * d)])


def _attn_fwd(q, k, v, name, ts=512):
    s, d = q.shape

    def fn(first, last, qv, kv, vv):
        return [_attn_tile(qv.astype(F32), kv[...].astype(F32), vv[...].astype(F32))], []

    return _rowwise(fn, n_rows=s, ts=ts, name=name, rows=[(q, 0, d)], vecs=[k, v], row_outs=[(d, BF16)])[0]


def _attn_bwd(q, k, v, do, name, ts=512):
    s, d = q.shape
    m = k.shape[0]

    def fn(first, last, qv, dov, kv, vv):
        _, vjp = jax.vjp(_attn_tile, qv.astype(F32), kv[...].astype(F32), vv[...].astype(F32))
        dq, dk, dv = vjp(dov.astype(F32))
        return [dq], [dk, dv]

    return _rowwise(fn, n_rows=s, ts=ts, name=name, rows=[(q, 0, d), (do, 0, d)], vecs=[k, v],
                    row_outs=[(d, BF16)], acc_outs=[(m, d), (m, d)])


def _pad_lanes(a, off=0):
    r, n = a.shape
    return jnp.pad(a, ((0, 0), (off, LANES - off - n)))


def _local_step(x, mem, tgt, w, fetch=None, emit=None, first_after=()):
    s, d = x.shape
    heads = d // DN_HEAD_DIM
    g = {}
    if fetch is None:
        fetch = lambda group, after: None
    if emit is None:
        emit = lambda group, grads: ()

    def add_res(acc, res):
        return (res + acc,)

    def add_res_rms(acc, res, gain):
        h = res + acc
        return h, _rms(h, gain)

    def rms_bwd_epi(acc, hx, dres, gain):
        _, vjp = jax.vjp(_rms, hx, gain)
        dx, dg = vjp(acc)
        return dres + dx, dg

    w_int = w["dn_w_in"][0]
    assert w_int.shape[0] == 4 * d + 2 * heads
    w_bat = jnp.pad(w_int[4 * d:], ((0, LANES - 2 * heads), (0, 0)))
    dn_norm = w["dn_norm"]
    alog = _pad_lanes(w["dn_a_log"], heads)
    dtb = _pad_lanes(w["dn_dt_bias"], heads)
    wconv = w["dn_w_conv"][0]
    n0 = _rms_fwd(x, dn_norm, "dn_rms", after=first_after)
    qkvz = _matmul(n0, w_int, "nt", [F32], name="dn_in_proj", b_rows=4 * d)
    ba = _matmul(n0, w_bat, "nt", [F32], name="dn_in_proj_ba")
    q, k, v, gb = _dn_pre_fwd(qkvz, ba, wconv, alog, dtb, heads)
    og, states, tms = _delta_fwd(q, k, v, gb, qkvz, w["dn_out_norm"], heads)
    fetch(1, [og])
    h1, nq0 = _matmul(og, w["dn_w_out"][0], "nn", [F32, BF16], name="dn_out_proj", epi=add_res_rms,
                      mn_extras=[x], row_extras=[w["xa_norm"][0:1]], slab=EPI_SLAB)

    def xattn_fwd(h, nq, layer, next_gain):
        qx = _matmul(nq, w["xa_w_q"][layer], "nn", [BF16], name=f"xa{layer}_q")
        mn = _rms_fwd(mem, w["xa_mem_norm"][layer:layer + 1], f"xa{layer}_mem_rms")
        kv = _matmul(mn, w["xa_w_kv"][layer], "nn", [BF16], name=f"xa{layer}_kv")
        kx, vx = kv[:, :d], kv[:, d:]
        ox = _attn_fwd(qx, kx, vx, f"xa{layer}_attn")
        hn, nn = _matmul(ox, w["xa_w_o"][layer], "nn", [F32, BF16], name=f"xa{layer}_o", epi=add_res_rms,
                         mn_extras=[h], row_extras=[next_gain], slab=EPI_SLAB)
        return hn, nn, (h, nq, qx, mn, kx, vx, ox)

    def sq_relu(t):
        r = jnp.maximum(t.astype(F32), 0.0)
        return r * r

    def loss_epi(acc, res, target, gain):
        def cols(hh, gg):
            e = _rms(hh, gg) - target
            return _colsum(e * e) * (0.5 / d)

        per_col, vjp = jax.vjp(cols, res + acc, gain)
        dhx, dgain = vjp(jnp.ones_like(per_col))
        return dhx, dgain, per_col

    def mlp_fwd(h, nm, layer, next_gain):
        u = _matmul(nm, w["mlp_w_up"][layer], "nn", [BF16], name=f"mlp{layer}_up")
        if next_gain is None:
            hn, *nn = _matmul(u, w["mlp_w_down"][layer], "nn", [F32], name=f"mlp{layer}_down_loss",
                              epi=loss_epi, mn_extras=[h, tgt], row_extras=[w["final_norm"].reshape(1, d)],
                              n_rowsum=2, tk=MM_DEEP, slab=EPI_SLAB, a_pre=sq_relu)
        else:
            hn, nn = _matmul(u, w["mlp_w_down"][layer], "nn", [F32, BF16], name=f"mlp{layer}_down",
                             epi=add_res_rms, mn_extras=[h], row_extras=[next_gain], tk=MM_DEEP, slab=EPI_SLAB,
                             a_pre=sq_relu)
        return hn, nn, (h, nm, u)

    h2, nm0, xa0 = xattn_fwd(h1, nq0, 0, w["mlp_norm"][0:1])
    fetch(2, [h2])
    h3, n1, mlp0 = mlp_fwd(h2, nm0, 0, w["cv_norm"])

    u_cv = _matmul(n1, w["cv_w_pw1"][0], "nn", [F32], name="cv_pw1", epi=lambda acc, b: (acc + b,),
                   row_extras=[w["cv_b_pw1"]])
    wdw = jnp.pad(w["cv_w_dw"][0], ((0, CV_HALO - CV_WIDTH), (0, 0)))
    c_cv, s_cv = _cv_mid_fwd(u_cv, wdw, w["cv_b_dw"], w["cv_ln_g"], w["cv_ln_b"])
    h4, nq1 = _matmul(s_cv, w["cv_w_pw2"][0], "nn", [F32, BF16], name="cv_pw2",
                      epi=lambda acc, res, b, gain: add_res_rms(acc + b, res, gain), mn_extras=[h3],
                      row_extras=[w["cv_b_pw2"], w["xa_norm"][1:2]], slab=EPI_SLAB)
    fetch(3, [h4])
    h5, nm1, xa1 = xattn_fwd(h4, nq1, 1, w["mlp_norm"][1:2])
    dh, (g_fn, loss_cols), mlp1 = mlp_fwd(h5, nm1, 1, None)
    g["final_norm"] = g_fn.reshape(d)
    loss = jnp.sum(loss_cols, axis=1, keepdims=True)

    def mlp_bwd(dh, layer, saved, after=()):
        h, nm, u = saved
        du = _matmul(dh, w["mlp_w_down"][layer], "nt", [BF16], name=f"mlp{layer}_down_dx", after=after,
                     epi=lambda acc, uu: (acc * 2.0 * jnp.maximum(uu.astype(F32), 0.0),), mn_extras=[u])
        gdown = _matmul(u, dh, "tn", [BF16], name=f"mlp{layer}_down_dw", tm=MM_DEEP, a_pre=sq_relu)
        dhn, gn = _matmul(du, w["mlp_w_up"][layer], "nt", [F32], name=f"mlp{layer}_up_dx", epi=rms_bwd_epi,
                          mn_extras=[h, dh], row_extras=[w["mlp_norm"][layer:layer + 1]], n_rowsum=1,
                          slab=EPI_SLAB)
        gup = _matmul(nm, du, "tn", [BF16], name=f"mlp{layer}_up_dw", out_dm=True, tk=MM_DEEP)
        return dhn, gup, gdown, gn

    def xattn_bwd(dh, layer, saved):
        h, nq, qx, mn, kx, vx, ox = saved
        dox = _matmul(dh, w["xa_w_o"][layer], "nt", [BF16], name=f"xa{layer}_o_dx")
        go = _matmul(ox, dh, "tn", [BF16], name=f"xa{layer}_o_dw", tk=MM_DEEP)
        dqx, dkx, dvx = _attn_bwd(qx, kx, vx, dox, f"xa{layer}_attn_bwd")

        def epi(acc, hx, dres, gain):
            dhx, dg = rms_bwd_epi(acc, hx, dres, gain)
            return dhx, dg, _colsum(dhx)

        dhn, gn, dh_cols = _matmul(dqx, w["xa_w_q"][layer], "nt", [F32], name=f"xa{layer}_q_dx", epi=epi,
                                   mn_extras=[h, dh], row_extras=[w["xa_norm"][layer:layer + 1]], n_rowsum=2,
                                   slab=EPI_SLAB)
        gq = _matmul(nq, dqx, "tn", [BF16], name=f"xa{layer}_q_dw", tk=MM_DEEP)
        dkv = jnp.concatenate([dkx, dvx], axis=-1)
        gkv = _matmul(mn, dkv, "tn", [BF16], name=f"xa{layer}_kv_dw", out_dm=True)
        dmn = _matmul(dkv, w["xa_w_kv"][layer], "nt", [F32], name=f"xa{layer}_kv_dx", tk=MM_DEEP)
        _, gmem = _rms_bwd(mem, dmn, dmn, w["xa_mem_norm"][layer:layer + 1], f"xa{layer}_mem_rms_bwd")
        return dhn, gq, gkv, go, gn, gmem, dh_cols

    dh, gup1, gdown1, gmn1 = mlp_bwd(dh, 1, mlp1)
    dh, gq1, gkv1, go1, gxn1, gmem1, g_b2 = xattn_bwd(dh, 1, xa1)
    g.update(mlp_w_up=[None, gup1], mlp_w_down=[None, gdown1], xa_w_q=[None, gq1], xa_w_kv=[None, gkv1],
             xa_w_o=[None, go1])
    tok = emit(3, g)

    def ln_bwd_epi(acc, cx, gain, bias):
        _, vjp = jax.vjp(_ln_silu, cx, gain, bias)
        dc, dg, db = vjp(acc)
        return dc, dg, db, _colsum(dc)

    dc_cv, g_lng, g_lnb, g_bdw = _matmul(dh, w["cv_w_pw2"][0], "nt", [F32], name="cv_pw2_dx", after=tok,
                                        epi=ln_bwd_epi, mn_extras=[c_cv],
                                        row_extras=[w["cv_ln_g"], w["cv_ln_b"]], n_rowsum=3, slab=EPI_SLAB)
    g["cv_w_pw2"] = [_matmul(s_cv, dh, "tn", [BF16], name="cv_pw2_dw", tk=MM_DEEP)]
    du_cv, g_wdw, g_b1 = _cv_mid_bwd2(dc_cv, u_cv, wdw)
    g["cv_w_pw1"] = [_matmul(n1, du_cv, "tn", [BF16], name="cv_pw1_dw", out_dm=True, tk=MM_DEEP)]
    dh, g_cvn = _matmul(du_cv, w["cv_w_pw1"][0], "nt", [F32], name="cv_pw1_dx", epi=rms_bwd_epi,
                        mn_extras=[h3, dh], row_extras=[w["cv_norm"]], n_rowsum=1, slab=EPI_SLAB)
    g.update(cv_ln_g=g_lng, cv_ln_b=g_lnb, cv_b_dw=g_bdw, cv_b_pw2=g_b2, cv_b_pw1=g_b1, cv_norm=g_cvn,
             cv_w_dw=g_wdw[:CV_WIDTH][None])

    tok = emit(2, g)
    dh, gup0, gdown0, gmn0 = mlp_bwd(dh, 0, mlp0, after=tok)
    dh, gq0, gkv0, go0, gxn0, gmem0, _ = xattn_bwd(dh, 0, xa0)
    g["mlp_w_up"][0] = gup0
    g["mlp_w_down"][0] = gdown0
    g["mlp_norm"] = jnp.concatenate([gmn0, gmn1], axis=0)
    g["xa_w_q"][0] = gq0
    g["xa_w_kv"][0] = gkv0
    g["xa_w_o"][0] = go0
    g["xa_norm"] = jnp.concatenate([gxn0, gxn1], axis=0)
    g["xa_mem_norm"] = jnp.concatenate([gmem0, gmem1], axis=0)
    tok = emit(1, g)

    dog = _matmul(dh, w["dn_w_out"][0], "nt", [BF16], name="dn_out_proj_dx", after=tok)
    g["dn_w_out"] = [_matmul(og, dh, "tn", [BF16], name="dn_out_proj_dw", tk=MM_DEEP)]
    dq, dk, dv, dgb, dz, g_on = _delta_bwd(q, k, v, gb, qkvz, w["dn_out_norm"], states, tms, dog, heads)
    dqkvz, dba, g_wconv, g_alog, g_dt = _dn_pre_bwd(qkvz, ba, wconv, alog, dtb, dq, dk, dv, dgb, dz, heads)
    g_qkvzt = _matmul(dqkvz, n0, "tn", [BF16], name="dn_in_proj_dw", tk=MM_DEEP)
    g_bat = _matmul(dba, n0, "tn", [BF16], name="dn_in_proj_ba_dw", tk=MM_DEEP)
    g["dn_w_in"] = [jnp.concatenate([g_qkvzt, g_bat[:2 * heads]], axis=0)]
    g["dn_w_conv"] = g_wconv[None]
    tok = emit(0, g)
    dn0a = _matmul(dba, w_bat, "nn", [F32], name="dn_in_proj_ba_dx", after=tok)
    grad_x, g_dnn = _matmul(dqkvz, w_int, "nn", [F32], name="dn_in_proj_dx", b_rows=4 * d,
                            epi=lambda acc, part, hx, dres, gain: rms_bwd_epi(acc + part, hx, dres, gain),
                            mn_extras=[dn0a, x, dh], row_extras=[dn_norm], n_rowsum=1, slab=EPI_SLAB)
    g.update(dn_norm=g_dnn, dn_out_norm=g_on,
             dn_a_log=g_alog[:, heads:2 * heads], dn_dt_bias=g_dt[:, heads:2 * heads])
    return loss, grad_x, g


def _round_up(n, m):
    return (n + m - 1) // m * m


def _pack_rows(parts, cols, row_mult):
    lead = parts[0].shape[:-1]
    flat, offs, off = [], [], 0
    for p in parts:
        n = _round_up(p.shape[-1], cols)
        flat.append(jnp.pad(p, [(0, 0)] * len(lead) + [(0, n - p.shape[-1])]))
        offs.append(off)
        off += n
    total = _round_up(off, cols * row_mult)
    if total > off:
        flat.append(jnp.zeros(lead + (total - off,), parts[0].dtype))
    return jnp.concatenate(flat, axis=-1).reshape(lead + (total // cols, cols)), offs


def _unpack(packed, offs, shapes):
    lead = packed.shape[:-2]
    flat = packed.reshape(lead + (-1,))
    out = []
    for off, shp in zip(offs, shapes):
        n = 1
        for v in shp:
            n *= v
        out.append(flat[..., off:off + n].reshape(lead + tuple(shp)))
    return out


def kernel(x, mem, dn_norm, dn_w_in, dn_w_conv, dn_a_log, dn_dt_bias, dn_out_norm, dn_w_out, cv_norm, cv_w_pw1, cv_b_pw1, cv_w_dw, cv_b_dw, cv_ln_g, cv_ln_b, cv_w_pw2, cv_b_pw2, xa_norm, xa_mem_norm, xa_w_q, xa_w_kv, xa_w_o, mlp_norm, mlp_w_up, mlp_w_down, final_norm, loss_target, m_dn_norm, m_dn_w_in, m_dn_w_conv, m_dn_a_log, m_dn_dt_bias, m_dn_out_norm, m_dn_w_out, m_cv_norm, m_cv_w_pw1, m_cv_b_pw1, m_cv_w_dw, m_cv_b_dw, m_cv_ln_g, m_cv_ln_b, m_cv_w_pw2, m_cv_b_pw2, m_xa_norm, m_xa_mem_norm, m_xa_w_q, m_xa_w_kv, m_xa_w_o, m_mlp_norm, m_mlp_w_up, m_mlp_w_down, m_final_norm, v_dn_norm, v_dn_w_in, v_dn_w_conv, v_dn_a_log, v_dn_dt_bias, v_dn_out_norm, v_dn_w_out, v_cv_norm, v_cv_w_pw1, v_cv_b_pw1, v_cv_w_dw, v_cv_b_dw, v_cv_ln_g, v_cv_ln_b, v_cv_w_pw2, v_cv_b_pw2, v_xa_norm, v_xa_mem_norm, v_xa_w_q, v_xa_w_kv, v_xa_w_o, v_mlp_norm, v_mlp_w_up, v_mlp_w_down, v_final_norm):
    wsh = dict(dn_norm=dn_norm, dn_w_in=dn_w_in, dn_w_conv=dn_w_conv, dn_a_log=dn_a_log, dn_dt_bias=dn_dt_bias, dn_out_norm=dn_out_norm, dn_w_out=dn_w_out, cv_norm=cv_norm, cv_w_pw1=cv_w_pw1, cv_b_pw1=cv_b_pw1, cv_w_dw=cv_w_dw, cv_b_dw=cv_b_dw, cv_ln_g=cv_ln_g, cv_ln_b=cv_ln_b, cv_w_pw2=cv_w_pw2, cv_b_pw2=cv_b_pw2, xa_norm=xa_norm, xa_mem_norm=xa_mem_norm, xa_w_q=xa_w_q, xa_w_kv=xa_w_kv, xa_w_o=xa_w_o, mlp_norm=mlp_norm, mlp_w_up=mlp_w_up, mlp_w_down=mlp_w_down, final_norm=final_norm)
    msh = dict(dn_norm=m_dn_norm, dn_w_in=m_dn_w_in, dn_w_conv=m_dn_w_conv, dn_a_log=m_dn_a_log, dn_dt_bias=m_dn_dt_bias, dn_out_norm=m_dn_out_norm, dn_w_out=m_dn_w_out, cv_norm=m_cv_norm, cv_w_pw1=m_cv_w_pw1, cv_b_pw1=m_cv_b_pw1, cv_w_dw=m_cv_w_dw, cv_b_dw=m_cv_b_dw, cv_ln_g=m_cv_ln_g, cv_ln_b=m_cv_ln_b, cv_w_pw2=m_cv_w_pw2, cv_b_pw2=m_cv_b_pw2, xa_norm=m_xa_norm, xa_mem_norm=m_xa_mem_norm, xa_w_q=m_xa_w_q, xa_w_kv=m_xa_w_kv, xa_w_o=m_xa_w_o, mlp_norm=m_mlp_norm, mlp_w_up=m_mlp_w_up, mlp_w_down=m_mlp_w_down, final_norm=m_final_norm)
    vsh = dict(dn_norm=v_dn_norm, dn_w_in=v_dn_w_in, dn_w_conv=v_dn_w_conv, dn_a_log=v_dn_a_log, dn_dt_bias=v_dn_dt_bias, dn_out_norm=v_dn_out_norm, dn_w_out=v_dn_w_out, cv_norm=v_cv_norm, cv_w_pw1=v_cv_w_pw1, cv_b_pw1=v_cv_b_pw1, cv_w_dw=v_cv_w_dw, cv_b_dw=v_cv_b_dw, cv_ln_g=v_cv_ln_g, cv_ln_b=v_cv_ln_b, cv_w_pw2=v_cv_w_pw2, cv_b_pw2=v_cv_b_pw2, xa_norm=v_xa_norm, xa_mem_norm=v_xa_mem_norm, xa_w_q=v_xa_w_q, xa_w_kv=v_xa_w_kv, xa_w_o=v_xa_w_o, mlp_norm=v_mlp_norm, mlp_w_up=v_mlp_w_up, mlp_w_down=v_mlp_w_down, final_norm=v_final_norm)

    big_axis = dict(BIG)
    for src in (wsh, msh, vsh):
        src["dn_w_in"] = jnp.swapaxes(src["dn_w_in"], 1, 2)
    big_axis["dn_w_in"] = 1

    small_pack, small_offs = _pack_rows([wsh[nm].reshape(-1) for nm in SMALL_SH], LANES, 8)
    w = {nm: [None] * wsh[nm].shape[0] for nm in big_axis}

    def put_weights(group, gathered):
        for (nm, layer), gth in zip(group, gathered):
            if big_axis[nm] == 1:
                w[nm][layer] = gth.reshape(N_DEV * gth.shape[1], gth.shape[2])
            else:
                w[nm][layer] = gth

    first = _all_gather([wsh[nm][layer].astype(BF16) for nm, layer in GATHER_GROUPS[0]] + [small_pack],
                        "weights_all_gather_0")
    put_weights(GATHER_GROUPS[0], first)
    me = _dev_index(*_mesh_me())
    gather_handles, tokens = {}, []
    for gi in range(1, len(GATHER_GROUPS)):
        shards = [wsh[nm][layer].astype(BF16) for nm, layer in GATHER_GROUPS[gi]]
        lands = [lax.dynamic_update_slice(lax.empty((N_DEV,) + s.shape, s.dtype), s[None], (me, 0, 0))
                 for s in shards]
        gather_handles[gi], tok = _exchange_start(shards, lands, False, f"weights_gather_{gi}",
                                                  after=[first[-1]] + tokens)
        tokens.append(tok)
    for nm, gth in zip(SMALL_SH, _unpack(first[-1], small_offs, [wsh[nm].shape for nm in SMALL_SH])):
        w[nm] = jnp.moveaxis(gth, 0, -2).reshape(gth.shape[1:-1] + (N_DEV * gth.shape[-1],))
    for nm in REPL:
        w[nm] = wsh[nm]

    def fetch(gi, after):
        put_weights(GATHER_GROUPS[gi], _exchange_wait(gather_handles[gi], after)[1])

    scatter_handles = {}

    def emit(gi, g):
        blocks = []
        for nm, layer in SCATTER_GROUPS[gi]:
            gw = g[nm][layer]
            if big_axis[nm] == 1:
                gw = gw.reshape(N_DEV, gw.shape[0] // N_DEV, gw.shape[1])
            blocks.append(gw)
        if gi == 0:
            gsmall_pack, _ = _pack_rows(
                [jnp.moveaxis(g[nm].reshape(g[nm].shape[:-1] + (N_DEV, -1)), -2, 0).reshape(N_DEV, -1)
                 for nm in SMALL_SH], LANES, 8)
            blocks.append(gsmall_pack)
        lands = [lax.empty(b.shape, b.dtype) for b in blocks]
        scatter_handles[gi], tok = _exchange_start(blocks, lands, True, f"grads_scatter_{gi}")
        return [tok]

    loss_part, grad_x, g = _local_step(x[0], mem[0], loss_target[0], w, fetch, emit, tokens)

    recv = {nm: [None] * wsh[nm].shape[0] for nm in big_axis}
    sent = {nm: [None] * wsh[nm].shape[0] for nm in big_axis}
    gsh, delta, new_m, new_v = {}, {}, {}, {}
    after = [grad_x]
    done = set()
    me_arr = me.astype(jnp.int32).reshape(1)
    def small_adamw(names, name):
        packs = []
        for src in (wsh, gsh, msh, vsh):
            pk, offs = _pack_rows([src[nm].reshape(-1) for nm in names], LANES, 8)
            packs.append(pk)
        outs = _adamw(*packs, name)
        for dst, pk in zip((delta, new_m, new_v), outs):
            for nm, val in zip(names, _unpack(pk, offs, [wsh[nm].shape for nm in names])):
                dst[nm] = val
        return outs[0]

    for gi in reversed(range(len(SCATTER_GROUPS))):
        if gi == 0:
            repl_pack, repl_offs = _pack_rows([g[nm].reshape(-1) for nm in REPL] + [loss_part[:, :1].reshape(-1)],
                                              LANES, 8)
            (repl_all,) = _all_gather([repl_pack], "repl_grads_all_gather")
            repl_red = _slot_sum(repl_all, "repl_grads_sum", SLOT_SUM_ROWS)
            *repl_vals, loss_sum = _unpack(repl_red, repl_offs, [wsh[nm].shape for nm in REPL] + [(1,)])
            for nm, val in zip(REPL, repl_vals):
                gsh[nm] = val
            after = after + [small_adamw(list(REPL), "adamw_repl")]
        sources, landed = _exchange_wait(scatter_handles[gi], after)
        for (nm, layer), src, r in zip(SCATTER_GROUPS[gi], sources, landed):
            sent[nm][layer], recv[nm][layer] = src, r
        if gi == 0:
            slot = lax.broadcasted_iota(jnp.int32, landed[-1].shape, 0)
            rsmall = jnp.where(slot == me, sources[-1], landed[-1])
        for nm in big_axis:
            if nm not in done and all(r is not None for r in recv[nm]):
                gsh[nm], delta[nm], new_m[nm], new_v[nm] = _adamw_reduce(
                    me_arr, recv[nm], sent[nm], wsh[nm], msh[nm], vsh[nm], f"adamw_{nm}")
                done.add(nm)
                after = [delta[nm]]
    gsmall_red = _slot_sum(rsmall, "grads_small_sum", SLOT_SUM_ROWS)
    for nm, val in zip(SMALL_SH, _unpack(gsmall_red, small_offs, [wsh[nm].shape for nm in SMALL_SH])):
        gsh[nm] = val
    small_adamw(list(SMALL_SH), "adamw_small")
    for dst in (gsh, delta, new_m, new_v):
        dst["dn_w_in"] = jnp.swapaxes(dst["dn_w_in"], 1, 2)
    return (loss_sum.reshape(()), grad_x[None], *[gsh[nm] for nm in WEIGHTS], *[delta[nm] for nm in WEIGHTS],
            *[new_m[nm] for nm in WEIGHTS], *[new_v[nm] for nm in WEIGHTS])
```

```python
import functools

import jax
import jax.numpy as jnp
from jax import lax
from jax.experimental import pallas as pl
from jax.experimental.pallas import tpu as pltpu

F32 = jnp.float32
BF16 = jnp.bfloat16
MESH_IDS = pl.DeviceIdType.MESH

N_DEV = 8
LANES = 128
RMS_EPS = 1e-6
LN_EPS = 1e-5
DN_HEAD_DIM = 128
DN_CONV = 4
DN_CHUNK = 64
CV_WIDTH = 31
XA_HEADS = 4
MM_TILE = 1024
MM_DEEP = 2048
EPI_SLAB = 256
SLOT_SUM_ROWS = 512
DN_HALO = 8
CV_HALO = 32

ADAM_LR = 0.001
ADAM_B1 = 0.9
ADAM_B2 = 0.999
ADAM_EPS = 1e-08
ADAM_WD = 0.01
ADAM_STEP = 10

BIG = (("dn_w_in", 2), ("dn_w_out", 1), ("cv_w_pw1", 2), ("cv_w_pw2", 1), ("xa_w_q", 1), ("xa_w_kv", 2),
       ("xa_w_o", 1), ("mlp_w_up", 2), ("mlp_w_down", 1))
_LAYER_GROUP = ("xa_w_q", "xa_w_o", "mlp_w_down", "xa_w_kv", "mlp_w_up")
GATHER_GROUPS = (
    (("dn_w_in", 0),),
    (("dn_w_out", 0),) + tuple((nm, 0) for nm in _LAYER_GROUP),
    (("cv_w_pw2", 0), ("cv_w_pw1", 0)),
    tuple((nm, 1) for nm in _LAYER_GROUP),
)
SCATTER_GROUPS = (
    (("dn_w_out", 0), ("dn_w_in", 0)),
    tuple((nm, 0) for nm in _LAYER_GROUP),
    (("cv_w_pw2", 0), ("cv_w_pw1", 0)),
    tuple((nm, 1) for nm in _LAYER_GROUP),
)
SMALL_SH = ("cv_norm", "cv_b_pw1", "cv_b_dw", "cv_ln_g", "cv_ln_b", "cv_b_pw2", "cv_w_dw", "dn_w_conv")
REPL = ("dn_norm", "dn_a_log", "dn_dt_bias", "dn_out_norm", "xa_norm", "xa_mem_norm", "mlp_norm", "final_norm")
WEIGHTS = ("dn_norm", "dn_w_in", "dn_w_conv", "dn_a_log", "dn_dt_bias", "dn_out_norm", "dn_w_out", "cv_norm",
           "cv_w_pw1", "cv_b_pw1", "cv_w_dw", "cv_b_dw", "cv_ln_g", "cv_ln_b", "cv_w_pw2", "cv_b_pw2", "xa_norm",
           "xa_mem_norm", "xa_w_q", "xa_w_kv", "xa_w_o", "mlp_norm", "mlp_w_up", "mlp_w_down", "final_norm")


def _dot_dims(mode, batched):
    o = 1 if batched else 0
    contract = {"nn": ((1 + o,), (o,)), "nt": ((1 + o,), (1 + o,)), "tn": ((o,), (o,))}[mode]
    return (contract, (((0,), (0,)) if batched else ((), ())))


def _bdot(a, b, mode):
    return lax.dot_general(a.astype(BF16), b.astype(BF16), _dot_dims(mode, a.ndim == 3),
                           preferred_element_type=F32)


@functools.partial(jax.custom_vjp, nondiff_argnums=(2,))
def _mm(a, b, mode):
    return _bdot(a, b, mode)


def _mm_fwd(a, b, mode):
    return _bdot(a, b, mode), (a, b)


def _mm_bwd(mode, res, ct):
    a, b = res
    if mode == "nn":
        da, db = _bdot(ct, b, "nt"), _bdot(a, ct, "tn")
    elif mode == "nt":
        da, db = _bdot(ct, b, "nn"), _bdot(ct, a, "tn")
    else:
        da, db = _bdot(b, ct, "nt"), _bdot(a, ct, "nn")
    return da.astype(a.dtype), db.astype(b.dtype)


_mm.defvjp(_mm_fwd, _mm_bwd)


def _sigmoid(x):
    return 0.5 * (jnp.tanh(0.5 * x) + 1.0)


def _silu(x):
    return x * _sigmoid(x)


def _softplus(x):
    return jnp.maximum(x, 0.0) + jnp.log(1.0 + jnp.exp(-jnp.abs(x)))


def _rms(x, g):
    r = lax.rsqrt(jnp.mean(x * x, axis=-1, keepdims=True) + RMS_EPS)
    return x * r * g


def _shift_rows(x, off):
    if off == 0:
        return x
    return pltpu.roll(x, x.shape[0] - off, 0)


def _series_dot(a, b, mode):
    return _bdot(a, b, mode)


def _chunk_masks(c):
    ii = lax.broadcasted_iota(jnp.int32, (c, c), 0)
    jj = lax.broadcasted_iota(jnp.int32, (c, c), 1)
    return (ii == jj).astype(F32), ii >= jj, ii > jj


def _neumann_inverse(lm):
    n = lm.shape[-1]
    t = -lm
    p = lm
    size = 2
    while size < n:
        size *= 2
        p = _series_dot(p, p, "nn")
        t = t + p + _series_dot(t, p, "nn")
    return t


def _apply_inverse(tm, rhs, mode):
    return rhs + _series_dot(tm, rhs, mode)


@jax.custom_vjp
def _unit_lower_solve(lm, rhs, tm):
    return _apply_inverse(tm, rhs, "nn")


def _uls_fwd(lm, rhs, tm):
    sol = _apply_inverse(tm, rhs, "nn")
    return sol, (tm, sol)


def _uls_bwd(res, ct):
    tm, sol = res
    d_rhs = _apply_inverse(tm, ct, "tn")
    return -_bdot(d_rhs, sol, "nt"), d_rhs, jnp.zeros_like(tm)


_unit_lower_solve.defvjp(_uls_fwd, _uls_bwd)


def _delta_chunk(q, k, v, gcol, bcol, s0, tm=None):
    c = q.shape[1]
    eye, causal, strict = _chunk_masks(c)
    grow = jnp.sum(eye * gcol, axis=1, keepdims=True)
    gc = jnp.sum(jnp.where(causal, grow, 0.0), axis=2, keepdims=True)
    gc_row = jnp.sum(eye * gc, axis=1, keepdims=True)
    decay = jnp.exp(jnp.where(causal, gc - gc_row, -jnp.inf))
    kb = k * bcol
    on_k = _mm(jnp.concatenate([kb, q], axis=1), k, "nt")
    lm = jnp.where(strict, on_k[:, :c] * decay, 0.0)
    attn = on_k[:, c:] * decay
    if tm is None:
        tm = _neumann_inverse(lax.stop_gradient(lm))
    egc = jnp.exp(gc)
    rhs = jnp.concatenate([v * bcol, kb * egc], axis=-1)
    sol = _unit_lower_solve(lm, rhs, tm)
    dv_ = v.shape[-1]
    u, w = sol[..., :dv_], sol[..., dv_:]
    gl = jnp.sum(grow, axis=2, keepdims=True)
    kd = k * jnp.exp(gl - gc)
    on_s = _mm(jnp.concatenate([w, q * egc], axis=1), s0, "nn")
    v_new = u - on_s[:, :c]
    o = on_s[:, c:] + _mm(attn, v_new, "nn")
    s1 = s0 * jnp.exp(gl) + _mm(kd, v_new, "tn")
    return o, s1, tm


def _dn_point(cv, ba, alog, dt, heads):
    a = _silu(cv)
    d = cv.shape[1] // 3
    qs, ks = [], []
    for h in range(heads):
        qh = a[:, h * DN_HEAD_DIM:(h + 1) * DN_HEAD_DIM]
        qs.append(qh * lax.rsqrt(jnp.sum(qh * qh, axis=-1, keepdims=True) + 1e-6) * (DN_HEAD_DIM ** -0.5))
        kh = a[:, d + h * DN_HEAD_DIM:d + (h + 1) * DN_HEAD_DIM]
        ks.append(kh * lax.rsqrt(jnp.sum(kh * kh, axis=-1, keepdims=True) + 1e-6))
    q = jnp.concatenate(qs, axis=-1)
    k = jnp.concatenate(ks, axis=-1)
    v = a[:, 2 * d:]
    lane = lax.broadcasted_iota(jnp.int32, ba.shape, 1)
    beta = _sigmoid(ba)
    g = -jnp.exp(alog) * _softplus(ba + dt)
    gb = jnp.where(lane < heads, beta, jnp.where(lane < 2 * heads, g, 0.0))
    return q, k, v, gb


def _attn_tile(q, k, v):
    hd = q.shape[1] // XA_HEADS
    outs = []
    for h in range(XA_HEADS):
        sl = slice(h * hd, (h + 1) * hd)
        s = _mm(q[:, sl], k[:, sl], "nt") * (hd ** -0.5)
        m = lax.stop_gradient(jnp.max(s, axis=-1, keepdims=True))
        e = jnp.exp(s - m)
        p = e / jnp.sum(e, axis=-1, keepdims=True)
        outs.append(_mm(p, v[:, sl], "nn"))
    return jnp.concatenate(outs, axis=-1)


def _ln_silu(c, g, b):
    mu = jnp.mean(c, axis=-1, keepdims=True)
    xc = c - mu
    y = xc * lax.rsqrt(jnp.mean(xc * xc, axis=-1, keepdims=True) + LN_EPS)
    return _silu(y * g + b)


def _causal_conv(xext, w, width, lead, ts):
    acc = None
    for j in range(width):
        term = _shift_rows(xext, lead + j)[:ts] * w[j:j + 1, :]
        acc = term if acc is None else acc + term
    return acc


def _colsum(x):
    return jnp.sum(x, axis=0, keepdims=True)


def _stack_rows(rows, n_rows):
    c = rows[0].shape[1]
    ridx = lax.broadcasted_iota(jnp.int32, (n_rows, c), 0)
    out = jnp.zeros((n_rows, c), F32)
    for j, r in enumerate(rows):
        out = out + jnp.where(ridx == j, r, 0.0)
    return out


def _matmul(a, b, mode, out_dtypes, *, name, epi=None, mn_extras=(), row_extras=(), out_dm=False, after=(),
            n_rowsum=0, slab=0, b_rows=None, a_pre=None, tm=MM_TILE, tn=MM_TILE, tk=MM_TILE):
    b_dm = b.ndim == 3
    b_shape = (b.shape[1], N_DEV * b.shape[2]) if b_dm else b.shape
    if b_rows is not None:
        assert not b_dm and b_rows <= b.shape[0]
        b_shape = (b_rows, b.shape[1])
    if mode == "nn":
        (m, k), (k2, n) = a.shape, b_shape
    elif mode == "nt":
        (m, k), (n, k2) = a.shape, b_shape
    else:
        (k, m), (k2, n) = a.shape, b_shape
    assert k == k2, (a.shape, b.shape, mode)
    tm, tn, tk = min(tm, m), min(tn, n), min(tk, k)
    cb, nb = 0, 1
    if b_dm:
        assert mode in ("nn", "nt")
        cb = b.shape[2]
        nb = max(1, (tn if mode == "nn" else tk) // cb)
        if mode == "nn":
            tn = nb * cb
        else:
            tk = nb * cb
    co, no = 0, 1
    if out_dm:
        co = n // N_DEV
        no = max(1, tn // co)
        tn = no * co
    assert m % tm == 0 and n % tn == 0 and k % tk == 0, (m, n, k, tm, tn, tk)
    nk = k // tk
    if mode == "tn":
        a_spec = pl.BlockSpec((tk, tm), lambda j, i, kk: (kk, i))
    else:
        a_spec = pl.BlockSpec((tm, tk), lambda j, i, kk: (i, kk))
    if b_dm:
        b_spec = (pl.BlockSpec((nb, tn, cb), lambda j, i, kk: (kk, j, 0)) if mode == "nt"
                  else pl.BlockSpec((nb, tk, cb), lambda j, i, kk: (j, kk, 0)))
    else:
        b_spec = (pl.BlockSpec((tn, tk), lambda j, i, kk: (j, kk)) if mode == "nt"
                  else pl.BlockSpec((tk, tn), lambda j, i, kk: (kk, j)))
    mn_spec = pl.BlockSpec((tm, tn), lambda j, i, kk: (i, j))
    row_spec = pl.BlockSpec((1, tn), lambda j, i, kk: (0, j))
    n_extra = len(mn_extras) + len(row_extras)
    n_out = len(out_dtypes)
    in_specs = ([a_spec, b_spec] + [mn_spec] * len(mn_extras) + [row_spec] * len(row_extras)
                + [_ANY_SPEC] * len(after))
    args = [a, b, *mn_extras, *row_extras, *after]
    if out_dm:
        out_specs = [pl.BlockSpec((no, tm, co), lambda j, i, kk: (j, i, 0))] * n_out
        out_shape = [jax.ShapeDtypeStruct((N_DEV, m, co), dt) for dt in out_dtypes]
    else:
        out_specs = [mn_spec] * n_out
        out_shape = [jax.ShapeDtypeStruct((m, n), dt) for dt in out_dtypes]
    out_specs = out_specs + [row_spec] * n_rowsum
    out_shape = out_shape + [jax.ShapeDtypeStruct((1, n), F32)] * n_rowsum
    n_in = len(args)
    n_mn = len(mn_extras)
    step = min(slab, tm) if slab else tm
    assert tm % step == 0

    def dot(a_ref, b_ref):
        a_val = a_ref[...] if a_pre is None else a_pre(a_ref[...])
        if not b_dm:
            return _bdot(a_val, b_ref[...], mode)
        if mode == "nn":
            parts = [_bdot(a_val, b_ref[dd], "nn") for dd in range(nb)]
            return parts[0] if nb == 1 else jnp.concatenate(parts, axis=1)
        out = None
        for dd in range(nb):
            part = _bdot(a_val[:, dd * cb:(dd + 1) * cb], b_ref[dd], "nt")
            out = part if out is None else out + part
        return out

    def finish(acc_src, extras, outs):
        sums = [None] * n_rowsum
        for r0 in range(0, tm, step):
            rs = slice(r0, r0 + step)
            acc_val = acc_src[rs, :]
            if epi is None:
                vals = (acc_val,)
            else:
                vals = epi(acc_val, *[e[rs, :] for e in extras[:n_mn]], *[e[...] for e in extras[n_mn:]])
            for o_ref, val in zip(outs[:n_out], vals[:n_out]):
                if out_dm:
                    for dd in range(no):
                        o_ref[dd, rs, :] = val[:, dd * co:(dd + 1) * co].astype(o_ref.dtype)
                else:
                    o_ref[rs, :] = val.astype(o_ref.dtype)
            for q in range(n_rowsum):
                sums[q] = vals[n_out + q] if sums[q] is None else sums[q] + vals[n_out + q]
        for q in range(n_rowsum):
            s_ref = outs[n_out + q]

            @pl.when(pl.program_id(1) == 0)
            def _():
                s_ref[...] = sums[q]

            @pl.when(pl.program_id(1) > 0)
            def _():
                s_ref[...] += sums[q]

    def body_one_step(*refs):
        finish(dot(refs[0], refs[1]), refs[2:2 + n_extra], refs[n_in:])

    def body(*refs):
        a_ref, b_ref = refs[0], refs[1]
        acc = refs[-1]
        kk = pl.program_id(2)

        @pl.when(kk == 0)
        def _():
            acc[...] = jnp.zeros_like(acc)

        acc[...] += dot(a_ref, b_ref)

        @pl.when(kk == nk - 1)
        def _():
            finish(acc, refs[2:2 + n_extra], refs[n_in:-1])

    res = pl.pallas_call(
        body_one_step if nk == 1 else body, name=name,
        grid=(n // tn, m // tm, nk),
        in_specs=in_specs, out_specs=out_specs, out_shape=out_shape,
        scratch_shapes=[] if nk == 1 else [pltpu.VMEM((tm, tn), F32)],
        compiler_params=pltpu.CompilerParams(
            dimension_semantics=("parallel", "arbitrary" if n_rowsum else "parallel", "arbitrary")),
    )(*args)
    return res[0] if n_out + n_rowsum == 1 else res


def _rowwise(fn, *, n_rows, ts, name, rows=(), prevs=(), nexts=(), vecs=(), row_outs=(), acc_outs=(), after=()):
    ts = min(ts, n_rows)
    assert n_rows % ts == 0
    nblk = n_rows // ts
    in_specs, args = [], []
    for arr, cb, w in rows:
        in_specs.append(pl.BlockSpec((ts, w), functools.partial(lambda i, cb: (i, cb), cb=cb)))
        args.append(arr)
    for arr, cb, w, halo in prevs:
        per = ts // halo
        in_specs.append(pl.BlockSpec(
            (halo, w), functools.partial(lambda i, cb, per: (jnp.maximum(i * per - 1, 0), cb), cb=cb, per=per)))
        args.append(arr)
    for arr, cb, w, halo in nexts:
        per = ts // halo
        last_blk = n_rows // halo - 1
        in_specs.append(pl.BlockSpec(
            (halo, w), functools.partial(lambda i, cb, per, lb: (jnp.minimum((i + 1) * per, lb), cb),
                                         cb=cb, per=per, lb=last_blk)))
        args.append(arr)
    for arr in vecs:
        in_specs.append(pl.BlockSpec(arr.shape, functools.partial(lambda i, nd: (0,) * nd, nd=arr.ndim)))
        args.append(arr)
    out_specs, out_shape = [], []
    for w, dt in row_outs:
        out_specs.append(pl.BlockSpec((ts, w), lambda i: (i, 0)))
        out_shape.append(jax.ShapeDtypeStruct((n_rows, w), dt))
    for shp in acc_outs:
        out_specs.append(pl.BlockSpec(shp, functools.partial(lambda i, nd: (0,) * nd, nd=len(shp))))
        out_shape.append(jax.ShapeDtypeStruct(shp, F32))
    n_used = len(args)
    n_tiles = n_used - len(vecs)
    in_specs += [_ANY_SPEC] * len(after)
    args += list(after)
    n_in, n_ro, n_acc = len(args), len(row_outs), len(acc_outs)

    def body(*refs):
        ins, ro, ac = refs[:n_used], refs[n_in:n_in + n_ro], refs[n_in + n_ro:]
        i = pl.program_id(0)
        rvals, avals = fn(i == 0, i == nblk - 1, *[r[...] for r in ins[:n_tiles]], *ins[n_tiles:])
        for r, val in zip(ro, rvals):
            r[...] = val.astype(r.dtype)
        if n_acc:
            @pl.when(i == 0)
            def _():
                for r in ac:
                    r[...] = jnp.zeros_like(r)

            for r, val in zip(ac, avals):
                r[...] += val

    res = pl.pallas_call(
        body, name=name, grid=(nblk,), in_specs=in_specs, out_specs=out_specs, out_shape=out_shape,
        compiler_params=pltpu.CompilerParams(dimension_semantics=("arbitrary",)),
    )(*args)
    return res


def _gated_out(o, z, onorm):
    return o * lax.rsqrt(jnp.mean(o * o, axis=-1, keepdims=True) + RMS_EPS) * onorm * _silu(z)


def _head_blocks(ref, heads, col0=0):
    return jnp.stack([ref[:, col0 + h * DN_HEAD_DIM:col0 + (h + 1) * DN_HEAD_DIM] for h in range(heads)])


def _split_heads(q_ref, k_ref, v_ref, gbv, heads):
    gcol = jnp.stack([gbv[:, heads + h:heads + h + 1] for h in range(heads)])
    bcol = jnp.stack([gbv[:, h:h + 1] for h in range(heads)])
    return _head_blocks(q_ref, heads), _head_blocks(k_ref, heads), _head_blocks(v_ref, heads), gcol, bcol


def _delta_fwd(q, k, v, gb, qkvz, onorm, heads):
    s, hd = q.shape
    n = s // DN_CHUNK
    blk = pl.BlockSpec((DN_CHUNK, hd), lambda c: (c, 0))
    gspec = pl.BlockSpec((DN_CHUNK, LANES), lambda c: (c, 0))

    def body(q_ref, k_ref, v_ref, gb_ref, z_ref, on_ref, og_ref, st_ref, tm_ref, state):
        @pl.when(pl.program_id(0) == 0)
        def _():
            state[...] = jnp.zeros_like(state)

        s0 = state[...]
        st_ref[0] = s0
        o, s1, tm = _delta_chunk(*_split_heads(q_ref, k_ref, v_ref, gb_ref[...], heads), s0)
        og = _gated_out(o, _head_blocks(z_ref, heads), on_ref[...])
        for h in range(heads):
            og_ref[:, h * DN_HEAD_DIM:(h + 1) * DN_HEAD_DIM] = og[h].astype(og_ref.dtype)
        state[...] = s1
        tm_ref[0] = tm

    return pl.pallas_call(
        body, name="dn_delta_fwd", grid=(n,),
        in_specs=[blk, blk, blk, gspec, pl.BlockSpec((DN_CHUNK, hd), lambda c: (c, 3)),
                  pl.BlockSpec(onorm.shape, lambda c: (0, 0))],
        out_specs=[blk, pl.BlockSpec((1, heads, DN_HEAD_DIM, DN_HEAD_DIM), lambda c: (c, 0, 0, 0)),
                   pl.BlockSpec((1, heads, DN_CHUNK, DN_CHUNK), lambda c: (c, 0, 0, 0))],
        out_shape=[jax.ShapeDtypeStruct((s, hd), BF16),
                   jax.ShapeDtypeStruct((n, heads, DN_HEAD_DIM, DN_HEAD_DIM), F32),
                   jax.ShapeDtypeStruct((n, heads, DN_CHUNK, DN_CHUNK), F32)],
        scratch_shapes=[pltpu.VMEM((heads, DN_HEAD_DIM, DN_HEAD_DIM), F32)],
        compiler_params=pltpu.CompilerParams(dimension_semantics=("arbitrary",)),
    )(q, k, v, gb, qkvz, onorm)


def _delta_bwd(q, k, v, gb, qkvz, onorm, states, tms, dog, heads):
    s, hd = q.shape
    n = s // DN_CHUNK
    blk = pl.BlockSpec((DN_CHUNK, hd), lambda c: (n - 1 - c, 0))
    gspec = pl.BlockSpec((DN_CHUNK, LANES), lambda c: (n - 1 - c, 0))
    sspec = pl.BlockSpec((1, heads, DN_HEAD_DIM, DN_HEAD_DIM), lambda c: (n - 1 - c, 0, 0, 0))
    tspec = pl.BlockSpec((1, heads, DN_CHUNK, DN_CHUNK), lambda c: (n - 1 - c, 0, 0, 0))
    nspec = pl.BlockSpec(onorm.shape, lambda c: (0, 0))

    def body(q_ref, k_ref, v_ref, gb_ref, z_ref, on_ref, st_ref, tm_ref, dog_ref,
             dq_ref, dk_ref, dv_ref, dgb_ref, dz_ref, don_ref, dstate):
        @pl.when(pl.program_id(0) == 0)
        def _():
            dstate[...] = jnp.zeros_like(dstate)
            don_ref[...] = jnp.zeros_like(don_ref)

        gbv = gb_ref[...]
        tm = tm_ref[0]

        def chunk(qh, kh, vh, gcol, bcol, s0, zh, on):
            o, s1, _ = _delta_chunk(qh, kh, vh, gcol, bcol, s0, tm)
            return _gated_out(o, zh, on), s1

        _, vjp = jax.vjp(chunk, *_split_heads(q_ref, k_ref, v_ref, gbv, heads), st_ref[0],
                         _head_blocks(z_ref, heads), on_ref[...])
        dq, dk, dv, dg, db, ds0, dz, don = vjp((_head_blocks(dog_ref, heads).astype(F32), dstate[...]))
        dstate[...] = ds0
        don_ref[...] += don
        lane = lax.broadcasted_iota(jnp.int32, gbv.shape, 1)
        dgb = jnp.zeros(gbv.shape, F32)
        for h in range(heads):
            sl = slice(h * DN_HEAD_DIM, (h + 1) * DN_HEAD_DIM)
            dq_ref[:, sl] = dq[h]
            dk_ref[:, sl] = dk[h]
            dv_ref[:, sl] = dv[h]
            dz_ref[:, sl] = dz[h]
            dgb = dgb + jnp.where(lane == h, db[h], 0.0) + jnp.where(lane == heads + h, dg[h], 0.0)
        dgb_ref[...] = dgb

    return pl.pallas_call(
        body, name="dn_delta_bwd", grid=(n,),
        in_specs=[blk, blk, blk, gspec, pl.BlockSpec((DN_CHUNK, hd), lambda c: (n - 1 - c, 3)), nspec,
                  sspec, tspec, blk],
        out_specs=[blk, blk, blk, gspec, blk, nspec],
        out_shape=[jax.ShapeDtypeStruct((s, hd), F32)] * 3 + [jax.ShapeDtypeStruct((s, LANES), F32),
                                                              jax.ShapeDtypeStruct((s, hd), F32),
                                                              jax.ShapeDtypeStruct(onorm.shape, F32)],
        scratch_shapes=[pltpu.VMEM((heads, DN_HEAD_DIM, DN_HEAD_DIM), F32)],
        compiler_params=pltpu.CompilerParams(dimension_semantics=("arbitrary",)),
    )(q, k, v, gb, qkvz, onorm, states, tms, dog)


def _dev_index(px, py, pc):
    return 4 * px + 2 * py + pc


def _all_gather(arrs, name):
    n = len(arrs)

    def body(*refs):
        xs, outs = refs[:n], refs[n:2 * n]
        send_sems, recv_sems, local_sems = refs[2 * n:]
        x, y, c = lax.axis_index("x"), lax.axis_index("y"), lax.axis_index("c")
        me, sibling = (x, y, c), (x, y, 1 - c)
        chips = [(1 - x, y), (x, 1 - y), (1 - x, 1 - y)]

        def copy(a, kk, block, to, src=None):
            dst = outs[a].at[_dev_index(*block)]
            return pltpu.make_async_remote_copy(
                src_ref=dst if src is None else src, dst_ref=dst,
                send_sem=send_sems.at[a * 7 + kk], recv_sem=recv_sems.at[a * 7 + kk],
                device_id=to, device_id_type=MESH_IDS)

        mine = [pltpu.make_async_copy(xs[a], outs[a].at[_dev_index(*me)], local_sems.at[a]) for a in range(n)]
        for cp in mine:
            cp.start()
        first = []
        for a in range(n):
            first.append(copy(a, 0, me, sibling, src=xs[a]))
            first += [copy(a, 1 + j, me, (*chip, c), src=xs[a]) for j, chip in enumerate(chips)]
        for cp in first:
            cp.start()
        passed = []
        for j, chip in enumerate(chips):
            for a in range(n):
                copy(a, 1 + j, (*chip, c), me).wait_recv()
                fwd = copy(a, 4 + j, (*chip, c), sibling)
                fwd.start()
                passed.append(fwd)
        for a in range(n):
            copy(a, 0, sibling, me).wait_recv()
        for j, chip in enumerate(chips):
            for a in range(n):
                copy(a, 4 + j, (*chip, 1 - c), me).wait_recv()
        for cp in first + passed:
            cp.wait_send()
        for cp in mine:
            cp.wait()

    hbm = pl.BlockSpec(memory_space=pltpu.HBM)
    res = pl.pallas_call(
        body, name=name,
        in_specs=[hbm] * n, out_specs=[hbm] * n,
        out_shape=[jax.ShapeDtypeStruct((N_DEV,) + a.shape, a.dtype) for a in arrs],
        scratch_shapes=[pltpu.SemaphoreType.DMA((7 * n,)), pltpu.SemaphoreType.DMA((7 * n,)),
                        pltpu.SemaphoreType.DMA((n,))],
    )(*arrs)
    return list(res)


_FLIPS = ((0, 0, 1), (1, 0, 0), (0, 1, 0), (1, 1, 0), (1, 0, 1), (0, 1, 1), (1, 1, 1))
_HBM_SPEC = pl.BlockSpec(memory_space=pltpu.HBM)
_SEM_SPEC = pl.BlockSpec(memory_space=pltpu.SEMAPHORE)
_ANY_SPEC = pl.BlockSpec(memory_space=pl.ANY)
_DATAFLOW = pltpu.SideEffectType.DATAFLOW_SIDE_EFFECTING
TOKEN_SHAPE = (8, LANES)


def _mesh_me():
    return lax.axis_index("x"), lax.axis_index("y"), lax.axis_index("c")


def _flipped(me, f):
    return tuple(1 - v if fl else v for v, fl in zip(me, f))


def _exchange_copies(xs, lands, send_sems, recv_sems, scatter, landed):
    me = _mesh_me()
    cps = []
    for kk, f in enumerate(_FLIPS):
        p = _flipped(me, f)
        for a in range(len(xs)):
            cps.append(pltpu.make_async_remote_copy(
                src_ref=xs[a].at[_dev_index(*p)] if scatter else xs[a],
                dst_ref=lands[a].at[_dev_index(*(p if landed else me))],
                send_sem=send_sems.at[a * 7 + kk], recv_sem=recv_sems.at[a * 7 + kk],
                device_id=p, device_id_type=MESH_IDS))
    return cps


def _exchange_start(srcs, lands, scatter, name, after=()):
    n = len(srcs)

    n_after = len(after)

    def body(*refs):
        xs, ls = refs[:n], refs[n:2 * n]
        send_sems, recv_sems = refs[2 * n + n_after], refs[2 * n + n_after + 1]
        token = refs[-1]
        for cp in _exchange_copies(xs, ls, send_sems, recv_sems, scatter, landed=False):
            cp.start()
        token[...] = jnp.zeros_like(token)

    operands = [pltpu.with_memory_space_constraint(a, pltpu.HBM) for a in list(srcs) + list(lands)]
    res = pl.pallas_call(
        body, name=name,
        in_specs=[_HBM_SPEC] * (2 * n) + [_ANY_SPEC] * len(after),
        out_specs=[_SEM_SPEC, _SEM_SPEC] + [_HBM_SPEC] * (2 * n) + [pl.BlockSpec(memory_space=pltpu.VMEM)],
        out_shape=[pltpu.SemaphoreType.DMA((7 * n,)), pltpu.SemaphoreType.DMA((7 * n,))]
        + [pltpu.HBM(a.shape, a.dtype) for a in operands] + [jax.ShapeDtypeStruct(TOKEN_SHAPE, F32)],
        input_output_aliases={i: 2 + i for i in range(2 * n)},
        compiler_params=pltpu.CompilerParams(has_side_effects=_DATAFLOW),
    )(*operands, *after)
    return (res[0], res[1], list(res[2:2 + n]), list(res[2 + n:2 + 2 * n]), scatter, name), res[-1]


def _exchange_wait(handle, after):
    send_sems, recv_sems, srcs, lands, scatter, name = handle
    n = len(srcs)
    n_after = len(after)

    def body(*refs):
        xs, ls = refs[:n], refs[n:2 * n]
        send_sems_ref, recv_sems_ref = refs[2 * n], refs[2 * n + 1]
        for cp in _exchange_copies(xs, ls, send_sems_ref, recv_sems_ref, scatter, landed=True):
            cp.wait_send()
            cp.wait_recv()

    res = pl.pallas_call(
        body, name=name + "_wait",
        in_specs=[_HBM_SPEC] * (2 * n) + [_SEM_SPEC, _SEM_SPEC] + [_ANY_SPEC] * n_after,
        out_specs=[_HBM_SPEC] * (2 * n),
        out_shape=[pltpu.HBM(a.shape, a.dtype) for a in srcs + lands],
        input_output_aliases={i: i for i in range(2 * n)},
        compiler_params=pltpu.CompilerParams(has_side_effects=_DATAFLOW),
    )(*srcs, *lands, send_sems, recv_sems, *after)
    return list(res[:n]), list(res[n:])


def _slot_sum(g, name, tr):
    _, r, c = g.shape
    tr = min(tr, r)
    assert r % tr == 0

    def body(g_ref, o_ref):
        acc = g_ref[0].astype(F32)
        for s in range(1, N_DEV):
            acc = acc + g_ref[s].astype(F32)
        o_ref[...] = acc

    return pl.pallas_call(
        body, name=name, grid=(r // tr,),
        in_specs=[pl.BlockSpec((N_DEV, tr, c), lambda i: (0, i, 0))],
        out_specs=pl.BlockSpec((tr, c), lambda i: (i, 0)),
        out_shape=jax.ShapeDtypeStruct((r, c), F32),
        compiler_params=pltpu.CompilerParams(dimension_semantics=("parallel",)),
    )(g)


def _adam_update(w, gg, m, v):
    c1 = 1.0 / (1.0 - ADAM_B1 ** ADAM_STEP)
    c2 = 1.0 / (1.0 - ADAM_B2 ** ADAM_STEP)
    nm = ADAM_B1 * m + (1.0 - ADAM_B1) * gg
    nv = ADAM_B2 * v + (1.0 - ADAM_B2) * (gg * gg)
    return -ADAM_LR * ((nm * c1) / (jnp.sqrt(nv * c2) + ADAM_EPS) + ADAM_WD * w), nm, nv


def _adamw_reduce(me, recvs, owns, w, m, v, name, tr=256):
    nl, r, c = w.shape
    assert len(recvs) == nl and len(owns) == nl
    tr = min(tr, r)
    if r % tr == 0:
        tc, nblk = c, r // tr
        at = lambda i: (i, 0)
    else:
        tr, tc = r, min(c, 4 * LANES)
        assert c % tc == 0
        nblk = c // tc
        at = lambda i: (0, i)

    def parked(li, l, i):
        return jnp.where(l < li, 0, jnp.where(l > li, nblk - 1, i))

    def recv_spec(li):
        return pl.BlockSpec((N_DEV, tr, tc), lambda l, i, me_ref: (0, *at(parked(li, l, i))))

    def own_spec(li):
        return pl.BlockSpec((None, tr, tc), lambda l, i, me_ref: (me_ref[0], *at(parked(li, l, i))))

    def body(me_ref, *refs):
        rrefs, orefs = refs[:nl], refs[nl:2 * nl]
        w_ref, m_ref, v_ref, g_ref, d_ref, nm_ref, nv_ref = refs[2 * nl:]
        l = pl.program_id(0)

        def of_layer(vals):
            out = vals[0]
            for li in range(1, nl):
                out = jnp.where(l == li, vals[li], out)
            return out

        own = of_layer([o[...].astype(F32) for o in orefs])
        gg = None
        for s in range(N_DEV):
            slot = jnp.where(me_ref[0] == s, own, of_layer([rr[s].astype(F32) for rr in rrefs]))
            gg = slot if gg is None else gg + slot
        g_ref[...] = gg
        d_ref[...], nm_ref[...], nv_ref[...] = _adam_update(w_ref[...], gg, m_ref[...], v_ref[...])

    spec = pl.BlockSpec((None, tr, tc), lambda l, i, me_ref: (l, *at(i)))
    return pl.pallas_call(
        body, name=name,
        grid_spec=pltpu.PrefetchScalarGridSpec(
            num_scalar_prefetch=1, grid=(nl, nblk),
            in_specs=[recv_spec(li) for li in range(nl)] + [own_spec(li) for li in range(nl)] + [spec] * 3,
            out_specs=[spec] * 4),
        out_shape=[jax.ShapeDtypeStruct((nl, r, c), F32)] * 4,
        compiler_params=pltpu.CompilerParams(dimension_semantics=("arbitrary", "arbitrary")),
    )(me, *recvs, *owns, w, m, v)


def _adamw(w, g, m, v, name, tr=256):
    r, c = w.shape
    tr = min(tr, r)
    assert r % tr == 0

    def body(w_ref, g_ref, m_ref, v_ref, d_ref, nm_ref, nv_ref):
        d_ref[...], nm_ref[...], nv_ref[...] = _adam_update(w_ref[...], g_ref[...], m_ref[...], v_ref[...])

    spec = pl.BlockSpec((tr, c), lambda i: (i, 0))
    return pl.pallas_call(
        body, name=name, grid=(r // tr,), in_specs=[spec] * 4, out_specs=[spec] * 3,
        out_shape=[jax.ShapeDtypeStruct((r, c), F32)] * 3,
        compiler_params=pltpu.CompilerParams(dimension_semantics=("parallel",)),
    )(w, g, m, v)


def _rms_fwd(x, g, name, ts=512, after=()):
    s, d = x.shape

    def fn(first, last, xv, gv):
        return [_rms(xv, gv[...])], []

    return _rowwise(fn, n_rows=s, ts=ts, name=name, rows=[(x, 0, d)], vecs=[g], row_outs=[(d, BF16)],
                    after=after)[0]


def _rms_bwd(x, dn, dres, g, name, ts=256):
    s, d = x.shape

    def fn(first, last, xv, dnv, drv, gv):
        _, vjp = jax.vjp(_rms, xv, gv[...])
        dx, dg = vjp(dnv.astype(F32))
        return [drv + dx], [dg]

    return _rowwise(fn, n_rows=s, ts=ts, name=name, rows=[(x, 0, d), (dn, 0, d), (dres, 0, d)], vecs=[g],
                    row_outs=[(d, F32)], acc_outs=[(1, d)])


def _dn_pre_fwd(qkvz, ba, wconv, alog, dt, heads, ts=256):
    s = qkvz.shape[0]
    d3 = wconv.shape[1]
    d = d3 // 3

    def fn(first, last, xc, bav, xp, wv, av, dv):
        xext = jnp.concatenate([jnp.where(first, 0.0, xp), xc], axis=0)
        cv = _causal_conv(xext, wv, DN_CONV, DN_HALO - (DN_CONV - 1), xc.shape[0])
        return list(_dn_point(cv, bav, av[...], dv[...], heads)), []

    return _rowwise(fn, n_rows=s, ts=ts, name="dn_pre_fwd", rows=[(qkvz, 0, d3), (ba, 0, LANES)],
                    prevs=[(qkvz, 0, d3, DN_HALO)], vecs=[wconv, alog, dt],
                    row_outs=[(d, F32), (d, F32), (d, F32), (LANES, F32)])


def _dn_pre_bwd(qkvz, ba, wconv, alog, dt, dq, dk, dv, dgb, dz, heads, ts=256):
    s = qkvz.shape[0]
    d3 = wconv.shape[1]
    d = d3 // 3
    lead = DN_HALO - (DN_CONV - 1)

    def fn(first, last, xc, bac, dqc, dkc, dvc, dgbc, dzc, xp, xn, ban, dqn, dkn, dvn, dgbn, wv, av, dtv):
        n = xc.shape[0]
        ext = lambda cur, nxt: jnp.concatenate([cur, nxt], axis=0)
        live = lambda nxt: jnp.where(last, 0.0, nxt)
        xall = jnp.concatenate([jnp.where(first, 0.0, xp), xc, live(xn)], axis=0)
        cv = _causal_conv(xall, wv, DN_CONV, lead, n + DN_HALO)
        (_, _, _, gbv), vjp = jax.vjp(lambda c, b: _dn_point(c, b, av[...], dtv[...], heads), cv, ext(bac, ban))
        dc, dba = vjp((ext(dqc, live(dqn)), ext(dkc, live(dkn)), ext(dvc, live(dvn)), ext(dgbc, live(dgbn))))
        dx = None
        dw = []
        for j in range(DN_CONV):
            term = _shift_rows(dc, DN_CONV - 1 - j)[:n] * wv[j:j + 1, :]
            dx = term if dx is None else dx + term
            dw.append(_colsum(dc[:n] * _shift_rows(xall, lead + j)[:n]))
        dba = dba[:n]
        return ([jnp.concatenate([dx, dzc], axis=-1), dba],
                [_stack_rows(dw, DN_CONV), _colsum(dgbc * gbv[:n]), _colsum(dba)])

    return _rowwise(fn, n_rows=s, ts=ts, name="dn_pre_bwd",
                    rows=[(qkvz, 0, d3), (ba, 0, LANES), (dq, 0, d), (dk, 0, d), (dv, 0, d), (dgb, 0, LANES),
                          (dz, 0, d)],
                    prevs=[(qkvz, 0, d3, DN_HALO)],
                    nexts=[(qkvz, 0, d3, DN_HALO), (ba, 0, LANES, DN_HALO), (dq, 0, d, DN_HALO),
                           (dk, 0, d, DN_HALO), (dv, 0, d, DN_HALO), (dgb, 0, LANES, DN_HALO)],
                    vecs=[wconv, alog, dt],
                    row_outs=[(4 * d, BF16), (LANES, BF16)], acc_outs=[(DN_CONV, d3), (1, LANES), (1, LANES)])


def _cv_mid_fwd(u, wdw, bdw, lng, lnb, ts=512):
    s = u.shape[0]
    d = u.shape[1] // 2

    def fn(first, last, uc, up, wv, bv, gv, lbv):
        uext = jnp.concatenate([jnp.where(first, 0.0, up), uc], axis=0)
        glu = uext[:, :d] * _sigmoid(uext[:, d:])
        c = _causal_conv(glu, wv, CV_WIDTH, CV_HALO - (CV_WIDTH - 1), uc.shape[0]) + bv[...]
        return [c, _ln_silu(c, gv[...], lbv[...])], []

    return _rowwise(fn, n_rows=s, ts=ts, name="cv_mid_fwd", rows=[(u, 0, 2 * d)], prevs=[(u, 0, 2 * d, CV_HALO)],
                    vecs=[wdw, bdw, lng, lnb], row_outs=[(d, F32), (d, BF16)])


def _cv_mid_bwd2(dc, u, wdw, ts=512):
    s, d = dc.shape

    def fn(first, last, dcc, uc, up, dcn, wv):
        n = dcc.shape[0]
        dcext = jnp.concatenate([dcc, jnp.where(last, 0.0, dcn)], axis=0)
        uext = jnp.concatenate([jnp.where(first, 0.0, up), uc], axis=0)
        glu = uext[:, :d] * _sigmoid(uext[:, d:])
        dglu = None
        dw = []
        for j in range(CV_WIDTH):
            term = _shift_rows(dcext, CV_WIDTH - 1 - j)[:n] * wv[j:j + 1, :]
            dglu = term if dglu is None else dglu + term
            dw.append(_colsum(dcc * _shift_rows(glu, CV_HALO - (CV_WIDTH - 1) + j)[:n]))
        u1, sg = uc[:, :d], _sigmoid(uc[:, d:])
        du = jnp.concatenate([dglu * sg, dglu * u1 * sg * (1.0 - sg)], axis=-1)
        return [du], [_stack_rows(dw, CV_HALO), _colsum(du)]

    return _rowwise(fn, n_rows=s, ts=ts, name="cv_mid_bwd2", rows=[(dc, 0, d), (u, 0, 2 * d)],
                    prevs=[(u, 0, 2 * d, CV_HALO)], nexts=[(dc, 0, d, CV_HALO)], vecs=[wdw],
                    row_outs=[(2 * d, BF16)], acc_outs=[(CV_HALO, d), (1, 2 * d)])


def _attn_fwd(q, k, v, name, ts=512):
    s, d = q.shape

    def fn(first, last, qv, kv, vv):
        return [_attn_tile(qv.astype(F32), kv[...].astype(F32), vv[...].astype(F32))], []

    return _rowwise(fn, n_rows=s, ts=ts, name=name, rows=[(q, 0, d)], vecs=[k, v], row_outs=[(d, BF16)])[0]


def _attn_bwd(q, k, v, do, name, ts=512):
    s, d = q.shape
    m = k.shape[0]

    def fn(first, last, qv, dov, kv, vv):
        _, vjp = jax.vjp(_attn_tile, qv.astype(F32), kv[...].astype(F32), vv[...].astype(F32))
        dq, dk, dv = vjp(dov.astype(F32))
        return [dq], [dk, dv]

    return _rowwise(fn, n_rows=s, ts=ts, name=name, rows=[(q, 0, d), (do, 0, d)], vecs=[k, v],
                    row_outs=[(d, BF16)], acc_outs=[(m, d), (m, d)])


def _pad_lanes(a, off=0):
    r, n = a.shape
    return jnp.pad(a, ((0, 0), (off, LANES - off - n)))


def _local_step(x, mem, tgt, w, fetch=None, emit=None, first_after=()):
    s, d = x.shape
    heads = d // DN_HEAD_DIM
    g = {}
    if fetch is None:
        fetch = lambda group, after: None
    if emit is None:
        emit = lambda group, grads: ()

    def add_res(acc, res):
        return (res + acc,)

    def add_res_rms(acc, res, gain):
        h = res + acc
        return h, _rms(h, gain)

    def rms_bwd_epi(acc, hx, dres, gain):
        _, vjp = jax.vjp(_rms, hx, gain)
        dx, dg = vjp(acc)
        return dres + dx, dg

    w_int = w["dn_w_in"][0]
    assert w_int.shape[0] == 4 * d + 2 * heads
    w_bat = jnp.pad(w_int[4 * d:], ((0, LANES - 2 * heads), (0, 0)))
    dn_norm = w["dn_norm"]
    alog = _pad_lanes(w["dn_a_log"], heads)
    dtb = _pad_lanes(w["dn_dt_bias"], heads)
    wconv = w["dn_w_conv"][0]
    n0 = _rms_fwd(x, dn_norm, "dn_rms", after=first_after)
    qkvz = _matmul(n0, w_int, "nt", [F32], name="dn_in_proj", b_rows=4 * d)
    ba = _matmul(n0, w_bat, "nt", [F32], name="dn_in_proj_ba")
    q, k, v, gb = _dn_pre_fwd(qkvz, ba, wconv, alog, dtb, heads)
    og, states, tms = _delta_fwd(q, k, v, gb, qkvz, w["dn_out_norm"], heads)
    fetch(1, [og])
    h1, nq0 = _matmul(og, w["dn_w_out"][0], "nn", [F32, BF16], name="dn_out_proj", epi=add_res_rms,
                      mn_extras=[x], row_extras=[w["xa_norm"][0:1]], slab=EPI_SLAB)

    xa_tm = (512, MM_TILE)

    def xattn_fwd(h, nq, layer, next_gain):
        qx = _matmul(nq, w["xa_w_q"][layer], "nn", [BF16], name=f"xa{layer}_q", tm=xa_tm[layer])
        mn = _rms_fwd(mem, w["xa_mem_norm"][layer:layer + 1], f"xa{layer}_mem_rms")
        kv = _matmul(mn, w["xa_w_kv"][layer], "nn", [BF16], name=f"xa{layer}_kv")
        kx, vx = kv[:, :d], kv[:, d:]
        ox = _attn_fwd(qx, kx, vx, f"xa{layer}_attn")
        hn, nn = _matmul(ox, w["xa_w_o"][layer], "nn", [F32, BF16], name=f"xa{layer}_o", epi=add_res_rms,
                         mn_extras=[h], row_extras=[next_gain], slab=EPI_SLAB, tm=xa_tm[layer])
        return hn, nn, (h, nq, qx, mn, kx, vx, ox)

    def sq_relu(t):
        r = jnp.maximum(t.astype(F32), 0.0)
        return r * r

    def loss_epi(acc, res, target, gain):
        def cols(hh, gg):
            e = _rms(hh, gg) - target
            return _colsum(e * e) * (0.5 / d)

        per_col, vjp = jax.vjp(cols, res + acc, gain)
        dhx, dgain = vjp(jnp.ones_like(per_col))
        return dhx, dgain, per_col

    def mlp_fwd(h, nm, layer, next_gain):
        u = _matmul(nm, w["mlp_w_up"][layer], "nn", [BF16], name=f"mlp{layer}_up")
        if next_gain is None:
            hn, *nn = _matmul(u, w["mlp_w_down"][layer], "nn", [F32], name=f"mlp{layer}_down_loss",
                              epi=loss_epi, mn_extras=[h, tgt], row_extras=[w["final_norm"].reshape(1, d)],
                              n_rowsum=2, tk=MM_DEEP, slab=EPI_SLAB, a_pre=sq_relu)
        else:
            hn, nn = _matmul(u, w["mlp_w_down"][layer], "nn", [F32, BF16], name=f"mlp{layer}_down",
                             epi=add_res_rms, mn_extras=[h], row_extras=[next_gain], tk=MM_DEEP, slab=EPI_SLAB,
                             a_pre=sq_relu)
        return hn, nn, (h, nm, u)

    h2, nm0, xa0 = xattn_fwd(h1, nq0, 0, w["mlp_norm"][0:1])
    fetch(2, [h2])
    h3, n1, mlp0 = mlp_fwd(h2, nm0, 0, w["cv_norm"])

    u_cv = _matmul(n1, w["cv_w_pw1"][0], "nn", [F32], name="cv_pw1", epi=lambda acc, b: (acc + b,),
                   row_extras=[w["cv_b_pw1"]])
    wdw = jnp.pad(w["cv_w_dw"][0], ((0, CV_HALO - CV_WIDTH), (0, 0)))
    c_cv, s_cv = _cv_mid_fwd(u_cv, wdw, w["cv_b_dw"], w["cv_ln_g"], w["cv_ln_b"])
    h4, nq1 = _matmul(s_cv, w["cv_w_pw2"][0], "nn", [F32, BF16], name="cv_pw2",
                      epi=lambda acc, res, b, gain: add_res_rms(acc + b, res, gain), mn_extras=[h3],
                      row_extras=[w["cv_b_pw2"], w["xa_norm"][1:2]], slab=EPI_SLAB)
    fetch(3, [h4])
    h5, nm1, xa1 = xattn_fwd(h4, nq1, 1, w["mlp_norm"][1:2])
    dh, (g_fn, loss_cols), mlp1 = mlp_fwd(h5, nm1, 1, None)
    g["final_norm"] = g_fn.reshape(d)
    loss = jnp.sum(loss_cols, axis=1, keepdims=True)

    def mlp_bwd(dh, layer, saved, after=()):
        h, nm, u = saved
        du = _matmul(dh, w["mlp_w_down"][layer], "nt", [BF16], name=f"mlp{layer}_down_dx", after=after,
                     epi=lambda acc, uu: (acc * 2.0 * jnp.maximum(uu.astype(F32), 0.0),), mn_extras=[u])
        gdown = _matmul(u, dh, "tn", [BF16], name=f"mlp{layer}_down_dw", tm=MM_DEEP, a_pre=sq_relu)
        dhn, gn = _matmul(du, w["mlp_w_up"][layer], "nt", [F32], name=f"mlp{layer}_up_dx", epi=rms_bwd_epi,
                          mn_extras=[h, dh], row_extras=[w["mlp_norm"][layer:layer + 1]], n_rowsum=1,
                          slab=EPI_SLAB, tk=(MM_TILE, MM_DEEP)[layer])
        gup = _matmul(nm, du, "tn", [BF16], name=f"mlp{layer}_up_dw", out_dm=True, tk=MM_DEEP)
        return dhn, gup, gdown, gn

    def xattn_bwd(dh, layer, saved):
        h, nq, qx, mn, kx, vx, ox = saved
        dox = _matmul(dh, w["xa_w_o"][layer], "nt", [BF16], name=f"xa{layer}_o_dx", tm=xa_tm[layer])
        go = _matmul(ox, dh, "tn", [BF16], name=f"xa{layer}_o_dw", tk=MM_DEEP, tm=xa_tm[layer])
        dqx, dkx, dvx = _attn_bwd(qx, kx, vx, dox, f"xa{layer}_attn_bwd")

        def epi(acc, hx, dres, gain):
            dhx, dg = rms_bwd_epi(acc, hx, dres, gain)
            return dhx, dg, _colsum(dhx)

        dhn, gn, dh_cols = _matmul(dqx, w["xa_w_q"][layer], "nt", [F32], name=f"xa{layer}_q_dx", epi=epi,
                                   mn_extras=[h, dh], row_extras=[w["xa_norm"][layer:layer + 1]], n_rowsum=2,
                                   slab=EPI_SLAB, tm=xa_tm[layer])
        gq = _matmul(nq, dqx, "tn", [BF16], name=f"xa{layer}_q_dw", tk=MM_DEEP, tm=xa_tm[layer])
        dkv = jnp.concatenate([dkx, dvx], axis=-1)
        gkv = _matmul(mn, dkv, "tn", [BF16], name=f"xa{layer}_kv_dw", out_dm=True)
        dmn = _matmul(dkv, w["xa_w_kv"][layer], "nt", [F32], name=f"xa{layer}_kv_dx", tk=MM_DEEP)
        _, gmem = _rms_bwd(mem, dmn, dmn, w["xa_mem_norm"][layer:layer + 1], f"xa{layer}_mem_rms_bwd")
        return dhn, gq, gkv, go, gn, gmem, dh_cols

    dh, gup1, gdown1, gmn1 = mlp_bwd(dh, 1, mlp1)
    dh, gq1, gkv1, go1, gxn1, gmem1, g_b2 = xattn_bwd(dh, 1, xa1)
    g.update(mlp_w_up=[None, gup1], mlp_w_down=[None, gdown1], xa_w_q=[None, gq1], xa_w_kv=[None, gkv1],
             xa_w_o=[None, go1])
    tok = emit(3, g)

    def ln_bwd_epi(acc, cx, gain, bias):
        _, vjp = jax.vjp(_ln_silu, cx, gain, bias)
        dc, dg, db = vjp(acc)
        return dc, dg, db, _colsum(dc)

    dc_cv, g_lng, g_lnb, g_bdw = _matmul(dh, w["cv_w_pw2"][0], "nt", [F32], name="cv_pw2_dx", after=tok,
                                        epi=ln_bwd_epi, mn_extras=[c_cv],
                                        row_extras=[w["cv_ln_g"], w["cv_ln_b"]], n_rowsum=3, slab=EPI_SLAB)
    g["cv_w_pw2"] = [_matmul(s_cv, dh, "tn", [BF16], name="cv_pw2_dw", tk=MM_DEEP)]
    du_cv, g_wdw, g_b1 = _cv_mid_bwd2(dc_cv, u_cv, wdw)
    g["cv_w_pw1"] = [_matmul(n1, du_cv, "tn", [BF16], name="cv_pw1_dw", out_dm=True, tk=MM_DEEP)]
    dh, g_cvn = _matmul(du_cv, w["cv_w_pw1"][0], "nt", [F32], name="cv_pw1_dx", epi=rms_bwd_epi,
                        mn_extras=[h3, dh], row_extras=[w["cv_norm"]], n_rowsum=1, slab=EPI_SLAB)
    g.update(cv_ln_g=g_lng, cv_ln_b=g_lnb, cv_b_dw=g_bdw, cv_b_pw2=g_b2, cv_b_pw1=g_b1, cv_norm=g_cvn,
             cv_w_dw=g_wdw[:CV_WIDTH][None])

    tok = emit(2, g)
    dh, gup0, gdown0, gmn0 = mlp_bwd(dh, 0, mlp0, after=tok)
    dh, gq0, gkv0, go0, gxn0, gmem0, _ = xattn_bwd(dh, 0, xa0)
    g["mlp_w_up"][0] = gup0
    g["mlp_w_down"][0] = gdown0
    g["mlp_norm"] = jnp.concatenate([gmn0, gmn1], axis=0)
    g["xa_w_q"][0] = gq0
    g["xa_w_kv"][0] = gkv0
    g["xa_w_o"][0] = go0
    g["xa_norm"] = jnp.concatenate([gxn0, gxn1], axis=0)
    g["xa_mem_norm"] = jnp.concatenate([gmem0, gmem1], axis=0)
    tok = emit(1, g)

    dog = _matmul(dh, w["dn_w_out"][0], "nt", [BF16], name="dn_out_proj_dx", after=tok)
    g["dn_w_out"] = [_matmul(og, dh, "tn", [BF16], name="dn_out_proj_dw", tk=MM_DEEP)]
    dq, dk, dv, dgb, dz, g_on = _delta_bwd(q, k, v, gb, qkvz, w["dn_out_norm"], states, tms, dog, heads)
    dqkvz, dba, g_wconv, g_alog, g_dt = _dn_pre_bwd(qkvz, ba, wconv, alog, dtb, dq, dk, dv, dgb, dz, heads)
    g_qkvzt = _matmul(dqkvz, n0, "tn", [BF16], name="dn_in_proj_dw", tk=MM_DEEP)
    g_bat = _matmul(dba, n0, "tn", [BF16], name="dn_in_proj_ba_dw", tk=MM_DEEP)
    g["dn_w_in"] = [jnp.concatenate([g_qkvzt, g_bat[:2 * heads]], axis=0)]
    g["dn_w_conv"] = g_wconv[None]
    tok = emit(0, g)
    dn0a = _matmul(dba, w_bat, "nn", [F32], name="dn_in_proj_ba_dx", after=tok)
    grad_x, g_dnn = _matmul(dqkvz, w_int, "nn", [F32], name="dn_in_proj_dx", b_rows=4 * d,
                            epi=lambda acc, part, hx, dres, gain: rms_bwd_epi(acc + part, hx, dres, gain),
                            mn_extras=[dn0a, x, dh], row_extras=[dn_norm], n_rowsum=1, slab=EPI_SLAB)
    g.update(dn_norm=g_dnn, dn_out_norm=g_on,
             dn_a_log=g_alog[:, heads:2 * heads], dn_dt_bias=g_dt[:, heads:2 * heads])
    return loss, grad_x, g


def _round_up(n, m):
    return (n + m - 1) // m * m


def _pack_rows(parts, cols, row_mult):
    lead = parts[0].shape[:-1]
    flat, offs, off = [], [], 0
    for p in parts:
        n = _round_up(p.shape[-1], cols)
        flat.append(jnp.pad(p, [(0, 0)] * len(lead) + [(0, n - p.shape[-1])]))
        offs.append(off)
        off += n
    total = _round_up(off, cols * row_mult)
    if total > off:
        flat.append(jnp.zeros(lead + (total - off,), parts[0].dtype))
    return jnp.concatenate(flat, axis=-1).reshape(lead + (total // cols, cols)), offs


def _unpack(packed, offs, shapes):
    lead = packed.shape[:-2]
    flat = packed.reshape(lead + (-1,))
    out = []
    for off, shp in zip(offs, shapes):
        n = 1
        for v in shp:
            n *= v
        out.append(flat[..., off:off + n].reshape(lead + tuple(shp)))
    return out


def kernel(x, mem, dn_norm, dn_w_in, dn_w_conv, dn_a_log, dn_dt_bias, dn_out_norm, dn_w_out, cv_norm, cv_w_pw1, cv_b_pw1, cv_w_dw, cv_b_dw, cv_ln_g, cv_ln_b, cv_w_pw2, cv_b_pw2, xa_norm, xa_mem_norm, xa_w_q, xa_w_kv, xa_w_o, mlp_norm, mlp_w_up, mlp_w_down, final_norm, loss_target, m_dn_norm, m_dn_w_in, m_dn_w_conv, m_dn_a_log, m_dn_dt_bias, m_dn_out_norm, m_dn_w_out, m_cv_norm, m_cv_w_pw1, m_cv_b_pw1, m_cv_w_dw, m_cv_b_dw, m_cv_ln_g, m_cv_ln_b, m_cv_w_pw2, m_cv_b_pw2, m_xa_norm, m_xa_mem_norm, m_xa_w_q, m_xa_w_kv, m_xa_w_o, m_mlp_norm, m_mlp_w_up, m_mlp_w_down, m_final_norm, v_dn_norm, v_dn_w_in, v_dn_w_conv, v_dn_a_log, v_dn_dt_bias, v_dn_out_norm, v_dn_w_out, v_cv_norm, v_cv_w_pw1, v_cv_b_pw1, v_cv_w_dw, v_cv_b_dw, v_cv_ln_g, v_cv_ln_b, v_cv_w_pw2, v_cv_b_pw2, v_xa_norm, v_xa_mem_norm, v_xa_w_q, v_xa_w_kv, v_xa_w_o, v_mlp_norm, v_mlp_w_up, v_mlp_w_down, v_final_norm):
    wsh = dict(dn_norm=dn_norm, dn_w_in=dn_w_in, dn_w_conv=dn_w_conv, dn_a_log=dn_a_log, dn_dt_bias=dn_dt_bias, dn_out_norm=dn_out_norm, dn_w_out=dn_w_out, cv_norm=cv_norm, cv_w_pw1=cv_w_pw1, cv_b_pw1=cv_b_pw1, cv_w_dw=cv_w_dw, cv_b_dw=cv_b_dw, cv_ln_g=cv_ln_g, cv_ln_b=cv_ln_b, cv_w_pw2=cv_w_pw2, cv_b_pw2=cv_b_pw2, xa_norm=xa_norm, xa_mem_norm=xa_mem_norm, xa_w_q=xa_w_q, xa_w_kv=xa_w_kv, xa_w_o=xa_w_o, mlp_norm=mlp_norm, mlp_w_up=mlp_w_up, mlp_w_down=mlp_w_down, final_norm=final_norm)
    msh = dict(dn_norm=m_dn_norm, dn_w_in=m_dn_w_in, dn_w_conv=m_dn_w_conv, dn_a_log=m_dn_a_log, dn_dt_bias=m_dn_dt_bias, dn_out_norm=m_dn_out_norm, dn_w_out=m_dn_w_out, cv_norm=m_cv_norm, cv_w_pw1=m_cv_w_pw1, cv_b_pw1=m_cv_b_pw1, cv_w_dw=m_cv_w_dw, cv_b_dw=m_cv_b_dw, cv_ln_g=m_cv_ln_g, cv_ln_b=m_cv_ln_b, cv_w_pw2=m_cv_w_pw2, cv_b_pw2=m_cv_b_pw2, xa_norm=m_xa_norm, xa_mem_norm=m_xa_mem_norm, xa_w_q=m_xa_w_q, xa_w_kv=m_xa_w_kv, xa_w_o=m_xa_w_o, mlp_norm=m_mlp_norm, mlp_w_up=m_mlp_w_up, mlp_w_down=m_mlp_w_down, final_norm=m_final_norm)
    vsh = dict(dn_norm=v_dn_norm, dn_w_in=v_dn_w_in, dn_w_conv=v_dn_w_conv, dn_a_log=v_dn_a_log, dn_dt_bias=v_dn_dt_bias, dn_out_norm=v_dn_out_norm, dn_w_out=v_dn_w_out, cv_norm=v_cv_norm, cv_w_pw1=v_cv_w_pw1, cv_b_pw1=v_cv_b_pw1, cv_w_dw=v_cv_w_dw, cv_b_dw=v_cv_b_dw, cv_ln_g=v_cv_ln_g, cv_ln_b=v_cv_ln_b, cv_w_pw2=v_cv_w_pw2, cv_b_pw2=v_cv_b_pw2, xa_norm=v_xa_norm, xa_mem_norm=v_xa_mem_norm, xa_w_q=v_xa_w_q, xa_w_kv=v_xa_w_kv, xa_w_o=v_xa_w_o, mlp_norm=v_mlp_norm, mlp_w_up=v_mlp_w_up, mlp_w_down=v_mlp_w_down, final_norm=v_final_norm)

    big_axis = dict(BIG)
    for src in (wsh, msh, vsh):
        src["dn_w_in"] = jnp.swapaxes(src["dn_w_in"], 1, 2)
    big_axis["dn_w_in"] = 1

    small_pack, small_offs = _pack_rows([wsh[nm].reshape(-1) for nm in SMALL_SH], LANES, 8)
    w = {nm: [None] * wsh[nm].shape[0] for nm in big_axis}

    def put_weights(group, gathered):
        for (nm, layer), gth in zip(group, gathered):
            if big_axis[nm] == 1:
                w[nm][layer] = gth.reshape(N_DEV * gth.shape[1], gth.shape[2])
            else:
                w[nm][layer] = gth

    first = _all_gather([wsh[nm][layer].astype(BF16) for nm, layer in GATHER_GROUPS[0]] + [small_pack],
                        "weights_all_gather_0")
    put_weights(GATHER_GROUPS[0], first)
    me = _dev_index(*_mesh_me())
    gather_handles, tokens = {}, []
    for gi in range(1, len(GATHER_GROUPS)):
        shards = [wsh[nm][layer].astype(BF16) for nm, layer in GATHER_GROUPS[gi]]
        lands = [lax.dynamic_update_slice(lax.empty((N_DEV,) + s.shape, s.dtype), s[None], (me, 0, 0))
                 for s in shards]
        gather_handles[gi], tok = _exchange_start(shards, lands, False, f"weights_gather_{gi}",
                                                  after=[first[-1]] + tokens)
        tokens.append(tok)
    for nm, gth in zip(SMALL_SH, _unpack(first[-1], small_offs, [wsh[nm].shape for nm in SMALL_SH])):
        w[nm] = jnp.moveaxis(gth, 0, -2).reshape(gth.shape[1:-1] + (N_DEV * gth.shape[-1],))
    for nm in REPL:
        w[nm] = wsh[nm]

    def fetch(gi, after):
        put_weights(GATHER_GROUPS[gi], _exchange_wait(gather_handles[gi], after)[1])

    scatter_handles = {}

    def emit(gi, g):
        blocks = []
        for nm, layer in SCATTER_GROUPS[gi]:
            gw = g[nm][layer]
            if big_axis[nm] == 1:
                gw = gw.reshape(N_DEV, gw.shape[0] // N_DEV, gw.shape[1])
            blocks.append(gw)
        if gi == 0:
            gsmall_pack, _ = _pack_rows(
                [jnp.moveaxis(g[nm].reshape(g[nm].shape[:-1] + (N_DEV, -1)), -2, 0).reshape(N_DEV, -1)
                 for nm in SMALL_SH], LANES, 8)
            blocks.append(gsmall_pack)
        lands = [lax.empty(b.shape, b.dtype) for b in blocks]
        scatter_handles[gi], tok = _exchange_start(blocks, lands, True, f"grads_scatter_{gi}")
        return [tok]

    loss_part, grad_x, g = _local_step(x[0], mem[0], loss_target[0], w, fetch, emit, tokens)

    recv = {nm: [None] * wsh[nm].shape[0] for nm in big_axis}
    sent = {nm: [None] * wsh[nm].shape[0] for nm in big_axis}
    gsh, delta, new_m, new_v = {}, {}, {}, {}
    after = [grad_x]
    done = set()
    me_arr = me.astype(jnp.int32).reshape(1)
    def small_adamw(names, name):
        packs = []
        for src in (wsh, gsh, msh, vsh):
            pk, offs = _pack_rows([src[nm].reshape(-1) for nm in names], LANES, 8)
            packs.append(pk)
        outs = _adamw(*packs, name)
        for dst, pk in zip((delta, new_m, new_v), outs):
            for nm, val in zip(names, _unpack(pk, offs, [wsh[nm].shape for nm in names])):
                dst[nm] = val
        return outs[0]

    for gi in reversed(range(len(SCATTER_GROUPS))):
        if gi == 0:
            repl_pack, repl_offs = _pack_rows([g[nm].reshape(-1) for nm in REPL] + [loss_part[:, :1].reshape(-1)],
                                              LANES, 8)
            (repl_all,) = _all_gather([repl_pack], "repl_grads_all_gather")
            repl_red = _slot_sum(repl_all, "repl_grads_sum", SLOT_SUM_ROWS)
            *repl_vals, loss_sum = _unpack(repl_red, repl_offs, [wsh[nm].shape for nm in REPL] + [(1,)])
            for nm, val in zip(REPL, repl_vals):
                gsh[nm] = val
            after = after + [small_adamw(list(REPL), "adamw_repl")]
        sources, landed = _exchange_wait(scatter_handles[gi], after)
        for (nm, layer), src, r in zip(SCATTER_GROUPS[gi], sources, landed):
            sent[nm][layer], recv[nm][layer] = src, r
        if gi == 0:
            slot = lax.broadcasted_iota(jnp.int32, landed[-1].shape, 0)
            rsmall = jnp.where(slot == me, sources[-1], landed[-1])
        for nm in big_axis:
            if nm not in done and all(r is not None for r in recv[nm]):
                gsh[nm], delta[nm], new_m[nm], new_v[nm] = _adamw_reduce(
                    me_arr, recv[nm], sent[nm], wsh[nm], msh[nm], vsh[nm], f"adamw_{nm}")
                done.add(nm)
                after = [delta[nm]]
    gsmall_red = _slot_sum(rsmall, "grads_small_sum", SLOT_SUM_ROWS)
    for nm, val in zip(SMALL_SH, _unpack(gsmall_red, small_offs, [wsh[nm].shape for nm in SMALL_SH])):
        gsh[nm] = val
    small_adamw(list(SMALL_SH), "adamw_small")
    for dst in (gsh, delta, new_m, new_v):
        dst["dn_w_in"] = jnp.swapaxes(dst["dn_w_in"], 1, 2)
    return (loss_sum.reshape(()), grad_x[None], *[gsh[nm] for nm in WEIGHTS], *[delta[nm] for nm in WEIGHTS],
            *[new_m[nm] for nm in WEIGHTS], *[new_v[nm] for nm in WEIGHTS])
```

```python
import functools

import jax
import jax.numpy as jnp
from jax import lax
from jax.experimental import pallas as pl
from jax.experimental.pallas import tpu as pltpu

F32 = jnp.float32
BF16 = jnp.bfloat16
MESH_IDS = pl.DeviceIdType.MESH

N_DEV = 8
LANES = 128
RMS_EPS = 1e-6
LN_EPS = 1e-5
DN_HEAD_DIM = 128
DN_CONV = 4
DN_CHUNK = 64
CV_WIDTH = 31
XA_HEADS = 4
MM_TILE = 1024
MM_DEEP = 2048
EPI_SLAB = 256
SLOT_SUM_ROWS = 512
DN_HALO = 8
CV_HALO = 32

ADAM_LR = 0.001
ADAM_B1 = 0.9
ADAM_B2 = 0.999
ADAM_EPS = 1e-08
ADAM_WD = 0.01
ADAM_STEP = 10

BIG = (("dn_w_in", 2), ("dn_w_out", 1), ("cv_w_pw1", 2), ("cv_w_pw2", 1), ("xa_w_q", 1), ("xa_w_kv", 2),
       ("xa_w_o", 1), ("mlp_w_up", 2), ("mlp_w_down", 1))
_LAYER_GROUP = ("xa_w_q", "xa_w_o", "mlp_w_down", "xa_w_kv", "mlp_w_up")
GATHER_GROUPS = (
    (("dn_w_in", 0),),
    (("dn_w_out", 0),) + tuple((nm, 0) for nm in _LAYER_GROUP),
    (("cv_w_pw2", 0), ("cv_w_pw1", 0)),
    tuple((nm, 1) for nm in _LAYER_GROUP),
)
SCATTER_GROUPS = (
    (("dn_w_out", 0), ("dn_w_in", 0)),
    tuple((nm, 0) for nm in _LAYER_GROUP),
    (("cv_w_pw2", 0), ("cv_w_pw1", 0)),
    tuple((nm, 1) for nm in _LAYER_GROUP),
)
SMALL_SH = ("cv_norm", "cv_b_pw1", "cv_b_dw", "cv_ln_g", "cv_ln_b", "cv_b_pw2", "cv_w_dw", "dn_w_conv")
REPL = ("dn_norm", "dn_a_log", "dn_dt_bias", "dn_out_norm", "xa_norm", "xa_mem_norm", "mlp_norm", "final_norm")
WEIGHTS = ("dn_norm", "dn_w_in", "dn_w_conv", "dn_a_log", "dn_dt_bias", "dn_out_norm", "dn_w_out", "cv_norm",
           "cv_w_pw1", "cv_b_pw1", "cv_w_dw", "cv_b_dw", "cv_ln_g", "cv_ln_b", "cv_w_pw2", "cv_b_pw2", "xa_norm",
           "xa_mem_norm", "xa_w_q", "xa_w_kv", "xa_w_o", "mlp_norm", "mlp_w_up", "mlp_w_down", "final_norm")


def _dot_dims(mode, batched):
    o = 1 if batched else 0
    contract = {"nn": ((1 + o,), (o,)), "nt": ((1 + o,), (1 + o,)), "tn": ((o,), (o,))}[mode]
    return (contract, (((0,), (0,)) if batched else ((), ())))


def _bdot(a, b, mode):
    return lax.dot_general(a.astype(BF16), b.astype(BF16), _dot_dims(mode, a.ndim == 3),
                           preferred_element_type=F32)


@functools.partial(jax.custom_vjp, nondiff_argnums=(2,))
def _mm(a, b, mode):
    return _bdot(a, b, mode)


def _mm_fwd(a, b, mode):
    return _bdot(a, b, mode), (a, b)


def _mm_bwd(mode, res, ct):
    a, b = res
    if mode == "nn":
        da, db = _bdot(ct, b, "nt"), _bdot(a, ct, "tn")
    elif mode == "nt":
        da, db = _bdot(ct, b, "nn"), _bdot(ct, a, "tn")
    else:
        da, db = _bdot(b, ct, "nt"), _bdot(a, ct, "nn")
    return da.astype(a.dtype), db.astype(b.dtype)


_mm.defvjp(_mm_fwd, _mm_bwd)


def _sigmoid(x):
    return 0.5 * (jnp.tanh(0.5 * x) + 1.0)


def _silu(x):
    return x * _sigmoid(x)


def _softplus(x):
    return jnp.maximum(x, 0.0) + jnp.log(1.0 + jnp.exp(-jnp.abs(x)))


def _rms(x, g):
    r = lax.rsqrt(jnp.mean(x * x, axis=-1, keepdims=True) + RMS_EPS)
    return x * r * g


def _shift_rows(x, off):
    if off == 0:
        return x
    return pltpu.roll(x, x.shape[0] - off, 0)


def _series_dot(a, b, mode):
    return _bdot(a, b, mode)


def _chunk_masks(c):
    ii = lax.broadcasted_iota(jnp.int32, (c, c), 0)
    jj = lax.broadcasted_iota(jnp.int32, (c, c), 1)
    return (ii == jj).astype(F32), ii >= jj, ii > jj


def _neumann_inverse(lm):
    n = lm.shape[-1]
    t = -lm
    p = lm
    size = 2
    while size < n:
        size *= 2
        p = _series_dot(p, p, "nn")
        t = t + p + _series_dot(t, p, "nn")
    return t


def _apply_inverse(tm, rhs, mode):
    return rhs + _series_dot(tm, rhs, mode)


@jax.custom_vjp
def _unit_lower_solve(lm, rhs, tm):
    return _apply_inverse(tm, rhs, "nn")


def _uls_fwd(lm, rhs, tm):
    sol = _apply_inverse(tm, rhs, "nn")
    return sol, (tm, sol)


def _uls_bwd(res, ct):
    tm, sol = res
    d_rhs = _apply_inverse(tm, ct, "tn")
    return -_bdot(d_rhs, sol, "nt"), d_rhs, jnp.zeros_like(tm)


_unit_lower_solve.defvjp(_uls_fwd, _uls_bwd)


def _delta_chunk(q, k, v, gcol, bcol, s0, tm=None):
    c = q.shape[1]
    eye, causal, strict = _chunk_masks(c)
    grow = jnp.sum(eye * gcol, axis=1, keepdims=True)
    gc = jnp.sum(jnp.where(causal, grow, 0.0), axis=2, keepdims=True)
    gc_row = jnp.sum(eye * gc, axis=1, keepdims=True)
    decay = jnp.exp(jnp.where(causal, gc - gc_row, -jnp.inf))
    kb = k * bcol
    on_k = _mm(jnp.concatenate([kb, q], axis=1), k, "nt")
    lm = jnp.where(strict, on_k[:, :c] * decay, 0.0)
    attn = on_k[:, c:] * decay
    if tm is None:
        tm = _neumann_inverse(lax.stop_gradient(lm))
    egc = jnp.exp(gc)
    rhs = jnp.concatenate([v * bcol, kb * egc], axis=-1)
    sol = _unit_lower_solve(lm, rhs, tm)
    dv_ = v.shape[-1]
    u, w = sol[..., :dv_], sol[..., dv_:]
    gl = jnp.sum(grow, axis=2, keepdims=True)
    kd = k * jnp.exp(gl - gc)
    on_s = _mm(jnp.concatenate([w, q * egc], axis=1), s0, "nn")
    v_new = u - on_s[:, :c]
    o = on_s[:, c:] + _mm(attn, v_new, "nn")
    s1 = s0 * jnp.exp(gl) + _mm(kd, v_new, "tn")
    return o, s1, tm


def _dn_point(cv, ba, alog, dt, heads):
    a = _silu(cv)
    d = cv.shape[1] // 3
    qs, ks = [], []
    for h in range(heads):
        qh = a[:, h * DN_HEAD_DIM:(h + 1) * DN_HEAD_DIM]
        qs.append(qh * lax.rsqrt(jnp.sum(qh * qh, axis=-1, keepdims=True) + 1e-6) * (DN_HEAD_DIM ** -0.5))
        kh = a[:, d + h * DN_HEAD_DIM:d + (h + 1) * DN_HEAD_DIM]
        ks.append(kh * lax.rsqrt(jnp.sum(kh * kh, axis=-1, keepdims=True) + 1e-6))
    q = jnp.concatenate(qs, axis=-1)
    k = jnp.concatenate(ks, axis=-1)
    v = a[:, 2 * d:]
    lane = lax.broadcasted_iota(jnp.int32, ba.shape, 1)
    beta = _sigmoid(ba)
    g = -jnp.exp(alog) * _softplus(ba + dt)
    gb = jnp.where(lane < heads, beta, jnp.where(lane < 2 * heads, g, 0.0))
    return q, k, v, gb


def _attn_tile(q, k, v):
    hd = q.shape[1] // XA_HEADS
    outs = []
    for h in range(XA_HEADS):
        sl = slice(h * hd, (h + 1) * hd)
        s = _mm(q[:, sl], k[:, sl], "nt") * (hd ** -0.5)
        m = lax.stop_gradient(jnp.max(s, axis=-1, keepdims=True))
        e = jnp.exp(s - m)
        p = e / jnp.sum(e, axis=-1, keepdims=True)
        outs.append(_mm(p, v[:, sl], "nn"))
    return jnp.concatenate(outs, axis=-1)


def _ln_silu(c, g, b):
    mu = jnp.mean(c, axis=-1, keepdims=True)
    xc = c - mu
    y = xc * lax.rsqrt(jnp.mean(xc * xc, axis=-1, keepdims=True) + LN_EPS)
    return _silu(y * g + b)


def _causal_conv(xext, w, width, lead, ts):
    acc = None
    for j in range(width):
        term = _shift_rows(xext, lead + j)[:ts] * w[j:j + 1, :]
        acc = term if acc is None else acc + term
    return acc


def _colsum(x):
    return jnp.sum(x, axis=0, keepdims=True)


def _stack_rows(rows, n_rows):
    c = rows[0].shape[1]
    ridx = lax.broadcasted_iota(jnp.int32, (n_rows, c), 0)
    out = jnp.zeros((n_rows, c), F32)
    for j, r in enumerate(rows):
        out = out + jnp.where(ridx == j, r, 0.0)
    return out


def _matmul(a, b, mode, out_dtypes, *, name, epi=None, mn_extras=(), row_extras=(), out_dm=False, after=(),
            n_rowsum=0, slab=0, b_rows=None, a_pre=None, tm=MM_TILE, tn=MM_TILE, tk=MM_TILE):
    b_dm = b.ndim == 3
    b_shape = (b.shape[1], N_DEV * b.shape[2]) if b_dm else b.shape
    if b_rows is not None:
        assert not b_dm and b_rows <= b.shape[0]
        b_shape = (b_rows, b.shape[1])
    if mode == "nn":
        (m, k), (k2, n) = a.shape, b_shape
    elif mode == "nt":
        (m, k), (n, k2) = a.shape, b_shape
    else:
        (k, m), (k2, n) = a.shape, b_shape
    assert k == k2, (a.shape, b.shape, mode)
    tm, tn, tk = min(tm, m), min(tn, n), min(tk, k)
    cb, nb = 0, 1
    if b_dm:
        assert mode in ("nn", "nt")
        cb = b.shape[2]
        nb = max(1, (tn if mode == "nn" else tk) // cb)
        if mode == "nn":
            tn = nb * cb
        else:
            tk = nb * cb
    co, no = 0, 1
    if out_dm:
        co = n // N_DEV
        no = max(1, tn // co)
        tn = no * co
    assert m % tm == 0 and n % tn == 0 and k % tk == 0, (m, n, k, tm, tn, tk)
    nk = k // tk
    if mode == "tn":
        a_spec = pl.BlockSpec((tk, tm), lambda j, i, kk: (kk, i))
    else:
        a_spec = pl.BlockSpec((tm, tk), lambda j, i, kk: (i, kk))
    if b_dm:
        b_spec = (pl.BlockSpec((nb, tn, cb), lambda j, i, kk: (kk, j, 0)) if mode == "nt"
                  else pl.BlockSpec((nb, tk, cb), lambda j, i, kk: (j, kk, 0)))
    else:
        b_spec = (pl.BlockSpec((tn, tk), lambda j, i, kk: (j, kk)) if mode == "nt"
                  else pl.BlockSpec((tk, tn), lambda j, i, kk: (kk, j)))
    mn_spec = pl.BlockSpec((tm, tn), lambda j, i, kk: (i, j))
    row_spec = pl.BlockSpec((1, tn), lambda j, i, kk: (0, j))
    n_extra = len(mn_extras) + len(row_extras)
    n_out = len(out_dtypes)
    in_specs = ([a_spec, b_spec] + [mn_spec] * len(mn_extras) + [row_spec] * len(row_extras)
                + [_ANY_SPEC] * len(after))
    args = [a, b, *mn_extras, *row_extras, *after]
    if out_dm:
        out_specs = [pl.BlockSpec((no, tm, co), lambda j, i, kk: (j, i, 0))] * n_out
        out_shape = [jax.ShapeDtypeStruct((N_DEV, m, co), dt) for dt in out_dtypes]
    else:
        out_specs = [mn_spec] * n_out
        out_shape = [jax.ShapeDtypeStruct((m, n), dt) for dt in out_dtypes]
    out_specs = out_specs + [row_spec] * n_rowsum
    out_shape = out_shape + [jax.ShapeDtypeStruct((1, n), F32)] * n_rowsum
    n_in = len(args)
    n_mn = len(mn_extras)
    step = min(slab, tm) if slab else tm
    assert tm % step == 0

    def dot(a_ref, b_ref):
        a_val = a_ref[...] if a_pre is None else a_pre(a_ref[...])
        if not b_dm:
            return _bdot(a_val, b_ref[...], mode)
        if mode == "nn":
            parts = [_bdot(a_val, b_ref[dd], "nn") for dd in range(nb)]
            return parts[0] if nb == 1 else jnp.concatenate(parts, axis=1)
        out = None
        for dd in range(nb):
            part = _bdot(a_val[:, dd * cb:(dd + 1) * cb], b_ref[dd], "nt")
            out = part if out is None else out + part
        return out

    def finish(acc_src, extras, outs):
        sums = [None] * n_rowsum
        for r0 in range(0, tm, step):
            rs = slice(r0, r0 + step)
            acc_val = acc_src[rs, :]
            if epi is None:
                vals = (acc_val,)
            else:
                vals = epi(acc_val, *[e[rs, :] for e in extras[:n_mn]], *[e[...] for e in extras[n_mn:]])
            for o_ref, val in zip(outs[:n_out], vals[:n_out]):
                if out_dm:
                    for dd in range(no):
                        o_ref[dd, rs, :] = val[:, dd * co:(dd + 1) * co].astype(o_ref.dtype)
                else:
                    o_ref[rs, :] = val.astype(o_ref.dtype)
            for q in range(n_rowsum):
                sums[q] = vals[n_out + q] if sums[q] is None else sums[q] + vals[n_out + q]
        for q in range(n_rowsum):
            s_ref = outs[n_out + q]

            @pl.when(pl.program_id(1) == 0)
            def _():
                s_ref[...] = sums[q]

            @pl.when(pl.program_id(1) > 0)
            def _():
                s_ref[...] += sums[q]

    def body_one_step(*refs):
        finish(dot(refs[0], refs[1]), refs[2:2 + n_extra], refs[n_in:])

    def body(*refs):
        a_ref, b_ref = refs[0], refs[1]
        acc = refs[-1]
        kk = pl.program_id(2)

        @pl.when(kk == 0)
        def _():
            acc[...] = jnp.zeros_like(acc)

        acc[...] += dot(a_ref, b_ref)

        @pl.when(kk == nk - 1)
        def _():
            finish(acc, refs[2:2 + n_extra], refs[n_in:-1])

    res = pl.pallas_call(
        body_one_step if nk == 1 else body, name=name,
        grid=(n // tn, m // tm, nk),
        in_specs=in_specs, out_specs=out_specs, out_shape=out_shape,
        scratch_shapes=[] if nk == 1 else [pltpu.VMEM((tm, tn), F32)],
        compiler_params=pltpu.CompilerParams(
            dimension_semantics=("parallel", "arbitrary" if n_rowsum else "parallel", "arbitrary")),
    )(*args)
    return res[0] if n_out + n_rowsum == 1 else res


def _rowwise(fn, *, n_rows, ts, name, rows=(), prevs=(), nexts=(), vecs=(), row_outs=(), acc_outs=(), after=()):
    ts = min(ts, n_rows)
    assert n_rows % ts == 0
    nblk = n_rows // ts
    in_specs, args = [], []
    for arr, cb, w in rows:
        in_specs.append(pl.BlockSpec((ts, w), functools.partial(lambda i, cb: (i, cb), cb=cb)))
        args.append(arr)
    for arr, cb, w, halo in prevs:
        per = ts // halo
        in_specs.append(pl.BlockSpec(
            (halo, w), functools.partial(lambda i, cb, per: (jnp.maximum(i * per - 1, 0), cb), cb=cb, per=per)))
        args.append(arr)
    for arr, cb, w, halo in nexts:
        per = ts // halo
        last_blk = n_rows // halo - 1
        in_specs.append(pl.BlockSpec(
            (halo, w), functools.partial(lambda i, cb, per, lb: (jnp.minimum((i + 1) * per, lb), cb),
                                         cb=cb, per=per, lb=last_blk)))
        args.append(arr)
    for arr in vecs:
        in_specs.append(pl.BlockSpec(arr.shape, functools.partial(lambda i, nd: (0,) * nd, nd=arr.ndim)))
        args.append(arr)
    out_specs, out_shape = [], []
    for w, dt in row_outs:
        out_specs.append(pl.BlockSpec((ts, w), lambda i: (i, 0)))
        out_shape.append(jax.ShapeDtypeStruct((n_rows, w), dt))
    for shp in acc_outs:
        out_specs.append(pl.BlockSpec(shp, functools.partial(lambda i, nd: (0,) * nd, nd=len(shp))))
        out_shape.append(jax.ShapeDtypeStruct(shp, F32))
    n_used = len(args)
    n_tiles = n_used - len(vecs)
    in_specs += [_ANY_SPEC] * len(after)
    args += list(after)
    n_in, n_ro, n_acc = len(args), len(row_outs), len(acc_outs)

    def body(*refs):
        ins, ro, ac = refs[:n_used], refs[n_in:n_in + n_ro], refs[n_in + n_ro:]
        i = pl.program_id(0)
        rvals, avals = fn(i == 0, i == nblk - 1, *[r[...] for r in ins[:n_tiles]], *ins[n_tiles:])
        for r, val in zip(ro, rvals):
            r[...] = val.astype(r.dtype)
        if n_acc:
            @pl.when(i == 0)
            def _():
                for r in ac:
                    r[...] = jnp.zeros_like(r)

            for r, val in zip(ac, avals):
                r[...] += val

    res = pl.pallas_call(
        body, name=name, grid=(nblk,), in_specs=in_specs, out_specs=out_specs, out_shape=out_shape,
        compiler_params=pltpu.CompilerParams(dimension_semantics=("arbitrary",)),
    )(*args)
    return res


def _gated_out(o, z, onorm):
    return o * lax.rsqrt(jnp.mean(o * o, axis=-1, keepdims=True) + RMS_EPS) * onorm * _silu(z)


def _head_blocks(ref, heads, col0=0):
    return jnp.stack([ref[:, col0 + h * DN_HEAD_DIM:col0 + (h + 1) * DN_HEAD_DIM] for h in range(heads)])


def _split_heads(q_ref, k_ref, v_ref, gbv, heads):
    gcol = jnp.stack([gbv[:, heads + h:heads + h + 1] for h in range(heads)])
    bcol = jnp.stack([gbv[:, h:h + 1] for h in range(heads)])
    return _head_blocks(q_ref, heads), _head_blocks(k_ref, heads), _head_blocks(v_ref, heads), gcol, bcol


def _delta_fwd(q, k, v, gb, qkvz, onorm, heads):
    s, hd = q.shape
    n = s // DN_CHUNK
    blk = pl.BlockSpec((DN_CHUNK, hd), lambda c: (c, 0))
    gspec = pl.BlockSpec((DN_CHUNK, LANES), lambda c: (c, 0))

    def body(q_ref, k_ref, v_ref, gb_ref, z_ref, on_ref, og_ref, st_ref, tm_ref, state):
        @pl.when(pl.program_id(0) == 0)
        def _():
            state[...] = jnp.zeros_like(state)

        s0 = state[...]
        st_ref[0] = s0
        o, s1, tm = _delta_chunk(*_split_heads(q_ref, k_ref, v_ref, gb_ref[...], heads), s0)
        og = _gated_out(o, _head_blocks(z_ref, heads), on_ref[...])
        for h in range(heads):
            og_ref[:, h * DN_HEAD_DIM:(h + 1) * DN_HEAD_DIM] = og[h].astype(og_ref.dtype)
        state[...] = s1
        tm_ref[0] = tm

    return pl.pallas_call(
        body, name="dn_delta_fwd", grid=(n,),
        in_specs=[blk, blk, blk, gspec, pl.BlockSpec((DN_CHUNK, hd), lambda c: (c, 3)),
                  pl.BlockSpec(onorm.shape, lambda c: (0, 0))],
        out_specs=[blk, pl.BlockSpec((1, heads, DN_HEAD_DIM, DN_HEAD_DIM), lambda c: (c, 0, 0, 0)),
                   pl.BlockSpec((1, heads, DN_CHUNK, DN_CHUNK), lambda c: (c, 0, 0, 0))],
        out_shape=[jax.ShapeDtypeStruct((s, hd), BF16),
                   jax.ShapeDtypeStruct((n, heads, DN_HEAD_DIM, DN_HEAD_DIM), F32),
                   jax.ShapeDtypeStruct((n, heads, DN_CHUNK, DN_CHUNK), F32)],
        scratch_shapes=[pltpu.VMEM((heads, DN_HEAD_DIM, DN_HEAD_DIM), F32)],
        compiler_params=pltpu.CompilerParams(dimension_semantics=("arbitrary",)),
    )(q, k, v, gb, qkvz, onorm)


def _delta_bwd(q, k, v, gb, qkvz, onorm, states, tms, dog, heads):
    s, hd = q.shape
    n = s // DN_CHUNK
    blk = pl.BlockSpec((DN_CHUNK, hd), lambda c: (n - 1 - c, 0))
    gspec = pl.BlockSpec((DN_CHUNK, LANES), lambda c: (n - 1 - c, 0))
    sspec = pl.BlockSpec((1, heads, DN_HEAD_DIM, DN_HEAD_DIM), lambda c: (n - 1 - c, 0, 0, 0))
    tspec = pl.BlockSpec((1, heads, DN_CHUNK, DN_CHUNK), lambda c: (n - 1 - c, 0, 0, 0))
    nspec = pl.BlockSpec(onorm.shape, lambda c: (0, 0))

    def body(q_ref, k_ref, v_ref, gb_ref, z_ref, on_ref, st_ref, tm_ref, dog_ref,
             dq_ref, dk_ref, dv_ref, dgb_ref, dz_ref, don_ref, dstate):
        @pl.when(pl.program_id(0) == 0)
        def _():
            dstate[...] = jnp.zeros_like(dstate)
            don_ref[...] = jnp.zeros_like(don_ref)

        gbv = gb_ref[...]
        tm = tm_ref[0]

        def chunk(qh, kh, vh, gcol, bcol, s0, zh, on):
            o, s1, _ = _delta_chunk(qh, kh, vh, gcol, bcol, s0, tm)
            return _gated_out(o, zh, on), s1

        _, vjp = jax.vjp(chunk, *_split_heads(q_ref, k_ref, v_ref, gbv, heads), st_ref[0],
                         _head_blocks(z_ref, heads), on_ref[...])
        dq, dk, dv, dg, db, ds0, dz, don = vjp((_head_blocks(dog_ref, heads).astype(F32), dstate[...]))
        dstate[...] = ds0
        don_ref[...] += don
        lane = lax.broadcasted_iota(jnp.int32, gbv.shape, 1)
        dgb = jnp.zeros(gbv.shape, F32)
        for h in range(heads):
            sl = slice(h * DN_HEAD_DIM, (h + 1) * DN_HEAD_DIM)
            dq_ref[:, sl] = dq[h]
            dk_ref[:, sl] = dk[h]
            dv_ref[:, sl] = dv[h]
            dz_ref[:, sl] = dz[h]
            dgb = dgb + jnp.where(lane == h, db[h], 0.0) + jnp.where(lane == heads + h, dg[h], 0.0)
        dgb_ref[...] = dgb

    return pl.pallas_call(
        body, name="dn_delta_bwd", grid=(n,),
        in_specs=[blk, blk, blk, gspec, pl.BlockSpec((DN_CHUNK, hd), lambda c: (n - 1 - c, 3)), nspec,
                  sspec, tspec, blk],
        out_specs=[blk, blk, blk, gspec, blk, nspec],
        out_shape=[jax.ShapeDtypeStruct((s, hd), F32)] * 3 + [jax.ShapeDtypeStruct((s, LANES), F32),
                                                              jax.ShapeDtypeStruct((s, hd), F32),
                                                              jax.ShapeDtypeStruct(onorm.shape, F32)],
        scratch_shapes=[pltpu.VMEM((heads, DN_HEAD_DIM, DN_HEAD_DIM), F32)],
        compiler_params=pltpu.CompilerParams(dimension_semantics=("arbitrary",)),
    )(q, k, v, gb, qkvz, onorm, states, tms, dog)


def _dev_index(px, py, pc):
    return 4 * px + 2 * py + pc


def _all_gather(arrs, name):
    n = len(arrs)

    def body(*refs):
        xs, outs = refs[:n], refs[n:2 * n]
        send_sems, recv_sems, local_sems = refs[2 * n:]
        x, y, c = lax.axis_index("x"), lax.axis_index("y"), lax.axis_index("c")
        me, sibling = (x, y, c), (x, y, 1 - c)
        chips = [(1 - x, y), (x, 1 - y), (1 - x, 1 - y)]

        def copy(a, kk, block, to, src=None):
            dst = outs[a].at[_dev_index(*block)]
            return pltpu.make_async_remote_copy(
                src_ref=dst if src is None else src, dst_ref=dst,
                send_sem=send_sems.at[a * 7 + kk], recv_sem=recv_sems.at[a * 7 + kk],
                device_id=to, device_id_type=MESH_IDS)

        mine = [pltpu.make_async_copy(xs[a], outs[a].at[_dev_index(*me)], local_sems.at[a]) for a in range(n)]
        for cp in mine:
            cp.start()
        first = []
        for a in range(n):
            first.append(copy(a, 0, me, sibling, src=xs[a]))
            first += [copy(a, 1 + j, me, (*chip, c), src=xs[a]) for j, chip in enumerate(chips)]
        for cp in first:
            cp.start()
        passed = []
        for j, chip in enumerate(chips):
            for a in range(n):
                copy(a, 1 + j, (*chip, c), me).wait_recv()
                fwd = copy(a, 4 + j, (*chip, c), sibling)
                fwd.start()
                passed.append(fwd)
        for a in range(n):
            copy(a, 0, sibling, me).wait_recv()
        for j, chip in enumerate(chips):
            for a in range(n):
                copy(a, 4 + j, (*chip, 1 - c), me).wait_recv()
        for cp in first + passed:
            cp.wait_send()
        for cp in mine:
            cp.wait()

    hbm = pl.BlockSpec(memory_space=pltpu.HBM)
    res = pl.pallas_call(
        body, name=name,
        in_specs=[hbm] * n, out_specs=[hbm] * n,
        out_shape=[jax.ShapeDtypeStruct((N_DEV,) + a.shape, a.dtype) for a in arrs],
        scratch_shapes=[pltpu.SemaphoreType.DMA((7 * n,)), pltpu.SemaphoreType.DMA((7 * n,)),
                        pltpu.SemaphoreType.DMA((n,))],
    )(*arrs)
    return list(res)


_FLIPS = ((0, 0, 1), (1, 0, 0), (0, 1, 0), (1, 1, 0), (1, 0, 1), (0, 1, 1), (1, 1, 1))
_HBM_SPEC = pl.BlockSpec(memory_space=pltpu.HBM)
_SEM_SPEC = pl.BlockSpec(memory_space=pltpu.SEMAPHORE)
_ANY_SPEC = pl.BlockSpec(memory_space=pl.ANY)
_DATAFLOW = pltpu.SideEffectType.DATAFLOW_SIDE_EFFECTING
TOKEN_SHAPE = (8, LANES)


def _mesh_me():
    return lax.axis_index("x"), lax.axis_index("y"), lax.axis_index("c")


def _flipped(me, f):
    return tuple(1 - v if fl else v for v, fl in zip(me, f))


def _exchange_copies(xs, lands, send_sems, recv_sems, scatter, landed):
    me = _mesh_me()
    cps = []
    for kk, f in enumerate(_FLIPS):
        p = _flipped(me, f)
        for a in range(len(xs)):
            cps.append(pltpu.make_async_remote_copy(
                src_ref=xs[a].at[_dev_index(*p)] if scatter else xs[a],
                dst_ref=lands[a].at[_dev_index(*(p if landed else me))],
                send_sem=send_sems.at[a * 7 + kk], recv_sem=recv_sems.at[a * 7 + kk],
                device_id=p, device_id_type=MESH_IDS))
    return cps


def _exchange_start(srcs, lands, scatter, name, after=()):
    n = len(srcs)

    n_after = len(after)

    def body(*refs):
        xs, ls = refs[:n], refs[n:2 * n]
        send_sems, recv_sems = refs[2 * n + n_after], refs[2 * n + n_after + 1]
        token = refs[-1]
        for cp in _exchange_copies(xs, ls, send_sems, recv_sems, scatter, landed=False):
            cp.start()
        token[...] = jnp.zeros_like(token)

    operands = [pltpu.with_memory_space_constraint(a, pltpu.HBM) for a in list(srcs) + list(lands)]
    res = pl.pallas_call(
        body, name=name,
        in_specs=[_HBM_SPEC] * (2 * n) + [_ANY_SPEC] * len(after),
        out_specs=[_SEM_SPEC, _SEM_SPEC] + [_HBM_SPEC] * (2 * n) + [pl.BlockSpec(memory_space=pltpu.VMEM)],
        out_shape=[pltpu.SemaphoreType.DMA((7 * n,)), pltpu.SemaphoreType.DMA((7 * n,))]
        + [pltpu.HBM(a.shape, a.dtype) for a in operands] + [jax.ShapeDtypeStruct(TOKEN_SHAPE, F32)],
        input_output_aliases={i: 2 + i for i in range(2 * n)},
        compiler_params=pltpu.CompilerParams(has_side_effects=_DATAFLOW),
    )(*operands, *after)
    return (res[0], res[1], list(res[2:2 + n]), list(res[2 + n:2 + 2 * n]), scatter, name), res[-1]


def _exchange_wait(handle, after):
    send_sems, recv_sems, srcs, lands, scatter, name = handle
    n = len(srcs)
    n_after = len(after)

    def body(*refs):
        xs, ls = refs[:n], refs[n:2 * n]
        send_sems_ref, recv_sems_ref = refs[2 * n], refs[2 * n + 1]
        for cp in _exchange_copies(xs, ls, send_sems_ref, recv_sems_ref, scatter, landed=True):
            cp.wait_send()
            cp.wait_recv()

    res = pl.pallas_call(
        body, name=name + "_wait",
        in_specs=[_HBM_SPEC] * (2 * n) + [_SEM_SPEC, _SEM_SPEC] + [_ANY_SPEC] * n_after,
        out_specs=[_HBM_SPEC] * (2 * n),
        out_shape=[pltpu.HBM(a.shape, a.dtype) for a in srcs + lands],
        input_output_aliases={i: i for i in range(2 * n)},
        compiler_params=pltpu.CompilerParams(has_side_effects=_DATAFLOW),
    )(*srcs, *lands, send_sems, recv_sems, *after)
    return list(res[:n]), list(res[n:])


def _slot_sum(g, name, tr):
    _, r, c = g.shape
    tr = min(tr, r)
    assert r % tr == 0

    def body(g_ref, o_ref):
        acc = g_ref[0].astype(F32)
        for s in range(1, N_DEV):
            acc = acc + g_ref[s].astype(F32)
        o_ref[...] = acc

    return pl.pallas_call(
        body, name=name, grid=(r // tr,),
        in_specs=[pl.BlockSpec((N_DEV, tr, c), lambda i: (0, i, 0))],
        out_specs=pl.BlockSpec((tr, c), lambda i: (i, 0)),
        out_shape=jax.ShapeDtypeStruct((r, c), F32),
        compiler_params=pltpu.CompilerParams(dimension_semantics=("parallel",)),
    )(g)


def _adam_update(w, gg, m, v):
    c1 = 1.0 / (1.0 - ADAM_B1 ** ADAM_STEP)
    c2 = 1.0 / (1.0 - ADAM_B2 ** ADAM_STEP)
    nm = ADAM_B1 * m + (1.0 - ADAM_B1) * gg
    nv = ADAM_B2 * v + (1.0 - ADAM_B2) * (gg * gg)
    return -ADAM_LR * ((nm * c1) / (jnp.sqrt(nv * c2) + ADAM_EPS) + ADAM_WD * w), nm, nv


def _adamw_reduce(me, recvs, owns, w, m, v, name, tr=256):
    nl, r, c = w.shape
    assert len(recvs) == nl and len(owns) == nl
    tr = min(tr, r)
    if r % tr == 0:
        tc, nblk = c, r // tr
        at = lambda i: (i, 0)
    else:
        tr, tc = r, min(c, 4 * LANES)
        assert c % tc == 0
        nblk = c // tc
        at = lambda i: (0, i)

    def parked(li, l, i):
        return jnp.where(l < li, 0, jnp.where(l > li, nblk - 1, i))

    def recv_spec(li):
        return pl.BlockSpec((N_DEV, tr, tc), lambda l, i, me_ref: (0, *at(parked(li, l, i))))

    def own_spec(li):
        return pl.BlockSpec((None, tr, tc), lambda l, i, me_ref: (me_ref[0], *at(parked(li, l, i))))

    def body(me_ref, *refs):
        rrefs, orefs = refs[:nl], refs[nl:2 * nl]
        w_ref, m_ref, v_ref, g_ref, d_ref, nm_ref, nv_ref = refs[2 * nl:]
        l = pl.program_id(0)

        def of_layer(vals):
            out = vals[0]
            for li in range(1, nl):
                out = jnp.where(l == li, vals[li], out)
            return out

        own = of_layer([o[...].astype(F32) for o in orefs])
        gg = None
        for s in range(N_DEV):
            slot = jnp.where(me_ref[0] == s, own, of_layer([rr[s].astype(F32) for rr in rrefs]))
            gg = slot if gg is None else gg + slot
        g_ref[...] = gg
        d_ref[...], nm_ref[...], nv_ref[...] = _adam_update(w_ref[...], gg, m_ref[...], v_ref[...])

    spec = pl.BlockSpec((None, tr, tc), lambda l, i, me_ref: (l, *at(i)))
    return pl.pallas_call(
        body, name=name,
        grid_spec=pltpu.PrefetchScalarGridSpec(
            num_scalar_prefetch=1, grid=(nl, nblk),
            in_specs=[recv_spec(li) for li in range(nl)] + [own_spec(li) for li in range(nl)] + [spec] * 3,
            out_specs=[spec] * 4),
        out_shape=[jax.ShapeDtypeStruct((nl, r, c), F32)] * 4,
        compiler_params=pltpu.CompilerParams(dimension_semantics=("arbitrary", "arbitrary")),
    )(me, *recvs, *owns, w, m, v)


def _adamw(w, g, m, v, name, tr=256):
    r, c = w.shape
    tr = min(tr, r)
    assert r % tr == 0

    def body(w_ref, g_ref, m_ref, v_ref, d_ref, nm_ref, nv_ref):
        d_ref[...], nm_ref[...], nv_ref[...] = _adam_update(w_ref[...], g_ref[...], m_ref[...], v_ref[...])

    spec = pl.BlockSpec((tr, c), lambda i: (i, 0))
    return pl.pallas_call(
        body, name=name, grid=(r // tr,), in_specs=[spec] * 4, out_specs=[spec] * 3,
        out_shape=[jax.ShapeDtypeStruct((r, c), F32)] * 3,
        compiler_params=pltpu.CompilerParams(dimension_semantics=("parallel",)),
    )(w, g, m, v)


def _rms_fwd(x, g, name, ts=512, after=()):
    s, d = x.shape

    def fn(first, last, xv, gv):
        return [_rms(xv, gv[...])], []

    return _rowwise(fn, n_rows=s, ts=ts, name=name, rows=[(x, 0, d)], vecs=[g], row_outs=[(d, BF16)],
                    after=after)[0]


def _rms_bwd(x, dn, dres, g, name, ts=256):
    s, d = x.shape

    def fn(first, last, xv, dnv, drv, gv):
        _, vjp = jax.vjp(_rms, xv, gv[...])
        dx, dg = vjp(dnv.astype(F32))
        return [drv + dx], [dg]

    return _rowwise(fn, n_rows=s, ts=ts, name=name, rows=[(x, 0, d), (dn, 0, d), (dres, 0, d)], vecs=[g],
                    row_outs=[(d, F32)], acc_outs=[(1, d)])


def _dn_pre_fwd(qkvz, ba, wconv, alog, dt, heads, ts=256):
    s = qkvz.shape[0]
    d3 = wconv.shape[1]
    d = d3 // 3

    def fn(first, last, xc, bav, xp, wv, av, dv):
        xext = jnp.concatenate([jnp.where(first, 0.0, xp), xc], axis=0)
        cv = _causal_conv(xext, wv, DN_CONV, DN_HALO - (DN_CONV - 1), xc.shape[0])
        return list(_dn_point(cv, bav, av[...], dv[...], heads)), []

    return _rowwise(fn, n_rows=s, ts=ts, name="dn_pre_fwd", rows=[(qkvz, 0, d3), (ba, 0, LANES)],
                    prevs=[(qkvz, 0, d3, DN_HALO)], vecs=[wconv, alog, dt],
                    row_outs=[(d, F32), (d, F32), (d, F32), (LANES, F32)])


def _dn_pre_bwd(qkvz, ba, wconv, alog, dt, dq, dk, dv, dgb, dz, heads, ts=256):
    s = qkvz.shape[0]
    d3 = wconv.shape[1]
    d = d3 // 3
    lead = DN_HALO - (DN_CONV - 1)

    def fn(first, last, xc, bac, dqc, dkc, dvc, dgbc, dzc, xp, xn, ban, dqn, dkn, dvn, dgbn, wv, av, dtv):
        n = xc.shape[0]
        ext = lambda cur, nxt: jnp.concatenate([cur, nxt], axis=0)
        live = lambda nxt: jnp.where(last, 0.0, nxt)
        xall = jnp.concatenate([jnp.where(first, 0.0, xp), xc, live(xn)], axis=0)
        cv = _causal_conv(xall, wv, DN_CONV, lead, n + DN_HALO)
        (_, _, _, gbv), vjp = jax.vjp(lambda c, b: _dn_point(c, b, av[...], dtv[...], heads), cv, ext(bac, ban))
        dc, dba = vjp((ext(dqc, live(dqn)), ext(dkc, live(dkn)), ext(dvc, live(dvn)), ext(dgbc, live(dgbn))))
        dx = None
        dw = []
        for j in range(DN_CONV):
            term = _shift_rows(dc, DN_CONV - 1 - j)[:n] * wv[j:j + 1, :]
            dx = term if dx is None else dx + term
            dw.append(_colsum(dc[:n] * _shift_rows(xall, lead + j)[:n]))
        dba = dba[:n]
        return ([jnp.concatenate([dx, dzc], axis=-1), dba],
                [_stack_rows(dw, DN_CONV), _colsum(dgbc * gbv[:n]), _colsum(dba)])

    return _rowwise(fn, n_rows=s, ts=ts, name="dn_pre_bwd",
                    rows=[(qkvz, 0, d3), (ba, 0, LANES), (dq, 0, d), (dk, 0, d), (dv, 0, d), (dgb, 0, LANES),
                          (dz, 0, d)],
                    prevs=[(qkvz, 0, d3, DN_HALO)],
                    nexts=[(qkvz, 0, d3, DN_HALO), (ba, 0, LANES, DN_HALO), (dq, 0, d, DN_HALO),
                           (dk, 0, d, DN_HALO), (dv, 0, d, DN_HALO), (dgb, 0, LANES, DN_HALO)],
                    vecs=[wconv, alog, dt],
                    row_outs=[(4 * d, BF16), (LANES, BF16)], acc_outs=[(DN_CONV, d3), (1, LANES), (1, LANES)])


def _cv_mid_fwd(u, wdw, bdw, lng, lnb, ts=512):
    s = u.shape[0]
    d = u.shape[1] // 2

    def fn(first, last, uc, up, wv, bv, gv, lbv):
        uext = jnp.concatenate([jnp.where(first, 0.0, up), uc], axis=0)
        glu = uext[:, :d] * _sigmoid(uext[:, d:])
        c = _causal_conv(glu, wv, CV_WIDTH, CV_HALO - (CV_WIDTH - 1), uc.shape[0]) + bv[...]
        return [c, _ln_silu(c, gv[...], lbv[...])], []

    return _rowwise(fn, n_rows=s, ts=ts, name="cv_mid_fwd", rows=[(u, 0, 2 * d)], prevs=[(u, 0, 2 * d, CV_HALO)],
                    vecs=[wdw, bdw, lng, lnb], row_outs=[(d, F32), (d, BF16)])


def _cv_mid_bwd2(dc, u, wdw, ts=512):
    s, d = dc.shape

    def fn(first, last, dcc, uc, up, dcn, wv):
        n = dcc.shape[0]
        dcext = jnp.concatenate([dcc, jnp.where(last, 0.0, dcn)], axis=0)
        uext = jnp.concatenate([jnp.where(first, 0.0, up), uc], axis=0)
        glu = uext[:, :d] * _sigmoid(uext[:, d:])
        dglu = None
        dw = []
        for j in range(CV_WIDTH):
            term = _shift_rows(dcext, CV_WIDTH - 1 - j)[:n] * wv[j:j + 1, :]
            dglu = term if dglu is None else dglu + term
            dw.append(_colsum(dcc * _shift_rows(glu, CV_HALO - (CV_WIDTH - 1) + j)[:n]))
        u1, sg = uc[:, :d], _sigmoid(uc[:, d:])
        du = jnp.concatenate([dglu * sg, dglu * u1 * sg * (1.0 - sg)], axis=-1)
        return [du], [_stack_rows(dw, CV_HALO), _colsum(du)]

    return _rowwise(fn, n_rows=s, ts=ts, name="cv_mid_bwd2", rows=[(dc, 0, d), (u, 0, 2 * d)],
                    prevs=[(u, 0, 2 * d, CV_HALO)], nexts=[(dc, 0, d, CV_HALO)], vecs=[wdw],
                    row_outs=[(2 * d, BF16)], acc_outs=[(CV_HALO, d), (1, 2 * d)])


def _attn_fwd(q, k, v, name, ts=512):
    s, d = q.shape

    def fn(first, last, qv, kv, vv):
        return [_attn_tile(qv.astype(F32), kv[...].astype(F32), vv[...].astype(F32))], []

    return _rowwise(fn, n_rows=s, ts=ts, name=name, rows=[(q, 0, d)], vecs=[k, v], row_outs=[(d, BF16)])[0]


def _attn_bwd(q, k, v, do, name, ts=512):
    s, d = q.shape
    m = k.shape[0]

    def fn(first, last, qv, dov, kv, vv):
        _, vjp = jax.vjp(_attn_tile, qv.astype(F32), kv[...].astype(F32), vv[...].astype(F32))
        dq, dk, dv = vjp(dov.astype(F32))
        return [dq], [dk, dv]

    return _rowwise(fn, n_rows=s, ts=ts, name=name, rows=[(q, 0, d), (do, 0, d)], vecs=[k, v],
                    row_outs=[(d, BF16)], acc_outs=[(m, d), (m, d)])


def _pad_lanes(a, off=0):
    r, n = a.shape
    return jnp.pad(a, ((0, 0), (off, LANES - off - n)))


def _local_step(x, mem, tgt, w, fetch=None, emit=None, first_after=()):
    s, d = x.shape
    heads = d // DN_HEAD_DIM
    g = {}
    if fetch is None:
        fetch = lambda group, after: None
    if emit is None:
        emit = lambda group, grads: ()

    def add_res(acc, res):
        return (res + acc,)

    def add_res_rms(acc, res, gain):
        h = res + acc
        return h, _rms(h, gain)

    def rms_bwd_epi(acc, hx, dres, gain):
        _, vjp = jax.vjp(_rms, hx, gain)
        dx, dg = vjp(acc)
        return dres + dx, dg

    w_int = w["dn_w_in"][0]
    assert w_int.shape[0] == 4 * d + 2 * heads
    w_bat = jnp.pad(w_int[4 * d:], ((0, LANES - 2 * heads), (0, 0)))
    dn_norm = w["dn_norm"]
    alog = _pad_lanes(w["dn_a_log"], heads)
    dtb = _pad_lanes(w["dn_dt_bias"], heads)
    wconv = w["dn_w_conv"][0]
    n0 = _rms_fwd(x, dn_norm, "dn_rms", after=first_after)
    qkvz = _matmul(n0, w_int, "nt", [F32], name="dn_in_proj", b_rows=4 * d)
    ba = _matmul(n0, w_bat, "nt", [F32], name="dn_in_proj_ba")
    q, k, v, gb = _dn_pre_fwd(qkvz, ba, wconv, alog, dtb, heads)
    og, states, tms = _delta_fwd(q, k, v, gb, qkvz, w["dn_out_norm"], heads)
    fetch(1, [og])
    h1, nq0 = _matmul(og, w["dn_w_out"][0], "nn", [F32, BF16], name="dn_out_proj", epi=add_res_rms,
                      mn_extras=[x], row_extras=[w["xa_norm"][0:1]], slab=EPI_SLAB)

    def xattn_fwd(h, nq, layer, next_gain):
        qx = _matmul(nq, w["xa_w_q"][layer], "nn", [BF16], name=f"xa{layer}_q")
        mn = _rms_fwd(mem, w["xa_mem_norm"][layer:layer + 1], f"xa{layer}_mem_rms")
        kv = _matmul(mn, w["xa_w_kv"][layer], "nn", [BF16], name=f"xa{layer}_kv")
        kx, vx = kv[:, :d], kv[:, d:]
        ox = _attn_fwd(qx, kx, vx, f"xa{layer}_attn")
        hn, nn = _matmul(ox, w["xa_w_o"][layer], "nn", [F32, BF16], name=f"xa{layer}_o", epi=add_res_rms,
                         mn_extras=[h], row_extras=[next_gain], slab=EPI_SLAB)
        return hn, nn, (h, nq, qx, mn, kx, vx, ox)

    def sq_relu(t):
        r = jnp.maximum(t.astype(F32), 0.0)
        return r * r

    def loss_epi(acc, res, target, gain):
        def cols(hh, gg):
            e = _rms(hh, gg) - target
            return _colsum(e * e) * (0.5 / d)

        per_col, vjp = jax.vjp(cols, res + acc, gain)
        dhx, dgain = vjp(jnp.ones_like(per_col))
        return dhx, dgain, per_col

    def mlp_fwd(h, nm, layer, next_gain):
        u = _matmul(nm, w["mlp_w_up"][layer], "nn", [BF16], name=f"mlp{layer}_up")
        if next_gain is None:
            hn, *nn = _matmul(u, w["mlp_w_down"][layer], "nn", [F32], name=f"mlp{layer}_down_loss",
                              epi=loss_epi, mn_extras=[h, tgt], row_extras=[w["final_norm"].reshape(1, d)],
                              n_rowsum=2, tk=MM_DEEP, slab=EPI_SLAB, a_pre=sq_relu)
        else:
            hn, nn = _matmul(u, w["mlp_w_down"][layer], "nn", [F32, BF16], name=f"mlp{layer}_down",
                             epi=add_res_rms, mn_extras=[h], row_extras=[next_gain], tk=MM_DEEP, slab=EPI_SLAB,
                             a_pre=sq_relu)
        return hn, nn, (h, nm, u)

    h2, nm0, xa0 = xattn_fwd(h1, nq0, 0, w["mlp_norm"][0:1])
    fetch(2, [h2])
    h3, n1, mlp0 = mlp_fwd(h2, nm0, 0, w["cv_norm"])

    u_cv = _matmul(n1, w["cv_w_pw1"][0], "nn", [F32], name="cv_pw1", epi=lambda acc, b: (acc + b,),
                   row_extras=[w["cv_b_pw1"]])
    wdw = jnp.pad(w["cv_w_dw"][0], ((0, CV_HALO - CV_WIDTH), (0, 0)))
    c_cv, s_cv = _cv_mid_fwd(u_cv, wdw, w["cv_b_dw"], w["cv_ln_g"], w["cv_ln_b"])
    h4, nq1 = _matmul(s_cv, w["cv_w_pw2"][0], "nn", [F32, BF16], name="cv_pw2",
                      epi=lambda acc, res, b, gain: add_res_rms(acc + b, res, gain), mn_extras=[h3],
                      row_extras=[w["cv_b_pw2"], w["xa_norm"][1:2]], slab=EPI_SLAB)
    fetch(3, [h4])
    h5, nm1, xa1 = xattn_fwd(h4, nq1, 1, w["mlp_norm"][1:2])
    dh, (g_fn, loss_cols), mlp1 = mlp_fwd(h5, nm1, 1, None)
    g["final_norm"] = g_fn.reshape(d)
    loss = jnp.sum(loss_cols, axis=1, keepdims=True)

    def mlp_bwd(dh, layer, saved, after=()):
        h, nm, u = saved
        du = _matmul(dh, w["mlp_w_down"][layer], "nt", [BF16], name=f"mlp{layer}_down_dx", after=after,
                     epi=lambda acc, uu: (acc * 2.0 * jnp.maximum(uu.astype(F32), 0.0),), mn_extras=[u])
        gdown = _matmul(u, dh, "tn", [BF16], name=f"mlp{layer}_down_dw", tm=MM_DEEP, a_pre=sq_relu)
        dhn, gn = _matmul(du, w["mlp_w_up"][layer], "nt", [F32], name=f"mlp{layer}_up_dx", epi=rms_bwd_epi,
                          mn_extras=[h, dh], row_extras=[w["mlp_norm"][layer:layer + 1]], n_rowsum=1,
                          slab=EPI_SLAB, tk=MM_DEEP)
        gup = _matmul(nm, du, "tn", [BF16], name=f"mlp{layer}_up_dw", out_dm=True, tk=MM_DEEP)
        return dhn, gup, gdown, gn

    def xattn_bwd(dh, layer, saved):
        h, nq, qx, mn, kx, vx, ox = saved
        dox = _matmul(dh, w["xa_w_o"][layer], "nt", [BF16], name=f"xa{layer}_o_dx")
        go = _matmul(ox, dh, "tn", [BF16], name=f"xa{layer}_o_dw", tk=MM_DEEP)
        dqx, dkx, dvx = _attn_bwd(qx, kx, vx, dox, f"xa{layer}_attn_bwd")

        def epi(acc, hx, dres, gain):
            dhx, dg = rms_bwd_epi(acc, hx, dres, gain)
            return dhx, dg, _colsum(dhx)

        dhn, gn, dh_cols = _matmul(dqx, w["xa_w_q"][layer], "nt", [F32], name=f"xa{layer}_q_dx", epi=epi,
                                   mn_extras=[h, dh], row_extras=[w["xa_norm"][layer:layer + 1]], n_rowsum=2,
                                   slab=EPI_SLAB)
        gq = _matmul(nq, dqx, "tn", [BF16], name=f"xa{layer}_q_dw", tk=MM_DEEP)
        dkv = jnp.concatenate([dkx, dvx], axis=-1)
        gkv = _matmul(mn, dkv, "tn", [BF16], name=f"xa{layer}_kv_dw", out_dm=True)
        dmn = _matmul(dkv, w["xa_w_kv"][layer], "nt", [F32], name=f"xa{layer}_kv_dx", tk=MM_DEEP)
        _, gmem = _rms_bwd(mem, dmn, dmn, w["xa_mem_norm"][layer:layer + 1], f"xa{layer}_mem_rms_bwd")
        return dhn, gq, gkv, go, gn, gmem, dh_cols

    dh, gup1, gdown1, gmn1 = mlp_bwd(dh, 1, mlp1)
    dh, gq1, gkv1, go1, gxn1, gmem1, g_b2 = xattn_bwd(dh, 1, xa1)
    g.update(mlp_w_up=[None, gup1], mlp_w_down=[None, gdown1], xa_w_q=[None, gq1], xa_w_kv=[None, gkv1],
             xa_w_o=[None, go1])
    tok = emit(3, g)

    def ln_bwd_epi(acc, cx, gain, bias):
        _, vjp = jax.vjp(_ln_silu, cx, gain, bias)
        dc, dg, db = vjp(acc)
        return dc, dg, db, _colsum(dc)

    dc_cv, g_lng, g_lnb, g_bdw = _matmul(dh, w["cv_w_pw2"][0], "nt", [F32], name="cv_pw2_dx", after=tok,
                                        epi=ln_bwd_epi, mn_extras=[c_cv],
                                        row_extras=[w["cv_ln_g"], w["cv_ln_b"]], n_rowsum=3, slab=EPI_SLAB)
    g["cv_w_pw2"] = [_matmul(s_cv, dh, "tn", [BF16], name="cv_pw2_dw", tk=MM_DEEP)]
    du_cv, g_wdw, g_b1 = _cv_mid_bwd2(dc_cv, u_cv, wdw)
    g["cv_w_pw1"] = [_matmul(n1, du_cv, "tn", [BF16], name="cv_pw1_dw", out_dm=True, tk=MM_DEEP)]
    dh, g_cvn = _matmul(du_cv, w["cv_w_pw1"][0], "nt", [F32], name="cv_pw1_dx", epi=rms_bwd_epi,
                        mn_extras=[h3, dh], row_extras=[w["cv_norm"]], n_rowsum=1, slab=EPI_SLAB, tk=MM_DEEP)
    g.update(cv_ln_g=g_lng, cv_ln_b=g_lnb, cv_b_dw=g_bdw, cv_b_pw2=g_b2, cv_b_pw1=g_b1, cv_norm=g_cvn,
             cv_w_dw=g_wdw[:CV_WIDTH][None])

    tok = emit(2, g)
    dh, gup0, gdown0, gmn0 = mlp_bwd(dh, 0, mlp0, after=tok)
    dh, gq0, gkv0, go0, gxn0, gmem0, _ = xattn_bwd(dh, 0, xa0)
    g["mlp_w_up"][0] = gup0
    g["mlp_w_down"][0] = gdown0
    g["mlp_norm"] = jnp.concatenate([gmn0, gmn1], axis=0)
    g["xa_w_q"][0] = gq0
    g["xa_w_kv"][0] = gkv0
    g["xa_w_o"][0] = go0
    g["xa_norm"] = jnp.concatenate([gxn0, gxn1], axis=0)
    g["xa_mem_norm"] = jnp.concatenate([gmem0, gmem1], axis=0)
    tok = emit(1, g)

    dog = _matmul(dh, w["dn_w_out"][0], "nt", [BF16], name="dn_out_proj_dx", after=tok)
    g["dn_w_out"] = [_matmul(og, dh, "tn", [BF16], name="dn_out_proj_dw", tk=MM_DEEP)]
    dq, dk, dv, dgb, dz, g_on = _delta_bwd(q, k, v, gb, qkvz, w["dn_out_norm"], states, tms, dog, heads)
    dqkvz, dba, g_wconv, g_alog, g_dt = _dn_pre_bwd(qkvz, ba, wconv, alog, dtb, dq, dk, dv, dgb, dz, heads)
    g_qkvzt = _matmul(dqkvz, n0, "tn", [BF16], name="dn_in_proj_dw", tk=MM_DEEP)
    g_bat = _matmul(dba, n0, "tn", [BF16], name="dn_in_proj_ba_dw", tk=MM_DEEP)
    g["dn_w_in"] = [jnp.concatenate([g_qkvzt, g_bat[:2 * heads]], axis=0)]
    g["dn_w_conv"] = g_wconv[None]
    tok = emit(0, g)
    dn0a = _matmul(dba, w_bat, "nn", [F32], name="dn_in_proj_ba_dx", after=tok)
    grad_x, g_dnn = _matmul(dqkvz, w_int, "nn", [F32], name="dn_in_proj_dx", b_rows=4 * d,
                            epi=lambda acc, part, hx, dres, gain: rms_bwd_epi(acc + part, hx, dres, gain),
                            mn_extras=[dn0a, x, dh], row_extras=[dn_norm], n_rowsum=1, slab=EPI_SLAB)
    g.update(dn_norm=g_dnn, dn_out_norm=g_on,
             dn_a_log=g_alog[:, heads:2 * heads], dn_dt_bias=g_dt[:, heads:2 * heads])
    return loss, grad_x, g


def _round_up(n, m):
    return (n + m - 1) // m * m


def _pack_rows(parts, cols, row_mult):
    lead = parts[0].shape[:-1]
    flat, offs, off = [], [], 0
    for p in parts:
        n = _round_up(p.shape[-1], cols)
        flat.append(jnp.pad(p, [(0, 0)] * len(lead) + [(0, n - p.shape[-1])]))
        offs.append(off)
        off += n
    total = _round_up(off, cols * row_mult)
    if total > off:
        flat.append(jnp.zeros(lead + (total - off,), parts[0].dtype))
    return jnp.concatenate(flat, axis=-1).reshape(lead + (total // cols, cols)), offs


def _unpack(packed, offs, shapes):
    lead = packed.shape[:-2]
    flat = packed.reshape(lead + (-1,))
    out = []
    for off, shp in zip(offs, shapes):
        n = 1
        for v in shp:
            n *= v
        out.append(flat[..., off:off + n].reshape(lead + tuple(shp)))
    return out


def kernel(x, mem, dn_norm, dn_w_in, dn_w_conv, dn_a_log, dn_dt_bias, dn_out_norm, dn_w_out, cv_norm, cv_w_pw1, cv_b_pw1, cv_w_dw, cv_b_dw, cv_ln_g, cv_ln_b, cv_w_pw2, cv_b_pw2, xa_norm, xa_mem_norm, xa_w_q, xa_w_kv, xa_w_o, mlp_norm, mlp_w_up, mlp_w_down, final_norm, loss_target, m_dn_norm, m_dn_w_in, m_dn_w_conv, m_dn_a_log, m_dn_dt_bias, m_dn_out_norm, m_dn_w_out, m_cv_norm, m_cv_w_pw1, m_cv_b_pw1, m_cv_w_dw, m_cv_b_dw, m_cv_ln_g, m_cv_ln_b, m_cv_w_pw2, m_cv_b_pw2, m_xa_norm, m_xa_mem_norm, m_xa_w_q, m_xa_w_kv, m_xa_w_o, m_mlp_norm, m_mlp_w_up, m_mlp_w_down, m_final_norm, v_dn_norm, v_dn_w_in, v_dn_w_conv, v_dn_a_log, v_dn_dt_bias, v_dn_out_norm, v_dn_w_out, v_cv_norm, v_cv_w_pw1, v_cv_b_pw1, v_cv_w_dw, v_cv_b_dw, v_cv_ln_g, v_cv_ln_b, v_cv_w_pw2, v_cv_b_pw2, v_xa_norm, v_xa_mem_norm, v_xa_w_q, v_xa_w_kv, v_xa_w_o, v_mlp_norm, v_mlp_w_up, v_mlp_w_down, v_final_norm):
    wsh = dict(dn_norm=dn_norm, dn_w_in=dn_w_in, dn_w_conv=dn_w_conv, dn_a_log=dn_a_log, dn_dt_bias=dn_dt_bias, dn_out_norm=dn_out_norm, dn_w_out=dn_w_out, cv_norm=cv_norm, cv_w_pw1=cv_w_pw1, cv_b_pw1=cv_b_pw1, cv_w_dw=cv_w_dw, cv_b_dw=cv_b_dw, cv_ln_g=cv_ln_g, cv_ln_b=cv_ln_b, cv_w_pw2=cv_w_pw2, cv_b_pw2=cv_b_pw2, xa_norm=xa_norm, xa_mem_norm=xa_mem_norm, xa_w_q=xa_w_q, xa_w_kv=xa_w_kv, xa_w_o=xa_w_o, mlp_norm=mlp_norm, mlp_w_up=mlp_w_up, mlp_w_down=mlp_w_down, final_norm=final_norm)
    msh = dict(dn_norm=m_dn_norm, dn_w_in=m_dn_w_in, dn_w_conv=m_dn_w_conv, dn_a_log=m_dn_a_log, dn_dt_bias=m_dn_dt_bias, dn_out_norm=m_dn_out_norm, dn_w_out=m_dn_w_out, cv_norm=m_cv_norm, cv_w_pw1=m_cv_w_pw1, cv_b_pw1=m_cv_b_pw1, cv_w_dw=m_cv_w_dw, cv_b_dw=m_cv_b_dw, cv_ln_g=m_cv_ln_g, cv_ln_b=m_cv_ln_b, cv_w_pw2=m_cv_w_pw2, cv_b_pw2=m_cv_b_pw2, xa_norm=m_xa_norm, xa_mem_norm=m_xa_mem_norm, xa_w_q=m_xa_w_q, xa_w_kv=m_xa_w_kv, xa_w_o=m_xa_w_o, mlp_norm=m_mlp_norm, mlp_w_up=m_mlp_w_up, mlp_w_down=m_mlp_w_down, final_norm=m_final_norm)
    vsh = dict(dn_norm=v_dn_norm, dn_w_in=v_dn_w_in, dn_w_conv=v_dn_w_conv, dn_a_log=v_dn_a_log, dn_dt_bias=v_dn_dt_bias, dn_out_norm=v_dn_out_norm, dn_w_out=v_dn_w_out, cv_norm=v_cv_norm, cv_w_pw1=v_cv_w_pw1, cv_b_pw1=v_cv_b_pw1, cv_w_dw=v_cv_w_dw, cv_b_dw=v_cv_b_dw, cv_ln_g=v_cv_ln_g, cv_ln_b=v_cv_ln_b, cv_w_pw2=v_cv_w_pw2, cv_b_pw2=v_cv_b_pw2, xa_norm=v_xa_norm, xa_mem_norm=v_xa_mem_norm, xa_w_q=v_xa_w_q, xa_w_kv=v_xa_w_kv, xa_w_o=v_xa_w_o, mlp_norm=v_mlp_norm, mlp_w_up=v_mlp_w_up, mlp_w_down=v_mlp_w_down, final_norm=v_final_norm)

    big_axis = dict(BIG)
    for src in (wsh, msh, vsh):
        src["dn_w_in"] = jnp.swapaxes(src["dn_w_in"], 1, 2)
    big_axis["dn_w_in"] = 1

    small_pack, small_offs = _pack_rows([wsh[nm].reshape(-1) for nm in SMALL_SH], LANES, 8)
    w = {nm: [None] * wsh[nm].shape[0] for nm in big_axis}

    def put_weights(group, gathered):
        for (nm, layer), gth in zip(group, gathered):
            if big_axis[nm] == 1:
                w[nm][layer] = gth.reshape(N_DEV * gth.shape[1], gth.shape[2])
            else:
                w[nm][layer] = gth

    first = _all_gather([wsh[nm][layer].astype(BF16) for nm, layer in GATHER_GROUPS[0]] + [small_pack],
                        "weights_all_gather_0")
    put_weights(GATHER_GROUPS[0], first)
    me = _dev_index(*_mesh_me())
    gather_handles, tokens = {}, []
    for gi in range(1, len(GATHER_GROUPS)):
        shards = [wsh[nm][layer].astype(BF16) for nm, layer in GATHER_GROUPS[gi]]
        lands = [lax.dynamic_update_slice(lax.empty((N_DEV,) + s.shape, s.dtype), s[None], (me, 0, 0))
                 for s in shards]
        gather_handles[gi], tok = _exchange_start(shards, lands, False, f"weights_gather_{gi}",
                                                  after=[first[-1]] + tokens)
        tokens.append(tok)
    for nm, gth in zip(SMALL_SH, _unpack(first[-1], small_offs, [wsh[nm].shape for nm in SMALL_SH])):
        w[nm] = jnp.moveaxis(gth, 0, -2).reshape(gth.shape[1:-1] + (N_DEV * gth.shape[-1],))
    for nm in REPL:
        w[nm] = wsh[nm]

    def fetch(gi, after):
        put_weights(GATHER_GROUPS[gi], _exchange_wait(gather_handles[gi], after)[1])

    scatter_handles = {}

    def emit(gi, g):
        blocks = []
        for nm, layer in SCATTER_GROUPS[gi]:
            gw = g[nm][layer]
            if big_axis[nm] == 1:
                gw = gw.reshape(N_DEV, gw.shape[0] // N_DEV, gw.shape[1])
            blocks.append(gw)
        if gi == 0:
            gsmall_pack, _ = _pack_rows(
                [jnp.moveaxis(g[nm].reshape(g[nm].shape[:-1] + (N_DEV, -1)), -2, 0).reshape(N_DEV, -1)
                 for nm in SMALL_SH], LANES, 8)
            blocks.append(gsmall_pack)
        lands = [lax.empty(b.shape, b.dtype) for b in blocks]
        scatter_handles[gi], tok = _exchange_start(blocks, lands, True, f"grads_scatter_{gi}")
        return [tok]

    loss_part, grad_x, g = _local_step(x[0], mem[0], loss_target[0], w, fetch, emit, tokens)

    recv = {nm: [None] * wsh[nm].shape[0] for nm in big_axis}
    sent = {nm: [None] * wsh[nm].shape[0] for nm in big_axis}
    gsh, delta, new_m, new_v = {}, {}, {}, {}
    after = [grad_x]
    done = set()
    me_arr = me.astype(jnp.int32).reshape(1)
    def small_adamw(names, name):
        packs = []
        for src in (wsh, gsh, msh, vsh):
            pk, offs = _pack_rows([src[nm].reshape(-1) for nm in names], LANES, 8)
            packs.append(pk)
        outs = _adamw(*packs, name)
        for dst, pk in zip((delta, new_m, new_v), outs):
            for nm, val in zip(names, _unpack(pk, offs, [wsh[nm].shape for nm in names])):
                dst[nm] = val
        return outs[0]

    for gi in reversed(range(len(SCATTER_GROUPS))):
        if gi == 0:
            repl_pack, repl_offs = _pack_rows([g[nm].reshape(-1) for nm in REPL] + [loss_part[:, :1].reshape(-1)],
                                              LANES, 8)
            (repl_all,) = _all_gather([repl_pack], "repl_grads_all_gather")
            repl_red = _slot_sum(repl_all, "repl_grads_sum", SLOT_SUM_ROWS)
            *repl_vals, loss_sum = _unpack(repl_red, repl_offs, [wsh[nm].shape for nm in REPL] + [(1,)])
            for nm, val in zip(REPL, repl_vals):
                gsh[nm] = val
            after = after + [small_adamw(list(REPL), "adamw_repl")]
        sources, landed = _exchange_wait(scatter_handles[gi], after)
        for (nm, layer), src, r in zip(SCATTER_GROUPS[gi], sources, landed):
            sent[nm][layer], recv[nm][layer] = src, r
        if gi == 0:
            slot = lax.broadcasted_iota(jnp.int32, landed[-1].shape, 0)
            rsmall = jnp.where(slot == me, sources[-1], landed[-1])
        for nm in big_axis:
            if nm not in done and all(r is not None for r in recv[nm]):
                gsh[nm], delta[nm], new_m[nm], new_v[nm] = _adamw_reduce(
                    me_arr, recv[nm], sent[nm], wsh[nm], msh[nm], vsh[nm], f"adamw_{nm}")
                done.add(nm)
                after = [delta[nm]]
    gsmall_red = _slot_sum(rsmall, "grads_small_sum", SLOT_SUM_ROWS)
    for nm, val in zip(SMALL_SH, _unpack(gsmall_red, small_offs, [wsh[nm].shape for nm in SMALL_SH])):
        gsh[nm] = val
    small_adamw(list(SMALL_SH), "adamw_small")
    for dst in (gsh, delta, new_m, new_v):
        dst["dn_w_in"] = jnp.swapaxes(dst["dn_w_in"], 1, 2)
    return (loss_sum.reshape(()), grad_x[None], *[gsh[nm] for nm in WEIGHTS], *[delta[nm] for nm in WEIGHTS],
            *[new_m[nm] for nm in WEIGHTS], *[new_v[nm] for nm in WEIGHTS])
```

```python
import functools

import jax
import jax.numpy as jnp
from jax import lax
from jax.experimental import pallas as pl
from jax.experimental.pallas import tpu as pltpu

F32 = jnp.float32
BF16 = jnp.bfloat16
MESH_IDS = pl.DeviceIdType.MESH

N_DEV = 8
LANES = 128
RMS_EPS = 1e-6
LN_EPS = 1e-5
DN_HEAD_DIM = 128
DN_CONV = 4
DN_CHUNK = 64
CV_WIDTH = 31
XA_HEADS = 4
MM_TILE = 1024
MM_DEEP = 2048
EPI_SLAB = 256
SLOT_SUM_ROWS = 512
DN_HALO = 8
CV_HALO = 32

ADAM_LR = 0.001
ADAM_B1 = 0.9
ADAM_B2 = 0.999
ADAM_EPS = 1e-08
ADAM_WD = 0.01
ADAM_STEP = 10

BIG = (("dn_w_in", 2), ("dn_w_out", 1), ("cv_w_pw1", 2), ("cv_w_pw2", 1), ("xa_w_q", 1), ("xa_w_kv", 2),
       ("xa_w_o", 1), ("mlp_w_up", 2), ("mlp_w_down", 1))
_LAYER_GROUP = ("xa_w_q", "xa_w_o", "mlp_w_down", "xa_w_kv", "mlp_w_up")
GATHER_GROUPS = (
    (("dn_w_in", 0),),
    (("dn_w_out", 0),) + tuple((nm, 0) for nm in _LAYER_GROUP),
    (("cv_w_pw2", 0), ("cv_w_pw1", 0)),
    (("xa_w_q", 1), ("xa_w_o", 1), ("xa_w_kv", 1)),
    (("mlp_w_down", 1), ("mlp_w_up", 1)),
)
SCATTER_GROUPS = (
    (("dn_w_in", 0),),
    (("dn_w_out", 0),) + tuple((nm, 0) for nm in _LAYER_GROUP),
    (("cv_w_pw2", 0), ("cv_w_pw1", 0)),
    tuple((nm, 1) for nm in _LAYER_GROUP),
)
SMALL_SH = ("cv_norm", "cv_b_pw1", "cv_b_dw", "cv_ln_g", "cv_ln_b", "cv_b_pw2", "cv_w_dw", "dn_w_conv")
REPL = ("dn_norm", "dn_a_log", "dn_dt_bias", "dn_out_norm", "xa_norm", "xa_mem_norm", "mlp_norm", "final_norm")
WEIGHTS = ("dn_norm", "dn_w_in", "dn_w_conv", "dn_a_log", "dn_dt_bias", "dn_out_norm", "dn_w_out", "cv_norm",
           "cv_w_pw1", "cv_b_pw1", "cv_w_dw", "cv_b_dw", "cv_ln_g", "cv_ln_b", "cv_w_pw2", "cv_b_pw2", "xa_norm",
           "xa_mem_norm", "xa_w_q", "xa_w_kv", "xa_w_o", "mlp_norm", "mlp_w_up", "mlp_w_down", "final_norm")


def _dot_dims(mode, batched):
    o = 1 if batched else 0
    contract = {"nn": ((1 + o,), (o,)), "nt": ((1 + o,), (1 + o,)), "tn": ((o,), (o,))}[mode]
    return (contract, (((0,), (0,)) if batched else ((), ())))


def _bdot(a, b, mode):
    return lax.dot_general(a.astype(BF16), b.astype(BF16), _dot_dims(mode, a.ndim == 3),
                           preferred_element_type=F32)


@functools.partial(jax.custom_vjp, nondiff_argnums=(2,))
def _mm(a, b, mode):
    return _bdot(a, b, mode)


def _mm_fwd(a, b, mode):
    return _bdot(a, b, mode), (a, b)


def _mm_bwd(mode, res, ct):
    a, b = res
    if mode == "nn":
        da, db = _bdot(ct, b, "nt"), _bdot(a, ct, "tn")
    elif mode == "nt":
        da, db = _bdot(ct, b, "nn"), _bdot(ct, a, "tn")
    else:
        da, db = _bdot(b, ct, "nt"), _bdot(a, ct, "nn")
    return da.astype(a.dtype), db.astype(b.dtype)


_mm.defvjp(_mm_fwd, _mm_bwd)


def _sigmoid(x):
    return 0.5 * (jnp.tanh(0.5 * x) + 1.0)


def _silu(x):
    return x * _sigmoid(x)


def _softplus(x):
    return jnp.maximum(x, 0.0) + jnp.log(1.0 + jnp.exp(-jnp.abs(x)))


def _rms(x, g):
    r = lax.rsqrt(jnp.mean(x * x, axis=-1, keepdims=True) + RMS_EPS)
    return x * r * g


def _shift_rows(x, off):
    if off == 0:
        return x
    return pltpu.roll(x, x.shape[0] - off, 0)


def _series_dot(a, b, mode):
    return _bdot(a, b, mode)


def _chunk_masks(c):
    ii = lax.broadcasted_iota(jnp.int32, (c, c), 0)
    jj = lax.broadcasted_iota(jnp.int32, (c, c), 1)
    return (ii == jj).astype(F32), ii >= jj, ii > jj


def _neumann_inverse(lm):
    n = lm.shape[-1]
    t = -lm
    p = lm
    size = 2
    while size < n:
        size *= 2
        p = _series_dot(p, p, "nn")
        t = t + p + _series_dot(t, p, "nn")
    return t


def _apply_inverse(tm, rhs, mode):
    return rhs + _series_dot(tm, rhs, mode)


@jax.custom_vjp
def _unit_lower_solve(lm, rhs, tm):
    return _apply_inverse(tm, rhs, "nn")


def _uls_fwd(lm, rhs, tm):
    sol = _apply_inverse(tm, rhs, "nn")
    return sol, (tm, sol)


def _uls_bwd(res, ct):
    tm, sol = res
    d_rhs = _apply_inverse(tm, ct, "tn")
    return -_bdot(d_rhs, sol, "nt"), d_rhs, jnp.zeros_like(tm)


_unit_lower_solve.defvjp(_uls_fwd, _uls_bwd)


def _delta_chunk(q, k, v, gcol, bcol, s0, tm=None):
    c = q.shape[1]
    eye, causal, strict = _chunk_masks(c)
    grow = jnp.sum(eye * gcol, axis=1, keepdims=True)
    gc = jnp.sum(jnp.where(causal, grow, 0.0), axis=2, keepdims=True)
    gc_row = jnp.sum(eye * gc, axis=1, keepdims=True)
    decay = jnp.exp(jnp.where(causal, gc - gc_row, -jnp.inf))
    kb = k * bcol
    on_k = _mm(jnp.concatenate([kb, q], axis=1), k, "nt")
    lm = jnp.where(strict, on_k[:, :c] * decay, 0.0)
    attn = on_k[:, c:] * decay
    if tm is None:
        tm = _neumann_inverse(lax.stop_gradient(lm))
    egc = jnp.exp(gc)
    rhs = jnp.concatenate([v * bcol, kb * egc], axis=-1)
    sol = _unit_lower_solve(lm, rhs, tm)
    dv_ = v.shape[-1]
    u, w = sol[..., :dv_], sol[..., dv_:]
    gl = jnp.sum(grow, axis=2, keepdims=True)
    kd = k * jnp.exp(gl - gc)
    on_s = _mm(jnp.concatenate([w, q * egc], axis=1), s0, "nn")
    v_new = u - on_s[:, :c]
    o = on_s[:, c:] + _mm(attn, v_new, "nn")
    s1 = s0 * jnp.exp(gl) + _mm(kd, v_new, "tn")
    return o, s1, tm


def _dn_point(cv, ba, alog, dt, heads):
    a = _silu(cv)
    d = cv.shape[1] // 3
    qs, ks = [], []
    for h in range(heads):
        qh = a[:, h * DN_HEAD_DIM:(h + 1) * DN_HEAD_DIM]
        qs.append(qh * lax.rsqrt(jnp.sum(qh * qh, axis=-1, keepdims=True) + 1e-6) * (DN_HEAD_DIM ** -0.5))
        kh = a[:, d + h * DN_HEAD_DIM:d + (h + 1) * DN_HEAD_DIM]
        ks.append(kh * lax.rsqrt(jnp.sum(kh * kh, axis=-1, keepdims=True) + 1e-6))
    q = jnp.concatenate(qs, axis=-1)
    k = jnp.concatenate(ks, axis=-1)
    v = a[:, 2 * d:]
    lane = lax.broadcasted_iota(jnp.int32, ba.shape, 1)
    beta = _sigmoid(ba)
    g = -jnp.exp(alog) * _softplus(ba + dt)
    gb = jnp.where(lane < heads, beta, jnp.where(lane < 2 * heads, g, 0.0))
    return q, k, v, gb


def _attn_tile(q, k, v):
    hd = q.shape[1] // XA_HEADS
    outs = []
    for h in range(XA_HEADS):
        sl = slice(h * hd, (h + 1) * hd)
        s = _mm(q[:, sl], k[:, sl], "nt") * (hd ** -0.5)
        m = lax.stop_gradient(jnp.max(s, axis=-1, keepdims=True))
        e = jnp.exp(s - m)
        p = e / jnp.sum(e, axis=-1, keepdims=True)
        outs.append(_mm(p, v[:, sl], "nn"))
    return jnp.concatenate(outs, axis=-1)


def _ln_silu(c, g, b):
    mu = jnp.mean(c, axis=-1, keepdims=True)
    xc = c - mu
    y = xc * lax.rsqrt(jnp.mean(xc * xc, axis=-1, keepdims=True) + LN_EPS)
    return _silu(y * g + b)


def _causal_conv(xext, w, width, lead, ts):
    acc = None
    for j in range(width):
        term = _shift_rows(xext, lead + j)[:ts] * w[j:j + 1, :]
        acc = term if acc is None else acc + term
    return acc


def _colsum(x):
    return jnp.sum(x, axis=0, keepdims=True)


def _stack_rows(rows, n_rows):
    c = rows[0].shape[1]
    ridx = lax.broadcasted_iota(jnp.int32, (n_rows, c), 0)
    out = jnp.zeros((n_rows, c), F32)
    for j, r in enumerate(rows):
        out = out + jnp.where(ridx == j, r, 0.0)
    return out


def _matmul(a, b, mode, out_dtypes, *, name, epi=None, mn_extras=(), row_extras=(), out_dm=False, after=(),
            n_rowsum=0, slab=0, b_rows=None, a_pre=None, tm=MM_TILE, tn=MM_TILE, tk=MM_TILE):
    b_dm = b.ndim == 3
    b_shape = (b.shape[1], N_DEV * b.shape[2]) if b_dm else b.shape
    if b_rows is not None:
        assert not b_dm and b_rows <= b.shape[0]
        b_shape = (b_rows, b.shape[1])
    if mode == "nn":
        (m, k), (k2, n) = a.shape, b_shape
    elif mode == "nt":
        (m, k), (n, k2) = a.shape, b_shape
    else:
        (k, m), (k2, n) = a.shape, b_shape
    assert k == k2, (a.shape, b.shape, mode)
    tm, tn, tk = min(tm, m), min(tn, n), min(tk, k)
    cb, nb = 0, 1
    if b_dm:
        assert mode in ("nn", "nt")
        cb = b.shape[2]
        nb = max(1, (tn if mode == "nn" else tk) // cb)
        if mode == "nn":
            tn = nb * cb
        else:
            tk = nb * cb
    co, no = 0, 1
    if out_dm:
        co = n // N_DEV
        no = max(1, tn // co)
        tn = no * co
    assert m % tm == 0 and n % tn == 0 and k % tk == 0, (m, n, k, tm, tn, tk)
    nk = k // tk
    if mode == "tn":
        a_spec = pl.BlockSpec((tk, tm), lambda j, i, kk: (kk, i))
    else:
        a_spec = pl.BlockSpec((tm, tk), lambda j, i, kk: (i, kk))
    if b_dm:
        b_spec = (pl.BlockSpec((nb, tn, cb), lambda j, i, kk: (kk, j, 0)) if mode == "nt"
                  else pl.BlockSpec((nb, tk, cb), lambda j, i, kk: (j, kk, 0)))
    else:
        b_spec = (pl.BlockSpec((tn, tk), lambda j, i, kk: (j, kk)) if mode == "nt"
                  else pl.BlockSpec((tk, tn), lambda j, i, kk: (kk, j)))
    mn_spec = pl.BlockSpec((tm, tn), lambda j, i, kk: (i, j))
    row_spec = pl.BlockSpec((1, tn), lambda j, i, kk: (0, j))
    n_extra = len(mn_extras) + len(row_extras)
    n_out = len(out_dtypes)
    in_specs = ([a_spec, b_spec] + [mn_spec] * len(mn_extras) + [row_spec] * len(row_extras)
                + [_ANY_SPEC] * len(after))
    args = [a, b, *mn_extras, *row_extras, *after]
    if out_dm:
        out_specs = [pl.BlockSpec((no, tm, co), lambda j, i, kk: (j, i, 0))] * n_out
        out_shape = [jax.ShapeDtypeStruct((N_DEV, m, co), dt) for dt in out_dtypes]
    else:
        out_specs = [mn_spec] * n_out
        out_shape = [jax.ShapeDtypeStruct((m, n), dt) for dt in out_dtypes]
    out_specs = out_specs + [row_spec] * n_rowsum
    out_shape = out_shape + [jax.ShapeDtypeStruct((1, n), F32)] * n_rowsum
    n_in = len(args)
    n_mn = len(mn_extras)
    step = min(slab, tm) if slab else tm
    assert tm % step == 0

    def dot(a_ref, b_ref):
        a_val = a_ref[...] if a_pre is None else a_pre(a_ref[...])
        if not b_dm:
            return _bdot(a_val, b_ref[...], mode)
        if mode == "nn":
            parts = [_bdot(a_val, b_ref[dd], "nn") for dd in range(nb)]
            return parts[0] if nb == 1 else jnp.concatenate(parts, axis=1)
        out = None
        for dd in range(nb):
            part = _bdot(a_val[:, dd * cb:(dd + 1) * cb], b_ref[dd], "nt")
            out = part if out is None else out + part
        return out

    def finish(acc_src, extras, outs):
        sums = [None] * n_rowsum
        for r0 in range(0, tm, step):
            rs = slice(r0, r0 + step)
            acc_val = acc_src[rs, :]
            if epi is None:
                vals = (acc_val,)
            else:
                vals = epi(acc_val, *[e[rs, :] for e in extras[:n_mn]], *[e[...] for e in extras[n_mn:]])
            for o_ref, val in zip(outs[:n_out], vals[:n_out]):
                if out_dm:
                    for dd in range(no):
                        o_ref[dd, rs, :] = val[:, dd * co:(dd + 1) * co].astype(o_ref.dtype)
                else:
                    o_ref[rs, :] = val.astype(o_ref.dtype)
            for q in range(n_rowsum):
                sums[q] = vals[n_out + q] if sums[q] is None else sums[q] + vals[n_out + q]
        for q in range(n_rowsum):
            s_ref = outs[n_out + q]

            @pl.when(pl.program_id(1) == 0)
            def _():
                s_ref[...] = sums[q]

            @pl.when(pl.program_id(1) > 0)
            def _():
                s_ref[...] += sums[q]

    def body_one_step(*refs):
        finish(dot(refs[0], refs[1]), refs[2:2 + n_extra], refs[n_in:])

    def body(*refs):
        a_ref, b_ref = refs[0], refs[1]
        acc = refs[-1]
        kk = pl.program_id(2)

        @pl.when(kk == 0)
        def _():
            acc[...] = jnp.zeros_like(acc)

        acc[...] += dot(a_ref, b_ref)

        @pl.when(kk == nk - 1)
        def _():
            finish(acc, refs[2:2 + n_extra], refs[n_in:-1])

    res = pl.pallas_call(
        body_one_step if nk == 1 else body, name=name,
        grid=(n // tn, m // tm, nk),
        in_specs=in_specs, out_specs=out_specs, out_shape=out_shape,
        scratch_shapes=[] if nk == 1 else [pltpu.VMEM((tm, tn), F32)],
        compiler_params=pltpu.CompilerParams(
            dimension_semantics=("parallel", "arbitrary" if n_rowsum else "parallel", "arbitrary")),
    )(*args)
    return res[0] if n_out + n_rowsum == 1 else res


def _rowwise(fn, *, n_rows, ts, name, rows=(), prevs=(), nexts=(), vecs=(), row_outs=(), acc_outs=(), after=()):
    ts = min(ts, n_rows)
    assert n_rows % ts == 0
    nblk = n_rows // ts
    in_specs, args = [], []
    for arr, cb, w in rows:
        in_specs.append(pl.BlockSpec((ts, w), functools.partial(lambda i, cb: (i, cb), cb=cb)))
        args.append(arr)
    for arr, cb, w, halo in prevs:
        per = ts // halo
        in_specs.append(pl.BlockSpec(
            (halo, w), functools.partial(lambda i, cb, per: (jnp.maximum(i * per - 1, 0), cb), cb=cb, per=per)))
        args.append(arr)
    for arr, cb, w, halo in nexts:
        per = ts // halo
        last_blk = n_rows // halo - 1
        in_specs.append(pl.BlockSpec(
            (halo, w), functools.partial(lambda i, cb, per, lb: (jnp.minimum((i + 1) * per, lb), cb),
                                         cb=cb, per=per, lb=last_blk)))
        args.append(arr)
    for arr in vecs:
        in_specs.append(pl.BlockSpec(arr.shape, functools.partial(lambda i, nd: (0,) * nd, nd=arr.ndim)))
        args.append(arr)
    out_specs, out_shape = [], []
    for w, dt in row_outs:
        out_specs.append(pl.BlockSpec((ts, w), lambda i: (i, 0)))
        out_shape.append(jax.ShapeDtypeStruct((n_rows, w), dt))
    for shp in acc_outs:
        out_specs.append(pl.BlockSpec(shp, functools.partial(lambda i, nd: (0,) * nd, nd=len(shp))))
        out_shape.append(jax.ShapeDtypeStruct(shp, F32))
    n_used = len(args)
    n_tiles = n_used - len(vecs)
    in_specs += [_ANY_SPEC] * len(after)
    args += list(after)
    n_in, n_ro, n_acc = len(args), len(row_outs), len(acc_outs)

    def body(*refs):
        ins, ro, ac = refs[:n_used], refs[n_in:n_in + n_ro], refs[n_in + n_ro:]
        i = pl.program_id(0)
        rvals, avals = fn(i == 0, i == nblk - 1, *[r[...] for r in ins[:n_tiles]], *ins[n_tiles:])
        for r, val in zip(ro, rvals):
            r[...] = val.astype(r.dtype)
        if n_acc:
            @pl.when(i == 0)
            def _():
                for r in ac:
                    r[...] = jnp.zeros_like(r)

            for r, val in zip(ac, avals):
                r[...] += val

    res = pl.pallas_call(
        body, name=name, grid=(nblk,), in_specs=in_specs, out_specs=out_specs, out_shape=out_shape,
        compiler_params=pltpu.CompilerParams(dimension_semantics=("arbitrary",)),
    )(*args)
    return res


def _gated_out(o, z, onorm):
    return o * lax.rsqrt(jnp.mean(o * o, axis=-1, keepdims=True) + RMS_EPS) * onorm * _silu(z)


def _head_blocks(ref, heads, col0=0):
    return jnp.stack([ref[:, col0 + h * DN_HEAD_DIM:col0 + (h + 1) * DN_HEAD_DIM] for h in range(heads)])


def _split_heads(q_ref, k_ref, v_ref, gbv, heads):
    gcol = jnp.stack([gbv[:, heads + h:heads + h + 1] for h in range(heads)])
    bcol = jnp.stack([gbv[:, h:h + 1] for h in range(heads)])
    return _head_blocks(q_ref, heads), _head_blocks(k_ref, heads), _head_blocks(v_ref, heads), gcol, bcol


def _delta_fwd(q, k, v, gb, qkvz, onorm, heads):
    s, hd = q.shape
    n = s // DN_CHUNK
    blk = pl.BlockSpec((DN_CHUNK, hd), lambda c: (c, 0))
    gspec = pl.BlockSpec((DN_CHUNK, LANES), lambda c: (c, 0))

    def body(q_ref, k_ref, v_ref, gb_ref, z_ref, on_ref, og_ref, st_ref, tm_ref, state):
        @pl.when(pl.program_id(0) == 0)
        def _():
            state[...] = jnp.zeros_like(state)

        s0 = state[...]
        st_ref[0] = s0
        o, s1, tm = _delta_chunk(*_split_heads(q_ref, k_ref, v_ref, gb_ref[...], heads), s0)
        og = _gated_out(o, _head_blocks(z_ref, heads), on_ref[...])
        for h in range(heads):
            og_ref[:, h * DN_HEAD_DIM:(h + 1) * DN_HEAD_DIM] = og[h].astype(og_ref.dtype)
        state[...] = s1
        tm_ref[0] = tm

    return pl.pallas_call(
        body, name="dn_delta_fwd", grid=(n,),
        in_specs=[blk, blk, blk, gspec, pl.BlockSpec((DN_CHUNK, hd), lambda c: (c, 3)),
                  pl.BlockSpec(onorm.shape, lambda c: (0, 0))],
        out_specs=[blk, pl.BlockSpec((1, heads, DN_HEAD_DIM, DN_HEAD_DIM), lambda c: (c, 0, 0, 0)),
                   pl.BlockSpec((1, heads, DN_CHUNK, DN_CHUNK), lambda c: (c, 0, 0, 0))],
        out_shape=[jax.ShapeDtypeStruct((s, hd), BF16),
                   jax.ShapeDtypeStruct((n, heads, DN_HEAD_DIM, DN_HEAD_DIM), F32),
                   jax.ShapeDtypeStruct((n, heads, DN_CHUNK, DN_CHUNK), F32)],
        scratch_shapes=[pltpu.VMEM((heads, DN_HEAD_DIM, DN_HEAD_DIM), F32)],
        compiler_params=pltpu.CompilerParams(dimension_semantics=("arbitrary",)),
    )(q, k, v, gb, qkvz, onorm)


def _delta_bwd(q, k, v, gb, qkvz, onorm, states, tms, dog, heads):
    s, hd = q.shape
    n = s // DN_CHUNK
    blk = pl.BlockSpec((DN_CHUNK, hd), lambda c: (n - 1 - c, 0))
    gspec = pl.BlockSpec((DN_CHUNK, LANES), lambda c: (n - 1 - c, 0))
    sspec = pl.BlockSpec((1, heads, DN_HEAD_DIM, DN_HEAD_DIM), lambda c: (n - 1 - c, 0, 0, 0))
    tspec = pl.BlockSpec((1, heads, DN_CHUNK, DN_CHUNK), lambda c: (n - 1 - c, 0, 0, 0))
    nspec = pl.BlockSpec(onorm.shape, lambda c: (0, 0))

    def body(q_ref, k_ref, v_ref, gb_ref, z_ref, on_ref, st_ref, tm_ref, dog_ref,
             dq_ref, dk_ref, dv_ref, dgb_ref, dz_ref, don_ref, dstate):
        @pl.when(pl.program_id(0) == 0)
        def _():
            dstate[...] = jnp.zeros_like(dstate)
            don_ref[...] = jnp.zeros_like(don_ref)

        gbv = gb_ref[...]
        tm = tm_ref[0]

        def chunk(qh, kh, vh, gcol, bcol, s0, zh, on):
            o, s1, _ = _delta_chunk(qh, kh, vh, gcol, bcol, s0, tm)
            return _gated_out(o, zh, on), s1

        _, vjp = jax.vjp(chunk, *_split_heads(q_ref, k_ref, v_ref, gbv, heads), st_ref[0],
                         _head_blocks(z_ref, heads), on_ref[...])
        dq, dk, dv, dg, db, ds0, dz, don = vjp((_head_blocks(dog_ref, heads).astype(F32), dstate[...]))
        dstate[...] = ds0
        don_ref[...] += don
        lane = lax.broadcasted_iota(jnp.int32, gbv.shape, 1)
        dgb = jnp.zeros(gbv.shape, F32)
        for h in range(heads):
            sl = slice(h * DN_HEAD_DIM, (h + 1) * DN_HEAD_DIM)
            dq_ref[:, sl] = dq[h]
            dk_ref[:, sl] = dk[h]
            dv_ref[:, sl] = dv[h]
            dz_ref[:, sl] = dz[h]
            dgb = dgb + jnp.where(lane == h, db[h], 0.0) + jnp.where(lane == heads + h, dg[h], 0.0)
        dgb_ref[...] = dgb

    return pl.pallas_call(
        body, name="dn_delta_bwd", grid=(n,),
        in_specs=[blk, blk, blk, gspec, pl.BlockSpec((DN_CHUNK, hd), lambda c: (n - 1 - c, 3)), nspec,
                  sspec, tspec, blk],
        out_specs=[blk, blk, blk, gspec, blk, nspec],
        out_shape=[jax.ShapeDtypeStruct((s, hd), F32)] * 3 + [jax.ShapeDtypeStruct((s, LANES), F32),
                                                              jax.ShapeDtypeStruct((s, hd), F32),
                                                              jax.ShapeDtypeStruct(onorm.shape, F32)],
        scratch_shapes=[pltpu.VMEM((heads, DN_HEAD_DIM, DN_HEAD_DIM), F32)],
        compiler_params=pltpu.CompilerParams(dimension_semantics=("arbitrary",)),
    )(q, k, v, gb, qkvz, onorm, states, tms, dog)


def _dev_index(px, py, pc):
    return 4 * px + 2 * py + pc


def _all_gather(arrs, name):
    n = len(arrs)

    def body(*refs):
        xs, outs = refs[:n], refs[n:2 * n]
        send_sems, recv_sems, local_sems = refs[2 * n:]
        x, y, c = lax.axis_index("x"), lax.axis_index("y"), lax.axis_index("c")
        me, sibling = (x, y, c), (x, y, 1 - c)
        chips = [(1 - x, y), (x, 1 - y), (1 - x, 1 - y)]

        def copy(a, kk, block, to, src=None):
            dst = outs[a].at[_dev_index(*block)]
            return pltpu.make_async_remote_copy(
                src_ref=dst if src is None else src, dst_ref=dst,
                send_sem=send_sems.at[a * 7 + kk], recv_sem=recv_sems.at[a * 7 + kk],
                device_id=to, device_id_type=MESH_IDS)

        mine = [pltpu.make_async_copy(xs[a], outs[a].at[_dev_index(*me)], local_sems.at[a]) for a in range(n)]
        for cp in mine:
            cp.start()
        first = []
        for a in range(n):
            first.append(copy(a, 0, me, sibling, src=xs[a]))
            first += [copy(a, 1 + j, me, (*chip, c), src=xs[a]) for j, chip in enumerate(chips)]
        for cp in first:
            cp.start()
        passed = []
        for j, chip in enumerate(chips):
            for a in range(n):
                copy(a, 1 + j, (*chip, c), me).wait_recv()
                fwd = copy(a, 4 + j, (*chip, c), sibling)
                fwd.start()
                passed.append(fwd)
        for a in range(n):
            copy(a, 0, sibling, me).wait_recv()
        for j, chip in enumerate(chips):
            for a in range(n):
                copy(a, 4 + j, (*chip, 1 - c), me).wait_recv()
        for cp in first + passed:
            cp.wait_send()
        for cp in mine:
            cp.wait()

    hbm = pl.BlockSpec(memory_space=pltpu.HBM)
    res = pl.pallas_call(
        body, name=name,
        in_specs=[hbm] * n, out_specs=[hbm] * n,
        out_shape=[jax.ShapeDtypeStruct((N_DEV,) + a.shape, a.dtype) for a in arrs],
        scratch_shapes=[pltpu.SemaphoreType.DMA((7 * n,)), pltpu.SemaphoreType.DMA((7 * n,)),
                        pltpu.SemaphoreType.DMA((n,))],
    )(*arrs)
    return list(res)


_FLIPS = ((0, 0, 1), (1, 0, 0), (0, 1, 0), (1, 1, 0), (1, 0, 1), (0, 1, 1), (1, 1, 1))
_HBM_SPEC = pl.BlockSpec(memory_space=pltpu.HBM)
_SEM_SPEC = pl.BlockSpec(memory_space=pltpu.SEMAPHORE)
_ANY_SPEC = pl.BlockSpec(memory_space=pl.ANY)
_DATAFLOW = pltpu.SideEffectType.DATAFLOW_SIDE_EFFECTING
TOKEN_SHAPE = (8, LANES)


def _mesh_me():
    return lax.axis_index("x"), lax.axis_index("y"), lax.axis_index("c")


def _flipped(me, f):
    return tuple(1 - v if fl else v for v, fl in zip(me, f))


def _exchange_copies(xs, lands, send_sems, recv_sems, scatter, landed):
    me = _mesh_me()
    cps = []
    for kk, f in enumerate(_FLIPS):
        p = _flipped(me, f)
        for a in range(len(xs)):
            cps.append(pltpu.make_async_remote_copy(
                src_ref=xs[a].at[_dev_index(*p)] if scatter else xs[a],
                dst_ref=lands[a].at[_dev_index(*(p if landed else me))],
                send_sem=send_sems.at[a * 7 + kk], recv_sem=recv_sems.at[a * 7 + kk],
                device_id=p, device_id_type=MESH_IDS))
    return cps


def _exchange_start(srcs, lands, scatter, name, after=()):
    n = len(srcs)

    n_after = len(after)

    def body(*refs):
        xs, ls = refs[:n], refs[n:2 * n]
        send_sems, recv_sems = refs[2 * n + n_after], refs[2 * n + n_after + 1]
        token = refs[-1]
        for cp in _exchange_copies(xs, ls, send_sems, recv_sems, scatter, landed=False):
            cp.start()
        token[...] = jnp.zeros_like(token)

    operands = [pltpu.with_memory_space_constraint(a, pltpu.HBM) for a in list(srcs) + list(lands)]
    res = pl.pallas_call(
        body, name=name,
        in_specs=[_HBM_SPEC] * (2 * n) + [_ANY_SPEC] * len(after),
        out_specs=[_SEM_SPEC, _SEM_SPEC] + [_HBM_SPEC] * (2 * n) + [pl.BlockSpec(memory_space=pltpu.VMEM)],
        out_shape=[pltpu.SemaphoreType.DMA((7 * n,)), pltpu.SemaphoreType.DMA((7 * n,))]
        + [pltpu.HBM(a.shape, a.dtype) for a in operands] + [jax.ShapeDtypeStruct(TOKEN_SHAPE, F32)],
        input_output_aliases={i: 2 + i for i in range(2 * n)},
        compiler_params=pltpu.CompilerParams(has_side_effects=_DATAFLOW),
    )(*operands, *after)
    return (res[0], res[1], list(res[2:2 + n]), list(res[2 + n:2 + 2 * n]), scatter, name), res[-1]


def _exchange_wait(handle, after):
    send_sems, recv_sems, srcs, lands, scatter, name = handle
    n = len(srcs)
    n_after = len(after)

    def body(*refs):
        xs, ls = refs[:n], refs[n:2 * n]
        send_sems_ref, recv_sems_ref = refs[2 * n], refs[2 * n + 1]
        for cp in _exchange_copies(xs, ls, send_sems_ref, recv_sems_ref, scatter, landed=True):
            cp.wait_send()
            cp.wait_recv()

    res = pl.pallas_call(
        body, name=name + "_wait",
        in_specs=[_HBM_SPEC] * (2 * n) + [_SEM_SPEC, _SEM_SPEC] + [_ANY_SPEC] * n_after,
        out_specs=[_HBM_SPEC] * (2 * n),
        out_shape=[pltpu.HBM(a.shape, a.dtype) for a in srcs + lands],
        input_output_aliases={i: i for i in range(2 * n)},
        compiler_params=pltpu.CompilerParams(has_side_effects=_DATAFLOW),
    )(*srcs, *lands, send_sems, recv_sems, *after)
    return list(res[:n]), list(res[n:])


def _slot_sum(g, name, tr):
    _, r, c = g.shape
    tr = min(tr, r)
    assert r % tr == 0

    def body(g_ref, o_ref):
        acc = g_ref[0].astype(F32)
        for s in range(1, N_DEV):
            acc = acc + g_ref[s].astype(F32)
        o_ref[...] = acc

    return pl.pallas_call(
        body, name=name, grid=(r // tr,),
        in_specs=[pl.BlockSpec((N_DEV, tr, c), lambda i: (0, i, 0))],
        out_specs=pl.BlockSpec((tr, c), lambda i: (i, 0)),
        out_shape=jax.ShapeDtypeStruct((r, c), F32),
        compiler_params=pltpu.CompilerParams(dimension_semantics=("parallel",)),
    )(g)


def _adam_update(w, gg, m, v):
    c1 = 1.0 / (1.0 - ADAM_B1 ** ADAM_STEP)
    c2 = 1.0 / (1.0 - ADAM_B2 ** ADAM_STEP)
    nm = ADAM_B1 * m + (1.0 - ADAM_B1) * gg
    nv = ADAM_B2 * v + (1.0 - ADAM_B2) * (gg * gg)
    return -ADAM_LR * ((nm * c1) / (jnp.sqrt(nv * c2) + ADAM_EPS) + ADAM_WD * w), nm, nv


def _adamw_reduce(me, recvs, owns, w, m, v, name, tr=256):
    nl, r, c = w.shape
    assert len(recvs) == nl and len(owns) == nl
    tr = min(tr, r)
    if r % tr == 0:
        tc, nblk = c, r // tr
        at = lambda i: (i, 0)
    else:
        tr, tc = r, min(c, 4 * LANES)
        assert c % tc == 0
        nblk = c // tc
        at = lambda i: (0, i)

    def parked(li, l, i):
        return jnp.where(l < li, 0, jnp.where(l > li, nblk - 1, i))

    def recv_spec(li):
        return pl.BlockSpec((N_DEV, tr, tc), lambda l, i, me_ref: (0, *at(parked(li, l, i))))

    def own_spec(li):
        return pl.BlockSpec((None, tr, tc), lambda l, i, me_ref: (me_ref[0], *at(parked(li, l, i))))

    def body(me_ref, *refs):
        rrefs, orefs = refs[:nl], refs[nl:2 * nl]
        w_ref, m_ref, v_ref, g_ref, d_ref, nm_ref, nv_ref = refs[2 * nl:]
        l = pl.program_id(0)

        def of_layer(vals):
            out = vals[0]
            for li in range(1, nl):
                out = jnp.where(l == li, vals[li], out)
            return out

        own = of_layer([o[...].astype(F32) for o in orefs])
        gg = None
        for s in range(N_DEV):
            slot = jnp.where(me_ref[0] == s, own, of_layer([rr[s].astype(F32) for rr in rrefs]))
            gg = slot if gg is None else gg + slot
        g_ref[...] = gg
        d_ref[...], nm_ref[...], nv_ref[...] = _adam_update(w_ref[...], gg, m_ref[...], v_ref[...])

    spec = pl.BlockSpec((None, tr, tc), lambda l, i, me_ref: (l, *at(i)))
    return pl.pallas_call(
        body, name=name,
        grid_spec=pltpu.PrefetchScalarGridSpec(
            num_scalar_prefetch=1, grid=(nl, nblk),
            in_specs=[recv_spec(li) for li in range(nl)] + [own_spec(li) for li in range(nl)] + [spec] * 3,
            out_specs=[spec] * 4),
        out_shape=[jax.ShapeDtypeStruct((nl, r, c), F32)] * 4,
        compiler_params=pltpu.CompilerParams(dimension_semantics=("arbitrary", "arbitrary")),
    )(me, *recvs, *owns, w, m, v)


def _adamw(w, g, m, v, name, tr=256):
    r, c = w.shape
    tr = min(tr, r)
    assert r % tr == 0

    def body(w_ref, g_ref, m_ref, v_ref, d_ref, nm_ref, nv_ref):
        d_ref[...], nm_ref[...], nv_ref[...] = _adam_update(w_ref[...], g_ref[...], m_ref[...], v_ref[...])

    spec = pl.BlockSpec((tr, c), lambda i: (i, 0))
    return pl.pallas_call(
        body, name=name, grid=(r // tr,), in_specs=[spec] * 4, out_specs=[spec] * 3,
        out_shape=[jax.ShapeDtypeStruct((r, c), F32)] * 3,
        compiler_params=pltpu.CompilerParams(dimension_semantics=("parallel",)),
    )(w, g, m, v)


def _rms_fwd(x, g, name, ts=512, after=()):
    s, d = x.shape

    def fn(first, last, xv, gv):
        return [_rms(xv, gv[...])], []

    return _rowwise(fn, n_rows=s, ts=ts, name=name, rows=[(x, 0, d)], vecs=[g], row_outs=[(d, BF16)],
                    after=after)[0]


def _rms_bwd(x, dn, dres, g, name, ts=256):
    s, d = x.shape

    def fn(first, last, xv, dnv, drv, gv):
        _, vjp = jax.vjp(_rms, xv, gv[...])
        dx, dg = vjp(dnv.astype(F32))
        return [drv + dx], [dg]

    return _rowwise(fn, n_rows=s, ts=ts, name=name, rows=[(x, 0, d), (dn, 0, d), (dres, 0, d)], vecs=[g],
                    row_outs=[(d, F32)], acc_outs=[(1, d)])


def _dn_pre_fwd(qkvz, ba, wconv, alog, dt, heads, ts=256):
    s = qkvz.shape[0]
    d3 = wconv.shape[1]
    d = d3 // 3

    def fn(first, last, xc, bav, xp, wv, av, dv):
        xext = jnp.concatenate([jnp.where(first, 0.0, xp), xc], axis=0)
        cv = _causal_conv(xext, wv, DN_CONV, DN_HALO - (DN_CONV - 1), xc.shape[0])
        return list(_dn_point(cv, bav, av[...], dv[...], heads)), []

    return _rowwise(fn, n_rows=s, ts=ts, name="dn_pre_fwd", rows=[(qkvz, 0, d3), (ba, 0, LANES)],
                    prevs=[(qkvz, 0, d3, DN_HALO)], vecs=[wconv, alog, dt],
                    row_outs=[(d, F32), (d, F32), (d, F32), (LANES, F32)])


def _dn_pre_bwd(qkvz, ba, wconv, alog, dt, dq, dk, dv, dgb, dz, heads, ts=256):
    s = qkvz.shape[0]
    d3 = wconv.shape[1]
    d = d3 // 3
    lead = DN_HALO - (DN_CONV - 1)

    def fn(first, last, xc, bac, dqc, dkc, dvc, dgbc, dzc, xp, xn, ban, dqn, dkn, dvn, dgbn, wv, av, dtv):
        n = xc.shape[0]
        ext = lambda cur, nxt: jnp.concatenate([cur, nxt], axis=0)
        live = lambda nxt: jnp.where(last, 0.0, nxt)
        xall = jnp.concatenate([jnp.where(first, 0.0, xp), xc, live(xn)], axis=0)
        cv = _causal_conv(xall, wv, DN_CONV, lead, n + DN_HALO)
        (_, _, _, gbv), vjp = jax.vjp(lambda c, b: _dn_point(c, b, av[...], dtv[...], heads), cv, ext(bac, ban))
        dc, dba = vjp((ext(dqc, live(dqn)), ext(dkc, live(dkn)), ext(dvc, live(dvn)), ext(dgbc, live(dgbn))))
        dx = None
        dw = []
        for j in range(DN_CONV):
            term = _shift_rows(dc, DN_CONV - 1 - j)[:n] * wv[j:j + 1, :]
            dx = term if dx is None else dx + term
            dw.append(_colsum(dc[:n] * _shift_rows(xall, lead + j)[:n]))
        dba = dba[:n]
        return ([jnp.concatenate([dx, dzc], axis=-1), dba],
                [_stack_rows(dw, DN_CONV), _colsum(dgbc * gbv[:n]), _colsum(dba)])

    return _rowwise(fn, n_rows=s, ts=ts, name="dn_pre_bwd",
                    rows=[(qkvz, 0, d3), (ba, 0, LANES), (dq, 0, d), (dk, 0, d), (dv, 0, d), (dgb, 0, LANES),
                          (dz, 0, d)],
                    prevs=[(qkvz, 0, d3, DN_HALO)],
                    nexts=[(qkvz, 0, d3, DN_HALO), (ba, 0, LANES, DN_HALO), (dq, 0, d, DN_HALO),
                           (dk, 0, d, DN_HALO), (dv, 0, d, DN_HALO), (dgb, 0, LANES, DN_HALO)],
                    vecs=[wconv, alog, dt],
                    row_outs=[(4 * d, BF16), (LANES, BF16)], acc_outs=[(DN_CONV, d3), (1, LANES), (1, LANES)])


def _cv_mid_fwd(u, wdw, bdw, lng, lnb, ts=512):
    s = u.shape[0]
    d = u.shape[1] // 2

    def fn(first, last, uc, up, wv, bv, gv, lbv):
        uext = jnp.concatenate([jnp.where(first, 0.0, up), uc], axis=0)
        glu = uext[:, :d] * _sigmoid(uext[:, d:])
        c = _causal_conv(glu, wv, CV_WIDTH, CV_HALO - (CV_WIDTH - 1), uc.shape[0]) + bv[...]
        return [c, _ln_silu(c, gv[...], lbv[...])], []

    return _rowwise(fn, n_rows=s, ts=ts, name="cv_mid_fwd", rows=[(u, 0, 2 * d)], prevs=[(u, 0, 2 * d, CV_HALO)],
                    vecs=[wdw, bdw, lng, lnb], row_outs=[(d, F32), (d, BF16)])


def _cv_mid_bwd2(dc, u, wdw, ts=512):
    s, d = dc.shape

    def fn(first, last, dcc, uc, up, dcn, wv):
        n = dcc.shape[0]
        dcext = jnp.concatenate([dcc, jnp.where(last, 0.0, dcn)], axis=0)
        uext = jnp.concatenate([jnp.where(first, 0.0, up), uc], axis=0)
        glu = uext[:, :d] * _sigmoid(uext[:, d:])
        dglu = None
        dw = []
        for j in range(CV_WIDTH):
            term = _shift_rows(dcext, CV_WIDTH - 1 - j)[:n] * wv[j:j + 1, :]
            dglu = term if dglu is None else dglu + term
            dw.append(_colsum(dcc * _shift_rows(glu, CV_HALO - (CV_WIDTH - 1) + j)[:n]))
        u1, sg = uc[:, :d], _sigmoid(uc[:, d:])
        du = jnp.concatenate([dglu * sg, dglu * u1 * sg * (1.0 - sg)], axis=-1)
        return [du], [_stack_rows(dw, CV_HALO), _colsum(du)]

    return _rowwise(fn, n_rows=s, ts=ts, name="cv_mid_bwd2", rows=[(dc, 0, d), (u, 0, 2 * d)],
                    prevs=[(u, 0, 2 * d, CV_HALO)], nexts=[(dc, 0, d, CV_HALO)], vecs=[wdw],
                    row_outs=[(2 * d, BF16)], acc_outs=[(CV_HALO, d), (1, 2 * d)])


def _attn_fwd(q, k, v, name, ts=512):
    s, d = q.shape

    def fn(first, last, qv, kv, vv):
        return [_attn_tile(qv.astype(F32), kv[...].astype(F32), vv[...].astype(F32))], []

    return _rowwise(fn, n_rows=s, ts=ts, name=name, rows=[(q, 0, d)], vecs=[k, v], row_outs=[(d, BF16)])[0]


def _attn_bwd(q, k, v, do, name, ts=512):
    s, d = q.shape
    m = k.shape[0]

    def fn(first, last, qv, dov, kv, vv):
        _, vjp = jax.vjp(_attn_tile, qv.astype(F32), kv[...].astype(F32), vv[...].astype(F32))
        dq, dk, dv = vjp(dov.astype(F32))
        return [dq], [dk, dv]

    return _rowwise(fn, n_rows=s, ts=ts, name=name, rows=[(q, 0, d), (do, 0, d)], vecs=[k, v],
                    row_outs=[(d, BF16)], acc_outs=[(m, d), (m, d)])


def _pad_lanes(a, off=0):
    r, n = a.shape
    return jnp.pad(a, ((0, 0), (off, LANES - off - n)))


def _local_step(x, mem, tgt, w, fetch=None, emit=None, first_after=()):
    s, d = x.shape
    heads = d // DN_HEAD_DIM
    g = {}
    if fetch is None:
        fetch = lambda group, after: None
    if emit is None:
        emit = lambda group, grads: ()

    def add_res(acc, res):
        return (res + acc,)

    def add_res_rms(acc, res, gain):
        h = res + acc
        return h, _rms(h, gain)

    def rms_bwd_epi(acc, hx, dres, gain):
        _, vjp = jax.vjp(_rms, hx, gain)
        dx, dg = vjp(acc)
        return dres + dx, dg

    w_int = w["dn_w_in"][0]
    assert w_int.shape[0] == 4 * d + 2 * heads
    w_bat = jnp.pad(w_int[4 * d:], ((0, LANES - 2 * heads), (0, 0)))
    dn_norm = w["dn_norm"]
    alog = _pad_lanes(w["dn_a_log"], heads)
    dtb = _pad_lanes(w["dn_dt_bias"], heads)
    wconv = w["dn_w_conv"][0]
    n0 = _rms_fwd(x, dn_norm, "dn_rms", after=first_after)
    qkvz = _matmul(n0, w_int, "nt", [F32], name="dn_in_proj", b_rows=4 * d)
    ba = _matmul(n0, w_bat, "nt", [F32], name="dn_in_proj_ba")
    q, k, v, gb = _dn_pre_fwd(qkvz, ba, wconv, alog, dtb, heads)
    og, states, tms = _delta_fwd(q, k, v, gb, qkvz, w["dn_out_norm"], heads)
    fetch(1, [og])
    h1, nq0 = _matmul(og, w["dn_w_out"][0], "nn", [F32, BF16], name="dn_out_proj", epi=add_res_rms,
                      mn_extras=[x], row_extras=[w["xa_norm"][0:1]], slab=EPI_SLAB)

    def xattn_fwd(h, nq, layer, next_gain):
        qx = _matmul(nq, w["xa_w_q"][layer], "nn", [BF16], name=f"xa{layer}_q")
        mn = _rms_fwd(mem, w["xa_mem_norm"][layer:layer + 1], f"xa{layer}_mem_rms")
        kv = _matmul(mn, w["xa_w_kv"][layer], "nn", [BF16], name=f"xa{layer}_kv")
        kx, vx = kv[:, :d], kv[:, d:]
        ox = _attn_fwd(qx, kx, vx, f"xa{layer}_attn")
        hn, nn = _matmul(ox, w["xa_w_o"][layer], "nn", [F32, BF16], name=f"xa{layer}_o", epi=add_res_rms,
                         mn_extras=[h], row_extras=[next_gain], slab=EPI_SLAB)
        return hn, nn, (h, nq, qx, mn, kx, vx, ox)

    def sq_relu(t):
        r = jnp.maximum(t.astype(F32), 0.0)
        return r * r

    def loss_epi(acc, res, target, gain):
        def cols(hh, gg):
            e = _rms(hh, gg) - target
            return _colsum(e * e) * (0.5 / d)

        per_col, vjp = jax.vjp(cols, res + acc, gain)
        dhx, dgain = vjp(jnp.ones_like(per_col))
        return dhx, dgain, per_col

    def mlp_fwd(h, nm, layer, next_gain):
        u = _matmul(nm, w["mlp_w_up"][layer], "nn", [BF16], name=f"mlp{layer}_up")
        if next_gain is None:
            hn, *nn = _matmul(u, w["mlp_w_down"][layer], "nn", [F32], name=f"mlp{layer}_down_loss",
                              epi=loss_epi, mn_extras=[h, tgt], row_extras=[w["final_norm"].reshape(1, d)],
                              n_rowsum=2, tk=MM_DEEP, slab=EPI_SLAB, a_pre=sq_relu)
        else:
            hn, nn = _matmul(u, w["mlp_w_down"][layer], "nn", [F32, BF16], name=f"mlp{layer}_down",
                             epi=add_res_rms, mn_extras=[h], row_extras=[next_gain], tk=MM_DEEP, slab=EPI_SLAB,
                             a_pre=sq_relu)
        return hn, nn, (h, nm, u)

    h2, nm0, xa0 = xattn_fwd(h1, nq0, 0, w["mlp_norm"][0:1])
    fetch(2, [h2])
    h3, n1, mlp0 = mlp_fwd(h2, nm0, 0, w["cv_norm"])

    u_cv = _matmul(n1, w["cv_w_pw1"][0], "nn", [F32], name="cv_pw1", epi=lambda acc, b: (acc + b,),
                   row_extras=[w["cv_b_pw1"]])
    wdw = jnp.pad(w["cv_w_dw"][0], ((0, CV_HALO - CV_WIDTH), (0, 0)))
    c_cv, s_cv = _cv_mid_fwd(u_cv, wdw, w["cv_b_dw"], w["cv_ln_g"], w["cv_ln_b"])
    h4, nq1 = _matmul(s_cv, w["cv_w_pw2"][0], "nn", [F32, BF16], name="cv_pw2",
                      epi=lambda acc, res, b, gain: add_res_rms(acc + b, res, gain), mn_extras=[h3],
                      row_extras=[w["cv_b_pw2"], w["xa_norm"][1:2]], slab=EPI_SLAB)
    fetch(3, [h4])
    h5, nm1, xa1 = xattn_fwd(h4, nq1, 1, w["mlp_norm"][1:2])
    fetch(4, [h5])
    dh, (g_fn, loss_cols), mlp1 = mlp_fwd(h5, nm1, 1, None)
    g["final_norm"] = g_fn.reshape(d)
    loss = jnp.sum(loss_cols, axis=1, keepdims=True)

    def mlp_bwd(dh, layer, saved, after=()):
        h, nm, u = saved
        du = _matmul(dh, w["mlp_w_down"][layer], "nt", [BF16], name=f"mlp{layer}_down_dx", after=after,
                     epi=lambda acc, uu: (acc * 2.0 * jnp.maximum(uu.astype(F32), 0.0),), mn_extras=[u])
        gdown = _matmul(u, dh, "tn", [BF16], name=f"mlp{layer}_down_dw", tm=MM_DEEP, a_pre=sq_relu)
        dhn, gn = _matmul(du, w["mlp_w_up"][layer], "nt", [F32], name=f"mlp{layer}_up_dx", epi=rms_bwd_epi,
                          mn_extras=[h, dh], row_extras=[w["mlp_norm"][layer:layer + 1]], n_rowsum=1,
                          slab=EPI_SLAB, tk=MM_DEEP)
        gup = _matmul(nm, du, "tn", [BF16], name=f"mlp{layer}_up_dw", out_dm=True, tk=MM_DEEP)
        return dhn, gup, gdown, gn

    def xattn_bwd(dh, layer, saved):
        h, nq, qx, mn, kx, vx, ox = saved
        dox = _matmul(dh, w["xa_w_o"][layer], "nt", [BF16], name=f"xa{layer}_o_dx")
        go = _matmul(ox, dh, "tn", [BF16], name=f"xa{layer}_o_dw", tk=MM_DEEP)
        dqx, dkx, dvx = _attn_bwd(qx, kx, vx, dox, f"xa{layer}_attn_bwd")

        def epi(acc, hx, dres, gain):
            dhx, dg = rms_bwd_epi(acc, hx, dres, gain)
            return dhx, dg, _colsum(dhx)

        dhn, gn, dh_cols = _matmul(dqx, w["xa_w_q"][layer], "nt", [F32], name=f"xa{layer}_q_dx", epi=epi,
                                   mn_extras=[h, dh], row_extras=[w["xa_norm"][layer:layer + 1]], n_rowsum=2,
                                   slab=EPI_SLAB)
        gq = _matmul(nq, dqx, "tn", [BF16], name=f"xa{layer}_q_dw", tk=MM_DEEP)
        dkv = jnp.concatenate([dkx, dvx], axis=-1)
        gkv = _matmul(mn, dkv, "tn", [BF16], name=f"xa{layer}_kv_dw", out_dm=True)
        dmn = _matmul(dkv, w["xa_w_kv"][layer], "nt", [F32], name=f"xa{layer}_kv_dx", tk=MM_DEEP)
        _, gmem = _rms_bwd(mem, dmn, dmn, w["xa_mem_norm"][layer:layer + 1], f"xa{layer}_mem_rms_bwd")
        return dhn, gq, gkv, go, gn, gmem, dh_cols

    dh, gup1, gdown1, gmn1 = mlp_bwd(dh, 1, mlp1)
    dh, gq1, gkv1, go1, gxn1, gmem1, g_b2 = xattn_bwd(dh, 1, xa1)
    g.update(mlp_w_up=[None, gup1], mlp_w_down=[None, gdown1], xa_w_q=[None, gq1], xa_w_kv=[None, gkv1],
             xa_w_o=[None, go1])
    tok = emit(3, g)

    def ln_bwd_epi(acc, cx, gain, bias):
        _, vjp = jax.vjp(_ln_silu, cx, gain, bias)
        dc, dg, db = vjp(acc)
        return dc, dg, db, _colsum(dc)

    dc_cv, g_lng, g_lnb, g_bdw = _matmul(dh, w["cv_w_pw2"][0], "nt", [F32], name="cv_pw2_dx", after=tok,
                                        epi=ln_bwd_epi, mn_extras=[c_cv],
                                        row_extras=[w["cv_ln_g"], w["cv_ln_b"]], n_rowsum=3, slab=EPI_SLAB)
    g["cv_w_pw2"] = [_matmul(s_cv, dh, "tn", [BF16], name="cv_pw2_dw", tk=MM_DEEP)]
    du_cv, g_wdw, g_b1 = _cv_mid_bwd2(dc_cv, u_cv, wdw)
    g["cv_w_pw1"] = [_matmul(n1, du_cv, "tn", [BF16], name="cv_pw1_dw", out_dm=True, tk=MM_DEEP)]
    dh, g_cvn = _matmul(du_cv, w["cv_w_pw1"][0], "nt", [F32], name="cv_pw1_dx", epi=rms_bwd_epi,
                        mn_extras=[h3, dh], row_extras=[w["cv_norm"]], n_rowsum=1, slab=EPI_SLAB, tk=MM_DEEP)
    g.update(cv_ln_g=g_lng, cv_ln_b=g_lnb, cv_b_dw=g_bdw, cv_b_pw2=g_b2, cv_b_pw1=g_b1, cv_norm=g_cvn,
             cv_w_dw=g_wdw[:CV_WIDTH][None])

    tok = emit(2, g)
    dh, gup0, gdown0, gmn0 = mlp_bwd(dh, 0, mlp0, after=tok)
    dh, gq0, gkv0, go0, gxn0, gmem0, _ = xattn_bwd(dh, 0, xa0)
    g["mlp_w_up"][0] = gup0
    g["mlp_w_down"][0] = gdown0
    g["mlp_norm"] = jnp.concatenate([gmn0, gmn1], axis=0)
    g["xa_w_q"][0] = gq0
    g["xa_w_kv"][0] = gkv0
    g["xa_w_o"][0] = go0
    g["xa_norm"] = jnp.concatenate([gxn0, gxn1], axis=0)
    g["xa_mem_norm"] = jnp.concatenate([gmem0, gmem1], axis=0)
    g["dn_w_out"] = [_matmul(og, dh, "tn", [BF16], name="dn_out_proj_dw", tk=MM_DEEP)]
    tok = emit(1, g)
    dog = _matmul(dh, w["dn_w_out"][0], "nt", [BF16], name="dn_out_proj_dx", after=tok)
    dq, dk, dv, dgb, dz, g_on = _delta_bwd(q, k, v, gb, qkvz, w["dn_out_norm"], states, tms, dog, heads)
    dqkvz, dba, g_wconv, g_alog, g_dt = _dn_pre_bwd(qkvz, ba, wconv, alog, dtb, dq, dk, dv, dgb, dz, heads)
    g_qkvzt = _matmul(dqkvz, n0, "tn", [BF16], name="dn_in_proj_dw", tk=MM_DEEP)
    g_bat = _matmul(dba, n0, "tn", [BF16], name="dn_in_proj_ba_dw", tk=MM_DEEP)
    g["dn_w_in"] = [jnp.concatenate([g_qkvzt, g_bat[:2 * heads]], axis=0)]
    g["dn_w_conv"] = g_wconv[None]
    tok = emit(0, g)
    dn0a = _matmul(dba, w_bat, "nn", [F32], name="dn_in_proj_ba_dx", after=tok)
    grad_x, g_dnn = _matmul(dqkvz, w_int, "nn", [F32], name="dn_in_proj_dx", b_rows=4 * d,
                            epi=lambda acc, part, hx, dres, gain: rms_bwd_epi(acc + part, hx, dres, gain),
                            mn_extras=[dn0a, x, dh], row_extras=[dn_norm], n_rowsum=1, slab=EPI_SLAB)
    g.update(dn_norm=g_dnn, dn_out_norm=g_on,
             dn_a_log=g_alog[:, heads:2 * heads], dn_dt_bias=g_dt[:, heads:2 * heads])
    return loss, grad_x, g


def _round_up(n, m):
    return (n + m - 1) // m * m


def _pack_rows(parts, cols, row_mult):
    lead = parts[0].shape[:-1]
    flat, offs, off = [], [], 0
    for p in parts:
        n = _round_up(p.shape[-1], cols)
        flat.append(jnp.pad(p, [(0, 0)] * len(lead) + [(0, n - p.shape[-1])]))
        offs.append(off)
        off += n
    total = _round_up(off, cols * row_mult)
    if total > off:
        flat.append(jnp.zeros(lead + (total - off,), parts[0].dtype))
    return jnp.concatenate(flat, axis=-1).reshape(lead + (total // cols, cols)), offs


def _unpack(packed, offs, shapes):
    lead = packed.shape[:-2]
    flat = packed.reshape(lead + (-1,))
    out = []
    for off, shp in zip(offs, shapes):
        n = 1
        for v in shp:
            n *= v
        out.append(flat[..., off:off + n].reshape(lead + tuple(shp)))
    return out


def kernel(x, mem, dn_norm, dn_w_in, dn_w_conv, dn_a_log, dn_dt_bias, dn_out_norm, dn_w_out, cv_norm, cv_w_pw1, cv_b_pw1, cv_w_dw, cv_b_dw, cv_ln_g, cv_ln_b, cv_w_pw2, cv_b_pw2, xa_norm, xa_mem_norm, xa_w_q, xa_w_kv, xa_w_o, mlp_norm, mlp_w_up, mlp_w_down, final_norm, loss_target, m_dn_norm, m_dn_w_in, m_dn_w_conv, m_dn_a_log, m_dn_dt_bias, m_dn_out_norm, m_dn_w_out, m_cv_norm, m_cv_w_pw1, m_cv_b_pw1, m_cv_w_dw, m_cv_b_dw, m_cv_ln_g, m_cv_ln_b, m_cv_w_pw2, m_cv_b_pw2, m_xa_norm, m_xa_mem_norm, m_xa_w_q, m_xa_w_kv, m_xa_w_o, m_mlp_norm, m_mlp_w_up, m_mlp_w_down, m_final_norm, v_dn_norm, v_dn_w_in, v_dn_w_conv, v_dn_a_log, v_dn_dt_bias, v_dn_out_norm, v_dn_w_out, v_cv_norm, v_cv_w_pw1, v_cv_b_pw1, v_cv_w_dw, v_cv_b_dw, v_cv_ln_g, v_cv_ln_b, v_cv_w_pw2, v_cv_b_pw2, v_xa_norm, v_xa_mem_norm, v_xa_w_q, v_xa_w_kv, v_xa_w_o, v_mlp_norm, v_mlp_w_up, v_mlp_w_down, v_final_norm):
    wsh = dict(dn_norm=dn_norm, dn_w_in=dn_w_in, dn_w_conv=dn_w_conv, dn_a_log=dn_a_log, dn_dt_bias=dn_dt_bias, dn_out_norm=dn_out_norm, dn_w_out=dn_w_out, cv_norm=cv_norm, cv_w_pw1=cv_w_pw1, cv_b_pw1=cv_b_pw1, cv_w_dw=cv_w_dw, cv_b_dw=cv_b_dw, cv_ln_g=cv_ln_g, cv_ln_b=cv_ln_b, cv_w_pw2=cv_w_pw2, cv_b_pw2=cv_b_pw2, xa_norm=xa_norm, xa_mem_norm=xa_mem_norm, xa_w_q=xa_w_q, xa_w_kv=xa_w_kv, xa_w_o=xa_w_o, mlp_norm=mlp_norm, mlp_w_up=mlp_w_up, mlp_w_down=mlp_w_down, final_norm=final_norm)
    msh = dict(dn_norm=m_dn_norm, dn_w_in=m_dn_w_in, dn_w_conv=m_dn_w_conv, dn_a_log=m_dn_a_log, dn_dt_bias=m_dn_dt_bias, dn_out_norm=m_dn_out_norm, dn_w_out=m_dn_w_out, cv_norm=m_cv_norm, cv_w_pw1=m_cv_w_pw1, cv_b_pw1=m_cv_b_pw1, cv_w_dw=m_cv_w_dw, cv_b_dw=m_cv_b_dw, cv_ln_g=m_cv_ln_g, cv_ln_b=m_cv_ln_b, cv_w_pw2=m_cv_w_pw2, cv_b_pw2=m_cv_b_pw2, xa_norm=m_xa_norm, xa_mem_norm=m_xa_mem_norm, xa_w_q=m_xa_w_q, xa_w_kv=m_xa_w_kv, xa_w_o=m_xa_w_o, mlp_norm=m_mlp_norm, mlp_w_up=m_mlp_w_up, mlp_w_down=m_mlp_w_down, final_norm=m_final_norm)
    vsh = dict(dn_norm=v_dn_norm, dn_w_in=v_dn_w_in, dn_w_conv=v_dn_w_conv, dn_a_log=v_dn_a_log, dn_dt_bias=v_dn_dt_bias, dn_out_norm=v_dn_out_norm, dn_w_out=v_dn_w_out, cv_norm=v_cv_norm, cv_w_pw1=v_cv_w_pw1, cv_b_pw1=v_cv_b_pw1, cv_w_dw=v_cv_w_dw, cv_b_dw=v_cv_b_dw, cv_ln_g=v_cv_ln_g, cv_ln_b=v_cv_ln_b, cv_w_pw2=v_cv_w_pw2, cv_b_pw2=v_cv_b_pw2, xa_norm=v_xa_norm, xa_mem_norm=v_xa_mem_norm, xa_w_q=v_xa_w_q, xa_w_kv=v_xa_w_kv, xa_w_o=v_xa_w_o, mlp_norm=v_mlp_norm, mlp_w_up=v_mlp_w_up, mlp_w_down=v_mlp_w_down, final_norm=v_final_norm)

    big_axis = dict(BIG)
    for src in (wsh, msh, vsh):
        src["dn_w_in"] = jnp.swapaxes(src["dn_w_in"], 1, 2)
    big_axis["dn_w_in"] = 1

    small_pack, small_offs = _pack_rows([wsh[nm].reshape(-1) for nm in SMALL_SH], LANES, 8)
    w = {nm: [None] * wsh[nm].shape[0] for nm in big_axis}

    def put_weights(group, gathered):
        for (nm, layer), gth in zip(group, gathered):
            if big_axis[nm] == 1:
                w[nm][layer] = gth.reshape(N_DEV * gth.shape[1], gth.shape[2])
            else:
                w[nm][layer] = gth

    first = _all_gather([wsh[nm][layer].astype(BF16) for nm, layer in GATHER_GROUPS[0]] + [small_pack],
                        "weights_all_gather_0")
    put_weights(GATHER_GROUPS[0], first)
    me = _dev_index(*_mesh_me())
    gather_handles, tokens = {}, []
    for gi in range(1, len(GATHER_GROUPS)):
        shards = [wsh[nm][layer].astype(BF16) for nm, layer in GATHER_GROUPS[gi]]
        lands = [lax.dynamic_update_slice(lax.empty((N_DEV,) + s.shape, s.dtype), s[None], (me, 0, 0))
                 for s in shards]
        gather_handles[gi], tok = _exchange_start(shards, lands, False, f"weights_gather_{gi}",
                                                  after=[first[-1]] + tokens)
        tokens.append(tok)
    for nm, gth in zip(SMALL_SH, _unpack(first[-1], small_offs, [wsh[nm].shape for nm in SMALL_SH])):
        w[nm] = jnp.moveaxis(gth, 0, -2).reshape(gth.shape[1:-1] + (N_DEV * gth.shape[-1],))
    for nm in REPL:
        w[nm] = wsh[nm]

    def fetch(gi, after):
        put_weights(GATHER_GROUPS[gi], _exchange_wait(gather_handles[gi], after)[1])

    scatter_handles = {}

    def emit(gi, g):
        blocks = []
        for nm, layer in SCATTER_GROUPS[gi]:
            gw = g[nm][layer]
            if big_axis[nm] == 1:
                gw = gw.reshape(N_DEV, gw.shape[0] // N_DEV, gw.shape[1])
            blocks.append(gw)
        if gi == 0:
            gsmall_pack, _ = _pack_rows(
                [jnp.moveaxis(g[nm].reshape(g[nm].shape[:-1] + (N_DEV, -1)), -2, 0).reshape(N_DEV, -1)
                 for nm in SMALL_SH], LANES, 8)
            blocks.append(gsmall_pack)
        lands = [lax.empty(b.shape, b.dtype) for b in blocks]
        scatter_handles[gi], tok = _exchange_start(blocks, lands, True, f"grads_scatter_{gi}")
        return [tok]

    loss_part, grad_x, g = _local_step(x[0], mem[0], loss_target[0], w, fetch, emit, tokens)

    recv = {nm: [None] * wsh[nm].shape[0] for nm in big_axis}
    sent = {nm: [None] * wsh[nm].shape[0] for nm in big_axis}
    gsh, delta, new_m, new_v = {}, {}, {}, {}
    after = [grad_x]
    done = set()
    me_arr = me.astype(jnp.int32).reshape(1)
    def small_adamw(names, name):
        packs = []
        for src in (wsh, gsh, msh, vsh):
            pk, offs = _pack_rows([src[nm].reshape(-1) for nm in names], LANES, 8)
            packs.append(pk)
        outs = _adamw(*packs, name)
        for dst, pk in zip((delta, new_m, new_v), outs):
            for nm, val in zip(names, _unpack(pk, offs, [wsh[nm].shape for nm in names])):
                dst[nm] = val
        return outs[0]

    for gi in reversed(range(len(SCATTER_GROUPS))):
        if gi == 0:
            repl_pack, repl_offs = _pack_rows([g[nm].reshape(-1) for nm in REPL] + [loss_part[:, :1].reshape(-1)],
                                              LANES, 8)
            (repl_all,) = _all_gather([repl_pack], "repl_grads_all_gather")
            repl_red = _slot_sum(repl_all, "repl_grads_sum", SLOT_SUM_ROWS)
            *repl_vals, loss_sum = _unpack(repl_red, repl_offs, [wsh[nm].shape for nm in REPL] + [(1,)])
            for nm, val in zip(REPL, repl_vals):
                gsh[nm] = val
            after = after + [small_adamw(list(REPL), "adamw_repl")]
        sources, landed = _exchange_wait(scatter_handles[gi], after)
        for (nm, layer), src, r in zip(SCATTER_GROUPS[gi], sources, landed):
            sent[nm][layer], recv[nm][layer] = src, r
        if gi == 0:
            slot = lax.broadcasted_iota(jnp.int32, landed[-1].shape, 0)
            rsmall = jnp.where(slot == me, sources[-1], landed[-1])
        for nm in big_axis:
            if nm not in done and all(r is not None for r in recv[nm]):
                gsh[nm], delta[nm], new_m[nm], new_v[nm] = _adamw_reduce(
                    me_arr, recv[nm], sent[nm], wsh[nm], msh[nm], vsh[nm], f"adamw_{nm}")
                done.add(nm)
                after = [delta[nm]]
    gsmall_red = _slot_sum(rsmall, "grads_small_sum", SLOT_SUM_ROWS)
    for nm, val in zip(SMALL_SH, _unpack(gsmall_red, small_offs, [wsh[nm].shape for nm in SMALL_SH])):
        gsh[nm] = val
    small_adamw(list(SMALL_SH), "adamw_small")
    for dst in (gsh, delta, new_m, new_v):
        dst["dn_w_in"] = jnp.swapaxes(dst["dn_w_in"], 1, 2)
    return (loss_sum.reshape(()), grad_x[None], *[gsh[nm] for nm in WEIGHTS], *[delta[nm] for nm in WEIGHTS],
            *[new_m[nm] for nm in WEIGHTS], *[new_v[nm] for nm in WEIGHTS])
```

```python
import functools

import jax
import jax.numpy as jnp
from jax import lax
from jax.experimental import pallas as pl
from jax.experimental.pallas import tpu as pltpu

F32 = jnp.float32
BF16 = jnp.bfloat16
MESH_IDS = pl.DeviceIdType.MESH

N_DEV = 8
LANES = 128
RMS_EPS = 1e-6
LN_EPS = 1e-5
DN_HEAD_DIM = 128
DN_CONV = 4
DN_CHUNK = 64
CV_WIDTH = 31
XA_HEADS = 4
MM_TILE = 1024
MM_DEEP = 2048
EPI_SLAB = 256
SLOT_SUM_ROWS = 512
DN_HALO = 8
CV_HALO = 32

ADAM_LR = 0.001
ADAM_B1 = 0.9
ADAM_B2 = 0.999
ADAM_EPS = 1e-08
ADAM_WD = 0.01
ADAM_STEP = 10

BIG = (("dn_w_in", 2), ("dn_w_out", 1), ("cv_w_pw1", 2), ("cv_w_pw2", 1), ("xa_w_q", 1), ("xa_w_kv", 2),
       ("xa_w_o", 1), ("mlp_w_up", 2), ("mlp_w_down", 1))
_LAYER_GROUP = ("xa_w_q", "xa_w_o", "mlp_w_down", "xa_w_kv", "mlp_w_up")
GATHER_GROUPS = (
    (("dn_w_in", 0),),
    (("dn_w_out", 0),) + tuple((nm, 0) for nm in _LAYER_GROUP),
    (("cv_w_pw2", 0), ("cv_w_pw1", 0)),
    (("xa_w_q", 1), ("xa_w_o", 1), ("xa_w_kv", 1)),
    (("mlp_w_down", 1), ("mlp_w_up", 1)),
)
SCATTER_GROUPS = (
    (("dn_w_in", 0),),
    (("dn_w_out", 0),) + tuple((nm, 0) for nm in _LAYER_GROUP),
    (("cv_w_pw2", 0), ("cv_w_pw1", 0)),
    tuple((nm, 1) for nm in _LAYER_GROUP),
)
SMALL_SH = ("cv_norm", "cv_b_pw1", "cv_b_dw", "cv_ln_g", "cv_ln_b", "cv_b_pw2", "cv_w_dw", "dn_w_conv")
REPL = ("dn_norm", "dn_a_log", "dn_dt_bias", "dn_out_norm", "xa_norm", "xa_mem_norm", "mlp_norm", "final_norm")
WEIGHTS = ("dn_norm", "dn_w_in", "dn_w_conv", "dn_a_log", "dn_dt_bias", "dn_out_norm", "dn_w_out", "cv_norm",
           "cv_w_pw1", "cv_b_pw1", "cv_w_dw", "cv_b_dw", "cv_ln_g", "cv_ln_b", "cv_w_pw2", "cv_b_pw2", "xa_norm",
           "xa_mem_norm", "xa_w_q", "xa_w_kv", "xa_w_o", "mlp_norm", "mlp_w_up", "mlp_w_down", "final_norm")


def _dot_dims(mode, batched):
    o = 1 if batched else 0
    contract = {"nn": ((1 + o,), (o,)), "nt": ((1 + o,), (1 + o,)), "tn": ((o,), (o,))}[mode]
    return (contract, (((0,), (0,)) if batched else ((), ())))


def _bdot(a, b, mode):
    return lax.dot_general(a.astype(BF16), b.astype(BF16), _dot_dims(mode, a.ndim == 3),
                           preferred_element_type=F32)


@functools.partial(jax.custom_vjp, nondiff_argnums=(2,))
def _mm(a, b, mode):
    return _bdot(a, b, mode)


def _mm_fwd(a, b, mode):
    return _bdot(a, b, mode), (a, b)


def _mm_bwd(mode, res, ct):
    a, b = res
    if mode == "nn":
        da, db = _bdot(ct, b, "nt"), _bdot(a, ct, "tn")
    elif mode == "nt":
        da, db = _bdot(ct, b, "nn"), _bdot(ct, a, "tn")
    else:
        da, db = _bdot(b, ct, "nt"), _bdot(a, ct, "nn")
    return da.astype(a.dtype), db.astype(b.dtype)


_mm.defvjp(_mm_fwd, _mm_bwd)


def _sigmoid(x):
    return 0.5 * (jnp.tanh(0.5 * x) + 1.0)


def _silu(x):
    return x * _sigmoid(x)


def _softplus(x):
    return jnp.maximum(x, 0.0) + jnp.log(1.0 + jnp.exp(-jnp.abs(x)))


def _rms(x, g):
    r = lax.rsqrt(jnp.mean(x * x, axis=-1, keepdims=True) + RMS_EPS)
    return x * r * g


def _shift_rows(x, off):
    if off == 0:
        return x
    return pltpu.roll(x, x.shape[0] - off, 0)


def _series_dot(a, b, mode):
    return _bdot(a, b, mode)


def _chunk_masks(c):
    ii = lax.broadcasted_iota(jnp.int32, (c, c), 0)
    jj = lax.broadcasted_iota(jnp.int32, (c, c), 1)
    return (ii == jj).astype(F32), ii >= jj, ii > jj


def _neumann_inverse(lm):
    n = lm.shape[-1]
    t = -lm
    p = lm
    size = 2
    while size < n:
        size *= 2
        p = _series_dot(p, p, "nn")
        t = t + p + _series_dot(t, p, "nn")
    return t


def _apply_inverse(tm, rhs, mode):
    return rhs + _series_dot(tm, rhs, mode)


@jax.custom_vjp
def _unit_lower_solve(lm, rhs, tm):
    return _apply_inverse(tm, rhs, "nn")


def _uls_fwd(lm, rhs, tm):
    sol = _apply_inverse(tm, rhs, "nn")
    return sol, (tm, sol)


def _uls_bwd(res, ct):
    tm, sol = res
    d_rhs = _apply_inverse(tm, ct, "tn")
    return -_bdot(d_rhs, sol, "nt"), d_rhs, jnp.zeros_like(tm)


_unit_lower_solve.defvjp(_uls_fwd, _uls_bwd)


def _delta_chunk(q, k, v, gcol, bcol, s0, tm=None):
    c = q.shape[1]
    eye, causal, strict = _chunk_masks(c)
    grow = jnp.sum(eye * gcol, axis=1, keepdims=True)
    gc = jnp.sum(jnp.where(causal, grow, 0.0), axis=2, keepdims=True)
    gc_row = jnp.sum(eye * gc, axis=1, keepdims=True)
    decay = jnp.exp(jnp.where(causal, gc - gc_row, -jnp.inf))
    kb = k * bcol
    on_k = _mm(jnp.concatenate([kb, q], axis=1), k, "nt")
    lm = jnp.where(strict, on_k[:, :c] * decay, 0.0)
    attn = on_k[:, c:] * decay
    if tm is None:
        tm = _neumann_inverse(lax.stop_gradient(lm))
    egc = jnp.exp(gc)
    rhs = jnp.concatenate([v * bcol, kb * egc], axis=-1)
    sol = _unit_lower_solve(lm, rhs, tm)
    dv_ = v.shape[-1]
    u, w = sol[..., :dv_], sol[..., dv_:]
    gl = jnp.sum(grow, axis=2, keepdims=True)
    kd = k * jnp.exp(gl - gc)
    on_s = _mm(jnp.concatenate([w, q * egc], axis=1), s0, "nn")
    v_new = u - on_s[:, :c]
    o = on_s[:, c:] + _mm(attn, v_new, "nn")
    s1 = s0 * jnp.exp(gl) + _mm(kd, v_new, "tn")
    return o, s1, tm


def _dn_point(cv, ba, alog, dt, heads):
    a = _silu(cv)
    d = cv.shape[1] // 3
    qs, ks = [], []
    for h in range(heads):
        qh = a[:, h * DN_HEAD_DIM:(h + 1) * DN_HEAD_DIM]
        qs.append(qh * lax.rsqrt(jnp.sum(qh * qh, axis=-1, keepdims=True) + 1e-6) * (DN_HEAD_DIM ** -0.5))
        kh = a[:, d + h * DN_HEAD_DIM:d + (h + 1) * DN_HEAD_DIM]
        ks.append(kh * lax.rsqrt(jnp.sum(kh * kh, axis=-1, keepdims=True) + 1e-6))
    q = jnp.concatenate(qs, axis=-1)
    k = jnp.concatenate(ks, axis=-1)
    v = a[:, 2 * d:]
    lane = lax.broadcasted_iota(jnp.int32, ba.shape, 1)
    beta = _sigmoid(ba)
    g = -jnp.exp(alog) * _softplus(ba + dt)
    gb = jnp.where(lane < heads, beta, jnp.where(lane < 2 * heads, g, 0.0))
    return q, k, v, gb


def _attn_tile(q, k, v):
    hd = q.shape[1] // XA_HEADS
    outs = []
    for h in range(XA_HEADS):
        sl = slice(h * hd, (h + 1) * hd)
        s = _mm(q[:, sl], k[:, sl], "nt") * (hd ** -0.5)
        m = lax.stop_gradient(jnp.max(s, axis=-1, keepdims=True))
        e = jnp.exp(s - m)
        p = e / jnp.sum(e, axis=-1, keepdims=True)
        outs.append(_mm(p, v[:, sl], "nn"))
    return jnp.concatenate(outs, axis=-1)


def _ln_silu(c, g, b):
    mu = jnp.mean(c, axis=-1, keepdims=True)
    xc = c - mu
    y = xc * lax.rsqrt(jnp.mean(xc * xc, axis=-1, keepdims=True) + LN_EPS)
    return _silu(y * g + b)


def _causal_conv(xext, w, width, lead, ts):
    acc = None
    for j in range(width):
        term = _shift_rows(xext, lead + j)[:ts] * w[j:j + 1, :]
        acc = term if acc is None else acc + term
    return acc


def _colsum(x):
    return jnp.sum(x, axis=0, keepdims=True)


def _stack_rows(rows, n_rows):
    c = rows[0].shape[1]
    ridx = lax.broadcasted_iota(jnp.int32, (n_rows, c), 0)
    out = jnp.zeros((n_rows, c), F32)
    for j, r in enumerate(rows):
        out = out + jnp.where(ridx == j, r, 0.0)
    return out


def _matmul(a, b, mode, out_dtypes, *, name, epi=None, mn_extras=(), row_extras=(), out_dm=False, after=(),
            n_rowsum=0, slab=0, b_rows=None, a_pre=None, tm=MM_TILE, tn=MM_TILE, tk=MM_TILE):
    b_dm = b.ndim == 3
    b_shape = (b.shape[1], N_DEV * b.shape[2]) if b_dm else b.shape
    if b_rows is not None:
        assert not b_dm and b_rows <= b.shape[0]
        b_shape = (b_rows, b.shape[1])
    if mode == "nn":
        (m, k), (k2, n) = a.shape, b_shape
    elif mode == "nt":
        (m, k), (n, k2) = a.shape, b_shape
    else:
        (k, m), (k2, n) = a.shape, b_shape
    assert k == k2, (a.shape, b.shape, mode)
    tm, tn, tk = min(tm, m), min(tn, n), min(tk, k)
    cb, nb = 0, 1
    if b_dm:
        assert mode in ("nn", "nt")
        cb = b.shape[2]
        nb = max(1, (tn if mode == "nn" else tk) // cb)
        if mode == "nn":
            tn = nb * cb
        else:
            tk = nb * cb
    co, no = 0, 1
    if out_dm:
        co = n // N_DEV
        no = max(1, tn // co)
        tn = no * co
    assert m % tm == 0 and n % tn == 0 and k % tk == 0, (m, n, k, tm, tn, tk)
    nk = k // tk
    if mode == "tn":
        a_spec = pl.BlockSpec((tk, tm), lambda j, i, kk: (kk, i))
    else:
        a_spec = pl.BlockSpec((tm, tk), lambda j, i, kk: (i, kk))
    if b_dm:
        b_spec = (pl.BlockSpec((nb, tn, cb), lambda j, i, kk: (kk, j, 0)) if mode == "nt"
                  else pl.BlockSpec((nb, tk, cb), lambda j, i, kk: (j, kk, 0)))
    else:
        b_spec = (pl.BlockSpec((tn, tk), lambda j, i, kk: (j, kk)) if mode == "nt"
                  else pl.BlockSpec((tk, tn), lambda j, i, kk: (kk, j)))
    mn_spec = pl.BlockSpec((tm, tn), lambda j, i, kk: (i, j))
    row_spec = pl.BlockSpec((1, tn), lambda j, i, kk: (0, j))
    n_extra = len(mn_extras) + len(row_extras)
    n_out = len(out_dtypes)
    in_specs = ([a_spec, b_spec] + [mn_spec] * len(mn_extras) + [row_spec] * len(row_extras)
                + [_ANY_SPEC] * len(after))
    args = [a, b, *mn_extras, *row_extras, *after]
    if out_dm:
        out_specs = [pl.BlockSpec((no, tm, co), lambda j, i, kk: (j, i, 0))] * n_out
        out_shape = [jax.ShapeDtypeStruct((N_DEV, m, co), dt) for dt in out_dtypes]
    else:
        out_specs = [mn_spec] * n_out
        out_shape = [jax.ShapeDtypeStruct((m, n), dt) for dt in out_dtypes]
    out_specs = out_specs + [row_spec] * n_rowsum
    out_shape = out_shape + [jax.ShapeDtypeStruct((1, n), F32)] * n_rowsum
    n_in = len(args)
    n_mn = len(mn_extras)
    step = min(slab, tm) if slab else tm
    assert tm % step == 0

    def dot(a_ref, b_ref):
        a_val = a_ref[...] if a_pre is None else a_pre(a_ref[...])
        if not b_dm:
            return _bdot(a_val, b_ref[...], mode)
        if mode == "nn":
            parts = [_bdot(a_val, b_ref[dd], "nn") for dd in range(nb)]
            return parts[0] if nb == 1 else jnp.concatenate(parts, axis=1)
        out = None
        for dd in range(nb):
            part = _bdot(a_val[:, dd * cb:(dd + 1) * cb], b_ref[dd], "nt")
            out = part if out is None else out + part
        return out

    def finish(acc_src, extras, outs):
        sums = [None] * n_rowsum
        for r0 in range(0, tm, step):
            rs = slice(r0, r0 + step)
            acc_val = acc_src[rs, :]
            if epi is None:
                vals = (acc_val,)
            else:
                vals = epi(acc_val, *[e[rs, :] for e in extras[:n_mn]], *[e[...] for e in extras[n_mn:]])
            for o_ref, val in zip(outs[:n_out], vals[:n_out]):
                if out_dm:
                    for dd in range(no):
                        o_ref[dd, rs, :] = val[:, dd * co:(dd + 1) * co].astype(o_ref.dtype)
                else:
                    o_ref[rs, :] = val.astype(o_ref.dtype)
            for q in range(n_rowsum):
                sums[q] = vals[n_out + q] if sums[q] is None else sums[q] + vals[n_out + q]
        for q in range(n_rowsum):
            s_ref = outs[n_out + q]

            @pl.when(pl.program_id(1) == 0)
            def _():
                s_ref[...] = sums[q]

            @pl.when(pl.program_id(1) > 0)
            def _():
                s_ref[...] += sums[q]

    def body_one_step(*refs):
        finish(dot(refs[0], refs[1]), refs[2:2 + n_extra], refs[n_in:])

    def body(*refs):
        a_ref, b_ref = refs[0], refs[1]
        acc = refs[-1]
        kk = pl.program_id(2)

        @pl.when(kk == 0)
        def _():
            acc[...] = jnp.zeros_like(acc)

        acc[...] += dot(a_ref, b_ref)

        @pl.when(kk == nk - 1)
        def _():
            finish(acc, refs[2:2 + n_extra], refs[n_in:-1])

    res = pl.pallas_call(
        body_one_step if nk == 1 else body, name=name,
        grid=(n // tn, m // tm, nk),
        in_specs=in_specs, out_specs=out_specs, out_shape=out_shape,
        scratch_shapes=[] if nk == 1 else [pltpu.VMEM((tm, tn), F32)],
        compiler_params=pltpu.CompilerParams(
            dimension_semantics=("parallel", "arbitrary" if n_rowsum else "parallel", "arbitrary")),
    )(*args)
    return res[0] if n_out + n_rowsum == 1 else res


def _rowwise(fn, *, n_rows, ts, name, rows=(), prevs=(), nexts=(), vecs=(), row_outs=(), acc_outs=(), after=()):
    ts = min(ts, n_rows)
    assert n_rows % ts == 0
    nblk = n_rows // ts
    in_specs, args = [], []
    for arr, cb, w in rows:
        in_specs.append(pl.BlockSpec((ts, w), functools.partial(lambda i, cb: (i, cb), cb=cb)))
        args.append(arr)
    for arr, cb, w, halo in prevs:
        per = ts // halo
        in_specs.append(pl.BlockSpec(
            (halo, w), functools.partial(lambda i, cb, per: (jnp.maximum(i * per - 1, 0), cb), cb=cb, per=per)))
        args.append(arr)
    for arr, cb, w, halo in nexts:
        per = ts // halo
        last_blk = n_rows // halo - 1
        in_specs.append(pl.BlockSpec(
            (halo, w), functools.partial(lambda i, cb, per, lb: (jnp.minimum((i + 1) * per, lb), cb),
                                         cb=cb, per=per, lb=last_blk)))
        args.append(arr)
    for arr in vecs:
        in_specs.append(pl.BlockSpec(arr.shape, functools.partial(lambda i, nd: (0,) * nd, nd=arr.ndim)))
        args.append(arr)
    out_specs, out_shape = [], []
    for w, dt in row_outs:
        out_specs.append(pl.BlockSpec((ts, w), lambda i: (i, 0)))
        out_shape.append(jax.ShapeDtypeStruct((n_rows, w), dt))
    for shp in acc_outs:
        out_specs.append(pl.BlockSpec(shp, functools.partial(lambda i, nd: (0,) * nd, nd=len(shp))))
        out_shape.append(jax.ShapeDtypeStruct(shp, F32))
    n_used = len(args)
    n_tiles = n_used - len(vecs)
    in_specs += [_ANY_SPEC] * len(after)
    args += list(after)
    n_in, n_ro, n_acc = len(args), len(row_outs), len(acc_outs)

    def body(*refs):
        ins, ro, ac = refs[:n_used], refs[n_in:n_in + n_ro], refs[n_in + n_ro:]
        i = pl.program_id(0)
        rvals, avals = fn(i == 0, i == nblk - 1, *[r[...] for r in ins[:n_tiles]], *ins[n_tiles:])
        for r, val in zip(ro, rvals):
            r[...] = val.astype(r.dtype)
        if n_acc:
            @pl.when(i == 0)
            def _():
                for r in ac:
                    r[...] = jnp.zeros_like(r)

            for r, val in zip(ac, avals):
                r[...] += val

    res = pl.pallas_call(
        body, name=name, grid=(nblk,), in_specs=in_specs, out_specs=out_specs, out_shape=out_shape,
        compiler_params=pltpu.CompilerParams(dimension_semantics=("arbitrary",)),
    )(*args)
    return res


def _gated_out(o, z, onorm):
    return o * lax.rsqrt(jnp.mean(o * o, axis=-1, keepdims=True) + RMS_EPS) * onorm * _silu(z)


def _head_blocks(ref, heads, col0=0):
    return jnp.stack([ref[:, col0 + h * DN_HEAD_DIM:col0 + (h + 1) * DN_HEAD_DIM] for h in range(heads)])


def _split_heads(q_ref, k_ref, v_ref, gbv, heads):
    gcol = jnp.stack([gbv[:, heads + h:heads + h + 1] for h in range(heads)])
    bcol = jnp.stack([gbv[:, h:h + 1] for h in range(heads)])
    return _head_blocks(q_ref, heads), _head_blocks(k_ref, heads), _head_blocks(v_ref, heads), gcol, bcol


def _delta_fwd(q, k, v, gb, qkvz, onorm, heads):
    s, hd = q.shape
    n = s // DN_CHUNK
    blk = pl.BlockSpec((DN_CHUNK, hd), lambda c: (c, 0))
    gspec = pl.BlockSpec((DN_CHUNK, LANES), lambda c: (c, 0))

    def body(q_ref, k_ref, v_ref, gb_ref, z_ref, on_ref, og_ref, st_ref, tm_ref, state):
        @pl.when(pl.program_id(0) == 0)
        def _():
            state[...] = jnp.zeros_like(state)

        s0 = state[...]
        st_ref[0] = s0
        o, s1, tm = _delta_chunk(*_split_heads(q_ref, k_ref, v_ref, gb_ref[...], heads), s0)
        og = _gated_out(o, _head_blocks(z_ref, heads), on_ref[...])
        for h in range(heads):
            og_ref[:, h * DN_HEAD_DIM:(h + 1) * DN_HEAD_DIM] = og[h].astype(og_ref.dtype)
        state[...] = s1
        tm_ref[0] = tm

    return pl.pallas_call(
        body, name="dn_delta_fwd", grid=(n,),
        in_specs=[blk, blk, blk, gspec, pl.BlockSpec((DN_CHUNK, hd), lambda c: (c, 3)),
                  pl.BlockSpec(onorm.shape, lambda c: (0, 0))],
        out_specs=[blk, pl.BlockSpec((1, heads, DN_HEAD_DIM, DN_HEAD_DIM), lambda c: (c, 0, 0, 0)),
                   pl.BlockSpec((1, heads, DN_CHUNK, DN_CHUNK), lambda c: (c, 0, 0, 0))],
        out_shape=[jax.ShapeDtypeStruct((s, hd), BF16),
                   jax.ShapeDtypeStruct((n, heads, DN_HEAD_DIM, DN_HEAD_DIM), F32),
                   jax.ShapeDtypeStruct((n, heads, DN_CHUNK, DN_CHUNK), F32)],
        scratch_shapes=[pltpu.VMEM((heads, DN_HEAD_DIM, DN_HEAD_DIM), F32)],
        compiler_params=pltpu.CompilerParams(dimension_semantics=("arbitrary",)),
    )(q, k, v, gb, qkvz, onorm)


def _delta_bwd(q, k, v, gb, qkvz, onorm, states, tms, dog, heads):
    s, hd = q.shape
    n = s // DN_CHUNK
    blk = pl.BlockSpec((DN_CHUNK, hd), lambda c: (n - 1 - c, 0))
    gspec = pl.BlockSpec((DN_CHUNK, LANES), lambda c: (n - 1 - c, 0))
    sspec = pl.BlockSpec((1, heads, DN_HEAD_DIM, DN_HEAD_DIM), lambda c: (n - 1 - c, 0, 0, 0))
    tspec = pl.BlockSpec((1, heads, DN_CHUNK, DN_CHUNK), lambda c: (n - 1 - c, 0, 0, 0))
    nspec = pl.BlockSpec(onorm.shape, lambda c: (0, 0))

    def body(q_ref, k_ref, v_ref, gb_ref, z_ref, on_ref, st_ref, tm_ref, dog_ref,
             dq_ref, dk_ref, dv_ref, dgb_ref, dz_ref, don_ref, dstate):
        @pl.when(pl.program_id(0) == 0)
        def _():
            dstate[...] = jnp.zeros_like(dstate)
            don_ref[...] = jnp.zeros_like(don_ref)

        gbv = gb_ref[...]
        tm = tm_ref[0]

        def chunk(qh, kh, vh, gcol, bcol, s0, zh, on):
            o, s1, _ = _delta_chunk(qh, kh, vh, gcol, bcol, s0, tm)
            return _gated_out(o, zh, on), s1

        _, vjp = jax.vjp(chunk, *_split_heads(q_ref, k_ref, v_ref, gbv, heads), st_ref[0],
                         _head_blocks(z_ref, heads), on_ref[...])
        dq, dk, dv, dg, db, ds0, dz, don = vjp((_head_blocks(dog_ref, heads).astype(F32), dstate[...]))
        dstate[...] = ds0
        don_ref[...] += don
        lane = lax.broadcasted_iota(jnp.int32, gbv.shape, 1)
        dgb = jnp.zeros(gbv.shape, F32)
        for h in range(heads):
            sl = slice(h * DN_HEAD_DIM, (h + 1) * DN_HEAD_DIM)
            dq_ref[:, sl] = dq[h]
            dk_ref[:, sl] = dk[h]
            dv_ref[:, sl] = dv[h]
            dz_ref[:, sl] = dz[h]
            dgb = dgb + jnp.where(lane == h, db[h], 0.0) + jnp.where(lane == heads + h, dg[h], 0.0)
        dgb_ref[...] = dgb

    return pl.pallas_call(
        body, name="dn_delta_bwd", grid=(n,),
        in_specs=[blk, blk, blk, gspec, pl.BlockSpec((DN_CHUNK, hd), lambda c: (n - 1 - c, 3)), nspec,
                  sspec, tspec, blk],
        out_specs=[blk, blk, blk, gspec, blk, nspec],
        out_shape=[jax.ShapeDtypeStruct((s, hd), F32)] * 3 + [jax.ShapeDtypeStruct((s, LANES), F32),
                                                              jax.ShapeDtypeStruct((s, hd), F32),
                                                              jax.ShapeDtypeStruct(onorm.shape, F32)],
        scratch_shapes=[pltpu.VMEM((heads, DN_HEAD_DIM, DN_HEAD_DIM), F32)],
        compiler_params=pltpu.CompilerParams(dimension_semantics=("arbitrary",)),
    )(q, k, v, gb, qkvz, onorm, states, tms, dog)


def _dev_index(px, py, pc):
    return 4 * px + 2 * py + pc


def _all_gather(arrs, name):
    n = len(arrs)

    def body(*refs):
        xs, outs = refs[:n], refs[n:2 * n]
        send_sems, recv_sems, local_sems = refs[2 * n:]
        x, y, c = lax.axis_index("x"), lax.axis_index("y"), lax.axis_index("c")
        me, sibling = (x, y, c), (x, y, 1 - c)
        chips = [(1 - x, y), (x, 1 - y), (1 - x, 1 - y)]

        def copy(a, kk, block, to, src=None):
            dst = outs[a].at[_dev_index(*block)]
            return pltpu.make_async_remote_copy(
                src_ref=dst if src is None else src, dst_ref=dst,
                send_sem=send_sems.at[a * 7 + kk], recv_sem=recv_sems.at[a * 7 + kk],
                device_id=to, device_id_type=MESH_IDS)

        mine = [pltpu.make_async_copy(xs[a], outs[a].at[_dev_index(*me)], local_sems.at[a]) for a in range(n)]
        for cp in mine:
            cp.start()
        first = []
        for a in range(n):
            first.append(copy(a, 0, me, sibling, src=xs[a]))
            first += [copy(a, 1 + j, me, (*chip, c), src=xs[a]) for j, chip in enumerate(chips)]
        for cp in first:
            cp.start()
        passed = []
        for j, chip in enumerate(chips):
            for a in range(n):
                copy(a, 1 + j, (*chip, c), me).wait_recv()
                fwd = copy(a, 4 + j, (*chip, c), sibling)
                fwd.start()
                passed.append(fwd)
        for a in range(n):
            copy(a, 0, sibling, me).wait_recv()
        for j, chip in enumerate(chips):
            for a in range(n):
                copy(a, 4 + j, (*chip, 1 - c), me).wait_recv()
        for cp in first + passed:
            cp.wait_send()
        for cp in mine:
            cp.wait()

    hbm = pl.BlockSpec(memory_space=pltpu.HBM)
    res = pl.pallas_call(
        body, name=name,
        in_specs=[hbm] * n, out_specs=[hbm] * n,
        out_shape=[jax.ShapeDtypeStruct((N_DEV,) + a.shape, a.dtype) for a in arrs],
        scratch_shapes=[pltpu.SemaphoreType.DMA((7 * n,)), pltpu.SemaphoreType.DMA((7 * n,)),
                        pltpu.SemaphoreType.DMA((n,))],
    )(*arrs)
    return list(res)


_FLIPS = ((0, 0, 1), (1, 0, 0), (0, 1, 0), (1, 1, 0), (1, 0, 1), (0, 1, 1), (1, 1, 1))
_HBM_SPEC = pl.BlockSpec(memory_space=pltpu.HBM)
_SEM_SPEC = pl.BlockSpec(memory_space=pltpu.SEMAPHORE)
_ANY_SPEC = pl.BlockSpec(memory_space=pl.ANY)
_DATAFLOW = pltpu.SideEffectType.DATAFLOW_SIDE_EFFECTING
TOKEN_SHAPE = (8, LANES)


def _mesh_me():
    return lax.axis_index("x"), lax.axis_index("y"), lax.axis_index("c")


def _flipped(me, f):
    return tuple(1 - v if fl else v for v, fl in zip(me, f))


def _exchange_copies(xs, lands, send_sems, recv_sems, scatter, landed):
    me = _mesh_me()
    cps = []
    for kk, f in enumerate(_FLIPS):
        p = _flipped(me, f)
        for a in range(len(xs)):
            cps.append(pltpu.make_async_remote_copy(
                src_ref=xs[a].at[_dev_index(*p)] if scatter else xs[a],
                dst_ref=lands[a].at[_dev_index(*(p if landed else me))],
                send_sem=send_sems.at[a * 7 + kk], recv_sem=recv_sems.at[a * 7 + kk],
                device_id=p, device_id_type=MESH_IDS))
    return cps


def _exchange_start(srcs, lands, scatter, name, after=()):
    n = len(srcs)

    n_after = len(after)

    def body(*refs):
        xs, ls = refs[:n], refs[n:2 * n]
        send_sems, recv_sems = refs[2 * n + n_after], refs[2 * n + n_after + 1]
        token = refs[-1]
        for cp in _exchange_copies(xs, ls, send_sems, recv_sems, scatter, landed=False):
            cp.start()
        token[...] = jnp.zeros_like(token)

    operands = [pltpu.with_memory_space_constraint(a, pltpu.HBM) for a in list(srcs) + list(lands)]
    res = pl.pallas_call(
        body, name=name,
        in_specs=[_HBM_SPEC] * (2 * n) + [_ANY_SPEC] * len(after),
        out_specs=[_SEM_SPEC, _SEM_SPEC] + [_HBM_SPEC] * (2 * n) + [pl.BlockSpec(memory_space=pltpu.VMEM)],
        out_shape=[pltpu.SemaphoreType.DMA((7 * n,)), pltpu.SemaphoreType.DMA((7 * n,))]
        + [pltpu.HBM(a.shape, a.dtype) for a in operands] + [jax.ShapeDtypeStruct(TOKEN_SHAPE, F32)],
        input_output_aliases={i: 2 + i for i in range(2 * n)},
        compiler_params=pltpu.CompilerParams(has_side_effects=_DATAFLOW),
    )(*operands, *after)
    return (res[0], res[1], list(res[2:2 + n]), list(res[2 + n:2 + 2 * n]), scatter, name), res[-1]


def _exchange_wait(handle, after):
    send_sems, recv_sems, srcs, lands, scatter, name = handle
    n = len(srcs)
    n_after = len(after)

    def body(*refs):
        xs, ls = refs[:n], refs[n:2 * n]
        send_sems_ref, recv_sems_ref = refs[2 * n], refs[2 * n + 1]
        for cp in _exchange_copies(xs, ls, send_sems_ref, recv_sems_ref, scatter, landed=True):
            cp.wait_send()
            cp.wait_recv()

    res = pl.pallas_call(
        body, name=name + "_wait",
        in_specs=[_HBM_SPEC] * (2 * n) + [_SEM_SPEC, _SEM_SPEC] + [_ANY_SPEC] * n_after,
        out_specs=[_HBM_SPEC] * (2 * n),
        out_shape=[pltpu.HBM(a.shape, a.dtype) for a in srcs + lands],
        input_output_aliases={i: i for i in range(2 * n)},
        compiler_params=pltpu.CompilerParams(has_side_effects=_DATAFLOW),
    )(*srcs, *lands, send_sems, recv_sems, *after)
    return list(res[:n]), list(res[n:])


def _slot_sum(g, name, tr):
    _, r, c = g.shape
    tr = min(tr, r)
    assert r % tr == 0

    def body(g_ref, o_ref):
        acc = g_ref[0].astype(F32)
        for s in range(1, N_DEV):
            acc = acc + g_ref[s].astype(F32)
        o_ref[...] = acc

    return pl.pallas_call(
        body, name=name, grid=(r // tr,),
        in_specs=[pl.BlockSpec((N_DEV, tr, c), lambda i: (0, i, 0))],
        out_specs=pl.BlockSpec((tr, c), lambda i: (i, 0)),
        out_shape=jax.ShapeDtypeStruct((r, c), F32),
        compiler_params=pltpu.CompilerParams(dimension_semantics=("parallel",)),
    )(g)


def _adam_update(w, gg, m, v):
    c1 = 1.0 / (1.0 - ADAM_B1 ** ADAM_STEP)
    c2 = 1.0 / (1.0 - ADAM_B2 ** ADAM_STEP)
    nm = ADAM_B1 * m + (1.0 - ADAM_B1) * gg
    nv = ADAM_B2 * v + (1.0 - ADAM_B2) * (gg * gg)
    return -ADAM_LR * ((nm * c1) / (jnp.sqrt(nv * c2) + ADAM_EPS) + ADAM_WD * w), nm, nv


def _adamw_reduce(me, recvs, owns, w, m, v, name, tr=256):
    nl, r, c = w.shape
    assert len(recvs) == nl and len(owns) == nl
    tr = min(tr, r)
    if r % tr == 0:
        tc, nblk = c, r // tr
        at = lambda i: (i, 0)
    else:
        tr, tc = r, min(c, 4 * LANES)
        assert c % tc == 0
        nblk = c // tc
        at = lambda i: (0, i)

    def parked(li, l, i):
        return jnp.where(l < li, 0, jnp.where(l > li, nblk - 1, i))

    def recv_spec(li):
        return pl.BlockSpec((N_DEV, tr, tc), lambda l, i, me_ref: (0, *at(parked(li, l, i))))

    def own_spec(li):
        return pl.BlockSpec((None, tr, tc), lambda l, i, me_ref: (me_ref[0], *at(parked(li, l, i))))

    def body(me_ref, *refs):
        rrefs, orefs = refs[:nl], refs[nl:2 * nl]
        w_ref, m_ref, v_ref, g_ref, d_ref, nm_ref, nv_ref = refs[2 * nl:]
        l = pl.program_id(0)

        def of_layer(vals):
            out = vals[0]
            for li in range(1, nl):
                out = jnp.where(l == li, vals[li], out)
            return out

        own = of_layer([o[...].astype(F32) for o in orefs])
        gg = None
        for s in range(N_DEV):
            slot = jnp.where(me_ref[0] == s, own, of_layer([rr[s].astype(F32) for rr in rrefs]))
            gg = slot if gg is None else gg + slot
        g_ref[...] = gg
        d_ref[...], nm_ref[...], nv_ref[...] = _adam_update(w_ref[...], gg, m_ref[...], v_ref[...])

    spec = pl.BlockSpec((None, tr, tc), lambda l, i, me_ref: (l, *at(i)))
    return pl.pallas_call(
        body, name=name,
        grid_spec=pltpu.PrefetchScalarGridSpec(
            num_scalar_prefetch=1, grid=(nl, nblk),
            in_specs=[recv_spec(li) for li in range(nl)] + [own_spec(li) for li in range(nl)] + [spec] * 3,
            out_specs=[spec] * 4),
        out_shape=[jax.ShapeDtypeStruct((nl, r, c), F32)] * 4,
        compiler_params=pltpu.CompilerParams(dimension_semantics=("arbitrary", "arbitrary")),
    )(me, *recvs, *owns, w, m, v)


def _adamw(w, g, m, v, name, tr=256):
    r, c = w.shape
    tr = min(tr, r)
    assert r % tr == 0

    def body(w_ref, g_ref, m_ref, v_ref, d_ref, nm_ref, nv_ref):
        d_ref[...], nm_ref[...], nv_ref[...] = _adam_update(w_ref[...], g_ref[...], m_ref[...], v_ref[...])

    spec = pl.BlockSpec((tr, c), lambda i: (i, 0))
    return pl.pallas_call(
        body, name=name, grid=(r // tr,), in_specs=[spec] * 4, out_specs=[spec] * 3,
        out_shape=[jax.ShapeDtypeStruct((r, c), F32)] * 3,
        compiler_params=pltpu.CompilerParams(dimension_semantics=("parallel",)),
    )(w, g, m, v)


def _rms_fwd(x, g, name, ts=1024, after=()):
    s, d = x.shape

    def fn(first, last, xv, gv):
        return [_rms(xv, gv[...])], []

    return _rowwise(fn, n_rows=s, ts=ts, name=name, rows=[(x, 0, d)], vecs=[g], row_outs=[(d, BF16)],
                    after=after)[0]


def _rms_bwd(x, dn, dres, g, name, ts=256):
    s, d = x.shape

    def fn(first, last, xv, dnv, drv, gv):
        _, vjp = jax.vjp(_rms, xv, gv[...])
        dx, dg = vjp(dnv.astype(F32))
        return [drv + dx], [dg]

    return _rowwise(fn, n_rows=s, ts=ts, name=name, rows=[(x, 0, d), (dn, 0, d), (dres, 0, d)], vecs=[g],
                    row_outs=[(d, F32)], acc_outs=[(1, d)])


def _dn_pre_fwd(qkvz, ba, wconv, alog, dt, heads, ts=256):
    s = qkvz.shape[0]
    d3 = wconv.shape[1]
    d = d3 // 3

    def fn(first, last, xc, bav, xp, wv, av, dv):
        xext = jnp.concatenate([jnp.where(first, 0.0, xp), xc], axis=0)
        cv = _causal_conv(xext, wv, DN_CONV, DN_HALO - (DN_CONV - 1), xc.shape[0])
        return list(_dn_point(cv, bav, av[...], dv[...], heads)), []

    return _rowwise(fn, n_rows=s, ts=ts, name="dn_pre_fwd", rows=[(qkvz, 0, d3), (ba, 0, LANES)],
                    prevs=[(qkvz, 0, d3, DN_HALO)], vecs=[wconv, alog, dt],
                    row_outs=[(d, F32), (d, F32), (d, F32), (LANES, F32)])


def _dn_pre_bwd(qkvz, ba, wconv, alog, dt, dq, dk, dv, dgb, dz, heads, ts=256):
    s = qkvz.shape[0]
    d3 = wconv.shape[1]
    d = d3 // 3
    lead = DN_HALO - (DN_CONV - 1)

    def fn(first, last, xc, bac, dqc, dkc, dvc, dgbc, dzc, xp, xn, ban, dqn, dkn, dvn, dgbn, wv, av, dtv):
        n = xc.shape[0]
        ext = lambda cur, nxt: jnp.concatenate([cur, nxt], axis=0)
        live = lambda nxt: jnp.where(last, 0.0, nxt)
        xall = jnp.concatenate([jnp.where(first, 0.0, xp), xc, live(xn)], axis=0)
        cv = _causal_conv(xall, wv, DN_CONV, lead, n + DN_HALO)
        (_, _, _, gbv), vjp = jax.vjp(lambda c, b: _dn_point(c, b, av[...], dtv[...], heads), cv, ext(bac, ban))
        dc, dba = vjp((ext(dqc, live(dqn)), ext(dkc, live(dkn)), ext(dvc, live(dvn)), ext(dgbc, live(dgbn))))
        dx = None
        dw = []
        for j in range(DN_CONV):
            term = _shift_rows(dc, DN_CONV - 1 - j)[:n] * wv[j:j + 1, :]
            dx = term if dx is None else dx + term
            dw.append(_colsum(dc[:n] * _shift_rows(xall, lead + j)[:n]))
        dba = dba[:n]
        return ([jnp.concatenate([dx, dzc], axis=-1), dba],
                [_stack_rows(dw, DN_CONV), _colsum(dgbc * gbv[:n]), _colsum(dba)])

    return _rowwise(fn, n_rows=s, ts=ts, name="dn_pre_bwd",
                    rows=[(qkvz, 0, d3), (ba, 0, LANES), (dq, 0, d), (dk, 0, d), (dv, 0, d), (dgb, 0, LANES),
                          (dz, 0, d)],
                    prevs=[(qkvz, 0, d3, DN_HALO)],
                    nexts=[(qkvz, 0, d3, DN_HALO), (ba, 0, LANES, DN_HALO), (dq, 0, d, DN_HALO),
                           (dk, 0, d, DN_HALO), (dv, 0, d, DN_HALO), (dgb, 0, LANES, DN_HALO)],
                    vecs=[wconv, alog, dt],
                    row_outs=[(4 * d, BF16), (LANES, BF16)], acc_outs=[(DN_CONV, d3), (1, LANES), (1, LANES)])


def _cv_mid_fwd(u, wdw, bdw, lng, lnb, ts=512):
    s = u.shape[0]
    d = u.shape[1] // 2

    def fn(first, last, uc, up, wv, bv, gv, lbv):
        uext = jnp.concatenate([jnp.where(first, 0.0, up), uc], axis=0)
        glu = uext[:, :d] * _sigmoid(uext[:, d:])
        c = _causal_conv(glu, wv, CV_WIDTH, CV_HALO - (CV_WIDTH - 1), uc.shape[0]) + bv[...]
        return [c, _ln_silu(c, gv[...], lbv[...])], []

    return _rowwise(fn, n_rows=s, ts=ts, name="cv_mid_fwd", rows=[(u, 0, 2 * d)], prevs=[(u, 0, 2 * d, CV_HALO)],
                    vecs=[wdw, bdw, lng, lnb], row_outs=[(d, F32), (d, BF16)])


def _cv_mid_bwd2(dc, u, wdw, ts=512):
    s, d = dc.shape

    def fn(first, last, dcc, uc, up, dcn, wv):
        n = dcc.shape[0]
        dcext = jnp.concatenate([dcc, jnp.where(last, 0.0, dcn)], axis=0)
        uext = jnp.concatenate([jnp.where(first, 0.0, up), uc], axis=0)
        glu = uext[:, :d] * _sigmoid(uext[:, d:])
        dglu = None
        dw = []
        for j in range(CV_WIDTH):
            term = _shift_rows(dcext, CV_WIDTH - 1 - j)[:n] * wv[j:j + 1, :]
            dglu = term if dglu is None else dglu + term
            dw.append(_colsum(dcc * _shift_rows(glu, CV_HALO - (CV_WIDTH - 1) + j)[:n]))
        u1, sg = uc[:, :d], _sigmoid(uc[:, d:])
        du = jnp.concatenate([dglu * sg, dglu * u1 * sg * (1.0 - sg)], axis=-1)
        return [du], [_stack_rows(dw, CV_HALO), _colsum(du)]

    return _rowwise(fn, n_rows=s, ts=ts, name="cv_mid_bwd2", rows=[(dc, 0, d), (u, 0, 2 * d)],
                    prevs=[(u, 0, 2 * d, CV_HALO)], nexts=[(dc, 0, d, CV_HALO)], vecs=[wdw],
                    row_outs=[(2 * d, BF16)], acc_outs=[(CV_HALO, d), (1, 2 * d)])


def _attn_fwd(q, k, v, name, ts=1024):
    s, d = q.shape

    def fn(first, last, qv, kv, vv):
        return [_attn_tile(qv.astype(F32), kv[...].astype(F32), vv[...].astype(F32))], []

    return _rowwise(fn, n_rows=s, ts=ts, name=name, rows=[(q, 0, d)], vecs=[k, v], row_outs=[(d, BF16)])[0]


def _attn_bwd(q, k, v, do, name, ts=1024):
    s, d = q.shape
    m = k.shape[0]

    def fn(first, last, qv, dov, kv, vv):
        _, vjp = jax.vjp(_attn_tile, qv.astype(F32), kv[...].astype(F32), vv[...].astype(F32))
        dq, dk, dv = vjp(dov.astype(F32))
        return [dq], [dk, dv]

    return _rowwise(fn, n_rows=s, ts=ts, name=name, rows=[(q, 0, d), (do, 0, d)], vecs=[k, v],
                    row_outs=[(d, BF16)], acc_outs=[(m, d), (m, d)])


def _pad_lanes(a, off=0):
    r, n = a.shape
    return jnp.pad(a, ((0, 0), (off, LANES - off - n)))


def _local_step(x, mem, tgt, w, fetch=None, emit=None, first_after=()):
    s, d = x.shape
    heads = d // DN_HEAD_DIM
    g = {}
    if fetch is None:
        fetch = lambda group, after: None
    if emit is None:
        emit = lambda group, grads: ()

    def add_res_rms(acc, res, gain):
        h = res + acc
        return h, _rms(h, gain)

    def rms_bwd_epi(acc, hx, dres, gain):
        _, vjp = jax.vjp(_rms, hx, gain)
        dx, dg = vjp(acc)
        return dres + dx, dg

    w_int = w["dn_w_in"][0]
    assert w_int.shape[0] == 4 * d + 2 * heads
    w_bat = jnp.pad(w_int[4 * d:], ((0, LANES - 2 * heads), (0, 0)))
    dn_norm = w["dn_norm"]
    alog = _pad_lanes(w["dn_a_log"], heads)
    dtb = _pad_lanes(w["dn_dt_bias"], heads)
    wconv = w["dn_w_conv"][0]
    n0 = _rms_fwd(x, dn_norm, "dn_rms", after=first_after)
    qkvz = _matmul(n0, w_int, "nt", [F32], name="dn_in_proj", b_rows=4 * d)
    ba = _matmul(n0, w_bat, "nt", [F32], name="dn_in_proj_ba")
    q, k, v, gb = _dn_pre_fwd(qkvz, ba, wconv, alog, dtb, heads)
    og, states, tms = _delta_fwd(q, k, v, gb, qkvz, w["dn_out_norm"], heads)
    fetch(1, [og])
    h1, nq0 = _matmul(og, w["dn_w_out"][0], "nn", [F32, BF16], name="dn_out_proj", epi=add_res_rms,
                      mn_extras=[x], row_extras=[w["xa_norm"][0:1]], slab=EPI_SLAB)

    def xattn_fwd(h, nq, layer, next_gain):
        qx = _matmul(nq, w["xa_w_q"][layer], "nn", [BF16], name=f"xa{layer}_q")
        mn = _rms_fwd(mem, w["xa_mem_norm"][layer:layer + 1], f"xa{layer}_mem_rms")
        kv = _matmul(mn, w["xa_w_kv"][layer], "nn", [BF16], name=f"xa{layer}_kv")
        kx, vx = kv[:, :d], kv[:, d:]
        ox = _attn_fwd(qx, kx, vx, f"xa{layer}_attn")
        hn, nn = _matmul(ox, w["xa_w_o"][layer], "nn", [F32, BF16], name=f"xa{layer}_o", epi=add_res_rms,
                         mn_extras=[h], row_extras=[next_gain], slab=EPI_SLAB)
        return hn, nn, (h, nq, qx, mn, kx, vx, ox)

    def sq_relu(t):
        r = jnp.maximum(t.astype(F32), 0.0)
        return r * r

    def loss_epi(acc, res, target, gain):
        def cols(hh, gg):
            e = _rms(hh, gg) - target
            return _colsum(e * e) * (0.5 / d)

        per_col, vjp = jax.vjp(cols, res + acc, gain)
        dhx, dgain = vjp(jnp.ones_like(per_col))
        return dhx, dgain, per_col

    def mlp_fwd(h, nm, layer, next_gain):
        u = _matmul(nm, w["mlp_w_up"][layer], "nn", [BF16], name=f"mlp{layer}_up")
        if next_gain is None:
            hn, *nn = _matmul(u, w["mlp_w_down"][layer], "nn", [F32], name=f"mlp{layer}_down_loss",
                              epi=loss_epi, mn_extras=[h, tgt], row_extras=[w["final_norm"].reshape(1, d)],
                              n_rowsum=2, tk=MM_DEEP, slab=EPI_SLAB, a_pre=sq_relu)
        else:
            hn, nn = _matmul(u, w["mlp_w_down"][layer], "nn", [F32, BF16], name=f"mlp{layer}_down",
                             epi=add_res_rms, mn_extras=[h], row_extras=[next_gain], tk=MM_DEEP, slab=EPI_SLAB,
                             a_pre=sq_relu)
        return hn, nn, (h, nm, u)

    h2, nm0, xa0 = xattn_fwd(h1, nq0, 0, w["mlp_norm"][0:1])
    fetch(2, [h2])
    h3, n1, mlp0 = mlp_fwd(h2, nm0, 0, w["cv_norm"])

    u_cv = _matmul(n1, w["cv_w_pw1"][0], "nn", [F32], name="cv_pw1", epi=lambda acc, b: (acc + b,),
                   row_extras=[w["cv_b_pw1"]])
    wdw = jnp.pad(w["cv_w_dw"][0], ((0, CV_HALO - CV_WIDTH), (0, 0)))
    c_cv, s_cv = _cv_mid_fwd(u_cv, wdw, w["cv_b_dw"], w["cv_ln_g"], w["cv_ln_b"])
    h4, nq1 = _matmul(s_cv, w["cv_w_pw2"][0], "nn", [F32, BF16], name="cv_pw2",
                      epi=lambda acc, res, b, gain: add_res_rms(acc + b, res, gain), mn_extras=[h3],
                      row_extras=[w["cv_b_pw2"], w["xa_norm"][1:2]], slab=EPI_SLAB)
    fetch(3, [h4])
    h5, nm1, xa1 = xattn_fwd(h4, nq1, 1, w["mlp_norm"][1:2])
    fetch(4, [h5])
    dh, (g_fn, loss_cols), mlp1 = mlp_fwd(h5, nm1, 1, None)
    g["final_norm"] = g_fn.reshape(d)
    loss = jnp.sum(loss_cols, axis=1, keepdims=True)

    def mlp_bwd(dh, layer, saved, after=()):
        h, nm, u = saved
        du = _matmul(dh, w["mlp_w_down"][layer], "nt", [BF16], name=f"mlp{layer}_down_dx", after=after,
                     epi=lambda acc, uu: (acc * 2.0 * jnp.maximum(uu.astype(F32), 0.0),), mn_extras=[u])
        gdown = _matmul(u, dh, "tn", [BF16], name=f"mlp{layer}_down_dw", tm=MM_DEEP, a_pre=sq_relu)
        dhn, gn = _matmul(du, w["mlp_w_up"][layer], "nt", [F32], name=f"mlp{layer}_up_dx", epi=rms_bwd_epi,
                          mn_extras=[h, dh], row_extras=[w["mlp_norm"][layer:layer + 1]], n_rowsum=1,
                          slab=EPI_SLAB, tk=MM_DEEP)
        gup = _matmul(nm, du, "tn", [BF16], name=f"mlp{layer}_up_dw", out_dm=True, tk=MM_DEEP)
        return dhn, gup, gdown, gn

    def xattn_bwd(dh, layer, saved):
        h, nq, qx, mn, kx, vx, ox = saved
        dox = _matmul(dh, w["xa_w_o"][layer], "nt", [BF16], name=f"xa{layer}_o_dx")
        go = _matmul(ox, dh, "tn", [BF16], name=f"xa{layer}_o_dw", tk=MM_DEEP)
        dqx, dkx, dvx = _attn_bwd(qx, kx, vx, dox, f"xa{layer}_attn_bwd")

        def epi(acc, hx, dres, gain):
            dhx, dg = rms_bwd_epi(acc, hx, dres, gain)
            return dhx, dg, _colsum(dhx)

        dhn, gn, dh_cols = _matmul(dqx, w["xa_w_q"][layer], "nt", [F32], name=f"xa{layer}_q_dx", epi=epi,
                                   mn_extras=[h, dh], row_extras=[w["xa_norm"][layer:layer + 1]], n_rowsum=2,
                                   slab=EPI_SLAB)
        gq = _matmul(nq, dqx, "tn", [BF16], name=f"xa{layer}_q_dw", tk=MM_DEEP)
        dkv = jnp.concatenate([dkx, dvx], axis=-1)
        gkv = _matmul(mn, dkv, "tn", [BF16], name=f"xa{layer}_kv_dw", out_dm=True)
        dmn = _matmul(dkv, w["xa_w_kv"][layer], "nt", [F32], name=f"xa{layer}_kv_dx", tk=MM_DEEP)
        _, gmem = _rms_bwd(mem, dmn, dmn, w["xa_mem_norm"][layer:layer + 1], f"xa{layer}_mem_rms_bwd")
        return dhn, gq, gkv, go, gn, gmem, dh_cols

    dh, gup1, gdown1, gmn1 = mlp_bwd(dh, 1, mlp1)
    dh, gq1, gkv1, go1, gxn1, gmem1, g_b2 = xattn_bwd(dh, 1, xa1)
    g.update(mlp_w_up=[None, gup1], mlp_w_down=[None, gdown1], xa_w_q=[None, gq1], xa_w_kv=[None, gkv1],
             xa_w_o=[None, go1])
    tok = emit(3, g)

    def ln_bwd_epi(acc, cx, gain, bias):
        _, vjp = jax.vjp(_ln_silu, cx, gain, bias)
        dc, dg, db = vjp(acc)
        return dc, dg, db, _colsum(dc)

    dc_cv, g_lng, g_lnb, g_bdw = _matmul(dh, w["cv_w_pw2"][0], "nt", [F32], name="cv_pw2_dx", after=tok,
                                        epi=ln_bwd_epi, mn_extras=[c_cv],
                                        row_extras=[w["cv_ln_g"], w["cv_ln_b"]], n_rowsum=3, slab=EPI_SLAB)
    g["cv_w_pw2"] = [_matmul(s_cv, dh, "tn", [BF16], name="cv_pw2_dw", tk=MM_DEEP)]
    du_cv, g_wdw, g_b1 = _cv_mid_bwd2(dc_cv, u_cv, wdw)
    g["cv_w_pw1"] = [_matmul(n1, du_cv, "tn", [BF16], name="cv_pw1_dw", out_dm=True, tk=MM_DEEP)]
    dh, g_cvn = _matmul(du_cv, w["cv_w_pw1"][0], "nt", [F32], name="cv_pw1_dx", epi=rms_bwd_epi,
                        mn_extras=[h3, dh], row_extras=[w["cv_norm"]], n_rowsum=1, slab=EPI_SLAB, tk=MM_DEEP)
    g.update(cv_ln_g=g_lng, cv_ln_b=g_lnb, cv_b_dw=g_bdw, cv_b_pw2=g_b2, cv_b_pw1=g_b1, cv_norm=g_cvn,
             cv_w_dw=g_wdw[:CV_WIDTH][None])

    tok = emit(2, g)
    dh, gup0, gdown0, gmn0 = mlp_bwd(dh, 0, mlp0, after=tok)
    dh, gq0, gkv0, go0, gxn0, gmem0, _ = xattn_bwd(dh, 0, xa0)
    g["mlp_w_up"][0] = gup0
    g["mlp_w_down"][0] = gdown0
    g["mlp_norm"] = jnp.concatenate([gmn0, gmn1], axis=0)
    g["xa_w_q"][0] = gq0
    g["xa_w_kv"][0] = gkv0
    g["xa_w_o"][0] = go0
    g["xa_norm"] = jnp.concatenate([gxn0, gxn1], axis=0)
    g["xa_mem_norm"] = jnp.concatenate([gmem0, gmem1], axis=0)
    g["dn_w_out"] = [_matmul(og, dh, "tn", [BF16], name="dn_out_proj_dw", tk=MM_DEEP)]
    tok = emit(1, g)
    dog = _matmul(dh, w["dn_w_out"][0], "nt", [BF16], name="dn_out_proj_dx", after=tok)
    dq, dk, dv, dgb, dz, g_on = _delta_bwd(q, k, v, gb, qkvz, w["dn_out_norm"], states, tms, dog, heads)
    dqkvz, dba, g_wconv, g_alog, g_dt = _dn_pre_bwd(qkvz, ba, wconv, alog, dtb, dq, dk, dv, dgb, dz, heads)
    g_qkvzt = _matmul(dqkvz, n0, "tn", [BF16], name="dn_in_proj_dw", tk=MM_DEEP)
    g_bat = _matmul(dba, n0, "tn", [BF16], name="dn_in_proj_ba_dw", tk=MM_DEEP)
    g["dn_w_in"] = [jnp.concatenate([g_qkvzt, g_bat[:2 * heads]], axis=0)]
    g["dn_w_conv"] = g_wconv[None]
    tok = emit(0, g)
    dn0a = _matmul(dba, w_bat, "nn", [F32], name="dn_in_proj_ba_dx", after=tok)
    grad_x, g_dnn = _matmul(dqkvz, w_int, "nn", [F32], name="dn_in_proj_dx", b_rows=4 * d,
                            epi=lambda acc, part, hx, dres, gain: rms_bwd_epi(acc + part, hx, dres, gain),
                            mn_extras=[dn0a, x, dh], row_extras=[dn_norm], n_rowsum=1, slab=EPI_SLAB)
    g.update(dn_norm=g_dnn, dn_out_norm=g_on,
             dn_a_log=g_alog[:, heads:2 * heads], dn_dt_bias=g_dt[:, heads:2 * heads])
    return loss, grad_x, g


def _round_up(n, m):
    return (n + m - 1) // m * m


def _pack_rows(parts, cols, row_mult):
    lead = parts[0].shape[:-1]
    flat, offs, off = [], [], 0
    for p in parts:
        n = _round_up(p.shape[-1], cols)
        flat.append(jnp.pad(p, [(0, 0)] * len(lead) + [(0, n - p.shape[-1])]))
        offs.append(off)
        off += n
    total = _round_up(off, cols * row_mult)
    if total > off:
        flat.append(jnp.zeros(lead + (total - off,), parts[0].dtype))
    return jnp.concatenate(flat, axis=-1).reshape(lead + (total // cols, cols)), offs


def _unpack(packed, offs, shapes):
    lead = packed.shape[:-2]
    flat = packed.reshape(lead + (-1,))
    out = []
    for off, shp in zip(offs, shapes):
        n = 1
        for v in shp:
            n *= v
        out.append(flat[..., off:off + n].reshape(lead + tuple(shp)))
    return out


def kernel(x, mem, dn_norm, dn_w_in, dn_w_conv, dn_a_log, dn_dt_bias, dn_out_norm, dn_w_out, cv_norm, cv_w_pw1, cv_b_pw1, cv_w_dw, cv_b_dw, cv_ln_g, cv_ln_b, cv_w_pw2, cv_b_pw2, xa_norm, xa_mem_norm, xa_w_q, xa_w_kv, xa_w_o, mlp_norm, mlp_w_up, mlp_w_down, final_norm, loss_target, m_dn_norm, m_dn_w_in, m_dn_w_conv, m_dn_a_log, m_dn_dt_bias, m_dn_out_norm, m_dn_w_out, m_cv_norm, m_cv_w_pw1, m_cv_b_pw1, m_cv_w_dw, m_cv_b_dw, m_cv_ln_g, m_cv_ln_b, m_cv_w_pw2, m_cv_b_pw2, m_xa_norm, m_xa_mem_norm, m_xa_w_q, m_xa_w_kv, m_xa_w_o, m_mlp_norm, m_mlp_w_up, m_mlp_w_down, m_final_norm, v_dn_norm, v_dn_w_in, v_dn_w_conv, v_dn_a_log, v_dn_dt_bias, v_dn_out_norm, v_dn_w_out, v_cv_norm, v_cv_w_pw1, v_cv_b_pw1, v_cv_w_dw, v_cv_b_dw, v_cv_ln_g, v_cv_ln_b, v_cv_w_pw2, v_cv_b_pw2, v_xa_norm, v_xa_mem_norm, v_xa_w_q, v_xa_w_kv, v_xa_w_o, v_mlp_norm, v_mlp_w_up, v_mlp_w_down, v_final_norm):
    wsh = dict(dn_norm=dn_norm, dn_w_in=dn_w_in, dn_w_conv=dn_w_conv, dn_a_log=dn_a_log, dn_dt_bias=dn_dt_bias, dn_out_norm=dn_out_norm, dn_w_out=dn_w_out, cv_norm=cv_norm, cv_w_pw1=cv_w_pw1, cv_b_pw1=cv_b_pw1, cv_w_dw=cv_w_dw, cv_b_dw=cv_b_dw, cv_ln_g=cv_ln_g, cv_ln_b=cv_ln_b, cv_w_pw2=cv_w_pw2, cv_b_pw2=cv_b_pw2, xa_norm=xa_norm, xa_mem_norm=xa_mem_norm, xa_w_q=xa_w_q, xa_w_kv=xa_w_kv, xa_w_o=xa_w_o, mlp_norm=mlp_norm, mlp_w_up=mlp_w_up, mlp_w_down=mlp_w_down, final_norm=final_norm)
    msh = dict(dn_norm=m_dn_norm, dn_w_in=m_dn_w_in, dn_w_conv=m_dn_w_conv, dn_a_log=m_dn_a_log, dn_dt_bias=m_dn_dt_bias, dn_out_norm=m_dn_out_norm, dn_w_out=m_dn_w_out, cv_norm=m_cv_norm, cv_w_pw1=m_cv_w_pw1, cv_b_pw1=m_cv_b_pw1, cv_w_dw=m_cv_w_dw, cv_b_dw=m_cv_b_dw, cv_ln_g=m_cv_ln_g, cv_ln_b=m_cv_ln_b, cv_w_pw2=m_cv_w_pw2, cv_b_pw2=m_cv_b_pw2, xa_norm=m_xa_norm, xa_mem_norm=m_xa_mem_norm, xa_w_q=m_xa_w_q, xa_w_kv=m_xa_w_kv, xa_w_o=m_xa_w_o, mlp_norm=m_mlp_norm, mlp_w_up=m_mlp_w_up, mlp_w_down=m_mlp_w_down, final_norm=m_final_norm)
    vsh = dict(dn_norm=v_dn_norm, dn_w_in=v_dn_w_in, dn_w_conv=v_dn_w_conv, dn_a_log=v_dn_a_log, dn_dt_bias=v_dn_dt_bias, dn_out_norm=v_dn_out_norm, dn_w_out=v_dn_w_out, cv_norm=v_cv_norm, cv_w_pw1=v_cv_w_pw1, cv_b_pw1=v_cv_b_pw1, cv_w_dw=v_cv_w_dw, cv_b_dw=v_cv_b_dw, cv_ln_g=v_cv_ln_g, cv_ln_b=v_cv_ln_b, cv_w_pw2=v_cv_w_pw2, cv_b_pw2=v_cv_b_pw2, xa_norm=v_xa_norm, xa_mem_norm=v_xa_mem_norm, xa_w_q=v_xa_w_q, xa_w_kv=v_xa_w_kv, xa_w_o=v_xa_w_o, mlp_norm=v_mlp_norm, mlp_w_up=v_mlp_w_up, mlp_w_down=v_mlp_w_down, final_norm=v_final_norm)

    big_axis = dict(BIG)
    for src in (wsh, msh, vsh):
        src["dn_w_in"] = jnp.swapaxes(src["dn_w_in"], 1, 2)
    big_axis["dn_w_in"] = 1

    small_pack, small_offs = _pack_rows([wsh[nm].reshape(-1) for nm in SMALL_SH], LANES, 8)
    w = {nm: [None] * wsh[nm].shape[0] for nm in big_axis}

    def put_weights(group, gathered):
        for (nm, layer), gth in zip(group, gathered):
            if big_axis[nm] == 1:
                w[nm][layer] = gth.reshape(N_DEV * gth.shape[1], gth.shape[2])
            else:
                w[nm][layer] = gth

    first = _all_gather([wsh[nm][layer].astype(BF16) for nm, layer in GATHER_GROUPS[0]] + [small_pack],
                        "weights_all_gather_0")
    put_weights(GATHER_GROUPS[0], first)
    me = _dev_index(*_mesh_me())
    gather_handles, tokens = {}, []
    for gi in range(1, len(GATHER_GROUPS)):
        shards = [wsh[nm][layer].astype(BF16) for nm, layer in GATHER_GROUPS[gi]]
        lands = [lax.dynamic_update_slice(lax.empty((N_DEV,) + s.shape, s.dtype), s[None], (me, 0, 0))
                 for s in shards]
        gather_handles[gi], tok = _exchange_start(shards, lands, False, f"weights_gather_{gi}",
                                                  after=[first[-1]] + tokens)
        tokens.append(tok)
    for nm, gth in zip(SMALL_SH, _unpack(first[-1], small_offs, [wsh[nm].shape for nm in SMALL_SH])):
        w[nm] = jnp.moveaxis(gth, 0, -2).reshape(gth.shape[1:-1] + (N_DEV * gth.shape[-1],))
    for nm in REPL:
        w[nm] = wsh[nm]

    def fetch(gi, after):
        put_weights(GATHER_GROUPS[gi], _exchange_wait(gather_handles[gi], after)[1])

    scatter_handles = {}

    def emit(gi, g):
        blocks = []
        for nm, layer in SCATTER_GROUPS[gi]:
            gw = g[nm][layer]
            if big_axis[nm] == 1:
                gw = gw.reshape(N_DEV, gw.shape[0] // N_DEV, gw.shape[1])
            blocks.append(gw)
        if gi == 0:
            gsmall_pack, _ = _pack_rows(
                [jnp.moveaxis(g[nm].reshape(g[nm].shape[:-1] + (N_DEV, -1)), -2, 0).reshape(N_DEV, -1)
                 for nm in SMALL_SH], LANES, 8)
            blocks.append(gsmall_pack)
        lands = [lax.empty(b.shape, b.dtype) for b in blocks]
        scatter_handles[gi], tok = _exchange_start(blocks, lands, True, f"grads_scatter_{gi}")
        return [tok]

    loss_part, grad_x, g = _local_step(x[0], mem[0], loss_target[0], w, fetch, emit, tokens)

    recv = {nm: [None] * wsh[nm].shape[0] for nm in big_axis}
    sent = {nm: [None] * wsh[nm].shape[0] for nm in big_axis}
    gsh, delta, new_m, new_v = {}, {}, {}, {}
    after = [grad_x]
    done = set()
    me_arr = me.astype(jnp.int32).reshape(1)
    def small_adamw(names, name):
        packs = []
        for src in (wsh, gsh, msh, vsh):
            pk, offs = _pack_rows([src[nm].reshape(-1) for nm in names], LANES, 8)
            packs.append(pk)
        outs = _adamw(*packs, name)
        for dst, pk in zip((delta, new_m, new_v), outs):
            for nm, val in zip(names, _unpack(pk, offs, [wsh[nm].shape for nm in names])):
                dst[nm] = val
        return outs[0]

    for gi in reversed(range(len(SCATTER_GROUPS))):
        if gi == 0:
            repl_pack, repl_offs = _pack_rows([g[nm].reshape(-1) for nm in REPL] + [loss_part[:, :1].reshape(-1)],
                                              LANES, 8)
            (repl_all,) = _all_gather([repl_pack], "repl_grads_all_gather")
            repl_red = _slot_sum(repl_all, "repl_grads_sum", SLOT_SUM_ROWS)
            *repl_vals, loss_sum = _unpack(repl_red, repl_offs, [wsh[nm].shape for nm in REPL] + [(1,)])
            for nm, val in zip(REPL, repl_vals):
                gsh[nm] = val
            after = after + [small_adamw(list(REPL), "adamw_repl")]
        sources, landed = _exchange_wait(scatter_handles[gi], after)
        for (nm, layer), src, r in zip(SCATTER_GROUPS[gi], sources, landed):
            sent[nm][layer], recv[nm][layer] = src, r
        if gi == 0:
            slot = lax.broadcasted_iota(jnp.int32, landed[-1].shape, 0)
            rsmall = jnp.where(slot == me, sources[-1], landed[-1])
        for nm in big_axis:
            if nm not in done and all(r is not None for r in recv[nm]):
                gsh[nm], delta[nm], new_m[nm], new_v[nm] = _adamw_reduce(
                    me_arr, recv[nm], sent[nm], wsh[nm], msh[nm], vsh[nm], f"adamw_{nm}")
                done.add(nm)
                after = [delta[nm]]
    gsmall_red = _slot_sum(rsmall, "grads_small_sum", SLOT_SUM_ROWS)
    for nm, val in zip(SMALL_SH, _unpack(gsmall_red, small_offs, [wsh[nm].shape for nm in SMALL_SH])):
        gsh[nm] = val
    small_adamw(list(SMALL_SH), "adamw_small")
    for dst in (gsh, delta, new_m, new_v):
        dst["dn_w_in"] = jnp.swapaxes(dst["dn_w_in"], 1, 2)
    return (loss_sum.reshape(()), grad_x[None], *[gsh[nm] for nm in WEIGHTS], *[delta[nm] for nm in WEIGHTS],
            *[new_m[nm] for nm in WEIGHTS], *[new_v[nm] for nm in WEIGHTS])
```

```python
import functools

import jax
import jax.numpy as jnp
from jax import lax
from jax.experimental import pallas as pl
from jax.experimental.pallas import tpu as pltpu

F32 = jnp.float32
BF16 = jnp.bfloat16
MESH_IDS = pl.DeviceIdType.MESH

N_DEV = 8
LANES = 128
RMS_EPS = 1e-6
LN_EPS = 1e-5
DN_HEAD_DIM = 128
DN_CONV = 4
DN_CHUNK = 64
CV_WIDTH = 31
XA_HEADS = 4
MM_TILE = 1024
MM_DEEP = 2048
EPI_SLAB = 256
SLOT_SUM_ROWS = 512
DN_HALO = 8
CV_HALO = 32

ADAM_LR = 0.001
ADAM_B1 = 0.9
ADAM_B2 = 0.999
ADAM_EPS = 1e-08
ADAM_WD = 0.01
ADAM_STEP = 10

BIG = (("dn_w_in", 2), ("dn_w_out", 1), ("cv_w_pw1", 2), ("cv_w_pw2", 1), ("xa_w_q", 1), ("xa_w_kv", 2),
       ("xa_w_o", 1), ("mlp_w_up", 2), ("mlp_w_down", 1))
_LAYER_GROUP = ("xa_w_q", "xa_w_o", "mlp_w_down", "xa_w_kv", "mlp_w_up")
GATHER_GROUPS = (
    (("dn_w_in", 0),),
    (("dn_w_out", 0),) + tuple((nm, 0) for nm in _LAYER_GROUP),
    (("cv_w_pw2", 0), ("cv_w_pw1", 0)),
    (("xa_w_q", 1), ("xa_w_o", 1), ("xa_w_kv", 1)),
    (("mlp_w_down", 1), ("mlp_w_up", 1)),
)
SCATTER_GROUPS = (
    (("dn_w_in", 0),),
    (("dn_w_out", 0),) + tuple((nm, 0) for nm in _LAYER_GROUP),
    (("cv_w_pw2", 0), ("cv_w_pw1", 0)),
    tuple((nm, 1) for nm in _LAYER_GROUP),
)
SMALL_SH = ("cv_norm", "cv_b_pw1", "cv_b_dw", "cv_ln_g", "cv_ln_b", "cv_b_pw2", "cv_w_dw", "dn_w_conv")
REPL = ("dn_norm", "dn_a_log", "dn_dt_bias", "dn_out_norm", "xa_norm", "xa_mem_norm", "mlp_norm", "final_norm")
WEIGHTS = ("dn_norm", "dn_w_in", "dn_w_conv", "dn_a_log", "dn_dt_bias", "dn_out_norm", "dn_w_out", "cv_norm",
           "cv_w_pw1", "cv_b_pw1", "cv_w_dw", "cv_b_dw", "cv_ln_g", "cv_ln_b", "cv_w_pw2", "cv_b_pw2", "xa_norm",
           "xa_mem_norm", "xa_w_q", "xa_w_kv", "xa_w_o", "mlp_norm", "mlp_w_up", "mlp_w_down", "final_norm")


def _dot_dims(mode, batched):
    o = 1 if batched else 0
    contract = {"nn": ((1 + o,), (o,)), "nt": ((1 + o,), (1 + o,)), "tn": ((o,), (o,))}[mode]
    return (contract, (((0,), (0,)) if batched else ((), ())))


def _bdot(a, b, mode):
    return lax.dot_general(a.astype(BF16), b.astype(BF16), _dot_dims(mode, a.ndim == 3),
                           preferred_element_type=F32)


@functools.partial(jax.custom_vjp, nondiff_argnums=(2,))
def _mm(a, b, mode):
    return _bdot(a, b, mode)


def _mm_fwd(a, b, mode):
    return _bdot(a, b, mode), (a, b)


def _mm_bwd(mode, res, ct):
    a, b = res
    if mode == "nn":
        da, db = _bdot(ct, b, "nt"), _bdot(a, ct, "tn")
    elif mode == "nt":
        da, db = _bdot(ct, b, "nn"), _bdot(ct, a, "tn")
    else:
        da, db = _bdot(b, ct, "nt"), _bdot(a, ct, "nn")
    return da.astype(a.dtype), db.astype(b.dtype)


_mm.defvjp(_mm_fwd, _mm_bwd)


def _sigmoid(x):
    return 0.5 * (jnp.tanh(0.5 * x) + 1.0)


def _silu(x):
    return x * _sigmoid(x)


def _softplus(x):
    return jnp.maximum(x, 0.0) + jnp.log(1.0 + jnp.exp(-jnp.abs(x)))


def _rms(x, g):
    r = lax.rsqrt(jnp.mean(x * x, axis=-1, keepdims=True) + RMS_EPS)
    return x * r * g


def _shift_rows(x, off):
    if off == 0:
        return x
    return pltpu.roll(x, x.shape[0] - off, 0)


def _series_dot(a, b, mode):
    return _bdot(a, b, mode)


def _chunk_masks(c):
    ii = lax.broadcasted_iota(jnp.int32, (c, c), 0)
    jj = lax.broadcasted_iota(jnp.int32, (c, c), 1)
    return (ii == jj).astype(F32), ii >= jj, ii > jj


def _neumann_inverse(lm):
    n = lm.shape[-1]
    t = -lm
    p = lm
    size = 2
    while size < n:
        size *= 2
        p = _series_dot(p, p, "nn")
        t = t + p + _series_dot(t, p, "nn")
    return t


def _apply_inverse(tm, rhs, mode):
    return rhs + _series_dot(tm, rhs, mode)


@jax.custom_vjp
def _unit_lower_solve(lm, rhs, tm):
    return _apply_inverse(tm, rhs, "nn")


def _uls_fwd(lm, rhs, tm):
    sol = _apply_inverse(tm, rhs, "nn")
    return sol, (tm, sol)


def _uls_bwd(res, ct):
    tm, sol = res
    d_rhs = _apply_inverse(tm, ct, "tn")
    return -_bdot(d_rhs, sol, "nt"), d_rhs, jnp.zeros_like(tm)


_unit_lower_solve.defvjp(_uls_fwd, _uls_bwd)


def _delta_chunk(q, k, v, gcol, bcol, s0, tm=None):
    c = q.shape[1]
    eye, causal, strict = _chunk_masks(c)
    grow = jnp.sum(eye * gcol, axis=1, keepdims=True)
    gc = jnp.sum(jnp.where(causal, grow, 0.0), axis=2, keepdims=True)
    gc_row = jnp.sum(eye * gc, axis=1, keepdims=True)
    decay = jnp.exp(jnp.where(causal, gc - gc_row, -jnp.inf))
    kb = k * bcol
    on_k = _mm(jnp.concatenate([kb, q], axis=1), k, "nt")
    lm = jnp.where(strict, on_k[:, :c] * decay, 0.0)
    attn = on_k[:, c:] * decay
    if tm is None:
        tm = _neumann_inverse(lax.stop_gradient(lm))
    egc = jnp.exp(gc)
    rhs = jnp.concatenate([v * bcol, kb * egc], axis=-1)
    sol = _unit_lower_solve(lm, rhs, tm)
    dv_ = v.shape[-1]
    u, w = sol[..., :dv_], sol[..., dv_:]
    gl = jnp.sum(grow, axis=2, keepdims=True)
    kd = k * jnp.exp(gl - gc)
    on_s = _mm(jnp.concatenate([w, q * egc], axis=1), s0, "nn")
    v_new = u - on_s[:, :c]
    o = on_s[:, c:] + _mm(attn, v_new, "nn")
    s1 = s0 * jnp.exp(gl) + _mm(kd, v_new, "tn")
    return o, s1, tm


def _dn_point(cv, ba, alog, dt, heads):
    a = _silu(cv)
    d = cv.shape[1] // 3
    qs, ks = [], []
    for h in range(heads):
        qh = a[:, h * DN_HEAD_DIM:(h + 1) * DN_HEAD_DIM]
        qs.append(qh * lax.rsqrt(jnp.sum(qh * qh, axis=-1, keepdims=True) + 1e-6) * (DN_HEAD_DIM ** -0.5))
        kh = a[:, d + h * DN_HEAD_DIM:d + (h + 1) * DN_HEAD_DIM]
        ks.append(kh * lax.rsqrt(jnp.sum(kh * kh, axis=-1, keepdims=True) + 1e-6))
    q = jnp.concatenate(qs, axis=-1)
    k = jnp.concatenate(ks, axis=-1)
    v = a[:, 2 * d:]
    lane = lax.broadcasted_iota(jnp.int32, ba.shape, 1)
    beta = _sigmoid(ba)
    g = -jnp.exp(alog) * _softplus(ba + dt)
    gb = jnp.where(lane < heads, beta, jnp.where(lane < 2 * heads, g, 0.0))
    return q, k, v, gb


def _attn_tile(q, k, v):
    hd = q.shape[1] // XA_HEADS
    outs = []
    for h in range(XA_HEADS):
        sl = slice(h * hd, (h + 1) * hd)
        s = _mm(q[:, sl], k[:, sl], "nt") * (hd ** -0.5)
        m = lax.stop_gradient(jnp.max(s, axis=-1, keepdims=True))
        e = jnp.exp(s - m)
        p = e / jnp.sum(e, axis=-1, keepdims=True)
        outs.append(_mm(p, v[:, sl], "nn"))
    return jnp.concatenate(outs, axis=-1)


def _ln_silu(c, g, b):
    mu = jnp.mean(c, axis=-1, keepdims=True)
    xc = c - mu
    y = xc * lax.rsqrt(jnp.mean(xc * xc, axis=-1, keepdims=True) + LN_EPS)
    return _silu(y * g + b)


def _causal_conv(xext, w, width, lead, ts):
    acc = None
    for j in range(width):
        term = _shift_rows(xext, lead + j)[:ts] * w[j:j + 1, :]
        acc = term if acc is None else acc + term
    return acc


def _colsum(x):
    return jnp.sum(x, axis=0, keepdims=True)


def _stack_rows(rows, n_rows):
    c = rows[0].shape[1]
    ridx = lax.broadcasted_iota(jnp.int32, (n_rows, c), 0)
    out = jnp.zeros((n_rows, c), F32)
    for j, r in enumerate(rows):
        out = out + jnp.where(ridx == j, r, 0.0)
    return out


def _matmul(a, b, mode, out_dtypes, *, name, epi=None, mn_extras=(), row_extras=(), out_dm=False, after=(),
            n_rowsum=0, slab=0, b_rows=None, a_pre=None, tm=MM_TILE, tn=MM_TILE, tk=MM_TILE):
    b_dm = b.ndim == 3
    b_shape = (b.shape[1], N_DEV * b.shape[2]) if b_dm else b.shape
    if b_rows is not None:
        assert not b_dm and b_rows <= b.shape[0]
        b_shape = (b_rows, b.shape[1])
    if mode == "nn":
        (m, k), (k2, n) = a.shape, b_shape
    elif mode == "nt":
        (m, k), (n, k2) = a.shape, b_shape
    else:
        (k, m), (k2, n) = a.shape, b_shape
    assert k == k2, (a.shape, b.shape, mode)
    tm, tn, tk = min(tm, m), min(tn, n), min(tk, k)
    cb, nb = 0, 1
    if b_dm:
        assert mode in ("nn", "nt")
        cb = b.shape[2]
        nb = max(1, (tn if mode == "nn" else tk) // cb)
        if mode == "nn":
            tn = nb * cb
        else:
            tk = nb * cb
    co, no = 0, 1
    if out_dm:
        co = n // N_DEV
        no = max(1, tn // co)
        tn = no * co
    assert m % tm == 0 and n % tn == 0 and k % tk == 0, (m, n, k, tm, tn, tk)
    nk = k // tk
    if mode == "tn":
        a_spec = pl.BlockSpec((tk, tm), lambda j, i, kk: (kk, i))
    else:
        a_spec = pl.BlockSpec((tm, tk), lambda j, i, kk: (i, kk))
    if b_dm:
        b_spec = (pl.BlockSpec((nb, tn, cb), lambda j, i, kk: (kk, j, 0)) if mode == "nt"
                  else pl.BlockSpec((nb, tk, cb), lambda j, i, kk: (j, kk, 0)))
    else:
        b_spec = (pl.BlockSpec((tn, tk), lambda j, i, kk: (j, kk)) if mode == "nt"
                  else pl.BlockSpec((tk, tn), lambda j, i, kk: (kk, j)))
    mn_spec = pl.BlockSpec((tm, tn), lambda j, i, kk: (i, j))
    row_spec = pl.BlockSpec((1, tn), lambda j, i, kk: (0, j))
    n_extra = len(mn_extras) + len(row_extras)
    n_out = len(out_dtypes)
    in_specs = ([a_spec, b_spec] + [mn_spec] * len(mn_extras) + [row_spec] * len(row_extras)
                + [_ANY_SPEC] * len(after))
    args = [a, b, *mn_extras, *row_extras, *after]
    if out_dm:
        out_specs = [pl.BlockSpec((no, tm, co), lambda j, i, kk: (j, i, 0))] * n_out
        out_shape = [jax.ShapeDtypeStruct((N_DEV, m, co), dt) for dt in out_dtypes]
    else:
        out_specs = [mn_spec] * n_out
        out_shape = [jax.ShapeDtypeStruct((m, n), dt) for dt in out_dtypes]
    out_specs = out_specs + [row_spec] * n_rowsum
    out_shape = out_shape + [jax.ShapeDtypeStruct((1, n), F32)] * n_rowsum
    n_in = len(args)
    n_mn = len(mn_extras)
    step = min(slab, tm) if slab else tm
    assert tm % step == 0

    def dot(a_ref, b_ref):
        a_val = a_ref[...] if a_pre is None else a_pre(a_ref[...])
        if not b_dm:
            return _bdot(a_val, b_ref[...], mode)
        if mode == "nn":
            parts = [_bdot(a_val, b_ref[dd], "nn") for dd in range(nb)]
            return parts[0] if nb == 1 else jnp.concatenate(parts, axis=1)
        out = None
        for dd in range(nb):
            part = _bdot(a_val[:, dd * cb:(dd + 1) * cb], b_ref[dd], "nt")
            out = part if out is None else out + part
        return out

    def finish(acc_src, extras, outs):
        sums = [None] * n_rowsum
        for r0 in range(0, tm, step):
            rs = slice(r0, r0 + step)
            acc_val = acc_src[rs, :]
            if epi is None:
                vals = (acc_val,)
            else:
                vals = epi(acc_val, *[e[rs, :] for e in extras[:n_mn]], *[e[...] for e in extras[n_mn:]])
            for o_ref, val in zip(outs[:n_out], vals[:n_out]):
                if out_dm:
                    for dd in range(no):
                        o_ref[dd, rs, :] = val[:, dd * co:(dd + 1) * co].astype(o_ref.dtype)
                else:
                    o_ref[rs, :] = val.astype(o_ref.dtype)
            for q in range(n_rowsum):
                sums[q] = vals[n_out + q] if sums[q] is None else sums[q] + vals[n_out + q]
        for q in range(n_rowsum):
            s_ref = outs[n_out + q]

            @pl.when(pl.program_id(1) == 0)
            def _():
                s_ref[...] = sums[q]

            @pl.when(pl.program_id(1) > 0)
            def _():
                s_ref[...] += sums[q]

    def body_one_step(*refs):
        finish(dot(refs[0], refs[1]), refs[2:2 + n_extra], refs[n_in:])

    def body(*refs):
        a_ref, b_ref = refs[0], refs[1]
        acc = refs[-1]
        kk = pl.program_id(2)

        @pl.when(kk == 0)
        def _():
            acc[...] = jnp.zeros_like(acc)

        acc[...] += dot(a_ref, b_ref)

        @pl.when(kk == nk - 1)
        def _():
            finish(acc, refs[2:2 + n_extra], refs[n_in:-1])

    res = pl.pallas_call(
        body_one_step if nk == 1 else body, name=name,
        grid=(n // tn, m // tm, nk),
        in_specs=in_specs, out_specs=out_specs, out_shape=out_shape,
        scratch_shapes=[] if nk == 1 else [pltpu.VMEM((tm, tn), F32)],
        compiler_params=pltpu.CompilerParams(
            dimension_semantics=("parallel", "arbitrary" if n_rowsum else "parallel", "arbitrary")),
    )(*args)
    return res[0] if n_out + n_rowsum == 1 else res


def _rowwise(fn, *, n_rows, ts, name, rows=(), prevs=(), nexts=(), vecs=(), row_outs=(), acc_outs=(), after=()):
    ts = min(ts, n_rows)
    assert n_rows % ts == 0
    nblk = n_rows // ts
    in_specs, args = [], []
    for arr, cb, w in rows:
        in_specs.append(pl.BlockSpec((ts, w), functools.partial(lambda i, cb: (i, cb), cb=cb)))
        args.append(arr)
    for arr, cb, w, halo in prevs:
        per = ts // halo
        in_specs.append(pl.BlockSpec(
            (halo, w), functools.partial(lambda i, cb, per: (jnp.maximum(i * per - 1, 0), cb), cb=cb, per=per)))
        args.append(arr)
    for arr, cb, w, halo in nexts:
        per = ts // halo
        last_blk = n_rows // halo - 1
        in_specs.append(pl.BlockSpec(
            (halo, w), functools.partial(lambda i, cb, per, lb: (jnp.minimum((i + 1) * per, lb), cb),
                                         cb=cb, per=per, lb=last_blk)))
        args.append(arr)
    for arr in vecs:
        in_specs.append(pl.BlockSpec(arr.shape, functools.partial(lambda i, nd: (0,) * nd, nd=arr.ndim)))
        args.append(arr)
    out_specs, out_shape = [], []
    for w, dt in row_outs:
        out_specs.append(pl.BlockSpec((ts, w), lambda i: (i, 0)))
        out_shape.append(jax.ShapeDtypeStruct((n_rows, w), dt))
    for shp in acc_outs:
        out_specs.append(pl.BlockSpec(shp, functools.partial(lambda i, nd: (0,) * nd, nd=len(shp))))
        out_shape.append(jax.ShapeDtypeStruct(shp, F32))
    n_used = len(args)
    n_tiles = n_used - len(vecs)
    in_specs += [_ANY_SPEC] * len(after)
    args += list(after)
    n_in, n_ro, n_acc = len(args), len(row_outs), len(acc_outs)

    def body(*refs):
        ins, ro, ac = refs[:n_used], refs[n_in:n_in + n_ro], refs[n_in + n_ro:]
        i = pl.program_id(0)
        rvals, avals = fn(i == 0, i == nblk - 1, *[r[...] for r in ins[:n_tiles]], *ins[n_tiles:])
        for r, val in zip(ro, rvals):
            r[...] = val.astype(r.dtype)
        if n_acc:
            @pl.when(i == 0)
            def _():
                for r in ac:
                    r[...] = jnp.zeros_like(r)

            for r, val in zip(ac, avals):
                r[...] += val

    res = pl.pallas_call(
        body, name=name, grid=(nblk,), in_specs=in_specs, out_specs=out_specs, out_shape=out_shape,
        compiler_params=pltpu.CompilerParams(dimension_semantics=("arbitrary",)),
    )(*args)
    return res


def _gated_out(o, z, onorm):
    return o * lax.rsqrt(jnp.mean(o * o, axis=-1, keepdims=True) + RMS_EPS) * onorm * _silu(z)


def _head_blocks(ref, heads, col0=0):
    return jnp.stack([ref[:, col0 + h * DN_HEAD_DIM:col0 + (h + 1) * DN_HEAD_DIM] for h in range(heads)])


def _split_heads(q_ref, k_ref, v_ref, gbv, heads):
    gcol = jnp.stack([gbv[:, heads + h:heads + h + 1] for h in range(heads)])
    bcol = jnp.stack([gbv[:, h:h + 1] for h in range(heads)])
    return _head_blocks(q_ref, heads), _head_blocks(k_ref, heads), _head_blocks(v_ref, heads), gcol, bcol


def _delta_fwd(q, k, v, gb, qkvz, onorm, heads):
    s, hd = q.shape
    n = s // DN_CHUNK
    blk = pl.BlockSpec((DN_CHUNK, hd), lambda c: (c, 0))
    gspec = pl.BlockSpec((DN_CHUNK, LANES), lambda c: (c, 0))

    def body(q_ref, k_ref, v_ref, gb_ref, z_ref, on_ref, og_ref, st_ref, tm_ref, state):
        @pl.when(pl.program_id(0) == 0)
        def _():
            state[...] = jnp.zeros_like(state)

        s0 = state[...]
        st_ref[0] = s0
        o, s1, tm = _delta_chunk(*_split_heads(q_ref, k_ref, v_ref, gb_ref[...], heads), s0)
        og = _gated_out(o, _head_blocks(z_ref, heads), on_ref[...])
        for h in range(heads):
            og_ref[:, h * DN_HEAD_DIM:(h + 1) * DN_HEAD_DIM] = og[h].astype(og_ref.dtype)
        state[...] = s1
        tm_ref[0] = tm

    return pl.pallas_call(
        body, name="dn_delta_fwd", grid=(n,),
        in_specs=[blk, blk, blk, gspec, pl.BlockSpec((DN_CHUNK, hd), lambda c: (c, 3)),
                  pl.BlockSpec(onorm.shape, lambda c: (0, 0))],
        out_specs=[blk, pl.BlockSpec((1, heads, DN_HEAD_DIM, DN_HEAD_DIM), lambda c: (c, 0, 0, 0)),
                   pl.BlockSpec((1, heads, DN_CHUNK, DN_CHUNK), lambda c: (c, 0, 0, 0))],
        out_shape=[jax.ShapeDtypeStruct((s, hd), BF16),
                   jax.ShapeDtypeStruct((n, heads, DN_HEAD_DIM, DN_HEAD_DIM), F32),
                   jax.ShapeDtypeStruct((n, heads, DN_CHUNK, DN_CHUNK), F32)],
        scratch_shapes=[pltpu.VMEM((heads, DN_HEAD_DIM, DN_HEAD_DIM), F32)],
        compiler_params=pltpu.CompilerParams(dimension_semantics=("arbitrary",)),
    )(q, k, v, gb, qkvz, onorm)


def _delta_bwd(q, k, v, gb, qkvz, onorm, states, tms, dog, heads):
    s, hd = q.shape
    n = s // DN_CHUNK
    blk = pl.BlockSpec((DN_CHUNK, hd), lambda c: (n - 1 - c, 0))
    gspec = pl.BlockSpec((DN_CHUNK, LANES), lambda c: (n - 1 - c, 0))
    sspec = pl.BlockSpec((1, heads, DN_HEAD_DIM, DN_HEAD_DIM), lambda c: (n - 1 - c, 0, 0, 0))
    tspec = pl.BlockSpec((1, heads, DN_CHUNK, DN_CHUNK), lambda c: (n - 1 - c, 0, 0, 0))
    nspec = pl.BlockSpec(onorm.shape, lambda c: (0, 0))

    def body(q_ref, k_ref, v_ref, gb_ref, z_ref, on_ref, st_ref, tm_ref, dog_ref,
             dq_ref, dk_ref, dv_ref, dgb_ref, dz_ref, don_ref, dstate):
        @pl.when(pl.program_id(0) == 0)
        def _():
            dstate[...] = jnp.zeros_like(dstate)
            don_ref[...] = jnp.zeros_like(don_ref)

        gbv = gb_ref[...]
        tm = tm_ref[0]

        def chunk(qh, kh, vh, gcol, bcol, s0, zh, on):
            o, s1, _ = _delta_chunk(qh, kh, vh, gcol, bcol, s0, tm)
            return _gated_out(o, zh, on), s1

        _, vjp = jax.vjp(chunk, *_split_heads(q_ref, k_ref, v_ref, gbv, heads), st_ref[0],
                         _head_blocks(z_ref, heads), on_ref[...])
        dq, dk, dv, dg, db, ds0, dz, don = vjp((_head_blocks(dog_ref, heads).astype(F32), dstate[...]))
        dstate[...] = ds0
        don_ref[...] += don
        lane = lax.broadcasted_iota(jnp.int32, gbv.shape, 1)
        dgb = jnp.zeros(gbv.shape, F32)
        for h in range(heads):
            sl = slice(h * DN_HEAD_DIM, (h + 1) * DN_HEAD_DIM)
            dq_ref[:, sl] = dq[h]
            dk_ref[:, sl] = dk[h]
            dv_ref[:, sl] = dv[h]
            dz_ref[:, sl] = dz[h]
            dgb = dgb + jnp.where(lane == h, db[h], 0.0) + jnp.where(lane == heads + h, dg[h], 0.0)
        dgb_ref[...] = dgb

    return pl.pallas_call(
        body, name="dn_delta_bwd", grid=(n,),
        in_specs=[blk, blk, blk, gspec, pl.BlockSpec((DN_CHUNK, hd), lambda c: (n - 1 - c, 3)), nspec,
                  sspec, tspec, blk],
        out_specs=[blk, blk, blk, gspec, blk, nspec],
        out_shape=[jax.ShapeDtypeStruct((s, hd), F32)] * 3 + [jax.ShapeDtypeStruct((s, LANES), F32),
                                                              jax.ShapeDtypeStruct((s, hd), F32),
                                                              jax.ShapeDtypeStruct(onorm.shape, F32)],
        scratch_shapes=[pltpu.VMEM((heads, DN_HEAD_DIM, DN_HEAD_DIM), F32)],
        compiler_params=pltpu.CompilerParams(dimension_semantics=("arbitrary",)),
    )(q, k, v, gb, qkvz, onorm, states, tms, dog)


def _dev_index(px, py, pc):
    return 4 * px + 2 * py + pc


def _all_gather(arrs, name):
    n = len(arrs)

    def body(*refs):
        xs, outs = refs[:n], refs[n:2 * n]
        send_sems, recv_sems, local_sems = refs[2 * n:]
        x, y, c = lax.axis_index("x"), lax.axis_index("y"), lax.axis_index("c")
        me, sibling = (x, y, c), (x, y, 1 - c)
        chips = [(1 - x, y), (x, 1 - y), (1 - x, 1 - y)]

        def copy(a, kk, block, to, src=None):
            dst = outs[a].at[_dev_index(*block)]
            return pltpu.make_async_remote_copy(
                src_ref=dst if src is None else src, dst_ref=dst,
                send_sem=send_sems.at[a * 7 + kk], recv_sem=recv_sems.at[a * 7 + kk],
                device_id=to, device_id_type=MESH_IDS)

        mine = [pltpu.make_async_copy(xs[a], outs[a].at[_dev_index(*me)], local_sems.at[a]) for a in range(n)]
        for cp in mine:
            cp.start()
        first = []
        for a in range(n):
            first.append(copy(a, 0, me, sibling, src=xs[a]))
            first += [copy(a, 1 + j, me, (*chip, c), src=xs[a]) for j, chip in enumerate(chips)]
        for cp in first:
            cp.start()
        passed = []
        for j, chip in enumerate(chips):
            for a in range(n):
                copy(a, 1 + j, (*chip, c), me).wait_recv()
                fwd = copy(a, 4 + j, (*chip, c), sibling)
                fwd.start()
                passed.append(fwd)
        for a in range(n):
            copy(a, 0, sibling, me).wait_recv()
        for j, chip in enumerate(chips):
            for a in range(n):
                copy(a, 4 + j, (*chip, 1 - c), me).wait_recv()
        for cp in first + passed:
            cp.wait_send()
        for cp in mine:
            cp.wait()

    hbm = pl.BlockSpec(memory_space=pltpu.HBM)
    res = pl.pallas_call(
        body, name=name,
        in_specs=[hbm] * n, out_specs=[hbm] * n,
        out_shape=[jax.ShapeDtypeStruct((N_DEV,) + a.shape, a.dtype) for a in arrs],
        scratch_shapes=[pltpu.SemaphoreType.DMA((7 * n,)), pltpu.SemaphoreType.DMA((7 * n,)),
                        pltpu.SemaphoreType.DMA((n,))],
    )(*arrs)
    return list(res)


_FLIPS = ((0, 0, 1), (1, 0, 0), (0, 1, 0), (1, 1, 0), (1, 0, 1), (0, 1, 1), (1, 1, 1))
_HBM_SPEC = pl.BlockSpec(memory_space=pltpu.HBM)
_SEM_SPEC = pl.BlockSpec(memory_space=pltpu.SEMAPHORE)
_ANY_SPEC = pl.BlockSpec(memory_space=pl.ANY)
_DATAFLOW = pltpu.SideEffectType.DATAFLOW_SIDE_EFFECTING
TOKEN_SHAPE = (8, LANES)


def _mesh_me():
    return lax.axis_index("x"), lax.axis_index("y"), lax.axis_index("c")


def _flipped(me, f):
    return tuple(1 - v if fl else v for v, fl in zip(me, f))


def _exchange_copies(xs, lands, send_sems, recv_sems, scatter, landed):
    me = _mesh_me()
    cps = []
    for kk, f in enumerate(_FLIPS):
        p = _flipped(me, f)
        for a in range(len(xs)):
            cps.append(pltpu.make_async_remote_copy(
                src_ref=xs[a].at[_dev_index(*p)] if scatter else xs[a],
                dst_ref=lands[a].at[_dev_index(*(p if landed else me))],
                send_sem=send_sems.at[a * 7 + kk], recv_sem=recv_sems.at[a * 7 + kk],
                device_id=p, device_id_type=MESH_IDS))
    return cps


def _exchange_start(srcs, lands, scatter, name, after=()):
    n = len(srcs)

    n_after = len(after)

    def body(*refs):
        xs, ls = refs[:n], refs[n:2 * n]
        send_sems, recv_sems = refs[2 * n + n_after], refs[2 * n + n_after + 1]
        token = refs[-1]
        for cp in _exchange_copies(xs, ls, send_sems, recv_sems, scatter, landed=False):
            cp.start()
        token[...] = jnp.zeros_like(token)

    operands = [pltpu.with_memory_space_constraint(a, pltpu.HBM) for a in list(srcs) + list(lands)]
    res = pl.pallas_call(
        body, name=name,
        in_specs=[_HBM_SPEC] * (2 * n) + [_ANY_SPEC] * len(after),
        out_specs=[_SEM_SPEC, _SEM_SPEC] + [_HBM_SPEC] * (2 * n) + [pl.BlockSpec(memory_space=pltpu.VMEM)],
        out_shape=[pltpu.SemaphoreType.DMA((7 * n,)), pltpu.SemaphoreType.DMA((7 * n,))]
        + [pltpu.HBM(a.shape, a.dtype) for a in operands] + [jax.ShapeDtypeStruct(TOKEN_SHAPE, F32)],
        input_output_aliases={i: 2 + i for i in range(2 * n)},
        compiler_params=pltpu.CompilerParams(has_side_effects=_DATAFLOW),
    )(*operands, *after)
    return (res[0], res[1], list(res[2:2 + n]), list(res[2 + n:2 + 2 * n]), scatter, name), res[-1]


def _exchange_wait(handle, after):
    send_sems, recv_sems, srcs, lands, scatter, name = handle
    n = len(srcs)
    n_after = len(after)

    def body(*refs):
        xs, ls = refs[:n], refs[n:2 * n]
        send_sems_ref, recv_sems_ref = refs[2 * n], refs[2 * n + 1]
        for cp in _exchange_copies(xs, ls, send_sems_ref, recv_sems_ref, scatter, landed=True):
            cp.wait_send()
            cp.wait_recv()

    res = pl.pallas_call(
        body, name=name + "_wait",
        in_specs=[_HBM_SPEC] * (2 * n) + [_SEM_SPEC, _SEM_SPEC] + [_ANY_SPEC] * n_after,
        out_specs=[_HBM_SPEC] * (2 * n),
        out_shape=[pltpu.HBM(a.shape, a.dtype) for a in srcs + lands],
        input_output_aliases={i: i for i in range(2 * n)},
        compiler_params=pltpu.CompilerParams(has_side_effects=_DATAFLOW),
    )(*srcs, *lands, send_sems, recv_sems, *after)
    return list(res[:n]), list(res[n:])


def _slot_sum(g, name, tr):
    _, r, c = g.shape
    tr = min(tr, r)
    assert r % tr == 0

    def body(g_ref, o_ref):
        acc = g_ref[0].astype(F32)
        for s in range(1, N_DEV):
            acc = acc + g_ref[s].astype(F32)
        o_ref[...] = acc

    return pl.pallas_call(
        body, name=name, grid=(r // tr,),
        in_specs=[pl.BlockSpec((N_DEV, tr, c), lambda i: (0, i, 0))],
        out_specs=pl.BlockSpec((tr, c), lambda i: (i, 0)),
        out_shape=jax.ShapeDtypeStruct((r, c), F32),
        compiler_params=pltpu.CompilerParams(dimension_semantics=("parallel",)),
    )(g)


def _adam_update(w, gg, m, v):
    c1 = 1.0 / (1.0 - ADAM_B1 ** ADAM_STEP)
    c2 = 1.0 / (1.0 - ADAM_B2 ** ADAM_STEP)
    nm = ADAM_B1 * m + (1.0 - ADAM_B1) * gg
    nv = ADAM_B2 * v + (1.0 - ADAM_B2) * (gg * gg)
    return -ADAM_LR * ((nm * c1) / (jnp.sqrt(nv * c2) + ADAM_EPS) + ADAM_WD * w), nm, nv


def _adamw_reduce(me, recvs, owns, w, m, v, name, tr=256):
    nl, r, c = w.shape
    assert len(recvs) == nl and len(owns) == nl
    tr = min(tr, r)
    if r % tr == 0:
        tc, nblk = c, r // tr
        at = lambda i: (i, 0)
    else:
        tr, tc = r, min(c, 4 * LANES)
        assert c % tc == 0
        nblk = c // tc
        at = lambda i: (0, i)

    def parked(li, l, i):
        return jnp.where(l < li, 0, jnp.where(l > li, nblk - 1, i))

    def recv_spec(li):
        return pl.BlockSpec((N_DEV, tr, tc), lambda l, i, me_ref: (0, *at(parked(li, l, i))))

    def own_spec(li):
        return pl.BlockSpec((None, tr, tc), lambda l, i, me_ref: (me_ref[0], *at(parked(li, l, i))))

    def body(me_ref, *refs):
        rrefs, orefs = refs[:nl], refs[nl:2 * nl]
        w_ref, m_ref, v_ref, g_ref, d_ref, nm_ref, nv_ref = refs[2 * nl:]
        l = pl.program_id(0)

        def of_layer(vals):
            out = vals[0]
            for li in range(1, nl):
                out = jnp.where(l == li, vals[li], out)
            return out

        own = of_layer([o[...].astype(F32) for o in orefs])
        gg = None
        for s in range(N_DEV):
            slot = jnp.where(me_ref[0] == s, own, of_layer([rr[s].astype(F32) for rr in rrefs]))
            gg = slot if gg is None else gg + slot
        g_ref[...] = gg
        d_ref[...], nm_ref[...], nv_ref[...] = _adam_update(w_ref[...], gg, m_ref[...], v_ref[...])

    spec = pl.BlockSpec((None, tr, tc), lambda l, i, me_ref: (l, *at(i)))
    return pl.pallas_call(
        body, name=name,
        grid_spec=pltpu.PrefetchScalarGridSpec(
            num_scalar_prefetch=1, grid=(nl, nblk),
            in_specs=[recv_spec(li) for li in range(nl)] + [own_spec(li) for li in range(nl)] + [spec] * 3,
            out_specs=[spec] * 4),
        out_shape=[jax.ShapeDtypeStruct((nl, r, c), F32)] * 4,
        compiler_params=pltpu.CompilerParams(dimension_semantics=("arbitrary", "arbitrary")),
    )(me, *recvs, *owns, w, m, v)


def _adamw(w, g, m, v, name, tr=256):
    r, c = w.shape
    tr = min(tr, r)
    assert r % tr == 0

    def body(w_ref, g_ref, m_ref, v_ref, d_ref, nm_ref, nv_ref):
        d_ref[...], nm_ref[...], nv_ref[...] = _adam_update(w_ref[...], g_ref[...], m_ref[...], v_ref[...])

    spec = pl.BlockSpec((tr, c), lambda i: (i, 0))
    return pl.pallas_call(
        body, name=name, grid=(r // tr,), in_specs=[spec] * 4, out_specs=[spec] * 3,
        out_shape=[jax.ShapeDtypeStruct((r, c), F32)] * 3,
        compiler_params=pltpu.CompilerParams(dimension_semantics=("parallel",)),
    )(w, g, m, v)


def _rms_fwd(x, g, name, ts=1024, after=()):
    s, d = x.shape

    def fn(first, last, xv, gv):
        return [_rms(xv, gv[...])], []

    return _rowwise(fn, n_rows=s, ts=ts, name=name, rows=[(x, 0, d)], vecs=[g], row_outs=[(d, BF16)],
                    after=after)[0]


def _rms_bwd(x, dn, dres, g, name, ts=256):
    s, d = x.shape

    def fn(first, last, xv, dnv, drv, gv):
        _, vjp = jax.vjp(_rms, xv, gv[...])
        dx, dg = vjp(dnv.astype(F32))
        return [drv + dx], [dg]

    return _rowwise(fn, n_rows=s, ts=ts, name=name, rows=[(x, 0, d), (dn, 0, d), (dres, 0, d)], vecs=[g],
                    row_outs=[(d, F32)], acc_outs=[(1, d)])


def _dn_pre_fwd(qkvz, ba, wconv, alog, dt, heads, ts=512):
    s = qkvz.shape[0]
    d3 = wconv.shape[1]
    d = d3 // 3

    def fn(first, last, xc, bav, xp, wv, av, dv):
        xext = jnp.concatenate([jnp.where(first, 0.0, xp), xc], axis=0)
        cv = _causal_conv(xext, wv, DN_CONV, DN_HALO - (DN_CONV - 1), xc.shape[0])
        return list(_dn_point(cv, bav, av[...], dv[...], heads)), []

    return _rowwise(fn, n_rows=s, ts=ts, name="dn_pre_fwd", rows=[(qkvz, 0, d3), (ba, 0, LANES)],
                    prevs=[(qkvz, 0, d3, DN_HALO)], vecs=[wconv, alog, dt],
                    row_outs=[(d, F32), (d, F32), (d, F32), (LANES, F32)])


def _dn_pre_bwd(qkvz, ba, wconv, alog, dt, dq, dk, dv, dgb, dz, heads, ts=256):
    s = qkvz.shape[0]
    d3 = wconv.shape[1]
    d = d3 // 3
    lead = DN_HALO - (DN_CONV - 1)

    def fn(first, last, xc, bac, dqc, dkc, dvc, dgbc, dzc, xp, xn, ban, dqn, dkn, dvn, dgbn, wv, av, dtv):
        n = xc.shape[0]
        ext = lambda cur, nxt: jnp.concatenate([cur, nxt], axis=0)
        live = lambda nxt: jnp.where(last, 0.0, nxt)
        xall = jnp.concatenate([jnp.where(first, 0.0, xp), xc, live(xn)], axis=0)
        cv = _causal_conv(xall, wv, DN_CONV, lead, n + DN_HALO)
        (_, _, _, gbv), vjp = jax.vjp(lambda c, b: _dn_point(c, b, av[...], dtv[...], heads), cv, ext(bac, ban))
        dc, dba = vjp((ext(dqc, live(dqn)), ext(dkc, live(dkn)), ext(dvc, live(dvn)), ext(dgbc, live(dgbn))))
        dx = None
        dw = []
        for j in range(DN_CONV):
            term = _shift_rows(dc, DN_CONV - 1 - j)[:n] * wv[j:j + 1, :]
            dx = term if dx is None else dx + term
            dw.append(_colsum(dc[:n] * _shift_rows(xall, lead + j)[:n]))
        dba = dba[:n]
        return ([jnp.concatenate([dx, dzc], axis=-1), dba],
                [_stack_rows(dw, DN_CONV), _colsum(dgbc * gbv[:n]), _colsum(dba)])

    return _rowwise(fn, n_rows=s, ts=ts, name="dn_pre_bwd",
                    rows=[(qkvz, 0, d3), (ba, 0, LANES), (dq, 0, d), (dk, 0, d), (dv, 0, d), (dgb, 0, LANES),
                          (dz, 0, d)],
                    prevs=[(qkvz, 0, d3, DN_HALO)],
                    nexts=[(qkvz, 0, d3, DN_HALO), (ba, 0, LANES, DN_HALO), (dq, 0, d, DN_HALO),
                           (dk, 0, d, DN_HALO), (dv, 0, d, DN_HALO), (dgb, 0, LANES, DN_HALO)],
                    vecs=[wconv, alog, dt],
                    row_outs=[(4 * d, BF16), (LANES, BF16)], acc_outs=[(DN_CONV, d3), (1, LANES), (1, LANES)])


def _cv_mid_fwd(u, wdw, bdw, lng, lnb, ts=512):
    s = u.shape[0]
    d = u.shape[1] // 2

    def fn(first, last, uc, up, wv, bv, gv, lbv):
        uext = jnp.concatenate([jnp.where(first, 0.0, up), uc], axis=0)
        glu = uext[:, :d] * _sigmoid(uext[:, d:])
        c = _causal_conv(glu, wv, CV_WIDTH, CV_HALO - (CV_WIDTH - 1), uc.shape[0]) + bv[...]
        return [c, _ln_silu(c, gv[...], lbv[...])], []

    return _rowwise(fn, n_rows=s, ts=ts, name="cv_mid_fwd", rows=[(u, 0, 2 * d)], prevs=[(u, 0, 2 * d, CV_HALO)],
                    vecs=[wdw, bdw, lng, lnb], row_outs=[(d, F32), (d, BF16)])


def _cv_mid_bwd2(dc, u, wdw, ts=512):
    s, d = dc.shape

    def fn(first, last, dcc, uc, up, dcn, wv):
        n = dcc.shape[0]
        dcext = jnp.concatenate([dcc, jnp.where(last, 0.0, dcn)], axis=0)
        uext = jnp.concatenate([jnp.where(first, 0.0, up), uc], axis=0)
        glu = uext[:, :d] * _sigmoid(uext[:, d:])
        dglu = None
        dw = []
        for j in range(CV_WIDTH):
            term = _shift_rows(dcext, CV_WIDTH - 1 - j)[:n] * wv[j:j + 1, :]
            dglu = term if dglu is None else dglu + term
            dw.append(_colsum(dcc * _shift_rows(glu, CV_HALO - (CV_WIDTH - 1) + j)[:n]))
        u1, sg = uc[:, :d], _sigmoid(uc[:, d:])
        du = jnp.concatenate([dglu * sg, dglu * u1 * sg * (1.0 - sg)], axis=-1)
        return [du], [_stack_rows(dw, CV_HALO), _colsum(du)]

    return _rowwise(fn, n_rows=s, ts=ts, name="cv_mid_bwd2", rows=[(dc, 0, d), (u, 0, 2 * d)],
                    prevs=[(u, 0, 2 * d, CV_HALO)], nexts=[(dc, 0, d, CV_HALO)], vecs=[wdw],
                    row_outs=[(2 * d, BF16)], acc_outs=[(CV_HALO, d), (1, 2 * d)])


def _attn_fwd(q, k, v, name, ts=2048):
    s, d = q.shape

    def fn(first, last, qv, kv, vv):
        return [_attn_tile(qv.astype(F32), kv[...].astype(F32), vv[...].astype(F32))], []

    return _rowwise(fn, n_rows=s, ts=ts, name=name, rows=[(q, 0, d)], vecs=[k, v], row_outs=[(d, BF16)])[0]


def _attn_bwd(q, k, v, do, name, ts=2048):
    s, d = q.shape
    m = k.shape[0]

    def fn(first, last, qv, dov, kv, vv):
        _, vjp = jax.vjp(_attn_tile, qv.astype(F32), kv[...].astype(F32), vv[...].astype(F32))
        dq, dk, dv = vjp(dov.astype(F32))
        return [dq], [dk, dv]

    return _rowwise(fn, n_rows=s, ts=ts, name=name, rows=[(q, 0, d), (do, 0, d)], vecs=[k, v],
                    row_outs=[(d, BF16)], acc_outs=[(m, d), (m, d)])


def _pad_lanes(a, off=0):
    r, n = a.shape
    return jnp.pad(a, ((0, 0), (off, LANES - off - n)))


def _local_step(x, mem, tgt, w, fetch=None, emit=None, first_after=()):
    s, d = x.shape
    heads = d // DN_HEAD_DIM
    g = {}
    if fetch is None:
        fetch = lambda group, after: None
    if emit is None:
        emit = lambda group, grads: ()

    def add_res_rms(acc, res, gain):
        h = res + acc
        return h, _rms(h, gain)

    def rms_bwd_epi(acc, hx, dres, gain):
        _, vjp = jax.vjp(_rms, hx, gain)
        dx, dg = vjp(acc)
        return dres + dx, dg

    w_int = w["dn_w_in"][0]
    assert w_int.shape[0] == 4 * d + 2 * heads
    w_bat = jnp.pad(w_int[4 * d:], ((0, LANES - 2 * heads), (0, 0)))
    dn_norm = w["dn_norm"]
    alog = _pad_lanes(w["dn_a_log"], heads)
    dtb = _pad_lanes(w["dn_dt_bias"], heads)
    wconv = w["dn_w_conv"][0]
    n0 = _rms_fwd(x, dn_norm, "dn_rms", after=first_after)
    qkvz = _matmul(n0, w_int, "nt", [F32], name="dn_in_proj", b_rows=4 * d)
    ba = _matmul(n0, w_bat, "nt", [F32], name="dn_in_proj_ba")
    q, k, v, gb = _dn_pre_fwd(qkvz, ba, wconv, alog, dtb, heads)
    og, states, tms = _delta_fwd(q, k, v, gb, qkvz, w["dn_out_norm"], heads)
    fetch(1, [og])
    h1, nq0 = _matmul(og, w["dn_w_out"][0], "nn", [F32, BF16], name="dn_out_proj", epi=add_res_rms,
                      mn_extras=[x], row_extras=[w["xa_norm"][0:1]], slab=EPI_SLAB)

    def xattn_fwd(h, nq, layer, next_gain):
        qx = _matmul(nq, w["xa_w_q"][layer], "nn", [BF16], name=f"xa{layer}_q")
        mn = _rms_fwd(mem, w["xa_mem_norm"][layer:layer + 1], f"xa{layer}_mem_rms")
        kv = _matmul(mn, w["xa_w_kv"][layer], "nn", [BF16], name=f"xa{layer}_kv")
        kx, vx = kv[:, :d], kv[:, d:]
        ox = _attn_fwd(qx, kx, vx, f"xa{layer}_attn")
        hn, nn = _matmul(ox, w["xa_w_o"][layer], "nn", [F32, BF16], name=f"xa{layer}_o", epi=add_res_rms,
                         mn_extras=[h], row_extras=[next_gain], slab=EPI_SLAB)
        return hn, nn, (h, nq, qx, mn, kx, vx, ox)

    def sq_relu(t):
        r = jnp.maximum(t.astype(F32), 0.0)
        return r * r

    def loss_epi(acc, res, target, gain):
        def cols(hh, gg):
            e = _rms(hh, gg) - target
            return _colsum(e * e) * (0.5 / d)

        per_col, vjp = jax.vjp(cols, res + acc, gain)
        dhx, dgain = vjp(jnp.ones_like(per_col))
        return dhx, dgain, per_col

    def mlp_fwd(h, nm, layer, next_gain):
        u = _matmul(nm, w["mlp_w_up"][layer], "nn", [BF16], name=f"mlp{layer}_up")
        if next_gain is None:
            hn, *nn = _matmul(u, w["mlp_w_down"][layer], "nn", [F32], name=f"mlp{layer}_down_loss",
                              epi=loss_epi, mn_extras=[h, tgt], row_extras=[w["final_norm"].reshape(1, d)],
                              n_rowsum=2, tk=MM_DEEP, slab=EPI_SLAB, a_pre=sq_relu)
        else:
            hn, nn = _matmul(u, w["mlp_w_down"][layer], "nn", [F32, BF16], name=f"mlp{layer}_down",
                             epi=add_res_rms, mn_extras=[h], row_extras=[next_gain], tk=MM_DEEP, slab=EPI_SLAB,
                             a_pre=sq_relu)
        return hn, nn, (h, nm, u)

    h2, nm0, xa0 = xattn_fwd(h1, nq0, 0, w["mlp_norm"][0:1])
    fetch(2, [h2])
    h3, n1, mlp0 = mlp_fwd(h2, nm0, 0, w["cv_norm"])

    u_cv = _matmul(n1, w["cv_w_pw1"][0], "nn", [F32], name="cv_pw1", epi=lambda acc, b: (acc + b,),
                   row_extras=[w["cv_b_pw1"]])
    wdw = jnp.pad(w["cv_w_dw"][0], ((0, CV_HALO - CV_WIDTH), (0, 0)))
    c_cv, s_cv = _cv_mid_fwd(u_cv, wdw, w["cv_b_dw"], w["cv_ln_g"], w["cv_ln_b"])
    h4, nq1 = _matmul(s_cv, w["cv_w_pw2"][0], "nn", [F32, BF16], name="cv_pw2",
                      epi=lambda acc, res, b, gain: add_res_rms(acc + b, res, gain), mn_extras=[h3],
                      row_extras=[w["cv_b_pw2"], w["xa_norm"][1:2]], slab=EPI_SLAB)
    fetch(3, [h4])
    h5, nm1, xa1 = xattn_fwd(h4, nq1, 1, w["mlp_norm"][1:2])
    fetch(4, [h5])
    dh, (g_fn, loss_cols), mlp1 = mlp_fwd(h5, nm1, 1, None)
    g["final_norm"] = g_fn.reshape(d)
    loss = jnp.sum(loss_cols, axis=1, keepdims=True)

    def mlp_bwd(dh, layer, saved, after=()):
        h, nm, u = saved
        du = _matmul(dh, w["mlp_w_down"][layer], "nt", [BF16], name=f"mlp{layer}_down_dx", after=after,
                     epi=lambda acc, uu: (acc * 2.0 * jnp.maximum(uu.astype(F32), 0.0),), mn_extras=[u])
        gdown = _matmul(u, dh, "tn", [BF16], name=f"mlp{layer}_down_dw", tm=MM_DEEP, a_pre=sq_relu)
        dhn, gn = _matmul(du, w["mlp_w_up"][layer], "nt", [F32], name=f"mlp{layer}_up_dx", epi=rms_bwd_epi,
                          mn_extras=[h, dh], row_extras=[w["mlp_norm"][layer:layer + 1]], n_rowsum=1,
                          slab=EPI_SLAB, tk=MM_DEEP)
        gup = _matmul(nm, du, "tn", [BF16], name=f"mlp{layer}_up_dw", out_dm=True, tk=MM_DEEP)
        return dhn, gup, gdown, gn

    def xattn_bwd(dh, layer, saved):
        h, nq, qx, mn, kx, vx, ox = saved
        dox = _matmul(dh, w["xa_w_o"][layer], "nt", [BF16], name=f"xa{layer}_o_dx")
        go = _matmul(ox, dh, "tn", [BF16], name=f"xa{layer}_o_dw", tk=MM_DEEP)
        dqx, dkx, dvx = _attn_bwd(qx, kx, vx, dox, f"xa{layer}_attn_bwd")

        def epi(acc, hx, dres, gain):
            dhx, dg = rms_bwd_epi(acc, hx, dres, gain)
            return dhx, dg, _colsum(dhx)

        dhn, gn, dh_cols = _matmul(dqx, w["xa_w_q"][layer], "nt", [F32], name=f"xa{layer}_q_dx", epi=epi,
                                   mn_extras=[h, dh], row_extras=[w["xa_norm"][layer:layer + 1]], n_rowsum=2,
                                   slab=EPI_SLAB)
        gq = _matmul(nq, dqx, "tn", [BF16], name=f"xa{layer}_q_dw", tk=MM_DEEP)
        dkv = jnp.concatenate([dkx, dvx], axis=-1)
        gkv = _matmul(mn, dkv, "tn", [BF16], name=f"xa{layer}_kv_dw", out_dm=True)
        dmn = _matmul(dkv, w["xa_w_kv"][layer], "nt", [F32], name=f"xa{layer}_kv_dx", tk=MM_DEEP)
        _, gmem = _rms_bwd(mem, dmn, dmn, w["xa_mem_norm"][layer:layer + 1], f"xa{layer}_mem_rms_bwd")
        return dhn, gq, gkv, go, gn, gmem, dh_cols

    dh, gup1, gdown1, gmn1 = mlp_bwd(dh, 1, mlp1)
    dh, gq1, gkv1, go1, gxn1, gmem1, g_b2 = xattn_bwd(dh, 1, xa1)
    g.update(mlp_w_up=[None, gup1], mlp_w_down=[None, gdown1], xa_w_q=[None, gq1], xa_w_kv=[None, gkv1],
             xa_w_o=[None, go1])
    tok = emit(3, g)

    def ln_bwd_epi(acc, cx, gain, bias):
        _, vjp = jax.vjp(_ln_silu, cx, gain, bias)
        dc, dg, db = vjp(acc)
        return dc, dg, db, _colsum(dc)

    dc_cv, g_lng, g_lnb, g_bdw = _matmul(dh, w["cv_w_pw2"][0], "nt", [F32], name="cv_pw2_dx", after=tok,
                                        epi=ln_bwd_epi, mn_extras=[c_cv],
                                        row_extras=[w["cv_ln_g"], w["cv_ln_b"]], n_rowsum=3, slab=EPI_SLAB)
    g["cv_w_pw2"] = [_matmul(s_cv, dh, "tn", [BF16], name="cv_pw2_dw", tk=MM_DEEP)]
    du_cv, g_wdw, g_b1 = _cv_mid_bwd2(dc_cv, u_cv, wdw)
    g["cv_w_pw1"] = [_matmul(n1, du_cv, "tn", [BF16], name="cv_pw1_dw", out_dm=True, tk=MM_DEEP)]
    dh, g_cvn = _matmul(du_cv, w["cv_w_pw1"][0], "nt", [F32], name="cv_pw1_dx", epi=rms_bwd_epi,
                        mn_extras=[h3, dh], row_extras=[w["cv_norm"]], n_rowsum=1, slab=EPI_SLAB, tk=MM_DEEP)
    g.update(cv_ln_g=g_lng, cv_ln_b=g_lnb, cv_b_dw=g_bdw, cv_b_pw2=g_b2, cv_b_pw1=g_b1, cv_norm=g_cvn,
             cv_w_dw=g_wdw[:CV_WIDTH][None])

    tok = emit(2, g)
    dh, gup0, gdown0, gmn0 = mlp_bwd(dh, 0, mlp0, after=tok)
    dh, gq0, gkv0, go0, gxn0, gmem0, _ = xattn_bwd(dh, 0, xa0)
    g["mlp_w_up"][0] = gup0
    g["mlp_w_down"][0] = gdown0
    g["mlp_norm"] = jnp.concatenate([gmn0, gmn1], axis=0)
    g["xa_w_q"][0] = gq0
    g["xa_w_kv"][0] = gkv0
    g["xa_w_o"][0] = go0
    g["xa_norm"] = jnp.concatenate([gxn0, gxn1], axis=0)
    g["xa_mem_norm"] = jnp.concatenate([gmem0, gmem1], axis=0)
    g["dn_w_out"] = [_matmul(og, dh, "tn", [BF16], name="dn_out_proj_dw", tk=MM_DEEP)]
    tok = emit(1, g)
    dog = _matmul(dh, w["dn_w_out"][0], "nt", [BF16], name="dn_out_proj_dx", after=tok)
    dq, dk, dv, dgb, dz, g_on = _delta_bwd(q, k, v, gb, qkvz, w["dn_out_norm"], states, tms, dog, heads)
    dqkvz, dba, g_wconv, g_alog, g_dt = _dn_pre_bwd(qkvz, ba, wconv, alog, dtb, dq, dk, dv, dgb, dz, heads)
    g_qkvzt = _matmul(dqkvz, n0, "tn", [BF16], name="dn_in_proj_dw", tk=MM_DEEP)
    g_bat = _matmul(dba, n0, "tn", [BF16], name="dn_in_proj_ba_dw", tk=MM_DEEP)
    g["dn_w_in"] = [jnp.concatenate([g_qkvzt, g_bat[:2 * heads]], axis=0)]
    g["dn_w_conv"] = g_wconv[None]
    tok = emit(0, g)
    dn0a = _matmul(dba, w_bat, "nn", [F32], name="dn_in_proj_ba_dx", after=tok)
    grad_x, g_dnn = _matmul(dqkvz, w_int, "nn", [F32], name="dn_in_proj_dx", b_rows=4 * d,
                            epi=lambda acc, part, hx, dres, gain: rms_bwd_epi(acc + part, hx, dres, gain),
                            mn_extras=[dn0a, x, dh], row_extras=[dn_norm], n_rowsum=1, slab=EPI_SLAB)
    g.update(dn_norm=g_dnn, dn_out_norm=g_on,
             dn_a_log=g_alog[:, heads:2 * heads], dn_dt_bias=g_dt[:, heads:2 * heads])
    return loss, grad_x, g


def _round_up(n, m):
    return (n + m - 1) // m * m


def _pack_rows(parts, cols, row_mult):
    lead = parts[0].shape[:-1]
    flat, offs, off = [], [], 0
    for p in parts:
        n = _round_up(p.shape[-1], cols)
        flat.append(jnp.pad(p, [(0, 0)] * len(lead) + [(0, n - p.shape[-1])]))
        offs.append(off)
        off += n
    total = _round_up(off, cols * row_mult)
    if total > off:
        flat.append(jnp.zeros(lead + (total - off,), parts[0].dtype))
    return jnp.concatenate(flat, axis=-1).reshape(lead + (total // cols, cols)), offs


def _unpack(packed, offs, shapes):
    lead = packed.shape[:-2]
    flat = packed.reshape(lead + (-1,))
    out = []
    for off, shp in zip(offs, shapes):
        n = 1
        for v in shp:
            n *= v
        out.append(flat[..., off:off + n].reshape(lead + tuple(shp)))
    return out


def kernel(x, mem, dn_norm, dn_w_in, dn_w_conv, dn_a_log, dn_dt_bias, dn_out_norm, dn_w_out, cv_norm, cv_w_pw1, cv_b_pw1, cv_w_dw, cv_b_dw, cv_ln_g, cv_ln_b, cv_w_pw2, cv_b_pw2, xa_norm, xa_mem_norm, xa_w_q, xa_w_kv, xa_w_o, mlp_norm, mlp_w_up, mlp_w_down, final_norm, loss_target, m_dn_norm, m_dn_w_in, m_dn_w_conv, m_dn_a_log, m_dn_dt_bias, m_dn_out_norm, m_dn_w_out, m_cv_norm, m_cv_w_pw1, m_cv_b_pw1, m_cv_w_dw, m_cv_b_dw, m_cv_ln_g, m_cv_ln_b, m_cv_w_pw2, m_cv_b_pw2, m_xa_norm, m_xa_mem_norm, m_xa_w_q, m_xa_w_kv, m_xa_w_o, m_mlp_norm, m_mlp_w_up, m_mlp_w_down, m_final_norm, v_dn_norm, v_dn_w_in, v_dn_w_conv, v_dn_a_log, v_dn_dt_bias, v_dn_out_norm, v_dn_w_out, v_cv_norm, v_cv_w_pw1, v_cv_b_pw1, v_cv_w_dw, v_cv_b_dw, v_cv_ln_g, v_cv_ln_b, v_cv_w_pw2, v_cv_b_pw2, v_xa_norm, v_xa_mem_norm, v_xa_w_q, v_xa_w_kv, v_xa_w_o, v_mlp_norm, v_mlp_w_up, v_mlp_w_down, v_final_norm):
    wsh = dict(dn_norm=dn_norm, dn_w_in=dn_w_in, dn_w_conv=dn_w_conv, dn_a_log=dn_a_log, dn_dt_bias=dn_dt_bias, dn_out_norm=dn_out_norm, dn_w_out=dn_w_out, cv_norm=cv_norm, cv_w_pw1=cv_w_pw1, cv_b_pw1=cv_b_pw1, cv_w_dw=cv_w_dw, cv_b_dw=cv_b_dw, cv_ln_g=cv_ln_g, cv_ln_b=cv_ln_b, cv_w_pw2=cv_w_pw2, cv_b_pw2=cv_b_pw2, xa_norm=xa_norm, xa_mem_norm=xa_mem_norm, xa_w_q=xa_w_q, xa_w_kv=xa_w_kv, xa_w_o=xa_w_o, mlp_norm=mlp_norm, mlp_w_up=mlp_w_up, mlp_w_down=mlp_w_down, final_norm=final_norm)
    msh = dict(dn_norm=m_dn_norm, dn_w_in=m_dn_w_in, dn_w_conv=m_dn_w_conv, dn_a_log=m_dn_a_log, dn_dt_bias=m_dn_dt_bias, dn_out_norm=m_dn_out_norm, dn_w_out=m_dn_w_out, cv_norm=m_cv_norm, cv_w_pw1=m_cv_w_pw1, cv_b_pw1=m_cv_b_pw1, cv_w_dw=m_cv_w_dw, cv_b_dw=m_cv_b_dw, cv_ln_g=m_cv_ln_g, cv_ln_b=m_cv_ln_b, cv_w_pw2=m_cv_w_pw2, cv_b_pw2=m_cv_b_pw2, xa_norm=m_xa_norm, xa_mem_norm=m_xa_mem_norm, xa_w_q=m_xa_w_q, xa_w_kv=m_xa_w_kv, xa_w_o=m_xa_w_o, mlp_norm=m_mlp_norm, mlp_w_up=m_mlp_w_up, mlp_w_down=m_mlp_w_down, final_norm=m_final_norm)
    vsh = dict(dn_norm=v_dn_norm, dn_w_in=v_dn_w_in, dn_w_conv=v_dn_w_conv, dn_a_log=v_dn_a_log, dn_dt_bias=v_dn_dt_bias, dn_out_norm=v_dn_out_norm, dn_w_out=v_dn_w_out, cv_norm=v_cv_norm, cv_w_pw1=v_cv_w_pw1, cv_b_pw1=v_cv_b_pw1, cv_w_dw=v_cv_w_dw, cv_b_dw=v_cv_b_dw, cv_ln_g=v_cv_ln_g, cv_ln_b=v_cv_ln_b, cv_w_pw2=v_cv_w_pw2, cv_b_pw2=v_cv_b_pw2, xa_norm=v_xa_norm, xa_mem_norm=v_xa_mem_norm, xa_w_q=v_xa_w_q, xa_w_kv=v_xa_w_kv, xa_w_o=v_xa_w_o, mlp_norm=v_mlp_norm, mlp_w_up=v_mlp_w_up, mlp_w_down=v_mlp_w_down, final_norm=v_final_norm)

    big_axis = dict(BIG)
    for src in (wsh, msh, vsh):
        src["dn_w_in"] = jnp.swapaxes(src["dn_w_in"], 1, 2)
    big_axis["dn_w_in"] = 1

    small_pack, small_offs = _pack_rows([wsh[nm].reshape(-1) for nm in SMALL_SH], LANES, 8)
    w = {nm: [None] * wsh[nm].shape[0] for nm in big_axis}

    def put_weights(group, gathered):
        for (nm, layer), gth in zip(group, gathered):
            if big_axis[nm] == 1:
                w[nm][layer] = gth.reshape(N_DEV * gth.shape[1], gth.shape[2])
            else:
                w[nm][layer] = gth

    first = _all_gather([wsh[nm][layer].astype(BF16) for nm, layer in GATHER_GROUPS[0]] + [small_pack],
                        "weights_all_gather_0")
    put_weights(GATHER_GROUPS[0], first)
    me = _dev_index(*_mesh_me())
    gather_handles, tokens = {}, []
    for gi in range(1, len(GATHER_GROUPS)):
        shards = [wsh[nm][layer].astype(BF16) for nm, layer in GATHER_GROUPS[gi]]
        lands = [lax.dynamic_update_slice(lax.empty((N_DEV,) + s.shape, s.dtype), s[None], (me, 0, 0))
                 for s in shards]
        gather_handles[gi], tok = _exchange_start(shards, lands, False, f"weights_gather_{gi}",
                                                  after=[first[-1]] + tokens)
        tokens.append(tok)
    for nm, gth in zip(SMALL_SH, _unpack(first[-1], small_offs, [wsh[nm].shape for nm in SMALL_SH])):
        w[nm] = jnp.moveaxis(gth, 0, -2).reshape(gth.shape[1:-1] + (N_DEV * gth.shape[-1],))
    for nm in REPL:
        w[nm] = wsh[nm]

    def fetch(gi, after):
        put_weights(GATHER_GROUPS[gi], _exchange_wait(gather_handles[gi], after)[1])

    scatter_handles = {}

    def emit(gi, g):
        blocks = []
        for nm, layer in SCATTER_GROUPS[gi]:
            gw = g[nm][layer]
            if big_axis[nm] == 1:
                gw = gw.reshape(N_DEV, gw.shape[0] // N_DEV, gw.shape[1])
            blocks.append(gw)
        if gi == 0:
            gsmall_pack, _ = _pack_rows(
                [jnp.moveaxis(g[nm].reshape(g[nm].shape[:-1] + (N_DEV, -1)), -2, 0).reshape(N_DEV, -1)
                 for nm in SMALL_SH], LANES, 8)
            blocks.append(gsmall_pack)
        lands = [lax.empty(b.shape, b.dtype) for b in blocks]
        scatter_handles[gi], tok = _exchange_start(blocks, lands, True, f"grads_scatter_{gi}")
        return [tok]

    loss_part, grad_x, g = _local_step(x[0], mem[0], loss_target[0], w, fetch, emit, tokens)

    recv = {nm: [None] * wsh[nm].shape[0] for nm in big_axis}
    sent = {nm: [None] * wsh[nm].shape[0] for nm in big_axis}
    gsh, delta, new_m, new_v = {}, {}, {}, {}
    after = [grad_x]
    done = set()
    me_arr = me.astype(jnp.int32).reshape(1)
    def small_adamw(names, name):
        packs = []
        for src in (wsh, gsh, msh, vsh):
            pk, offs = _pack_rows([src[nm].reshape(-1) for nm in names], LANES, 8)
            packs.append(pk)
        outs = _adamw(*packs, name)
        for dst, pk in zip((delta, new_m, new_v), outs):
            for nm, val in zip(names, _unpack(pk, offs, [wsh[nm].shape for nm in names])):
                dst[nm] = val
        return outs[0]

    for gi in reversed(range(len(SCATTER_GROUPS))):
        if gi == 0:
            repl_pack, repl_offs = _pack_rows([g[nm].reshape(-1) for nm in REPL] + [loss_part[:, :1].reshape(-1)],
                                              LANES, 8)
            (repl_all,) = _all_gather([repl_pack], "repl_grads_all_gather")
            repl_red = _slot_sum(repl_all, "repl_grads_sum", SLOT_SUM_ROWS)
            *repl_vals, loss_sum = _unpack(repl_red, repl_offs, [wsh[nm].shape for nm in REPL] + [(1,)])
            for nm, val in zip(REPL, repl_vals):
                gsh[nm] = val
            after = after + [small_adamw(list(REPL), "adamw_repl")]
        sources, landed = _exchange_wait(scatter_handles[gi], after)
        for (nm, layer), src, r in zip(SCATTER_GROUPS[gi], sources, landed):
            sent[nm][layer], recv[nm][layer] = src, r
        if gi == 0:
            slot = lax.broadcasted_iota(jnp.int32, landed[-1].shape, 0)
            rsmall = jnp.where(slot == me, sources[-1], landed[-1])
        for nm in big_axis:
            if nm not in done and all(r is not None for r in recv[nm]):
                gsh[nm], delta[nm], new_m[nm], new_v[nm] = _adamw_reduce(
                    me_arr, recv[nm], sent[nm], wsh[nm], msh[nm], vsh[nm], f"adamw_{nm}")
                done.add(nm)
                after = [delta[nm]]
    gsmall_red = _slot_sum(rsmall, "grads_small_sum", SLOT_SUM_ROWS)
    for nm, val in zip(SMALL_SH, _unpack(gsmall_red, small_offs, [wsh[nm].shape for nm in SMALL_SH])):
        gsh[nm] = val
    small_adamw(list(SMALL_SH), "adamw_small")
    for dst in (gsh, delta, new_m, new_v):
        dst["dn_w_in"] = jnp.swapaxes(dst["dn_w_in"], 1, 2)
    return (loss_sum.reshape(()), grad_x[None], *[gsh[nm] for nm in WEIGHTS], *[delta[nm] for nm in WEIGHTS],
            *[new_m[nm] for nm in WEIGHTS], *[new_v[nm] for nm in WEIGHTS])
```

```python
import functools

import jax
import jax.numpy as jnp
from jax import lax
from jax.experimental import pallas as pl
from jax.experimental.pallas import tpu as pltpu

F32 = jnp.float32
BF16 = jnp.bfloat16
MESH_IDS = pl.DeviceIdType.MESH

N_DEV = 8
LANES = 128
RMS_EPS = 1e-6
LN_EPS = 1e-5
DN_HEAD_DIM = 128
DN_CONV = 4
DN_CHUNK = 64
CV_WIDTH = 31
XA_HEADS = 4
MM_TILE = 1024
MM_DEEP = 2048
EPI_SLAB = 256
SLOT_SUM_ROWS = 512
DN_HALO = 8
CV_HALO = 32

ADAM_LR = 0.001
ADAM_B1 = 0.9
ADAM_B2 = 0.999
ADAM_EPS = 1e-08
ADAM_WD = 0.01
ADAM_STEP = 10

BIG = (("dn_w_in", 2), ("dn_w_out", 1), ("cv_w_pw1", 2), ("cv_w_pw2", 1), ("xa_w_q", 1), ("xa_w_kv", 2),
       ("xa_w_o", 1), ("mlp_w_up", 2), ("mlp_w_down", 1))
_LAYER_GROUP = ("xa_w_q", "xa_w_o", "mlp_w_down", "xa_w_kv", "mlp_w_up")
GATHER_GROUPS = (
    (("dn_w_in", 0),),
    (("dn_w_out", 0),) + tuple((nm, 0) for nm in _LAYER_GROUP),
    (("cv_w_pw2", 0), ("cv_w_pw1", 0)),
    (("xa_w_q", 1), ("xa_w_o", 1), ("xa_w_kv", 1)),
    (("mlp_w_down", 1), ("mlp_w_up", 1)),
)
SCATTER_GROUPS = (
    (("dn_w_in", 0),),
    (("dn_w_out", 0),) + tuple((nm, 0) for nm in _LAYER_GROUP),
    (("cv_w_pw2", 0), ("cv_w_pw1", 0)),
    tuple((nm, 1) for nm in _LAYER_GROUP),
)
SMALL_SH = ("cv_norm", "cv_b_pw1", "cv_b_dw", "cv_ln_g", "cv_ln_b", "cv_b_pw2", "cv_w_dw", "dn_w_conv")
REPL = ("dn_norm", "dn_a_log", "dn_dt_bias", "dn_out_norm", "xa_norm", "xa_mem_norm", "mlp_norm", "final_norm")
WEIGHTS = ("dn_norm", "dn_w_in", "dn_w_conv", "dn_a_log", "dn_dt_bias", "dn_out_norm", "dn_w_out", "cv_norm",
           "cv_w_pw1", "cv_b_pw1", "cv_w_dw", "cv_b_dw", "cv_ln_g", "cv_ln_b", "cv_w_pw2", "cv_b_pw2", "xa_norm",
           "xa_mem_norm", "xa_w_q", "xa_w_kv", "xa_w_o", "mlp_norm", "mlp_w_up", "mlp_w_down", "final_norm")


def _dot_dims(mode, batched):
    o = 1 if batched else 0
    contract = {"nn": ((1 + o,), (o,)), "nt": ((1 + o,), (1 + o,)), "tn": ((o,), (o,))}[mode]
    return (contract, (((0,), (0,)) if batched else ((), ())))


def _bdot(a, b, mode):
    return lax.dot_general(a.astype(BF16), b.astype(BF16), _dot_dims(mode, a.ndim == 3),
                           preferred_element_type=F32)


@functools.partial(jax.custom_vjp, nondiff_argnums=(2,))
def _mm(a, b, mode):
    return _bdot(a, b, mode)


def _mm_fwd(a, b, mode):
    return _bdot(a, b, mode), (a, b)


def _mm_bwd(mode, res, ct):
    a, b = res
    if mode == "nn":
        da, db = _bdot(ct, b, "nt"), _bdot(a, ct, "tn")
    elif mode == "nt":
        da, db = _bdot(ct, b, "nn"), _bdot(ct, a, "tn")
    else:
        da, db = _bdot(b, ct, "nt"), _bdot(a, ct, "nn")
    return da.astype(a.dtype), db.astype(b.dtype)


_mm.defvjp(_mm_fwd, _mm_bwd)


def _sigmoid(x):
    return 0.5 * (jnp.tanh(0.5 * x) + 1.0)


def _silu(x):
    return x * _sigmoid(x)


def _softplus(x):
    return jnp.maximum(x, 0.0) + jnp.log(1.0 + jnp.exp(-jnp.abs(x)))


def _rms(x, g):
    r = lax.rsqrt(jnp.mean(x * x, axis=-1, keepdims=True) + RMS_EPS)
    return x * r * g


def _shift_rows(x, off):
    if off == 0:
        return x
    return pltpu.roll(x, x.shape[0] - off, 0)


def _series_dot(a, b, mode):
    return _bdot(a, b, mode)


def _chunk_masks(c):
    ii = lax.broadcasted_iota(jnp.int32, (c, c), 0)
    jj = lax.broadcasted_iota(jnp.int32, (c, c), 1)
    return (ii == jj).astype(F32), ii >= jj, ii > jj


def _neumann_inverse(lm):
    n = lm.shape[-1]
    t = -lm
    p = lm
    size = 2
    while size < n:
        size *= 2
        p = _series_dot(p, p, "nn")
        t = t + p + _series_dot(t, p, "nn")
    return t


def _apply_inverse(tm, rhs, mode):
    return rhs + _series_dot(tm, rhs, mode)


@jax.custom_vjp
def _unit_lower_solve(lm, rhs, tm):
    return _apply_inverse(tm, rhs, "nn")


def _uls_fwd(lm, rhs, tm):
    sol = _apply_inverse(tm, rhs, "nn")
    return sol, (tm, sol)


def _uls_bwd(res, ct):
    tm, sol = res
    d_rhs = _apply_inverse(tm, ct, "tn")
    return -_bdot(d_rhs, sol, "nt"), d_rhs, jnp.zeros_like(tm)


_unit_lower_solve.defvjp(_uls_fwd, _uls_bwd)


def _delta_chunk(q, k, v, gcol, bcol, s0, tm=None):
    c = q.shape[1]
    eye, causal, strict = _chunk_masks(c)
    grow = jnp.sum(eye * gcol, axis=1, keepdims=True)
    gc = jnp.sum(jnp.where(causal, grow, 0.0), axis=2, keepdims=True)
    gc_row = jnp.sum(eye * gc, axis=1, keepdims=True)
    decay = jnp.exp(jnp.where(causal, gc - gc_row, -jnp.inf))
    kb = k * bcol
    on_k = _mm(jnp.concatenate([kb, q], axis=1), k, "nt")
    lm = jnp.where(strict, on_k[:, :c] * decay, 0.0)
    attn = on_k[:, c:] * decay
    if tm is None:
        tm = _neumann_inverse(lax.stop_gradient(lm))
    egc = jnp.exp(gc)
    rhs = jnp.concatenate([v * bcol, kb * egc], axis=-1)
    sol = _unit_lower_solve(lm, rhs, tm)
    dv_ = v.shape[-1]
    u, w = sol[..., :dv_], sol[..., dv_:]
    gl = jnp.sum(grow, axis=2, keepdims=True)
    kd = k * jnp.exp(gl - gc)
    on_s = _mm(jnp.concatenate([w, q * egc], axis=1), s0, "nn")
    v_new = u - on_s[:, :c]
    o = on_s[:, c:] + _mm(attn, v_new, "nn")
    s1 = s0 * jnp.exp(gl) + _mm(kd, v_new, "tn")
    return o, s1, tm


def _dn_point(cv, ba, alog, dt, heads):
    a = _silu(cv)
    d = cv.shape[1] // 3
    qs, ks = [], []
    for h in range(heads):
        qh = a[:, h * DN_HEAD_DIM:(h + 1) * DN_HEAD_DIM]
        qs.append(qh * lax.rsqrt(jnp.sum(qh * qh, axis=-1, keepdims=True) + 1e-6) * (DN_HEAD_DIM ** -0.5))
        kh = a[:, d + h * DN_HEAD_DIM:d + (h + 1) * DN_HEAD_DIM]
        ks.append(kh * lax.rsqrt(jnp.sum(kh * kh, axis=-1, keepdims=True) + 1e-6))
    q = jnp.concatenate(qs, axis=-1)
    k = jnp.concatenate(ks, axis=-1)
    v = a[:, 2 * d:]
    lane = lax.broadcasted_iota(jnp.int32, ba.shape, 1)
    beta = _sigmoid(ba)
    g = -jnp.exp(alog) * _softplus(ba + dt)
    gb = jnp.where(lane < heads, beta, jnp.where(lane < 2 * heads, g, 0.0))
    return q, k, v, gb


def _attn_tile(q, k, v):
    hd = q.shape[1] // XA_HEADS
    outs = []
    for h in range(XA_HEADS):
        sl = slice(h * hd, (h + 1) * hd)
        s = _mm(q[:, sl], k[:, sl], "nt") * (hd ** -0.5)
        m = lax.stop_gradient(jnp.max(s, axis=-1, keepdims=True))
        e = jnp.exp(s - m)
        p = e / jnp.sum(e, axis=-1, keepdims=True)
        outs.append(_mm(p, v[:, sl], "nn"))
    return jnp.concatenate(outs, axis=-1)


def _ln_silu(c, g, b):
    mu = jnp.mean(c, axis=-1, keepdims=True)
    xc = c - mu
    y = xc * lax.rsqrt(jnp.mean(xc * xc, axis=-1, keepdims=True) + LN_EPS)
    return _silu(y * g + b)


def _causal_conv(xext, w, width, lead, ts):
    acc = None
    for j in range(width):
        term = _shift_rows(xext, lead + j)[:ts] * w[j:j + 1, :]
        acc = term if acc is None else acc + term
    return acc


def _colsum(x):
    return jnp.sum(x, axis=0, keepdims=True)


def _stack_rows(rows, n_rows):
    c = rows[0].shape[1]
    ridx = lax.broadcasted_iota(jnp.int32, (n_rows, c), 0)
    out = jnp.zeros((n_rows, c), F32)
    for j, r in enumerate(rows):
        out = out + jnp.where(ridx == j, r, 0.0)
    return out


def _matmul(a, b, mode, out_dtypes, *, name, epi=None, mn_extras=(), row_extras=(), out_dm=False, after=(),
            n_rowsum=0, slab=0, b_rows=None, a_pre=None, tm=MM_TILE, tn=MM_TILE, tk=MM_TILE):
    b_dm = b.ndim == 3
    b_shape = (b.shape[1], N_DEV * b.shape[2]) if b_dm else b.shape
    if b_rows is not None:
        assert not b_dm and b_rows <= b.shape[0]
        b_shape = (b_rows, b.shape[1])
    if mode == "nn":
        (m, k), (k2, n) = a.shape, b_shape
    elif mode == "nt":
        (m, k), (n, k2) = a.shape, b_shape
    else:
        (k, m), (k2, n) = a.shape, b_shape
    assert k == k2, (a.shape, b.shape, mode)
    tm, tn, tk = min(tm, m), min(tn, n), min(tk, k)
    cb, nb = 0, 1
    if b_dm:
        assert mode in ("nn", "nt")
        cb = b.shape[2]
        nb = max(1, (tn if mode == "nn" else tk) // cb)
        if mode == "nn":
            tn = nb * cb
        else:
            tk = nb * cb
    co, no = 0, 1
    if out_dm:
        co = n // N_DEV
        no = max(1, tn // co)
        tn = no * co
    assert m % tm == 0 and n % tn == 0 and k % tk == 0, (m, n, k, tm, tn, tk)
    nk = k // tk
    if mode == "tn":
        a_spec = pl.BlockSpec((tk, tm), lambda j, i, kk: (kk, i))
    else:
        a_spec = pl.BlockSpec((tm, tk), lambda j, i, kk: (i, kk))
    if b_dm:
        b_spec = (pl.BlockSpec((nb, tn, cb), lambda j, i, kk: (kk, j, 0)) if mode == "nt"
                  else pl.BlockSpec((nb, tk, cb), lambda j, i, kk: (j, kk, 0)))
    else:
        b_spec = (pl.BlockSpec((tn, tk), lambda j, i, kk: (j, kk)) if mode == "nt"
                  else pl.BlockSpec((tk, tn), lambda j, i, kk: (kk, j)))
    mn_spec = pl.BlockSpec((tm, tn), lambda j, i, kk: (i, j))
    row_spec = pl.BlockSpec((1, tn), lambda j, i, kk: (0, j))
    n_extra = len(mn_extras) + len(row_extras)
    n_out = len(out_dtypes)
    in_specs = ([a_spec, b_spec] + [mn_spec] * len(mn_extras) + [row_spec] * len(row_extras)
                + [_ANY_SPEC] * len(after))
    args = [a, b, *mn_extras, *row_extras, *after]
    if out_dm:
        out_specs = [pl.BlockSpec((no, tm, co), lambda j, i, kk: (j, i, 0))] * n_out
        out_shape = [jax.ShapeDtypeStruct((N_DEV, m, co), dt) for dt in out_dtypes]
    else:
        out_specs = [mn_spec] * n_out
        out_shape = [jax.ShapeDtypeStruct((m, n), dt) for dt in out_dtypes]
    out_specs = out_specs + [row_spec] * n_rowsum
    out_shape = out_shape + [jax.ShapeDtypeStruct((1, n), F32)] * n_rowsum
    n_in = len(args)
    n_mn = len(mn_extras)
    step = min(slab, tm) if slab else tm
    assert tm % step == 0

    def dot(a_ref, b_ref):
        a_val = a_ref[...] if a_pre is None else a_pre(a_ref[...])
        if not b_dm:
            return _bdot(a_val, b_ref[...], mode)
        if mode == "nn":
            parts = [_bdot(a_val, b_ref[dd], "nn") for dd in range(nb)]
            return parts[0] if nb == 1 else jnp.concatenate(parts, axis=1)
        out = None
        for dd in range(nb):
            part = _bdot(a_val[:, dd * cb:(dd + 1) * cb], b_ref[dd], "nt")
            out = part if out is None else out + part
        return out

    def finish(acc_src, extras, outs):
        sums = [None] * n_rowsum
        for r0 in range(0, tm, step):
            rs = slice(r0, r0 + step)
            acc_val = acc_src[rs, :]
            if epi is None:
                vals = (acc_val,)
            else:
                vals = epi(acc_val, *[e[rs, :] for e in extras[:n_mn]], *[e[...] for e in extras[n_mn:]])
            for o_ref, val in zip(outs[:n_out], vals[:n_out]):
                if out_dm:
                    for dd in range(no):
                        o_ref[dd, rs, :] = val[:, dd * co:(dd + 1) * co].astype(o_ref.dtype)
                else:
                    o_ref[rs, :] = val.astype(o_ref.dtype)
            for q in range(n_rowsum):
                sums[q] = vals[n_out + q] if sums[q] is None else sums[q] + vals[n_out + q]
        for q in range(n_rowsum):
            s_ref = outs[n_out + q]

            @pl.when(pl.program_id(1) == 0)
            def _():
                s_ref[...] = sums[q]

            @pl.when(pl.program_id(1) > 0)
            def _():
                s_ref[...] += sums[q]

    def body_one_step(*refs):
        finish(dot(refs[0], refs[1]), refs[2:2 + n_extra], refs[n_in:])

    def body(*refs):
        a_ref, b_ref = refs[0], refs[1]
        acc = refs[-1]
        kk = pl.program_id(2)

        @pl.when(kk == 0)
        def _():
            acc[...] = jnp.zeros_like(acc)

        acc[...] += dot(a_ref, b_ref)

        @pl.when(kk == nk - 1)
        def _():
            finish(acc, refs[2:2 + n_extra], refs[n_in:-1])

    res = pl.pallas_call(
        body_one_step if nk == 1 else body, name=name,
        grid=(n // tn, m // tm, nk),
        in_specs=in_specs, out_specs=out_specs, out_shape=out_shape,
        scratch_shapes=[] if nk == 1 else [pltpu.VMEM((tm, tn), F32)],
        compiler_params=pltpu.CompilerParams(
            dimension_semantics=("parallel", "arbitrary" if n_rowsum else "parallel", "arbitrary")),
    )(*args)
    return res[0] if n_out + n_rowsum == 1 else res


def _rowwise(fn, *, n_rows, ts, name, rows=(), prevs=(), nexts=(), vecs=(), row_outs=(), acc_outs=(), after=()):
    ts = min(ts, n_rows)
    assert n_rows % ts == 0
    nblk = n_rows // ts
    in_specs, args = [], []
    for arr, cb, w in rows:
        in_specs.append(pl.BlockSpec((ts, w), functools.partial(lambda i, cb: (i, cb), cb=cb)))
        args.append(arr)
    for arr, cb, w, halo in prevs:
        per = ts // halo
        in_specs.append(pl.BlockSpec(
            (halo, w), functools.partial(lambda i, cb, per: (jnp.maximum(i * per - 1, 0), cb), cb=cb, per=per)))
        args.append(arr)
    for arr, cb, w, halo in nexts:
        per = ts // halo
        last_blk = n_rows // halo - 1
        in_specs.append(pl.BlockSpec(
            (halo, w), functools.partial(lambda i, cb, per, lb: (jnp.minimum((i + 1) * per, lb), cb),
                                         cb=cb, per=per, lb=last_blk)))
        args.append(arr)
    for arr in vecs:
        in_specs.append(pl.BlockSpec(arr.shape, functools.partial(lambda i, nd: (0,) * nd, nd=arr.ndim)))
        args.append(arr)
    out_specs, out_shape = [], []
    for w, dt in row_outs:
        out_specs.append(pl.BlockSpec((ts, w), lambda i: (i, 0)))
        out_shape.append(jax.ShapeDtypeStruct((n_rows, w), dt))
    for shp in acc_outs:
        out_specs.append(pl.BlockSpec(shp, functools.partial(lambda i, nd: (0,) * nd, nd=len(shp))))
        out_shape.append(jax.ShapeDtypeStruct(shp, F32))
    n_used = len(args)
    n_tiles = n_used - len(vecs)
    in_specs += [_ANY_SPEC] * len(after)
    args += list(after)
    n_in, n_ro, n_acc = len(args), len(row_outs), len(acc_outs)

    def body(*refs):
        ins, ro, ac = refs[:n_used], refs[n_in:n_in + n_ro], refs[n_in + n_ro:]
        i = pl.program_id(0)
        rvals, avals = fn(i == 0, i == nblk - 1, *[r[...] for r in ins[:n_tiles]], *ins[n_tiles:])
        for r, val in zip(ro, rvals):
            r[...] = val.astype(r.dtype)
        if n_acc:
            @pl.when(i == 0)
            def _():
                for r in ac:
                    r[...] = jnp.zeros_like(r)

            for r, val in zip(ac, avals):
                r[...] += val

    res = pl.pallas_call(
        body, name=name, grid=(nblk,), in_specs=in_specs, out_specs=out_specs, out_shape=out_shape,
        compiler_params=pltpu.CompilerParams(dimension_semantics=("arbitrary",)),
    )(*args)
    return res


def _gated_out(o, z, onorm):
    return o * lax.rsqrt(jnp.mean(o * o, axis=-1, keepdims=True) + RMS_EPS) * onorm * _silu(z)


def _head_blocks(ref, heads, col0=0):
    return jnp.stack([ref[:, col0 + h * DN_HEAD_DIM:col0 + (h + 1) * DN_HEAD_DIM] for h in range(heads)])


def _split_heads(q_ref, k_ref, v_ref, gbv, heads):
    gcol = jnp.stack([gbv[:, heads + h:heads + h + 1] for h in range(heads)])
    bcol = jnp.stack([gbv[:, h:h + 1] for h in range(heads)])
    return _head_blocks(q_ref, heads), _head_blocks(k_ref, heads), _head_blocks(v_ref, heads), gcol, bcol


def _delta_fwd(q, k, v, gb, qkvz, onorm, heads):
    s, hd = q.shape
    n = s // DN_CHUNK
    blk = pl.BlockSpec((DN_CHUNK, hd), lambda c: (c, 0))
    gspec = pl.BlockSpec((DN_CHUNK, LANES), lambda c: (c, 0))

    def body(q_ref, k_ref, v_ref, gb_ref, z_ref, on_ref, og_ref, st_ref, tm_ref, state):
        @pl.when(pl.program_id(0) == 0)
        def _():
            state[...] = jnp.zeros_like(state)

        s0 = state[...]
        st_ref[0] = s0
        o, s1, tm = _delta_chunk(*_split_heads(q_ref, k_ref, v_ref, gb_ref[...], heads), s0)
        og = _gated_out(o, _head_blocks(z_ref, heads), on_ref[...])
        for h in range(heads):
            og_ref[:, h * DN_HEAD_DIM:(h + 1) * DN_HEAD_DIM] = og[h].astype(og_ref.dtype)
        state[...] = s1
        tm_ref[0] = tm

    return pl.pallas_call(
        body, name="dn_delta_fwd", grid=(n,),
        in_specs=[blk, blk, blk, gspec, pl.BlockSpec((DN_CHUNK, hd), lambda c: (c, 3)),
                  pl.BlockSpec(onorm.shape, lambda c: (0, 0))],
        out_specs=[blk, pl.BlockSpec((1, heads, DN_HEAD_DIM, DN_HEAD_DIM), lambda c: (c, 0, 0, 0)),
                   pl.BlockSpec((1, heads, DN_CHUNK, DN_CHUNK), lambda c: (c, 0, 0, 0))],
        out_shape=[jax.ShapeDtypeStruct((s, hd), BF16),
                   jax.ShapeDtypeStruct((n, heads, DN_HEAD_DIM, DN_HEAD_DIM), F32),
                   jax.ShapeDtypeStruct((n, heads, DN_CHUNK, DN_CHUNK), F32)],
        scratch_shapes=[pltpu.VMEM((heads, DN_HEAD_DIM, DN_HEAD_DIM), F32)],
        compiler_params=pltpu.CompilerParams(dimension_semantics=("arbitrary",)),
    )(q, k, v, gb, qkvz, onorm)


def _delta_bwd(q, k, v, gb, qkvz, onorm, states, tms, dog, heads):
    s, hd = q.shape
    n = s // DN_CHUNK
    blk = pl.BlockSpec((DN_CHUNK, hd), lambda c: (n - 1 - c, 0))
    gspec = pl.BlockSpec((DN_CHUNK, LANES), lambda c: (n - 1 - c, 0))
    sspec = pl.BlockSpec((1, heads, DN_HEAD_DIM, DN_HEAD_DIM), lambda c: (n - 1 - c, 0, 0, 0))
    tspec = pl.BlockSpec((1, heads, DN_CHUNK, DN_CHUNK), lambda c: (n - 1 - c, 0, 0, 0))
    nspec = pl.BlockSpec(onorm.shape, lambda c: (0, 0))

    def body(q_ref, k_ref, v_ref, gb_ref, z_ref, on_ref, st_ref, tm_ref, dog_ref,
             dq_ref, dk_ref, dv_ref, dgb_ref, dz_ref, don_ref, dstate):
        @pl.when(pl.program_id(0) == 0)
        def _():
            dstate[...] = jnp.zeros_like(dstate)
            don_ref[...] = jnp.zeros_like(don_ref)

        gbv = gb_ref[...]
        tm = tm_ref[0]

        def chunk(qh, kh, vh, gcol, bcol, s0, zh, on):
            o, s1, _ = _delta_chunk(qh, kh, vh, gcol, bcol, s0, tm)
            return _gated_out(o, zh, on), s1

        _, vjp = jax.vjp(chunk, *_split_heads(q_ref, k_ref, v_ref, gbv, heads), st_ref[0],
                         _head_blocks(z_ref, heads), on_ref[...])
        dq, dk, dv, dg, db, ds0, dz, don = vjp((_head_blocks(dog_ref, heads).astype(F32), dstate[...]))
        dstate[...] = ds0
        don_ref[...] += don
        lane = lax.broadcasted_iota(jnp.int32, gbv.shape, 1)
        dgb = jnp.zeros(gbv.shape, F32)
        for h in range(heads):
            sl = slice(h * DN_HEAD_DIM, (h + 1) * DN_HEAD_DIM)
            dq_ref[:, sl] = dq[h]
            dk_ref[:, sl] = dk[h]
            dv_ref[:, sl] = dv[h]
            dz_ref[:, sl] = dz[h]
            dgb = dgb + jnp.where(lane == h, db[h], 0.0) + jnp.where(lane == heads + h, dg[h], 0.0)
        dgb_ref[...] = dgb

    return pl.pallas_call(
        body, name="dn_delta_bwd", grid=(n,),
        in_specs=[blk, blk, blk, gspec, pl.BlockSpec((DN_CHUNK, hd), lambda c: (n - 1 - c, 3)), nspec,
                  sspec, tspec, blk],
        out_specs=[blk, blk, blk, gspec, blk, nspec],
        out_shape=[jax.ShapeDtypeStruct((s, hd), F32)] * 3 + [jax.ShapeDtypeStruct((s, LANES), F32),
                                                              jax.ShapeDtypeStruct((s, hd), F32),
                                                              jax.ShapeDtypeStruct(onorm.shape, F32)],
        scratch_shapes=[pltpu.VMEM((heads, DN_HEAD_DIM, DN_HEAD_DIM), F32)],
        compiler_params=pltpu.CompilerParams(dimension_semantics=("arbitrary",)),
    )(q, k, v, gb, qkvz, onorm, states, tms, dog)


def _dev_index(px, py, pc):
    return 4 * px + 2 * py + pc


def _all_gather(arrs, name):
    n = len(arrs)

    def body(*refs):
        xs, outs = refs[:n], refs[n:2 * n]
        send_sems, recv_sems, local_sems = refs[2 * n:]
        x, y, c = lax.axis_index("x"), lax.axis_index("y"), lax.axis_index("c")
        me, sibling = (x, y, c), (x, y, 1 - c)
        chips = [(1 - x, y), (x, 1 - y), (1 - x, 1 - y)]

        def copy(a, kk, block, to, src=None):
            dst = outs[a].at[_dev_index(*block)]
            return pltpu.make_async_remote_copy(
                src_ref=dst if src is None else src, dst_ref=dst,
                send_sem=send_sems.at[a * 7 + kk], recv_sem=recv_sems.at[a * 7 + kk],
                device_id=to, device_id_type=MESH_IDS)

        mine = [pltpu.make_async_copy(xs[a], outs[a].at[_dev_index(*me)], local_sems.at[a]) for a in range(n)]
        for cp in mine:
            cp.start()
        first = []
        for a in range(n):
            first.append(copy(a, 0, me, sibling, src=xs[a]))
            first += [copy(a, 1 + j, me, (*chip, c), src=xs[a]) for j, chip in enumerate(chips)]
        for cp in first:
            cp.start()
        passed = []
        for j, chip in enumerate(chips):
            for a in range(n):
                copy(a, 1 + j, (*chip, c), me).wait_recv()
                fwd = copy(a, 4 + j, (*chip, c), sibling)
                fwd.start()
                passed.append(fwd)
        for a in range(n):
            copy(a, 0, sibling, me).wait_recv()
        for j, chip in enumerate(chips):
            for a in range(n):
                copy(a, 4 + j, (*chip, 1 - c), me).wait_recv()
        for cp in first + passed:
            cp.wait_send()
        for cp in mine:
            cp.wait()

    hbm = pl.BlockSpec(memory_space=pltpu.HBM)
    res = pl.pallas_call(
        body, name=name,
        in_specs=[hbm] * n, out_specs=[hbm] * n,
        out_shape=[jax.ShapeDtypeStruct((N_DEV,) + a.shape, a.dtype) for a in arrs],
        scratch_shapes=[pltpu.SemaphoreType.DMA((7 * n,)), pltpu.SemaphoreType.DMA((7 * n,)),
                        pltpu.SemaphoreType.DMA((n,))],
    )(*arrs)
    return list(res)


_FLIPS = ((0, 0, 1), (1, 0, 0), (0, 1, 0), (1, 1, 0), (1, 0, 1), (0, 1, 1), (1, 1, 1))
_HBM_SPEC = pl.BlockSpec(memory_space=pltpu.HBM)
_SEM_SPEC = pl.BlockSpec(memory_space=pltpu.SEMAPHORE)
_ANY_SPEC = pl.BlockSpec(memory_space=pl.ANY)
_DATAFLOW = pltpu.SideEffectType.DATAFLOW_SIDE_EFFECTING
TOKEN_SHAPE = (8, LANES)


def _mesh_me():
    return lax.axis_index("x"), lax.axis_index("y"), lax.axis_index("c")


def _flipped(me, f):
    return tuple(1 - v if fl else v for v, fl in zip(me, f))


def _exchange_copies(xs, lands, send_sems, recv_sems, scatter, landed):
    me = _mesh_me()
    cps = []
    for kk, f in enumerate(_FLIPS):
        p = _flipped(me, f)
        for a in range(len(xs)):
            cps.append(pltpu.make_async_remote_copy(
                src_ref=xs[a].at[_dev_index(*p)] if scatter else xs[a],
                dst_ref=lands[a].at[_dev_index(*(p if landed else me))],
                send_sem=send_sems.at[a * 7 + kk], recv_sem=recv_sems.at[a * 7 + kk],
                device_id=p, device_id_type=MESH_IDS))
    return cps


def _exchange_start(srcs, lands, scatter, name, after=()):
    n = len(srcs)

    n_after = len(after)

    def body(*refs):
        xs, ls = refs[:n], refs[n:2 * n]
        send_sems, recv_sems = refs[2 * n + n_after], refs[2 * n + n_after + 1]
        token = refs[-1]
        for cp in _exchange_copies(xs, ls, send_sems, recv_sems, scatter, landed=False):
            cp.start()
        token[...] = jnp.zeros_like(token)

    operands = [pltpu.with_memory_space_constraint(a, pltpu.HBM) for a in list(srcs) + list(lands)]
    res = pl.pallas_call(
        body, name=name,
        in_specs=[_HBM_SPEC] * (2 * n) + [_ANY_SPEC] * len(after),
        out_specs=[_SEM_SPEC, _SEM_SPEC] + [_HBM_SPEC] * (2 * n) + [pl.BlockSpec(memory_space=pltpu.VMEM)],
        out_shape=[pltpu.SemaphoreType.DMA((7 * n,)), pltpu.SemaphoreType.DMA((7 * n,))]
        + [pltpu.HBM(a.shape, a.dtype) for a in operands] + [jax.ShapeDtypeStruct(TOKEN_SHAPE, F32)],
        input_output_aliases={i: 2 + i for i in range(2 * n)},
        compiler_params=pltpu.CompilerParams(has_side_effects=_DATAFLOW),
    )(*operands, *after)
    return (res[0], res[1], list(res[2:2 + n]), list(res[2 + n:2 + 2 * n]), scatter, name), res[-1]


def _exchange_wait(handle, after):
    send_sems, recv_sems, srcs, lands, scatter, name = handle
    n = len(srcs)
    n_after = len(after)

    def body(*refs):
        xs, ls = refs[:n], refs[n:2 * n]
        send_sems_ref, recv_sems_ref = refs[2 * n], refs[2 * n + 1]
        for cp in _exchange_copies(xs, ls, send_sems_ref, recv_sems_ref, scatter, landed=True):
            cp.wait_send()
            cp.wait_recv()

    res = pl.pallas_call(
        body, name=name + "_wait",
        in_specs=[_HBM_SPEC] * (2 * n) + [_SEM_SPEC, _SEM_SPEC] + [_ANY_SPEC] * n_after,
        out_specs=[_HBM_SPEC] * (2 * n),
        out_shape=[pltpu.HBM(a.shape, a.dtype) for a in srcs + lands],
        input_output_aliases={i: i for i in range(2 * n)},
        compiler_params=pltpu.CompilerParams(has_side_effects=_DATAFLOW),
    )(*srcs, *lands, send_sems, recv_sems, *after)
    return list(res[:n]), list(res[n:])


def _slot_sum(g, name, tr):
    _, r, c = g.shape
    tr = min(tr, r)
    assert r % tr == 0

    def body(g_ref, o_ref):
        acc = g_ref[0].astype(F32)
        for s in range(1, N_DEV):
            acc = acc + g_ref[s].astype(F32)
        o_ref[...] = acc

    return pl.pallas_call(
        body, name=name, grid=(r // tr,),
        in_specs=[pl.BlockSpec((N_DEV, tr, c), lambda i: (0, i, 0))],
        out_specs=pl.BlockSpec((tr, c), lambda i: (i, 0)),
        out_shape=jax.ShapeDtypeStruct((r, c), F32),
        compiler_params=pltpu.CompilerParams(dimension_semantics=("parallel",)),
    )(g)


def _adam_update(w, gg, m, v):
    c1 = 1.0 / (1.0 - ADAM_B1 ** ADAM_STEP)
    c2 = 1.0 / (1.0 - ADAM_B2 ** ADAM_STEP)
    nm = ADAM_B1 * m + (1.0 - ADAM_B1) * gg
    nv = ADAM_B2 * v + (1.0 - ADAM_B2) * (gg * gg)
    return -ADAM_LR * ((nm * c1) / (jnp.sqrt(nv * c2) + ADAM_EPS) + ADAM_WD * w), nm, nv


def _adamw_reduce(me, recvs, owns, w, m, v, name, tr=256):
    nl, r, c = w.shape
    assert len(recvs) == nl and len(owns) == nl
    tr = min(tr, r)
    if r % tr == 0:
        tc, nblk = c, r // tr
        at = lambda i: (i, 0)
    else:
        tr, tc = r, min(c, 4 * LANES)
        assert c % tc == 0
        nblk = c // tc
        at = lambda i: (0, i)

    def parked(li, l, i):
        return jnp.where(l < li, 0, jnp.where(l > li, nblk - 1, i))

    def recv_spec(li):
        return pl.BlockSpec((N_DEV, tr, tc), lambda l, i, me_ref: (0, *at(parked(li, l, i))))

    def own_spec(li):
        return pl.BlockSpec((None, tr, tc), lambda l, i, me_ref: (me_ref[0], *at(parked(li, l, i))))

    def body(me_ref, *refs):
        rrefs, orefs = refs[:nl], refs[nl:2 * nl]
        w_ref, m_ref, v_ref, g_ref, d_ref, nm_ref, nv_ref = refs[2 * nl:]
        l = pl.program_id(0)

        def of_layer(vals):
            out = vals[0]
            for li in range(1, nl):
                out = jnp.where(l == li, vals[li], out)
            return out

        own = of_layer([o[...].astype(F32) for o in orefs])
        gg = None
        for s in range(N_DEV):
            slot = jnp.where(me_ref[0] == s, own, of_layer([rr[s].astype(F32) for rr in rrefs]))
            gg = slot if gg is None else gg + slot
        g_ref[...] = gg
        d_ref[...], nm_ref[...], nv_ref[...] = _adam_update(w_ref[...], gg, m_ref[...], v_ref[...])

    spec = pl.BlockSpec((None, tr, tc), lambda l, i, me_ref: (l, *at(i)))
    return pl.pallas_call(
        body, name=name,
        grid_spec=pltpu.PrefetchScalarGridSpec(
            num_scalar_prefetch=1, grid=(nl, nblk),
            in_specs=[recv_spec(li) for li in range(nl)] + [own_spec(li) for li in range(nl)] + [spec] * 3,
            out_specs=[spec] * 4),
        out_shape=[jax.ShapeDtypeStruct((nl, r, c), F32)] * 4,
        compiler_params=pltpu.CompilerParams(dimension_semantics=("arbitrary", "arbitrary")),
    )(me, *recvs, *owns, w, m, v)


def _adamw(w, g, m, v, name, tr=256):
    r, c = w.shape
    tr = min(tr, r)
    assert r % tr == 0

    def body(w_ref, g_ref, m_ref, v_ref, d_ref, nm_ref, nv_ref):
        d_ref[...], nm_ref[...], nv_ref[...] = _adam_update(w_ref[...], g_ref[...], m_ref[...], v_ref[...])

    spec = pl.BlockSpec((tr, c), lambda i: (i, 0))
    return pl.pallas_call(
        body, name=name, grid=(r // tr,), in_specs=[spec] * 4, out_specs=[spec] * 3,
        out_shape=[jax.ShapeDtypeStruct((r, c), F32)] * 3,
        compiler_params=pltpu.CompilerParams(dimension_semantics=("parallel",)),
    )(w, g, m, v)


def _rms_fwd(x, g, name, ts=1024, after=()):
    s, d = x.shape

    def fn(first, last, xv, gv):
        return [_rms(xv, gv[...])], []

    return _rowwise(fn, n_rows=s, ts=ts, name=name, rows=[(x, 0, d)], vecs=[g], row_outs=[(d, BF16)],
                    after=after)[0]


def _rms_bwd(x, dn, dres, g, name, ts=256):
    s, d = x.shape

    def fn(first, last, xv, dnv, drv, gv):
        _, vjp = jax.vjp(_rms, xv, gv[...])
        dx, dg = vjp(dnv.astype(F32))
        return [drv + dx], [dg]

    return _rowwise(fn, n_rows=s, ts=ts, name=name, rows=[(x, 0, d), (dn, 0, d), (dres, 0, d)], vecs=[g],
                    row_outs=[(d, F32)], acc_outs=[(1, d)])


def _dn_pre_fwd(qkvz, ba, wconv, alog, dt, heads, ts=512):
    s = qkvz.shape[0]
    d3 = wconv.shape[1]
    d = d3 // 3

    def fn(first, last, xc, bav, xp, wv, av, dv):
        xext = jnp.concatenate([jnp.where(first, 0.0, xp), xc], axis=0)
        cv = _causal_conv(xext, wv, DN_CONV, DN_HALO - (DN_CONV - 1), xc.shape[0])
        return list(_dn_point(cv, bav, av[...], dv[...], heads)), []

    return _rowwise(fn, n_rows=s, ts=ts, name="dn_pre_fwd", rows=[(qkvz, 0, d3), (ba, 0, LANES)],
                    prevs=[(qkvz, 0, d3, DN_HALO)], vecs=[wconv, alog, dt],
                    row_outs=[(d, F32), (d, F32), (d, F32), (LANES, F32)])


def _dn_pre_bwd(qkvz, ba, wconv, alog, dt, dq, dk, dv, dgb, dz, heads, ts=256):
    s = qkvz.shape[0]
    d3 = wconv.shape[1]
    d = d3 // 3
    lead = DN_HALO - (DN_CONV - 1)

    def fn(first, last, xc, bac, dqc, dkc, dvc, dgbc, dzc, xp, xn, ban, dqn, dkn, dvn, dgbn, wv, av, dtv):
        n = xc.shape[0]
        ext = lambda cur, nxt: jnp.concatenate([cur, nxt], axis=0)
        live = lambda nxt: jnp.where(last, 0.0, nxt)
        xall = jnp.concatenate([jnp.where(first, 0.0, xp), xc, live(xn)], axis=0)
        cv = _causal_conv(xall, wv, DN_CONV, lead, n + DN_HALO)
        (_, _, _, gbv), vjp = jax.vjp(lambda c, b: _dn_point(c, b, av[...], dtv[...], heads), cv, ext(bac, ban))
        dc, dba = vjp((ext(dqc, live(dqn)), ext(dkc, live(dkn)), ext(dvc, live(dvn)), ext(dgbc, live(dgbn))))
        dx = None
        dw = []
        for j in range(DN_CONV):
            term = _shift_rows(dc, DN_CONV - 1 - j)[:n] * wv[j:j + 1, :]
            dx = term if dx is None else dx + term
            dw.append(_colsum(dc[:n] * _shift_rows(xall, lead + j)[:n]))
        dba = dba[:n]
        return ([jnp.concatenate([dx, dzc], axis=-1), dba],
                [_stack_rows(dw, DN_CONV), _colsum(dgbc * gbv[:n]), _colsum(dba)])

    return _rowwise(fn, n_rows=s, ts=ts, name="dn_pre_bwd",
                    rows=[(qkvz, 0, d3), (ba, 0, LANES), (dq, 0, d), (dk, 0, d), (dv, 0, d), (dgb, 0, LANES),
                          (dz, 0, d)],
                    prevs=[(qkvz, 0, d3, DN_HALO)],
                    nexts=[(qkvz, 0, d3, DN_HALO), (ba, 0, LANES, DN_HALO), (dq, 0, d, DN_HALO),
                           (dk, 0, d, DN_HALO), (dv, 0, d, DN_HALO), (dgb, 0, LANES, DN_HALO)],
                    vecs=[wconv, alog, dt],
                    row_outs=[(4 * d, BF16), (LANES, BF16)], acc_outs=[(DN_CONV, d3), (1, LANES), (1, LANES)])


def _cv_mid_fwd(u, wdw, bdw, lng, lnb, ts=512):
    s = u.shape[0]
    d = u.shape[1] // 2

    def fn(first, last, uc, up, wv, bv, gv, lbv):
        uext = jnp.concatenate([jnp.where(first, 0.0, up), uc], axis=0)
        glu = uext[:, :d] * _sigmoid(uext[:, d:])
        c = _causal_conv(glu, wv, CV_WIDTH, CV_HALO - (CV_WIDTH - 1), uc.shape[0]) + bv[...]
        return [c, _ln_silu(c, gv[...], lbv[...])], []

    return _rowwise(fn, n_rows=s, ts=ts, name="cv_mid_fwd", rows=[(u, 0, 2 * d)], prevs=[(u, 0, 2 * d, CV_HALO)],
                    vecs=[wdw, bdw, lng, lnb], row_outs=[(d, F32), (d, BF16)])


def _cv_mid_bwd2(dc, u, wdw, ts=512):
    s, d = dc.shape

    def fn(first, last, dcc, uc, up, dcn, wv):
        n = dcc.shape[0]
        dcext = jnp.concatenate([dcc, jnp.where(last, 0.0, dcn)], axis=0)
        uext = jnp.concatenate([jnp.where(first, 0.0, up), uc], axis=0)
        glu = uext[:, :d] * _sigmoid(uext[:, d:])
        dglu = None
        dw = []
        for j in range(CV_WIDTH):
            term = _shift_rows(dcext, CV_WIDTH - 1 - j)[:n] * wv[j:j + 1, :]
            dglu = term if dglu is None else dglu + term
            dw.append(_colsum(dcc * _shift_rows(glu, CV_HALO - (CV_WIDTH - 1) + j)[:n]))
        u1, sg = uc[:, :d], _sigmoid(uc[:, d:])
        du = jnp.concatenate([dglu * sg, dglu * u1 * sg * (1.0 - sg)], axis=-1)
        return [du], [_stack_rows(dw, CV_HALO), _colsum(du)]

    return _rowwise(fn, n_rows=s, ts=ts, name="cv_mid_bwd2", rows=[(dc, 0, d), (u, 0, 2 * d)],
                    prevs=[(u, 0, 2 * d, CV_HALO)], nexts=[(dc, 0, d, CV_HALO)], vecs=[wdw],
                    row_outs=[(2 * d, BF16)], acc_outs=[(CV_HALO, d), (1, 2 * d)])


def _attn_fwd(q, k, v, name, ts=2048):
    s, d = q.shape

    def fn(first, last, qv, kv, vv):
        return [_attn_tile(qv.astype(F32), kv[...].astype(F32), vv[...].astype(F32))], []

    return _rowwise(fn, n_rows=s, ts=ts, name=name, rows=[(q, 0, d)], vecs=[k, v], row_outs=[(d, BF16)])[0]


def _attn_bwd(q, k, v, do, name, ts=2048):
    s, d = q.shape
    m = k.shape[0]

    def fn(first, last, qv, dov, kv, vv):
        _, vjp = jax.vjp(_attn_tile, qv.astype(F32), kv[...].astype(F32), vv[...].astype(F32))
        dq, dk, dv = vjp(dov.astype(F32))
        return [dq], [dk, dv]

    return _rowwise(fn, n_rows=s, ts=ts, name=name, rows=[(q, 0, d), (do, 0, d)], vecs=[k, v],
                    row_outs=[(d, BF16)], acc_outs=[(m, d), (m, d)])


def _pad_lanes(a, off=0):
    r, n = a.shape
    return jnp.pad(a, ((0, 0), (off, LANES - off - n)))


def _local_step(x, mem, tgt, w, fetch=None, emit=None, first_after=()):
    s, d = x.shape
    heads = d // DN_HEAD_DIM
    g = {}
    if fetch is None:
        fetch = lambda group, after: None
    if emit is None:
        emit = lambda group, grads: ()

    def add_res_rms(acc, res, gain):
        h = res + acc
        return h, _rms(h, gain)

    def rms_bwd_epi(acc, hx, dres, gain):
        _, vjp = jax.vjp(_rms, hx, gain)
        dx, dg = vjp(acc)
        return dres + dx, dg

    w_int = w["dn_w_in"][0]
    assert w_int.shape[0] == 4 * d + 2 * heads
    w_bat = jnp.pad(w_int[4 * d:], ((0, LANES - 2 * heads), (0, 0)))
    dn_norm = w["dn_norm"]
    alog = _pad_lanes(w["dn_a_log"], heads)
    dtb = _pad_lanes(w["dn_dt_bias"], heads)
    wconv = w["dn_w_conv"][0]
    n0 = _rms_fwd(x, dn_norm, "dn_rms", after=first_after)
    qkvz = _matmul(n0, w_int, "nt", [F32], name="dn_in_proj", b_rows=4 * d, tn=MM_DEEP)
    ba = _matmul(n0, w_bat, "nt", [F32], name="dn_in_proj_ba")
    q, k, v, gb = _dn_pre_fwd(qkvz, ba, wconv, alog, dtb, heads)
    og, states, tms = _delta_fwd(q, k, v, gb, qkvz, w["dn_out_norm"], heads)
    fetch(1, [og])
    h1, nq0 = _matmul(og, w["dn_w_out"][0], "nn", [F32, BF16], name="dn_out_proj", epi=add_res_rms,
                      mn_extras=[x], row_extras=[w["xa_norm"][0:1]], slab=EPI_SLAB)

    def xattn_fwd(h, nq, layer, next_gain):
        qx = _matmul(nq, w["xa_w_q"][layer], "nn", [BF16], name=f"xa{layer}_q")
        mn = _rms_fwd(mem, w["xa_mem_norm"][layer:layer + 1], f"xa{layer}_mem_rms")
        kv = _matmul(mn, w["xa_w_kv"][layer], "nn", [BF16], name=f"xa{layer}_kv")
        kx, vx = kv[:, :d], kv[:, d:]
        ox = _attn_fwd(qx, kx, vx, f"xa{layer}_attn")
        hn, nn = _matmul(ox, w["xa_w_o"][layer], "nn", [F32, BF16], name=f"xa{layer}_o", epi=add_res_rms,
                         mn_extras=[h], row_extras=[next_gain], slab=EPI_SLAB)
        return hn, nn, (h, nq, qx, mn, kx, vx, ox)

    def sq_relu(t):
        r = jnp.maximum(t.astype(F32), 0.0)
        return r * r

    def loss_epi(acc, res, target, gain):
        def cols(hh, gg):
            e = _rms(hh, gg) - target
            return _colsum(e * e) * (0.5 / d)

        per_col, vjp = jax.vjp(cols, res + acc, gain)
        dhx, dgain = vjp(jnp.ones_like(per_col))
        return dhx, dgain, per_col

    def mlp_fwd(h, nm, layer, next_gain):
        u = _matmul(nm, w["mlp_w_up"][layer], "nn", [BF16], name=f"mlp{layer}_up", tn=MM_DEEP)
        if next_gain is None:
            hn, *nn = _matmul(u, w["mlp_w_down"][layer], "nn", [F32], name=f"mlp{layer}_down_loss",
                              epi=loss_epi, mn_extras=[h, tgt], row_extras=[w["final_norm"].reshape(1, d)],
                              n_rowsum=2, tk=MM_DEEP, slab=EPI_SLAB, a_pre=sq_relu)
        else:
            hn, nn = _matmul(u, w["mlp_w_down"][layer], "nn", [F32, BF16], name=f"mlp{layer}_down",
                             epi=add_res_rms, mn_extras=[h], row_extras=[next_gain], tk=MM_DEEP, slab=EPI_SLAB,
                             a_pre=sq_relu)
        return hn, nn, (h, nm, u)

    h2, nm0, xa0 = xattn_fwd(h1, nq0, 0, w["mlp_norm"][0:1])
    fetch(2, [h2])
    h3, n1, mlp0 = mlp_fwd(h2, nm0, 0, w["cv_norm"])

    u_cv = _matmul(n1, w["cv_w_pw1"][0], "nn", [F32], name="cv_pw1", epi=lambda acc, b: (acc + b,),
                   row_extras=[w["cv_b_pw1"]])
    wdw = jnp.pad(w["cv_w_dw"][0], ((0, CV_HALO - CV_WIDTH), (0, 0)))
    c_cv, s_cv = _cv_mid_fwd(u_cv, wdw, w["cv_b_dw"], w["cv_ln_g"], w["cv_ln_b"])
    h4, nq1 = _matmul(s_cv, w["cv_w_pw2"][0], "nn", [F32, BF16], name="cv_pw2",
                      epi=lambda acc, res, b, gain: add_res_rms(acc + b, res, gain), mn_extras=[h3],
                      row_extras=[w["cv_b_pw2"], w["xa_norm"][1:2]], slab=EPI_SLAB)
    fetch(3, [h4])
    h5, nm1, xa1 = xattn_fwd(h4, nq1, 1, w["mlp_norm"][1:2])
    fetch(4, [h5])
    dh, (g_fn, loss_cols), mlp1 = mlp_fwd(h5, nm1, 1, None)
    g["final_norm"] = g_fn.reshape(d)
    loss = jnp.sum(loss_cols, axis=1, keepdims=True)

    def mlp_bwd(dh, layer, saved, after=()):
        h, nm, u = saved
        du = _matmul(dh, w["mlp_w_down"][layer], "nt", [BF16], name=f"mlp{layer}_down_dx", after=after,
                     epi=lambda acc, uu: (acc * 2.0 * jnp.maximum(uu.astype(F32), 0.0),), mn_extras=[u])
        gdown = _matmul(u, dh, "tn", [BF16], name=f"mlp{layer}_down_dw", tm=MM_DEEP, a_pre=sq_relu)
        dhn, gn = _matmul(du, w["mlp_w_up"][layer], "nt", [F32], name=f"mlp{layer}_up_dx", epi=rms_bwd_epi,
                          mn_extras=[h, dh], row_extras=[w["mlp_norm"][layer:layer + 1]], n_rowsum=1,
                          slab=EPI_SLAB, tk=MM_DEEP)
        gup = _matmul(nm, du, "tn", [BF16], name=f"mlp{layer}_up_dw", out_dm=True, tk=MM_DEEP)
        return dhn, gup, gdown, gn

    def xattn_bwd(dh, layer, saved):
        h, nq, qx, mn, kx, vx, ox = saved
        dox = _matmul(dh, w["xa_w_o"][layer], "nt", [BF16], name=f"xa{layer}_o_dx")
        go = _matmul(ox, dh, "tn", [BF16], name=f"xa{layer}_o_dw", tk=MM_DEEP)
        dqx, dkx, dvx = _attn_bwd(qx, kx, vx, dox, f"xa{layer}_attn_bwd")

        def epi(acc, hx, dres, gain):
            dhx, dg = rms_bwd_epi(acc, hx, dres, gain)
            return dhx, dg, _colsum(dhx)

        dhn, gn, dh_cols = _matmul(dqx, w["xa_w_q"][layer], "nt", [F32], name=f"xa{layer}_q_dx", epi=epi,
                                   mn_extras=[h, dh], row_extras=[w["xa_norm"][layer:layer + 1]], n_rowsum=2,
                                   slab=EPI_SLAB)
        gq = _matmul(nq, dqx, "tn", [BF16], name=f"xa{layer}_q_dw", tk=MM_DEEP)
        dkv = jnp.concatenate([dkx, dvx], axis=-1)
        gkv = _matmul(mn, dkv, "tn", [BF16], name=f"xa{layer}_kv_dw", out_dm=True)
        dmn = _matmul(dkv, w["xa_w_kv"][layer], "nt", [F32], name=f"xa{layer}_kv_dx", tk=MM_DEEP)
        _, gmem = _rms_bwd(mem, dmn, dmn, w["xa_mem_norm"][layer:layer + 1], f"xa{layer}_mem_rms_bwd")
        return dhn, gq, gkv, go, gn, gmem, dh_cols

    dh, gup1, gdown1, gmn1 = mlp_bwd(dh, 1, mlp1)
    dh, gq1, gkv1, go1, gxn1, gmem1, g_b2 = xattn_bwd(dh, 1, xa1)
    g.update(mlp_w_up=[None, gup1], mlp_w_down=[None, gdown1], xa_w_q=[None, gq1], xa_w_kv=[None, gkv1],
             xa_w_o=[None, go1])
    tok = emit(3, g)

    def ln_bwd_epi(acc, cx, gain, bias):
        _, vjp = jax.vjp(_ln_silu, cx, gain, bias)
        dc, dg, db = vjp(acc)
        return dc, dg, db, _colsum(dc)

    dc_cv, g_lng, g_lnb, g_bdw = _matmul(dh, w["cv_w_pw2"][0], "nt", [F32], name="cv_pw2_dx", after=tok,
                                        epi=ln_bwd_epi, mn_extras=[c_cv],
                                        row_extras=[w["cv_ln_g"], w["cv_ln_b"]], n_rowsum=3, slab=EPI_SLAB)
    g["cv_w_pw2"] = [_matmul(s_cv, dh, "tn", [BF16], name="cv_pw2_dw", tk=MM_DEEP)]
    du_cv, g_wdw, g_b1 = _cv_mid_bwd2(dc_cv, u_cv, wdw)
    g["cv_w_pw1"] = [_matmul(n1, du_cv, "tn", [BF16], name="cv_pw1_dw", out_dm=True, tk=MM_DEEP)]
    dh, g_cvn = _matmul(du_cv, w["cv_w_pw1"][0], "nt", [F32], name="cv_pw1_dx", epi=rms_bwd_epi,
                        mn_extras=[h3, dh], row_extras=[w["cv_norm"]], n_rowsum=1, slab=EPI_SLAB, tk=MM_DEEP)
    g.update(cv_ln_g=g_lng, cv_ln_b=g_lnb, cv_b_dw=g_bdw, cv_b_pw2=g_b2, cv_b_pw1=g_b1, cv_norm=g_cvn,
             cv_w_dw=g_wdw[:CV_WIDTH][None])

    tok = emit(2, g)
    dh, gup0, gdown0, gmn0 = mlp_bwd(dh, 0, mlp0, after=tok)
    dh, gq0, gkv0, go0, gxn0, gmem0, _ = xattn_bwd(dh, 0, xa0)
    g["mlp_w_up"][0] = gup0
    g["mlp_w_down"][0] = gdown0
    g["mlp_norm"] = jnp.concatenate([gmn0, gmn1], axis=0)
    g["xa_w_q"][0] = gq0
    g["xa_w_kv"][0] = gkv0
    g["xa_w_o"][0] = go0
    g["xa_norm"] = jnp.concatenate([gxn0, gxn1], axis=0)
    g["xa_mem_norm"] = jnp.concatenate([gmem0, gmem1], axis=0)
    g["dn_w_out"] = [_matmul(og, dh, "tn", [BF16], name="dn_out_proj_dw", tk=MM_DEEP)]
    tok = emit(1, g)
    dog = _matmul(dh, w["dn_w_out"][0], "nt", [BF16], name="dn_out_proj_dx", after=tok)
    dq, dk, dv, dgb, dz, g_on = _delta_bwd(q, k, v, gb, qkvz, w["dn_out_norm"], states, tms, dog, heads)
    dqkvz, dba, g_wconv, g_alog, g_dt = _dn_pre_bwd(qkvz, ba, wconv, alog, dtb, dq, dk, dv, dgb, dz, heads)
    g_qkvzt = _matmul(dqkvz, n0, "tn", [BF16], name="dn_in_proj_dw", tk=MM_DEEP)
    g_bat = _matmul(dba, n0, "tn", [BF16], name="dn_in_proj_ba_dw", tk=MM_DEEP)
    g["dn_w_in"] = [jnp.concatenate([g_qkvzt, g_bat[:2 * heads]], axis=0)]
    g["dn_w_conv"] = g_wconv[None]
    tok = emit(0, g)
    dn0a = _matmul(dba, w_bat, "nn", [F32], name="dn_in_proj_ba_dx", after=tok)
    grad_x, g_dnn = _matmul(dqkvz, w_int, "nn", [F32], name="dn_in_proj_dx", b_rows=4 * d,
                            epi=lambda acc, part, hx, dres, gain: rms_bwd_epi(acc + part, hx, dres, gain),
                            mn_extras=[dn0a, x, dh], row_extras=[dn_norm], n_rowsum=1, slab=EPI_SLAB)
    g.update(dn_norm=g_dnn, dn_out_norm=g_on,
             dn_a_log=g_alog[:, heads:2 * heads], dn_dt_bias=g_dt[:, heads:2 * heads])
    return loss, grad_x, g


def _round_up(n, m):
    return (n + m - 1) // m * m


def _pack_rows(parts, cols, row_mult):
    lead = parts[0].shape[:-1]
    flat, offs, off = [], [], 0
    for p in parts:
        n = _round_up(p.shape[-1], cols)
        flat.append(jnp.pad(p, [(0, 0)] * len(lead) + [(0, n - p.shape[-1])]))
        offs.append(off)
        off += n
    total = _round_up(off, cols * row_mult)
    if total > off:
        flat.append(jnp.zeros(lead + (total - off,), parts[0].dtype))
    return jnp.concatenate(flat, axis=-1).reshape(lead + (total // cols, cols)), offs


def _unpack(packed, offs, shapes):
    lead = packed.shape[:-2]
    flat = packed.reshape(lead + (-1,))
    out = []
    for off, shp in zip(offs, shapes):
        n = 1
        for v in shp:
            n *= v
        out.append(flat[..., off:off + n].reshape(lead + tuple(shp)))
    return out


def kernel(x, mem, dn_norm, dn_w_in, dn_w_conv, dn_a_log, dn_dt_bias, dn_out_norm, dn_w_out, cv_norm, cv_w_pw1, cv_b_pw1, cv_w_dw, cv_b_dw, cv_ln_g, cv_ln_b, cv_w_pw2, cv_b_pw2, xa_norm, xa_mem_norm, xa_w_q, xa_w_kv, xa_w_o, mlp_norm, mlp_w_up, mlp_w_down, final_norm, loss_target, m_dn_norm, m_dn_w_in, m_dn_w_conv, m_dn_a_log, m_dn_dt_bias, m_dn_out_norm, m_dn_w_out, m_cv_norm, m_cv_w_pw1, m_cv_b_pw1, m_cv_w_dw, m_cv_b_dw, m_cv_ln_g, m_cv_ln_b, m_cv_w_pw2, m_cv_b_pw2, m_xa_norm, m_xa_mem_norm, m_xa_w_q, m_xa_w_kv, m_xa_w_o, m_mlp_norm, m_mlp_w_up, m_mlp_w_down, m_final_norm, v_dn_norm, v_dn_w_in, v_dn_w_conv, v_dn_a_log, v_dn_dt_bias, v_dn_out_norm, v_dn_w_out, v_cv_norm, v_cv_w_pw1, v_cv_b_pw1, v_cv_w_dw, v_cv_b_dw, v_cv_ln_g, v_cv_ln_b, v_cv_w_pw2, v_cv_b_pw2, v_xa_norm, v_xa_mem_norm, v_xa_w_q, v_xa_w_kv, v_xa_w_o, v_mlp_norm, v_mlp_w_up, v_mlp_w_down, v_final_norm):
    wsh = dict(dn_norm=dn_norm, dn_w_in=dn_w_in, dn_w_conv=dn_w_conv, dn_a_log=dn_a_log, dn_dt_bias=dn_dt_bias, dn_out_norm=dn_out_norm, dn_w_out=dn_w_out, cv_norm=cv_norm, cv_w_pw1=cv_w_pw1, cv_b_pw1=cv_b_pw1, cv_w_dw=cv_w_dw, cv_b_dw=cv_b_dw, cv_ln_g=cv_ln_g, cv_ln_b=cv_ln_b, cv_w_pw2=cv_w_pw2, cv_b_pw2=cv_b_pw2, xa_norm=xa_norm, xa_mem_norm=xa_mem_norm, xa_w_q=xa_w_q, xa_w_kv=xa_w_kv, xa_w_o=xa_w_o, mlp_norm=mlp_norm, mlp_w_up=mlp_w_up, mlp_w_down=mlp_w_down, final_norm=final_norm)
    msh = dict(dn_norm=m_dn_norm, dn_w_in=m_dn_w_in, dn_w_conv=m_dn_w_conv, dn_a_log=m_dn_a_log, dn_dt_bias=m_dn_dt_bias, dn_out_norm=m_dn_out_norm, dn_w_out=m_dn_w_out, cv_norm=m_cv_norm, cv_w_pw1=m_cv_w_pw1, cv_b_pw1=m_cv_b_pw1, cv_w_dw=m_cv_w_dw, cv_b_dw=m_cv_b_dw, cv_ln_g=m_cv_ln_g, cv_ln_b=m_cv_ln_b, cv_w_pw2=m_cv_w_pw2, cv_b_pw2=m_cv_b_pw2, xa_norm=m_xa_norm, xa_mem_norm=m_xa_mem_norm, xa_w_q=m_xa_w_q, xa_w_kv=m_xa_w_kv, xa_w_o=m_xa_w_o, mlp_norm=m_mlp_norm, mlp_w_up=m_mlp_w_up, mlp_w_down=m_mlp_w_down, final_norm=m_final_norm)
    vsh = dict(dn_norm=v_dn_norm, dn_w_in=v_dn_w_in, dn_w_conv=v_dn_w_conv, dn_a_log=v_dn_a_log, dn_dt_bias=v_dn_dt_bias, dn_out_norm=v_dn_out_norm, dn_w_out=v_dn_w_out, cv_norm=v_cv_norm, cv_w_pw1=v_cv_w_pw1, cv_b_pw1=v_cv_b_pw1, cv_w_dw=v_cv_w_dw, cv_b_dw=v_cv_b_dw, cv_ln_g=v_cv_ln_g, cv_ln_b=v_cv_ln_b, cv_w_pw2=v_cv_w_pw2, cv_b_pw2=v_cv_b_pw2, xa_norm=v_xa_norm, xa_mem_norm=v_xa_mem_norm, xa_w_q=v_xa_w_q, xa_w_kv=v_xa_w_kv, xa_w_o=v_xa_w_o, mlp_norm=v_mlp_norm, mlp_w_up=v_mlp_w_up, mlp_w_down=v_mlp_w_down, final_norm=v_final_norm)

    big_axis = dict(BIG)
    for src in (wsh, msh, vsh):
        src["dn_w_in"] = jnp.swapaxes(src["dn_w_in"], 1, 2)
    big_axis["dn_w_in"] = 1

    small_pack, small_offs = _pack_rows([wsh[nm].reshape(-1) for nm in SMALL_SH], LANES, 8)
    w = {nm: [None] * wsh[nm].shape[0] for nm in big_axis}

    def put_weights(group, gathered):
        for (nm, layer), gth in zip(group, gathered):
            if big_axis[nm] == 1:
                w[nm][layer] = gth.reshape(N_DEV * gth.shape[1], gth.shape[2])
            else:
                w[nm][layer] = gth

    first = _all_gather([wsh[nm][layer].astype(BF16) for nm, layer in GATHER_GROUPS[0]] + [small_pack],
                        "weights_all_gather_0")
    put_weights(GATHER_GROUPS[0], first)
    me = _dev_index(*_mesh_me())
    gather_handles, tokens = {}, []
    for gi in range(1, len(GATHER_GROUPS)):
        shards = [wsh[nm][layer].astype(BF16) for nm, layer in GATHER_GROUPS[gi]]
        lands = [lax.dynamic_update_slice(lax.empty((N_DEV,) + s.shape, s.dtype), s[None], (me, 0, 0))
                 for s in shards]
        gather_handles[gi], tok = _exchange_start(shards, lands, False, f"weights_gather_{gi}",
                                                  after=[first[-1]] + tokens)
        tokens.append(tok)
    for nm, gth in zip(SMALL_SH, _unpack(first[-1], small_offs, [wsh[nm].shape for nm in SMALL_SH])):
        w[nm] = jnp.moveaxis(gth, 0, -2).reshape(gth.shape[1:-1] + (N_DEV * gth.shape[-1],))
    for nm in REPL:
        w[nm] = wsh[nm]

    def fetch(gi, after):
        put_weights(GATHER_GROUPS[gi], _exchange_wait(gather_handles[gi], after)[1])

    scatter_handles = {}

    def emit(gi, g):
        blocks = []
        for nm, layer in SCATTER_GROUPS[gi]:
            gw = g[nm][layer]
            if big_axis[nm] == 1:
                gw = gw.reshape(N_DEV, gw.shape[0] // N_DEV, gw.shape[1])
            blocks.append(gw)
        if gi == 0:
            gsmall_pack, _ = _pack_rows(
                [jnp.moveaxis(g[nm].reshape(g[nm].shape[:-1] + (N_DEV, -1)), -2, 0).reshape(N_DEV, -1)
                 for nm in SMALL_SH], LANES, 8)
            blocks.append(gsmall_pack)
        lands = [lax.empty(b.shape, b.dtype) for b in blocks]
        scatter_handles[gi], tok = _exchange_start(blocks, lands, True, f"grads_scatter_{gi}")
        return [tok]

    loss_part, grad_x, g = _local_step(x[0], mem[0], loss_target[0], w, fetch, emit, tokens)

    recv = {nm: [None] * wsh[nm].shape[0] for nm in big_axis}
    sent = {nm: [None] * wsh[nm].shape[0] for nm in big_axis}
    gsh, delta, new_m, new_v = {}, {}, {}, {}
    after = [grad_x]
    done = set()
    me_arr = me.astype(jnp.int32).reshape(1)
    def small_adamw(names, name):
        packs = []
        for src in (wsh, gsh, msh, vsh):
            pk, offs = _pack_rows([src[nm].reshape(-1) for nm in names], LANES, 8)
            packs.append(pk)
        outs = _adamw(*packs, name)
        for dst, pk in zip((delta, new_m, new_v), outs):
            for nm, val in zip(names, _unpack(pk, offs, [wsh[nm].shape for nm in names])):
                dst[nm] = val
        return outs[0]

    for gi in reversed(range(len(SCATTER_GROUPS))):
        if gi == 0:
            repl_pack, repl_offs = _pack_rows([g[nm].reshape(-1) for nm in REPL] + [loss_part[:, :1].reshape(-1)],
                                              LANES, 8)
            (repl_all,) = _all_gather([repl_pack], "repl_grads_all_gather")
            repl_red = _slot_sum(repl_all, "repl_grads_sum", SLOT_SUM_ROWS)
            *repl_vals, loss_sum = _unpack(repl_red, repl_offs, [wsh[nm].shape for nm in REPL] + [(1,)])
            for nm, val in zip(REPL, repl_vals):
                gsh[nm] = val
            after = after + [small_adamw(list(REPL), "adamw_repl")]
        sources, landed = _exchange_wait(scatter_handles[gi], after)
        for (nm, layer), src, r in zip(SCATTER_GROUPS[gi], sources, landed):
            sent[nm][layer], recv[nm][layer] = src, r
        if gi == 0:
            slot = lax.broadcasted_iota(jnp.int32, landed[-1].shape, 0)
            rsmall = jnp.where(slot == me, sources[-1], landed[-1])
        for nm in big_axis:
            if nm not in done and all(r is not None for r in recv[nm]):
                gsh[nm], delta[nm], new_m[nm], new_v[nm] = _adamw_reduce(
                    me_arr, recv[nm], sent[nm], wsh[nm], msh[nm], vsh[nm], f"adamw_{nm}")
                done.add(nm)
                after = [delta[nm]]
    gsmall_red = _slot_sum(rsmall, "grads_small_sum", SLOT_SUM_ROWS)
    for nm, val in zip(SMALL_SH, _unpack(gsmall_red, small_offs, [wsh[nm].shape for nm in SMALL_SH])):
        gsh[nm] = val
    small_adamw(list(SMALL_SH), "adamw_small")
    for dst in (gsh, delta, new_m, new_v):
        dst["dn_w_in"] = jnp.swapaxes(dst["dn_w_in"], 1, 2)
    return (loss_sum.reshape(()), grad_x[None], *[gsh[nm] for nm in WEIGHTS], *[delta[nm] for nm in WEIGHTS],
            *[new_m[nm] for nm in WEIGHTS], *[new_v[nm] for nm in WEIGHTS])
```

```python
import functools

import jax
import jax.numpy as jnp
from jax import lax
from jax.experimental import pallas as pl
from jax.experimental.pallas import tpu as pltpu

F32 = jnp.float32
BF16 = jnp.bfloat16
MESH_IDS = pl.DeviceIdType.MESH

N_DEV = 8
LANES = 128
RMS_EPS = 1e-6
LN_EPS = 1e-5
DN_HEAD_DIM = 128
DN_CONV = 4
DN_CHUNK = 64
CV_WIDTH = 31
XA_HEADS = 4
MM_TILE = 1024
MM_DEEP = 2048
EPI_SLAB = 256
SLOT_SUM_ROWS = 512
DN_HALO = 8
CV_HALO = 32

ADAM_LR = 0.001
ADAM_B1 = 0.9
ADAM_B2 = 0.999
ADAM_EPS = 1e-08
ADAM_WD = 0.01
ADAM_STEP = 10

BIG = (("dn_w_in", 2), ("dn_w_out", 1), ("cv_w_pw1", 2), ("cv_w_pw2", 1), ("xa_w_q", 1), ("xa_w_kv", 2),
       ("xa_w_o", 1), ("mlp_w_up", 2), ("mlp_w_down", 1))
_LAYER_GROUP = ("xa_w_q", "xa_w_o", "mlp_w_down", "xa_w_kv", "mlp_w_up")
GATHER_GROUPS = (
    (("dn_w_in", 0),),
    (("dn_w_out", 0),) + tuple((nm, 0) for nm in _LAYER_GROUP),
    (("cv_w_pw2", 0), ("cv_w_pw1", 0)),
    (("xa_w_q", 1), ("xa_w_o", 1), ("xa_w_kv", 1)),
    (("mlp_w_down", 1), ("mlp_w_up", 1)),
)
SCATTER_GROUPS = (
    (("dn_w_in", 0),),
    (("dn_w_out", 0),) + tuple((nm, 0) for nm in _LAYER_GROUP),
    (("cv_w_pw2", 0), ("cv_w_pw1", 0)),
    tuple((nm, 1) for nm in _LAYER_GROUP),
)
SMALL_SH = ("cv_norm", "cv_b_pw1", "cv_b_dw", "cv_ln_g", "cv_ln_b", "cv_b_pw2", "cv_w_dw", "dn_w_conv")
REPL = ("dn_norm", "dn_a_log", "dn_dt_bias", "dn_out_norm", "xa_norm", "xa_mem_norm", "mlp_norm", "final_norm")
WEIGHTS = ("dn_norm", "dn_w_in", "dn_w_conv", "dn_a_log", "dn_dt_bias", "dn_out_norm", "dn_w_out", "cv_norm",
           "cv_w_pw1", "cv_b_pw1", "cv_w_dw", "cv_b_dw", "cv_ln_g", "cv_ln_b", "cv_w_pw2", "cv_b_pw2", "xa_norm",
           "xa_mem_norm", "xa_w_q", "xa_w_kv", "xa_w_o", "mlp_norm", "mlp_w_up", "mlp_w_down", "final_norm")


def _dot_dims(mode, batched):
    o = 1 if batched else 0
    contract = {"nn": ((1 + o,), (o,)), "nt": ((1 + o,), (1 + o,)), "tn": ((o,), (o,))}[mode]
    return (contract, (((0,), (0,)) if batched else ((), ())))


def _bdot(a, b, mode):
    return lax.dot_general(a.astype(BF16), b.astype(BF16), _dot_dims(mode, a.ndim == 3),
                           preferred_element_type=F32)


@functools.partial(jax.custom_vjp, nondiff_argnums=(2,))
def _mm(a, b, mode):
    return _bdot(a, b, mode)


def _mm_fwd(a, b, mode):
    return _bdot(a, b, mode), (a, b)


def _mm_bwd(mode, res, ct):
    a, b = res
    if mode == "nn":
        da, db = _bdot(ct, b, "nt"), _bdot(a, ct, "tn")
    elif mode == "nt":
        da, db = _bdot(ct, b, "nn"), _bdot(ct, a, "tn")
    else:
        da, db = _bdot(b, ct, "nt"), _bdot(a, ct, "nn")
    return da.astype(a.dtype), db.astype(b.dtype)


_mm.defvjp(_mm_fwd, _mm_bwd)


def _sigmoid(x):
    return 0.5 * (jnp.tanh(0.5 * x) + 1.0)


def _silu(x):
    return x * _sigmoid(x)


def _softplus(x):
    return jnp.maximum(x, 0.0) + jnp.log(1.0 + jnp.exp(-jnp.abs(x)))


def _rms(x, g):
    r = lax.rsqrt(jnp.mean(x * x, axis=-1, keepdims=True) + RMS_EPS)
    return x * r * g


def _shift_rows(x, off):
    if off == 0:
        return x
    return pltpu.roll(x, x.shape[0] - off, 0)


def _series_dot(a, b, mode):
    return _bdot(a, b, mode)


def _chunk_masks(c):
    ii = lax.broadcasted_iota(jnp.int32, (c, c), 0)
    jj = lax.broadcasted_iota(jnp.int32, (c, c), 1)
    return (ii == jj).astype(F32), ii >= jj, ii > jj


def _neumann_inverse(lm):
    n = lm.shape[-1]
    t = -lm
    p = lm
    size = 2
    while size < n:
        size *= 2
        p = _series_dot(p, p, "nn")
        t = t + p + _series_dot(t, p, "nn")
    return t


def _apply_inverse(tm, rhs, mode):
    return rhs + _series_dot(tm, rhs, mode)


@jax.custom_vjp
def _unit_lower_solve(lm, rhs, tm):
    return _apply_inverse(tm, rhs, "nn")


def _uls_fwd(lm, rhs, tm):
    sol = _apply_inverse(tm, rhs, "nn")
    return sol, (tm, sol)


def _uls_bwd(res, ct):
    tm, sol = res
    d_rhs = _apply_inverse(tm, ct, "tn")
    return -_bdot(d_rhs, sol, "nt"), d_rhs, jnp.zeros_like(tm)


_unit_lower_solve.defvjp(_uls_fwd, _uls_bwd)


def _delta_chunk(q, k, v, gcol, bcol, s0, tm=None):
    c = q.shape[1]
    eye, causal, strict = _chunk_masks(c)
    grow = jnp.sum(eye * gcol, axis=1, keepdims=True)
    gc = jnp.sum(jnp.where(causal, grow, 0.0), axis=2, keepdims=True)
    gc_row = jnp.sum(eye * gc, axis=1, keepdims=True)
    decay = jnp.exp(jnp.where(causal, gc - gc_row, -jnp.inf))
    kb = k * bcol
    on_k = _mm(jnp.concatenate([kb, q], axis=1), k, "nt")
    lm = jnp.where(strict, on_k[:, :c] * decay, 0.0)
    attn = on_k[:, c:] * decay
    if tm is None:
        tm = _neumann_inverse(lax.stop_gradient(lm))
    egc = jnp.exp(gc)
    rhs = jnp.concatenate([v * bcol, kb * egc], axis=-1)
    sol = _unit_lower_solve(lm, rhs, tm)
    dv_ = v.shape[-1]
    u, w = sol[..., :dv_], sol[..., dv_:]
    gl = jnp.sum(grow, axis=2, keepdims=True)
    kd = k * jnp.exp(gl - gc)
    on_s = _mm(jnp.concatenate([w, q * egc], axis=1), s0, "nn")
    v_new = u - on_s[:, :c]
    o = on_s[:, c:] + _mm(attn, v_new, "nn")
    s1 = s0 * jnp.exp(gl) + _mm(kd, v_new, "tn")
    return o, s1, tm


def _dn_point(cv, ba, alog, dt, heads):
    a = _silu(cv)
    d = cv.shape[1] // 3
    qs, ks = [], []
    for h in range(heads):
        qh = a[:, h * DN_HEAD_DIM:(h + 1) * DN_HEAD_DIM]
        qs.append(qh * lax.rsqrt(jnp.sum(qh * qh, axis=-1, keepdims=True) + 1e-6) * (DN_HEAD_DIM ** -0.5))
        kh = a[:, d + h * DN_HEAD_DIM:d + (h + 1) * DN_HEAD_DIM]
        ks.append(kh * lax.rsqrt(jnp.sum(kh * kh, axis=-1, keepdims=True) + 1e-6))
    q = jnp.concatenate(qs, axis=-1)
    k = jnp.concatenate(ks, axis=-1)
    v = a[:, 2 * d:]
    lane = lax.broadcasted_iota(jnp.int32, ba.shape, 1)
    beta = _sigmoid(ba)
    g = -jnp.exp(alog) * _softplus(ba + dt)
    gb = jnp.where(lane < heads, beta, jnp.where(lane < 2 * heads, g, 0.0))
    return q, k, v, gb


def _attn_tile(q, k, v):
    hd = q.shape[1] // XA_HEADS
    outs = []
    for h in range(XA_HEADS):
        sl = slice(h * hd, (h + 1) * hd)
        s = _mm(q[:, sl], k[:, sl], "nt") * (hd ** -0.5)
        m = lax.stop_gradient(jnp.max(s, axis=-1, keepdims=True))
        e = jnp.exp(s - m)
        p = e / jnp.sum(e, axis=-1, keepdims=True)
        outs.append(_mm(p, v[:, sl], "nn"))
    return jnp.concatenate(outs, axis=-1)


def _ln_silu(c, g, b):
    mu = jnp.mean(c, axis=-1, keepdims=True)
    xc = c - mu
    y = xc * lax.rsqrt(jnp.mean(xc * xc, axis=-1, keepdims=True) + LN_EPS)
    return _silu(y * g + b)


def _causal_conv(xext, w, width, lead, ts):
    acc = None
    for j in range(width):
        term = _shift_rows(xext, lead + j)[:ts] * w[j:j + 1, :]
        acc = term if acc is None else acc + term
    return acc


def _colsum(x):
    return jnp.sum(x, axis=0, keepdims=True)


def _stack_rows(rows, n_rows):
    c = rows[0].shape[1]
    ridx = lax.broadcasted_iota(jnp.int32, (n_rows, c), 0)
    out = jnp.zeros((n_rows, c), F32)
    for j, r in enumerate(rows):
        out = out + jnp.where(ridx == j, r, 0.0)
    return out


def _matmul(a, b, mode, out_dtypes, *, name, epi=None, mn_extras=(), row_extras=(), out_dm=False, after=(),
            n_rowsum=0, slab=0, b_rows=None, a_pre=None, tm=MM_TILE, tn=MM_TILE, tk=MM_TILE):
    b_dm = b.ndim == 3
    b_shape = (b.shape[1], N_DEV * b.shape[2]) if b_dm else b.shape
    if b_rows is not None:
        assert not b_dm and b_rows <= b.shape[0]
        b_shape = (b_rows, b.shape[1])
    if mode == "nn":
        (m, k), (k2, n) = a.shape, b_shape
    elif mode == "nt":
        (m, k), (n, k2) = a.shape, b_shape
    else:
        (k, m), (k2, n) = a.shape, b_shape
    assert k == k2, (a.shape, b.shape, mode)
    tm, tn, tk = min(tm, m), min(tn, n), min(tk, k)
    cb, nb = 0, 1
    if b_dm:
        assert mode in ("nn", "nt")
        cb = b.shape[2]
        nb = max(1, (tn if mode == "nn" else tk) // cb)
        if mode == "nn":
            tn = nb * cb
        else:
            tk = nb * cb
    co, no = 0, 1
    if out_dm:
        co = n // N_DEV
        no = max(1, tn // co)
        tn = no * co
    assert m % tm == 0 and n % tn == 0 and k % tk == 0, (m, n, k, tm, tn, tk)
    nk = k // tk
    if mode == "tn":
        a_spec = pl.BlockSpec((tk, tm), lambda j, i, kk: (kk, i))
    else:
        a_spec = pl.BlockSpec((tm, tk), lambda j, i, kk: (i, kk))
    if b_dm:
        b_spec = (pl.BlockSpec((nb, tn, cb), lambda j, i, kk: (kk, j, 0)) if mode == "nt"
                  else pl.BlockSpec((nb, tk, cb), lambda j, i, kk: (j, kk, 0)))
    else:
        b_spec = (pl.BlockSpec((tn, tk), lambda j, i, kk: (j, kk)) if mode == "nt"
                  else pl.BlockSpec((tk, tn), lambda j, i, kk: (kk, j)))
    mn_spec = pl.BlockSpec((tm, tn), lambda j, i, kk: (i, j))
    row_spec = pl.BlockSpec((1, tn), lambda j, i, kk: (0, j))
    n_extra = len(mn_extras) + len(row_extras)
    n_out = len(out_dtypes)
    in_specs = ([a_spec, b_spec] + [mn_spec] * len(mn_extras) + [row_spec] * len(row_extras)
                + [_ANY_SPEC] * len(after))
    args = [a, b, *mn_extras, *row_extras, *after]
    if out_dm:
        out_specs = [pl.BlockSpec((no, tm, co), lambda j, i, kk: (j, i, 0))] * n_out
        out_shape = [jax.ShapeDtypeStruct((N_DEV, m, co), dt) for dt in out_dtypes]
    else:
        out_specs = [mn_spec] * n_out
        out_shape = [jax.ShapeDtypeStruct((m, n), dt) for dt in out_dtypes]
    out_specs = out_specs + [row_spec] * n_rowsum
    out_shape = out_shape + [jax.ShapeDtypeStruct((1, n), F32)] * n_rowsum
    n_in = len(args)
    n_mn = len(mn_extras)
    step = min(slab, tm) if slab else tm
    assert tm % step == 0

    def dot(a_ref, b_ref):
        a_val = a_ref[...] if a_pre is None else a_pre(a_ref[...])
        if not b_dm:
            return _bdot(a_val, b_ref[...], mode)
        if mode == "nn":
            parts = [_bdot(a_val, b_ref[dd], "nn") for dd in range(nb)]
            return parts[0] if nb == 1 else jnp.concatenate(parts, axis=1)
        out = None
        for dd in range(nb):
            part = _bdot(a_val[:, dd * cb:(dd + 1) * cb], b_ref[dd], "nt")
            out = part if out is None else out + part
        return out

    def finish(acc_src, extras, outs):
        sums = [None] * n_rowsum
        for r0 in range(0, tm, step):
            rs = slice(r0, r0 + step)
            acc_val = acc_src[rs, :]
            if epi is None:
                vals = (acc_val,)
            else:
                vals = epi(acc_val, *[e[rs, :] for e in extras[:n_mn]], *[e[...] for e in extras[n_mn:]])
            for o_ref, val in zip(outs[:n_out], vals[:n_out]):
                if out_dm:
                    for dd in range(no):
                        o_ref[dd, rs, :] = val[:, dd * co:(dd + 1) * co].astype(o_ref.dtype)
                else:
                    o_ref[rs, :] = val.astype(o_ref.dtype)
            for q in range(n_rowsum):
                sums[q] = vals[n_out + q] if sums[q] is None else sums[q] + vals[n_out + q]
        for q in range(n_rowsum):
            s_ref = outs[n_out + q]

            @pl.when(pl.program_id(1) == 0)
            def _():
                s_ref[...] = sums[q]

            @pl.when(pl.program_id(1) > 0)
            def _():
                s_ref[...] += sums[q]

    def body_one_step(*refs):
        finish(dot(refs[0], refs[1]), refs[2:2 + n_extra], refs[n_in:])

    def body(*refs):
        a_ref, b_ref = refs[0], refs[1]
        acc = refs[-1]
        kk = pl.program_id(2)

        @pl.when(kk == 0)
        def _():
            acc[...] = jnp.zeros_like(acc)

        acc[...] += dot(a_ref, b_ref)

        @pl.when(kk == nk - 1)
        def _():
            finish(acc, refs[2:2 + n_extra], refs[n_in:-1])

    res = pl.pallas_call(
        body_one_step if nk == 1 else body, name=name,
        grid=(n // tn, m // tm, nk),
        in_specs=in_specs, out_specs=out_specs, out_shape=out_shape,
        scratch_shapes=[] if nk == 1 else [pltpu.VMEM((tm, tn), F32)],
        compiler_params=pltpu.CompilerParams(
            dimension_semantics=("parallel", "arbitrary" if n_rowsum else "parallel", "arbitrary")),
    )(*args)
    return res[0] if n_out + n_rowsum == 1 else res


def _rowwise(fn, *, n_rows, ts, name, rows=(), prevs=(), nexts=(), vecs=(), row_outs=(), acc_outs=(), after=()):
    ts = min(ts, n_rows)
    assert n_rows % ts == 0
    nblk = n_rows // ts
    in_specs, args = [], []
    for arr, cb, w in rows:
        in_specs.append(pl.BlockSpec((ts, w), functools.partial(lambda i, cb: (i, cb), cb=cb)))
        args.append(arr)
    for arr, cb, w, halo in prevs:
        per = ts // halo
        in_specs.append(pl.BlockSpec(
            (halo, w), functools.partial(lambda i, cb, per: (jnp.maximum(i * per - 1, 0), cb), cb=cb, per=per)))
        args.append(arr)
    for arr, cb, w, halo in nexts:
        per = ts // halo
        last_blk = n_rows // halo - 1
        in_specs.append(pl.BlockSpec(
            (halo, w), functools.partial(lambda i, cb, per, lb: (jnp.minimum((i + 1) * per, lb), cb),
                                         cb=cb, per=per, lb=last_blk)))
        args.append(arr)
    for arr in vecs:
        in_specs.append(pl.BlockSpec(arr.shape, functools.partial(lambda i, nd: (0,) * nd, nd=arr.ndim)))
        args.append(arr)
    out_specs, out_shape = [], []
    for w, dt in row_outs:
        out_specs.append(pl.BlockSpec((ts, w), lambda i: (i, 0)))
        out_shape.append(jax.ShapeDtypeStruct((n_rows, w), dt))
    for shp in acc_outs:
        out_specs.append(pl.BlockSpec(shp, functools.partial(lambda i, nd: (0,) * nd, nd=len(shp))))
        out_shape.append(jax.ShapeDtypeStruct(shp, F32))
    n_used = len(args)
    n_tiles = n_used - len(vecs)
    in_specs += [_ANY_SPEC] * len(after)
    args += list(after)
    n_in, n_ro, n_acc = len(args), len(row_outs), len(acc_outs)

    def body(*refs):
        ins, ro, ac = refs[:n_used], refs[n_in:n_in + n_ro], refs[n_in + n_ro:]
        i = pl.program_id(0)
        rvals, avals = fn(i == 0, i == nblk - 1, *[r[...] for r in ins[:n_tiles]], *ins[n_tiles:])
        for r, val in zip(ro, rvals):
            r[...] = val.astype(r.dtype)
        if n_acc:
            @pl.when(i == 0)
            def _():
                for r in ac:
                    r[...] = jnp.zeros_like(r)

            for r, val in zip(ac, avals):
                r[...] += val

    res = pl.pallas_call(
        body, name=name, grid=(nblk,), in_specs=in_specs, out_specs=out_specs, out_shape=out_shape,
        compiler_params=pltpu.CompilerParams(dimension_semantics=("arbitrary",)),
    )(*args)
    return res


def _gated_out(o, z, onorm):
    return o * lax.rsqrt(jnp.mean(o * o, axis=-1, keepdims=True) + RMS_EPS) * onorm * _silu(z)


def _head_blocks(ref, heads, col0=0):
    return jnp.stack([ref[:, col0 + h * DN_HEAD_DIM:col0 + (h + 1) * DN_HEAD_DIM] for h in range(heads)])


def _split_heads(q_ref, k_ref, v_ref, gbv, heads):
    gcol = jnp.stack([gbv[:, heads + h:heads + h + 1] for h in range(heads)])
    bcol = jnp.stack([gbv[:, h:h + 1] for h in range(heads)])
    return _head_blocks(q_ref, heads), _head_blocks(k_ref, heads), _head_blocks(v_ref, heads), gcol, bcol


def _delta_fwd(q, k, v, gb, qkvz, onorm, heads):
    s, hd = q.shape
    n = s // DN_CHUNK
    blk = pl.BlockSpec((DN_CHUNK, hd), lambda c: (c, 0))
    gspec = pl.BlockSpec((DN_CHUNK, LANES), lambda c: (c, 0))

    def body(q_ref, k_ref, v_ref, gb_ref, z_ref, on_ref, og_ref, st_ref, tm_ref, state):
        @pl.when(pl.program_id(0) == 0)
        def _():
            state[...] = jnp.zeros_like(state)

        s0 = state[...]
        st_ref[0] = s0
        o, s1, tm = _delta_chunk(*_split_heads(q_ref, k_ref, v_ref, gb_ref[...], heads), s0)
        og = _gated_out(o, _head_blocks(z_ref, heads), on_ref[...])
        for h in range(heads):
            og_ref[:, h * DN_HEAD_DIM:(h + 1) * DN_HEAD_DIM] = og[h].astype(og_ref.dtype)
        state[...] = s1
        tm_ref[0] = tm

    return pl.pallas_call(
        body, name="dn_delta_fwd", grid=(n,),
        in_specs=[blk, blk, blk, gspec, pl.BlockSpec((DN_CHUNK, hd), lambda c: (c, 3)),
                  pl.BlockSpec(onorm.shape, lambda c: (0, 0))],
        out_specs=[blk, pl.BlockSpec((1, heads, DN_HEAD_DIM, DN_HEAD_DIM), lambda c: (c, 0, 0, 0)),
                   pl.BlockSpec((1, heads, DN_CHUNK, DN_CHUNK), lambda c: (c, 0, 0, 0))],
        out_shape=[jax.ShapeDtypeStruct((s, hd), BF16),
                   jax.ShapeDtypeStruct((n, heads, DN_HEAD_DIM, DN_HEAD_DIM), F32),
                   jax.ShapeDtypeStruct((n, heads, DN_CHUNK, DN_CHUNK), F32)],
        scratch_shapes=[pltpu.VMEM((heads, DN_HEAD_DIM, DN_HEAD_DIM), F32)],
        compiler_params=pltpu.CompilerParams(dimension_semantics=("arbitrary",)),
    )(q, k, v, gb, qkvz, onorm)


def _delta_bwd(q, k, v, gb, qkvz, onorm, states, tms, dog, heads):
    s, hd = q.shape
    n = s // DN_CHUNK
    blk = pl.BlockSpec((DN_CHUNK, hd), lambda c: (n - 1 - c, 0))
    gspec = pl.BlockSpec((DN_CHUNK, LANES), lambda c: (n - 1 - c, 0))
    sspec = pl.BlockSpec((1, heads, DN_HEAD_DIM, DN_HEAD_DIM), lambda c: (n - 1 - c, 0, 0, 0))
    tspec = pl.BlockSpec((1, heads, DN_CHUNK, DN_CHUNK), lambda c: (n - 1 - c, 0, 0, 0))
    nspec = pl.BlockSpec(onorm.shape, lambda c: (0, 0))

    def body(q_ref, k_ref, v_ref, gb_ref, z_ref, on_ref, st_ref, tm_ref, dog_ref,
             dq_ref, dk_ref, dv_ref, dgb_ref, dz_ref, don_ref, dstate):
        @pl.when(pl.program_id(0) == 0)
        def _():
            dstate[...] = jnp.zeros_like(dstate)
            don_ref[...] = jnp.zeros_like(don_ref)

        gbv = gb_ref[...]
        tm = tm_ref[0]

        def chunk(qh, kh, vh, gcol, bcol, s0, zh, on):
            o, s1, _ = _delta_chunk(qh, kh, vh, gcol, bcol, s0, tm)
            return _gated_out(o, zh, on), s1

        _, vjp = jax.vjp(chunk, *_split_heads(q_ref, k_ref, v_ref, gbv, heads), st_ref[0],
                         _head_blocks(z_ref, heads), on_ref[...])
        dq, dk, dv, dg, db, ds0, dz, don = vjp((_head_blocks(dog_ref, heads).astype(F32), dstate[...]))
        dstate[...] = ds0
        don_ref[...] += don
        lane = lax.broadcasted_iota(jnp.int32, gbv.shape, 1)
        dgb = jnp.zeros(gbv.shape, F32)
        for h in range(heads):
            sl = slice(h * DN_HEAD_DIM, (h + 1) * DN_HEAD_DIM)
            dq_ref[:, sl] = dq[h]
            dk_ref[:, sl] = dk[h]
            dv_ref[:, sl] = dv[h]
            dz_ref[:, sl] = dz[h]
            dgb = dgb + jnp.where(lane == h, db[h], 0.0) + jnp.where(lane == heads + h, dg[h], 0.0)
        dgb_ref[...] = dgb

    return pl.pallas_call(
        body, name="dn_delta_bwd", grid=(n,),
        in_specs=[blk, blk, blk, gspec, pl.BlockSpec((DN_CHUNK, hd), lambda c: (n - 1 - c, 3)), nspec,
                  sspec, tspec, blk],
        out_specs=[blk, blk, blk, gspec, blk, nspec],
        out_shape=[jax.ShapeDtypeStruct((s, hd), F32)] * 3 + [jax.ShapeDtypeStruct((s, LANES), F32),
                                                              jax.ShapeDtypeStruct((s, hd), F32),
                                                              jax.ShapeDtypeStruct(onorm.shape, F32)],
        scratch_shapes=[pltpu.VMEM((heads, DN_HEAD_DIM, DN_HEAD_DIM), F32)],
        compiler_params=pltpu.CompilerParams(dimension_semantics=("arbitrary",)),
    )(q, k, v, gb, qkvz, onorm, states, tms, dog)


def _dev_index(px, py, pc):
    return 4 * px + 2 * py + pc


def _all_gather(arrs, name):
    n = len(arrs)

    def body(*refs):
        xs, outs = refs[:n], refs[n:2 * n]
        send_sems, recv_sems, local_sems = refs[2 * n:]
        x, y, c = lax.axis_index("x"), lax.axis_index("y"), lax.axis_index("c")
        me, sibling = (x, y, c), (x, y, 1 - c)
        chips = [(1 - x, y), (x, 1 - y), (1 - x, 1 - y)]

        def copy(a, kk, block, to, src=None):
            dst = outs[a].at[_dev_index(*block)]
            return pltpu.make_async_remote_copy(
                src_ref=dst if src is None else src, dst_ref=dst,
                send_sem=send_sems.at[a * 7 + kk], recv_sem=recv_sems.at[a * 7 + kk],
                device_id=to, device_id_type=MESH_IDS)

        mine = [pltpu.make_async_copy(xs[a], outs[a].at[_dev_index(*me)], local_sems.at[a]) for a in range(n)]
        for cp in mine:
            cp.start()
        first = []
        for a in range(n):
            first.append(copy(a, 0, me, sibling, src=xs[a]))
            first += [copy(a, 1 + j, me, (*chip, c), src=xs[a]) for j, chip in enumerate(chips)]
        for cp in first:
            cp.start()
        passed = []
        for j, chip in enumerate(chips):
            for a in range(n):
                copy(a, 1 + j, (*chip, c), me).wait_recv()
                fwd = copy(a, 4 + j, (*chip, c), sibling)
                fwd.start()
                passed.append(fwd)
        for a in range(n):
            copy(a, 0, sibling, me).wait_recv()
        for j, chip in enumerate(chips):
            for a in range(n):
                copy(a, 4 + j, (*chip, 1 - c), me).wait_recv()
        for cp in first + passed:
            cp.wait_send()
        for cp in mine:
            cp.wait()

    hbm = pl.BlockSpec(memory_space=pltpu.HBM)
    res = pl.pallas_call(
        body, name=name,
        in_specs=[hbm] * n, out_specs=[hbm] * n,
        out_shape=[jax.ShapeDtypeStruct((N_DEV,) + a.shape, a.dtype) for a in arrs],
        scratch_shapes=[pltpu.SemaphoreType.DMA((7 * n,)), pltpu.SemaphoreType.DMA((7 * n,)),
                        pltpu.SemaphoreType.DMA((n,))],
    )(*arrs)
    return list(res)


_FLIPS = ((0, 0, 1), (1, 0, 0), (0, 1, 0), (1, 1, 0), (1, 0, 1), (0, 1, 1), (1, 1, 1))
_HBM_SPEC = pl.BlockSpec(memory_space=pltpu.HBM)
_SEM_SPEC = pl.BlockSpec(memory_space=pltpu.SEMAPHORE)
_ANY_SPEC = pl.BlockSpec(memory_space=pl.ANY)
_DATAFLOW = pltpu.SideEffectType.DATAFLOW_SIDE_EFFECTING
TOKEN_SHAPE = (8, LANES)


def _mesh_me():
    return lax.axis_index("x"), lax.axis_index("y"), lax.axis_index("c")


def _flipped(me, f):
    return tuple(1 - v if fl else v for v, fl in zip(me, f))


def _exchange_copies(xs, lands, send_sems, recv_sems, scatter, landed):
    me = _mesh_me()
    cps = []
    for kk, f in enumerate(_FLIPS):
        p = _flipped(me, f)
        for a in range(len(xs)):
            cps.append(pltpu.make_async_remote_copy(
                src_ref=xs[a].at[_dev_index(*p)] if scatter else xs[a],
                dst_ref=lands[a].at[_dev_index(*(p if landed else me))],
                send_sem=send_sems.at[a * 7 + kk], recv_sem=recv_sems.at[a * 7 + kk],
                device_id=p, device_id_type=MESH_IDS))
    return cps


def _exchange_start(srcs, lands, scatter, name, after=()):
    n = len(srcs)

    n_after = len(after)

    def body(*refs):
        xs, ls = refs[:n], refs[n:2 * n]
        send_sems, recv_sems = refs[2 * n + n_after], refs[2 * n + n_after + 1]
        token = refs[-1]
        for cp in _exchange_copies(xs, ls, send_sems, recv_sems, scatter, landed=False):
            cp.start()
        token[...] = jnp.zeros_like(token)

    operands = [pltpu.with_memory_space_constraint(a, pltpu.HBM) for a in list(srcs) + list(lands)]
    res = pl.pallas_call(
        body, name=name,
        in_specs=[_HBM_SPEC] * (2 * n) + [_ANY_SPEC] * len(after),
        out_specs=[_SEM_SPEC, _SEM_SPEC] + [_HBM_SPEC] * (2 * n) + [pl.BlockSpec(memory_space=pltpu.VMEM)],
        out_shape=[pltpu.SemaphoreType.DMA((7 * n,)), pltpu.SemaphoreType.DMA((7 * n,))]
        + [pltpu.HBM(a.shape, a.dtype) for a in operands] + [jax.ShapeDtypeStruct(TOKEN_SHAPE, F32)],
        input_output_aliases={i: 2 + i for i in range(2 * n)},
        compiler_params=pltpu.CompilerParams(has_side_effects=_DATAFLOW),
    )(*operands, *after)
    return (res[0], res[1], list(res[2:2 + n]), list(res[2 + n:2 + 2 * n]), scatter, name), res[-1]


def _exchange_wait(handle, after):
    send_sems, recv_sems, srcs, lands, scatter, name = handle
    n = len(srcs)
    n_after = len(after)

    def body(*refs):
        xs, ls = refs[:n], refs[n:2 * n]
        send_sems_ref, recv_sems_ref = refs[2 * n], refs[2 * n + 1]
        for cp in _exchange_copies(xs, ls, send_sems_ref, recv_sems_ref, scatter, landed=True):
            cp.wait_send()
            cp.wait_recv()

    res = pl.pallas_call(
        body, name=name + "_wait",
        in_specs=[_HBM_SPEC] * (2 * n) + [_SEM_SPEC, _SEM_SPEC] + [_ANY_SPEC] * n_after,
        out_specs=[_HBM_SPEC] * (2 * n),
        out_shape=[pltpu.HBM(a.shape, a.dtype) for a in srcs + lands],
        input_output_aliases={i: i for i in range(2 * n)},
        compiler_params=pltpu.CompilerParams(has_side_effects=_DATAFLOW),
    )(*srcs, *lands, send_sems, recv_sems, *after)
    return list(res[:n]), list(res[n:])


def _slot_sum(g, name, tr):
    _, r, c = g.shape
    tr = min(tr, r)
    assert r % tr == 0

    def body(g_ref, o_ref):
        acc = g_ref[0].astype(F32)
        for s in range(1, N_DEV):
            acc = acc + g_ref[s].astype(F32)
        o_ref[...] = acc

    return pl.pallas_call(
        body, name=name, grid=(r // tr,),
        in_specs=[pl.BlockSpec((N_DEV, tr, c), lambda i: (0, i, 0))],
        out_specs=pl.BlockSpec((tr, c), lambda i: (i, 0)),
        out_shape=jax.ShapeDtypeStruct((r, c), F32),
        compiler_params=pltpu.CompilerParams(dimension_semantics=("parallel",)),
    )(g)


def _adam_update(w, gg, m, v):
    c1 = 1.0 / (1.0 - ADAM_B1 ** ADAM_STEP)
    c2 = 1.0 / (1.0 - ADAM_B2 ** ADAM_STEP)
    nm = ADAM_B1 * m + (1.0 - ADAM_B1) * gg
    nv = ADAM_B2 * v + (1.0 - ADAM_B2) * (gg * gg)
    return -ADAM_LR * ((nm * c1) / (jnp.sqrt(nv * c2) + ADAM_EPS) + ADAM_WD * w), nm, nv


def _adamw_reduce(me, recvs, owns, w, m, v, name, tr=256):
    nl, r, c = w.shape
    assert len(recvs) == nl and len(owns) == nl
    tr = min(tr, r)
    if r % tr == 0:
        tc, nblk = c, r // tr
        at = lambda i: (i, 0)
    else:
        tr, tc = r, min(c, 4 * LANES)
        assert c % tc == 0
        nblk = c // tc
        at = lambda i: (0, i)

    def parked(li, l, i):
        return jnp.where(l < li, 0, jnp.where(l > li, nblk - 1, i))

    def recv_spec(li):
        return pl.BlockSpec((N_DEV, tr, tc), lambda l, i, me_ref: (0, *at(parked(li, l, i))))

    def own_spec(li):
        return pl.BlockSpec((None, tr, tc), lambda l, i, me_ref: (me_ref[0], *at(parked(li, l, i))))

    def body(me_ref, *refs):
        rrefs, orefs = refs[:nl], refs[nl:2 * nl]
        w_ref, m_ref, v_ref, g_ref, d_ref, nm_ref, nv_ref = refs[2 * nl:]
        l = pl.program_id(0)

        def of_layer(vals):
            out = vals[0]
            for li in range(1, nl):
                out = jnp.where(l == li, vals[li], out)
            return out

        own = of_layer([o[...].astype(F32) for o in orefs])
        gg = None
        for s in range(N_DEV):
            slot = jnp.where(me_ref[0] == s, own, of_layer([rr[s].astype(F32) for rr in rrefs]))
            gg = slot if gg is None else gg + slot
        g_ref[...] = gg
        d_ref[...], nm_ref[...], nv_ref[...] = _adam_update(w_ref[...], gg, m_ref[...], v_ref[...])

    spec = pl.BlockSpec((None, tr, tc), lambda l, i, me_ref: (l, *at(i)))
    return pl.pallas_call(
        body, name=name,
        grid_spec=pltpu.PrefetchScalarGridSpec(
            num_scalar_prefetch=1, grid=(nl, nblk),
            in_specs=[recv_spec(li) for li in range(nl)] + [own_spec(li) for li in range(nl)] + [spec] * 3,
            out_specs=[spec] * 4),
        out_shape=[jax.ShapeDtypeStruct((nl, r, c), F32)] * 4,
        compiler_params=pltpu.CompilerParams(dimension_semantics=("arbitrary", "arbitrary")),
    )(me, *recvs, *owns, w, m, v)


def _adamw(w, g, m, v, name, tr=256):
    r, c = w.shape
    tr = min(tr, r)
    assert r % tr == 0

    def body(w_ref, g_ref, m_ref, v_ref, d_ref, nm_ref, nv_ref):
        d_ref[...], nm_ref[...], nv_ref[...] = _adam_update(w_ref[...], g_ref[...], m_ref[...], v_ref[...])

    spec = pl.BlockSpec((tr, c), lambda i: (i, 0))
    return pl.pallas_call(
        body, name=name, grid=(r // tr,), in_specs=[spec] * 4, out_specs=[spec] * 3,
        out_shape=[jax.ShapeDtypeStruct((r, c), F32)] * 3,
        compiler_params=pltpu.CompilerParams(dimension_semantics=("parallel",)),
    )(w, g, m, v)


def _rms_fwd(x, g, name, ts=1024, after=()):
    s, d = x.shape

    def fn(first, last, xv, gv):
        return [_rms(xv, gv[...])], []

    return _rowwise(fn, n_rows=s, ts=ts, name=name, rows=[(x, 0, d)], vecs=[g], row_outs=[(d, BF16)],
                    after=after)[0]


def _rms_bwd(x, dn, dres, g, name, ts=256):
    s, d = x.shape

    def fn(first, last, xv, dnv, drv, gv):
        _, vjp = jax.vjp(_rms, xv, gv[...])
        dx, dg = vjp(dnv.astype(F32))
        return [drv + dx], [dg]

    return _rowwise(fn, n_rows=s, ts=ts, name=name, rows=[(x, 0, d), (dn, 0, d), (dres, 0, d)], vecs=[g],
                    row_outs=[(d, F32)], acc_outs=[(1, d)])


def _dn_pre_fwd(qkvz, ba, wconv, alog, dt, heads, ts=512):
    s = qkvz.shape[0]
    d3 = wconv.shape[1]
    d = d3 // 3

    def fn(first, last, xc, bav, xp, wv, av, dv):
        xext = jnp.concatenate([jnp.where(first, 0.0, xp), xc], axis=0)
        cv = _causal_conv(xext, wv, DN_CONV, DN_HALO - (DN_CONV - 1), xc.shape[0])
        return list(_dn_point(cv, bav, av[...], dv[...], heads)), []

    return _rowwise(fn, n_rows=s, ts=ts, name="dn_pre_fwd", rows=[(qkvz, 0, d3), (ba, 0, LANES)],
                    prevs=[(qkvz, 0, d3, DN_HALO)], vecs=[wconv, alog, dt],
                    row_outs=[(d, F32), (d, F32), (d, F32), (LANES, F32)])


def _dn_pre_bwd(qkvz, ba, wconv, alog, dt, dq, dk, dv, dgb, dz, heads, ts=256):
    s = qkvz.shape[0]
    d3 = wconv.shape[1]
    d = d3 // 3
    lead = DN_HALO - (DN_CONV - 1)

    def fn(first, last, xc, bac, dqc, dkc, dvc, dgbc, dzc, xp, xn, ban, dqn, dkn, dvn, dgbn, wv, av, dtv):
        n = xc.shape[0]
        ext = lambda cur, nxt: jnp.concatenate([cur, nxt], axis=0)
        live = lambda nxt: jnp.where(last, 0.0, nxt)
        xall = jnp.concatenate([jnp.where(first, 0.0, xp), xc, live(xn)], axis=0)
        cv = _causal_conv(xall, wv, DN_CONV, lead, n + DN_HALO)
        (_, _, _, gbv), vjp = jax.vjp(lambda c, b: _dn_point(c, b, av[...], dtv[...], heads), cv, ext(bac, ban))
        dc, dba = vjp((ext(dqc, live(dqn)), ext(dkc, live(dkn)), ext(dvc, live(dvn)), ext(dgbc, live(dgbn))))
        dx = None
        dw = []
        for j in range(DN_CONV):
            term = _shift_rows(dc, DN_CONV - 1 - j)[:n] * wv[j:j + 1, :]
            dx = term if dx is None else dx + term
            dw.append(_colsum(dc[:n] * _shift_rows(xall, lead + j)[:n]))
        dba = dba[:n]
        return ([jnp.concatenate([dx, dzc], axis=-1), dba],
                [_stack_rows(dw, DN_CONV), _colsum(dgbc * gbv[:n]), _colsum(dba)])

    return _rowwise(fn, n_rows=s, ts=ts, name="dn_pre_bwd",
                    rows=[(qkvz, 0, d3), (ba, 0, LANES), (dq, 0, d), (dk, 0, d), (dv, 0, d), (dgb, 0, LANES),
                          (dz, 0, d)],
                    prevs=[(qkvz, 0, d3, DN_HALO)],
                    nexts=[(qkvz, 0, d3, DN_HALO), (ba, 0, LANES, DN_HALO), (dq, 0, d, DN_HALO),
                           (dk, 0, d, DN_HALO), (dv, 0, d, DN_HALO), (dgb, 0, LANES, DN_HALO)],
                    vecs=[wconv, alog, dt],
                    row_outs=[(4 * d, BF16), (LANES, BF16)], acc_outs=[(DN_CONV, d3), (1, LANES), (1, LANES)])


def _cv_mid_fwd(u, wdw, bdw, lng, lnb, ts=512):
    s = u.shape[0]
    d = u.shape[1] // 2

    def fn(first, last, uc, up, wv, bv, gv, lbv):
        uext = jnp.concatenate([jnp.where(first, 0.0, up), uc], axis=0)
        glu = uext[:, :d] * _sigmoid(uext[:, d:])
        c = _causal_conv(glu, wv, CV_WIDTH, CV_HALO - (CV_WIDTH - 1), uc.shape[0]) + bv[...]
        return [c, _ln_silu(c, gv[...], lbv[...])], []

    return _rowwise(fn, n_rows=s, ts=ts, name="cv_mid_fwd", rows=[(u, 0, 2 * d)], prevs=[(u, 0, 2 * d, CV_HALO)],
                    vecs=[wdw, bdw, lng, lnb], row_outs=[(d, F32), (d, BF16)])


def _cv_mid_bwd2(dc, u, wdw, ts=512):
    s, d = dc.shape

    def fn(first, last, dcc, uc, up, dcn, wv):
        n = dcc.shape[0]
        dcext = jnp.concatenate([dcc, jnp.where(last, 0.0, dcn)], axis=0)
        uext = jnp.concatenate([jnp.where(first, 0.0, up), uc], axis=0)
        glu = uext[:, :d] * _sigmoid(uext[:, d:])
        dglu = None
        dw = []
        for j in range(CV_WIDTH):
            term = _shift_rows(dcext, CV_WIDTH - 1 - j)[:n] * wv[j:j + 1, :]
            dglu = term if dglu is None else dglu + term
            dw.append(_colsum(dcc * _shift_rows(glu, CV_HALO - (CV_WIDTH - 1) + j)[:n]))
        u1, sg = uc[:, :d], _sigmoid(uc[:, d:])
        du = jnp.concatenate([dglu * sg, dglu * u1 * sg * (1.0 - sg)], axis=-1)
        return [du], [_stack_rows(dw, CV_HALO), _colsum(du)]

    return _rowwise(fn, n_rows=s, ts=ts, name="cv_mid_bwd2", rows=[(dc, 0, d), (u, 0, 2 * d)],
                    prevs=[(u, 0, 2 * d, CV_HALO)], nexts=[(dc, 0, d, CV_HALO)], vecs=[wdw],
                    row_outs=[(2 * d, BF16)], acc_outs=[(CV_HALO, d), (1, 2 * d)])


def _attn_fwd(q, k, v, name, ts=2048):
    s, d = q.shape

    def fn(first, last, qv, kv, vv):
        return [_attn_tile(qv.astype(F32), kv[...].astype(F32), vv[...].astype(F32))], []

    return _rowwise(fn, n_rows=s, ts=ts, name=name, rows=[(q, 0, d)], vecs=[k, v], row_outs=[(d, BF16)])[0]


def _attn_bwd(q, k, v, do, name, ts=2048):
    s, d = q.shape
    m = k.shape[0]

    def fn(first, last, qv, dov, kv, vv):
        _, vjp = jax.vjp(_attn_tile, qv.astype(F32), kv[...].astype(F32), vv[...].astype(F32))
        dq, dk, dv = vjp(dov.astype(F32))
        return [dq], [dk, dv]

    return _rowwise(fn, n_rows=s, ts=ts, name=name, rows=[(q, 0, d), (do, 0, d)], vecs=[k, v],
                    row_outs=[(d, BF16)], acc_outs=[(m, d), (m, d)])


def _pad_lanes(a, off=0):
    r, n = a.shape
    return jnp.pad(a, ((0, 0), (off, LANES - off - n)))


def _local_step(x, mem, tgt, w, fetch=None, emit=None, first_after=()):
    s, d = x.shape
    heads = d // DN_HEAD_DIM
    g = {}
    if fetch is None:
        fetch = lambda group, after: None
    if emit is None:
        emit = lambda group, grads: ()

    def add_res_rms(acc, res, gain):
        h = res + acc
        return h, _rms(h, gain)

    def rms_bwd_epi(acc, hx, dres, gain):
        _, vjp = jax.vjp(_rms, hx, gain)
        dx, dg = vjp(acc)
        return dres + dx, dg

    w_int = w["dn_w_in"][0]
    assert w_int.shape[0] == 4 * d + 2 * heads
    w_bat = jnp.pad(w_int[4 * d:], ((0, LANES - 2 * heads), (0, 0)))
    dn_norm = w["dn_norm"]
    alog = _pad_lanes(w["dn_a_log"], heads)
    dtb = _pad_lanes(w["dn_dt_bias"], heads)
    wconv = w["dn_w_conv"][0]
    n0 = _rms_fwd(x, dn_norm, "dn_rms", after=first_after)
    qkvz = _matmul(n0, w_int, "nt", [F32], name="dn_in_proj", b_rows=4 * d, tn=MM_DEEP)
    ba = _matmul(n0, w_bat, "nt", [F32], name="dn_in_proj_ba")
    q, k, v, gb = _dn_pre_fwd(qkvz, ba, wconv, alog, dtb, heads)
    og, states, tms = _delta_fwd(q, k, v, gb, qkvz, w["dn_out_norm"], heads)
    fetch(1, [og])
    h1, nq0 = _matmul(og, w["dn_w_out"][0], "nn", [F32, BF16], name="dn_out_proj", epi=add_res_rms,
                      mn_extras=[x], row_extras=[w["xa_norm"][0:1]], slab=EPI_SLAB)

    def xattn_fwd(h, nq, layer, next_gain):
        qx = _matmul(nq, w["xa_w_q"][layer], "nn", [BF16], name=f"xa{layer}_q")
        mn = _rms_fwd(mem, w["xa_mem_norm"][layer:layer + 1], f"xa{layer}_mem_rms")
        kv = _matmul(mn, w["xa_w_kv"][layer], "nn", [BF16], name=f"xa{layer}_kv")
        kx, vx = kv[:, :d], kv[:, d:]
        ox = _attn_fwd(qx, kx, vx, f"xa{layer}_attn")
        hn, nn = _matmul(ox, w["xa_w_o"][layer], "nn", [F32, BF16], name=f"xa{layer}_o", epi=add_res_rms,
                         mn_extras=[h], row_extras=[next_gain], slab=EPI_SLAB)
        return hn, nn, (h, nq, qx, mn, kx, vx, ox)

    def sq_relu(t):
        r = jnp.maximum(t.astype(F32), 0.0)
        return r * r

    def loss_epi(acc, res, target, gain):
        def cols(hh, gg):
            e = _rms(hh, gg) - target
            return _colsum(e * e) * (0.5 / d)

        per_col, vjp = jax.vjp(cols, res + acc, gain)
        dhx, dgain = vjp(jnp.ones_like(per_col))
        return dhx, dgain, per_col

    def mlp_fwd(h, nm, layer, next_gain):
        u = _matmul(nm, w["mlp_w_up"][layer], "nn", [BF16], name=f"mlp{layer}_up", tn=MM_DEEP)
        if next_gain is None:
            hn, *nn = _matmul(u, w["mlp_w_down"][layer], "nn", [F32], name=f"mlp{layer}_down_loss",
                              epi=loss_epi, mn_extras=[h, tgt], row_extras=[w["final_norm"].reshape(1, d)],
                              n_rowsum=2, tk=MM_DEEP, slab=EPI_SLAB, a_pre=sq_relu)
        else:
            hn, nn = _matmul(u, w["mlp_w_down"][layer], "nn", [F32, BF16], name=f"mlp{layer}_down",
                             epi=add_res_rms, mn_extras=[h], row_extras=[next_gain], tk=MM_DEEP, slab=EPI_SLAB,
                             a_pre=sq_relu)
        return hn, nn, (h, nm, u)

    h2, nm0, xa0 = xattn_fwd(h1, nq0, 0, w["mlp_norm"][0:1])
    fetch(2, [h2])
    h3, n1, mlp0 = mlp_fwd(h2, nm0, 0, w["cv_norm"])

    u_cv = _matmul(n1, w["cv_w_pw1"][0], "nn", [F32], name="cv_pw1", epi=lambda acc, b: (acc + b,),
                   row_extras=[w["cv_b_pw1"]], tn=MM_DEEP)
    wdw = jnp.pad(w["cv_w_dw"][0], ((0, CV_HALO - CV_WIDTH), (0, 0)))
    c_cv, s_cv = _cv_mid_fwd(u_cv, wdw, w["cv_b_dw"], w["cv_ln_g"], w["cv_ln_b"])
    h4, nq1 = _matmul(s_cv, w["cv_w_pw2"][0], "nn", [F32, BF16], name="cv_pw2",
                      epi=lambda acc, res, b, gain: add_res_rms(acc + b, res, gain), mn_extras=[h3],
                      row_extras=[w["cv_b_pw2"], w["xa_norm"][1:2]], slab=EPI_SLAB)
    fetch(3, [h4])
    h5, nm1, xa1 = xattn_fwd(h4, nq1, 1, w["mlp_norm"][1:2])
    fetch(4, [h5])
    dh, (g_fn, loss_cols), mlp1 = mlp_fwd(h5, nm1, 1, None)
    g["final_norm"] = g_fn.reshape(d)
    loss = jnp.sum(loss_cols, axis=1, keepdims=True)

    def mlp_bwd(dh, layer, saved, after=()):
        h, nm, u = saved
        du = _matmul(dh, w["mlp_w_down"][layer], "nt", [BF16], name=f"mlp{layer}_down_dx", after=after,
                     epi=lambda acc, uu: (acc * 2.0 * jnp.maximum(uu.astype(F32), 0.0),), mn_extras=[u])
        gdown = _matmul(u, dh, "tn", [BF16], name=f"mlp{layer}_down_dw", tm=MM_DEEP, a_pre=sq_relu)
        dhn, gn = _matmul(du, w["mlp_w_up"][layer], "nt", [F32], name=f"mlp{layer}_up_dx", epi=rms_bwd_epi,
                          mn_extras=[h, dh], row_extras=[w["mlp_norm"][layer:layer + 1]], n_rowsum=1,
                          slab=EPI_SLAB, tk=MM_DEEP)
        gup = _matmul(nm, du, "tn", [BF16], name=f"mlp{layer}_up_dw", out_dm=True, tk=MM_DEEP, tn=MM_DEEP)
        return dhn, gup, gdown, gn

    def xattn_bwd(dh, layer, saved):
        h, nq, qx, mn, kx, vx, ox = saved
        dox = _matmul(dh, w["xa_w_o"][layer], "nt", [BF16], name=f"xa{layer}_o_dx")
        go = _matmul(ox, dh, "tn", [BF16], name=f"xa{layer}_o_dw", tk=MM_DEEP)
        dqx, dkx, dvx = _attn_bwd(qx, kx, vx, dox, f"xa{layer}_attn_bwd")

        def epi(acc, hx, dres, gain):
            dhx, dg = rms_bwd_epi(acc, hx, dres, gain)
            return dhx, dg, _colsum(dhx)

        dhn, gn, dh_cols = _matmul(dqx, w["xa_w_q"][layer], "nt", [F32], name=f"xa{layer}_q_dx", epi=epi,
                                   mn_extras=[h, dh], row_extras=[w["xa_norm"][layer:layer + 1]], n_rowsum=2,
                                   slab=EPI_SLAB)
        gq = _matmul(nq, dqx, "tn", [BF16], name=f"xa{layer}_q_dw", tk=MM_DEEP)
        dkv = jnp.concatenate([dkx, dvx], axis=-1)
        gkv = _matmul(mn, dkv, "tn", [BF16], name=f"xa{layer}_kv_dw", out_dm=True)
        dmn = _matmul(dkv, w["xa_w_kv"][layer], "nt", [F32], name=f"xa{layer}_kv_dx", tk=MM_DEEP)
        _, gmem = _rms_bwd(mem, dmn, dmn, w["xa_mem_norm"][layer:layer + 1], f"xa{layer}_mem_rms_bwd")
        return dhn, gq, gkv, go, gn, gmem, dh_cols

    dh, gup1, gdown1, gmn1 = mlp_bwd(dh, 1, mlp1)
    dh, gq1, gkv1, go1, gxn1, gmem1, g_b2 = xattn_bwd(dh, 1, xa1)
    g.update(mlp_w_up=[None, gup1], mlp_w_down=[None, gdown1], xa_w_q=[None, gq1], xa_w_kv=[None, gkv1],
             xa_w_o=[None, go1])
    tok = emit(3, g)

    def ln_bwd_epi(acc, cx, gain, bias):
        _, vjp = jax.vjp(_ln_silu, cx, gain, bias)
        dc, dg, db = vjp(acc)
        return dc, dg, db, _colsum(dc)

    dc_cv, g_lng, g_lnb, g_bdw = _matmul(dh, w["cv_w_pw2"][0], "nt", [F32], name="cv_pw2_dx", after=tok,
                                        epi=ln_bwd_epi, mn_extras=[c_cv],
                                        row_extras=[w["cv_ln_g"], w["cv_ln_b"]], n_rowsum=3, slab=EPI_SLAB)
    g["cv_w_pw2"] = [_matmul(s_cv, dh, "tn", [BF16], name="cv_pw2_dw", tk=MM_DEEP)]
    du_cv, g_wdw, g_b1 = _cv_mid_bwd2(dc_cv, u_cv, wdw)
    g["cv_w_pw1"] = [_matmul(n1, du_cv, "tn", [BF16], name="cv_pw1_dw", out_dm=True, tk=MM_DEEP)]
    dh, g_cvn = _matmul(du_cv, w["cv_w_pw1"][0], "nt", [F32], name="cv_pw1_dx", epi=rms_bwd_epi,
                        mn_extras=[h3, dh], row_extras=[w["cv_norm"]], n_rowsum=1, slab=EPI_SLAB, tk=MM_DEEP)
    g.update(cv_ln_g=g_lng, cv_ln_b=g_lnb, cv_b_dw=g_bdw, cv_b_pw2=g_b2, cv_b_pw1=g_b1, cv_norm=g_cvn,
             cv_w_dw=g_wdw[:CV_WIDTH][None])

    tok = emit(2, g)
    dh, gup0, gdown0, gmn0 = mlp_bwd(dh, 0, mlp0, after=tok)
    dh, gq0, gkv0, go0, gxn0, gmem0, _ = xattn_bwd(dh, 0, xa0)
    g["mlp_w_up"][0] = gup0
    g["mlp_w_down"][0] = gdown0
    g["mlp_norm"] = jnp.concatenate([gmn0, gmn1], axis=0)
    g["xa_w_q"][0] = gq0
    g["xa_w_kv"][0] = gkv0
    g["xa_w_o"][0] = go0
    g["xa_norm"] = jnp.concatenate([gxn0, gxn1], axis=0)
    g["xa_mem_norm"] = jnp.concatenate([gmem0, gmem1], axis=0)
    g["dn_w_out"] = [_matmul(og, dh, "tn", [BF16], name="dn_out_proj_dw", tk=MM_DEEP)]
    tok = emit(1, g)
    dog = _matmul(dh, w["dn_w_out"][0], "nt", [BF16], name="dn_out_proj_dx", after=tok)
    dq, dk, dv, dgb, dz, g_on = _delta_bwd(q, k, v, gb, qkvz, w["dn_out_norm"], states, tms, dog, heads)
    dqkvz, dba, g_wconv, g_alog, g_dt = _dn_pre_bwd(qkvz, ba, wconv, alog, dtb, dq, dk, dv, dgb, dz, heads)
    g_qkvzt = _matmul(dqkvz, n0, "tn", [BF16], name="dn_in_proj_dw", tk=MM_DEEP)
    g_bat = _matmul(dba, n0, "tn", [BF16], name="dn_in_proj_ba_dw", tk=MM_DEEP)
    g["dn_w_in"] = [jnp.concatenate([g_qkvzt, g_bat[:2 * heads]], axis=0)]
    g["dn_w_conv"] = g_wconv[None]
    tok = emit(0, g)
    dn0a = _matmul(dba, w_bat, "nn", [F32], name="dn_in_proj_ba_dx", after=tok)
    grad_x, g_dnn = _matmul(dqkvz, w_int, "nn", [F32], name="dn_in_proj_dx", b_rows=4 * d,
                            epi=lambda acc, part, hx, dres, gain: rms_bwd_epi(acc + part, hx, dres, gain),
                            mn_extras=[dn0a, x, dh], row_extras=[dn_norm], n_rowsum=1, slab=EPI_SLAB)
    g.update(dn_norm=g_dnn, dn_out_norm=g_on,
             dn_a_log=g_alog[:, heads:2 * heads], dn_dt_bias=g_dt[:, heads:2 * heads])
    return loss, grad_x, g


def _round_up(n, m):
    return (n + m - 1) // m * m


def _pack_rows(parts, cols, row_mult):
    lead = parts[0].shape[:-1]
    flat, offs, off = [], [], 0
    for p in parts:
        n = _round_up(p.shape[-1], cols)
        flat.append(jnp.pad(p, [(0, 0)] * len(lead) + [(0, n - p.shape[-1])]))
        offs.append(off)
        off += n
    total = _round_up(off, cols * row_mult)
    if total > off:
        flat.append(jnp.zeros(lead + (total - off,), parts[0].dtype))
    return jnp.concatenate(flat, axis=-1).reshape(lead + (total // cols, cols)), offs


def _unpack(packed, offs, shapes):
    lead = packed.shape[:-2]
    flat = packed.reshape(lead + (-1,))
    out = []
    for off, shp in zip(offs, shapes):
        n = 1
        for v in shp:
            n *= v
        out.append(flat[..., off:off + n].reshape(lead + tuple(shp)))
    return out


def kernel(x, mem, dn_norm, dn_w_in, dn_w_conv, dn_a_log, dn_dt_bias, dn_out_norm, dn_w_out, cv_norm, cv_w_pw1, cv_b_pw1, cv_w_dw, cv_b_dw, cv_ln_g, cv_ln_b, cv_w_pw2, cv_b_pw2, xa_norm, xa_mem_norm, xa_w_q, xa_w_kv, xa_w_o, mlp_norm, mlp_w_up, mlp_w_down, final_norm, loss_target, m_dn_norm, m_dn_w_in, m_dn_w_conv, m_dn_a_log, m_dn_dt_bias, m_dn_out_norm, m_dn_w_out, m_cv_norm, m_cv_w_pw1, m_cv_b_pw1, m_cv_w_dw, m_cv_b_dw, m_cv_ln_g, m_cv_ln_b, m_cv_w_pw2, m_cv_b_pw2, m_xa_norm, m_xa_mem_norm, m_xa_w_q, m_xa_w_kv, m_xa_w_o, m_mlp_norm, m_mlp_w_up, m_mlp_w_down, m_final_norm, v_dn_norm, v_dn_w_in, v_dn_w_conv, v_dn_a_log, v_dn_dt_bias, v_dn_out_norm, v_dn_w_out, v_cv_norm, v_cv_w_pw1, v_cv_b_pw1, v_cv_w_dw, v_cv_b_dw, v_cv_ln_g, v_cv_ln_b, v_cv_w_pw2, v_cv_b_pw2, v_xa_norm, v_xa_mem_norm, v_xa_w_q, v_xa_w_kv, v_xa_w_o, v_mlp_norm, v_mlp_w_up, v_mlp_w_down, v_final_norm):
    wsh = dict(dn_norm=dn_norm, dn_w_in=dn_w_in, dn_w_conv=dn_w_conv, dn_a_log=dn_a_log, dn_dt_bias=dn_dt_bias, dn_out_norm=dn_out_norm, dn_w_out=dn_w_out, cv_norm=cv_norm, cv_w_pw1=cv_w_pw1, cv_b_pw1=cv_b_pw1, cv_w_dw=cv_w_dw, cv_b_dw=cv_b_dw, cv_ln_g=cv_ln_g, cv_ln_b=cv_ln_b, cv_w_pw2=cv_w_pw2, cv_b_pw2=cv_b_pw2, xa_norm=xa_norm, xa_mem_norm=xa_mem_norm, xa_w_q=xa_w_q, xa_w_kv=xa_w_kv, xa_w_o=xa_w_o, mlp_norm=mlp_norm, mlp_w_up=mlp_w_up, mlp_w_down=mlp_w_down, final_norm=final_norm)
    msh = dict(dn_norm=m_dn_norm, dn_w_in=m_dn_w_in, dn_w_conv=m_dn_w_conv, dn_a_log=m_dn_a_log, dn_dt_bias=m_dn_dt_bias, dn_out_norm=m_dn_out_norm, dn_w_out=m_dn_w_out, cv_norm=m_cv_norm, cv_w_pw1=m_cv_w_pw1, cv_b_pw1=m_cv_b_pw1, cv_w_dw=m_cv_w_dw, cv_b_dw=m_cv_b_dw, cv_ln_g=m_cv_ln_g, cv_ln_b=m_cv_ln_b, cv_w_pw2=m_cv_w_pw2, cv_b_pw2=m_cv_b_pw2, xa_norm=m_xa_norm, xa_mem_norm=m_xa_mem_norm, xa_w_q=m_xa_w_q, xa_w_kv=m_xa_w_kv, xa_w_o=m_xa_w_o, mlp_norm=m_mlp_norm, mlp_w_up=m_mlp_w_up, mlp_w_down=m_mlp_w_down, final_norm=m_final_norm)
    vsh = dict(dn_norm=v_dn_norm, dn_w_in=v_dn_w_in, dn_w_conv=v_dn_w_conv, dn_a_log=v_dn_a_log, dn_dt_bias=v_dn_dt_bias, dn_out_norm=v_dn_out_norm, dn_w_out=v_dn_w_out, cv_norm=v_cv_norm, cv_w_pw1=v_cv_w_pw1, cv_b_pw1=v_cv_b_pw1, cv_w_dw=v_cv_w_dw, cv_b_dw=v_cv_b_dw, cv_ln_g=v_cv_ln_g, cv_ln_b=v_cv_ln_b, cv_w_pw2=v_cv_w_pw2, cv_b_pw2=v_cv_b_pw2, xa_norm=v_xa_norm, xa_mem_norm=v_xa_mem_norm, xa_w_q=v_xa_w_q, xa_w_kv=v_xa_w_kv, xa_w_o=v_xa_w_o, mlp_norm=v_mlp_norm, mlp_w_up=v_mlp_w_up, mlp_w_down=v_mlp_w_down, final_norm=v_final_norm)

    big_axis = dict(BIG)
    for src in (wsh, msh, vsh):
        src["dn_w_in"] = jnp.swapaxes(src["dn_w_in"], 1, 2)
    big_axis["dn_w_in"] = 1

    small_pack, small_offs = _pack_rows([wsh[nm].reshape(-1) for nm in SMALL_SH], LANES, 8)
    w = {nm: [None] * wsh[nm].shape[0] for nm in big_axis}

    def put_weights(group, gathered):
        for (nm, layer), gth in zip(group, gathered):
            if big_axis[nm] == 1:
                w[nm][layer] = gth.reshape(N_DEV * gth.shape[1], gth.shape[2])
            else:
                w[nm][layer] = gth

    first = _all_gather([wsh[nm][layer].astype(BF16) for nm, layer in GATHER_GROUPS[0]] + [small_pack],
                        "weights_all_gather_0")
    put_weights(GATHER_GROUPS[0], first)
    me = _dev_index(*_mesh_me())
    gather_handles, tokens = {}, []
    for gi in range(1, len(GATHER_GROUPS)):
        shards = [wsh[nm][layer].astype(BF16) for nm, layer in GATHER_GROUPS[gi]]
        lands = [lax.dynamic_update_slice(lax.empty((N_DEV,) + s.shape, s.dtype), s[None], (me, 0, 0))
                 for s in shards]
        gather_handles[gi], tok = _exchange_start(shards, lands, False, f"weights_gather_{gi}",
                                                  after=[first[-1]] + tokens)
        tokens.append(tok)
    for nm, gth in zip(SMALL_SH, _unpack(first[-1], small_offs, [wsh[nm].shape for nm in SMALL_SH])):
        w[nm] = jnp.moveaxis(gth, 0, -2).reshape(gth.shape[1:-1] + (N_DEV * gth.shape[-1],))
    for nm in REPL:
        w[nm] = wsh[nm]

    def fetch(gi, after):
        put_weights(GATHER_GROUPS[gi], _exchange_wait(gather_handles[gi], after)[1])

    scatter_handles = {}

    def emit(gi, g):
        blocks = []
        for nm, layer in SCATTER_GROUPS[gi]:
            gw = g[nm][layer]
            if big_axis[nm] == 1:
                gw = gw.reshape(N_DEV, gw.shape[0] // N_DEV, gw.shape[1])
            blocks.append(gw)
        if gi == 0:
            gsmall_pack, _ = _pack_rows(
                [jnp.moveaxis(g[nm].reshape(g[nm].shape[:-1] + (N_DEV, -1)), -2, 0).reshape(N_DEV, -1)
                 for nm in SMALL_SH], LANES, 8)
            blocks.append(gsmall_pack)
        lands = [lax.empty(b.shape, b.dtype) for b in blocks]
        scatter_handles[gi], tok = _exchange_start(blocks, lands, True, f"grads_scatter_{gi}")
        return [tok]

    loss_part, grad_x, g = _local_step(x[0], mem[0], loss_target[0], w, fetch, emit, tokens)

    recv = {nm: [None] * wsh[nm].shape[0] for nm in big_axis}
    sent = {nm: [None] * wsh[nm].shape[0] for nm in big_axis}
    gsh, delta, new_m, new_v = {}, {}, {}, {}
    after = [grad_x]
    done = set()
    me_arr = me.astype(jnp.int32).reshape(1)
    def small_adamw(names, name):
        packs = []
        for src in (wsh, gsh, msh, vsh):
            pk, offs = _pack_rows([src[nm].reshape(-1) for nm in names], LANES, 8)
            packs.append(pk)
        outs = _adamw(*packs, name)
        for dst, pk in zip((delta, new_m, new_v), outs):
            for nm, val in zip(names, _unpack(pk, offs, [wsh[nm].shape for nm in names])):
                dst[nm] = val
        return outs[0]

    for gi in reversed(range(len(SCATTER_GROUPS))):
        if gi == 0:
            repl_pack, repl_offs = _pack_rows([g[nm].reshape(-1) for nm in REPL] + [loss_part[:, :1].reshape(-1)],
                                              LANES, 8)
            (repl_all,) = _all_gather([repl_pack], "repl_grads_all_gather")
            repl_red = _slot_sum(repl_all, "repl_grads_sum", SLOT_SUM_ROWS)
            *repl_vals, loss_sum = _unpack(repl_red, repl_offs, [wsh[nm].shape for nm in REPL] + [(1,)])
            for nm, val in zip(REPL, repl_vals):
                gsh[nm] = val
            after = after + [small_adamw(list(REPL), "adamw_repl")]
        sources, landed = _exchange_wait(scatter_handles[gi], after)
        for (nm, layer), src, r in zip(SCATTER_GROUPS[gi], sources, landed):
            sent[nm][layer], recv[nm][layer] = src, r
        if gi == 0:
            slot = lax.broadcasted_iota(jnp.int32, landed[-1].shape, 0)
            rsmall = jnp.where(slot == me, sources[-1], landed[-1])
        for nm in big_axis:
            if nm not in done and all(r is not None for r in recv[nm]):
                gsh[nm], delta[nm], new_m[nm], new_v[nm] = _adamw_reduce(
                    me_arr, recv[nm], sent[nm], wsh[nm], msh[nm], vsh[nm], f"adamw_{nm}")
                done.add(nm)
                after = [delta[nm]]
    gsmall_red = _slot_sum(rsmall, "grads_small_sum", SLOT_SUM_ROWS)
    for nm, val in zip(SMALL_SH, _unpack(gsmall_red, small_offs, [wsh[nm].shape for nm in SMALL_SH])):
        gsh[nm] = val
    small_adamw(list(SMALL_SH), "adamw_small")
    for dst in (gsh, delta, new_m, new_v):
        dst["dn_w_in"] = jnp.swapaxes(dst["dn_w_in"], 1, 2)
    return (loss_sum.reshape(()), grad_x[None], *[gsh[nm] for nm in WEIGHTS], *[delta[nm] for nm in WEIGHTS],
            *[new_m[nm] for nm in WEIGHTS], *[new_v[nm] for nm in WEIGHTS])
```

```python
import functools

import jax
import jax.numpy as jnp
from jax import lax
from jax.experimental import pallas as pl
from jax.experimental.pallas import tpu as pltpu

F32 = jnp.float32
BF16 = jnp.bfloat16
MESH_IDS = pl.DeviceIdType.MESH

N_DEV = 8
LANES = 128
RMS_EPS = 1e-6
LN_EPS = 1e-5
DN_HEAD_DIM = 128
DN_CONV = 4
DN_CHUNK = 64
CV_WIDTH = 31
XA_HEADS = 4
MM_TILE = 1024
MM_DEEP = 2048
EPI_SLAB = 512
SLOT_SUM_ROWS = 512
DN_HALO = 8
CV_HALO = 32

ADAM_LR = 0.001
ADAM_B1 = 0.9
ADAM_B2 = 0.999
ADAM_EPS = 1e-08
ADAM_WD = 0.01
ADAM_STEP = 10

BIG = (("dn_w_in", 2), ("dn_w_out", 1), ("cv_w_pw1", 2), ("cv_w_pw2", 1), ("xa_w_q", 1), ("xa_w_kv", 2),
       ("xa_w_o", 1), ("mlp_w_up", 2), ("mlp_w_down", 1))
_LAYER_GROUP = ("xa_w_q", "xa_w_o", "mlp_w_down", "xa_w_kv", "mlp_w_up")
GATHER_GROUPS = (
    (("dn_w_in", 0),),
    (("dn_w_out", 0),) + tuple((nm, 0) for nm in _LAYER_GROUP),
    (("cv_w_pw2", 0), ("cv_w_pw1", 0)),
    (("xa_w_q", 1), ("xa_w_o", 1), ("xa_w_kv", 1)),
    (("mlp_w_down", 1), ("mlp_w_up", 1)),
)
SCATTER_GROUPS = (
    (("dn_w_in", 0),),
    (("dn_w_out", 0),) + tuple((nm, 0) for nm in _LAYER_GROUP),
    (("cv_w_pw2", 0), ("cv_w_pw1", 0)),
    tuple((nm, 1) for nm in _LAYER_GROUP),
)
SMALL_SH = ("cv_norm", "cv_b_pw1", "cv_b_dw", "cv_ln_g", "cv_ln_b", "cv_b_pw2", "cv_w_dw", "dn_w_conv")
REPL = ("dn_norm", "dn_a_log", "dn_dt_bias", "dn_out_norm", "xa_norm", "xa_mem_norm", "mlp_norm", "final_norm")
WEIGHTS = ("dn_norm", "dn_w_in", "dn_w_conv", "dn_a_log", "dn_dt_bias", "dn_out_norm", "dn_w_out", "cv_norm",
           "cv_w_pw1", "cv_b_pw1", "cv_w_dw", "cv_b_dw", "cv_ln_g", "cv_ln_b", "cv_w_pw2", "cv_b_pw2", "xa_norm",
           "xa_mem_norm", "xa_w_q", "xa_w_kv", "xa_w_o", "mlp_norm", "mlp_w_up", "mlp_w_down", "final_norm")


def _dot_dims(mode, batched):
    o = 1 if batched else 0
    contract = {"nn": ((1 + o,), (o,)), "nt": ((1 + o,), (1 + o,)), "tn": ((o,), (o,))}[mode]
    return (contract, (((0,), (0,)) if batched else ((), ())))


def _bdot(a, b, mode):
    return lax.dot_general(a.astype(BF16), b.astype(BF16), _dot_dims(mode, a.ndim == 3),
                           preferred_element_type=F32)


@functools.partial(jax.custom_vjp, nondiff_argnums=(2,))
def _mm(a, b, mode):
    return _bdot(a, b, mode)


def _mm_fwd(a, b, mode):
    return _bdot(a, b, mode), (a, b)


def _mm_bwd(mode, res, ct):
    a, b = res
    if mode == "nn":
        da, db = _bdot(ct, b, "nt"), _bdot(a, ct, "tn")
    elif mode == "nt":
        da, db = _bdot(ct, b, "nn"), _bdot(ct, a, "tn")
    else:
        da, db = _bdot(b, ct, "nt"), _bdot(a, ct, "nn")
    return da.astype(a.dtype), db.astype(b.dtype)


_mm.defvjp(_mm_fwd, _mm_bwd)


def _sigmoid(x):
    return 0.5 * (jnp.tanh(0.5 * x) + 1.0)


def _silu(x):
    return x * _sigmoid(x)


def _softplus(x):
    return jnp.maximum(x, 0.0) + jnp.log(1.0 + jnp.exp(-jnp.abs(x)))


def _rms(x, g):
    r = lax.rsqrt(jnp.mean(x * x, axis=-1, keepdims=True) + RMS_EPS)
    return x * r * g


def _shift_rows(x, off):
    if off == 0:
        return x
    return pltpu.roll(x, x.shape[0] - off, 0)


def _series_dot(a, b, mode):
    return _bdot(a, b, mode)


def _chunk_masks(c):
    ii = lax.broadcasted_iota(jnp.int32, (c, c), 0)
    jj = lax.broadcasted_iota(jnp.int32, (c, c), 1)
    return (ii == jj).astype(F32), ii >= jj, ii > jj


def _neumann_inverse(lm):
    n = lm.shape[-1]
    t = -lm
    p = lm
    size = 2
    while size < n:
        size *= 2
        p = _series_dot(p, p, "nn")
        t = t + p + _series_dot(t, p, "nn")
    return t


def _apply_inverse(tm, rhs, mode):
    return rhs + _series_dot(tm, rhs, mode)


@jax.custom_vjp
def _unit_lower_solve(lm, rhs, tm):
    return _apply_inverse(tm, rhs, "nn")


def _uls_fwd(lm, rhs, tm):
    sol = _apply_inverse(tm, rhs, "nn")
    return sol, (tm, sol)


def _uls_bwd(res, ct):
    tm, sol = res
    d_rhs = _apply_inverse(tm, ct, "tn")
    return -_bdot(d_rhs, sol, "nt"), d_rhs, jnp.zeros_like(tm)


_unit_lower_solve.defvjp(_uls_fwd, _uls_bwd)


def _delta_chunk(q, k, v, gcol, bcol, s0, tm=None):
    c = q.shape[1]
    eye, causal, strict = _chunk_masks(c)
    grow = jnp.sum(eye * gcol, axis=1, keepdims=True)
    gc = jnp.sum(jnp.where(causal, grow, 0.0), axis=2, keepdims=True)
    gc_row = jnp.sum(eye * gc, axis=1, keepdims=True)
    decay = jnp.exp(jnp.where(causal, gc - gc_row, -jnp.inf))
    kb = k * bcol
    on_k = _mm(jnp.concatenate([kb, q], axis=1), k, "nt")
    lm = jnp.where(strict, on_k[:, :c] * decay, 0.0)
    attn = on_k[:, c:] * decay
    if tm is None:
        tm = _neumann_inverse(lax.stop_gradient(lm))
    egc = jnp.exp(gc)
    rhs = jnp.concatenate([v * bcol, kb * egc], axis=-1)
    sol = _unit_lower_solve(lm, rhs, tm)
    dv_ = v.shape[-1]
    u, w = sol[..., :dv_], sol[..., dv_:]
    gl = jnp.sum(grow, axis=2, keepdims=True)
    kd = k * jnp.exp(gl - gc)
    on_s = _mm(jnp.concatenate([w, q * egc], axis=1), s0, "nn")
    v_new = u - on_s[:, :c]
    o = on_s[:, c:] + _mm(attn, v_new, "nn")
    s1 = s0 * jnp.exp(gl) + _mm(kd, v_new, "tn")
    return o, s1, tm


def _dn_point(cv, ba, alog, dt, heads):
    a = _silu(cv)
    d = cv.shape[1] // 3
    qs, ks = [], []
    for h in range(heads):
        qh = a[:, h * DN_HEAD_DIM:(h + 1) * DN_HEAD_DIM]
        qs.append(qh * lax.rsqrt(jnp.sum(qh * qh, axis=-1, keepdims=True) + 1e-6) * (DN_HEAD_DIM ** -0.5))
        kh = a[:, d + h * DN_HEAD_DIM:d + (h + 1) * DN_HEAD_DIM]
        ks.append(kh * lax.rsqrt(jnp.sum(kh * kh, axis=-1, keepdims=True) + 1e-6))
    q = jnp.concatenate(qs, axis=-1)
    k = jnp.concatenate(ks, axis=-1)
    v = a[:, 2 * d:]
    lane = lax.broadcasted_iota(jnp.int32, ba.shape, 1)
    beta = _sigmoid(ba)
    g = -jnp.exp(alog) * _softplus(ba + dt)
    gb = jnp.where(lane < heads, beta, jnp.where(lane < 2 * heads, g, 0.0))
    return q, k, v, gb


def _attn_tile(q, k, v):
    hd = q.shape[1] // XA_HEADS
    outs = []
    for h in range(XA_HEADS):
        sl = slice(h * hd, (h + 1) * hd)
        s = _mm(q[:, sl], k[:, sl], "nt") * (hd ** -0.5)
        m = lax.stop_gradient(jnp.max(s, axis=-1, keepdims=True))
        e = jnp.exp(s - m)
        p = e / jnp.sum(e, axis=-1, keepdims=True)
        outs.append(_mm(p, v[:, sl], "nn"))
    return jnp.concatenate(outs, axis=-1)


def _ln_silu(c, g, b):
    mu = jnp.mean(c, axis=-1, keepdims=True)
    xc = c - mu
    y = xc * lax.rsqrt(jnp.mean(xc * xc, axis=-1, keepdims=True) + LN_EPS)
    return _silu(y * g + b)


def _causal_conv(xext, w, width, lead, ts):
    acc = None
    for j in range(width):
        term = _shift_rows(xext, lead + j)[:ts] * w[j:j + 1, :]
        acc = term if acc is None else acc + term
    return acc


def _colsum(x):
    return jnp.sum(x, axis=0, keepdims=True)


def _stack_rows(rows, n_rows):
    c = rows[0].shape[1]
    ridx = lax.broadcasted_iota(jnp.int32, (n_rows, c), 0)
    out = jnp.zeros((n_rows, c), F32)
    for j, r in enumerate(rows):
        out = out + jnp.where(ridx == j, r, 0.0)
    return out


def _matmul(a, b, mode, out_dtypes, *, name, epi=None, mn_extras=(), row_extras=(), out_dm=False, after=(),
            n_rowsum=0, slab=0, b_rows=None, a_pre=None, tm=MM_TILE, tn=MM_TILE, tk=MM_TILE):
    b_dm = b.ndim == 3
    b_shape = (b.shape[1], N_DEV * b.shape[2]) if b_dm else b.shape
    if b_rows is not None:
        assert not b_dm and b_rows <= b.shape[0]
        b_shape = (b_rows, b.shape[1])
    if mode == "nn":
        (m, k), (k2, n) = a.shape, b_shape
    elif mode == "nt":
        (m, k), (n, k2) = a.shape, b_shape
    else:
        (k, m), (k2, n) = a.shape, b_shape
    assert k == k2, (a.shape, b.shape, mode)
    tm, tn, tk = min(tm, m), min(tn, n), min(tk, k)
    cb, nb = 0, 1
    if b_dm:
        assert mode in ("nn", "nt")
        cb = b.shape[2]
        nb = max(1, (tn if mode == "nn" else tk) // cb)
        if mode == "nn":
            tn = nb * cb
        else:
            tk = nb * cb
    co, no = 0, 1
    if out_dm:
        co = n // N_DEV
        no = max(1, tn // co)
        tn = no * co
    assert m % tm == 0 and n % tn == 0 and k % tk == 0, (m, n, k, tm, tn, tk)
    nk = k // tk
    if mode == "tn":
        a_spec = pl.BlockSpec((tk, tm), lambda j, i, kk: (kk, i))
    else:
        a_spec = pl.BlockSpec((tm, tk), lambda j, i, kk: (i, kk))
    if b_dm:
        b_spec = (pl.BlockSpec((nb, tn, cb), lambda j, i, kk: (kk, j, 0)) if mode == "nt"
                  else pl.BlockSpec((nb, tk, cb), lambda j, i, kk: (j, kk, 0)))
    else:
        b_spec = (pl.BlockSpec((tn, tk), lambda j, i, kk: (j, kk)) if mode == "nt"
                  else pl.BlockSpec((tk, tn), lambda j, i, kk: (kk, j)))
    mn_spec = pl.BlockSpec((tm, tn), lambda j, i, kk: (i, j))
    row_spec = pl.BlockSpec((1, tn), lambda j, i, kk: (0, j))
    n_extra = len(mn_extras) + len(row_extras)
    n_out = len(out_dtypes)
    in_specs = ([a_spec, b_spec] + [mn_spec] * len(mn_extras) + [row_spec] * len(row_extras)
                + [_ANY_SPEC] * len(after))
    args = [a, b, *mn_extras, *row_extras, *after]
    if out_dm:
        out_specs = [pl.BlockSpec((no, tm, co), lambda j, i, kk: (j, i, 0))] * n_out
        out_shape = [jax.ShapeDtypeStruct((N_DEV, m, co), dt) for dt in out_dtypes]
    else:
        out_specs = [mn_spec] * n_out
        out_shape = [jax.ShapeDtypeStruct((m, n), dt) for dt in out_dtypes]
    out_specs = out_specs + [row_spec] * n_rowsum
    out_shape = out_shape + [jax.ShapeDtypeStruct((1, n), F32)] * n_rowsum
    n_in = len(args)
    n_mn = len(mn_extras)
    step = min(slab, tm) if slab else tm
    assert tm % step == 0

    def dot(a_ref, b_ref):
        a_val = a_ref[...] if a_pre is None else a_pre(a_ref[...])
        if not b_dm:
            return _bdot(a_val, b_ref[...], mode)
        if mode == "nn":
            parts = [_bdot(a_val, b_ref[dd], "nn") for dd in range(nb)]
            return parts[0] if nb == 1 else jnp.concatenate(parts, axis=1)
        out = None
        for dd in range(nb):
            part = _bdot(a_val[:, dd * cb:(dd + 1) * cb], b_ref[dd], "nt")
            out = part if out is None else out + part
        return out

    def finish(acc_src, extras, outs):
        sums = [None] * n_rowsum
        for r0 in range(0, tm, step):
            rs = slice(r0, r0 + step)
            acc_val = acc_src[rs, :]
            if epi is None:
                vals = (acc_val,)
            else:
                vals = epi(acc_val, *[e[rs, :] for e in extras[:n_mn]], *[e[...] for e in extras[n_mn:]])
            for o_ref, val in zip(outs[:n_out], vals[:n_out]):
                if out_dm:
                    for dd in range(no):
                        o_ref[dd, rs, :] = val[:, dd * co:(dd + 1) * co].astype(o_ref.dtype)
                else:
                    o_ref[rs, :] = val.astype(o_ref.dtype)
            for q in range(n_rowsum):
                sums[q] = vals[n_out + q] if sums[q] is None else sums[q] + vals[n_out + q]
        for q in range(n_rowsum):
            s_ref = outs[n_out + q]

            @pl.when(pl.program_id(1) == 0)
            def _():
                s_ref[...] = sums[q]

            @pl.when(pl.program_id(1) > 0)
            def _():
                s_ref[...] += sums[q]

    def body_one_step(*refs):
        finish(dot(refs[0], refs[1]), refs[2:2 + n_extra], refs[n_in:])

    def body(*refs):
        a_ref, b_ref = refs[0], refs[1]
        acc = refs[-1]
        kk = pl.program_id(2)

        @pl.when(kk == 0)
        def _():
            acc[...] = jnp.zeros_like(acc)

        acc[...] += dot(a_ref, b_ref)

        @pl.when(kk == nk - 1)
        def _():
            finish(acc, refs[2:2 + n_extra], refs[n_in:-1])

    res = pl.pallas_call(
        body_one_step if nk == 1 else body, name=name,
        grid=(n // tn, m // tm, nk),
        in_specs=in_specs, out_specs=out_specs, out_shape=out_shape,
        scratch_shapes=[] if nk == 1 else [pltpu.VMEM((tm, tn), F32)],
        compiler_params=pltpu.CompilerParams(
            dimension_semantics=("parallel", "arbitrary" if n_rowsum else "parallel", "arbitrary")),
    )(*args)
    return res[0] if n_out + n_rowsum == 1 else res


def _rowwise(fn, *, n_rows, ts, name, rows=(), prevs=(), nexts=(), vecs=(), row_outs=(), acc_outs=(), after=()):
    ts = min(ts, n_rows)
    assert n_rows % ts == 0
    nblk = n_rows // ts
    in_specs, args = [], []
    for arr, cb, w in rows:
        in_specs.append(pl.BlockSpec((ts, w), functools.partial(lambda i, cb: (i, cb), cb=cb)))
        args.append(arr)
    for arr, cb, w, halo in prevs:
        per = ts // halo
        in_specs.append(pl.BlockSpec(
            (halo, w), functools.partial(lambda i, cb, per: (jnp.maximum(i * per - 1, 0), cb), cb=cb, per=per)))
        args.append(arr)
    for arr, cb, w, halo in nexts:
        per = ts // halo
        last_blk = n_rows // halo - 1
        in_specs.append(pl.BlockSpec(
            (halo, w), functools.partial(lambda i, cb, per, lb: (jnp.minimum((i + 1) * per, lb), cb),
                                         cb=cb, per=per, lb=last_blk)))
        args.append(arr)
    for arr in vecs:
        in_specs.append(pl.BlockSpec(arr.shape, functools.partial(lambda i, nd: (0,) * nd, nd=arr.ndim)))
        args.append(arr)
    out_specs, out_shape = [], []
    for w, dt in row_outs:
        out_specs.append(pl.BlockSpec((ts, w), lambda i: (i, 0)))
        out_shape.append(jax.ShapeDtypeStruct((n_rows, w), dt))
    for shp in acc_outs:
        out_specs.append(pl.BlockSpec(shp, functools.partial(lambda i, nd: (0,) * nd, nd=len(shp))))
        out_shape.append(jax.ShapeDtypeStruct(shp, F32))
    n_used = len(args)
    n_tiles = n_used - len(vecs)
    in_specs += [_ANY_SPEC] * len(after)
    args += list(after)
    n_in, n_ro, n_acc = len(args), len(row_outs), len(acc_outs)

    def body(*refs):
        ins, ro, ac = refs[:n_used], refs[n_in:n_in + n_ro], refs[n_in + n_ro:]
        i = pl.program_id(0)
        rvals, avals = fn(i == 0, i == nblk - 1, *[r[...] for r in ins[:n_tiles]], *ins[n_tiles:])
        for r, val in zip(ro, rvals):
            r[...] = val.astype(r.dtype)
        if n_acc:
            @pl.when(i == 0)
            def _():
                for r in ac:
                    r[...] = jnp.zeros_like(r)

            for r, val in zip(ac, avals):
                r[...] += val

    res = pl.pallas_call(
        body, name=name, grid=(nblk,), in_specs=in_specs, out_specs=out_specs, out_shape=out_shape,
        compiler_params=pltpu.CompilerParams(dimension_semantics=("arbitrary",)),
    )(*args)
    return res


def _gated_out(o, z, onorm):
    return o * lax.rsqrt(jnp.mean(o * o, axis=-1, keepdims=True) + RMS_EPS) * onorm * _silu(z)


def _head_blocks(ref, heads, col0=0):
    return jnp.stack([ref[:, col0 + h * DN_HEAD_DIM:col0 + (h + 1) * DN_HEAD_DIM] for h in range(heads)])


def _split_heads(q_ref, k_ref, v_ref, gbv, heads):
    gcol = jnp.stack([gbv[:, heads + h:heads + h + 1] for h in range(heads)])
    bcol = jnp.stack([gbv[:, h:h + 1] for h in range(heads)])
    return _head_blocks(q_ref, heads), _head_blocks(k_ref, heads), _head_blocks(v_ref, heads), gcol, bcol


def _delta_fwd(q, k, v, gb, qkvz, onorm, heads):
    s, hd = q.shape
    n = s // DN_CHUNK
    blk = pl.BlockSpec((DN_CHUNK, hd), lambda c: (c, 0))
    gspec = pl.BlockSpec((DN_CHUNK, LANES), lambda c: (c, 0))

    def body(q_ref, k_ref, v_ref, gb_ref, z_ref, on_ref, og_ref, st_ref, tm_ref, state):
        @pl.when(pl.program_id(0) == 0)
        def _():
            state[...] = jnp.zeros_like(state)

        s0 = state[...]
        st_ref[0] = s0
        o, s1, tm = _delta_chunk(*_split_heads(q_ref, k_ref, v_ref, gb_ref[...], heads), s0)
        og = _gated_out(o, _head_blocks(z_ref, heads), on_ref[...])
        for h in range(heads):
            og_ref[:, h * DN_HEAD_DIM:(h + 1) * DN_HEAD_DIM] = og[h].astype(og_ref.dtype)
        state[...] = s1
        tm_ref[0] = tm

    return pl.pallas_call(
        body, name="dn_delta_fwd", grid=(n,),
        in_specs=[blk, blk, blk, gspec, pl.BlockSpec((DN_CHUNK, hd), lambda c: (c, 3)),
                  pl.BlockSpec(onorm.shape, lambda c: (0, 0))],
        out_specs=[blk, pl.BlockSpec((1, heads, DN_HEAD_DIM, DN_HEAD_DIM), lambda c: (c, 0, 0, 0)),
                   pl.BlockSpec((1, heads, DN_CHUNK, DN_CHUNK), lambda c: (c, 0, 0, 0))],
        out_shape=[jax.ShapeDtypeStruct((s, hd), BF16),
                   jax.ShapeDtypeStruct((n, heads, DN_HEAD_DIM, DN_HEAD_DIM), F32),
                   jax.ShapeDtypeStruct((n, heads, DN_CHUNK, DN_CHUNK), F32)],
        scratch_shapes=[pltpu.VMEM((heads, DN_HEAD_DIM, DN_HEAD_DIM), F32)],
        compiler_params=pltpu.CompilerParams(dimension_semantics=("arbitrary",)),
    )(q, k, v, gb, qkvz, onorm)


def _delta_bwd(q, k, v, gb, qkvz, onorm, states, tms, dog, heads):
    s, hd = q.shape
    n = s // DN_CHUNK
    blk = pl.BlockSpec((DN_CHUNK, hd), lambda c: (n - 1 - c, 0))
    gspec = pl.BlockSpec((DN_CHUNK, LANES), lambda c: (n - 1 - c, 0))
    sspec = pl.BlockSpec((1, heads, DN_HEAD_DIM, DN_HEAD_DIM), lambda c: (n - 1 - c, 0, 0, 0))
    tspec = pl.BlockSpec((1, heads, DN_CHUNK, DN_CHUNK), lambda c: (n - 1 - c, 0, 0, 0))
    nspec = pl.BlockSpec(onorm.shape, lambda c: (0, 0))

    def body(q_ref, k_ref, v_ref, gb_ref, z_ref, on_ref, st_ref, tm_ref, dog_ref,
             dq_ref, dk_ref, dv_ref, dgb_ref, dz_ref, don_ref, dstate):
        @pl.when(pl.program_id(0) == 0)
        def _():
            dstate[...] = jnp.zeros_like(dstate)
            don_ref[...] = jnp.zeros_like(don_ref)

        gbv = gb_ref[...]
        tm = tm_ref[0]

        def chunk(qh, kh, vh, gcol, bcol, s0, zh, on):
            o, s1, _ = _delta_chunk(qh, kh, vh, gcol, bcol, s0, tm)
            return _gated_out(o, zh, on), s1

        _, vjp = jax.vjp(chunk, *_split_heads(q_ref, k_ref, v_ref, gbv, heads), st_ref[0],
                         _head_blocks(z_ref, heads), on_ref[...])
        dq, dk, dv, dg, db, ds0, dz, don = vjp((_head_blocks(dog_ref, heads).astype(F32), dstate[...]))
        dstate[...] = ds0
        don_ref[...] += don
        lane = lax.broadcasted_iota(jnp.int32, gbv.shape, 1)
        dgb = jnp.zeros(gbv.shape, F32)
        for h in range(heads):
            sl = slice(h * DN_HEAD_DIM, (h + 1) * DN_HEAD_DIM)
            dq_ref[:, sl] = dq[h]
            dk_ref[:, sl] = dk[h]
            dv_ref[:, sl] = dv[h]
            dz_ref[:, sl] = dz[h]
            dgb = dgb + jnp.where(lane == h, db[h], 0.0) + jnp.where(lane == heads + h, dg[h], 0.0)
        dgb_ref[...] = dgb

    return pl.pallas_call(
        body, name="dn_delta_bwd", grid=(n,),
        in_specs=[blk, blk, blk, gspec, pl.BlockSpec((DN_CHUNK, hd), lambda c: (n - 1 - c, 3)), nspec,
                  sspec, tspec, blk],
        out_specs=[blk, blk, blk, gspec, blk, nspec],
        out_shape=[jax.ShapeDtypeStruct((s, hd), F32)] * 3 + [jax.ShapeDtypeStruct((s, LANES), F32),
                                                              jax.ShapeDtypeStruct((s, hd), F32),
                                                              jax.ShapeDtypeStruct(onorm.shape, F32)],
        scratch_shapes=[pltpu.VMEM((heads, DN_HEAD_DIM, DN_HEAD_DIM), F32)],
        compiler_params=pltpu.CompilerParams(dimension_semantics=("arbitrary",)),
    )(q, k, v, gb, qkvz, onorm, states, tms, dog)


def _dev_index(px, py, pc):
    return 4 * px + 2 * py + pc


def _all_gather(arrs, name):
    n = len(arrs)

    def body(*refs):
        xs, outs = refs[:n], refs[n:2 * n]
        send_sems, recv_sems, local_sems = refs[2 * n:]
        x, y, c = lax.axis_index("x"), lax.axis_index("y"), lax.axis_index("c")
        me, sibling = (x, y, c), (x, y, 1 - c)
        chips = [(1 - x, y), (x, 1 - y), (1 - x, 1 - y)]

        def copy(a, kk, block, to, src=None):
            dst = outs[a].at[_dev_index(*block)]
            return pltpu.make_async_remote_copy(
                src_ref=dst if src is None else src, dst_ref=dst,
                send_sem=send_sems.at[a * 7 + kk], recv_sem=recv_sems.at[a * 7 + kk],
                device_id=to, device_id_type=MESH_IDS)

        mine = [pltpu.make_async_copy(xs[a], outs[a].at[_dev_index(*me)], local_sems.at[a]) for a in range(n)]
        for cp in mine:
            cp.start()
        first = []
        for a in range(n):
            first.append(copy(a, 0, me, sibling, src=xs[a]))
            first += [copy(a, 1 + j, me, (*chip, c), src=xs[a]) for j, chip in enumerate(chips)]
        for cp in first:
            cp.start()
        passed = []
        for j, chip in enumerate(chips):
            for a in range(n):
                copy(a, 1 + j, (*chip, c), me).wait_recv()
                fwd = copy(a, 4 + j, (*chip, c), sibling)
                fwd.start()
                passed.append(fwd)
        for a in range(n):
            copy(a, 0, sibling, me).wait_recv()
        for j, chip in enumerate(chips):
            for a in range(n):
                copy(a, 4 + j, (*chip, 1 - c), me).wait_recv()
        for cp in first + passed:
            cp.wait_send()
        for cp in mine:
            cp.wait()

    hbm = pl.BlockSpec(memory_space=pltpu.HBM)
    res = pl.pallas_call(
        body, name=name,
        in_specs=[hbm] * n, out_specs=[hbm] * n,
        out_shape=[jax.ShapeDtypeStruct((N_DEV,) + a.shape, a.dtype) for a in arrs],
        scratch_shapes=[pltpu.SemaphoreType.DMA((7 * n,)), pltpu.SemaphoreType.DMA((7 * n,)),
                        pltpu.SemaphoreType.DMA((n,))],
    )(*arrs)
    return list(res)


_FLIPS = ((0, 0, 1), (1, 0, 0), (0, 1, 0), (1, 1, 0), (1, 0, 1), (0, 1, 1), (1, 1, 1))
_HBM_SPEC = pl.BlockSpec(memory_space=pltpu.HBM)
_SEM_SPEC = pl.BlockSpec(memory_space=pltpu.SEMAPHORE)
_ANY_SPEC = pl.BlockSpec(memory_space=pl.ANY)
_DATAFLOW = pltpu.SideEffectType.DATAFLOW_SIDE_EFFECTING
TOKEN_SHAPE = (8, LANES)


def _mesh_me():
    return lax.axis_index("x"), lax.axis_index("y"), lax.axis_index("c")


def _flipped(me, f):
    return tuple(1 - v if fl else v for v, fl in zip(me, f))


def _exchange_copies(xs, lands, send_sems, recv_sems, scatter, landed):
    me = _mesh_me()
    cps = []
    for kk, f in enumerate(_FLIPS):
        p = _flipped(me, f)
        for a in range(len(xs)):
            cps.append(pltpu.make_async_remote_copy(
                src_ref=xs[a].at[_dev_index(*p)] if scatter else xs[a],
                dst_ref=lands[a].at[_dev_index(*(p if landed else me))],
                send_sem=send_sems.at[a * 7 + kk], recv_sem=recv_sems.at[a * 7 + kk],
                device_id=p, device_id_type=MESH_IDS))
    return cps


def _exchange_start(srcs, lands, scatter, name, after=()):
    n = len(srcs)

    n_after = len(after)

    def body(*refs):
        xs, ls = refs[:n], refs[n:2 * n]
        send_sems, recv_sems = refs[2 * n + n_after], refs[2 * n + n_after + 1]
        token = refs[-1]
        for cp in _exchange_copies(xs, ls, send_sems, recv_sems, scatter, landed=False):
            cp.start()
        token[...] = jnp.zeros_like(token)

    operands = [pltpu.with_memory_space_constraint(a, pltpu.HBM) for a in list(srcs) + list(lands)]
    res = pl.pallas_call(
        body, name=name,
        in_specs=[_HBM_SPEC] * (2 * n) + [_ANY_SPEC] * len(after),
        out_specs=[_SEM_SPEC, _SEM_SPEC] + [_HBM_SPEC] * (2 * n) + [pl.BlockSpec(memory_space=pltpu.VMEM)],
        out_shape=[pltpu.SemaphoreType.DMA((7 * n,)), pltpu.SemaphoreType.DMA((7 * n,))]
        + [pltpu.HBM(a.shape, a.dtype) for a in operands] + [jax.ShapeDtypeStruct(TOKEN_SHAPE, F32)],
        input_output_aliases={i: 2 + i for i in range(2 * n)},
        compiler_params=pltpu.CompilerParams(has_side_effects=_DATAFLOW),
    )(*operands, *after)
    return (res[0], res[1], list(res[2:2 + n]), list(res[2 + n:2 + 2 * n]), scatter, name), res[-1]


def _exchange_wait(handle, after):
    send_sems, recv_sems, srcs, lands, scatter, name = handle
    n = len(srcs)
    n_after = len(after)

    def body(*refs):
        xs, ls = refs[:n], refs[n:2 * n]
        send_sems_ref, recv_sems_ref = refs[2 * n], refs[2 * n + 1]
        for cp in _exchange_copies(xs, ls, send_sems_ref, recv_sems_ref, scatter, landed=True):
            cp.wait_send()
            cp.wait_recv()

    res = pl.pallas_call(
        body, name=name + "_wait",
        in_specs=[_HBM_SPEC] * (2 * n) + [_SEM_SPEC, _SEM_SPEC] + [_ANY_SPEC] * n_after,
        out_specs=[_HBM_SPEC] * (2 * n),
        out_shape=[pltpu.HBM(a.shape, a.dtype) for a in srcs + lands],
        input_output_aliases={i: i for i in range(2 * n)},
        compiler_params=pltpu.CompilerParams(has_side_effects=_DATAFLOW),
    )(*srcs, *lands, send_sems, recv_sems, *after)
    return list(res[:n]), list(res[n:])


def _slot_sum(g, name, tr):
    _, r, c = g.shape
    tr = min(tr, r)
    assert r % tr == 0

    def body(g_ref, o_ref):
        acc = g_ref[0].astype(F32)
        for s in range(1, N_DEV):
            acc = acc + g_ref[s].astype(F32)
        o_ref[...] = acc

    return pl.pallas_call(
        body, name=name, grid=(r // tr,),
        in_specs=[pl.BlockSpec((N_DEV, tr, c), lambda i: (0, i, 0))],
        out_specs=pl.BlockSpec((tr, c), lambda i: (i, 0)),
        out_shape=jax.ShapeDtypeStruct((r, c), F32),
        compiler_params=pltpu.CompilerParams(dimension_semantics=("parallel",)),
    )(g)


def _adam_update(w, gg, m, v):
    c1 = 1.0 / (1.0 - ADAM_B1 ** ADAM_STEP)
    c2 = 1.0 / (1.0 - ADAM_B2 ** ADAM_STEP)
    nm = ADAM_B1 * m + (1.0 - ADAM_B1) * gg
    nv = ADAM_B2 * v + (1.0 - ADAM_B2) * (gg * gg)
    return -ADAM_LR * ((nm * c1) / (jnp.sqrt(nv * c2) + ADAM_EPS) + ADAM_WD * w), nm, nv


def _adamw_reduce(me, recvs, owns, w, m, v, name, tr=256):
    nl, r, c = w.shape
    assert len(recvs) == nl and len(owns) == nl
    tr = min(tr, r)
    if r % tr == 0:
        tc, nblk = c, r // tr
        at = lambda i: (i, 0)
    else:
        tr, tc = r, min(c, 4 * LANES)
        assert c % tc == 0
        nblk = c // tc
        at = lambda i: (0, i)

    def parked(li, l, i):
        return jnp.where(l < li, 0, jnp.where(l > li, nblk - 1, i))

    def recv_spec(li):
        return pl.BlockSpec((N_DEV, tr, tc), lambda l, i, me_ref: (0, *at(parked(li, l, i))))

    def own_spec(li):
        return pl.BlockSpec((None, tr, tc), lambda l, i, me_ref: (me_ref[0], *at(parked(li, l, i))))

    def body(me_ref, *refs):
        rrefs, orefs = refs[:nl], refs[nl:2 * nl]
        w_ref, m_ref, v_ref, g_ref, d_ref, nm_ref, nv_ref = refs[2 * nl:]
        l = pl.program_id(0)

        def of_layer(vals):
            out = vals[0]
            for li in range(1, nl):
                out = jnp.where(l == li, vals[li], out)
            return out

        own = of_layer([o[...].astype(F32) for o in orefs])
        gg = None
        for s in range(N_DEV):
            slot = jnp.where(me_ref[0] == s, own, of_layer([rr[s].astype(F32) for rr in rrefs]))
            gg = slot if gg is None else gg + slot
        g_ref[...] = gg
        d_ref[...], nm_ref[...], nv_ref[...] = _adam_update(w_ref[...], gg, m_ref[...], v_ref[...])

    spec = pl.BlockSpec((None, tr, tc), lambda l, i, me_ref: (l, *at(i)))
    return pl.pallas_call(
        body, name=name,
        grid_spec=pltpu.PrefetchScalarGridSpec(
            num_scalar_prefetch=1, grid=(nl, nblk),
            in_specs=[recv_spec(li) for li in range(nl)] + [own_spec(li) for li in range(nl)] + [spec] * 3,
            out_specs=[spec] * 4),
        out_shape=[jax.ShapeDtypeStruct((nl, r, c), F32)] * 4,
        compiler_params=pltpu.CompilerParams(dimension_semantics=("arbitrary", "arbitrary")),
    )(me, *recvs, *owns, w, m, v)


def _adamw(w, g, m, v, name, tr=256):
    r, c = w.shape
    tr = min(tr, r)
    assert r % tr == 0

    def body(w_ref, g_ref, m_ref, v_ref, d_ref, nm_ref, nv_ref):
        d_ref[...], nm_ref[...], nv_ref[...] = _adam_update(w_ref[...], g_ref[...], m_ref[...], v_ref[...])

    spec = pl.BlockSpec((tr, c), lambda i: (i, 0))
    return pl.pallas_call(
        body, name=name, grid=(r // tr,), in_specs=[spec] * 4, out_specs=[spec] * 3,
        out_shape=[jax.ShapeDtypeStruct((r, c), F32)] * 3,
        compiler_params=pltpu.CompilerParams(dimension_semantics=("parallel",)),
    )(w, g, m, v)


def _rms_fwd(x, g, name, ts=1024, after=()):
    s, d = x.shape

    def fn(first, last, xv, gv):
        return [_rms(xv, gv[...])], []

    return _rowwise(fn, n_rows=s, ts=ts, name=name, rows=[(x, 0, d)], vecs=[g], row_outs=[(d, BF16)],
                    after=after)[0]


def _rms_bwd(x, dn, dres, g, name, ts=256):
    s, d = x.shape

    def fn(first, last, xv, dnv, drv, gv):
        _, vjp = jax.vjp(_rms, xv, gv[...])
        dx, dg = vjp(dnv.astype(F32))
        return [drv + dx], [dg]

    return _rowwise(fn, n_rows=s, ts=ts, name=name, rows=[(x, 0, d), (dn, 0, d), (dres, 0, d)], vecs=[g],
                    row_outs=[(d, F32)], acc_outs=[(1, d)])


def _dn_pre_fwd(qkvz, ba, wconv, alog, dt, heads, ts=512):
    s = qkvz.shape[0]
    d3 = wconv.shape[1]
    d = d3 // 3

    def fn(first, last, xc, bav, xp, wv, av, dv):
        xext = jnp.concatenate([jnp.where(first, 0.0, xp), xc], axis=0)
        cv = _causal_conv(xext, wv, DN_CONV, DN_HALO - (DN_CONV - 1), xc.shape[0])
        return list(_dn_point(cv, bav, av[...], dv[...], heads)), []

    return _rowwise(fn, n_rows=s, ts=ts, name="dn_pre_fwd", rows=[(qkvz, 0, d3), (ba, 0, LANES)],
                    prevs=[(qkvz, 0, d3, DN_HALO)], vecs=[wconv, alog, dt],
                    row_outs=[(d, F32), (d, F32), (d, F32), (LANES, F32)])


def _dn_pre_bwd(qkvz, ba, wconv, alog, dt, dq, dk, dv, dgb, dz, heads, ts=256):
    s = qkvz.shape[0]
    d3 = wconv.shape[1]
    d = d3 // 3
    lead = DN_HALO - (DN_CONV - 1)

    def fn(first, last, xc, bac, dqc, dkc, dvc, dgbc, dzc, xp, xn, ban, dqn, dkn, dvn, dgbn, wv, av, dtv):
        n = xc.shape[0]
        ext = lambda cur, nxt: jnp.concatenate([cur, nxt], axis=0)
        live = lambda nxt: jnp.where(last, 0.0, nxt)
        xall = jnp.concatenate([jnp.where(first, 0.0, xp), xc, live(xn)], axis=0)
        cv = _causal_conv(xall, wv, DN_CONV, lead, n + DN_HALO)
        (_, _, _, gbv), vjp = jax.vjp(lambda c, b: _dn_point(c, b, av[...], dtv[...], heads), cv, ext(bac, ban))
        dc, dba = vjp((ext(dqc, live(dqn)), ext(dkc, live(dkn)), ext(dvc, live(dvn)), ext(dgbc, live(dgbn))))
        dx = None
        dw = []
        for j in range(DN_CONV):
            term = _shift_rows(dc, DN_CONV - 1 - j)[:n] * wv[j:j + 1, :]
            dx = term if dx is None else dx + term
            dw.append(_colsum(dc[:n] * _shift_rows(xall, lead + j)[:n]))
        dba = dba[:n]
        return ([jnp.concatenate([dx, dzc], axis=-1), dba],
                [_stack_rows(dw, DN_CONV), _colsum(dgbc * gbv[:n]), _colsum(dba)])

    return _rowwise(fn, n_rows=s, ts=ts, name="dn_pre_bwd",
                    rows=[(qkvz, 0, d3), (ba, 0, LANES), (dq, 0, d), (dk, 0, d), (dv, 0, d), (dgb, 0, LANES),
                          (dz, 0, d)],
                    prevs=[(qkvz, 0, d3, DN_HALO)],
                    nexts=[(qkvz, 0, d3, DN_HALO), (ba, 0, LANES, DN_HALO), (dq, 0, d, DN_HALO),
                           (dk, 0, d, DN_HALO), (dv, 0, d, DN_HALO), (dgb, 0, LANES, DN_HALO)],
                    vecs=[wconv, alog, dt],
                    row_outs=[(4 * d, BF16), (LANES, BF16)], acc_outs=[(DN_CONV, d3), (1, LANES), (1, LANES)])


def _cv_mid_fwd(u, wdw, bdw, lng, lnb, ts=512):
    s = u.shape[0]
    d = u.shape[1] // 2

    def fn(first, last, uc, up, wv, bv, gv, lbv):
        uext = jnp.concatenate([jnp.where(first, 0.0, up), uc], axis=0)
        glu = uext[:, :d] * _sigmoid(uext[:, d:])
        c = _causal_conv(glu, wv, CV_WIDTH, CV_HALO - (CV_WIDTH - 1), uc.shape[0]) + bv[...]
        return [c, _ln_silu(c, gv[...], lbv[...])], []

    return _rowwise(fn, n_rows=s, ts=ts, name="cv_mid_fwd", rows=[(u, 0, 2 * d)], prevs=[(u, 0, 2 * d, CV_HALO)],
                    vecs=[wdw, bdw, lng, lnb], row_outs=[(d, F32), (d, BF16)])


def _cv_mid_bwd2(dc, u, wdw, ts=512):
    s, d = dc.shape

    def fn(first, last, dcc, uc, up, dcn, wv):
        n = dcc.shape[0]
        dcext = jnp.concatenate([dcc, jnp.where(last, 0.0, dcn)], axis=0)
        uext = jnp.concatenate([jnp.where(first, 0.0, up), uc], axis=0)
        glu = uext[:, :d] * _sigmoid(uext[:, d:])
        dglu = None
        dw = []
        for j in range(CV_WIDTH):
            term = _shift_rows(dcext, CV_WIDTH - 1 - j)[:n] * wv[j:j + 1, :]
            dglu = term if dglu is None else dglu + term
            dw.append(_colsum(dcc * _shift_rows(glu, CV_HALO - (CV_WIDTH - 1) + j)[:n]))
        u1, sg = uc[:, :d], _sigmoid(uc[:, d:])
        du = jnp.concatenate([dglu * sg, dglu * u1 * sg * (1.0 - sg)], axis=-1)
        return [du], [_stack_rows(dw, CV_HALO), _colsum(du)]

    return _rowwise(fn, n_rows=s, ts=ts, name="cv_mid_bwd2", rows=[(dc, 0, d), (u, 0, 2 * d)],
                    prevs=[(u, 0, 2 * d, CV_HALO)], nexts=[(dc, 0, d, CV_HALO)], vecs=[wdw],
                    row_outs=[(2 * d, BF16)], acc_outs=[(CV_HALO, d), (1, 2 * d)])


def _attn_fwd(q, k, v, name, ts=2048):
    s, d = q.shape

    def fn(first, last, qv, kv, vv):
        return [_attn_tile(qv.astype(F32), kv[...].astype(F32), vv[...].astype(F32))], []

    return _rowwise(fn, n_rows=s, ts=ts, name=name, rows=[(q, 0, d)], vecs=[k, v], row_outs=[(d, BF16)])[0]


def _attn_bwd(q, k, v, do, name, ts=2048):
    s, d = q.shape
    m = k.shape[0]

    def fn(first, last, qv, dov, kv, vv):
        _, vjp = jax.vjp(_attn_tile, qv.astype(F32), kv[...].astype(F32), vv[...].astype(F32))
        dq, dk, dv = vjp(dov.astype(F32))
        return [dq], [dk, dv]

    return _rowwise(fn, n_rows=s, ts=ts, name=name, rows=[(q, 0, d), (do, 0, d)], vecs=[k, v],
                    row_outs=[(d, BF16)], acc_outs=[(m, d), (m, d)])


def _pad_lanes(a, off=0):
    r, n = a.shape
    return jnp.pad(a, ((0, 0), (off, LANES - off - n)))


def _local_step(x, mem, tgt, w, fetch=None, emit=None, first_after=()):
    s, d = x.shape
    heads = d // DN_HEAD_DIM
    g = {}
    if fetch is None:
        fetch = lambda group, after: None
    if emit is None:
        emit = lambda group, grads: ()

    def add_res_rms(acc, res, gain):
        h = res + acc
        return h, _rms(h, gain)

    def rms_bwd_epi(acc, hx, dres, gain):
        _, vjp = jax.vjp(_rms, hx, gain)
        dx, dg = vjp(acc)
        return dres + dx, dg

    w_int = w["dn_w_in"][0]
    assert w_int.shape[0] == 4 * d + 2 * heads
    w_bat = jnp.pad(w_int[4 * d:], ((0, LANES - 2 * heads), (0, 0)))
    dn_norm = w["dn_norm"]
    alog = _pad_lanes(w["dn_a_log"], heads)
    dtb = _pad_lanes(w["dn_dt_bias"], heads)
    wconv = w["dn_w_conv"][0]
    n0 = _rms_fwd(x, dn_norm, "dn_rms", after=first_after)
    qkvz = _matmul(n0, w_int, "nt", [F32], name="dn_in_proj", b_rows=4 * d, tn=MM_DEEP)
    ba = _matmul(n0, w_bat, "nt", [F32], name="dn_in_proj_ba")
    q, k, v, gb = _dn_pre_fwd(qkvz, ba, wconv, alog, dtb, heads)
    og, states, tms = _delta_fwd(q, k, v, gb, qkvz, w["dn_out_norm"], heads)
    fetch(1, [og])
    h1, nq0 = _matmul(og, w["dn_w_out"][0], "nn", [F32, BF16], name="dn_out_proj", epi=add_res_rms,
                      mn_extras=[x], row_extras=[w["xa_norm"][0:1]], slab=EPI_SLAB)

    def xattn_fwd(h, nq, layer, next_gain):
        qx = _matmul(nq, w["xa_w_q"][layer], "nn", [BF16], name=f"xa{layer}_q")
        mn = _rms_fwd(mem, w["xa_mem_norm"][layer:layer + 1], f"xa{layer}_mem_rms")
        kv = _matmul(mn, w["xa_w_kv"][layer], "nn", [BF16], name=f"xa{layer}_kv")
        kx, vx = kv[:, :d], kv[:, d:]
        ox = _attn_fwd(qx, kx, vx, f"xa{layer}_attn")
        hn, nn = _matmul(ox, w["xa_w_o"][layer], "nn", [F32, BF16], name=f"xa{layer}_o", epi=add_res_rms,
                         mn_extras=[h], row_extras=[next_gain], slab=EPI_SLAB)
        return hn, nn, (h, nq, qx, mn, kx, vx, ox)

    def sq_relu(t):
        r = jnp.maximum(t.astype(F32), 0.0)
        return r * r

    def loss_epi(acc, res, target, gain):
        def cols(hh, gg):
            e = _rms(hh, gg) - target
            return _colsum(e * e) * (0.5 / d)

        per_col, vjp = jax.vjp(cols, res + acc, gain)
        dhx, dgain = vjp(jnp.ones_like(per_col))
        return dhx, dgain, per_col

    def mlp_fwd(h, nm, layer, next_gain):
        u = _matmul(nm, w["mlp_w_up"][layer], "nn", [BF16], name=f"mlp{layer}_up", tn=MM_DEEP)
        if next_gain is None:
            hn, *nn = _matmul(u, w["mlp_w_down"][layer], "nn", [F32], name=f"mlp{layer}_down_loss",
                              epi=loss_epi, mn_extras=[h, tgt], row_extras=[w["final_norm"].reshape(1, d)],
                              n_rowsum=2, tk=MM_DEEP, slab=EPI_SLAB, a_pre=sq_relu)
        else:
            hn, nn = _matmul(u, w["mlp_w_down"][layer], "nn", [F32, BF16], name=f"mlp{layer}_down",
                             epi=add_res_rms, mn_extras=[h], row_extras=[next_gain], tk=MM_DEEP, slab=EPI_SLAB,
                             a_pre=sq_relu)
        return hn, nn, (h, nm, u)

    h2, nm0, xa0 = xattn_fwd(h1, nq0, 0, w["mlp_norm"][0:1])
    fetch(2, [h2])
    h3, n1, mlp0 = mlp_fwd(h2, nm0, 0, w["cv_norm"])

    u_cv = _matmul(n1, w["cv_w_pw1"][0], "nn", [F32], name="cv_pw1", epi=lambda acc, b: (acc + b,),
                   row_extras=[w["cv_b_pw1"]], tn=MM_DEEP)
    wdw = jnp.pad(w["cv_w_dw"][0], ((0, CV_HALO - CV_WIDTH), (0, 0)))
    c_cv, s_cv = _cv_mid_fwd(u_cv, wdw, w["cv_b_dw"], w["cv_ln_g"], w["cv_ln_b"])
    h4, nq1 = _matmul(s_cv, w["cv_w_pw2"][0], "nn", [F32, BF16], name="cv_pw2",
                      epi=lambda acc, res, b, gain: add_res_rms(acc + b, res, gain), mn_extras=[h3],
                      row_extras=[w["cv_b_pw2"], w["xa_norm"][1:2]], slab=EPI_SLAB)
    fetch(3, [h4])
    h5, nm1, xa1 = xattn_fwd(h4, nq1, 1, w["mlp_norm"][1:2])
    fetch(4, [h5])
    dh, (g_fn, loss_cols), mlp1 = mlp_fwd(h5, nm1, 1, None)
    g["final_norm"] = g_fn.reshape(d)
    loss = jnp.sum(loss_cols, axis=1, keepdims=True)

    def mlp_bwd(dh, layer, saved, after=()):
        h, nm, u = saved
        du = _matmul(dh, w["mlp_w_down"][layer], "nt", [BF16], name=f"mlp{layer}_down_dx", after=after,
                     epi=lambda acc, uu: (acc * 2.0 * jnp.maximum(uu.astype(F32), 0.0),), mn_extras=[u])
        gdown = _matmul(u, dh, "tn", [BF16], name=f"mlp{layer}_down_dw", tm=MM_DEEP, a_pre=sq_relu)
        dhn, gn = _matmul(du, w["mlp_w_up"][layer], "nt", [F32], name=f"mlp{layer}_up_dx", epi=rms_bwd_epi,
                          mn_extras=[h, dh], row_extras=[w["mlp_norm"][layer:layer + 1]], n_rowsum=1,
                          slab=EPI_SLAB, tk=MM_DEEP)
        gup = _matmul(nm, du, "tn", [BF16], name=f"mlp{layer}_up_dw", out_dm=True, tk=MM_DEEP, tn=MM_DEEP)
        return dhn, gup, gdown, gn

    def xattn_bwd(dh, layer, saved):
        h, nq, qx, mn, kx, vx, ox = saved
        dox = _matmul(dh, w["xa_w_o"][layer], "nt", [BF16], name=f"xa{layer}_o_dx")
        go = _matmul(ox, dh, "tn", [BF16], name=f"xa{layer}_o_dw", tk=MM_DEEP)
        dqx, dkx, dvx = _attn_bwd(qx, kx, vx, dox, f"xa{layer}_attn_bwd")

        def epi(acc, hx, dres, gain):
            dhx, dg = rms_bwd_epi(acc, hx, dres, gain)
            return dhx, dg, _colsum(dhx)

        dhn, gn, dh_cols = _matmul(dqx, w["xa_w_q"][layer], "nt", [F32], name=f"xa{layer}_q_dx", epi=epi,
                                   mn_extras=[h, dh], row_extras=[w["xa_norm"][layer:layer + 1]], n_rowsum=2,
                                   slab=EPI_SLAB)
        gq = _matmul(nq, dqx, "tn", [BF16], name=f"xa{layer}_q_dw", tk=MM_DEEP)
        dkv = jnp.concatenate([dkx, dvx], axis=-1)
        gkv = _matmul(mn, dkv, "tn", [BF16], name=f"xa{layer}_kv_dw", out_dm=True)
        dmn = _matmul(dkv, w["xa_w_kv"][layer], "nt", [F32], name=f"xa{layer}_kv_dx", tk=MM_DEEP)
        _, gmem = _rms_bwd(mem, dmn, dmn, w["xa_mem_norm"][layer:layer + 1], f"xa{layer}_mem_rms_bwd")
        return dhn, gq, gkv, go, gn, gmem, dh_cols

    dh, gup1, gdown1, gmn1 = mlp_bwd(dh, 1, mlp1)
    dh, gq1, gkv1, go1, gxn1, gmem1, g_b2 = xattn_bwd(dh, 1, xa1)
    g.update(mlp_w_up=[None, gup1], mlp_w_down=[None, gdown1], xa_w_q=[None, gq1], xa_w_kv=[None, gkv1],
             xa_w_o=[None, go1])
    tok = emit(3, g)

    def ln_bwd_epi(acc, cx, gain, bias):
        _, vjp = jax.vjp(_ln_silu, cx, gain, bias)
        dc, dg, db = vjp(acc)
        return dc, dg, db, _colsum(dc)

    dc_cv, g_lng, g_lnb, g_bdw = _matmul(dh, w["cv_w_pw2"][0], "nt", [F32], name="cv_pw2_dx", after=tok,
                                        epi=ln_bwd_epi, mn_extras=[c_cv],
                                        row_extras=[w["cv_ln_g"], w["cv_ln_b"]], n_rowsum=3, slab=EPI_SLAB)
    g["cv_w_pw2"] = [_matmul(s_cv, dh, "tn", [BF16], name="cv_pw2_dw", tk=MM_DEEP)]
    du_cv, g_wdw, g_b1 = _cv_mid_bwd2(dc_cv, u_cv, wdw)
    g["cv_w_pw1"] = [_matmul(n1, du_cv, "tn", [BF16], name="cv_pw1_dw", out_dm=True, tk=MM_DEEP)]
    dh, g_cvn = _matmul(du_cv, w["cv_w_pw1"][0], "nt", [F32], name="cv_pw1_dx", epi=rms_bwd_epi,
                        mn_extras=[h3, dh], row_extras=[w["cv_norm"]], n_rowsum=1, slab=EPI_SLAB, tk=MM_DEEP)
    g.update(cv_ln_g=g_lng, cv_ln_b=g_lnb, cv_b_dw=g_bdw, cv_b_pw2=g_b2, cv_b_pw1=g_b1, cv_norm=g_cvn,
             cv_w_dw=g_wdw[:CV_WIDTH][None])

    tok = emit(2, g)
    dh, gup0, gdown0, gmn0 = mlp_bwd(dh, 0, mlp0, after=tok)
    dh, gq0, gkv0, go0, gxn0, gmem0, _ = xattn_bwd(dh, 0, xa0)
    g["mlp_w_up"][0] = gup0
    g["mlp_w_down"][0] = gdown0
    g["mlp_norm"] = jnp.concatenate([gmn0, gmn1], axis=0)
    g["xa_w_q"][0] = gq0
    g["xa_w_kv"][0] = gkv0
    g["xa_w_o"][0] = go0
    g["xa_norm"] = jnp.concatenate([gxn0, gxn1], axis=0)
    g["xa_mem_norm"] = jnp.concatenate([gmem0, gmem1], axis=0)
    g["dn_w_out"] = [_matmul(og, dh, "tn", [BF16], name="dn_out_proj_dw", tk=MM_DEEP)]
    tok = emit(1, g)
    dog = _matmul(dh, w["dn_w_out"][0], "nt", [BF16], name="dn_out_proj_dx", after=tok)
    dq, dk, dv, dgb, dz, g_on = _delta_bwd(q, k, v, gb, qkvz, w["dn_out_norm"], states, tms, dog, heads)
    dqkvz, dba, g_wconv, g_alog, g_dt = _dn_pre_bwd(qkvz, ba, wconv, alog, dtb, dq, dk, dv, dgb, dz, heads)
    g_qkvzt = _matmul(dqkvz, n0, "tn", [BF16], name="dn_in_proj_dw", tk=MM_DEEP)
    g_bat = _matmul(dba, n0, "tn", [BF16], name="dn_in_proj_ba_dw", tk=MM_DEEP)
    g["dn_w_in"] = [jnp.concatenate([g_qkvzt, g_bat[:2 * heads]], axis=0)]
    g["dn_w_conv"] = g_wconv[None]
    tok = emit(0, g)
    dn0a = _matmul(dba, w_bat, "nn", [F32], name="dn_in_proj_ba_dx", after=tok)
    grad_x, g_dnn = _matmul(dqkvz, w_int, "nn", [F32], name="dn_in_proj_dx", b_rows=4 * d,
                            epi=lambda acc, part, hx, dres, gain: rms_bwd_epi(acc + part, hx, dres, gain),
                            mn_extras=[dn0a, x, dh], row_extras=[dn_norm], n_rowsum=1, slab=EPI_SLAB)
    g.update(dn_norm=g_dnn, dn_out_norm=g_on,
             dn_a_log=g_alog[:, heads:2 * heads], dn_dt_bias=g_dt[:, heads:2 * heads])
    return loss, grad_x, g


def _round_up(n, m):
    return (n + m - 1) // m * m


def _pack_rows(parts, cols, row_mult):
    lead = parts[0].shape[:-1]
    flat, offs, off = [], [], 0
    for p in parts:
        n = _round_up(p.shape[-1], cols)
        flat.append(jnp.pad(p, [(0, 0)] * len(lead) + [(0, n - p.shape[-1])]))
        offs.append(off)
        off += n
    total = _round_up(off, cols * row_mult)
    if total > off:
        flat.append(jnp.zeros(lead + (total - off,), parts[0].dtype))
    return jnp.concatenate(flat, axis=-1).reshape(lead + (total // cols, cols)), offs


def _unpack(packed, offs, shapes):
    lead = packed.shape[:-2]
    flat = packed.reshape(lead + (-1,))
    out = []
    for off, shp in zip(offs, shapes):
        n = 1
        for v in shp:
            n *= v
        out.append(flat[..., off:off + n].reshape(lead + tuple(shp)))
    return out


def kernel(x, mem, dn_norm, dn_w_in, dn_w_conv, dn_a_log, dn_dt_bias, dn_out_norm, dn_w_out, cv_norm, cv_w_pw1, cv_b_pw1, cv_w_dw, cv_b_dw, cv_ln_g, cv_ln_b, cv_w_pw2, cv_b_pw2, xa_norm, xa_mem_norm, xa_w_q, xa_w_kv, xa_w_o, mlp_norm, mlp_w_up, mlp_w_down, final_norm, loss_target, m_dn_norm, m_dn_w_in, m_dn_w_conv, m_dn_a_log, m_dn_dt_bias, m_dn_out_norm, m_dn_w_out, m_cv_norm, m_cv_w_pw1, m_cv_b_pw1, m_cv_w_dw, m_cv_b_dw, m_cv_ln_g, m_cv_ln_b, m_cv_w_pw2, m_cv_b_pw2, m_xa_norm, m_xa_mem_norm, m_xa_w_q, m_xa_w_kv, m_xa_w_o, m_mlp_norm, m_mlp_w_up, m_mlp_w_down, m_final_norm, v_dn_norm, v_dn_w_in, v_dn_w_conv, v_dn_a_log, v_dn_dt_bias, v_dn_out_norm, v_dn_w_out, v_cv_norm, v_cv_w_pw1, v_cv_b_pw1, v_cv_w_dw, v_cv_b_dw, v_cv_ln_g, v_cv_ln_b, v_cv_w_pw2, v_cv_b_pw2, v_xa_norm, v_xa_mem_norm, v_xa_w_q, v_xa_w_kv, v_xa_w_o, v_mlp_norm, v_mlp_w_up, v_mlp_w_down, v_final_norm):
    wsh = dict(dn_norm=dn_norm, dn_w_in=dn_w_in, dn_w_conv=dn_w_conv, dn_a_log=dn_a_log, dn_dt_bias=dn_dt_bias, dn_out_norm=dn_out_norm, dn_w_out=dn_w_out, cv_norm=cv_norm, cv_w_pw1=cv_w_pw1, cv_b_pw1=cv_b_pw1, cv_w_dw=cv_w_dw, cv_b_dw=cv_b_dw, cv_ln_g=cv_ln_g, cv_ln_b=cv_ln_b, cv_w_pw2=cv_w_pw2, cv_b_pw2=cv_b_pw2, xa_norm=xa_norm, xa_mem_norm=xa_mem_norm, xa_w_q=xa_w_q, xa_w_kv=xa_w_kv, xa_w_o=xa_w_o, mlp_norm=mlp_norm, mlp_w_up=mlp_w_up, mlp_w_down=mlp_w_down, final_norm=final_norm)
    msh = dict(dn_norm=m_dn_norm, dn_w_in=m_dn_w_in, dn_w_conv=m_dn_w_conv, dn_a_log=m_dn_a_log, dn_dt_bias=m_dn_dt_bias, dn_out_norm=m_dn_out_norm, dn_w_out=m_dn_w_out, cv_norm=m_cv_norm, cv_w_pw1=m_cv_w_pw1, cv_b_pw1=m_cv_b_pw1, cv_w_dw=m_cv_w_dw, cv_b_dw=m_cv_b_dw, cv_ln_g=m_cv_ln_g, cv_ln_b=m_cv_ln_b, cv_w_pw2=m_cv_w_pw2, cv_b_pw2=m_cv_b_pw2, xa_norm=m_xa_norm, xa_mem_norm=m_xa_mem_norm, xa_w_q=m_xa_w_q, xa_w_kv=m_xa_w_kv, xa_w_o=m_xa_w_o, mlp_norm=m_mlp_norm, mlp_w_up=m_mlp_w_up, mlp_w_down=m_mlp_w_down, final_norm=m_final_norm)
    vsh = dict(dn_norm=v_dn_norm, dn_w_in=v_dn_w_in, dn_w_conv=v_dn_w_conv, dn_a_log=v_dn_a_log, dn_dt_bias=v_dn_dt_bias, dn_out_norm=v_dn_out_norm, dn_w_out=v_dn_w_out, cv_norm=v_cv_norm, cv_w_pw1=v_cv_w_pw1, cv_b_pw1=v_cv_b_pw1, cv_w_dw=v_cv_w_dw, cv_b_dw=v_cv_b_dw, cv_ln_g=v_cv_ln_g, cv_ln_b=v_cv_ln_b, cv_w_pw2=v_cv_w_pw2, cv_b_pw2=v_cv_b_pw2, xa_norm=v_xa_norm, xa_mem_norm=v_xa_mem_norm, xa_w_q=v_xa_w_q, xa_w_kv=v_xa_w_kv, xa_w_o=v_xa_w_o, mlp_norm=v_mlp_norm, mlp_w_up=v_mlp_w_up, mlp_w_down=v_mlp_w_down, final_norm=v_final_norm)

    big_axis = dict(BIG)
    for src in (wsh, msh, vsh):
        src["dn_w_in"] = jnp.swapaxes(src["dn_w_in"], 1, 2)
    big_axis["dn_w_in"] = 1

    small_pack, small_offs = _pack_rows([wsh[nm].reshape(-1) for nm in SMALL_SH], LANES, 8)
    w = {nm: [None] * wsh[nm].shape[0] for nm in big_axis}

    def put_weights(group, gathered):
        for (nm, layer), gth in zip(group, gathered):
            if big_axis[nm] == 1:
                w[nm][layer] = gth.reshape(N_DEV * gth.shape[1], gth.shape[2])
            else:
                w[nm][layer] = gth

    first = _all_gather([wsh[nm][layer].astype(BF16) for nm, layer in GATHER_GROUPS[0]] + [small_pack],
                        "weights_all_gather_0")
    put_weights(GATHER_GROUPS[0], first)
    me = _dev_index(*_mesh_me())
    gather_handles, tokens = {}, []
    for gi in range(1, len(GATHER_GROUPS)):
        shards = [wsh[nm][layer].astype(BF16) for nm, layer in GATHER_GROUPS[gi]]
        lands = [lax.dynamic_update_slice(lax.empty((N_DEV,) + s.shape, s.dtype), s[None], (me, 0, 0))
                 for s in shards]
        gather_handles[gi], tok = _exchange_start(shards, lands, False, f"weights_gather_{gi}",
                                                  after=[first[-1]] + tokens)
        tokens.append(tok)
    for nm, gth in zip(SMALL_SH, _unpack(first[-1], small_offs, [wsh[nm].shape for nm in SMALL_SH])):
        w[nm] = jnp.moveaxis(gth, 0, -2).reshape(gth.shape[1:-1] + (N_DEV * gth.shape[-1],))
    for nm in REPL:
        w[nm] = wsh[nm]

    def fetch(gi, after):
        put_weights(GATHER_GROUPS[gi], _exchange_wait(gather_handles[gi], after)[1])

    scatter_handles = {}

    def emit(gi, g):
        blocks = []
        for nm, layer in SCATTER_GROUPS[gi]:
            gw = g[nm][layer]
            if big_axis[nm] == 1:
                gw = gw.reshape(N_DEV, gw.shape[0] // N_DEV, gw.shape[1])
            blocks.append(gw)
        if gi == 0:
            gsmall_pack, _ = _pack_rows(
                [jnp.moveaxis(g[nm].reshape(g[nm].shape[:-1] + (N_DEV, -1)), -2, 0).reshape(N_DEV, -1)
                 for nm in SMALL_SH], LANES, 8)
            blocks.append(gsmall_pack)
        lands = [lax.empty(b.shape, b.dtype) for b in blocks]
        scatter_handles[gi], tok = _exchange_start(blocks, lands, True, f"grads_scatter_{gi}")
        return [tok]

    loss_part, grad_x, g = _local_step(x[0], mem[0], loss_target[0], w, fetch, emit, tokens)

    recv = {nm: [None] * wsh[nm].shape[0] for nm in big_axis}
    sent = {nm: [None] * wsh[nm].shape[0] for nm in big_axis}
    gsh, delta, new_m, new_v = {}, {}, {}, {}
    after = [grad_x]
    done = set()
    me_arr = me.astype(jnp.int32).reshape(1)
    def small_adamw(names, name):
        packs = []
        for src in (wsh, gsh, msh, vsh):
            pk, offs = _pack_rows([src[nm].reshape(-1) for nm in names], LANES, 8)
            packs.append(pk)
        outs = _adamw(*packs, name)
        for dst, pk in zip((delta, new_m, new_v), outs):
            for nm, val in zip(names, _unpack(pk, offs, [wsh[nm].shape for nm in names])):
                dst[nm] = val
        return outs[0]

    for gi in reversed(range(len(SCATTER_GROUPS))):
        if gi == 0:
            repl_pack, repl_offs = _pack_rows([g[nm].reshape(-1) for nm in REPL] + [loss_part[:, :1].reshape(-1)],
                                              LANES, 8)
            (repl_all,) = _all_gather([repl_pack], "repl_grads_all_gather")
            repl_red = _slot_sum(repl_all, "repl_grads_sum", SLOT_SUM_ROWS)
            *repl_vals, loss_sum = _unpack(repl_red, repl_offs, [wsh[nm].shape for nm in REPL] + [(1,)])
            for nm, val in zip(REPL, repl_vals):
                gsh[nm] = val
            after = after + [small_adamw(list(REPL), "adamw_repl")]
        sources, landed = _exchange_wait(scatter_handles[gi], after)
        for (nm, layer), src, r in zip(SCATTER_GROUPS[gi], sources, landed):
            sent[nm][layer], recv[nm][layer] = src, r
        if gi == 0:
            slot = lax.broadcasted_iota(jnp.int32, landed[-1].shape, 0)
            rsmall = jnp.where(slot == me, sources[-1], landed[-1])
        for nm in big_axis:
            if nm not in done and all(r is not None for r in recv[nm]):
                gsh[nm], delta[nm], new_m[nm], new_v[nm] = _adamw_reduce(
                    me_arr, recv[nm], sent[nm], wsh[nm], msh[nm], vsh[nm], f"adamw_{nm}")
                done.add(nm)
                after = [delta[nm]]
    gsmall_red = _slot_sum(rsmall, "grads_small_sum", SLOT_SUM_ROWS)
    for nm, val in zip(SMALL_SH, _unpack(gsmall_red, small_offs, [wsh[nm].shape for nm in SMALL_SH])):
        gsh[nm] = val
    small_adamw(list(SMALL_SH), "adamw_small")
    for dst in (gsh, delta, new_m, new_v):
        dst["dn_w_in"] = jnp.swapaxes(dst["dn_w_in"], 1, 2)
    return (loss_sum.reshape(()), grad_x[None], *[gsh[nm] for nm in WEIGHTS], *[delta[nm] for nm in WEIGHTS],
            *[new_m[nm] for nm in WEIGHTS], *[new_v[nm] for nm in WEIGHTS])
```

```python
import functools

import jax
import jax.numpy as jnp
from jax import lax
from jax.experimental import pallas as pl
from jax.experimental.pallas import tpu as pltpu

F32 = jnp.float32
BF16 = jnp.bfloat16
MESH_IDS = pl.DeviceIdType.MESH

N_DEV = 8
LANES = 128
RMS_EPS = 1e-6
LN_EPS = 1e-5
DN_HEAD_DIM = 128
DN_CONV = 4
DN_CHUNK = 64
CV_WIDTH = 31
XA_HEADS = 4
MM_TILE = 1024
MM_DEEP = 2048
EPI_SLAB = 512
SLOT_SUM_ROWS = 512
DN_HALO = 8
CV_HALO = 32

ADAM_LR = 0.001
ADAM_B1 = 0.9
ADAM_B2 = 0.999
ADAM_EPS = 1e-08
ADAM_WD = 0.01
ADAM_STEP = 10

BIG = (("dn_w_in", 2), ("dn_w_out", 1), ("cv_w_pw1", 2), ("cv_w_pw2", 1), ("xa_w_q", 1), ("xa_w_kv", 2),
       ("xa_w_o", 1), ("mlp_w_up", 2), ("mlp_w_down", 1))
_LAYER_GROUP = ("xa_w_q", "xa_w_o", "mlp_w_down", "xa_w_kv", "mlp_w_up")
GATHER_GROUPS = (
    (("dn_w_in", 0),),
    (("dn_w_out", 0),) + tuple((nm, 0) for nm in _LAYER_GROUP),
    (("cv_w_pw2", 0), ("cv_w_pw1", 0)),
    (("xa_w_q", 1), ("xa_w_o", 1), ("xa_w_kv", 1)),
    (("mlp_w_down", 1), ("mlp_w_up", 1)),
)
SCATTER_GROUPS = (
    (("dn_w_in", 0),),
    (("dn_w_out", 0),) + tuple((nm, 0) for nm in _LAYER_GROUP),
    (("cv_w_pw2", 0), ("cv_w_pw1", 0)),
    tuple((nm, 1) for nm in _LAYER_GROUP),
)
SMALL_SH = ("cv_norm", "cv_b_pw1", "cv_b_dw", "cv_ln_g", "cv_ln_b", "cv_b_pw2", "cv_w_dw", "dn_w_conv")
REPL = ("dn_norm", "dn_a_log", "dn_dt_bias", "dn_out_norm", "xa_norm", "xa_mem_norm", "mlp_norm", "final_norm")
WEIGHTS = ("dn_norm", "dn_w_in", "dn_w_conv", "dn_a_log", "dn_dt_bias", "dn_out_norm", "dn_w_out", "cv_norm",
           "cv_w_pw1", "cv_b_pw1", "cv_w_dw", "cv_b_dw", "cv_ln_g", "cv_ln_b", "cv_w_pw2", "cv_b_pw2", "xa_norm",
           "xa_mem_norm", "xa_w_q", "xa_w_kv", "xa_w_o", "mlp_norm", "mlp_w_up", "mlp_w_down", "final_norm")


def _dot_dims(mode, batched):
    o = 1 if batched else 0
    contract = {"nn": ((1 + o,), (o,)), "nt": ((1 + o,), (1 + o,)), "tn": ((o,), (o,))}[mode]
    return (contract, (((0,), (0,)) if batched else ((), ())))


def _bdot(a, b, mode):
    return lax.dot_general(a.astype(BF16), b.astype(BF16), _dot_dims(mode, a.ndim == 3),
                           preferred_element_type=F32)


@functools.partial(jax.custom_vjp, nondiff_argnums=(2,))
def _mm(a, b, mode):
    return _bdot(a, b, mode)


def _mm_fwd(a, b, mode):
    return _bdot(a, b, mode), (a, b)


def _mm_bwd(mode, res, ct):
    a, b = res
    if mode == "nn":
        da, db = _bdot(ct, b, "nt"), _bdot(a, ct, "tn")
    elif mode == "nt":
        da, db = _bdot(ct, b, "nn"), _bdot(ct, a, "tn")
    else:
        da, db = _bdot(b, ct, "nt"), _bdot(a, ct, "nn")
    return da.astype(a.dtype), db.astype(b.dtype)


_mm.defvjp(_mm_fwd, _mm_bwd)


def _sigmoid(x):
    return 0.5 * (jnp.tanh(0.5 * x) + 1.0)


def _silu(x):
    h = 0.5 * x
    return h + h * jnp.tanh(h)


def _softplus(x):
    return jnp.maximum(x, 0.0) + jnp.log(1.0 + jnp.exp(-jnp.abs(x)))


def _rms(x, g):
    r = lax.rsqrt(jnp.mean(x * x, axis=-1, keepdims=True) + RMS_EPS)
    return x * r * g


def _shift_rows(x, off):
    if off == 0:
        return x
    return pltpu.roll(x, x.shape[0] - off, 0)


def _series_dot(a, b, mode):
    return _bdot(a, b, mode)


def _chunk_masks(c):
    ii = lax.broadcasted_iota(jnp.int32, (c, c), 0)
    jj = lax.broadcasted_iota(jnp.int32, (c, c), 1)
    return (ii == jj).astype(F32), ii >= jj, ii > jj


def _neumann_inverse(lm):
    n = lm.shape[-1]
    t = -lm
    p = lm
    size = 2
    while size < n:
        size *= 2
        p = _series_dot(p, p, "nn")
        t = t + p + _series_dot(t, p, "nn")
    return t


def _apply_inverse(tm, rhs, mode):
    return rhs + _series_dot(tm, rhs, mode)


@jax.custom_vjp
def _unit_lower_solve(lm, rhs, tm):
    return _apply_inverse(tm, rhs, "nn")


def _uls_fwd(lm, rhs, tm):
    sol = _apply_inverse(tm, rhs, "nn")
    return sol, (tm, sol)


def _uls_bwd(res, ct):
    tm, sol = res
    d_rhs = _apply_inverse(tm, ct, "tn")
    return -_bdot(d_rhs, sol, "nt"), d_rhs, jnp.zeros_like(tm)


_unit_lower_solve.defvjp(_uls_fwd, _uls_bwd)


def _delta_chunk(q, k, v, gcol, bcol, s0, tm=None):
    c = q.shape[1]
    eye, causal, strict = _chunk_masks(c)
    grow = jnp.sum(eye * gcol, axis=1, keepdims=True)
    gc = jnp.sum(jnp.where(causal, grow, 0.0), axis=2, keepdims=True)
    gc_row = jnp.sum(eye * gc, axis=1, keepdims=True)
    decay = jnp.exp(jnp.where(causal, gc - gc_row, -jnp.inf))
    kb = k * bcol
    on_k = _mm(jnp.concatenate([kb, q], axis=1), k, "nt")
    lm = jnp.where(strict, on_k[:, :c] * decay, 0.0)
    attn = on_k[:, c:] * decay
    if tm is None:
        tm = _neumann_inverse(lax.stop_gradient(lm))
    egc = jnp.exp(gc)
    rhs = jnp.concatenate([v * bcol, kb * egc], axis=-1)
    sol = _unit_lower_solve(lm, rhs, tm)
    dv_ = v.shape[-1]
    u, w = sol[..., :dv_], sol[..., dv_:]
    gl = jnp.sum(grow, axis=2, keepdims=True)
    kd = k * jnp.exp(gl - gc)
    on_s = _mm(jnp.concatenate([w, q * egc], axis=1), s0, "nn")
    v_new = u - on_s[:, :c]
    o = on_s[:, c:] + _mm(attn, v_new, "nn")
    s1 = s0 * jnp.exp(gl) + _mm(kd, v_new, "tn")
    return o, s1, tm


def _dn_point(cv, ba, alog, dt, heads):
    a = _silu(cv)
    d = cv.shape[1] // 3
    qs, ks = [], []
    for h in range(heads):
        qh = a[:, h * DN_HEAD_DIM:(h + 1) * DN_HEAD_DIM]
        qs.append(qh * lax.rsqrt(jnp.sum(qh * qh, axis=-1, keepdims=True) + 1e-6) * (DN_HEAD_DIM ** -0.5))
        kh = a[:, d + h * DN_HEAD_DIM:d + (h + 1) * DN_HEAD_DIM]
        ks.append(kh * lax.rsqrt(jnp.sum(kh * kh, axis=-1, keepdims=True) + 1e-6))
    q = jnp.concatenate(qs, axis=-1)
    k = jnp.concatenate(ks, axis=-1)
    v = a[:, 2 * d:]
    lane = lax.broadcasted_iota(jnp.int32, ba.shape, 1)
    beta = _sigmoid(ba)
    g = -jnp.exp(alog) * _softplus(ba + dt)
    gb = jnp.where(lane < heads, beta, jnp.where(lane < 2 * heads, g, 0.0))
    return q, k, v, gb


def _attn_tile(q, k, v):
    hd = q.shape[1] // XA_HEADS
    outs = []
    for h in range(XA_HEADS):
        sl = slice(h * hd, (h + 1) * hd)
        s = _mm(q[:, sl], k[:, sl], "nt") * (hd ** -0.5)
        m = lax.stop_gradient(jnp.max(s, axis=-1, keepdims=True))
        e = jnp.exp(s - m)
        p = e / jnp.sum(e, axis=-1, keepdims=True)
        outs.append(_mm(p, v[:, sl], "nn"))
    return jnp.concatenate(outs, axis=-1)


def _ln_silu(c, g, b):
    mu = jnp.mean(c, axis=-1, keepdims=True)
    xc = c - mu
    y = xc * lax.rsqrt(jnp.mean(xc * xc, axis=-1, keepdims=True) + LN_EPS)
    return _silu(y * g + b)


def _causal_conv(xext, w, width, lead, ts):
    acc = None
    for j in range(width):
        term = _shift_rows(xext, lead + j)[:ts] * w[j:j + 1, :]
        acc = term if acc is None else acc + term
    return acc


def _colsum(x):
    return jnp.sum(x, axis=0, keepdims=True)


def _stack_rows(rows, n_rows):
    c = rows[0].shape[1]
    ridx = lax.broadcasted_iota(jnp.int32, (n_rows, c), 0)
    out = jnp.zeros((n_rows, c), F32)
    for j, r in enumerate(rows):
        out = out + jnp.where(ridx == j, r, 0.0)
    return out


def _matmul(a, b, mode, out_dtypes, *, name, epi=None, mn_extras=(), row_extras=(), out_dm=False, after=(),
            n_rowsum=0, slab=0, b_rows=None, a_pre=None, tm=MM_TILE, tn=MM_TILE, tk=MM_TILE):
    b_dm = b.ndim == 3
    b_shape = (b.shape[1], N_DEV * b.shape[2]) if b_dm else b.shape
    if b_rows is not None:
        assert not b_dm and b_rows <= b.shape[0]
        b_shape = (b_rows, b.shape[1])
    if mode == "nn":
        (m, k), (k2, n) = a.shape, b_shape
    elif mode == "nt":
        (m, k), (n, k2) = a.shape, b_shape
    else:
        (k, m), (k2, n) = a.shape, b_shape
    assert k == k2, (a.shape, b.shape, mode)
    tm, tn, tk = min(tm, m), min(tn, n), min(tk, k)
    cb, nb = 0, 1
    if b_dm:
        assert mode in ("nn", "nt")
        cb = b.shape[2]
        nb = max(1, (tn if mode == "nn" else tk) // cb)
        if mode == "nn":
            tn = nb * cb
        else:
            tk = nb * cb
    co, no = 0, 1
    if out_dm:
        co = n // N_DEV
        no = max(1, tn // co)
        tn = no * co
    assert m % tm == 0 and n % tn == 0 and k % tk == 0, (m, n, k, tm, tn, tk)
    nk = k // tk
    if mode == "tn":
        a_spec = pl.BlockSpec((tk, tm), lambda j, i, kk: (kk, i))
    else:
        a_spec = pl.BlockSpec((tm, tk), lambda j, i, kk: (i, kk))
    if b_dm:
        b_spec = (pl.BlockSpec((nb, tn, cb), lambda j, i, kk: (kk, j, 0)) if mode == "nt"
                  else pl.BlockSpec((nb, tk, cb), lambda j, i, kk: (j, kk, 0)))
    else:
        b_spec = (pl.BlockSpec((tn, tk), lambda j, i, kk: (j, kk)) if mode == "nt"
                  else pl.BlockSpec((tk, tn), lambda j, i, kk: (kk, j)))
    mn_spec = pl.BlockSpec((tm, tn), lambda j, i, kk: (i, j))
    row_spec = pl.BlockSpec((1, tn), lambda j, i, kk: (0, j))
    n_extra = len(mn_extras) + len(row_extras)
    n_out = len(out_dtypes)
    in_specs = ([a_spec, b_spec] + [mn_spec] * len(mn_extras) + [row_spec] * len(row_extras)
                + [_ANY_SPEC] * len(after))
    args = [a, b, *mn_extras, *row_extras, *after]
    if out_dm:
        out_specs = [pl.BlockSpec((no, tm, co), lambda j, i, kk: (j, i, 0))] * n_out
        out_shape = [jax.ShapeDtypeStruct((N_DEV, m, co), dt) for dt in out_dtypes]
    else:
        out_specs = [mn_spec] * n_out
        out_shape = [jax.ShapeDtypeStruct((m, n), dt) for dt in out_dtypes]
    out_specs = out_specs + [row_spec] * n_rowsum
    out_shape = out_shape + [jax.ShapeDtypeStruct((1, n), F32)] * n_rowsum
    n_in = len(args)
    n_mn = len(mn_extras)
    step = min(slab, tm) if slab else tm
    assert tm % step == 0

    def dot(a_ref, b_ref):
        a_val = a_ref[...] if a_pre is None else a_pre(a_ref[...])
        if not b_dm:
            return _bdot(a_val, b_ref[...], mode)
        if mode == "nn":
            parts = [_bdot(a_val, b_ref[dd], "nn") for dd in range(nb)]
            return parts[0] if nb == 1 else jnp.concatenate(parts, axis=1)
        out = None
        for dd in range(nb):
            part = _bdot(a_val[:, dd * cb:(dd + 1) * cb], b_ref[dd], "nt")
            out = part if out is None else out + part
        return out

    def finish(acc_src, extras, outs):
        sums = [None] * n_rowsum
        for r0 in range(0, tm, step):
            rs = slice(r0, r0 + step)
            acc_val = acc_src[rs, :]
            if epi is None:
                vals = (acc_val,)
            else:
                vals = epi(acc_val, *[e[rs, :] for e in extras[:n_mn]], *[e[...] for e in extras[n_mn:]])
            for o_ref, val in zip(outs[:n_out], vals[:n_out]):
                if out_dm:
                    for dd in range(no):
                        o_ref[dd, rs, :] = val[:, dd * co:(dd + 1) * co].astype(o_ref.dtype)
                else:
                    o_ref[rs, :] = val.astype(o_ref.dtype)
            for q in range(n_rowsum):
                sums[q] = vals[n_out + q] if sums[q] is None else sums[q] + vals[n_out + q]
        for q in range(n_rowsum):
            s_ref = outs[n_out + q]

            @pl.when(pl.program_id(1) == 0)
            def _():
                s_ref[...] = sums[q]

            @pl.when(pl.program_id(1) > 0)
            def _():
                s_ref[...] += sums[q]

    def body_one_step(*refs):
        finish(dot(refs[0], refs[1]), refs[2:2 + n_extra], refs[n_in:])

    def body(*refs):
        a_ref, b_ref = refs[0], refs[1]
        acc = refs[-1]
        kk = pl.program_id(2)

        @pl.when(kk == 0)
        def _():
            acc[...] = jnp.zeros_like(acc)

        acc[...] += dot(a_ref, b_ref)

        @pl.when(kk == nk - 1)
        def _():
            finish(acc, refs[2:2 + n_extra], refs[n_in:-1])

    res = pl.pallas_call(
        body_one_step if nk == 1 else body, name=name,
        grid=(n // tn, m // tm, nk),
        in_specs=in_specs, out_specs=out_specs, out_shape=out_shape,
        scratch_shapes=[] if nk == 1 else [pltpu.VMEM((tm, tn), F32)],
        compiler_params=pltpu.CompilerParams(
            dimension_semantics=("parallel", "arbitrary" if n_rowsum else "parallel", "arbitrary")),
    )(*args)
    return res[0] if n_out + n_rowsum == 1 else res


def _rowwise(fn, *, n_rows, ts, name, rows=(), prevs=(), nexts=(), vecs=(), row_outs=(), acc_outs=(), after=()):
    ts = min(ts, n_rows)
    assert n_rows % ts == 0
    nblk = n_rows // ts
    in_specs, args = [], []
    for arr, cb, w in rows:
        in_specs.append(pl.BlockSpec((ts, w), functools.partial(lambda i, cb: (i, cb), cb=cb)))
        args.append(arr)
    for arr, cb, w, halo in prevs:
        per = ts // halo
        in_specs.append(pl.BlockSpec(
            (halo, w), functools.partial(lambda i, cb, per: (jnp.maximum(i * per - 1, 0), cb), cb=cb, per=per)))
        args.append(arr)
    for arr, cb, w, halo in nexts:
        per = ts // halo
        last_blk = n_rows // halo - 1
        in_specs.append(pl.BlockSpec(
            (halo, w), functools.partial(lambda i, cb, per, lb: (jnp.minimum((i + 1) * per, lb), cb),
                                         cb=cb, per=per, lb=last_blk)))
        args.append(arr)
    for arr in vecs:
        in_specs.append(pl.BlockSpec(arr.shape, functools.partial(lambda i, nd: (0,) * nd, nd=arr.ndim)))
        args.append(arr)
    out_specs, out_shape = [], []
    for w, dt in row_outs:
        out_specs.append(pl.BlockSpec((ts, w), lambda i: (i, 0)))
        out_shape.append(jax.ShapeDtypeStruct((n_rows, w), dt))
    for shp in acc_outs:
        out_specs.append(pl.BlockSpec(shp, functools.partial(lambda i, nd: (0,) * nd, nd=len(shp))))
        out_shape.append(jax.ShapeDtypeStruct(shp, F32))
    n_used = len(args)
    n_tiles = n_used - len(vecs)
    in_specs += [_ANY_SPEC] * len(after)
    args += list(after)
    n_in, n_ro, n_acc = len(args), len(row_outs), len(acc_outs)

    def body(*refs):
        ins, ro, ac = refs[:n_used], refs[n_in:n_in + n_ro], refs[n_in + n_ro:]
        i = pl.program_id(0)
        rvals, avals = fn(i == 0, i == nblk - 1, *[r[...] for r in ins[:n_tiles]], *ins[n_tiles:])
        for r, val in zip(ro, rvals):
            r[...] = val.astype(r.dtype)
        if n_acc:
            @pl.when(i == 0)
            def _():
                for r in ac:
                    r[...] = jnp.zeros_like(r)

            for r, val in zip(ac, avals):
                r[...] += val

    res = pl.pallas_call(
        body, name=name, grid=(nblk,), in_specs=in_specs, out_specs=out_specs, out_shape=out_shape,
        compiler_params=pltpu.CompilerParams(dimension_semantics=("arbitrary",)),
    )(*args)
    return res


def _gated_out(o, z, onorm):
    return o * lax.rsqrt(jnp.mean(o * o, axis=-1, keepdims=True) + RMS_EPS) * onorm * _silu(z)


def _head_blocks(ref, heads, col0=0):
    return jnp.stack([ref[:, col0 + h * DN_HEAD_DIM:col0 + (h + 1) * DN_HEAD_DIM] for h in range(heads)])


def _split_heads(q_ref, k_ref, v_ref, gbv, heads):
    gcol = jnp.stack([gbv[:, heads + h:heads + h + 1] for h in range(heads)])
    bcol = jnp.stack([gbv[:, h:h + 1] for h in range(heads)])
    return _head_blocks(q_ref, heads), _head_blocks(k_ref, heads), _head_blocks(v_ref, heads), gcol, bcol


def _delta_fwd(q, k, v, gb, qkvz, onorm, heads):
    s, hd = q.shape
    n = s // DN_CHUNK
    blk = pl.BlockSpec((DN_CHUNK, hd), lambda c: (c, 0))
    gspec = pl.BlockSpec((DN_CHUNK, LANES), lambda c: (c, 0))

    def body(q_ref, k_ref, v_ref, gb_ref, z_ref, on_ref, og_ref, st_ref, tm_ref, state):
        @pl.when(pl.program_id(0) == 0)
        def _():
            state[...] = jnp.zeros_like(state)

        s0 = state[...]
        st_ref[0] = s0
        o, s1, tm = _delta_chunk(*_split_heads(q_ref, k_ref, v_ref, gb_ref[...], heads), s0)
        og = _gated_out(o, _head_blocks(z_ref, heads), on_ref[...])
        for h in range(heads):
            og_ref[:, h * DN_HEAD_DIM:(h + 1) * DN_HEAD_DIM] = og[h].astype(og_ref.dtype)
        state[...] = s1
        tm_ref[0] = tm

    return pl.pallas_call(
        body, name="dn_delta_fwd", grid=(n,),
        in_specs=[blk, blk, blk, gspec, pl.BlockSpec((DN_CHUNK, hd), lambda c: (c, 3)),
                  pl.BlockSpec(onorm.shape, lambda c: (0, 0))],
        out_specs=[blk, pl.BlockSpec((1, heads, DN_HEAD_DIM, DN_HEAD_DIM), lambda c: (c, 0, 0, 0)),
                   pl.BlockSpec((1, heads, DN_CHUNK, DN_CHUNK), lambda c: (c, 0, 0, 0))],
        out_shape=[jax.ShapeDtypeStruct((s, hd), BF16),
                   jax.ShapeDtypeStruct((n, heads, DN_HEAD_DIM, DN_HEAD_DIM), F32),
                   jax.ShapeDtypeStruct((n, heads, DN_CHUNK, DN_CHUNK), F32)],
        scratch_shapes=[pltpu.VMEM((heads, DN_HEAD_DIM, DN_HEAD_DIM), F32)],
        compiler_params=pltpu.CompilerParams(dimension_semantics=("arbitrary",)),
    )(q, k, v, gb, qkvz, onorm)


def _delta_bwd(q, k, v, gb, qkvz, onorm, states, tms, dog, heads):
    s, hd = q.shape
    n = s // DN_CHUNK
    blk = pl.BlockSpec((DN_CHUNK, hd), lambda c: (n - 1 - c, 0))
    gspec = pl.BlockSpec((DN_CHUNK, LANES), lambda c: (n - 1 - c, 0))
    sspec = pl.BlockSpec((1, heads, DN_HEAD_DIM, DN_HEAD_DIM), lambda c: (n - 1 - c, 0, 0, 0))
    tspec = pl.BlockSpec((1, heads, DN_CHUNK, DN_CHUNK), lambda c: (n - 1 - c, 0, 0, 0))
    nspec = pl.BlockSpec(onorm.shape, lambda c: (0, 0))

    def body(q_ref, k_ref, v_ref, gb_ref, z_ref, on_ref, st_ref, tm_ref, dog_ref,
             dq_ref, dk_ref, dv_ref, dgb_ref, dz_ref, don_ref, dstate):
        @pl.when(pl.program_id(0) == 0)
        def _():
            dstate[...] = jnp.zeros_like(dstate)
            don_ref[...] = jnp.zeros_like(don_ref)

        gbv = gb_ref[...]
        tm = tm_ref[0]

        def chunk(qh, kh, vh, gcol, bcol, s0, zh, on):
            o, s1, _ = _delta_chunk(qh, kh, vh, gcol, bcol, s0, tm)
            return _gated_out(o, zh, on), s1

        _, vjp = jax.vjp(chunk, *_split_heads(q_ref, k_ref, v_ref, gbv, heads), st_ref[0],
                         _head_blocks(z_ref, heads), on_ref[...])
        dq, dk, dv, dg, db, ds0, dz, don = vjp((_head_blocks(dog_ref, heads).astype(F32), dstate[...]))
        dstate[...] = ds0
        don_ref[...] += don
        lane = lax.broadcasted_iota(jnp.int32, gbv.shape, 1)
        dgb = jnp.zeros(gbv.shape, F32)
        for h in range(heads):
            sl = slice(h * DN_HEAD_DIM, (h + 1) * DN_HEAD_DIM)
            dq_ref[:, sl] = dq[h]
            dk_ref[:, sl] = dk[h]
            dv_ref[:, sl] = dv[h]
            dz_ref[:, sl] = dz[h]
            dgb = dgb + jnp.where(lane == h, db[h], 0.0) + jnp.where(lane == heads + h, dg[h], 0.0)
        dgb_ref[...] = dgb

    return pl.pallas_call(
        body, name="dn_delta_bwd", grid=(n,),
        in_specs=[blk, blk, blk, gspec, pl.BlockSpec((DN_CHUNK, hd), lambda c: (n - 1 - c, 3)), nspec,
                  sspec, tspec, blk],
        out_specs=[blk, blk, blk, gspec, blk, nspec],
        out_shape=[jax.ShapeDtypeStruct((s, hd), F32)] * 3 + [jax.ShapeDtypeStruct((s, LANES), F32),
                                                              jax.ShapeDtypeStruct((s, hd), F32),
                                                              jax.ShapeDtypeStruct(onorm.shape, F32)],
        scratch_shapes=[pltpu.VMEM((heads, DN_HEAD_DIM, DN_HEAD_DIM), F32)],
        compiler_params=pltpu.CompilerParams(dimension_semantics=("arbitrary",)),
    )(q, k, v, gb, qkvz, onorm, states, tms, dog)


def _dev_index(px, py, pc):
    return 4 * px + 2 * py + pc


def _all_gather(arrs, name):
    n = len(arrs)

    def body(*refs):
        xs, outs = refs[:n], refs[n:2 * n]
        send_sems, recv_sems, local_sems = refs[2 * n:]
        x, y, c = lax.axis_index("x"), lax.axis_index("y"), lax.axis_index("c")
        me, sibling = (x, y, c), (x, y, 1 - c)
        chips = [(1 - x, y), (x, 1 - y), (1 - x, 1 - y)]

        def copy(a, kk, block, to, src=None):
            dst = outs[a].at[_dev_index(*block)]
            return pltpu.make_async_remote_copy(
                src_ref=dst if src is None else src, dst_ref=dst,
                send_sem=send_sems.at[a * 7 + kk], recv_sem=recv_sems.at[a * 7 + kk],
                device_id=to, device_id_type=MESH_IDS)

        mine = [pltpu.make_async_copy(xs[a], outs[a].at[_dev_index(*me)], local_sems.at[a]) for a in range(n)]
        for cp in mine:
            cp.start()
        first = []
        for a in range(n):
            first.append(copy(a, 0, me, sibling, src=xs[a]))
            first += [copy(a, 1 + j, me, (*chip, c), src=xs[a]) for j, chip in enumerate(chips)]
        for cp in first:
            cp.start()
        passed = []
        for j, chip in enumerate(chips):
            for a in range(n):
                copy(a, 1 + j, (*chip, c), me).wait_recv()
                fwd = copy(a, 4 + j, (*chip, c), sibling)
                fwd.start()
                passed.append(fwd)
        for a in range(n):
            copy(a, 0, sibling, me).wait_recv()
        for j, chip in enumerate(chips):
            for a in range(n):
                copy(a, 4 + j, (*chip, 1 - c), me).wait_recv()
        for cp in first + passed:
            cp.wait_send()
        for cp in mine:
            cp.wait()

    hbm = pl.BlockSpec(memory_space=pltpu.HBM)
    res = pl.pallas_call(
        body, name=name,
        in_specs=[hbm] * n, out_specs=[hbm] * n,
        out_shape=[jax.ShapeDtypeStruct((N_DEV,) + a.shape, a.dtype) for a in arrs],
        scratch_shapes=[pltpu.SemaphoreType.DMA((7 * n,)), pltpu.SemaphoreType.DMA((7 * n,)),
                        pltpu.SemaphoreType.DMA((n,))],
    )(*arrs)
    return list(res)


_FLIPS = ((0, 0, 1), (1, 0, 0), (0, 1, 0), (1, 1, 0), (1, 0, 1), (0, 1, 1), (1, 1, 1))
_HBM_SPEC = pl.BlockSpec(memory_space=pltpu.HBM)
_SEM_SPEC = pl.BlockSpec(memory_space=pltpu.SEMAPHORE)
_ANY_SPEC = pl.BlockSpec(memory_space=pl.ANY)
_DATAFLOW = pltpu.SideEffectType.DATAFLOW_SIDE_EFFECTING
TOKEN_SHAPE = (8, LANES)


def _mesh_me():
    return lax.axis_index("x"), lax.axis_index("y"), lax.axis_index("c")


def _flipped(me, f):
    return tuple(1 - v if fl else v for v, fl in zip(me, f))


def _exchange_copies(xs, lands, send_sems, recv_sems, scatter, landed):
    me = _mesh_me()
    cps = []
    for kk, f in enumerate(_FLIPS):
        p = _flipped(me, f)
        for a in range(len(xs)):
            cps.append(pltpu.make_async_remote_copy(
                src_ref=xs[a].at[_dev_index(*p)] if scatter else xs[a],
                dst_ref=lands[a].at[_dev_index(*(p if landed else me))],
                send_sem=send_sems.at[a * 7 + kk], recv_sem=recv_sems.at[a * 7 + kk],
                device_id=p, device_id_type=MESH_IDS))
    return cps


def _exchange_start(srcs, lands, scatter, name, after=()):
    n = len(srcs)

    n_after = len(after)

    def body(*refs):
        xs, ls = refs[:n], refs[n:2 * n]
        send_sems, recv_sems = refs[2 * n + n_after], refs[2 * n + n_after + 1]
        token = refs[-1]
        for cp in _exchange_copies(xs, ls, send_sems, recv_sems, scatter, landed=False):
            cp.start()
        token[...] = jnp.zeros_like(token)

    operands = [pltpu.with_memory_space_constraint(a, pltpu.HBM) for a in list(srcs) + list(lands)]
    res = pl.pallas_call(
        body, name=name,
        in_specs=[_HBM_SPEC] * (2 * n) + [_ANY_SPEC] * len(after),
        out_specs=[_SEM_SPEC, _SEM_SPEC] + [_HBM_SPEC] * (2 * n) + [pl.BlockSpec(memory_space=pltpu.VMEM)],
        out_shape=[pltpu.SemaphoreType.DMA((7 * n,)), pltpu.SemaphoreType.DMA((7 * n,))]
        + [pltpu.HBM(a.shape, a.dtype) for a in operands] + [jax.ShapeDtypeStruct(TOKEN_SHAPE, F32)],
        input_output_aliases={i: 2 + i for i in range(2 * n)},
        compiler_params=pltpu.CompilerParams(has_side_effects=_DATAFLOW),
    )(*operands, *after)
    return (res[0], res[1], list(res[2:2 + n]), list(res[2 + n:2 + 2 * n]), scatter, name), res[-1]


def _exchange_wait(handle, after):
    send_sems, recv_sems, srcs, lands, scatter, name = handle
    n = len(srcs)
    n_after = len(after)

    def body(*refs):
        xs, ls = refs[:n], refs[n:2 * n]
        send_sems_ref, recv_sems_ref = refs[2 * n], refs[2 * n + 1]
        for cp in _exchange_copies(xs, ls, send_sems_ref, recv_sems_ref, scatter, landed=True):
            cp.wait_send()
            cp.wait_recv()

    res = pl.pallas_call(
        body, name=name + "_wait",
        in_specs=[_HBM_SPEC] * (2 * n) + [_SEM_SPEC, _SEM_SPEC] + [_ANY_SPEC] * n_after,
        out_specs=[_HBM_SPEC] * (2 * n),
        out_shape=[pltpu.HBM(a.shape, a.dtype) for a in srcs + lands],
        input_output_aliases={i: i for i in range(2 * n)},
        compiler_params=pltpu.CompilerParams(has_side_effects=_DATAFLOW),
    )(*srcs, *lands, send_sems, recv_sems, *after)
    return list(res[:n]), list(res[n:])


def _slot_sum(g, name, tr):
    _, r, c = g.shape
    tr = min(tr, r)
    assert r % tr == 0

    def body(g_ref, o_ref):
        acc = g_ref[0].astype(F32)
        for s in range(1, N_DEV):
            acc = acc + g_ref[s].astype(F32)
        o_ref[...] = acc

    return pl.pallas_call(
        body, name=name, grid=(r // tr,),
        in_specs=[pl.BlockSpec((N_DEV, tr, c), lambda i: (0, i, 0))],
        out_specs=pl.BlockSpec((tr, c), lambda i: (i, 0)),
        out_shape=jax.ShapeDtypeStruct((r, c), F32),
        compiler_params=pltpu.CompilerParams(dimension_semantics=("parallel",)),
    )(g)


def _adam_update(w, gg, m, v):
    c1 = 1.0 / (1.0 - ADAM_B1 ** ADAM_STEP)
    c2 = 1.0 / (1.0 - ADAM_B2 ** ADAM_STEP)
    nm = ADAM_B1 * m + (1.0 - ADAM_B1) * gg
    nv = ADAM_B2 * v + (1.0 - ADAM_B2) * (gg * gg)
    return -ADAM_LR * ((nm * c1) / (jnp.sqrt(nv * c2) + ADAM_EPS) + ADAM_WD * w), nm, nv


def _adamw_reduce(me, recvs, owns, w, m, v, name, tr=256):
    nl, r, c = w.shape
    assert len(recvs) == nl and len(owns) == nl
    tr = min(tr, r)
    if r % tr == 0:
        tc, nblk = c, r // tr
        at = lambda i: (i, 0)
    else:
        tr, tc = r, min(c, 4 * LANES)
        assert c % tc == 0
        nblk = c // tc
        at = lambda i: (0, i)

    def parked(li, l, i):
        return jnp.where(l < li, 0, jnp.where(l > li, nblk - 1, i))

    def recv_spec(li):
        return pl.BlockSpec((N_DEV, tr, tc), lambda l, i, me_ref: (0, *at(parked(li, l, i))))

    def own_spec(li):
        return pl.BlockSpec((None, tr, tc), lambda l, i, me_ref: (me_ref[0], *at(parked(li, l, i))))

    def body(me_ref, *refs):
        rrefs, orefs = refs[:nl], refs[nl:2 * nl]
        w_ref, m_ref, v_ref, g_ref, d_ref, nm_ref, nv_ref = refs[2 * nl:]
        l = pl.program_id(0)

        def of_layer(vals):
            out = vals[0]
            for li in range(1, nl):
                out = jnp.where(l == li, vals[li], out)
            return out

        own = of_layer([o[...].astype(F32) for o in orefs])
        gg = None
        for s in range(N_DEV):
            slot = jnp.where(me_ref[0] == s, own, of_layer([rr[s].astype(F32) for rr in rrefs]))
            gg = slot if gg is None else gg + slot
        g_ref[...] = gg
        d_ref[...], nm_ref[...], nv_ref[...] = _adam_update(w_ref[...], gg, m_ref[...], v_ref[...])

    spec = pl.BlockSpec((None, tr, tc), lambda l, i, me_ref: (l, *at(i)))
    return pl.pallas_call(
        body, name=name,
        grid_spec=pltpu.PrefetchScalarGridSpec(
            num_scalar_prefetch=1, grid=(nl, nblk),
            in_specs=[recv_spec(li) for li in range(nl)] + [own_spec(li) for li in range(nl)] + [spec] * 3,
            out_specs=[spec] * 4),
        out_shape=[jax.ShapeDtypeStruct((nl, r, c), F32)] * 4,
        compiler_params=pltpu.CompilerParams(dimension_semantics=("arbitrary", "arbitrary")),
    )(me, *recvs, *owns, w, m, v)


def _adamw(w, g, m, v, name, tr=256):
    r, c = w.shape
    tr = min(tr, r)
    assert r % tr == 0

    def body(w_ref, g_ref, m_ref, v_ref, d_ref, nm_ref, nv_ref):
        d_ref[...], nm_ref[...], nv_ref[...] = _adam_update(w_ref[...], g_ref[...], m_ref[...], v_ref[...])

    spec = pl.BlockSpec((tr, c), lambda i: (i, 0))
    return pl.pallas_call(
        body, name=name, grid=(r // tr,), in_specs=[spec] * 4, out_specs=[spec] * 3,
        out_shape=[jax.ShapeDtypeStruct((r, c), F32)] * 3,
        compiler_params=pltpu.CompilerParams(dimension_semantics=("parallel",)),
    )(w, g, m, v)


def _rms_fwd(x, g, name, ts=1024, after=()):
    s, d = x.shape

    def fn(first, last, xv, gv):
        return [_rms(xv, gv[...])], []

    return _rowwise(fn, n_rows=s, ts=ts, name=name, rows=[(x, 0, d)], vecs=[g], row_outs=[(d, BF16)],
                    after=after)[0]


def _rms_bwd(x, dn, dres, g, name, ts=256):
    s, d = x.shape

    def fn(first, last, xv, dnv, drv, gv):
        _, vjp = jax.vjp(_rms, xv, gv[...])
        dx, dg = vjp(dnv.astype(F32))
        return [drv + dx], [dg]

    return _rowwise(fn, n_rows=s, ts=ts, name=name, rows=[(x, 0, d), (dn, 0, d), (dres, 0, d)], vecs=[g],
                    row_outs=[(d, F32)], acc_outs=[(1, d)])


def _dn_pre_fwd(qkvz, ba, wconv, alog, dt, heads, ts=512):
    s = qkvz.shape[0]
    d3 = wconv.shape[1]
    d = d3 // 3

    def fn(first, last, xc, bav, xp, wv, av, dv):
        xext = jnp.concatenate([jnp.where(first, 0.0, xp), xc], axis=0)
        cv = _causal_conv(xext, wv, DN_CONV, DN_HALO - (DN_CONV - 1), xc.shape[0])
        return list(_dn_point(cv, bav, av[...], dv[...], heads)), []

    return _rowwise(fn, n_rows=s, ts=ts, name="dn_pre_fwd", rows=[(qkvz, 0, d3), (ba, 0, LANES)],
                    prevs=[(qkvz, 0, d3, DN_HALO)], vecs=[wconv, alog, dt],
                    row_outs=[(d, F32), (d, F32), (d, F32), (LANES, F32)])


def _dn_pre_bwd(qkvz, ba, wconv, alog, dt, dq, dk, dv, dgb, dz, heads, ts=256):
    s = qkvz.shape[0]
    d3 = wconv.shape[1]
    d = d3 // 3
    lead = DN_HALO - (DN_CONV - 1)

    def fn(first, last, xc, bac, dqc, dkc, dvc, dgbc, dzc, xp, xn, ban, dqn, dkn, dvn, dgbn, wv, av, dtv):
        n = xc.shape[0]
        ext = lambda cur, nxt: jnp.concatenate([cur, nxt], axis=0)
        live = lambda nxt: jnp.where(last, 0.0, nxt)
        xall = jnp.concatenate([jnp.where(first, 0.0, xp), xc, live(xn)], axis=0)
        cv = _causal_conv(xall, wv, DN_CONV, lead, n + DN_HALO)
        (_, _, _, gbv), vjp = jax.vjp(lambda c, b: _dn_point(c, b, av[...], dtv[...], heads), cv, ext(bac, ban))
        dc, dba = vjp((ext(dqc, live(dqn)), ext(dkc, live(dkn)), ext(dvc, live(dvn)), ext(dgbc, live(dgbn))))
        dx = None
        dw = []
        for j in range(DN_CONV):
            term = _shift_rows(dc, DN_CONV - 1 - j)[:n] * wv[j:j + 1, :]
            dx = term if dx is None else dx + term
            dw.append(_colsum(dc[:n] * _shift_rows(xall, lead + j)[:n]))
        dba = dba[:n]
        return ([jnp.concatenate([dx, dzc], axis=-1), dba],
                [_stack_rows(dw, DN_CONV), _colsum(dgbc * gbv[:n]), _colsum(dba)])

    return _rowwise(fn, n_rows=s, ts=ts, name="dn_pre_bwd",
                    rows=[(qkvz, 0, d3), (ba, 0, LANES), (dq, 0, d), (dk, 0, d), (dv, 0, d), (dgb, 0, LANES),
                          (dz, 0, d)],
                    prevs=[(qkvz, 0, d3, DN_HALO)],
                    nexts=[(qkvz, 0, d3, DN_HALO), (ba, 0, LANES, DN_HALO), (dq, 0, d, DN_HALO),
                           (dk, 0, d, DN_HALO), (dv, 0, d, DN_HALO), (dgb, 0, LANES, DN_HALO)],
                    vecs=[wconv, alog, dt],
                    row_outs=[(4 * d, BF16), (LANES, BF16)], acc_outs=[(DN_CONV, d3), (1, LANES), (1, LANES)])


def _cv_mid_fwd(u, wdw, bdw, lng, lnb, ts=512):
    s = u.shape[0]
    d = u.shape[1] // 2

    def fn(first, last, uc, up, wv, bv, gv, lbv):
        uext = jnp.concatenate([jnp.where(first, 0.0, up), uc], axis=0)
        glu = uext[:, :d] * _sigmoid(uext[:, d:])
        c = _causal_conv(glu, wv, CV_WIDTH, CV_HALO - (CV_WIDTH - 1), uc.shape[0]) + bv[...]
        return [c, _ln_silu(c, gv[...], lbv[...])], []

    return _rowwise(fn, n_rows=s, ts=ts, name="cv_mid_fwd", rows=[(u, 0, 2 * d)], prevs=[(u, 0, 2 * d, CV_HALO)],
                    vecs=[wdw, bdw, lng, lnb], row_outs=[(d, F32), (d, BF16)])


def _cv_mid_bwd2(dc, u, wdw, ts=512):
    s, d = dc.shape

    def fn(first, last, dcc, uc, up, dcn, wv):
        n = dcc.shape[0]
        dcext = jnp.concatenate([dcc, jnp.where(last, 0.0, dcn)], axis=0)
        uext = jnp.concatenate([jnp.where(first, 0.0, up), uc], axis=0)
        glu = uext[:, :d] * _sigmoid(uext[:, d:])
        dglu = None
        dw = []
        for j in range(CV_WIDTH):
            term = _shift_rows(dcext, CV_WIDTH - 1 - j)[:n] * wv[j:j + 1, :]
            dglu = term if dglu is None else dglu + term
            dw.append(_colsum(dcc * _shift_rows(glu, CV_HALO - (CV_WIDTH - 1) + j)[:n]))
        u1, sg = uc[:, :d], _sigmoid(uc[:, d:])
        du = jnp.concatenate([dglu * sg, dglu * u1 * sg * (1.0 - sg)], axis=-1)
        return [du], [_stack_rows(dw, CV_HALO), _colsum(du)]

    return _rowwise(fn, n_rows=s, ts=ts, name="cv_mid_bwd2", rows=[(dc, 0, d), (u, 0, 2 * d)],
                    prevs=[(u, 0, 2 * d, CV_HALO)], nexts=[(dc, 0, d, CV_HALO)], vecs=[wdw],
                    row_outs=[(2 * d, BF16)], acc_outs=[(CV_HALO, d), (1, 2 * d)])


def _attn_fwd(q, k, v, name, ts=2048):
    s, d = q.shape

    def fn(first, last, qv, kv, vv):
        return [_attn_tile(qv.astype(F32), kv[...].astype(F32), vv[...].astype(F32))], []

    return _rowwise(fn, n_rows=s, ts=ts, name=name, rows=[(q, 0, d)], vecs=[k, v], row_outs=[(d, BF16)])[0]


def _attn_bwd(q, k, v, do, name, ts=2048):
    s, d = q.shape
    m = k.shape[0]

    def fn(first, last, qv, dov, kv, vv):
        _, vjp = jax.vjp(_attn_tile, qv.astype(F32), kv[...].astype(F32), vv[...].astype(F32))
        dq, dk, dv = vjp(dov.astype(F32))
        return [dq], [dk, dv]

    return _rowwise(fn, n_rows=s, ts=ts, name=name, rows=[(q, 0, d), (do, 0, d)], vecs=[k, v],
                    row_outs=[(d, BF16)], acc_outs=[(m, d), (m, d)])


def _pad_lanes(a, off=0):
    r, n = a.shape
    return jnp.pad(a, ((0, 0), (off, LANES - off - n)))


def _local_step(x, mem, tgt, w, fetch=None, emit=None, first_after=()):
    s, d = x.shape
    heads = d // DN_HEAD_DIM
    g = {}
    if fetch is None:
        fetch = lambda group, after: None
    if emit is None:
        emit = lambda group, grads: ()

    def add_res_rms(acc, res, gain):
        h = res + acc
        return h, _rms(h, gain)

    def rms_bwd_epi(acc, hx, dres, gain):
        _, vjp = jax.vjp(_rms, hx, gain)
        dx, dg = vjp(acc)
        return dres + dx, dg

    w_int = w["dn_w_in"][0]
    assert w_int.shape[0] == 4 * d + 2 * heads
    w_bat = jnp.pad(w_int[4 * d:], ((0, LANES - 2 * heads), (0, 0)))
    dn_norm = w["dn_norm"]
    alog = _pad_lanes(w["dn_a_log"], heads)
    dtb = _pad_lanes(w["dn_dt_bias"], heads)
    wconv = w["dn_w_conv"][0]
    n0 = _rms_fwd(x, dn_norm, "dn_rms", after=first_after)
    qkvz = _matmul(n0, w_int, "nt", [F32], name="dn_in_proj", b_rows=4 * d, tn=MM_DEEP)
    ba = _matmul(n0, w_bat, "nt", [F32], name="dn_in_proj_ba")
    q, k, v, gb = _dn_pre_fwd(qkvz, ba, wconv, alog, dtb, heads)
    og, states, tms = _delta_fwd(q, k, v, gb, qkvz, w["dn_out_norm"], heads)
    fetch(1, [og])
    h1, nq0 = _matmul(og, w["dn_w_out"][0], "nn", [F32, BF16], name="dn_out_proj", epi=add_res_rms,
                      mn_extras=[x], row_extras=[w["xa_norm"][0:1]], slab=EPI_SLAB)

    def xattn_fwd(h, nq, layer, next_gain):
        qx = _matmul(nq, w["xa_w_q"][layer], "nn", [BF16], name=f"xa{layer}_q")
        mn = _rms_fwd(mem, w["xa_mem_norm"][layer:layer + 1], f"xa{layer}_mem_rms")
        kv = _matmul(mn, w["xa_w_kv"][layer], "nn", [BF16], name=f"xa{layer}_kv")
        kx, vx = kv[:, :d], kv[:, d:]
        ox = _attn_fwd(qx, kx, vx, f"xa{layer}_attn")
        hn, nn = _matmul(ox, w["xa_w_o"][layer], "nn", [F32, BF16], name=f"xa{layer}_o", epi=add_res_rms,
                         mn_extras=[h], row_extras=[next_gain], slab=EPI_SLAB)
        return hn, nn, (h, nq, qx, mn, kx, vx, ox)

    def sq_relu(t):
        r = jnp.maximum(t.astype(F32), 0.0)
        return r * r

    def loss_epi(acc, res, target, gain):
        def cols(hh, gg):
            e = _rms(hh, gg) - target
            return _colsum(e * e) * (0.5 / d)

        per_col, vjp = jax.vjp(cols, res + acc, gain)
        dhx, dgain = vjp(jnp.ones_like(per_col))
        return dhx, dgain, per_col

    def mlp_fwd(h, nm, layer, next_gain):
        u = _matmul(nm, w["mlp_w_up"][layer], "nn", [BF16], name=f"mlp{layer}_up", tn=MM_DEEP)
        if next_gain is None:
            hn, *nn = _matmul(u, w["mlp_w_down"][layer], "nn", [F32], name=f"mlp{layer}_down_loss",
                              epi=loss_epi, mn_extras=[h, tgt], row_extras=[w["final_norm"].reshape(1, d)],
                              n_rowsum=2, tk=MM_DEEP, slab=EPI_SLAB, a_pre=sq_relu)
        else:
            hn, nn = _matmul(u, w["mlp_w_down"][layer], "nn", [F32, BF16], name=f"mlp{layer}_down",
                             epi=add_res_rms, mn_extras=[h], row_extras=[next_gain], tk=MM_DEEP, slab=EPI_SLAB,
                             a_pre=sq_relu)
        return hn, nn, (h, nm, u)

    h2, nm0, xa0 = xattn_fwd(h1, nq0, 0, w["mlp_norm"][0:1])
    fetch(2, [h2])
    h3, n1, mlp0 = mlp_fwd(h2, nm0, 0, w["cv_norm"])

    u_cv = _matmul(n1, w["cv_w_pw1"][0], "nn", [F32], name="cv_pw1", epi=lambda acc, b: (acc + b,),
                   row_extras=[w["cv_b_pw1"]], tn=MM_DEEP)
    wdw = jnp.pad(w["cv_w_dw"][0], ((0, CV_HALO - CV_WIDTH), (0, 0)))
    c_cv, s_cv = _cv_mid_fwd(u_cv, wdw, w["cv_b_dw"], w["cv_ln_g"], w["cv_ln_b"])
    h4, nq1 = _matmul(s_cv, w["cv_w_pw2"][0], "nn", [F32, BF16], name="cv_pw2",
                      epi=lambda acc, res, b, gain: add_res_rms(acc + b, res, gain), mn_extras=[h3],
                      row_extras=[w["cv_b_pw2"], w["xa_norm"][1:2]], slab=EPI_SLAB)
    fetch(3, [h4])
    h5, nm1, xa1 = xattn_fwd(h4, nq1, 1, w["mlp_norm"][1:2])
    fetch(4, [h5])
    dh, (g_fn, loss_cols), mlp1 = mlp_fwd(h5, nm1, 1, None)
    g["final_norm"] = g_fn.reshape(d)
    loss = jnp.sum(loss_cols, axis=1, keepdims=True)

    def mlp_bwd(dh, layer, saved, after=()):
        h, nm, u = saved
        du = _matmul(dh, w["mlp_w_down"][layer], "nt", [BF16], name=f"mlp{layer}_down_dx", after=after,
                     epi=lambda acc, uu: (acc * 2.0 * jnp.maximum(uu.astype(F32), 0.0),), mn_extras=[u])
        gdown = _matmul(u, dh, "tn", [BF16], name=f"mlp{layer}_down_dw", tm=MM_DEEP, a_pre=sq_relu)
        dhn, gn = _matmul(du, w["mlp_w_up"][layer], "nt", [F32], name=f"mlp{layer}_up_dx", epi=rms_bwd_epi,
                          mn_extras=[h, dh], row_extras=[w["mlp_norm"][layer:layer + 1]], n_rowsum=1,
                          slab=EPI_SLAB, tk=MM_DEEP)
        gup = _matmul(nm, du, "tn", [BF16], name=f"mlp{layer}_up_dw", out_dm=True, tk=MM_DEEP, tn=MM_DEEP)
        return dhn, gup, gdown, gn

    def xattn_bwd(dh, layer, saved):
        h, nq, qx, mn, kx, vx, ox = saved
        dox = _matmul(dh, w["xa_w_o"][layer], "nt", [BF16], name=f"xa{layer}_o_dx")
        go = _matmul(ox, dh, "tn", [BF16], name=f"xa{layer}_o_dw", tk=MM_DEEP)
        dqx, dkx, dvx = _attn_bwd(qx, kx, vx, dox, f"xa{layer}_attn_bwd")

        def epi(acc, hx, dres, gain):
            dhx, dg = rms_bwd_epi(acc, hx, dres, gain)
            return dhx, dg, _colsum(dhx)

        dhn, gn, dh_cols = _matmul(dqx, w["xa_w_q"][layer], "nt", [F32], name=f"xa{layer}_q_dx", epi=epi,
                                   mn_extras=[h, dh], row_extras=[w["xa_norm"][layer:layer + 1]], n_rowsum=2,
                                   slab=EPI_SLAB)
        gq = _matmul(nq, dqx, "tn", [BF16], name=f"xa{layer}_q_dw", tk=MM_DEEP)
        dkv = jnp.concatenate([dkx, dvx], axis=-1)
        gkv = _matmul(mn, dkv, "tn", [BF16], name=f"xa{layer}_kv_dw", out_dm=True)
        dmn = _matmul(dkv, w["xa_w_kv"][layer], "nt", [F32], name=f"xa{layer}_kv_dx", tk=MM_DEEP)
        _, gmem = _rms_bwd(mem, dmn, dmn, w["xa_mem_norm"][layer:layer + 1], f"xa{layer}_mem_rms_bwd")
        return dhn, gq, gkv, go, gn, gmem, dh_cols

    dh, gup1, gdown1, gmn1 = mlp_bwd(dh, 1, mlp1)
    dh, gq1, gkv1, go1, gxn1, gmem1, g_b2 = xattn_bwd(dh, 1, xa1)
    g.update(mlp_w_up=[None, gup1], mlp_w_down=[None, gdown1], xa_w_q=[None, gq1], xa_w_kv=[None, gkv1],
             xa_w_o=[None, go1])
    tok = emit(3, g)

    def ln_bwd_epi(acc, cx, gain, bias):
        _, vjp = jax.vjp(_ln_silu, cx, gain, bias)
        dc, dg, db = vjp(acc)
        return dc, dg, db, _colsum(dc)

    dc_cv, g_lng, g_lnb, g_bdw = _matmul(dh, w["cv_w_pw2"][0], "nt", [F32], name="cv_pw2_dx", after=tok,
                                        epi=ln_bwd_epi, mn_extras=[c_cv],
                                        row_extras=[w["cv_ln_g"], w["cv_ln_b"]], n_rowsum=3, slab=EPI_SLAB)
    g["cv_w_pw2"] = [_matmul(s_cv, dh, "tn", [BF16], name="cv_pw2_dw", tk=MM_DEEP)]
    du_cv, g_wdw, g_b1 = _cv_mid_bwd2(dc_cv, u_cv, wdw)
    g["cv_w_pw1"] = [_matmul(n1, du_cv, "tn", [BF16], name="cv_pw1_dw", out_dm=True, tk=MM_DEEP)]
    dh, g_cvn = _matmul(du_cv, w["cv_w_pw1"][0], "nt", [F32], name="cv_pw1_dx", epi=rms_bwd_epi,
                        mn_extras=[h3, dh], row_extras=[w["cv_norm"]], n_rowsum=1, slab=EPI_SLAB, tk=MM_DEEP)
    g.update(cv_ln_g=g_lng, cv_ln_b=g_lnb, cv_b_dw=g_bdw, cv_b_pw2=g_b2, cv_b_pw1=g_b1, cv_norm=g_cvn,
             cv_w_dw=g_wdw[:CV_WIDTH][None])

    tok = emit(2, g)
    dh, gup0, gdown0, gmn0 = mlp_bwd(dh, 0, mlp0, after=tok)
    dh, gq0, gkv0, go0, gxn0, gmem0, _ = xattn_bwd(dh, 0, xa0)
    g["mlp_w_up"][0] = gup0
    g["mlp_w_down"][0] = gdown0
    g["mlp_norm"] = jnp.concatenate([gmn0, gmn1], axis=0)
    g["xa_w_q"][0] = gq0
    g["xa_w_kv"][0] = gkv0
    g["xa_w_o"][0] = go0
    g["xa_norm"] = jnp.concatenate([gxn0, gxn1], axis=0)
    g["xa_mem_norm"] = jnp.concatenate([gmem0, gmem1], axis=0)
    g["dn_w_out"] = [_matmul(og, dh, "tn", [BF16], name="dn_out_proj_dw", tk=MM_DEEP)]
    tok = emit(1, g)
    dog = _matmul(dh, w["dn_w_out"][0], "nt", [BF16], name="dn_out_proj_dx", after=tok)
    dq, dk, dv, dgb, dz, g_on = _delta_bwd(q, k, v, gb, qkvz, w["dn_out_norm"], states, tms, dog, heads)
    dqkvz, dba, g_wconv, g_alog, g_dt = _dn_pre_bwd(qkvz, ba, wconv, alog, dtb, dq, dk, dv, dgb, dz, heads)
    g_qkvzt = _matmul(dqkvz, n0, "tn", [BF16], name="dn_in_proj_dw", tk=MM_DEEP)
    g_bat = _matmul(dba, n0, "tn", [BF16], name="dn_in_proj_ba_dw", tk=MM_DEEP)
    g["dn_w_in"] = [jnp.concatenate([g_qkvzt, g_bat[:2 * heads]], axis=0)]
    g["dn_w_conv"] = g_wconv[None]
    tok = emit(0, g)
    dn0a = _matmul(dba, w_bat, "nn", [F32], name="dn_in_proj_ba_dx", after=tok)
    grad_x, g_dnn = _matmul(dqkvz, w_int, "nn", [F32], name="dn_in_proj_dx", b_rows=4 * d,
                            epi=lambda acc, part, hx, dres, gain: rms_bwd_epi(acc + part, hx, dres, gain),
                            mn_extras=[dn0a, x, dh], row_extras=[dn_norm], n_rowsum=1, slab=EPI_SLAB)
    g.update(dn_norm=g_dnn, dn_out_norm=g_on,
             dn_a_log=g_alog[:, heads:2 * heads], dn_dt_bias=g_dt[:, heads:2 * heads])
    return loss, grad_x, g


def _round_up(n, m):
    return (n + m - 1) // m * m


def _pack_rows(parts, cols, row_mult):
    lead = parts[0].shape[:-1]
    flat, offs, off = [], [], 0
    for p in parts:
        n = _round_up(p.shape[-1], cols)
        flat.append(jnp.pad(p, [(0, 0)] * len(lead) + [(0, n - p.shape[-1])]))
        offs.append(off)
        off += n
    total = _round_up(off, cols * row_mult)
    if total > off:
        flat.append(jnp.zeros(lead + (total - off,), parts[0].dtype))
    return jnp.concatenate(flat, axis=-1).reshape(lead + (total // cols, cols)), offs


def _unpack(packed, offs, shapes):
    lead = packed.shape[:-2]
    flat = packed.reshape(lead + (-1,))
    out = []
    for off, shp in zip(offs, shapes):
        n = 1
        for v in shp:
            n *= v
        out.append(flat[..., off:off + n].reshape(lead + tuple(shp)))
    return out


def kernel(x, mem, dn_norm, dn_w_in, dn_w_conv, dn_a_log, dn_dt_bias, dn_out_norm, dn_w_out, cv_norm, cv_w_pw1, cv_b_pw1, cv_w_dw, cv_b_dw, cv_ln_g, cv_ln_b, cv_w_pw2, cv_b_pw2, xa_norm, xa_mem_norm, xa_w_q, xa_w_kv, xa_w_o, mlp_norm, mlp_w_up, mlp_w_down, final_norm, loss_target, m_dn_norm, m_dn_w_in, m_dn_w_conv, m_dn_a_log, m_dn_dt_bias, m_dn_out_norm, m_dn_w_out, m_cv_norm, m_cv_w_pw1, m_cv_b_pw1, m_cv_w_dw, m_cv_b_dw, m_cv_ln_g, m_cv_ln_b, m_cv_w_pw2, m_cv_b_pw2, m_xa_norm, m_xa_mem_norm, m_xa_w_q, m_xa_w_kv, m_xa_w_o, m_mlp_norm, m_mlp_w_up, m_mlp_w_down, m_final_norm, v_dn_norm, v_dn_w_in, v_dn_w_conv, v_dn_a_log, v_dn_dt_bias, v_dn_out_norm, v_dn_w_out, v_cv_norm, v_cv_w_pw1, v_cv_b_pw1, v_cv_w_dw, v_cv_b_dw, v_cv_ln_g, v_cv_ln_b, v_cv_w_pw2, v_cv_b_pw2, v_xa_norm, v_xa_mem_norm, v_xa_w_q, v_xa_w_kv, v_xa_w_o, v_mlp_norm, v_mlp_w_up, v_mlp_w_down, v_final_norm):
    wsh = dict(dn_norm=dn_norm, dn_w_in=dn_w_in, dn_w_conv=dn_w_conv, dn_a_log=dn_a_log, dn_dt_bias=dn_dt_bias, dn_out_norm=dn_out_norm, dn_w_out=dn_w_out, cv_norm=cv_norm, cv_w_pw1=cv_w_pw1, cv_b_pw1=cv_b_pw1, cv_w_dw=cv_w_dw, cv_b_dw=cv_b_dw, cv_ln_g=cv_ln_g, cv_ln_b=cv_ln_b, cv_w_pw2=cv_w_pw2, cv_b_pw2=cv_b_pw2, xa_norm=xa_norm, xa_mem_norm=xa_mem_norm, xa_w_q=xa_w_q, xa_w_kv=xa_w_kv, xa_w_o=xa_w_o, mlp_norm=mlp_norm, mlp_w_up=mlp_w_up, mlp_w_down=mlp_w_down, final_norm=final_norm)
    msh = dict(dn_norm=m_dn_norm, dn_w_in=m_dn_w_in, dn_w_conv=m_dn_w_conv, dn_a_log=m_dn_a_log, dn_dt_bias=m_dn_dt_bias, dn_out_norm=m_dn_out_norm, dn_w_out=m_dn_w_out, cv_norm=m_cv_norm, cv_w_pw1=m_cv_w_pw1, cv_b_pw1=m_cv_b_pw1, cv_w_dw=m_cv_w_dw, cv_b_dw=m_cv_b_dw, cv_ln_g=m_cv_ln_g, cv_ln_b=m_cv_ln_b, cv_w_pw2=m_cv_w_pw2, cv_b_pw2=m_cv_b_pw2, xa_norm=m_xa_norm, xa_mem_norm=m_xa_mem_norm, xa_w_q=m_xa_w_q, xa_w_kv=m_xa_w_kv, xa_w_o=m_xa_w_o, mlp_norm=m_mlp_norm, mlp_w_up=m_mlp_w_up, mlp_w_down=m_mlp_w_down, final_norm=m_final_norm)
    vsh = dict(dn_norm=v_dn_norm, dn_w_in=v_dn_w_in, dn_w_conv=v_dn_w_conv, dn_a_log=v_dn_a_log, dn_dt_bias=v_dn_dt_bias, dn_out_norm=v_dn_out_norm, dn_w_out=v_dn_w_out, cv_norm=v_cv_norm, cv_w_pw1=v_cv_w_pw1, cv_b_pw1=v_cv_b_pw1, cv_w_dw=v_cv_w_dw, cv_b_dw=v_cv_b_dw, cv_ln_g=v_cv_ln_g, cv_ln_b=v_cv_ln_b, cv_w_pw2=v_cv_w_pw2, cv_b_pw2=v_cv_b_pw2, xa_norm=v_xa_norm, xa_mem_norm=v_xa_mem_norm, xa_w_q=v_xa_w_q, xa_w_kv=v_xa_w_kv, xa_w_o=v_xa_w_o, mlp_norm=v_mlp_norm, mlp_w_up=v_mlp_w_up, mlp_w_down=v_mlp_w_down, final_norm=v_final_norm)

    big_axis = dict(BIG)
    for src in (wsh, msh, vsh):
        src["dn_w_in"] = jnp.swapaxes(src["dn_w_in"], 1, 2)
    big_axis["dn_w_in"] = 1

    small_pack, small_offs = _pack_rows([wsh[nm].reshape(-1) for nm in SMALL_SH], LANES, 8)
    w = {nm: [None] * wsh[nm].shape[0] for nm in big_axis}

    def put_weights(group, gathered):
        for (nm, layer), gth in zip(group, gathered):
            if big_axis[nm] == 1:
                w[nm][layer] = gth.reshape(N_DEV * gth.shape[1], gth.shape[2])
            else:
                w[nm][layer] = gth

    first = _all_gather([wsh[nm][layer].astype(BF16) for nm, layer in GATHER_GROUPS[0]] + [small_pack],
                        "weights_all_gather_0")
    put_weights(GATHER_GROUPS[0], first)
    me = _dev_index(*_mesh_me())
    gather_handles, tokens = {}, []
    for gi in range(1, len(GATHER_GROUPS)):
        shards = [wsh[nm][layer].astype(BF16) for nm, layer in GATHER_GROUPS[gi]]
        lands = [lax.dynamic_update_slice(lax.empty((N_DEV,) + s.shape, s.dtype), s[None], (me, 0, 0))
                 for s in shards]
        gather_handles[gi], tok = _exchange_start(shards, lands, False, f"weights_gather_{gi}",
                                                  after=[first[-1]] + tokens)
        tokens.append(tok)
    for nm, gth in zip(SMALL_SH, _unpack(first[-1], small_offs, [wsh[nm].shape for nm in SMALL_SH])):
        w[nm] = jnp.moveaxis(gth, 0, -2).reshape(gth.shape[1:-1] + (N_DEV * gth.shape[-1],))
    for nm in REPL:
        w[nm] = wsh[nm]

    def fetch(gi, after):
        put_weights(GATHER_GROUPS[gi], _exchange_wait(gather_handles[gi], after)[1])

    scatter_handles = {}

    def emit(gi, g):
        blocks = []
        for nm, layer in SCATTER_GROUPS[gi]:
            gw = g[nm][layer]
            if big_axis[nm] == 1:
                gw = gw.reshape(N_DEV, gw.shape[0] // N_DEV, gw.shape[1])
            blocks.append(gw)
        if gi == 0:
            gsmall_pack, _ = _pack_rows(
                [jnp.moveaxis(g[nm].reshape(g[nm].shape[:-1] + (N_DEV, -1)), -2, 0).reshape(N_DEV, -1)
                 for nm in SMALL_SH], LANES, 8)
            blocks.append(gsmall_pack)
        lands = [lax.empty(b.shape, b.dtype) for b in blocks]
        scatter_handles[gi], tok = _exchange_start(blocks, lands, True, f"grads_scatter_{gi}")
        return [tok]

    loss_part, grad_x, g = _local_step(x[0], mem[0], loss_target[0], w, fetch, emit, tokens)

    recv = {nm: [None] * wsh[nm].shape[0] for nm in big_axis}
    sent = {nm: [None] * wsh[nm].shape[0] for nm in big_axis}
    gsh, delta, new_m, new_v = {}, {}, {}, {}
    after = [grad_x]
    done = set()
    me_arr = me.astype(jnp.int32).reshape(1)
    def small_adamw(names, name):
        packs = []
        for src in (wsh, gsh, msh, vsh):
            pk, offs = _pack_rows([src[nm].reshape(-1) for nm in names], LANES, 8)
            packs.append(pk)
        outs = _adamw(*packs, name)
        for dst, pk in zip((delta, new_m, new_v), outs):
            for nm, val in zip(names, _unpack(pk, offs, [wsh[nm].shape for nm in names])):
                dst[nm] = val
        return outs[0]

    for gi in reversed(range(len(SCATTER_GROUPS))):
        if gi == 0:
            repl_pack, repl_offs = _pack_rows([g[nm].reshape(-1) for nm in REPL] + [loss_part[:, :1].reshape(-1)],
                                              LANES, 8)
            (repl_all,) = _all_gather([repl_pack], "repl_grads_all_gather")
            repl_red = _slot_sum(repl_all, "repl_grads_sum", SLOT_SUM_ROWS)
            *repl_vals, loss_sum = _unpack(repl_red, repl_offs, [wsh[nm].shape for nm in REPL] + [(1,)])
            for nm, val in zip(REPL, repl_vals):
                gsh[nm] = val
            after = after + [small_adamw(list(REPL), "adamw_repl")]
        sources, landed = _exchange_wait(scatter_handles[gi], after)
        for (nm, layer), src, r in zip(SCATTER_GROUPS[gi], sources, landed):
            sent[nm][layer], recv[nm][layer] = src, r
        if gi == 0:
            slot = lax.broadcasted_iota(jnp.int32, landed[-1].shape, 0)
            rsmall = jnp.where(slot == me, sources[-1], landed[-1])
        for nm in big_axis:
            if nm not in done and all(r is not None for r in recv[nm]):
                gsh[nm], delta[nm], new_m[nm], new_v[nm] = _adamw_reduce(
                    me_arr, recv[nm], sent[nm], wsh[nm], msh[nm], vsh[nm], f"adamw_{nm}")
                done.add(nm)
                after = [delta[nm]]
    gsmall_red = _slot_sum(rsmall, "grads_small_sum", SLOT_SUM_ROWS)
    for nm, val in zip(SMALL_SH, _unpack(gsmall_red, small_offs, [wsh[nm].shape for nm in SMALL_SH])):
        gsh[nm] = val
    small_adamw(list(SMALL_SH), "adamw_small")
    for dst in (gsh, delta, new_m, new_v):
        dst["dn_w_in"] = jnp.swapaxes(dst["dn_w_in"], 1, 2)
    return (loss_sum.reshape(()), grad_x[None], *[gsh[nm] for nm in WEIGHTS], *[delta[nm] for nm in WEIGHTS],
            *[new_m[nm] for nm in WEIGHTS], *[new_v[nm] for nm in WEIGHTS])
```
